```python
import jax, jax.numpy as jnp
from jax import lax
import numpy as np

D_MODEL = 1024
BATCH = 8
SEQ = 4096
DEPTH = 1

D_MIX = D_MODEL
W_A = D_MIX // 2
HA_HEAD_DIM = 128
HA_HEADS = W_A // HA_HEAD_DIM
W_B = D_MIX - W_A
HB_HEAD_DIM = 64
HB_HEADS = W_B // HB_HEAD_DIM
HGRN_CHUNK = 64
DECAY_LORA = max(32, int(round(1.8 * W_B ** 0.5 / 32)) * 32)
AAA_LORA = max(32, int(round(1.8 * W_B ** 0.5 / 32)) * 32)
GATE_LORA = max(32, int(round(0.6 * W_B ** 0.8 / 32)) * 32)
N_HGRN_COLS = 4 * W_A
N_RWKV_COLS = 3 * W_B + DECAY_LORA + AAA_LORA + GATE_LORA
D_IN_PROJ = N_HGRN_COLS + N_RWKV_COLS
D_FF = ((8 * D_MODEL // 3 + 255) // 256) * 256
NORM_EPS = 1e-6
RWKV_GN_EPS = 64e-5
L2_EPS = 1e-12

kernel_name = 'hybrid_hgrn2_rwkv7_macaron'


def rms_norm(x, g):
    xf = x.astype(jnp.float32)
    y = xf * lax.rsqrt(jnp.mean(xf * xf, axis=-1, keepdims=True) + NORM_EPS)
    return (y * g).astype(x.dtype)


def swiglu(h, w_gate, w_up, w_down):
    return (jax.nn.silu(h @ w_gate) * (h @ w_up)) @ w_down


def token_shift(t):
    return jnp.pad(t, ((0, 0), (1, 0), (0, 0)))[:, :-1]


def hgrn2_chunkwise(q, k, v, log_f):
    B, T, H, K = q.shape
    V = v.shape[-1]
    C = HGRN_CHUNK
    n = T // C

    def blocks(t):
        return t.reshape(B, n, C, H, t.shape[-1]).transpose(0, 3, 1, 2, 4)

    q, k, v, log_f = blocks(q), blocks(k), blocks(v), blocks(log_f)
    b = jnp.cumsum(log_f, axis=3)
    b_ref = b[:, :, :, C // 2:C // 2 + 1]
    b_last = b[:, :, :, C - 1:]
    scores = jnp.einsum('bhntk,bhnsk->bhnts', q * jnp.exp(b - b_ref), k * jnp.exp(b_ref - b))
    causal = jnp.tril(jnp.ones((C, C), dtype=bool))
    scores = jnp.where(causal, scores, 0.0)
    o = jnp.einsum('bhnts,bhnsv->bhntv', scores, v)
    u = jnp.einsum('bhnsk,bhnsv->bhnkv', k * jnp.exp(b_last - b), v)
    d = jnp.exp(b_last[:, :, :, 0])

    def chunk_step(s, inp):
        u_n, d_n = inp
        return d_n[..., None] * s + u_n, s

    _, s_prev = lax.scan(chunk_step, jnp.zeros((B, H, K, V), jnp.float32),
                         (jnp.moveaxis(u, 2, 0), jnp.moveaxis(d, 2, 0)))
    s_prev = jnp.moveaxis(s_prev, 0, 2)
    o = o + jnp.einsum('bhntk,bhnkv->bhntv', q * jnp.exp(b), s_prev)
    return o.transpose(0, 2, 3, 1, 4).reshape(B, T, H, V)


def rwkv7_scan(r, w, k, v, a, b):
    B, T, H, N = r.shape

    def step(s, inp):
        r_t, w_t, k_t, v_t, a_t, b_t = inp
        sa = jnp.einsum('bhvk,bhk->bhv', s, a_t)
        s = s * w_t[:, :, None, :] + sa[..., None] * b_t[:, :, None, :] + v_t[..., None] * k_t[:, :, None, :]
        return s, jnp.einsum('bhvk,bhk->bhv', s, r_t)

    tm = lambda t: jnp.moveaxis(t, 1, 0)
    _, y = lax.scan(step, jnp.zeros((B, H, N, N), jnp.float32),
                    (tm(r), tm(w), tm(k), tm(v), tm(a), tm(b)))
    return jnp.moveaxis(y, 0, 1)


def hybrid_mixer(h, w_in, lb, hgrn_out_norm, mu, w0, w2, a0, a2, g2, k_k, k_a, r_k, gn_w, gn_b, w_out):
    B, T, _ = h.shape
    f32 = jnp.float32
    p = h @ w_in

    q_a, f_a, i_a, g_a = jnp.split(p[..., :N_HGRN_COLS].astype(f32), 4, axis=-1)
    forget = lb + (1.0 - lb) * jax.nn.sigmoid(f_a)
    heads_a = lambda t: t.reshape(B, T, HA_HEADS, HA_HEAD_DIM)
    o_a = hgrn2_chunkwise(heads_a(jax.nn.silu(q_a)), heads_a(1.0 - forget),
                          heads_a(i_a), heads_a(jnp.log(forget)))
    o_a = o_a * lax.rsqrt(jnp.mean(o_a * o_a, axis=-1, keepdims=True) + NORM_EPS)
    o_a = o_a.reshape(B, T, W_A) * hgrn_out_norm * jax.nn.silu(g_a)

    pr = p[..., N_HGRN_COLS:].astype(f32)
    pr = pr + mu * (token_shift(pr) - pr)
    splits = [W_B, 2 * W_B, 3 * W_B, 3 * W_B + DECAY_LORA, 3 * W_B + DECAY_LORA + AAA_LORA]
    r, k, v, w_low, a_low, g_low = jnp.split(pr, splits, axis=-1)
    w_log = -jax.nn.softplus(-(w0 + jnp.tanh(w_low) @ w2)) - 0.5
    decay = jnp.exp(-jnp.exp(w_log))
    a = jax.nn.sigmoid(a0 + a_low @ a2)
    g = jax.nn.sigmoid(g_low) @ g2
    heads_b = lambda t: t.reshape(B, T, HB_HEADS, HB_HEAD_DIM)
    kk = heads_b(k * k_k)
    kk = kk / jnp.maximum(jnp.sqrt(jnp.sum(kk * kk, axis=-1, keepdims=True)), L2_EPS)
    k = k * (1.0 + (a - 1.0) * k_a)
    r_h, k_h, v_h, a_h = heads_b(r), heads_b(k), heads_b(v), heads_b(a)
    y = rwkv7_scan(r_h, heads_b(decay), k_h, v_h, -kk, kk * a_h)
    mean = jnp.mean(y, axis=-1, keepdims=True)
    var = jnp.mean(jnp.square(y - mean), axis=-1, keepdims=True)
    y = ((y - mean) * lax.rsqrt(var + RWKV_GN_EPS)).reshape(B, T, W_B) * gn_w + gn_b
    bonus = (jnp.sum(r_h * k_h * r_k, axis=-1, keepdims=True) * v_h).reshape(B, T, W_B)
    o_b = (y + bonus) * g

    return jnp.concatenate([o_a, o_b], axis=-1).astype(h.dtype) @ w_out


def _fwd_setup_inputs(seed: int = 0) -> dict:
    key = jax.random.key(seed)
    ks = iter(jax.random.split(key, 40))
    f32 = jnp.float32
    nrm = lambda shape, scale: jax.random.normal(next(ks), shape, f32) * scale
    uni = lambda shape, lo, hi: jax.random.uniform(next(ks), shape, f32, lo, hi)
    L = DEPTH
    return {
        'x': nrm((BATCH, SEQ, D_MODEL), 1.0),
        'ffn1_norm': 1.0 + nrm((L, D_MODEL), 0.02),
        'ffn1_w_gate': nrm((L, D_MODEL, D_FF), D_MODEL ** -0.5),
        'ffn1_w_up': nrm((L, D_MODEL, D_FF), D_MODEL ** -0.5),
        'ffn1_w_down': nrm((L, D_FF, D_MODEL), D_FF ** -0.5),
        'mix_norm': 1.0 + nrm((L, D_MODEL), 0.02),
        'w_in': nrm((L, D_MODEL, D_IN_PROJ), D_MODEL ** -0.5),
        'hgrn_lb_logits': nrm((L + 1, W_A), 0.1),
        'hgrn_out_norm': 1.0 + nrm((L, W_A), 0.02),
        'rwkv_shift_mu': uni((L, N_RWKV_COLS), 0.0, 1.0),
        'rwkv_w0': uni((L, W_B), -5.0, 1.0),
        'rwkv_w2': nrm((L, DECAY_LORA, W_B), 0.5 * DECAY_LORA ** -0.5),
        'rwkv_a0': nrm((L, W_B), 0.1),
        'rwkv_a2': nrm((L, AAA_LORA, W_B), 0.5 * AAA_LORA ** -0.5),
        'rwkv_g2': nrm((L, GATE_LORA, W_B), GATE_LORA ** -0.5),
        'rwkv_k_k': 0.85 + nrm((L, W_B), 0.05),
        'rwkv_k_a': 1.0 + nrm((L, W_B), 0.05),
        'rwkv_r_k': nrm((L, HB_HEADS, HB_HEAD_DIM), 0.1),
        'rwkv_gn_w': 1.0 + nrm((L, W_B), 0.02),
        'rwkv_gn_b': nrm((L, W_B), 0.02),
        'w_out': nrm((L, D_MIX, D_MODEL), D_MIX ** -0.5),
        'ffn2_norm': 1.0 + nrm((L, D_MODEL), 0.02),
        'ffn2_w_gate': nrm((L, D_MODEL, D_FF), D_MODEL ** -0.5),
        'ffn2_w_up': nrm((L, D_MODEL, D_FF), D_MODEL ** -0.5),
        'ffn2_w_down': nrm((L, D_FF, D_MODEL), D_FF ** -0.5),
        'final_norm': 1.0 + nrm((D_MODEL,), 0.02),
    }


def _fwd_reference(x, ffn1_norm, ffn1_w_gate, ffn1_w_up, ffn1_w_down, mix_norm, w_in, hgrn_lb_logits,
              hgrn_out_norm, rwkv_shift_mu, rwkv_w0, rwkv_w2, rwkv_a0, rwkv_a2, rwkv_g2, rwkv_k_k,
              rwkv_k_a, rwkv_r_k, rwkv_gn_w, rwkv_gn_b, w_out, ffn2_norm, ffn2_w_gate, ffn2_w_up,
              ffn2_w_down, final_norm):
    lower_bounds = jnp.cumsum(jax.nn.softmax(hgrn_lb_logits.astype(jnp.float32), axis=0), axis=0)
    for l in range(DEPTH):
        x = x + 0.5 * swiglu(rms_norm(x, ffn1_norm[l]), ffn1_w_gate[l], ffn1_w_up[l], ffn1_w_down[l])
        x = x + hybrid_mixer(rms_norm(x, mix_norm[l]), w_in[l], lower_bounds[l], hgrn_out_norm[l],
                             rwkv_shift_mu[l], rwkv_w0[l], rwkv_w2[l], rwkv_a0[l], rwkv_a2[l],
                             rwkv_g2[l], rwkv_k_k[l], rwkv_k_a[l], rwkv_r_k[l], rwkv_gn_w[l],
                             rwkv_gn_b[l], w_out[l])
        x = x + 0.5 * swiglu(rms_norm(x, ffn2_norm[l]), ffn2_w_gate[l], ffn2_w_up[l], ffn2_w_down[l])
    return rms_norm(x, final_norm)


import jax as _jax
import jax.numpy as _jnp

TWIN_FORMAT = 'train_step'
FWD_PARAMS = ['x', 'ffn1_norm', 'ffn1_w_gate', 'ffn1_w_up', 'ffn1_w_down', 'mix_norm', 'w_in', 'hgrn_lb_logits', 'hgrn_out_norm', 'rwkv_shift_mu', 'rwkv_w0', 'rwkv_w2', 'rwkv_a0', 'rwkv_a2', 'rwkv_g2', 'rwkv_k_k', 'rwkv_k_a', 'rwkv_r_k', 'rwkv_gn_w', 'rwkv_gn_b', 'w_out', 'ffn2_norm', 'ffn2_w_gate', 'ffn2_w_up', 'ffn2_w_down', 'final_norm']
TWIN_WEIGHTS = ['ffn1_norm', 'ffn1_w_gate', 'ffn1_w_up', 'ffn1_w_down', 'mix_norm', 'w_in', 'hgrn_lb_logits', 'hgrn_out_norm', 'rwkv_shift_mu', 'rwkv_w0', 'rwkv_w2', 'rwkv_a0', 'rwkv_a2', 'rwkv_g2', 'rwkv_k_k', 'rwkv_k_a', 'rwkv_r_k', 'rwkv_gn_w', 'rwkv_gn_b', 'w_out', 'ffn2_norm', 'ffn2_w_gate', 'ffn2_w_up', 'ffn2_w_down', 'final_norm']
TWIN_DIFF_INPUT = 'x'
TWIN_INPUTS = ['x', 'ffn1_norm', 'ffn1_w_gate', 'ffn1_w_up', 'ffn1_w_down', 'mix_norm', 'w_in', 'hgrn_lb_logits', 'hgrn_out_norm', 'rwkv_shift_mu', 'rwkv_w0', 'rwkv_w2', 'rwkv_a0', 'rwkv_a2', 'rwkv_g2', 'rwkv_k_k', 'rwkv_k_a', 'rwkv_r_k', 'rwkv_gn_w', 'rwkv_gn_b', 'w_out', 'ffn2_norm', 'ffn2_w_gate', 'ffn2_w_up', 'ffn2_w_down', 'final_norm', 'loss_target', 'm_ffn1_norm', 'm_ffn1_w_gate', 'm_ffn1_w_up', 'm_ffn1_w_down', 'm_mix_norm', 'm_w_in', 'm_hgrn_lb_logits', 'm_hgrn_out_norm', 'm_rwkv_shift_mu', 'm_rwkv_w0', 'm_rwkv_w2', 'm_rwkv_a0', 'm_rwkv_a2', 'm_rwkv_g2', 'm_rwkv_k_k', 'm_rwkv_k_a', 'm_rwkv_r_k', 'm_rwkv_gn_w', 'm_rwkv_gn_b', 'm_w_out', 'm_ffn2_norm', 'm_ffn2_w_gate', 'm_ffn2_w_up', 'm_ffn2_w_down', 'm_final_norm', 'v_ffn1_norm', 'v_ffn1_w_gate', 'v_ffn1_w_up', 'v_ffn1_w_down', 'v_mix_norm', 'v_w_in', 'v_hgrn_lb_logits', 'v_hgrn_out_norm', 'v_rwkv_shift_mu', 'v_rwkv_w0', 'v_rwkv_w2', 'v_rwkv_a0', 'v_rwkv_a2', 'v_rwkv_g2', 'v_rwkv_k_k', 'v_rwkv_k_a', 'v_rwkv_r_k', 'v_rwkv_gn_w', 'v_rwkv_gn_b', 'v_w_out', 'v_ffn2_norm', 'v_ffn2_w_gate', 'v_ffn2_w_up', 'v_ffn2_w_down', 'v_final_norm']
TWIN_OUTPUTS = ['loss', 'grad_x', 'grad_ffn1_norm', 'grad_ffn1_w_gate', 'grad_ffn1_w_up', 'grad_ffn1_w_down', 'grad_mix_norm', 'grad_w_in', 'grad_hgrn_lb_logits', 'grad_hgrn_out_norm', 'grad_rwkv_shift_mu', 'grad_rwkv_w0', 'grad_rwkv_w2', 'grad_rwkv_a0', 'grad_rwkv_a2', 'grad_rwkv_g2', 'grad_rwkv_k_k', 'grad_rwkv_k_a', 'grad_rwkv_r_k', 'grad_rwkv_gn_w', 'grad_rwkv_gn_b', 'grad_w_out', 'grad_ffn2_norm', 'grad_ffn2_w_gate', 'grad_ffn2_w_up', 'grad_ffn2_w_down', 'grad_final_norm', 'delta_ffn1_norm', 'delta_ffn1_w_gate', 'delta_ffn1_w_up', 'delta_ffn1_w_down', 'delta_mix_norm', 'delta_w_in', 'delta_hgrn_lb_logits', 'delta_hgrn_out_norm', 'delta_rwkv_shift_mu', 'delta_rwkv_w0', 'delta_rwkv_w2', 'delta_rwkv_a0', 'delta_rwkv_a2', 'delta_rwkv_g2', 'delta_rwkv_k_k', 'delta_rwkv_k_a', 'delta_rwkv_r_k', 'delta_rwkv_gn_w', 'delta_rwkv_gn_b', 'delta_w_out', 'delta_ffn2_norm', 'delta_ffn2_w_gate', 'delta_ffn2_w_up', 'delta_ffn2_w_down', 'delta_final_norm', 'new_m_ffn1_norm', 'new_m_ffn1_w_gate', 'new_m_ffn1_w_up', 'new_m_ffn1_w_down', 'new_m_mix_norm', 'new_m_w_in', 'new_m_hgrn_lb_logits', 'new_m_hgrn_out_norm', 'new_m_rwkv_shift_mu', 'new_m_rwkv_w0', 'new_m_rwkv_w2', 'new_m_rwkv_a0', 'new_m_rwkv_a2', 'new_m_rwkv_g2', 'new_m_rwkv_k_k', 'new_m_rwkv_k_a', 'new_m_rwkv_r_k', 'new_m_rwkv_gn_w', 'new_m_rwkv_gn_b', 'new_m_w_out', 'new_m_ffn2_norm', 'new_m_ffn2_w_gate', 'new_m_ffn2_w_up', 'new_m_ffn2_w_down', 'new_m_final_norm', 'new_v_ffn1_norm', 'new_v_ffn1_w_gate', 'new_v_ffn1_w_up', 'new_v_ffn1_w_down', 'new_v_mix_norm', 'new_v_w_in', 'new_v_hgrn_lb_logits', 'new_v_hgrn_out_norm', 'new_v_rwkv_shift_mu', 'new_v_rwkv_w0', 'new_v_rwkv_w2', 'new_v_rwkv_a0', 'new_v_rwkv_a2', 'new_v_rwkv_g2', 'new_v_rwkv_k_k', 'new_v_rwkv_k_a', 'new_v_rwkv_r_k', 'new_v_rwkv_gn_w', 'new_v_rwkv_gn_b', 'new_v_w_out', 'new_v_ffn2_norm', 'new_v_ffn2_w_gate', 'new_v_ffn2_w_up', 'new_v_ffn2_w_down', 'new_v_final_norm']
TWIN_LEAF_KINDS = {'loss': 'loss', 'grad_x': 'grad_x', 'grad_ffn1_norm': 'grad_w', 'grad_ffn1_w_gate': 'grad_w', 'grad_ffn1_w_up': 'grad_w', 'grad_ffn1_w_down': 'grad_w', 'grad_mix_norm': 'grad_w', 'grad_w_in': 'grad_w', 'grad_hgrn_lb_logits': 'grad_w', 'grad_hgrn_out_norm': 'grad_w', 'grad_rwkv_shift_mu': 'grad_w', 'grad_rwkv_w0': 'grad_w', 'grad_rwkv_w2': 'grad_w', 'grad_rwkv_a0': 'grad_w', 'grad_rwkv_a2': 'grad_w', 'grad_rwkv_g2': 'grad_w', 'grad_rwkv_k_k': 'grad_w', 'grad_rwkv_k_a': 'grad_w', 'grad_rwkv_r_k': 'grad_w', 'grad_rwkv_gn_w': 'grad_w', 'grad_rwkv_gn_b': 'grad_w', 'grad_w_out': 'grad_w', 'grad_ffn2_norm': 'grad_w', 'grad_ffn2_w_gate': 'grad_w', 'grad_ffn2_w_up': 'grad_w', 'grad_ffn2_w_down': 'grad_w', 'grad_final_norm': 'grad_w', 'delta_ffn1_norm': 'delta_w', 'delta_ffn1_w_gate': 'delta_w', 'delta_ffn1_w_up': 'delta_w', 'delta_ffn1_w_down': 'delta_w', 'delta_mix_norm': 'delta_w', 'delta_w_in': 'delta_w', 'delta_hgrn_lb_logits': 'delta_w', 'delta_hgrn_out_norm': 'delta_w', 'delta_rwkv_shift_mu': 'delta_w', 'delta_rwkv_w0': 'delta_w', 'delta_rwkv_w2': 'delta_w', 'delta_rwkv_a0': 'delta_w', 'delta_rwkv_a2': 'delta_w', 'delta_rwkv_g2': 'delta_w', 'delta_rwkv_k_k': 'delta_w', 'delta_rwkv_k_a': 'delta_w', 'delta_rwkv_r_k': 'delta_w', 'delta_rwkv_gn_w': 'delta_w', 'delta_rwkv_gn_b': 'delta_w', 'delta_w_out': 'delta_w', 'delta_ffn2_norm': 'delta_w', 'delta_ffn2_w_gate': 'delta_w', 'delta_ffn2_w_up': 'delta_w', 'delta_ffn2_w_down': 'delta_w', 'delta_final_norm': 'delta_w', 'new_m_ffn1_norm': 'new_m', 'new_m_ffn1_w_gate': 'new_m', 'new_m_ffn1_w_up': 'new_m', 'new_m_ffn1_w_down': 'new_m', 'new_m_mix_norm': 'new_m', 'new_m_w_in': 'new_m', 'new_m_hgrn_lb_logits': 'new_m', 'new_m_hgrn_out_norm': 'new_m', 'new_m_rwkv_shift_mu': 'new_m', 'new_m_rwkv_w0': 'new_m', 'new_m_rwkv_w2': 'new_m', 'new_m_rwkv_a0': 'new_m', 'new_m_rwkv_a2': 'new_m', 'new_m_rwkv_g2': 'new_m', 'new_m_rwkv_k_k': 'new_m', 'new_m_rwkv_k_a': 'new_m', 'new_m_rwkv_r_k': 'new_m', 'new_m_rwkv_gn_w': 'new_m', 'new_m_rwkv_gn_b': 'new_m', 'new_m_w_out': 'new_m', 'new_m_ffn2_norm': 'new_m', 'new_m_ffn2_w_gate': 'new_m', 'new_m_ffn2_w_up': 'new_m', 'new_m_ffn2_w_down': 'new_m', 'new_m_final_norm': 'new_m', 'new_v_ffn1_norm': 'new_v', 'new_v_ffn1_w_gate': 'new_v', 'new_v_ffn1_w_up': 'new_v', 'new_v_ffn1_w_down': 'new_v', 'new_v_mix_norm': 'new_v', 'new_v_w_in': 'new_v', 'new_v_hgrn_lb_logits': 'new_v', 'new_v_hgrn_out_norm': 'new_v', 'new_v_rwkv_shift_mu': 'new_v', 'new_v_rwkv_w0': 'new_v', 'new_v_rwkv_w2': 'new_v', 'new_v_rwkv_a0': 'new_v', 'new_v_rwkv_a2': 'new_v', 'new_v_rwkv_g2': 'new_v', 'new_v_rwkv_k_k': 'new_v', 'new_v_rwkv_k_a': 'new_v', 'new_v_rwkv_r_k': 'new_v', 'new_v_rwkv_gn_w': 'new_v', 'new_v_rwkv_gn_b': 'new_v', 'new_v_w_out': 'new_v', 'new_v_ffn2_norm': 'new_v', 'new_v_ffn2_w_gate': 'new_v', 'new_v_ffn2_w_up': 'new_v', 'new_v_ffn2_w_down': 'new_v', 'new_v_final_norm': 'new_v'}


def _forward(args):
    return _fwd_reference(*[args[k] for k in FWD_PARAMS])


def _output_shape():
    out = _jax.eval_shape(lambda: _forward(_fwd_setup_inputs(0)))
    return out.shape, out.dtype

N_MICROBATCH = 1
ADAM_LR = 0.001
ADAM_B1 = 0.9
ADAM_B2 = 0.999
ADAM_EPS = 1e-08
ADAM_WD = 0.01
ADAM_STEP = 10
PER_EXAMPLE_BATCH_AXIS = {'x': 0, 'loss_target': 0}
SHARED_INPUTS = []
_WEIGHT_DTYPES = {'ffn1_norm': _jnp.float32, 'ffn1_w_gate': _jnp.float32, 'ffn1_w_up': _jnp.float32, 'ffn1_w_down': _jnp.float32, 'mix_norm': _jnp.float32, 'w_in': _jnp.float32, 'hgrn_lb_logits': _jnp.float32, 'hgrn_out_norm': _jnp.float32, 'rwkv_shift_mu': _jnp.float32, 'rwkv_w0': _jnp.float32, 'rwkv_w2': _jnp.float32, 'rwkv_a0': _jnp.float32, 'rwkv_a2': _jnp.float32, 'rwkv_g2': _jnp.float32, 'rwkv_k_k': _jnp.float32, 'rwkv_k_a': _jnp.float32, 'rwkv_r_k': _jnp.float32, 'rwkv_gn_w': _jnp.float32, 'rwkv_gn_b': _jnp.float32, 'w_out': _jnp.float32, 'ffn2_norm': _jnp.float32, 'ffn2_w_gate': _jnp.float32, 'ffn2_w_up': _jnp.float32, 'ffn2_w_down': _jnp.float32, 'final_norm': _jnp.float32}
MOMENT_SCALE = {'ffn1_norm': 9.283981e-02, 'ffn1_w_gate': 3.829771e-02, 'ffn1_w_up': 3.703476e-02, 'ffn1_w_down': 6.141737e-02, 'mix_norm': 1.509299e-01, 'w_in': 7.503702e-02, 'hgrn_lb_logits': 8.738469e-03, 'hgrn_out_norm': 9.692267e-02, 'rwkv_shift_mu': 1.377327e-01, 'rwkv_w0': 4.472346e-02, 'rwkv_w2': 5.190318e-03, 'rwkv_a0': 3.532720e-02, 'rwkv_a2': 3.122919e-02, 'rwkv_g2': 8.679019e-02, 'rwkv_k_k': 8.076054e-02, 'rwkv_k_a': 8.526667e-02, 'rwkv_r_k': 2.104064e-01, 'rwkv_gn_w': 8.339854e-02, 'rwkv_gn_b': 9.069404e-02, 'w_out': 8.745459e-02, 'ffn2_norm': 6.424383e-02, 'ffn2_w_gate': 2.689178e-02, 'ffn2_w_up': 2.606208e-02, 'ffn2_w_down': 4.325823e-02, 'final_norm': 3.195786e+01}


def _to_microbatches(a, axis):
    t = _jnp.moveaxis(a, axis, 0)
    t = t.reshape((N_MICROBATCH, t.shape[0] // N_MICROBATCH) + t.shape[1:])
    return _jnp.moveaxis(t, 1, axis + 1)


def setup_inputs(seed: int = 0) -> dict:
    inp = _fwd_setup_inputs(seed)
    key = _jax.random.fold_in(_jax.random.key(seed), 7919)
    shape, _ = _output_shape()
    out = dict(inp)
    out["loss_target"] = _jax.random.normal(_jax.random.fold_in(key, 0), shape, _jnp.float32)
    for i, name in enumerate(TWIN_WEIGHTS):
        w = inp[name].astype(_jnp.float32)
        if MOMENT_SCALE is None:
            s = _jnp.sqrt(_jnp.mean(_jnp.square(w)) + 1e-30)
        else:
            s = MOMENT_SCALE[name]
        km, kv = _jax.random.split(_jax.random.fold_in(key, i + 1))
        out[name] = w
        out["m_" + name] = s * _jax.random.normal(km, w.shape, _jnp.float32)
        out["v_" + name] = (s * s) * _jax.random.uniform(kv, w.shape, _jnp.float32, 0.5, 1.5)
    if N_MICROBATCH > 1:
        for name, axis in PER_EXAMPLE_BATCH_AXIS.items():
            out[name] = _to_microbatches(out[name], axis)
    return {'x': out['x'], 'ffn1_norm': out['ffn1_norm'], 'ffn1_w_gate': out['ffn1_w_gate'], 'ffn1_w_up': out['ffn1_w_up'], 'ffn1_w_down': out['ffn1_w_down'], 'mix_norm': out['mix_norm'], 'w_in': out['w_in'], 'hgrn_lb_logits': out['hgrn_lb_logits'], 'hgrn_out_norm': out['hgrn_out_norm'], 'rwkv_shift_mu': out['rwkv_shift_mu'], 'rwkv_w0': out['rwkv_w0'], 'rwkv_w2': out['rwkv_w2'], 'rwkv_a0': out['rwkv_a0'], 'rwkv_a2': out['rwkv_a2'], 'rwkv_g2': out['rwkv_g2'], 'rwkv_k_k': out['rwkv_k_k'], 'rwkv_k_a': out['rwkv_k_a'], 'rwkv_r_k': out['rwkv_r_k'], 'rwkv_gn_w': out['rwkv_gn_w'], 'rwkv_gn_b': out['rwkv_gn_b'], 'w_out': out['w_out'], 'ffn2_norm': out['ffn2_norm'], 'ffn2_w_gate': out['ffn2_w_gate'], 'ffn2_w_up': out['ffn2_w_up'], 'ffn2_w_down': out['ffn2_w_down'], 'final_norm': out['final_norm'], 'loss_target': out['loss_target'], 'm_ffn1_norm': out['m_ffn1_norm'], 'm_ffn1_w_gate': out['m_ffn1_w_gate'], 'm_ffn1_w_up': out['m_ffn1_w_up'], 'm_ffn1_w_down': out['m_ffn1_w_down'], 'm_mix_norm': out['m_mix_norm'], 'm_w_in': out['m_w_in'], 'm_hgrn_lb_logits': out['m_hgrn_lb_logits'], 'm_hgrn_out_norm': out['m_hgrn_out_norm'], 'm_rwkv_shift_mu': out['m_rwkv_shift_mu'], 'm_rwkv_w0': out['m_rwkv_w0'], 'm_rwkv_w2': out['m_rwkv_w2'], 'm_rwkv_a0': out['m_rwkv_a0'], 'm_rwkv_a2': out['m_rwkv_a2'], 'm_rwkv_g2': out['m_rwkv_g2'], 'm_rwkv_k_k': out['m_rwkv_k_k'], 'm_rwkv_k_a': out['m_rwkv_k_a'], 'm_rwkv_r_k': out['m_rwkv_r_k'], 'm_rwkv_gn_w': out['m_rwkv_gn_w'], 'm_rwkv_gn_b': out['m_rwkv_gn_b'], 'm_w_out': out['m_w_out'], 'm_ffn2_norm': out['m_ffn2_norm'], 'm_ffn2_w_gate': out['m_ffn2_w_gate'], 'm_ffn2_w_up': out['m_ffn2_w_up'], 'm_ffn2_w_down': out['m_ffn2_w_down'], 'm_final_norm': out['m_final_norm'], 'v_ffn1_norm': out['v_ffn1_norm'], 'v_ffn1_w_gate': out['v_ffn1_w_gate'], 'v_ffn1_w_up': out['v_ffn1_w_up'], 'v_ffn1_w_down': out['v_ffn1_w_down'], 'v_mix_norm': out['v_mix_norm'], 'v_w_in': out['v_w_in'], 'v_hgrn_lb_logits': out['v_hgrn_lb_logits'], 'v_hgrn_out_norm': out['v_hgrn_out_norm'], 'v_rwkv_shift_mu': out['v_rwkv_shift_mu'], 'v_rwkv_w0': out['v_rwkv_w0'], 'v_rwkv_w2': out['v_rwkv_w2'], 'v_rwkv_a0': out['v_rwkv_a0'], 'v_rwkv_a2': out['v_rwkv_a2'], 'v_rwkv_g2': out['v_rwkv_g2'], 'v_rwkv_k_k': out['v_rwkv_k_k'], 'v_rwkv_k_a': out['v_rwkv_k_a'], 'v_rwkv_r_k': out['v_rwkv_r_k'], 'v_rwkv_gn_w': out['v_rwkv_gn_w'], 'v_rwkv_gn_b': out['v_rwkv_gn_b'], 'v_w_out': out['v_w_out'], 'v_ffn2_norm': out['v_ffn2_norm'], 'v_ffn2_w_gate': out['v_ffn2_w_gate'], 'v_ffn2_w_up': out['v_ffn2_w_up'], 'v_ffn2_w_down': out['v_ffn2_w_down'], 'v_final_norm': out['v_final_norm']}


def _loss(weights, diff, rest, loss_target):
    with _jax.named_scope("forward"):
        args = {**rest, TWIN_DIFF_INPUT: diff, **{k: w.astype(_WEIGHT_DTYPES[k]) for k, w in weights.items()}}
        y = _forward(args)
    with _jax.named_scope("loss_head"):
        err = _jnp.square(y.astype(_jnp.float32) - loss_target)
        return 0.5 * _jnp.sum(_jnp.mean(err, axis=-1)) if err.ndim else 0.5 * err


def _adamw(w, g, m, v):
    m = ADAM_B1 * m + (1.0 - ADAM_B1) * g
    v = ADAM_B2 * v + (1.0 - ADAM_B2) * _jnp.square(g)
    m_hat = m / (1.0 - ADAM_B1 ** ADAM_STEP)
    v_hat = v / (1.0 - ADAM_B2 ** ADAM_STEP)
    delta = -ADAM_LR * (m_hat / (_jnp.sqrt(v_hat) + ADAM_EPS) + ADAM_WD * w)
    return delta, m, v


def reference(x, ffn1_norm, ffn1_w_gate, ffn1_w_up, ffn1_w_down, mix_norm, w_in, hgrn_lb_logits, hgrn_out_norm, rwkv_shift_mu, rwkv_w0, rwkv_w2, rwkv_a0, rwkv_a2, rwkv_g2, rwkv_k_k, rwkv_k_a, rwkv_r_k, rwkv_gn_w, rwkv_gn_b, w_out, ffn2_norm, ffn2_w_gate, ffn2_w_up, ffn2_w_down, final_norm, loss_target, m_ffn1_norm, m_ffn1_w_gate, m_ffn1_w_up, m_ffn1_w_down, m_mix_norm, m_w_in, m_hgrn_lb_logits, m_hgrn_out_norm, m_rwkv_shift_mu, m_rwkv_w0, m_rwkv_w2, m_rwkv_a0, m_rwkv_a2, m_rwkv_g2, m_rwkv_k_k, m_rwkv_k_a, m_rwkv_r_k, m_rwkv_gn_w, m_rwkv_gn_b, m_w_out, m_ffn2_norm, m_ffn2_w_gate, m_ffn2_w_up, m_ffn2_w_down, m_final_norm, v_ffn1_norm, v_ffn1_w_gate, v_ffn1_w_up, v_ffn1_w_down, v_mix_norm, v_w_in, v_hgrn_lb_logits, v_hgrn_out_norm, v_rwkv_shift_mu, v_rwkv_w0, v_rwkv_w2, v_rwkv_a0, v_rwkv_a2, v_rwkv_g2, v_rwkv_k_k, v_rwkv_k_a, v_rwkv_r_k, v_rwkv_gn_w, v_rwkv_gn_b, v_w_out, v_ffn2_norm, v_ffn2_w_gate, v_ffn2_w_up, v_ffn2_w_down, v_final_norm):
    given = dict(x=x, ffn1_norm=ffn1_norm, ffn1_w_gate=ffn1_w_gate, ffn1_w_up=ffn1_w_up, ffn1_w_down=ffn1_w_down, mix_norm=mix_norm, w_in=w_in, hgrn_lb_logits=hgrn_lb_logits, hgrn_out_norm=hgrn_out_norm, rwkv_shift_mu=rwkv_shift_mu, rwkv_w0=rwkv_w0, rwkv_w2=rwkv_w2, rwkv_a0=rwkv_a0, rwkv_a2=rwkv_a2, rwkv_g2=rwkv_g2, rwkv_k_k=rwkv_k_k, rwkv_k_a=rwkv_k_a, rwkv_r_k=rwkv_r_k, rwkv_gn_w=rwkv_gn_w, rwkv_gn_b=rwkv_gn_b, w_out=w_out, ffn2_norm=ffn2_norm, ffn2_w_gate=ffn2_w_gate, ffn2_w_up=ffn2_w_up, ffn2_w_down=ffn2_w_down, final_norm=final_norm, loss_target=loss_target, m_ffn1_norm=m_ffn1_norm, m_ffn1_w_gate=m_ffn1_w_gate, m_ffn1_w_up=m_ffn1_w_up, m_ffn1_w_down=m_ffn1_w_down, m_mix_norm=m_mix_norm, m_w_in=m_w_in, m_hgrn_lb_logits=m_hgrn_lb_logits, m_hgrn_out_norm=m_hgrn_out_norm, m_rwkv_shift_mu=m_rwkv_shift_mu, m_rwkv_w0=m_rwkv_w0, m_rwkv_w2=m_rwkv_w2, m_rwkv_a0=m_rwkv_a0, m_rwkv_a2=m_rwkv_a2, m_rwkv_g2=m_rwkv_g2, m_rwkv_k_k=m_rwkv_k_k, m_rwkv_k_a=m_rwkv_k_a, m_rwkv_r_k=m_rwkv_r_k, m_rwkv_gn_w=m_rwkv_gn_w, m_rwkv_gn_b=m_rwkv_gn_b, m_w_out=m_w_out, m_ffn2_norm=m_ffn2_norm, m_ffn2_w_gate=m_ffn2_w_gate, m_ffn2_w_up=m_ffn2_w_up, m_ffn2_w_down=m_ffn2_w_down, m_final_norm=m_final_norm, v_ffn1_norm=v_ffn1_norm, v_ffn1_w_gate=v_ffn1_w_gate, v_ffn1_w_up=v_ffn1_w_up, v_ffn1_w_down=v_ffn1_w_down, v_mix_norm=v_mix_norm, v_w_in=v_w_in, v_hgrn_lb_logits=v_hgrn_lb_logits, v_hgrn_out_norm=v_hgrn_out_norm, v_rwkv_shift_mu=v_rwkv_shift_mu, v_rwkv_w0=v_rwkv_w0, v_rwkv_w2=v_rwkv_w2, v_rwkv_a0=v_rwkv_a0, v_rwkv_a2=v_rwkv_a2, v_rwkv_g2=v_rwkv_g2, v_rwkv_k_k=v_rwkv_k_k, v_rwkv_k_a=v_rwkv_k_a, v_rwkv_r_k=v_rwkv_r_k, v_rwkv_gn_w=v_rwkv_gn_w, v_rwkv_gn_b=v_rwkv_gn_b, v_w_out=v_w_out, v_ffn2_norm=v_ffn2_norm, v_ffn2_w_gate=v_ffn2_w_gate, v_ffn2_w_up=v_ffn2_w_up, v_ffn2_w_down=v_ffn2_w_down, v_final_norm=v_final_norm)
    weights = {n: given[n] for n in TWIN_WEIGHTS}
    shared = {n: given[n] for n in SHARED_INPUTS}
    per_example = {n: given[n] for n in ['x']}
    grad_fn = _jax.value_and_grad(_loss, argnums=(0, 1))

    def one_microbatch(ex, loss_target):
        ex = dict(ex)
        diff = ex.pop(TWIN_DIFF_INPUT)
        return grad_fn(weights, diff, {**shared, **ex}, loss_target)

    if N_MICROBATCH == 1:
        loss, (grad_w, grad_x) = one_microbatch(per_example, given["loss_target"])
    else:
        def body(carry, xs):
            loss_sum, grad_sum = carry
            l_k, (gw_k, gx_k) = one_microbatch(xs[0], xs[1])
            with _jax.named_scope("update"):
                return (loss_sum + l_k, _jax.tree.map(_jnp.add, grad_sum, gw_k)), gx_k

        init = (_jnp.zeros((), _jnp.float32), _jax.tree.map(_jnp.zeros_like, weights))
        (loss, grad_w), grad_x = _jax.lax.scan(body, init, (per_example, given["loss_target"]))
    with _jax.named_scope("update"):
        delta_w, new_m, new_v = {}, {}, {}
        for n in TWIN_WEIGHTS:
            delta_w[n], new_m[n], new_v[n] = _adamw(weights[n], grad_w[n], given["m_" + n], given["v_" + n])
    return (loss, grad_x, *[grad_w[n] for n in TWIN_WEIGHTS], *[delta_w[n] for n in TWIN_WEIGHTS],
            *[new_m[n] for n in TWIN_WEIGHTS], *[new_v[n] for n in TWIN_WEIGHTS])
```

```python
import functools
import math

import jax
import jax.numpy as jnp
from jax import lax
from jax.experimental import pallas as pl
from jax.experimental.pallas import tpu as pltpu

F32 = jnp.float32
MXU = jnp.bfloat16
D = 1024
FF = 2816
WA = 512
WB = 512
HD_B = 64
N_IN = 3744
N_INP = 3840
COL_R, COL_K, COL_V = 4, 5, 6
COL_L = 14
LORA = (32, 32, 96)
HG_CHUNK = 64
SCAN_CHUNK = 64
NORM_EPS = 1e-6
GN_EPS = 64e-5
L2_EPS = 1e-12
DECAY_C = math.exp(-0.5)
N_DEV = 8
LANES = 128
ROW_PAD = 512
VMEM_LIMIT = 56 * 1024 * 1024
ADAM_LR, ADAM_B1, ADAM_B2, ADAM_EPS, ADAM_WD, ADAM_STEP = 0.001, 0.9, 0.999, 1e-08, 0.01, 10
AXES = ("x", "y", "c")

SHARDED = ("ffn1_w_gate", "ffn1_w_up", "ffn1_w_down", "w_in", "rwkv_w2", "rwkv_a2", "rwkv_g2", "w_out",
           "ffn2_w_gate", "ffn2_w_up", "ffn2_w_down")
COL_SHARDED = {"ffn1_w_gate", "ffn1_w_up", "w_in", "rwkv_w2", "rwkv_a2", "rwkv_g2", "ffn2_w_gate", "ffn2_w_up"}
SMALL = ("ffn1_norm", "mix_norm", "hgrn_lb_logits", "hgrn_out_norm", "rwkv_shift_mu", "rwkv_w0", "rwkv_a0",
         "rwkv_k_k", "rwkv_k_a", "rwkv_r_k", "rwkv_gn_w", "rwkv_gn_b", "ffn2_norm", "final_norm")
WEIGHTS = ("ffn1_norm", "ffn1_w_gate", "ffn1_w_up", "ffn1_w_down", "mix_norm", "w_in", "hgrn_lb_logits",
           "hgrn_out_norm", "rwkv_shift_mu", "rwkv_w0", "rwkv_w2", "rwkv_a0", "rwkv_a2", "rwkv_g2", "rwkv_k_k",
           "rwkv_k_a", "rwkv_r_k", "rwkv_gn_w", "rwkv_gn_b", "w_out", "ffn2_norm", "ffn2_w_gate", "ffn2_w_up",
           "ffn2_w_down", "final_norm")


def _tile(n, cap):
    if n <= cap:
        return n
    for t in range(cap - cap % LANES, 0, -LANES):
        if n % t == 0:
            return t
    raise ValueError((n, cap))


def _params(n_axes):
    return pltpu.CompilerParams(dimension_semantics=("arbitrary",) * n_axes, vmem_limit_bytes=VMEM_LIMIT)


def _sig(x):
    return jax.nn.sigmoid(x)


def _dsilu(z, s):
    return s * (1.0 + z * (1.0 - s))


def _dot(a, b, dims=((1,), (0,)), precision=None):
    return lax.dot_general(a, b, (dims, ((), ())), preferred_element_type=F32, precision=precision)


_NT = ((1,), (1,))
_TN = ((0,), (0,))
_HI = lax.Precision.HIGHEST


def _iota(shape, dim):
    return lax.broadcasted_iota(jnp.int32, shape, dim)


def _split_dot(x, ones, passes):
    hi = x.astype(jnp.bfloat16)
    acc = _dot(hi, ones)
    rem = x
    for _ in range(passes - 1):
        rem = rem - hi.astype(F32)
        hi = rem.astype(jnp.bfloat16)
        acc = acc + _dot(hi, ones)
    return acc


def _head_ones(n, width):
    shift = width.bit_length() - 1
    return (_iota((n, n), 0) >> shift == _iota((n, n), 1) >> shift).astype(jnp.bfloat16)


def _matmul(a, b, *, ta=False, tb=False, out_dtype=F32, res=None, name):
    m, k = (a.shape[1], a.shape[0]) if ta else a.shape
    n = b.shape[0] if tb else b.shape[1]
    tm, tn, tk = _tile(m, 512), _tile(n, 768), _tile(k, 1024)
    nk = k // tk
    dims = ((0 if ta else 1,), (1 if tb else 0,))

    def body(*refs):
        a_ref, b_ref = refs[:2]
        o_ref, acc = refs[-2:]
        kk = pl.program_id(2)

        @pl.when(kk == 0)
        def _():
            acc[...] = jnp.zeros_like(acc)

        acc[...] += _dot(a_ref[...].astype(MXU), b_ref[...].astype(MXU), dims)

        @pl.when(kk == nk - 1)
        def _():
            v = acc[...]
            if res is not None:
                v = v + refs[2][...]
            o_ref[...] = v.astype(out_dtype)

    a_spec = pl.BlockSpec((tk, tm), lambda i, j, kk: (kk, i)) if ta else pl.BlockSpec((tm, tk), lambda i, j, kk: (i, kk))
    b_spec = pl.BlockSpec((tn, tk), lambda i, j, kk: (j, kk)) if tb else pl.BlockSpec((tk, tn), lambda i, j, kk: (kk, j))
    o_spec = pl.BlockSpec((tm, tn), lambda i, j, kk: (i, j))
    ins, specs = [a, b], [a_spec, b_spec]
    if res is not None:
        ins.append(res)
        specs.append(o_spec)
    return pl.pallas_call(
        body, name=name, grid=(m // tm, n // tn, nk), in_specs=specs, out_specs=o_spec,
        out_shape=jax.ShapeDtypeStruct((m, n), out_dtype), scratch_shapes=[pltpu.VMEM((tm, tn), F32)],
        compiler_params=_params(3))(*ins)


def _rms_fwd(x, g, name):
    t = x.shape[0]
    tb = _tile(t, 512)

    def body(x_ref, g_ref, o_ref):
        xv = x_ref[...]
        rinv = lax.rsqrt(jnp.mean(xv * xv, axis=-1, keepdims=True) + NORM_EPS)
        o_ref[...] = (xv * rinv * g_ref[...]).astype(MXU)

    return pl.pallas_call(
        body, name=name, grid=(t // tb,),
        in_specs=[pl.BlockSpec((tb, D), lambda i: (i, 0)), pl.BlockSpec((1, D), lambda i: (0, 0))],
        out_specs=pl.BlockSpec((tb, D), lambda i: (i, 0)), out_shape=jax.ShapeDtypeStruct((t, D), MXU),
        compiler_params=_params(1))(x, g)


def _rms_bwd(x, g, dh, dres, name):
    t = x.shape[0]
    tb = _tile(t, 512)

    def body(x_ref, g_ref, dh_ref, dres_ref, dx_ref, dg_ref):
        @pl.when(pl.program_id(0) == 0)
        def _():
            dg_ref[...] = jnp.zeros_like(dg_ref)

        xv = x_ref[...]
        rinv = lax.rsqrt(jnp.mean(xv * xv, axis=-1, keepdims=True) + NORM_EPS)
        xhat = xv * rinv
        dhv = dh_ref[...]
        dg_ref[...] += jnp.sum(dhv * xhat, axis=0, keepdims=True)
        dxhat = dhv * g_ref[...]
        dx_ref[...] = dres_ref[...] + rinv * (dxhat - xhat * jnp.mean(dxhat * xhat, axis=-1, keepdims=True))

    row = pl.BlockSpec((tb, D), lambda i: (i, 0))
    vec = pl.BlockSpec((1, D), lambda i: (0, 0))
    return pl.pallas_call(
        body, name=name, grid=(t // tb,), in_specs=[row, vec, row, row], out_specs=[row, vec],
        out_shape=[jax.ShapeDtypeStruct((t, D), F32), jax.ShapeDtypeStruct((1, D), F32)],
        compiler_params=_params(1))(x, g, dh, dres)


def _loss_head(x, g, target, name):
    t = x.shape[0]
    tb = _tile(t, 512)

    def body(x_ref, g_ref, t_ref, loss_ref, dx_ref, dg_ref):
        @pl.when(pl.program_id(0) == 0)
        def _():
            dg_ref[...] = jnp.zeros_like(dg_ref)
            loss_ref[...] = jnp.zeros_like(loss_ref)

        xv = x_ref[...]
        gv = g_ref[...]
        rinv = lax.rsqrt(jnp.mean(xv * xv, axis=-1, keepdims=True) + NORM_EPS)
        xhat = xv * rinv
        err = xhat * gv - t_ref[...]
        per_tok = jnp.mean(err * err, axis=-1, keepdims=True)
        loss_ref[...] += jnp.broadcast_to(0.5 * jnp.sum(per_tok, axis=0, keepdims=True), loss_ref.shape)
        dy = err * (1.0 / D)
        dg_ref[...] += jnp.sum(dy * xhat, axis=0, keepdims=True)
        dxhat = dy * gv
        dx_ref[...] = rinv * (dxhat - xhat * jnp.mean(dxhat * xhat, axis=-1, keepdims=True))

    row = pl.BlockSpec((tb, D), lambda i: (i, 0))
    vec = pl.BlockSpec((1, D), lambda i: (0, 0))
    return pl.pallas_call(
        body, name=name, grid=(t // tb,), in_specs=[row, vec, row],
        out_specs=[pl.BlockSpec((1, LANES), lambda i: (0, 0)), row, vec],
        out_shape=[jax.ShapeDtypeStruct((1, LANES), F32), jax.ShapeDtypeStruct((t, D), F32),
                   jax.ShapeDtypeStruct((1, D), F32)],
        compiler_params=_params(1))(x, g, target)


def _ffn_fwd(x, h, wg, wu, wd, name):
    t = x.shape[0]
    tb, fb = _tile(t, 1024), 256
    nf = FF // fb

    def body(x_ref, h_ref, wg_ref, wu_ref, wd_ref, o_ref, acc):
        f = pl.program_id(1)

        @pl.when(f == 0)
        def _():
            acc[...] = jnp.zeros_like(acc)

        hv = h_ref[...]
        gate = _dot(hv, wg_ref[...])
        up = _dot(hv, wu_ref[...])
        act = (gate * _sig(gate) * up).astype(MXU)
        acc[...] += _dot(act, wd_ref[...])

        @pl.when(f == nf - 1)
        def _():
            o_ref[...] = x_ref[...] + 0.5 * acc[...]

    row = pl.BlockSpec((tb, D), lambda i, f: (i, 0))
    col = pl.BlockSpec((D, fb), lambda i, f: (0, f))
    return pl.pallas_call(
        body, name=name, grid=(t // tb, nf),
        in_specs=[row, row, col, col, pl.BlockSpec((fb, D), lambda i, f: (f, 0))], out_specs=row,
        out_shape=jax.ShapeDtypeStruct((t, D), F32), scratch_shapes=[pltpu.VMEM((tb, D), F32)],
        compiler_params=_params(2))(x, h, wg, wu, wd)


def _ffn_bwd(h, dy, wg, wu, wd, name):
    t = h.shape[0]
    tb, fb = _tile(t, 512), 256
    nf = FF // fb

    def body(h_ref, dy_ref, wg_ref, wu_ref, wd_ref, dh_ref, act_ref, dg_ref, du_ref, dout_ref, acc):
        f = pl.program_id(1)

        @pl.when(f == 0)
        def _():
            acc[...] = jnp.zeros_like(acc)

        hv = h_ref[...]
        dout = (0.5 * dy_ref[...]).astype(MXU)
        dout_ref[...] = dout
        gate = _dot(hv, wg_ref[...])
        up = _dot(hv, wu_ref[...])
        dact = _dot(dout, wd_ref[...], _NT)
        s = _sig(gate)
        silu = gate * s
        act_ref[...] = (silu * up).astype(MXU)
        dup = (dact * silu).astype(MXU)
        dgate = (dact * up * _dsilu(gate, s)).astype(MXU)
        du_ref[...] = dup
        dg_ref[...] = dgate
        acc[...] += _dot(dgate, wg_ref[...], _NT) + _dot(dup, wu_ref[...], _NT)

        @pl.when(f == nf - 1)
        def _():
            dh_ref[...] = acc[...]

    row = pl.BlockSpec((tb, D), lambda i, f: (i, 0))
    col = pl.BlockSpec((D, fb), lambda i, f: (0, f))
    hid = pl.BlockSpec((tb, fb), lambda i, f: (i, f))
    hid_shape = jax.ShapeDtypeStruct((t, FF), MXU)
    return pl.pallas_call(
        body, name=name, grid=(t // tb, nf),
        in_specs=[row, row, col, col, pl.BlockSpec((fb, D), lambda i, f: (f, 0))],
        out_specs=[row, hid, hid, hid, row],
        out_shape=[jax.ShapeDtypeStruct((t, D), F32), hid_shape, hid_shape, hid_shape,
                   jax.ShapeDtypeStruct((t, D), MXU)],
        scratch_shapes=[pltpu.VMEM((tb, D), F32)], compiler_params=_params(2))(h, dy, wg, wu, wd)


def _hgrn_chunk(qa, fa, lbl_ref):
    c = HG_CHUNK
    lb = _sig(lbl_ref[0:1, :] - lbl_ref[1:2, :])
    sf = _sig(fa)
    forget = lb + (1.0 - lb) * sf
    kh = 1.0 - forget
    row, col = _iota((c, c), 0), _iota((c, c), 1)
    b = _dot((col <= row).astype(F32), jnp.log(forget), precision=_HI)
    bref, blast = b[c // 2:c // 2 + 1, :], b[c - 1:c, :]
    sq = _sig(qa)
    q = qa * sq
    qt, kt = q * jnp.exp(b - bref), kh * jnp.exp(bref - b)
    qb, kl = q * jnp.exp(b), kh * jnp.exp(blast - b)
    causal = col <= row
    return dict(lb=lb, sf=sf, forget=forget, sq=sq, qt=qt, kt=kt, qb=qb, kl=kl, decay=jnp.exp(blast),
                causal=causal, e_q=jnp.exp(b), e_qt=jnp.exp(b - bref), e_kt=jnp.exp(bref - b),
                e_kl=jnp.exp(blast - b))


def _hgrn_specs(t):
    c = HG_CHUNK
    return c, t // c, WA // LANES


def _hgrn_fwd(p_all, lbl, onorm, name):
    t = p_all.shape[0]
    c, n, nh = _hgrn_specs(t)

    def body(q_ref, f_ref, i_ref, g_ref, lbl_ref, on_ref, oa_ref, oraw_ref, st_ref, state):
        @pl.when(pl.program_id(1) == 0)
        def _():
            state[...] = jnp.zeros_like(state)

        k = _hgrn_chunk(q_ref[...], f_ref[...], lbl_ref)
        v = i_ref[...]
        st = state[...]
        st_ref[0, 0] = st
        a = jnp.where(k["causal"], _dot(k["qt"], k["kt"], _NT, _HI), 0.0)
        o = _dot(a, v, precision=_HI) + _dot(k["qb"], st, _NT, _HI)
        state[...] = st * k["decay"] + _dot(v, k["kl"], _TN, _HI)
        oraw_ref[...] = o
        rinv = lax.rsqrt(jnp.mean(o * o, axis=-1, keepdims=True) + NORM_EPS)
        ga = g_ref[...]
        oa_ref[...] = (o * rinv * on_ref[...] * (ga * _sig(ga))).astype(MXU)

    def blk(j):
        return pl.BlockSpec((c, LANES), lambda h, i: (i, j * nh + h))

    head = pl.BlockSpec((c, LANES), lambda h, i: (i, h))
    return pl.pallas_call(
        body, name=name, grid=(nh, n),
        in_specs=[blk(0), blk(1), blk(2), blk(3), pl.BlockSpec((2, LANES), lambda h, i: (0, h)),
                  pl.BlockSpec((1, LANES), lambda h, i: (0, h))],
        out_specs=[head, head, pl.BlockSpec((1, 1, LANES, LANES), lambda h, i: (h, i, 0, 0))],
        out_shape=[jax.ShapeDtypeStruct((t, WA), MXU), jax.ShapeDtypeStruct((t, WA), F32),
                   jax.ShapeDtypeStruct((nh, n, LANES, LANES), F32)],
        scratch_shapes=[pltpu.VMEM((LANES, LANES), F32)], compiler_params=_params(2))(
            p_all, p_all, p_all, p_all, lbl, onorm)


def _hgrn_bwd(p_all, lbl, onorm, oraw, states, doa, name):
    t = p_all.shape[0]
    c, n, nh = _hgrn_specs(t)

    def body(q_ref, f_ref, i_ref, g_ref, lbl_ref, on_ref, oraw_ref, st_ref, doa_ref,
             dq_ref, df_ref, di_ref, dg_ref, don_ref, dlbl_ref, dstate, dlb):
        @pl.when(pl.program_id(1) == 0)
        def _():
            dstate[...] = jnp.zeros_like(dstate)
            dlb[...] = jnp.zeros_like(dlb)
            don_ref[...] = jnp.zeros_like(don_ref)

        qa, fa, v, ga = q_ref[...], f_ref[...], i_ref[...], g_ref[...]
        k = _hgrn_chunk(qa, fa, lbl_ref)
        st, dst_next = st_ref[0, 0], dstate[...]
        o = oraw_ref[...]
        rinv = lax.rsqrt(jnp.mean(o * o, axis=-1, keepdims=True) + NORM_EPS)
        on = o * rinv
        sg = _sig(ga)
        gate = ga * sg
        dout = doa_ref[...]
        don_ref[...] += jnp.sum(dout * on * gate, axis=0, keepdims=True)
        dg_ref[...] = (dout * on * on_ref[...] * _dsilu(ga, sg)).astype(MXU)
        d_on = dout * on_ref[...] * gate
        do = rinv * (d_on - on * jnp.mean(d_on * on, axis=-1, keepdims=True))

        a = jnp.where(k["causal"], _dot(k["qt"], k["kt"], _NT, _HI), 0.0)
        dqb = _dot(do, st, precision=_HI)
        dstate[...] = dst_next * k["decay"] + _dot(do, k["qb"], _TN, _HI)
        da = jnp.where(k["causal"], _dot(do, v, _NT, _HI), 0.0)
        dqt = _dot(da, k["kt"], precision=_HI)
        dkt = _dot(da, k["qt"], _TN, _HI)
        dv = _dot(a, do, _TN, _HI) + _dot(k["kl"], dst_next, _NT, _HI)
        dkl = _dot(v, dst_next, precision=_HI)
        ddecay = jnp.sum(dst_next * st, axis=0, keepdims=True)
        dq = dqb * k["e_q"] + dqt * k["e_qt"]
        dk = dkt * k["e_kt"] + dkl * k["e_kl"]
        tq, tk, tl = dqt * k["qt"], dkt * k["kt"], dkl * k["kl"]
        db = dqb * k["qb"] + tq - tk - tl
        dbref = jnp.sum(tk - tq, axis=0, keepdims=True)
        dblast = jnp.sum(tl, axis=0, keepdims=True) + ddecay * k["decay"]
        rows = _iota((c, LANES), 0)
        db = db + jnp.where(rows == c // 2, dbref, 0.0) + jnp.where(rows == c - 1, dblast, 0.0)
        row, col = _iota((c, c), 0), _iota((c, c), 1)
        dlogf = _dot((col >= row).astype(F32), db, precision=_HI)
        dq_ref[...] = (dq * _dsilu(qa, k["sq"])).astype(MXU)
        di_ref[...] = dv.astype(MXU)
        dforget = dlogf / k["forget"] - dk
        sf, lb = k["sf"], k["lb"]
        df_ref[...] = (dforget * (1.0 - lb) * sf * (1.0 - sf)).astype(MXU)
        dlb[...] += jnp.sum(dforget * (1.0 - sf), axis=0, keepdims=True)
        dl0 = dlb[...] * lb * (1.0 - lb)
        dlbl_ref[...] = jnp.where(_iota((2, LANES), 0) == 0, dl0, -dl0)

    def blk(j):
        return pl.BlockSpec((c, LANES), lambda h, i: (n - 1 - i, j * nh + h))

    head = pl.BlockSpec((c, LANES), lambda h, i: (n - 1 - i, h))
    vec = pl.BlockSpec((1, LANES), lambda h, i: (0, h))
    lg = pl.BlockSpec((2, LANES), lambda h, i: (0, h))
    grad = jax.ShapeDtypeStruct((t, WA), MXU)
    return pl.pallas_call(
        body, name=name, grid=(nh, n),
        in_specs=[blk(0), blk(1), blk(2), blk(3), lg, vec, head,
                  pl.BlockSpec((1, 1, LANES, LANES), lambda h, i: (h, n - 1 - i, 0, 0)), head],
        out_specs=[head, head, head, head, vec, lg],
        out_shape=[grad, grad, grad, grad, jax.ShapeDtypeStruct((1, WA), F32), jax.ShapeDtypeStruct((2, WA), F32)],
        scratch_shapes=[pltpu.VMEM((LANES, LANES), F32), pltpu.VMEM((1, LANES), F32)],
        compiler_params=_params(2))(p_all, p_all, p_all, p_all, lbl, onorm, oraw, states, doa)


def _lora_act(x):
    lane = _iota(x.shape, 1)
    n_w, n_a, n_g = LORA
    return jnp.where(lane < n_w, jnp.tanh(x),
                     jnp.where(lane < n_w + n_a, x, jnp.where(lane < n_w + n_a + n_g, _sig(x), 0.0)))


def _lora_dact(x):
    lane = _iota(x.shape, 1)
    n_w, n_a, n_g = LORA
    th, s = jnp.tanh(x), _sig(x)
    return jnp.where(lane < n_w, 1.0 - th * th,
                     jnp.where(lane < n_w + n_a, 1.0, jnp.where(lane < n_w + n_a + n_g, s * (1.0 - s), 0.0)))


def _shift_down(cur, prev8, first):
    rolled = pltpu.roll(cur, 1, 0)
    edge = prev8[7:8, :] * jnp.where(first, 0.0, 1.0)
    return jnp.where(_iota(cur.shape, 0) == 0, edge, rolled)


def _shift_up(cur, next8, last):
    rows = cur.shape[0]
    rolled = pltpu.roll(cur, rows - 1, 0)
    edge = next8[0:1, :] * jnp.where(last, 0.0, 1.0)
    return jnp.where(_iota(cur.shape, 0) == rows - 1, edge, rolled)


def _rwkv_inputs(refs, first, ones):
    (pr, pk, pv, plo, qr, qk, qv, qlo, mr, mk, mv, mlo, w2c, w0, a0, kk_w, ka_w) = refs
    mix = lambda cur, prev, mu: cur[...] + mu[...] * (_shift_down(cur[...], prev[...], first) - cur[...])
    r, k, v, lo = mix(pr, qr, mr), mix(pk, qk, mk), mix(pv, qv, mv), mix(plo, qlo, mlo)
    z = _lora_act(lo)
    lin = _dot(z.astype(MXU), w2c[...])
    sg = _sig(w0[...] + lin[:, :WB])
    decay = jnp.exp(-DECAY_C * sg)
    a = _sig(a0[...] + lin[:, WB:2 * WB])
    g = lin[:, 2 * WB:]
    kk0 = k * kk_w[...]
    nrm = jnp.sqrt(_split_dot(kk0 * kk0, ones, 3))
    den = jnp.maximum(nrm, L2_EPS)
    kk = kk0 / den
    k2 = k * (1.0 + (a - 1.0) * ka_w[...])
    return dict(r=r, k=k, v=v, lo=lo, z=z, sg=sg, decay=decay, a=a, g=g, kk=kk, den=den, nrm=nrm, k2=k2)


def _rwkv_in_specs(t, tb):
    nt8 = tb // 8

    def cur(w, j):
        return pl.BlockSpec((tb, w), lambda i: (i, j))

    def prev(w, j):
        return pl.BlockSpec((8, w), lambda i: (jnp.maximum(i * nt8 - 1, 0), j))

    def vec(w, j=0):
        return pl.BlockSpec((1, w), lambda i: (0, j))

    return [cur(WB, COL_R), cur(WB, COL_K), cur(WB, COL_V), cur(256, COL_L),
            prev(WB, COL_R), prev(WB, COL_K), prev(WB, COL_V), prev(256, COL_L),
            vec(WB, 0), vec(WB, 1), vec(WB, 2), vec(256, 6),
            pl.BlockSpec((256, 3 * WB), lambda i: (0, 0)), vec(WB), vec(WB), vec(WB), vec(WB)]


def _rwkv_in_args(p_all, mu_pad, w2cat, w0, a0, k_k, k_a):
    return (p_all,) * 8 + (mu_pad,) * 4 + (w2cat, w0, a0, k_k, k_a)


def _rwkv_prep(p_all, mu_pad, w2cat, w0, a0, k_k, k_a, name):
    t = p_all.shape[0]
    tb = _tile(t, 256)

    def body(*refs):
        ins, outs = refs[:17], refs[17:]
        q = _rwkv_inputs(ins, pl.program_id(0) == 0, _head_ones(WB, HD_B))
        for ref, val in zip(outs, (q["r"], q["decay"], q["k2"], q["v"], -q["kk"], q["kk"] * q["a"], q["g"])):
            ref[...] = val

    out = pl.BlockSpec((tb, WB), lambda i: (i, 0))
    return pl.pallas_call(
        body, name=name, grid=(t // tb,), in_specs=_rwkv_in_specs(t, tb), out_specs=[out] * 7,
        out_shape=[jax.ShapeDtypeStruct((t, WB), F32)] * 7, compiler_params=_params(1))(
            *_rwkv_in_args(p_all, mu_pad, w2cat, w0, a0, k_k, k_a))


def _pair_rows(x8, i):
    return jnp.concatenate([jnp.broadcast_to(x8[i:i + 1, p * LANES:(p + 1) * LANES], (HD_B, LANES))
                            for p in range(4)], axis=0)


def _pair_sums(x):
    return jnp.concatenate([jnp.sum(x[p * HD_B:(p + 1) * HD_B], axis=0, keepdims=True) for p in range(4)], axis=1)


def _put_row(buf, i, row):
    return jnp.where(_iota(buf.shape, 0) == i, row, buf)


def _scan_consts():
    ones = _head_ones(LANES, HD_B)
    diag = (_iota((4 * HD_B, LANES), 0) & (HD_B - 1) == _iota((4 * HD_B, LANES), 1) & (HD_B - 1)).astype(F32)
    return ones, diag


def _rwkv_scan_fwd(r, w, k, v, a, b, name):
    t = r.shape[0]
    cc = min(t, SCAN_CHUNK)

    def body(r_ref, w_ref, k_ref, v_ref, a_ref, b_ref, y_ref, sall_ref, state):
        @pl.when(pl.program_id(0) == 0)
        def _():
            state[...] = jnp.zeros_like(state)

        ones, diag = _scan_consts()

        def block(j, s):
            base = pl.multiple_of(j * 8, 8)
            rows = pl.ds(base, 8)
            r8, w8, k8, v8, a8, b8 = (ref[rows, :] for ref in (r_ref, w_ref, k_ref, v_ref, a_ref, b_ref))
            y8 = jnp.zeros((8, WB), F32)
            for i in range(8):
                sa = _split_dot(s * _pair_rows(a8, i), ones, 3)
                vb = _split_dot(_pair_rows(v8, i) * diag, ones, 2)
                s = s * _pair_rows(w8, i) + sa * _pair_rows(b8, i) + vb * _pair_rows(k8, i)
                sall_ref[base + i] = s
                yb = _split_dot(s * _pair_rows(r8, i), ones, 2)
                y8 = _put_row(y8, i, _pair_sums(yb * diag))
            y_ref[rows, :] = y8
            return s

        state[...] = lax.fori_loop(0, cc // 8, block, state[...])

    row = pl.BlockSpec((cc, WB), lambda i: (i, 0))
    return pl.pallas_call(
        body, name=name, grid=(t // cc,), in_specs=[row] * 6,
        out_specs=[row, pl.BlockSpec((cc, 4 * HD_B, LANES), lambda i: (i, 0, 0))],
        out_shape=[jax.ShapeDtypeStruct((t, WB), F32), jax.ShapeDtypeStruct((t, 4 * HD_B, LANES), F32)],
        scratch_shapes=[pltpu.VMEM((4 * HD_B, LANES), F32)], compiler_params=_params(1))(r, w, k, v, a, b)


def _rwkv_scan_bwd(dy, r, w, k, v, a, b, sall, name):
    t = r.shape[0]
    cc = min(t, SCAN_CHUNK)
    n = t // cc

    def body(dy_ref, r_ref, w_ref, k_ref, v_ref, a_ref, b_ref, sall_ref, sprev_ref,
             dr_ref, dw_ref, dk_ref, dv_ref, da_ref, db_ref, dstate):
        @pl.when(pl.program_id(0) == 0)
        def _():
            dstate[...] = jnp.zeros_like(dstate)

        ones, diag = _scan_consts()
        before_chunk = jnp.where(pl.program_id(0) == n - 1, 0.0, 1.0) * sprev_ref[0]

        def block(jj, carry):
            ds, sc = carry
            j = cc // 8 - 1 - jj
            base = pl.multiple_of(j * 8, 8)
            rows = pl.ds(base, 8)
            dy8, r8, w8, k8, v8, a8, b8 = (ref[rows, :] for ref in
                                           (dy_ref, r_ref, w_ref, k_ref, v_ref, a_ref, b_ref))
            outs = [jnp.zeros((8, WB), F32) for _ in range(6)]
            for i in range(7, -1, -1):
                if i > 0:
                    sp = sall_ref[base + i - 1]
                else:
                    sp = jnp.where(j == 0, before_chunk, sall_ref[jnp.maximum(base - 1, 0)])
                a_b, b_b, k_b, w_b = _pair_rows(a8, i), _pair_rows(b8, i), _pair_rows(k8, i), _pair_rows(w8, i)
                dyb = _split_dot(_pair_rows(dy8, i) * diag, ones, 2)
                vb = _split_dot(_pair_rows(v8, i) * diag, ones, 2)
                ds = ds + dyb * _pair_rows(r8, i)
                sa = _split_dot(sp * a_b, ones, 3)
                dsa = _split_dot(ds * b_b, ones, 3)
                dvb = _split_dot(ds * k_b, ones, 2)
                vals = (_pair_sums(sc * dyb), _pair_sums(ds * sp), _pair_sums(ds * vb), _pair_sums(dvb * diag),
                        _pair_sums(sp * dsa), _pair_sums(ds * sa))
                outs = [_put_row(o, i, val) for o, val in zip(outs, vals)]
                ds = ds * w_b + dsa * a_b
                sc = sp
            for ref, o in zip((dr_ref, dw_ref, dk_ref, dv_ref, da_ref, db_ref), outs):
                ref[rows, :] = o
            return ds, sc

        ds, _ = lax.fori_loop(0, cc // 8, block, (dstate[...], sall_ref[cc - 1]))
        dstate[...] = ds

    row = pl.BlockSpec((cc, WB), lambda i: (n - 1 - i, 0))
    return pl.pallas_call(
        body, name=name, grid=(n,),
        in_specs=[row] * 7 + [pl.BlockSpec((cc, 4 * HD_B, LANES), lambda i: (n - 1 - i, 0, 0)),
                              pl.BlockSpec((1, 4 * HD_B, LANES), lambda i: (jnp.maximum((n - 1 - i) * cc - 1, 0), 0, 0))],
        out_specs=[row] * 6, out_shape=[jax.ShapeDtypeStruct((t, WB), F32)] * 6,
        scratch_shapes=[pltpu.VMEM((4 * HD_B, LANES), F32)], compiler_params=_params(1))(
            dy, r, w, k, v, a, b, sall, sall)


def _rwkv_post(y, r, k2, v, g, r_k, gn_w, gn_b, name):
    t = y.shape[0]
    tb = _tile(t, 256)

    def body(y_ref, r_ref, k_ref, v_ref, g_ref, rk_ref, gw_ref, gb_ref, o_ref):
        ones = _head_ones(WB, HD_B)
        yv = y_ref[...]
        yc = yv - _split_dot(yv, ones, 3) * (1.0 / HD_B)
        rstd = lax.rsqrt(_split_dot(yc * yc, ones, 3) * (1.0 / HD_B) + GN_EPS)
        rk = _split_dot(r_ref[...] * k_ref[...] * rk_ref[...], ones, 3)
        o_ref[...] = ((yc * rstd * gw_ref[...] + gb_ref[...] + rk * v_ref[...]) * g_ref[...]).astype(MXU)

    row = pl.BlockSpec((tb, WB), lambda i: (i, 0))
    vec = pl.BlockSpec((1, WB), lambda i: (0, 0))
    return pl.pallas_call(
        body, name=name, grid=(t // tb,), in_specs=[row] * 5 + [vec] * 3, out_specs=row,
        out_shape=jax.ShapeDtypeStruct((t, WB), MXU), compiler_params=_params(1))(y, r, k2, v, g, r_k, gn_w, gn_b)


def _rwkv_post_bwd(dob, y, r, k2, v, g, r_k, gn_w, gn_b, name):
    t = y.shape[0]
    tb = _tile(t, 256)

    def body(do_ref, y_ref, r_ref, k_ref, v_ref, g_ref, rk_ref, gw_ref, gb_ref,
             dy_ref, dg_ref, dr_ref, dk_ref, dv_ref, dgw_ref, dgb_ref, drk_ref):
        @pl.when(pl.program_id(0) == 0)
        def _():
            dgw_ref[...] = jnp.zeros_like(dgw_ref)
            dgb_ref[...] = jnp.zeros_like(dgb_ref)
            drk_ref[...] = jnp.zeros_like(drk_ref)

        ones = _head_ones(WB, HD_B)
        seg = lambda x: _split_dot(x, ones, 3)
        yv, rv, kv, vv, gv = y_ref[...], r_ref[...], k_ref[...], v_ref[...], g_ref[...]
        yc = yv - seg(yv) * (1.0 / HD_B)
        rstd = lax.rsqrt(seg(yc * yc) * (1.0 / HD_B) + GN_EPS)
        yn = yc * rstd
        rk = seg(rv * kv * rk_ref[...])
        dob_v = do_ref[...]
        dg_ref[...] = dob_v * (yn * gw_ref[...] + gb_ref[...] + rk * vv)
        dyg = dob_v * gv
        dgw_ref[...] += jnp.sum(dyg * yn, axis=0, keepdims=True)
        dgb_ref[...] += jnp.sum(dyg, axis=0, keepdims=True)
        dyn = dyg * gw_ref[...]
        dy_ref[...] = rstd * (dyn - (seg(dyn) + yn * seg(dyn * yn)) * (1.0 / HD_B))
        drk = seg(dyg * vv)
        dv_ref[...] = dyg * rk
        dr_ref[...] = drk * kv * rk_ref[...]
        dk_ref[...] = drk * rv * rk_ref[...]
        drk_ref[...] += jnp.sum(drk * rv * kv, axis=0, keepdims=True)

    row = pl.BlockSpec((tb, WB), lambda i: (i, 0))
    vec = pl.BlockSpec((1, WB), lambda i: (0, 0))
    full, small = jax.ShapeDtypeStruct((t, WB), F32), jax.ShapeDtypeStruct((1, WB), F32)
    return pl.pallas_call(
        body, name=name, grid=(t // tb,), in_specs=[row] * 6 + [vec] * 3, out_specs=[row] * 5 + [vec] * 3,
        out_shape=[full] * 5 + [small] * 3, compiler_params=_params(1))(dob, y, r, k2, v, g, r_k, gn_w, gn_b)


def _rwkv_prep_bwd(grads, p_all, mu_pad, w2cat, w0, a0, k_k, k_a, name):
    t = p_all.shape[0]
    tb = _tile(t, 256)

    def body(*refs):
        g_refs, ins, outs = refs[:10], refs[10:27], refs[27:]
        dr_s, dw, dk2_s, dv_s, das, dbs, dg, dr_b, dk2_b, dv_b = (ref[...] for ref in g_refs)
        dr_ref, dk_ref, dv_ref, dlo_ref, dw2_ref, dw0_ref, da0_ref, dkk_ref, dka_ref = outs

        @pl.when(pl.program_id(0) == 0)
        def _():
            for ref in (dw2_ref, dw0_ref, da0_ref, dkk_ref, dka_ref):
                ref[...] = jnp.zeros_like(ref)

        ones = _head_ones(WB, HD_B)
        q = _rwkv_inputs(ins, pl.program_id(0) == 0, ones)
        kk_w, ka_w = ins[15][...], ins[16][...]
        a, kk, k = q["a"], q["kk"], q["k"]
        dk2 = dk2_s + dk2_b
        dkk = dbs * a - das
        da = dbs * kk + dk2 * k * ka_w
        dk = dk2 * (1.0 + (a - 1.0) * ka_w)
        dka_ref[...] += jnp.sum(dk2 * k * (a - 1.0), axis=0, keepdims=True)
        proj = jnp.where(q["nrm"] > L2_EPS, _split_dot(dkk * kk, ones, 3), 0.0)
        dkk0 = (dkk - kk * proj) / q["den"]
        dk = dk + dkk0 * kk_w
        dkk_ref[...] += jnp.sum(dkk0 * k, axis=0, keepdims=True)
        dal = da * a * (1.0 - a)
        da0_ref[...] += jnp.sum(dal, axis=0, keepdims=True)
        sg = q["sg"]
        dwl = dw * q["decay"] * (-DECAY_C) * sg * (1.0 - sg)
        dw0_ref[...] += jnp.sum(dwl, axis=0, keepdims=True)
        dlin = jnp.concatenate([dwl, dal, dg], axis=1).astype(MXU)
        dw2_ref[...] += _dot(q["z"].astype(MXU), dlin, _TN)
        dz = _dot(dlin, ins[12][...], _NT)
        dlo_ref[...] = dz * _lora_dact(q["lo"])
        dr_ref[...] = dr_s + dr_b
        dk_ref[...] = dk
        dv_ref[...] = dv_s + dv_b

    row = pl.BlockSpec((tb, WB), lambda i: (i, 0))
    vec = pl.BlockSpec((1, WB), lambda i: (0, 0))
    full, small = jax.ShapeDtypeStruct((t, WB), F32), jax.ShapeDtypeStruct((1, WB), F32)
    return pl.pallas_call(
        body, name=name, grid=(t // tb,), in_specs=[row] * 10 + _rwkv_in_specs(t, tb),
        out_specs=[row] * 3 + [pl.BlockSpec((tb, 256), lambda i: (i, 0)),
                               pl.BlockSpec((256, 3 * WB), lambda i: (0, 0))] + [vec] * 4,
        out_shape=[full] * 3 + [jax.ShapeDtypeStruct((t, 256), F32), jax.ShapeDtypeStruct((256, 3 * WB), F32)]
        + [small] * 4, compiler_params=_params(1))(*grads, *_rwkv_in_args(p_all, mu_pad, w2cat, w0, a0, k_k, k_a))


def _shift_bwd(dshifted, p_all, mu_pad, name):
    t = p_all.shape[0]
    tb = _tile(t, 256)
    nt, nt8 = t // tb, tb // 8
    widths, cols, mus = (WB, WB, WB, 256), (COL_R, COL_K, COL_V, COL_L), (0, 1, 2, 6)

    def body(*refs):
        d_refs, n_refs, p_refs, q_refs, m_refs = refs[0:4], refs[4:8], refs[8:12], refs[12:16], refs[16:20]
        o_refs, dmu_refs = refs[20:24], refs[24:28]
        i = pl.program_id(0)

        @pl.when(i == 0)
        def _():
            for ref in dmu_refs:
                ref[...] = jnp.zeros_like(ref)

        for d, nx, p, q, m, o, dmu in zip(d_refs, n_refs, p_refs, q_refs, m_refs, o_refs, dmu_refs):
            dv, pv, mu = d[...], p[...], m[...]
            o[...] = (dv * (1.0 - mu) + mu * _shift_up(dv, nx[...], i == nt - 1)).astype(MXU)
            dmu[...] += jnp.sum(dv * (_shift_down(pv, q[...], i == 0) - pv), axis=0, keepdims=True)

    cur_d = [pl.BlockSpec((tb, w), lambda i: (i, 0)) for w in widths]
    next_d = [pl.BlockSpec((8, w), lambda i: (jnp.minimum((i + 1) * nt8, t // 8 - 1), 0)) for w in widths]
    cur_p = [pl.BlockSpec((tb, w), lambda i, j=j: (i, j)) for w, j in zip(widths, cols)]
    prev_p = [pl.BlockSpec((8, w), lambda i, j=j: (jnp.maximum(i * nt8 - 1, 0), j)) for w, j in zip(widths, cols)]
    mu_s = [pl.BlockSpec((1, w), lambda i, j=j: (0, j)) for w, j in zip(widths, mus)]
    vecs = [pl.BlockSpec((1, w), lambda i: (0, 0)) for w in widths]
    return pl.pallas_call(
        body, name=name, grid=(nt,), in_specs=cur_d + next_d + cur_p + prev_p + mu_s, out_specs=cur_d + vecs,
        out_shape=[jax.ShapeDtypeStruct((t, w), MXU) for w in widths]
        + [jax.ShapeDtypeStruct((1, w), F32) for w in widths],
        compiler_params=_params(1))(*dshifted, *dshifted, *(p_all,) * 8, *(mu_pad,) * 4)


def _peer(k):
    x, y, c = (lax.axis_index(n) for n in AXES)
    px = 1 - x if k & 4 else x
    py = 1 - y if k & 2 else y
    pc = 1 - c if k & 1 else c
    return (px, py, pc), 4 * px + 2 * py + pc


def _exchange(src, per_peer, name):
    shape = src.shape[1:] if per_peer else src.shape

    def body(src_ref, out_ref, send_sems, recv_sems, local_sem):
        _, me = _peer(0)

        def copy(k):
            peer, idx = _peer(k)
            return pltpu.make_async_remote_copy(
                src_ref=src_ref.at[idx] if per_peer else src_ref, dst_ref=out_ref.at[me],
                send_sem=send_sems.at[k - 1], recv_sem=recv_sems.at[k - 1],
                device_id=peer, device_id_type=pl.DeviceIdType.MESH)

        def arrival(k):
            peer, idx = _peer(k)
            return pltpu.make_async_remote_copy(
                src_ref=src_ref.at[idx] if per_peer else src_ref, dst_ref=out_ref.at[idx],
                send_sem=send_sems.at[k - 1], recv_sem=recv_sems.at[k - 1],
                device_id=peer, device_id_type=pl.DeviceIdType.MESH)

        mine = pltpu.make_async_copy(src_ref.at[me] if per_peer else src_ref, out_ref.at[me], local_sem)
        mine.start()
        sends = [copy(k) for k in range(1, N_DEV)]
        for cp in sends:
            cp.start()
        for k in range(1, N_DEV):
            arrival(k).wait_recv()
        for cp in sends:
            cp.wait_send()
        mine.wait()

    return pl.pallas_call(
        body, name=name, in_specs=[pl.BlockSpec(memory_space=pltpu.HBM)],
        out_specs=pl.BlockSpec(memory_space=pltpu.HBM),
        out_shape=jax.ShapeDtypeStruct((N_DEV,) + tuple(shape), src.dtype),
        scratch_shapes=[pltpu.SemaphoreType.DMA((N_DEV - 1,)), pltpu.SemaphoreType.DMA((N_DEV - 1,)),
                        pltpu.SemaphoreType.DMA])(src)


def _adamw(parts, w, m, v, name):
    rows = w.shape[0]
    tb = ROW_PAD
    c1, c2 = 1.0 - ADAM_B1 ** ADAM_STEP, 1.0 - ADAM_B2 ** ADAM_STEP

    def body(p_ref, w_ref, m_ref, v_ref, g_ref, d_ref, nm_ref, nv_ref):
        g = p_ref[0]
        for d in range(1, N_DEV):
            g = g + p_ref[d]
        nm = ADAM_B1 * m_ref[...] + (1.0 - ADAM_B1) * g
        nv = ADAM_B2 * v_ref[...] + (1.0 - ADAM_B2) * (g * g)
        g_ref[...] = g
        nm_ref[...] = nm
        nv_ref[...] = nv
        d_ref[...] = -ADAM_LR * ((nm / c1) / (jnp.sqrt(nv / c2) + ADAM_EPS) + ADAM_WD * w_ref[...])

    row = pl.BlockSpec((tb, LANES), lambda i: (i, 0))
    out = jax.ShapeDtypeStruct((rows, LANES), F32)
    return pl.pallas_call(
        body, name=name, grid=(rows // tb,),
        in_specs=[pl.BlockSpec((N_DEV, tb, LANES), lambda i: (0, i, 0)), row, row, row], out_specs=[row] * 4,
        out_shape=[out] * 4, compiler_params=_params(1))(parts, w, m, v)


def _rows(a, multiple):
    flat = a.reshape(-1)
    pad = -flat.shape[0] % (multiple * LANES)
    if pad:
        flat = jnp.concatenate([flat, jnp.zeros((pad,), a.dtype)])
    return flat.reshape(-1, LANES)


def _pack(arrs, multiple, total_multiple=None):
    pieces = [_rows(a, multiple) for a in arrs]
    if total_multiple:
        rows = sum(p.shape[0] for p in pieces)
        pieces.append(jnp.zeros((-rows % total_multiple, LANES), pieces[0].dtype))
    return jnp.concatenate(pieces, axis=0)


def _unpack(packed, shapes, multiple):
    lead = packed.shape[:-2]
    out, off = [], 0
    for shp in shapes:
        size = math.prod(shp)
        n = -(-size // (multiple * LANES)) * multiple
        piece = packed[..., off:off + n, :].reshape(lead + (n * LANES,))
        out.append(piece[..., :size].reshape(lead + tuple(shp)))
        off += n
    return out


def _gathered_to_full(g, name, shard_shape):
    g = g.reshape((N_DEV,) + shard_shape)
    if name in COL_SHARDED:
        return jnp.transpose(g, (1, 0, 2)).reshape(shard_shape[0], N_DEV * shard_shape[1])
    return g.reshape(N_DEV * shard_shape[0], shard_shape[1])


def _full_to_per_device(full, name):
    if name in COL_SHARDED:
        r, c = full.shape
        return jnp.transpose(full.reshape(r, N_DEV, c // N_DEV), (1, 0, 2))
    return full.reshape(N_DEV, full.shape[0] // N_DEV, full.shape[1])


def _w2cat(w2, a2, g2):
    n_w, n_a, n_g = LORA
    out = jnp.zeros((256, 3 * WB), w2.dtype)
    out = out.at[0:n_w, 0:WB].set(w2)
    out = out.at[n_w:n_w + n_a, WB:2 * WB].set(a2)
    return out.at[n_w + n_a:n_w + n_a + n_g, 2 * WB:].set(g2)


def _local_step(x, target, w):
    n_w, n_a, n_g = LORA
    w_in_pad = jnp.pad(w["w_in"], ((0, 0), (0, N_INP - N_IN)))
    mu_pad = jnp.pad(w["rwkv_shift_mu"], ((0, 0), (0, 1792 - 1696)))
    w2cat = _w2cat(w["rwkv_w2"], w["rwkv_a2"], w["rwkv_g2"])
    r_k = w["rwkv_r_k"].reshape(1, WB)
    rw = (mu_pad, w2cat, w["rwkv_w0"], w["rwkv_a0"], w["rwkv_k_k"], w["rwkv_k_a"])

    h1 = _rms_fwd(x, w["ffn1_norm"], "ffn1_norm")
    x1 = _ffn_fwd(x, h1, w["ffn1_w_gate"], w["ffn1_w_up"], w["ffn1_w_down"], "ffn1_fwd")
    h2 = _rms_fwd(x1, w["mix_norm"], "mix_norm")
    p_all = _matmul(h2, w_in_pad, name="in_proj")
    oa, oraw, states = _hgrn_fwd(p_all, w["hgrn_lb_logits"], w["hgrn_out_norm"], "hgrn_fwd")
    r, decay, k2, v, sa, sb, g = _rwkv_prep(p_all, *rw, "rwkv_prep")
    y, sall = _rwkv_scan_fwd(r, decay, k2, v, sa, sb, "rwkv_scan_fwd")
    post_w = (r_k, w["rwkv_gn_w"], w["rwkv_gn_b"])
    ob = _rwkv_post(y, r, k2, v, g, *post_w, "rwkv_post")
    o = jnp.concatenate([oa, ob], axis=1)
    x2 = _matmul(o, w["w_out"], res=x1, name="out_proj")
    h3 = _rms_fwd(x2, w["ffn2_norm"], "ffn2_norm")
    x3 = _ffn_fwd(x2, h3, w["ffn2_w_gate"], w["ffn2_w_up"], w["ffn2_w_down"], "ffn2_fwd")
    loss, dx3, d_final = _loss_head(x3, w["final_norm"].reshape(1, D), target, "loss_head")

    grads = {"final_norm": d_final.reshape(D)}

    def ffn_back(prefix, h, dy, x_in, norm):
        wg, wu, wd = (w[prefix + s] for s in ("_w_gate", "_w_up", "_w_down"))
        dh, act, dgate, dup, dout = _ffn_bwd(h, dy, wg, wu, wd, prefix + "_bwd")
        grads[prefix + "_w_gate"] = _matmul(h, dgate, ta=True, name=prefix + "_dwg")
        grads[prefix + "_w_up"] = _matmul(h, dup, ta=True, name=prefix + "_dwu")
        grads[prefix + "_w_down"] = _matmul(act, dout, ta=True, name=prefix + "_dwd")
        dx, grads[prefix + "_norm"] = _rms_bwd(x_in, norm, dh, dy, prefix + "_norm_bwd")
        return dx

    dx2 = ffn_back("ffn2", h3, dx3, x2, w["ffn2_norm"])
    grads["w_out"] = _matmul(o, dx2, ta=True, name="d_w_out")
    do = _matmul(dx2, w["w_out"], tb=True, name="d_mixed")
    dqa, dfa, dia, dga, grads["hgrn_out_norm"], grads["hgrn_lb_logits"] = _hgrn_bwd(
        p_all, w["hgrn_lb_logits"], w["hgrn_out_norm"], oraw, states, do[:, :WA], "hgrn_bwd")
    dy, dg, dr_b, dk2_b, dv_b, grads["rwkv_gn_w"], grads["rwkv_gn_b"], d_rk = _rwkv_post_bwd(
        do[:, WA:], y, r, k2, v, g, *post_w, "rwkv_post_bwd")
    grads["rwkv_r_k"] = d_rk.reshape(w["rwkv_r_k"].shape)
    dr, dw, dk2, dv, dsa, dsb = _rwkv_scan_bwd(dy, r, decay, k2, v, sa, sb, sall, "rwkv_scan_bwd")
    (dsr, dsk, dsv, dslo, dw2cat, grads["rwkv_w0"], grads["rwkv_a0"], grads["rwkv_k_k"],
     grads["rwkv_k_a"]) = _rwkv_prep_bwd((dr, dw, dk2, dv, dsa, dsb, dg, dr_b, dk2_b, dv_b), p_all, *rw,
                                         "rwkv_prep_bwd")
    grads["rwkv_w2"] = dw2cat[0:n_w, 0:WB]
    grads["rwkv_a2"] = dw2cat[n_w:n_w + n_a, WB:2 * WB]
    grads["rwkv_g2"] = dw2cat[n_w + n_a:n_w + n_a + n_g, 2 * WB:]
    dpr, dpk, dpv, dplo, dmu_r, dmu_k, dmu_v, dmu_lo = _shift_bwd((dsr, dsk, dsv, dslo), p_all, mu_pad, "shift_bwd")
    grads["rwkv_shift_mu"] = jnp.concatenate([dmu_r, dmu_k, dmu_v, dmu_lo], axis=1)[:, :1696]
    dp = jnp.concatenate([dqa, dfa, dia, dga, dpr, dpk, dpv, dplo], axis=1)
    grads["w_in"] = _matmul(h2, dp, ta=True, name="d_w_in")[:, :N_IN]
    dh2 = _matmul(dp, w_in_pad, tb=True, name="d_h2")
    dx1, grads["mix_norm"] = _rms_bwd(x1, w["mix_norm"], dh2, dx2, "mix_norm_bwd")
    dx0 = ffn_back("ffn1", h1, dx1, x, w["ffn1_norm"])
    return loss[0, 0], dx0, grads


def kernel(x, ffn1_norm, ffn1_w_gate, ffn1_w_up, ffn1_w_down, mix_norm, w_in, hgrn_lb_logits, hgrn_out_norm, rwkv_shift_mu, rwkv_w0, rwkv_w2, rwkv_a0, rwkv_a2, rwkv_g2, rwkv_k_k, rwkv_k_a, rwkv_r_k, rwkv_gn_w, rwkv_gn_b, w_out, ffn2_norm, ffn2_w_gate, ffn2_w_up, ffn2_w_down, final_norm, loss_target, m_ffn1_norm, m_ffn1_w_gate, m_ffn1_w_up, m_ffn1_w_down, m_mix_norm, m_w_in, m_hgrn_lb_logits, m_hgrn_out_norm, m_rwkv_shift_mu, m_rwkv_w0, m_rwkv_w2, m_rwkv_a0, m_rwkv_a2, m_rwkv_g2, m_rwkv_k_k, m_rwkv_k_a, m_rwkv_r_k, m_rwkv_gn_w, m_rwkv_gn_b, m_w_out, m_ffn2_norm, m_ffn2_w_gate, m_ffn2_w_up, m_ffn2_w_down, m_final_norm, v_ffn1_norm, v_ffn1_w_gate, v_ffn1_w_up, v_ffn1_w_down, v_mix_norm, v_w_in, v_hgrn_lb_logits, v_hgrn_out_norm, v_rwkv_shift_mu, v_rwkv_w0, v_rwkv_w2, v_rwkv_a0, v_rwkv_a2, v_rwkv_g2, v_rwkv_k_k, v_rwkv_k_a, v_rwkv_r_k, v_rwkv_gn_w, v_rwkv_gn_b, v_w_out, v_ffn2_norm, v_ffn2_w_gate, v_ffn2_w_up, v_ffn2_w_down, v_final_norm):
    args = dict(locals())
    wts = {n: args[n] for n in WEIGHTS}
    mom = {n: args["m_" + n] for n in WEIGHTS}
    var = {n: args["v_" + n] for n in WEIGHTS}
    shard_shapes = {n: wts[n].shape[1:] for n in SHARDED}

    gathered = _exchange(_pack([wts[n].astype(MXU) for n in SHARDED], 16), False, "gather_weights")
    per_dev = _unpack(gathered, [shard_shapes[n] for n in SHARDED], 16)
    full = {n: _gathered_to_full(g, n, shard_shapes[n]) for n, g in zip(SHARDED, per_dev)}
    for n in SMALL:
        full[n] = wts[n] if n in ("hgrn_lb_logits", "rwkv_r_k", "final_norm") else wts[n].reshape(1, -1)

    loss, grad_x, grads = _local_step(x[0], loss_target[0], full)
    loss = lax.psum(loss, AXES)

    names = SHARDED + SMALL
    big_g = [_full_to_per_device(grads[n], n) for n in SHARDED]
    contrib = jnp.stack([_pack([b[d] for b in big_g] + [grads[n] for n in SMALL], 8, ROW_PAD) for d in range(N_DEV)])
    parts = _exchange(contrib, True, "scatter_grads")

    w_rows, m_rows, v_rows = (_pack([src[n] for n in names], 8, ROW_PAD) for src in (wts, mom, var))
    outs = _adamw(parts, w_rows, m_rows, v_rows, "adamw")
    shapes = [wts[n].shape for n in names]
    g_new, d_new, m_new, v_new = ({n: a for n, a in zip(names, _unpack(o, shapes, 8))} for o in outs)
    return (loss, grad_x[None], *[g_new[n] for n in WEIGHTS], *[d_new[n] for n in WEIGHTS],
            *[m_new[n] for n in WEIGHTS], *[v_new[n] for n in WEIGHTS])
```

```python
import functools
import math

import jax
import jax.numpy as jnp
from jax import lax
from jax.experimental import pallas as pl
from jax.experimental.pallas import tpu as pltpu

F32 = jnp.float32
MXU = jnp.bfloat16
D = 1024
FF = 2816
WA = 512
WB = 512
HD_B = 64
N_IN = 3744
N_INP = 3840
COL_R, COL_K, COL_V = 4, 5, 6
COL_L = 14
LORA = (32, 32, 96)
HG_CHUNK = 64
SCAN_CHUNK = 64
NORM_EPS = 1e-6
GN_EPS = 64e-5
L2_EPS = 1e-12
DECAY_C = math.exp(-0.5)
N_DEV = 8
LANES = 128
ADAM_BLOCK_BYTES = 4 * 1024 * 1024
VMEM_LIMIT = 56 * 1024 * 1024
ADAM_LR, ADAM_B1, ADAM_B2, ADAM_EPS, ADAM_WD, ADAM_STEP = 0.001, 0.9, 0.999, 1e-08, 0.01, 10
AXES = ("x", "y", "c")

SHARDED = ("ffn1_w_gate", "ffn1_w_up", "ffn1_w_down", "w_in", "rwkv_w2", "rwkv_a2", "rwkv_g2", "w_out",
           "ffn2_w_gate", "ffn2_w_up", "ffn2_w_down")
COL_SHARDED = {"ffn1_w_gate", "ffn1_w_up", "w_in", "rwkv_w2", "rwkv_a2", "rwkv_g2", "ffn2_w_gate", "ffn2_w_up"}
SMALL = ("ffn1_norm", "mix_norm", "hgrn_lb_logits", "hgrn_out_norm", "rwkv_shift_mu", "rwkv_w0", "rwkv_a0",
         "rwkv_k_k", "rwkv_k_a", "rwkv_r_k", "rwkv_gn_w", "rwkv_gn_b", "ffn2_norm", "final_norm")
WEIGHTS = ("ffn1_norm", "ffn1_w_gate", "ffn1_w_up", "ffn1_w_down", "mix_norm", "w_in", "hgrn_lb_logits",
           "hgrn_out_norm", "rwkv_shift_mu", "rwkv_w0", "rwkv_w2", "rwkv_a0", "rwkv_a2", "rwkv_g2", "rwkv_k_k",
           "rwkv_k_a", "rwkv_r_k", "rwkv_gn_w", "rwkv_gn_b", "w_out", "ffn2_norm", "ffn2_w_gate", "ffn2_w_up",
           "ffn2_w_down", "final_norm")


def _tile(n, cap):
    if n <= cap:
        return n
    for t in range(cap - cap % LANES, 0, -LANES):
        if n % t == 0:
            return t
    raise ValueError((n, cap))


def _params(n_axes):
    return pltpu.CompilerParams(dimension_semantics=("arbitrary",) * n_axes, vmem_limit_bytes=VMEM_LIMIT)


def _sig(x):
    return jax.nn.sigmoid(x)


def _dsilu(z, s):
    return s * (1.0 + z * (1.0 - s))


def _dot(a, b, dims=((1,), (0,)), precision=None):
    return lax.dot_general(a, b, (dims, ((), ())), preferred_element_type=F32, precision=precision)


_NT = ((1,), (1,))
_TN = ((0,), (0,))
_HI = lax.Precision.HIGHEST


def _iota(shape, dim):
    return lax.broadcasted_iota(jnp.int32, shape, dim)


def _split_dot(x, ones, passes):
    hi = x.astype(jnp.bfloat16)
    acc = _dot(hi, ones)
    rem = x
    for _ in range(passes - 1):
        rem = rem - hi.astype(F32)
        hi = rem.astype(jnp.bfloat16)
        acc = acc + _dot(hi, ones)
    return acc


def _head_ones(n, width):
    shift = width.bit_length() - 1
    return (_iota((n, n), 0) >> shift == _iota((n, n), 1) >> shift).astype(jnp.bfloat16)


def _matmul(a, b, *, ta=False, tb=False, out_dtype=F32, res=None, name):
    m, k = (a.shape[1], a.shape[0]) if ta else a.shape
    n = b.shape[0] if tb else b.shape[1]
    tm, tn, tk = _tile(m, 512), _tile(n, 768), _tile(k, 1024)
    nk = k // tk
    dims = ((0 if ta else 1,), (1 if tb else 0,))

    def body(*refs):
        a_ref, b_ref = refs[:2]
        o_ref, acc = refs[-2:]
        kk = pl.program_id(2)

        @pl.when(kk == 0)
        def _():
            acc[...] = jnp.zeros_like(acc)

        acc[...] += _dot(a_ref[...].astype(MXU), b_ref[...].astype(MXU), dims)

        @pl.when(kk == nk - 1)
        def _():
            v = acc[...]
            if res is not None:
                v = v + refs[2][...]
            o_ref[...] = v.astype(out_dtype)

    a_spec = pl.BlockSpec((tk, tm), lambda i, j, kk: (kk, i)) if ta else pl.BlockSpec((tm, tk), lambda i, j, kk: (i, kk))
    b_spec = pl.BlockSpec((tn, tk), lambda i, j, kk: (j, kk)) if tb else pl.BlockSpec((tk, tn), lambda i, j, kk: (kk, j))
    o_spec = pl.BlockSpec((tm, tn), lambda i, j, kk: (i, j))
    ins, specs = [a, b], [a_spec, b_spec]
    if res is not None:
        ins.append(res)
        specs.append(o_spec)
    return pl.pallas_call(
        body, name=name, grid=(m // tm, n // tn, nk), in_specs=specs, out_specs=o_spec,
        out_shape=jax.ShapeDtypeStruct((m, n), out_dtype), scratch_shapes=[pltpu.VMEM((tm, tn), F32)],
        compiler_params=_params(3))(*ins)


def _rms_fwd(x, g, name):
    t = x.shape[0]
    tb = _tile(t, 512)

    def body(x_ref, g_ref, o_ref):
        xv = x_ref[...]
        rinv = lax.rsqrt(jnp.mean(xv * xv, axis=-1, keepdims=True) + NORM_EPS)
        o_ref[...] = (xv * rinv * g_ref[...]).astype(MXU)

    return pl.pallas_call(
        body, name=name, grid=(t // tb,),
        in_specs=[pl.BlockSpec((tb, D), lambda i: (i, 0)), pl.BlockSpec((1, D), lambda i: (0, 0))],
        out_specs=pl.BlockSpec((tb, D), lambda i: (i, 0)), out_shape=jax.ShapeDtypeStruct((t, D), MXU),
        compiler_params=_params(1))(x, g)


def _rms_bwd(x, g, dh, dres, name):
    t = x.shape[0]
    tb = _tile(t, 512)

    def body(x_ref, g_ref, dh_ref, dres_ref, dx_ref, dg_ref):
        @pl.when(pl.program_id(0) == 0)
        def _():
            dg_ref[...] = jnp.zeros_like(dg_ref)

        xv = x_ref[...]
        rinv = lax.rsqrt(jnp.mean(xv * xv, axis=-1, keepdims=True) + NORM_EPS)
        xhat = xv * rinv
        dhv = dh_ref[...]
        dg_ref[...] += jnp.sum(dhv * xhat, axis=0, keepdims=True)
        dxhat = dhv * g_ref[...]
        dx_ref[...] = dres_ref[...] + rinv * (dxhat - xhat * jnp.mean(dxhat * xhat, axis=-1, keepdims=True))

    row = pl.BlockSpec((tb, D), lambda i: (i, 0))
    vec = pl.BlockSpec((1, D), lambda i: (0, 0))
    return pl.pallas_call(
        body, name=name, grid=(t // tb,), in_specs=[row, vec, row, row], out_specs=[row, vec],
        out_shape=[jax.ShapeDtypeStruct((t, D), F32), jax.ShapeDtypeStruct((1, D), F32)],
        compiler_params=_params(1))(x, g, dh, dres)


def _loss_head(x, g, target, name):
    t = x.shape[0]
    tb = _tile(t, 512)

    def body(x_ref, g_ref, t_ref, loss_ref, dx_ref, dg_ref):
        @pl.when(pl.program_id(0) == 0)
        def _():
            dg_ref[...] = jnp.zeros_like(dg_ref)
            loss_ref[...] = jnp.zeros_like(loss_ref)

        xv = x_ref[...]
        gv = g_ref[...]
        rinv = lax.rsqrt(jnp.mean(xv * xv, axis=-1, keepdims=True) + NORM_EPS)
        xhat = xv * rinv
        err = xhat * gv - t_ref[...]
        per_tok = jnp.mean(err * err, axis=-1, keepdims=True)
        loss_ref[...] += jnp.broadcast_to(0.5 * jnp.sum(per_tok, axis=0, keepdims=True), loss_ref.shape)
        dy = err * (1.0 / D)
        dg_ref[...] += jnp.sum(dy * xhat, axis=0, keepdims=True)
        dxhat = dy * gv
        dx_ref[...] = rinv * (dxhat - xhat * jnp.mean(dxhat * xhat, axis=-1, keepdims=True))

    row = pl.BlockSpec((tb, D), lambda i: (i, 0))
    vec = pl.BlockSpec((1, D), lambda i: (0, 0))
    return pl.pallas_call(
        body, name=name, grid=(t // tb,), in_specs=[row, vec, row],
        out_specs=[pl.BlockSpec((1, LANES), lambda i: (0, 0)), row, vec],
        out_shape=[jax.ShapeDtypeStruct((1, LANES), F32), jax.ShapeDtypeStruct((t, D), F32),
                   jax.ShapeDtypeStruct((1, D), F32)],
        compiler_params=_params(1))(x, g, target)


def _ffn_fwd(x, h, wg, wu, wd, name):
    t = x.shape[0]
    tb, fb = _tile(t, 1024), 256
    nf = FF // fb

    def body(x_ref, h_ref, wg_ref, wu_ref, wd_ref, o_ref, acc):
        f = pl.program_id(1)

        @pl.when(f == 0)
        def _():
            acc[...] = jnp.zeros_like(acc)

        hv = h_ref[...]
        gate = _dot(hv, wg_ref[...])
        up = _dot(hv, wu_ref[...])
        act = (gate * _sig(gate) * up).astype(MXU)
        acc[...] += _dot(act, wd_ref[...])

        @pl.when(f == nf - 1)
        def _():
            o_ref[...] = x_ref[...] + 0.5 * acc[...]

    row = pl.BlockSpec((tb, D), lambda i, f: (i, 0))
    col = pl.BlockSpec((D, fb), lambda i, f: (0, f))
    return pl.pallas_call(
        body, name=name, grid=(t // tb, nf),
        in_specs=[row, row, col, col, pl.BlockSpec((fb, D), lambda i, f: (f, 0))], out_specs=row,
        out_shape=jax.ShapeDtypeStruct((t, D), F32), scratch_shapes=[pltpu.VMEM((tb, D), F32)],
        compiler_params=_params(2))(x, h, wg, wu, wd)


def _ffn_bwd(h, dy, wg, wu, wd, name):
    t = h.shape[0]
    tb, fb = _tile(t, 512), 256
    nf = FF // fb

    def body(h_ref, dy_ref, wg_ref, wu_ref, wd_ref, dh_ref, act_ref, dg_ref, du_ref, dout_ref, acc):
        f = pl.program_id(1)

        @pl.when(f == 0)
        def _():
            acc[...] = jnp.zeros_like(acc)

        hv = h_ref[...]
        dout = (0.5 * dy_ref[...]).astype(MXU)
        dout_ref[...] = dout
        gate = _dot(hv, wg_ref[...])
        up = _dot(hv, wu_ref[...])
        dact = _dot(dout, wd_ref[...], _NT)
        s = _sig(gate)
        silu = gate * s
        act_ref[...] = (silu * up).astype(MXU)
        dup = (dact * silu).astype(MXU)
        dgate = (dact * up * _dsilu(gate, s)).astype(MXU)
        du_ref[...] = dup
        dg_ref[...] = dgate
        acc[...] += _dot(dgate, wg_ref[...], _NT) + _dot(dup, wu_ref[...], _NT)

        @pl.when(f == nf - 1)
        def _():
            dh_ref[...] = acc[...]

    row = pl.BlockSpec((tb, D), lambda i, f: (i, 0))
    col = pl.BlockSpec((D, fb), lambda i, f: (0, f))
    hid = pl.BlockSpec((tb, fb), lambda i, f: (i, f))
    hid_shape = jax.ShapeDtypeStruct((t, FF), MXU)
    return pl.pallas_call(
        body, name=name, grid=(t // tb, nf),
        in_specs=[row, row, col, col, pl.BlockSpec((fb, D), lambda i, f: (f, 0))],
        out_specs=[row, hid, hid, hid, row],
        out_shape=[jax.ShapeDtypeStruct((t, D), F32), hid_shape, hid_shape, hid_shape,
                   jax.ShapeDtypeStruct((t, D), MXU)],
        scratch_shapes=[pltpu.VMEM((tb, D), F32)], compiler_params=_params(2))(h, dy, wg, wu, wd)


def _hgrn_chunk(qa, fa, lbl_ref):
    c = HG_CHUNK
    lb = _sig(lbl_ref[0:1, :] - lbl_ref[1:2, :])
    sf = _sig(fa)
    forget = lb + (1.0 - lb) * sf
    kh = 1.0 - forget
    row, col = _iota((c, c), 0), _iota((c, c), 1)
    b = _dot((col <= row).astype(F32), jnp.log(forget), precision=_HI)
    bref, blast = b[c // 2:c // 2 + 1, :], b[c - 1:c, :]
    sq = _sig(qa)
    q = qa * sq
    qt, kt = q * jnp.exp(b - bref), kh * jnp.exp(bref - b)
    qb, kl = q * jnp.exp(b), kh * jnp.exp(blast - b)
    causal = col <= row
    return dict(lb=lb, sf=sf, forget=forget, sq=sq, qt=qt, kt=kt, qb=qb, kl=kl, decay=jnp.exp(blast),
                causal=causal, e_q=jnp.exp(b), e_qt=jnp.exp(b - bref), e_kt=jnp.exp(bref - b),
                e_kl=jnp.exp(blast - b))


def _hgrn_specs(t):
    c = HG_CHUNK
    return c, t // c, WA // LANES


def _hgrn_fwd(p_all, lbl, onorm, name):
    t = p_all.shape[0]
    c, n, nh = _hgrn_specs(t)

    def body(q_ref, f_ref, i_ref, g_ref, lbl_ref, on_ref, oa_ref, oraw_ref, st_ref, state):
        @pl.when(pl.program_id(1) == 0)
        def _():
            state[...] = jnp.zeros_like(state)

        k = _hgrn_chunk(q_ref[...], f_ref[...], lbl_ref)
        v = i_ref[...]
        st = state[...]
        st_ref[0, 0] = st
        a = jnp.where(k["causal"], _dot(k["qt"], k["kt"], _NT, _HI), 0.0)
        o = _dot(a, v, precision=_HI) + _dot(k["qb"], st, _NT, _HI)
        state[...] = st * k["decay"] + _dot(v, k["kl"], _TN, _HI)
        oraw_ref[...] = o
        rinv = lax.rsqrt(jnp.mean(o * o, axis=-1, keepdims=True) + NORM_EPS)
        ga = g_ref[...]
        oa_ref[...] = (o * rinv * on_ref[...] * (ga * _sig(ga))).astype(MXU)

    def blk(j):
        return pl.BlockSpec((c, LANES), lambda h, i: (i, j * nh + h))

    head = pl.BlockSpec((c, LANES), lambda h, i: (i, h))
    return pl.pallas_call(
        body, name=name, grid=(nh, n),
        in_specs=[blk(0), blk(1), blk(2), blk(3), pl.BlockSpec((2, LANES), lambda h, i: (0, h)),
                  pl.BlockSpec((1, LANES), lambda h, i: (0, h))],
        out_specs=[head, head, pl.BlockSpec((1, 1, LANES, LANES), lambda h, i: (h, i, 0, 0))],
        out_shape=[jax.ShapeDtypeStruct((t, WA), MXU), jax.ShapeDtypeStruct((t, WA), F32),
                   jax.ShapeDtypeStruct((nh, n, LANES, LANES), F32)],
        scratch_shapes=[pltpu.VMEM((LANES, LANES), F32)], compiler_params=_params(2))(
            p_all, p_all, p_all, p_all, lbl, onorm)


def _hgrn_bwd(p_all, lbl, onorm, oraw, states, doa, name):
    t = p_all.shape[0]
    c, n, nh = _hgrn_specs(t)

    def body(q_ref, f_ref, i_ref, g_ref, lbl_ref, on_ref, oraw_ref, st_ref, doa_ref,
             dq_ref, df_ref, di_ref, dg_ref, don_ref, dlbl_ref, dstate, dlb):
        @pl.when(pl.program_id(1) == 0)
        def _():
            dstate[...] = jnp.zeros_like(dstate)
            dlb[...] = jnp.zeros_like(dlb)
            don_ref[...] = jnp.zeros_like(don_ref)

        qa, fa, v, ga = q_ref[...], f_ref[...], i_ref[...], g_ref[...]
        k = _hgrn_chunk(qa, fa, lbl_ref)
        st, dst_next = st_ref[0, 0], dstate[...]
        o = oraw_ref[...]
        rinv = lax.rsqrt(jnp.mean(o * o, axis=-1, keepdims=True) + NORM_EPS)
        on = o * rinv
        sg = _sig(ga)
        gate = ga * sg
        dout = doa_ref[...]
        don_ref[...] += jnp.sum(dout * on * gate, axis=0, keepdims=True)
        dg_ref[...] = (dout * on * on_ref[...] * _dsilu(ga, sg)).astype(MXU)
        d_on = dout * on_ref[...] * gate
        do = rinv * (d_on - on * jnp.mean(d_on * on, axis=-1, keepdims=True))

        a = jnp.where(k["causal"], _dot(k["qt"], k["kt"], _NT, _HI), 0.0)
        dqb = _dot(do, st, precision=_HI)
        dstate[...] = dst_next * k["decay"] + _dot(do, k["qb"], _TN, _HI)
        da = jnp.where(k["causal"], _dot(do, v, _NT, _HI), 0.0)
        dqt = _dot(da, k["kt"], precision=_HI)
        dkt = _dot(da, k["qt"], _TN, _HI)
        dv = _dot(a, do, _TN, _HI) + _dot(k["kl"], dst_next, _NT, _HI)
        dkl = _dot(v, dst_next, precision=_HI)
        ddecay = jnp.sum(dst_next * st, axis=0, keepdims=True)
        dq = dqb * k["e_q"] + dqt * k["e_qt"]
        dk = dkt * k["e_kt"] + dkl * k["e_kl"]
        tq, tk, tl = dqt * k["qt"], dkt * k["kt"], dkl * k["kl"]
        db = dqb * k["qb"] + tq - tk - tl
        dbref = jnp.sum(tk - tq, axis=0, keepdims=True)
        dblast = jnp.sum(tl, axis=0, keepdims=True) + ddecay * k["decay"]
        rows = _iota((c, LANES), 0)
        db = db + jnp.where(rows == c // 2, dbref, 0.0) + jnp.where(rows == c - 1, dblast, 0.0)
        row, col = _iota((c, c), 0), _iota((c, c), 1)
        dlogf = _dot((col >= row).astype(F32), db, precision=_HI)
        dq_ref[...] = (dq * _dsilu(qa, k["sq"])).astype(MXU)
        di_ref[...] = dv.astype(MXU)
        dforget = dlogf / k["forget"] - dk
        sf, lb = k["sf"], k["lb"]
        df_ref[...] = (dforget * (1.0 - lb) * sf * (1.0 - sf)).astype(MXU)
        dlb[...] += jnp.sum(dforget * (1.0 - sf), axis=0, keepdims=True)
        dl0 = dlb[...] * lb * (1.0 - lb)
        dlbl_ref[...] = jnp.where(_iota((2, LANES), 0) == 0, dl0, -dl0)

    def blk(j):
        return pl.BlockSpec((c, LANES), lambda h, i: (n - 1 - i, j * nh + h))

    head = pl.BlockSpec((c, LANES), lambda h, i: (n - 1 - i, h))
    vec = pl.BlockSpec((1, LANES), lambda h, i: (0, h))
    lg = pl.BlockSpec((2, LANES), lambda h, i: (0, h))
    grad = jax.ShapeDtypeStruct((t, WA), MXU)
    return pl.pallas_call(
        body, name=name, grid=(nh, n),
        in_specs=[blk(0), blk(1), blk(2), blk(3), lg, vec, head,
                  pl.BlockSpec((1, 1, LANES, LANES), lambda h, i: (h, n - 1 - i, 0, 0)), head],
        out_specs=[head, head, head, head, vec, lg],
        out_shape=[grad, grad, grad, grad, jax.ShapeDtypeStruct((1, WA), F32), jax.ShapeDtypeStruct((2, WA), F32)],
        scratch_shapes=[pltpu.VMEM((LANES, LANES), F32), pltpu.VMEM((1, LANES), F32)],
        compiler_params=_params(2))(p_all, p_all, p_all, p_all, lbl, onorm, oraw, states, doa)


def _lora_act(x):
    lane = _iota(x.shape, 1)
    n_w, n_a, n_g = LORA
    return jnp.where(lane < n_w, jnp.tanh(x),
                     jnp.where(lane < n_w + n_a, x, jnp.where(lane < n_w + n_a + n_g, _sig(x), 0.0)))


def _lora_dact(x):
    lane = _iota(x.shape, 1)
    n_w, n_a, n_g = LORA
    th, s = jnp.tanh(x), _sig(x)
    return jnp.where(lane < n_w, 1.0 - th * th,
                     jnp.where(lane < n_w + n_a, 1.0, jnp.where(lane < n_w + n_a + n_g, s * (1.0 - s), 0.0)))


def _shift_down(cur, prev8, first):
    rolled = pltpu.roll(cur, 1, 0)
    edge = prev8[7:8, :] * jnp.where(first, 0.0, 1.0)
    return jnp.where(_iota(cur.shape, 0) == 0, edge, rolled)


def _shift_up(cur, next8, last):
    rows = cur.shape[0]
    rolled = pltpu.roll(cur, rows - 1, 0)
    edge = next8[0:1, :] * jnp.where(last, 0.0, 1.0)
    return jnp.where(_iota(cur.shape, 0) == rows - 1, edge, rolled)


def _rwkv_inputs(refs, first, ones):
    (pr, pk, pv, plo, qr, qk, qv, qlo, mr, mk, mv, mlo, w2c, w0, a0, kk_w, ka_w) = refs
    mix = lambda cur, prev, mu: cur[...] + mu[...] * (_shift_down(cur[...], prev[...], first) - cur[...])
    r, k, v, lo = mix(pr, qr, mr), mix(pk, qk, mk), mix(pv, qv, mv), mix(plo, qlo, mlo)
    z = _lora_act(lo)
    lin = _dot(z.astype(MXU), w2c[...])
    sg = _sig(w0[...] + lin[:, :WB])
    decay = jnp.exp(-DECAY_C * sg)
    a = _sig(a0[...] + lin[:, WB:2 * WB])
    g = lin[:, 2 * WB:]
    kk0 = k * kk_w[...]
    nrm = jnp.sqrt(_split_dot(kk0 * kk0, ones, 3))
    den = jnp.maximum(nrm, L2_EPS)
    kk = kk0 / den
    k2 = k * (1.0 + (a - 1.0) * ka_w[...])
    return dict(r=r, k=k, v=v, lo=lo, z=z, sg=sg, decay=decay, a=a, g=g, kk=kk, den=den, nrm=nrm, k2=k2)


def _rwkv_in_specs(t, tb):
    nt8 = tb // 8

    def cur(w, j):
        return pl.BlockSpec((tb, w), lambda i: (i, j))

    def prev(w, j):
        return pl.BlockSpec((8, w), lambda i: (jnp.maximum(i * nt8 - 1, 0), j))

    def vec(w, j=0):
        return pl.BlockSpec((1, w), lambda i: (0, j))

    return [cur(WB, COL_R), cur(WB, COL_K), cur(WB, COL_V), cur(256, COL_L),
            prev(WB, COL_R), prev(WB, COL_K), prev(WB, COL_V), prev(256, COL_L),
            vec(WB, 0), vec(WB, 1), vec(WB, 2), vec(256, 6),
            pl.BlockSpec((256, 3 * WB), lambda i: (0, 0)), vec(WB), vec(WB), vec(WB), vec(WB)]


def _rwkv_in_args(p_all, mu_pad, w2cat, w0, a0, k_k, k_a):
    return (p_all,) * 8 + (mu_pad,) * 4 + (w2cat, w0, a0, k_k, k_a)


def _rwkv_prep(p_all, mu_pad, w2cat, w0, a0, k_k, k_a, name):
    t = p_all.shape[0]
    tb = _tile(t, 256)

    def body(*refs):
        ins, outs = refs[:17], refs[17:]
        q = _rwkv_inputs(ins, pl.program_id(0) == 0, _head_ones(WB, HD_B))
        for ref, val in zip(outs, (q["r"], q["decay"], q["k2"], q["v"], -q["kk"], q["kk"] * q["a"], q["g"])):
            ref[...] = val

    out = pl.BlockSpec((tb, WB), lambda i: (i, 0))
    return pl.pallas_call(
        body, name=name, grid=(t // tb,), in_specs=_rwkv_in_specs(t, tb), out_specs=[out] * 7,
        out_shape=[jax.ShapeDtypeStruct((t, WB), F32)] * 7, compiler_params=_params(1))(
            *_rwkv_in_args(p_all, mu_pad, w2cat, w0, a0, k_k, k_a))


def _pair_rows(x8, i):
    return jnp.concatenate([jnp.broadcast_to(x8[i:i + 1, p * LANES:(p + 1) * LANES], (HD_B, LANES))
                            for p in range(4)], axis=0)


def _pair_sums(x):
    return jnp.concatenate([jnp.sum(x[p * HD_B:(p + 1) * HD_B], axis=0, keepdims=True) for p in range(4)], axis=1)


def _put_row(buf, i, row):
    return jnp.where(_iota(buf.shape, 0) == i, row, buf)


def _scan_consts():
    ones = _head_ones(LANES, HD_B)
    diag = (_iota((4 * HD_B, LANES), 0) & (HD_B - 1) == _iota((4 * HD_B, LANES), 1) & (HD_B - 1)).astype(F32)
    return ones, diag


def _rwkv_scan_fwd(r, w, k, v, a, b, name):
    t = r.shape[0]
    cc = min(t, SCAN_CHUNK)

    def body(r_ref, w_ref, k_ref, v_ref, a_ref, b_ref, y_ref, sall_ref, state):
        @pl.when(pl.program_id(0) == 0)
        def _():
            state[...] = jnp.zeros_like(state)

        ones, diag = _scan_consts()

        def block(j, s):
            base = pl.multiple_of(j * 8, 8)
            rows = pl.ds(base, 8)
            r8, w8, k8, v8, a8, b8 = (ref[rows, :] for ref in (r_ref, w_ref, k_ref, v_ref, a_ref, b_ref))
            y8 = jnp.zeros((8, WB), F32)
            for i in range(8):
                sa = _split_dot(s * _pair_rows(a8, i), ones, 3)
                vb = _split_dot(_pair_rows(v8, i) * diag, ones, 2)
                s = s * _pair_rows(w8, i) + sa * _pair_rows(b8, i) + vb * _pair_rows(k8, i)
                sall_ref[base + i] = s
                yb = _split_dot(s * _pair_rows(r8, i), ones, 2)
                y8 = _put_row(y8, i, _pair_sums(yb * diag))
            y_ref[rows, :] = y8
            return s

        state[...] = lax.fori_loop(0, cc // 8, block, state[...])

    row = pl.BlockSpec((cc, WB), lambda i: (i, 0))
    return pl.pallas_call(
        body, name=name, grid=(t // cc,), in_specs=[row] * 6,
        out_specs=[row, pl.BlockSpec((cc, 4 * HD_B, LANES), lambda i: (i, 0, 0))],
        out_shape=[jax.ShapeDtypeStruct((t, WB), F32), jax.ShapeDtypeStruct((t, 4 * HD_B, LANES), F32)],
        scratch_shapes=[pltpu.VMEM((4 * HD_B, LANES), F32)], compiler_params=_params(1))(r, w, k, v, a, b)


def _rwkv_scan_bwd(dy, r, w, k, v, a, b, sall, name):
    t = r.shape[0]
    cc = min(t, SCAN_CHUNK)
    n = t // cc

    def body(dy_ref, r_ref, w_ref, k_ref, v_ref, a_ref, b_ref, sall_ref, sprev_ref,
             dr_ref, dw_ref, dk_ref, dv_ref, da_ref, db_ref, dstate):
        @pl.when(pl.program_id(0) == 0)
        def _():
            dstate[...] = jnp.zeros_like(dstate)

        ones, diag = _scan_consts()
        before_chunk = jnp.where(pl.program_id(0) == n - 1, 0.0, 1.0) * sprev_ref[0]

        def block(jj, carry):
            ds, sc = carry
            j = cc // 8 - 1 - jj
            base = pl.multiple_of(j * 8, 8)
            rows = pl.ds(base, 8)
            dy8, r8, w8, k8, v8, a8, b8 = (ref[rows, :] for ref in
                                           (dy_ref, r_ref, w_ref, k_ref, v_ref, a_ref, b_ref))
            outs = [jnp.zeros((8, WB), F32) for _ in range(6)]
            for i in range(7, -1, -1):
                if i > 0:
                    sp = sall_ref[base + i - 1]
                else:
                    sp = jnp.where(j == 0, before_chunk, sall_ref[jnp.maximum(base - 1, 0)])
                a_b, b_b, k_b, w_b = _pair_rows(a8, i), _pair_rows(b8, i), _pair_rows(k8, i), _pair_rows(w8, i)
                dyb = _split_dot(_pair_rows(dy8, i) * diag, ones, 2)
                vb = _split_dot(_pair_rows(v8, i) * diag, ones, 2)
                ds = ds + dyb * _pair_rows(r8, i)
                sa = _split_dot(sp * a_b, ones, 3)
                dsa = _split_dot(ds * b_b, ones, 3)
                dvb = _split_dot(ds * k_b, ones, 2)
                vals = (_pair_sums(sc * dyb), _pair_sums(ds * sp), _pair_sums(ds * vb), _pair_sums(dvb * diag),
                        _pair_sums(sp * dsa), _pair_sums(ds * sa))
                outs = [_put_row(o, i, val) for o, val in zip(outs, vals)]
                ds = ds * w_b + dsa * a_b
                sc = sp
            for ref, o in zip((dr_ref, dw_ref, dk_ref, dv_ref, da_ref, db_ref), outs):
                ref[rows, :] = o
            return ds, sc

        ds, _ = lax.fori_loop(0, cc // 8, block, (dstate[...], sall_ref[cc - 1]))
        dstate[...] = ds

    row = pl.BlockSpec((cc, WB), lambda i: (n - 1 - i, 0))
    return pl.pallas_call(
        body, name=name, grid=(n,),
        in_specs=[row] * 7 + [pl.BlockSpec((cc, 4 * HD_B, LANES), lambda i: (n - 1 - i, 0, 0)),
                              pl.BlockSpec((1, 4 * HD_B, LANES), lambda i: (jnp.maximum((n - 1 - i) * cc - 1, 0), 0, 0))],
        out_specs=[row] * 6, out_shape=[jax.ShapeDtypeStruct((t, WB), F32)] * 6,
        scratch_shapes=[pltpu.VMEM((4 * HD_B, LANES), F32)], compiler_params=_params(1))(
            dy, r, w, k, v, a, b, sall, sall)


def _rwkv_post(y, r, k2, v, g, r_k, gn_w, gn_b, name):
    t = y.shape[0]
    tb = _tile(t, 256)

    def body(y_ref, r_ref, k_ref, v_ref, g_ref, rk_ref, gw_ref, gb_ref, o_ref):
        ones = _head_ones(WB, HD_B)
        yv = y_ref[...]
        yc = yv - _split_dot(yv, ones, 3) * (1.0 / HD_B)
        rstd = lax.rsqrt(_split_dot(yc * yc, ones, 3) * (1.0 / HD_B) + GN_EPS)
        rk = _split_dot(r_ref[...] * k_ref[...] * rk_ref[...], ones, 3)
        o_ref[...] = ((yc * rstd * gw_ref[...] + gb_ref[...] + rk * v_ref[...]) * g_ref[...]).astype(MXU)

    row = pl.BlockSpec((tb, WB), lambda i: (i, 0))
    vec = pl.BlockSpec((1, WB), lambda i: (0, 0))
    return pl.pallas_call(
        body, name=name, grid=(t // tb,), in_specs=[row] * 5 + [vec] * 3, out_specs=row,
        out_shape=jax.ShapeDtypeStruct((t, WB), MXU), compiler_params=_params(1))(y, r, k2, v, g, r_k, gn_w, gn_b)


def _rwkv_post_bwd(dob, y, r, k2, v, g, r_k, gn_w, gn_b, name):
    t = y.shape[0]
    tb = _tile(t, 256)

    def body(do_ref, y_ref, r_ref, k_ref, v_ref, g_ref, rk_ref, gw_ref, gb_ref,
             dy_ref, dg_ref, dr_ref, dk_ref, dv_ref, dgw_ref, dgb_ref, drk_ref):
        @pl.when(pl.program_id(0) == 0)
        def _():
            dgw_ref[...] = jnp.zeros_like(dgw_ref)
            dgb_ref[...] = jnp.zeros_like(dgb_ref)
            drk_ref[...] = jnp.zeros_like(drk_ref)

        ones = _head_ones(WB, HD_B)
        seg = lambda x: _split_dot(x, ones, 3)
        yv, rv, kv, vv, gv = y_ref[...], r_ref[...], k_ref[...], v_ref[...], g_ref[...]
        yc = yv - seg(yv) * (1.0 / HD_B)
        rstd = lax.rsqrt(seg(yc * yc) * (1.0 / HD_B) + GN_EPS)
        yn = yc * rstd
        rk = seg(rv * kv * rk_ref[...])
        dob_v = do_ref[...]
        dg_ref[...] = dob_v * (yn * gw_ref[...] + gb_ref[...] + rk * vv)
        dyg = dob_v * gv
        dgw_ref[...] += jnp.sum(dyg * yn, axis=0, keepdims=True)
        dgb_ref[...] += jnp.sum(dyg, axis=0, keepdims=True)
        dyn = dyg * gw_ref[...]
        dy_ref[...] = rstd * (dyn - (seg(dyn) + yn * seg(dyn * yn)) * (1.0 / HD_B))
        drk = seg(dyg * vv)
        dv_ref[...] = dyg * rk
        dr_ref[...] = drk * kv * rk_ref[...]
        dk_ref[...] = drk * rv * rk_ref[...]
        drk_ref[...] += jnp.sum(drk * rv * kv, axis=0, keepdims=True)

    row = pl.BlockSpec((tb, WB), lambda i: (i, 0))
    vec = pl.BlockSpec((1, WB), lambda i: (0, 0))
    full, small = jax.ShapeDtypeStruct((t, WB), F32), jax.ShapeDtypeStruct((1, WB), F32)
    return pl.pallas_call(
        body, name=name, grid=(t // tb,), in_specs=[row] * 6 + [vec] * 3, out_specs=[row] * 5 + [vec] * 3,
        out_shape=[full] * 5 + [small] * 3, compiler_params=_params(1))(dob, y, r, k2, v, g, r_k, gn_w, gn_b)


def _rwkv_prep_bwd(grads, p_all, mu_pad, w2cat, w0, a0, k_k, k_a, name):
    t = p_all.shape[0]
    tb = _tile(t, 256)

    def body(*refs):
        g_refs, ins, outs = refs[:10], refs[10:27], refs[27:]
        dr_s, dw, dk2_s, dv_s, das, dbs, dg, dr_b, dk2_b, dv_b = (ref[...] for ref in g_refs)
        dr_ref, dk_ref, dv_ref, dlo_ref, dw2_ref, dw0_ref, da0_ref, dkk_ref, dka_ref = outs

        @pl.when(pl.program_id(0) == 0)
        def _():
            for ref in (dw2_ref, dw0_ref, da0_ref, dkk_ref, dka_ref):
                ref[...] = jnp.zeros_like(ref)

        ones = _head_ones(WB, HD_B)
        q = _rwkv_inputs(ins, pl.program_id(0) == 0, ones)
        kk_w, ka_w = ins[15][...], ins[16][...]
        a, kk, k = q["a"], q["kk"], q["k"]
        dk2 = dk2_s + dk2_b
        dkk = dbs * a - das
        da = dbs * kk + dk2 * k * ka_w
        dk = dk2 * (1.0 + (a - 1.0) * ka_w)
        dka_ref[...] += jnp.sum(dk2 * k * (a - 1.0), axis=0, keepdims=True)
        proj = jnp.where(q["nrm"] > L2_EPS, _split_dot(dkk * kk, ones, 3), 0.0)
        dkk0 = (dkk - kk * proj) / q["den"]
        dk = dk + dkk0 * kk_w
        dkk_ref[...] += jnp.sum(dkk0 * k, axis=0, keepdims=True)
        dal = da * a * (1.0 - a)
        da0_ref[...] += jnp.sum(dal, axis=0, keepdims=True)
        sg = q["sg"]
        dwl = dw * q["decay"] * (-DECAY_C) * sg * (1.0 - sg)
        dw0_ref[...] += jnp.sum(dwl, axis=0, keepdims=True)
        dlin = jnp.concatenate([dwl, dal, dg], axis=1).astype(MXU)
        dw2_ref[...] += _dot(q["z"].astype(MXU), dlin, _TN)
        dz = _dot(dlin, ins[12][...], _NT)
        dlo_ref[...] = dz * _lora_dact(q["lo"])
        dr_ref[...] = dr_s + dr_b
        dk_ref[...] = dk
        dv_ref[...] = dv_s + dv_b

    row = pl.BlockSpec((tb, WB), lambda i: (i, 0))
    vec = pl.BlockSpec((1, WB), lambda i: (0, 0))
    full, small = jax.ShapeDtypeStruct((t, WB), F32), jax.ShapeDtypeStruct((1, WB), F32)
    return pl.pallas_call(
        body, name=name, grid=(t // tb,), in_specs=[row] * 10 + _rwkv_in_specs(t, tb),
        out_specs=[row] * 3 + [pl.BlockSpec((tb, 256), lambda i: (i, 0)),
                               pl.BlockSpec((256, 3 * WB), lambda i: (0, 0))] + [vec] * 4,
        out_shape=[full] * 3 + [jax.ShapeDtypeStruct((t, 256), F32), jax.ShapeDtypeStruct((256, 3 * WB), F32)]
        + [small] * 4, compiler_params=_params(1))(*grads, *_rwkv_in_args(p_all, mu_pad, w2cat, w0, a0, k_k, k_a))


def _shift_bwd(dshifted, p_all, mu_pad, name):
    t = p_all.shape[0]
    tb = _tile(t, 256)
    nt, nt8 = t // tb, tb // 8
    widths, cols, mus = (WB, WB, WB, 256), (COL_R, COL_K, COL_V, COL_L), (0, 1, 2, 6)

    def body(*refs):
        d_refs, n_refs, p_refs, q_refs, m_refs = refs[0:4], refs[4:8], refs[8:12], refs[12:16], refs[16:20]
        o_refs, dmu_refs = refs[20:24], refs[24:28]
        i = pl.program_id(0)

        @pl.when(i == 0)
        def _():
            for ref in dmu_refs:
                ref[...] = jnp.zeros_like(ref)

        for d, nx, p, q, m, o, dmu in zip(d_refs, n_refs, p_refs, q_refs, m_refs, o_refs, dmu_refs):
            dv, pv, mu = d[...], p[...], m[...]
            o[...] = (dv * (1.0 - mu) + mu * _shift_up(dv, nx[...], i == nt - 1)).astype(MXU)
            dmu[...] += jnp.sum(dv * (_shift_down(pv, q[...], i == 0) - pv), axis=0, keepdims=True)

    cur_d = [pl.BlockSpec((tb, w), lambda i: (i, 0)) for w in widths]
    next_d = [pl.BlockSpec((8, w), lambda i: (jnp.minimum((i + 1) * nt8, t // 8 - 1), 0)) for w in widths]
    cur_p = [pl.BlockSpec((tb, w), lambda i, j=j: (i, j)) for w, j in zip(widths, cols)]
    prev_p = [pl.BlockSpec((8, w), lambda i, j=j: (jnp.maximum(i * nt8 - 1, 0), j)) for w, j in zip(widths, cols)]
    mu_s = [pl.BlockSpec((1, w), lambda i, j=j: (0, j)) for w, j in zip(widths, mus)]
    vecs = [pl.BlockSpec((1, w), lambda i: (0, 0)) for w in widths]
    return pl.pallas_call(
        body, name=name, grid=(nt,), in_specs=cur_d + next_d + cur_p + prev_p + mu_s, out_specs=cur_d + vecs,
        out_shape=[jax.ShapeDtypeStruct((t, w), MXU) for w in widths]
        + [jax.ShapeDtypeStruct((1, w), F32) for w in widths],
        compiler_params=_params(1))(*dshifted, *dshifted, *(p_all,) * 8, *(mu_pad,) * 4)


def _peer(k):
    x, y, c = (lax.axis_index(n) for n in AXES)
    px = 1 - x if k & 4 else x
    py = 1 - y if k & 2 else y
    pc = 1 - c if k & 1 else c
    return (px, py, pc), 4 * px + 2 * py + pc


def _exchange(srcs, per_peer, name):
    n = len(srcs)
    shapes = [tuple(s.shape[1:]) if pp else tuple(s.shape) for s, pp in zip(srcs, per_peer)]
    pairs = [(j, k) for k in range(1, N_DEV) for j in range(n)]

    def body(*refs):
        src_refs, out_refs = refs[:n], refs[n:2 * n]
        send_sems, recv_sems, local_sems = refs[2 * n:]
        _, me = _peer(0)

        def copy(j, k, arriving):
            peer, idx = _peer(k)
            sem = j * (N_DEV - 1) + k - 1
            return pltpu.make_async_remote_copy(
                src_ref=src_refs[j].at[idx] if per_peer[j] else src_refs[j],
                dst_ref=out_refs[j].at[idx if arriving else me],
                send_sem=send_sems.at[sem], recv_sem=recv_sems.at[sem],
                device_id=peer, device_id_type=pl.DeviceIdType.MESH)

        local = [pltpu.make_async_copy(src_refs[j].at[me] if per_peer[j] else src_refs[j], out_refs[j].at[me],
                                       local_sems.at[j]) for j in range(n)]
        for cp in local:
            cp.start()
        for j, k in pairs:
            copy(j, k, False).start()
        for j, k in pairs:
            copy(j, k, True).wait_recv()
        for j, k in pairs:
            copy(j, k, False).wait_send()
        for cp in local:
            cp.wait()

    hbm = pl.BlockSpec(memory_space=pltpu.HBM)
    return pl.pallas_call(
        body, name=name, in_specs=[hbm] * n, out_specs=[hbm] * n,
        out_shape=[jax.ShapeDtypeStruct((N_DEV,) + shp, s.dtype) for shp, s in zip(shapes, srcs)],
        scratch_shapes=[pltpu.SemaphoreType.DMA((n * (N_DEV - 1),)), pltpu.SemaphoreType.DMA((n * (N_DEV - 1),)),
                        pltpu.SemaphoreType.DMA((n,))])(*srcs)


def _adam_update(g, w, m, v):
    c1, c2 = 1.0 - ADAM_B1 ** ADAM_STEP, 1.0 - ADAM_B2 ** ADAM_STEP
    nm = ADAM_B1 * m + (1.0 - ADAM_B1) * g
    nv = ADAM_B2 * v + (1.0 - ADAM_B2) * (g * g)
    return -ADAM_LR * ((nm / c1) / (jnp.sqrt(nv / c2) + ADAM_EPS) + ADAM_WD * w), nm, nv


def _row_tile(rows, cols):
    padded = -(-cols // LANES) * LANES
    cap = max(8, ADAM_BLOCK_BYTES // (N_DEV * padded * 4))
    best = 8
    for t in range(8, min(rows, cap) + 1, 8):
        if rows % t == 0:
            best = t
    return best


def _adamw(parts, w, m, v, name):
    _, rows, cols = w.shape
    tb = _row_tile(rows, cols)

    def body(p_ref, w_ref, m_ref, v_ref, g_ref, d_ref, nm_ref, nv_ref):
        g = p_ref[0]
        for d in range(1, N_DEV):
            g = g + p_ref[d]
        g_ref[0] = g
        d_ref[0], nm_ref[0], nv_ref[0] = _adam_update(g, w_ref[0], m_ref[0], v_ref[0])

    row = pl.BlockSpec((1, tb, cols), lambda i: (0, i, 0))
    out = jax.ShapeDtypeStruct(w.shape, F32)
    return pl.pallas_call(
        body, name=name, grid=(rows // tb,),
        in_specs=[pl.BlockSpec((N_DEV, tb, cols), lambda i: (0, i, 0)), row, row, row], out_specs=[row] * 4,
        out_shape=[out] * 4, compiler_params=_params(1))(parts, w, m, v)


def _adamw_small(parts, ws, ms, vs, name):
    n = len(ws)

    def body(*refs):
        p_ref = refs[0]
        w_refs, m_refs, v_refs = refs[1:1 + n], refs[1 + n:1 + 2 * n], refs[1 + 2 * n:1 + 3 * n]
        outs = refs[1 + 3 * n:]
        base = 0
        for j in range(n):
            rows, cols = ws[j].shape
            size = rows * cols
            for ch in range(-(-size // LANES)):
                r, c0 = divmod(ch * LANES, cols)
                width = min(LANES, cols - c0)
                g = p_ref[0, base + ch:base + ch + 1, 0:width]
                for d in range(1, N_DEV):
                    g = g + p_ref[d, base + ch:base + ch + 1, 0:width]
                at = (slice(r, r + 1), slice(c0, c0 + width))
                delta, nm, nv = _adam_update(g, w_refs[j][at], m_refs[j][at], v_refs[j][at])
                for out, val in zip((outs[j], outs[n + j], outs[2 * n + j], outs[3 * n + j]), (g, delta, nm, nv)):
                    out[at] = val
            base += -(-size // (8 * LANES)) * 8

    vmem = pl.BlockSpec(memory_space=pltpu.VMEM)
    res = pl.pallas_call(
        body, name=name, in_specs=[vmem] * (1 + 3 * n), out_specs=[vmem] * (4 * n),
        out_shape=[jax.ShapeDtypeStruct(a.shape, F32) for a in ws] * 4)(parts, *ws, *ms, *vs)
    return res[:n], res[n:2 * n], res[2 * n:3 * n], res[3 * n:]


def _rows(a, multiple):
    flat = a.reshape(-1)
    pad = -flat.shape[0] % (multiple * LANES)
    if pad:
        flat = jnp.concatenate([flat, jnp.zeros((pad,), a.dtype)])
    return flat.reshape(-1, LANES)


def _pack(arrs, multiple):
    return jnp.concatenate([_rows(a, multiple) for a in arrs], axis=0)


def _gathered_to_full(g, name, shard_shape):
    g = g.reshape((N_DEV,) + shard_shape)
    if name in COL_SHARDED:
        return jnp.transpose(g, (1, 0, 2)).reshape(shard_shape[0], N_DEV * shard_shape[1])
    return g.reshape(N_DEV * shard_shape[0], shard_shape[1])


def _full_to_per_device(full, name):
    if name in COL_SHARDED:
        r, c = full.shape
        return jnp.transpose(full.reshape(r, N_DEV, c // N_DEV), (1, 0, 2))
    return full.reshape(N_DEV, full.shape[0] // N_DEV, full.shape[1])


def _w2cat(w2, a2, g2):
    n_w, n_a, n_g = LORA
    out = jnp.zeros((256, 3 * WB), w2.dtype)
    out = out.at[0:n_w, 0:WB].set(w2)
    out = out.at[n_w:n_w + n_a, WB:2 * WB].set(a2)
    return out.at[n_w + n_a:n_w + n_a + n_g, 2 * WB:].set(g2)


def _local_step(x, target, w):
    n_w, n_a, n_g = LORA
    w_in_pad = jnp.pad(w["w_in"], ((0, 0), (0, N_INP - N_IN)))
    mu_pad = jnp.pad(w["rwkv_shift_mu"], ((0, 0), (0, 1792 - 1696)))
    w2cat = _w2cat(w["rwkv_w2"], w["rwkv_a2"], w["rwkv_g2"])
    r_k = w["rwkv_r_k"].reshape(1, WB)
    rw = (mu_pad, w2cat, w["rwkv_w0"], w["rwkv_a0"], w["rwkv_k_k"], w["rwkv_k_a"])

    h1 = _rms_fwd(x, w["ffn1_norm"], "ffn1_norm")
    x1 = _ffn_fwd(x, h1, w["ffn1_w_gate"], w["ffn1_w_up"], w["ffn1_w_down"], "ffn1_fwd")
    h2 = _rms_fwd(x1, w["mix_norm"], "mix_norm")
    p_all = _matmul(h2, w_in_pad, name="in_proj")
    oa, oraw, states = _hgrn_fwd(p_all, w["hgrn_lb_logits"], w["hgrn_out_norm"], "hgrn_fwd")
    r, decay, k2, v, sa, sb, g = _rwkv_prep(p_all, *rw, "rwkv_prep")
    y, sall = _rwkv_scan_fwd(r, decay, k2, v, sa, sb, "rwkv_scan_fwd")
    post_w = (r_k, w["rwkv_gn_w"], w["rwkv_gn_b"])
    ob = _rwkv_post(y, r, k2, v, g, *post_w, "rwkv_post")
    o = jnp.concatenate([oa, ob], axis=1)
    x2 = _matmul(o, w["w_out"], res=x1, name="out_proj")
    h3 = _rms_fwd(x2, w["ffn2_norm"], "ffn2_norm")
    x3 = _ffn_fwd(x2, h3, w["ffn2_w_gate"], w["ffn2_w_up"], w["ffn2_w_down"], "ffn2_fwd")
    loss, dx3, d_final = _loss_head(x3, w["final_norm"].reshape(1, D), target, "loss_head")

    grads = {"final_norm": d_final.reshape(D)}

    def ffn_back(prefix, h, dy, x_in, norm):
        wg, wu, wd = (w[prefix + s] for s in ("_w_gate", "_w_up", "_w_down"))
        dh, act, dgate, dup, dout = _ffn_bwd(h, dy, wg, wu, wd, prefix + "_bwd")
        grads[prefix + "_w_gate"] = _matmul(h, dgate, ta=True, name=prefix + "_dwg")
        grads[prefix + "_w_up"] = _matmul(h, dup, ta=True, name=prefix + "_dwu")
        grads[prefix + "_w_down"] = _matmul(act, dout, ta=True, name=prefix + "_dwd")
        dx, grads[prefix + "_norm"] = _rms_bwd(x_in, norm, dh, dy, prefix + "_norm_bwd")
        return dx

    dx2 = ffn_back("ffn2", h3, dx3, x2, w["ffn2_norm"])
    grads["w_out"] = _matmul(o, dx2, ta=True, name="d_w_out")
    do = _matmul(dx2, w["w_out"], tb=True, name="d_mixed")
    dqa, dfa, dia, dga, grads["hgrn_out_norm"], grads["hgrn_lb_logits"] = _hgrn_bwd(
        p_all, w["hgrn_lb_logits"], w["hgrn_out_norm"], oraw, states, do[:, :WA], "hgrn_bwd")
    dy, dg, dr_b, dk2_b, dv_b, grads["rwkv_gn_w"], grads["rwkv_gn_b"], d_rk = _rwkv_post_bwd(
        do[:, WA:], y, r, k2, v, g, *post_w, "rwkv_post_bwd")
    grads["rwkv_r_k"] = d_rk.reshape(w["rwkv_r_k"].shape)
    dr, dw, dk2, dv, dsa, dsb = _rwkv_scan_bwd(dy, r, decay, k2, v, sa, sb, sall, "rwkv_scan_bwd")
    (dsr, dsk, dsv, dslo, dw2cat, grads["rwkv_w0"], grads["rwkv_a0"], grads["rwkv_k_k"],
     grads["rwkv_k_a"]) = _rwkv_prep_bwd((dr, dw, dk2, dv, dsa, dsb, dg, dr_b, dk2_b, dv_b), p_all, *rw,
                                         "rwkv_prep_bwd")
    grads["rwkv_w2"] = dw2cat[0:n_w, 0:WB]
    grads["rwkv_a2"] = dw2cat[n_w:n_w + n_a, WB:2 * WB]
    grads["rwkv_g2"] = dw2cat[n_w + n_a:n_w + n_a + n_g, 2 * WB:]
    dpr, dpk, dpv, dplo, dmu_r, dmu_k, dmu_v, dmu_lo = _shift_bwd((dsr, dsk, dsv, dslo), p_all, mu_pad, "shift_bwd")
    grads["rwkv_shift_mu"] = jnp.concatenate([dmu_r, dmu_k, dmu_v, dmu_lo], axis=1)[:, :1696]
    dp = jnp.concatenate([dqa, dfa, dia, dga, dpr, dpk, dpv, dplo], axis=1)
    grads["w_in"] = _matmul(h2, dp, ta=True, name="d_w_in")[:, :N_IN]
    dh2 = _matmul(dp, w_in_pad, tb=True, name="d_h2")
    dx1, grads["mix_norm"] = _rms_bwd(x1, w["mix_norm"], dh2, dx2, "mix_norm_bwd")
    dx0 = ffn_back("ffn1", h1, dx1, x, w["ffn1_norm"])
    return loss[0, 0], dx0, grads


def kernel(x, ffn1_norm, ffn1_w_gate, ffn1_w_up, ffn1_w_down, mix_norm, w_in, hgrn_lb_logits, hgrn_out_norm, rwkv_shift_mu, rwkv_w0, rwkv_w2, rwkv_a0, rwkv_a2, rwkv_g2, rwkv_k_k, rwkv_k_a, rwkv_r_k, rwkv_gn_w, rwkv_gn_b, w_out, ffn2_norm, ffn2_w_gate, ffn2_w_up, ffn2_w_down, final_norm, loss_target, m_ffn1_norm, m_ffn1_w_gate, m_ffn1_w_up, m_ffn1_w_down, m_mix_norm, m_w_in, m_hgrn_lb_logits, m_hgrn_out_norm, m_rwkv_shift_mu, m_rwkv_w0, m_rwkv_w2, m_rwkv_a0, m_rwkv_a2, m_rwkv_g2, m_rwkv_k_k, m_rwkv_k_a, m_rwkv_r_k, m_rwkv_gn_w, m_rwkv_gn_b, m_w_out, m_ffn2_norm, m_ffn2_w_gate, m_ffn2_w_up, m_ffn2_w_down, m_final_norm, v_ffn1_norm, v_ffn1_w_gate, v_ffn1_w_up, v_ffn1_w_down, v_mix_norm, v_w_in, v_hgrn_lb_logits, v_hgrn_out_norm, v_rwkv_shift_mu, v_rwkv_w0, v_rwkv_w2, v_rwkv_a0, v_rwkv_a2, v_rwkv_g2, v_rwkv_k_k, v_rwkv_k_a, v_rwkv_r_k, v_rwkv_gn_w, v_rwkv_gn_b, v_w_out, v_ffn2_norm, v_ffn2_w_gate, v_ffn2_w_up, v_ffn2_w_down, v_final_norm):
    args = dict(locals())
    wts = {n: args[n] for n in WEIGHTS}
    mom = {n: args["m_" + n] for n in WEIGHTS}
    var = {n: args["v_" + n] for n in WEIGHTS}
    shard_shapes = {n: wts[n].shape[1:] for n in SHARDED}

    gathered = _exchange([wts[n].astype(MXU) for n in SHARDED], [False] * len(SHARDED), "gather_weights")
    full = {n: _gathered_to_full(g, n, shard_shapes[n]) for n, g in zip(SHARDED, gathered)}
    for n in SMALL:
        full[n] = wts[n] if n in ("hgrn_lb_logits", "rwkv_r_k", "final_norm") else wts[n].reshape(1, -1)

    loss, grad_x, grads = _local_step(x[0], loss_target[0], full)
    loss = lax.psum(loss, AXES)

    contrib = [_full_to_per_device(grads[n], n) for n in SHARDED] + [_pack([grads[n] for n in SMALL], 8)]
    parts = _exchange(contrib, [True] * len(SHARDED) + [False], "scatter_grads")

    new = {}
    for n, p in zip(SHARDED, parts):
        new[n] = _adamw(p, wts[n], mom[n], var[n], "adamw_" + n)
    two_d = lambda a: a if a.ndim == 2 else a.reshape(1, -1)
    small = _adamw_small(parts[-1], *([two_d(src[n]) for n in SMALL] for src in (wts, mom, var)), "adamw_small")
    for j, n in enumerate(SMALL):
        new[n] = [res[j].reshape(wts[n].shape) for res in small]
    return (loss, grad_x[None], *[new[n][0] for n in WEIGHTS], *[new[n][1] for n in WEIGHTS],
            *[new[n][2] for n in WEIGHTS], *[new[n][3] for n in WEIGHTS])
```

```python
import functools
import math

import jax
import jax.numpy as jnp
from jax import lax
from jax.experimental import pallas as pl
from jax.experimental.pallas import tpu as pltpu

F32 = jnp.float32
MXU = jnp.bfloat16
D = 1024
FF = 2816
WA = 512
WB = 512
HD_B = 64
N_IN = 3744
N_INP = 3840
COL_R, COL_K, COL_V = 4, 5, 6
COL_L = 14
LORA = (32, 32, 96)
HG_CHUNK = 64
SCAN_CHUNK = 64
NORM_EPS = 1e-6
GN_EPS = 64e-5
L2_EPS = 1e-12
DECAY_C = math.exp(-0.5)
N_DEV = 8
LANES = 128
ADAM_BLOCK_BYTES = 4 * 1024 * 1024
MATMUL_BLOCK_BYTES = 40 * 1024 * 1024
VMEM_LIMIT = 56 * 1024 * 1024
ADAM_LR, ADAM_B1, ADAM_B2, ADAM_EPS, ADAM_WD, ADAM_STEP = 0.001, 0.9, 0.999, 1e-08, 0.01, 10
AXES = ("x", "y", "c")

SHARDED = ("ffn1_w_gate", "ffn1_w_up", "ffn1_w_down", "w_in", "rwkv_w2", "rwkv_a2", "rwkv_g2", "w_out",
           "ffn2_w_gate", "ffn2_w_up", "ffn2_w_down")
COL_SHARDED = {"ffn1_w_gate", "ffn1_w_up", "w_in", "rwkv_w2", "rwkv_a2", "rwkv_g2", "ffn2_w_gate", "ffn2_w_up"}
SMALL = ("ffn1_norm", "mix_norm", "hgrn_lb_logits", "hgrn_out_norm", "rwkv_shift_mu", "rwkv_w0", "rwkv_a0",
         "rwkv_k_k", "rwkv_k_a", "rwkv_r_k", "rwkv_gn_w", "rwkv_gn_b", "ffn2_norm", "final_norm")
WEIGHTS = ("ffn1_norm", "ffn1_w_gate", "ffn1_w_up", "ffn1_w_down", "mix_norm", "w_in", "hgrn_lb_logits",
           "hgrn_out_norm", "rwkv_shift_mu", "rwkv_w0", "rwkv_w2", "rwkv_a0", "rwkv_a2", "rwkv_g2", "rwkv_k_k",
           "rwkv_k_a", "rwkv_r_k", "rwkv_gn_w", "rwkv_gn_b", "w_out", "ffn2_norm", "ffn2_w_gate", "ffn2_w_up",
           "ffn2_w_down", "final_norm")


def _tile(n, cap):
    if n <= cap:
        return n
    for t in range(cap - cap % LANES, 0, -LANES):
        if n % t == 0:
            return t
    raise ValueError((n, cap))


def _params(n_axes):
    return pltpu.CompilerParams(dimension_semantics=("arbitrary",) * n_axes, vmem_limit_bytes=VMEM_LIMIT)


def _sig(x):
    return jax.nn.sigmoid(x)


def _dsilu(z, s):
    return s * (1.0 + z * (1.0 - s))


def _dot(a, b, dims=((1,), (0,)), precision=None):
    return lax.dot_general(a, b, (dims, ((), ())), preferred_element_type=F32, precision=precision)


_NT = ((1,), (1,))
_TN = ((0,), (0,))
_HI = lax.Precision.HIGHEST


def _iota(shape, dim):
    return lax.broadcasted_iota(jnp.int32, shape, dim)


def _split_dot(x, ones, passes):
    hi = x.astype(jnp.bfloat16)
    acc = _dot(hi, ones)
    rem = x
    for _ in range(passes - 1):
        rem = rem - hi.astype(F32)
        hi = rem.astype(jnp.bfloat16)
        acc = acc + _dot(hi, ones)
    return acc


def _head_ones(n, width):
    shift = width.bit_length() - 1
    return (_iota((n, n), 0) >> shift == _iota((n, n), 1) >> shift).astype(jnp.bfloat16)


def _matmul(a, b, *, ta=False, tb=False, out_dtype=F32, res=None, name):
    m, k = (a.shape[1], a.shape[0]) if ta else a.shape
    n = b.shape[0] if tb else b.shape[1]
    tm, tn = _tile(m, 1408), _tile(n, 1408)
    in_bytes = max(a.dtype.itemsize, b.dtype.itemsize)
    for tk in (_tile(k, 1024), _tile(k, 512), _tile(k, 256)):
        if 2 * (tm + tn) * tk * in_bytes + 3 * tm * tn * 4 <= MATMUL_BLOCK_BYTES:
            break
    nk = k // tk
    dims = ((0 if ta else 1,), (1 if tb else 0,))

    def body(*refs):
        a_ref, b_ref = refs[:2]
        o_ref, acc = refs[-2:]
        kk = pl.program_id(2)

        @pl.when(kk == 0)
        def _():
            acc[...] = jnp.zeros_like(acc)

        acc[...] += _dot(a_ref[...].astype(MXU), b_ref[...].astype(MXU), dims)

        @pl.when(kk == nk - 1)
        def _():
            v = acc[...]
            if res is not None:
                v = v + refs[2][...]
            o_ref[...] = v.astype(out_dtype)

    a_spec = pl.BlockSpec((tk, tm), lambda i, j, kk: (kk, i)) if ta else pl.BlockSpec((tm, tk), lambda i, j, kk: (i, kk))
    b_spec = pl.BlockSpec((tn, tk), lambda i, j, kk: (j, kk)) if tb else pl.BlockSpec((tk, tn), lambda i, j, kk: (kk, j))
    o_spec = pl.BlockSpec((tm, tn), lambda i, j, kk: (i, j))
    ins, specs = [a, b], [a_spec, b_spec]
    if res is not None:
        ins.append(res)
        specs.append(o_spec)
    return pl.pallas_call(
        body, name=name, grid=(m // tm, n // tn, nk), in_specs=specs, out_specs=o_spec,
        out_shape=jax.ShapeDtypeStruct((m, n), out_dtype), scratch_shapes=[pltpu.VMEM((tm, tn), F32)],
        compiler_params=_params(3))(*ins)


def _rms_fwd(x, g, name):
    t = x.shape[0]
    tb = _tile(t, 512)

    def body(x_ref, g_ref, o_ref):
        xv = x_ref[...]
        rinv = lax.rsqrt(jnp.mean(xv * xv, axis=-1, keepdims=True) + NORM_EPS)
        o_ref[...] = (xv * rinv * g_ref[...]).astype(MXU)

    return pl.pallas_call(
        body, name=name, grid=(t // tb,),
        in_specs=[pl.BlockSpec((tb, D), lambda i: (i, 0)), pl.BlockSpec((1, D), lambda i: (0, 0))],
        out_specs=pl.BlockSpec((tb, D), lambda i: (i, 0)), out_shape=jax.ShapeDtypeStruct((t, D), MXU),
        compiler_params=_params(1))(x, g)


def _rms_bwd(x, g, dh, dres, name):
    t = x.shape[0]
    tb = _tile(t, 512)

    def body(x_ref, g_ref, dh_ref, dres_ref, dx_ref, dg_ref):
        @pl.when(pl.program_id(0) == 0)
        def _():
            dg_ref[...] = jnp.zeros_like(dg_ref)

        xv = x_ref[...]
        rinv = lax.rsqrt(jnp.mean(xv * xv, axis=-1, keepdims=True) + NORM_EPS)
        xhat = xv * rinv
        dhv = dh_ref[...]
        dg_ref[...] += jnp.sum(dhv * xhat, axis=0, keepdims=True)
        dxhat = dhv * g_ref[...]
        dx_ref[...] = dres_ref[...] + rinv * (dxhat - xhat * jnp.mean(dxhat * xhat, axis=-1, keepdims=True))

    row = pl.BlockSpec((tb, D), lambda i: (i, 0))
    vec = pl.BlockSpec((1, D), lambda i: (0, 0))
    return pl.pallas_call(
        body, name=name, grid=(t // tb,), in_specs=[row, vec, row, row], out_specs=[row, vec],
        out_shape=[jax.ShapeDtypeStruct((t, D), F32), jax.ShapeDtypeStruct((1, D), F32)],
        compiler_params=_params(1))(x, g, dh, dres)


def _loss_head(x, g, target, name):
    t = x.shape[0]
    tb = _tile(t, 512)

    def body(x_ref, g_ref, t_ref, loss_ref, dx_ref, dg_ref):
        @pl.when(pl.program_id(0) == 0)
        def _():
            dg_ref[...] = jnp.zeros_like(dg_ref)
            loss_ref[...] = jnp.zeros_like(loss_ref)

        xv = x_ref[...]
        gv = g_ref[...]
        rinv = lax.rsqrt(jnp.mean(xv * xv, axis=-1, keepdims=True) + NORM_EPS)
        xhat = xv * rinv
        err = xhat * gv - t_ref[...]
        per_tok = jnp.mean(err * err, axis=-1, keepdims=True)
        loss_ref[...] += jnp.broadcast_to(0.5 * jnp.sum(per_tok, axis=0, keepdims=True), loss_ref.shape)
        dy = err * (1.0 / D)
        dg_ref[...] += jnp.sum(dy * xhat, axis=0, keepdims=True)
        dxhat = dy * gv
        dx_ref[...] = rinv * (dxhat - xhat * jnp.mean(dxhat * xhat, axis=-1, keepdims=True))

    row = pl.BlockSpec((tb, D), lambda i: (i, 0))
    vec = pl.BlockSpec((1, D), lambda i: (0, 0))
    return pl.pallas_call(
        body, name=name, grid=(t // tb,), in_specs=[row, vec, row],
        out_specs=[pl.BlockSpec((1, LANES), lambda i: (0, 0)), row, vec],
        out_shape=[jax.ShapeDtypeStruct((1, LANES), F32), jax.ShapeDtypeStruct((t, D), F32),
                   jax.ShapeDtypeStruct((1, D), F32)],
        compiler_params=_params(1))(x, g, target)


def _ffn_fwd(x, h, wg, wu, wd, name):
    t = x.shape[0]
    tb, fb = _tile(t, 1024), 256
    nf = FF // fb

    def body(x_ref, h_ref, wg_ref, wu_ref, wd_ref, o_ref, acc):
        f = pl.program_id(1)

        @pl.when(f == 0)
        def _():
            acc[...] = jnp.zeros_like(acc)

        hv = h_ref[...]
        gate = _dot(hv, wg_ref[...])
        up = _dot(hv, wu_ref[...])
        act = (gate * _sig(gate) * up).astype(MXU)
        acc[...] += _dot(act, wd_ref[...])

        @pl.when(f == nf - 1)
        def _():
            o_ref[...] = x_ref[...] + 0.5 * acc[...]

    row = pl.BlockSpec((tb, D), lambda i, f: (i, 0))
    col = pl.BlockSpec((D, fb), lambda i, f: (0, f))
    return pl.pallas_call(
        body, name=name, grid=(t // tb, nf),
        in_specs=[row, row, col, col, pl.BlockSpec((fb, D), lambda i, f: (f, 0))], out_specs=row,
        out_shape=jax.ShapeDtypeStruct((t, D), F32), scratch_shapes=[pltpu.VMEM((tb, D), F32)],
        compiler_params=_params(2))(x, h, wg, wu, wd)


def _ffn_bwd(h, dy, wg, wu, wd, name):
    t = h.shape[0]
    tb, fb = _tile(t, 1024), 256
    nf = FF // fb

    def body(h_ref, dy_ref, wg_ref, wu_ref, wd_ref, dh_ref, act_ref, dg_ref, du_ref, dout_ref, acc):
        f = pl.program_id(1)

        @pl.when(f == 0)
        def _():
            acc[...] = jnp.zeros_like(acc)

        hv = h_ref[...]
        dout = (0.5 * dy_ref[...]).astype(MXU)
        dout_ref[...] = dout
        gate = _dot(hv, wg_ref[...])
        up = _dot(hv, wu_ref[...])
        dact = _dot(dout, wd_ref[...], _NT)
        s = _sig(gate)
        silu = gate * s
        act_ref[...] = (silu * up).astype(MXU)
        dup = (dact * silu).astype(MXU)
        dgate = (dact * up * _dsilu(gate, s)).astype(MXU)
        du_ref[...] = dup
        dg_ref[...] = dgate
        acc[...] += _dot(dgate, wg_ref[...], _NT) + _dot(dup, wu_ref[...], _NT)

        @pl.when(f == nf - 1)
        def _():
            dh_ref[...] = acc[...]

    row = pl.BlockSpec((tb, D), lambda i, f: (i, 0))
    col = pl.BlockSpec((D, fb), lambda i, f: (0, f))
    hid = pl.BlockSpec((tb, fb), lambda i, f: (i, f))
    hid_shape = jax.ShapeDtypeStruct((t, FF), MXU)
    return pl.pallas_call(
        body, name=name, grid=(t // tb, nf),
        in_specs=[row, row, col, col, pl.BlockSpec((fb, D), lambda i, f: (f, 0))],
        out_specs=[row, hid, hid, hid, row],
        out_shape=[jax.ShapeDtypeStruct((t, D), F32), hid_shape, hid_shape, hid_shape,
                   jax.ShapeDtypeStruct((t, D), MXU)],
        scratch_shapes=[pltpu.VMEM((tb, D), F32)], compiler_params=_params(2))(h, dy, wg, wu, wd)


def _hgrn_chunk(qa, fa, lbl_ref):
    c = HG_CHUNK
    lb = _sig(lbl_ref[0:1, :] - lbl_ref[1:2, :])
    sf = _sig(fa)
    forget = lb + (1.0 - lb) * sf
    kh = 1.0 - forget
    row, col = _iota((c, c), 0), _iota((c, c), 1)
    b = _dot((col <= row).astype(F32), jnp.log(forget), precision=_HI)
    bref, blast = b[c // 2:c // 2 + 1, :], b[c - 1:c, :]
    sq = _sig(qa)
    q = qa * sq
    qt, kt = q * jnp.exp(b - bref), kh * jnp.exp(bref - b)
    qb, kl = q * jnp.exp(b), kh * jnp.exp(blast - b)
    causal = col <= row
    return dict(lb=lb, sf=sf, forget=forget, sq=sq, qt=qt, kt=kt, qb=qb, kl=kl, decay=jnp.exp(blast),
                causal=causal, e_q=jnp.exp(b), e_qt=jnp.exp(b - bref), e_kt=jnp.exp(bref - b),
                e_kl=jnp.exp(blast - b))


def _hgrn_specs(t):
    c = HG_CHUNK
    return c, t // c, WA // LANES


def _hgrn_fwd(p_all, lbl, onorm, name):
    t = p_all.shape[0]
    c, n, nh = _hgrn_specs(t)

    def body(q_ref, f_ref, i_ref, g_ref, lbl_ref, on_ref, oa_ref, oraw_ref, st_ref, state):
        @pl.when(pl.program_id(1) == 0)
        def _():
            state[...] = jnp.zeros_like(state)

        k = _hgrn_chunk(q_ref[...], f_ref[...], lbl_ref)
        v = i_ref[...]
        st = state[...]
        st_ref[0, 0] = st
        a = jnp.where(k["causal"], _dot(k["qt"], k["kt"], _NT, _HI), 0.0)
        o = _dot(a, v, precision=_HI) + _dot(k["qb"], st, _NT, _HI)
        state[...] = st * k["decay"] + _dot(v, k["kl"], _TN, _HI)
        oraw_ref[...] = o
        rinv = lax.rsqrt(jnp.mean(o * o, axis=-1, keepdims=True) + NORM_EPS)
        ga = g_ref[...]
        oa_ref[...] = (o * rinv * on_ref[...] * (ga * _sig(ga))).astype(MXU)

    def blk(j):
        return pl.BlockSpec((c, LANES), lambda h, i: (i, j * nh + h))

    head = pl.BlockSpec((c, LANES), lambda h, i: (i, h))
    return pl.pallas_call(
        body, name=name, grid=(nh, n),
        in_specs=[blk(0), blk(1), blk(2), blk(3), pl.BlockSpec((2, LANES), lambda h, i: (0, h)),
                  pl.BlockSpec((1, LANES), lambda h, i: (0, h))],
        out_specs=[head, head, pl.BlockSpec((1, 1, LANES, LANES), lambda h, i: (h, i, 0, 0))],
        out_shape=[jax.ShapeDtypeStruct((t, WA), MXU), jax.ShapeDtypeStruct((t, WA), F32),
                   jax.ShapeDtypeStruct((nh, n, LANES, LANES), F32)],
        scratch_shapes=[pltpu.VMEM((LANES, LANES), F32)], compiler_params=_params(2))(
            p_all, p_all, p_all, p_all, lbl, onorm)


def _hgrn_bwd(p_all, lbl, onorm, oraw, states, doa, name):
    t = p_all.shape[0]
    c, n, nh = _hgrn_specs(t)

    def body(q_ref, f_ref, i_ref, g_ref, lbl_ref, on_ref, oraw_ref, st_ref, doa_ref,
             dq_ref, df_ref, di_ref, dg_ref, don_ref, dlbl_ref, dstate, dlb):
        @pl.when(pl.program_id(1) == 0)
        def _():
            dstate[...] = jnp.zeros_like(dstate)
            dlb[...] = jnp.zeros_like(dlb)
            don_ref[...] = jnp.zeros_like(don_ref)

        qa, fa, v, ga = q_ref[...], f_ref[...], i_ref[...], g_ref[...]
        k = _hgrn_chunk(qa, fa, lbl_ref)
        st, dst_next = st_ref[0, 0], dstate[...]
        o = oraw_ref[...]
        rinv = lax.rsqrt(jnp.mean(o * o, axis=-1, keepdims=True) + NORM_EPS)
        on = o * rinv
        sg = _sig(ga)
        gate = ga * sg
        dout = doa_ref[...]
        don_ref[...] += jnp.sum(dout * on * gate, axis=0, keepdims=True)
        dg_ref[...] = (dout * on * on_ref[...] * _dsilu(ga, sg)).astype(MXU)
        d_on = dout * on_ref[...] * gate
        do = rinv * (d_on - on * jnp.mean(d_on * on, axis=-1, keepdims=True))

        a = jnp.where(k["causal"], _dot(k["qt"], k["kt"], _NT, _HI), 0.0)
        dqb = _dot(do, st, precision=_HI)
        dstate[...] = dst_next * k["decay"] + _dot(do, k["qb"], _TN, _HI)
        da = jnp.where(k["causal"], _dot(do, v, _NT, _HI), 0.0)
        dqt = _dot(da, k["kt"], precision=_HI)
        dkt = _dot(da, k["qt"], _TN, _HI)
        dv = _dot(a, do, _TN, _HI) + _dot(k["kl"], dst_next, _NT, _HI)
        dkl = _dot(v, dst_next, precision=_HI)
        ddecay = jnp.sum(dst_next * st, axis=0, keepdims=True)
        dq = dqb * k["e_q"] + dqt * k["e_qt"]
        dk = dkt * k["e_kt"] + dkl * k["e_kl"]
        tq, tk, tl = dqt * k["qt"], dkt * k["kt"], dkl * k["kl"]
        db = dqb * k["qb"] + tq - tk - tl
        dbref = jnp.sum(tk - tq, axis=0, keepdims=True)
        dblast = jnp.sum(tl, axis=0, keepdims=True) + ddecay * k["decay"]
        rows = _iota((c, LANES), 0)
        db = db + jnp.where(rows == c // 2, dbref, 0.0) + jnp.where(rows == c - 1, dblast, 0.0)
        row, col = _iota((c, c), 0), _iota((c, c), 1)
        dlogf = _dot((col >= row).astype(F32), db, precision=_HI)
        dq_ref[...] = (dq * _dsilu(qa, k["sq"])).astype(MXU)
        di_ref[...] = dv.astype(MXU)
        dforget = dlogf / k["forget"] - dk
        sf, lb = k["sf"], k["lb"]
        df_ref[...] = (dforget * (1.0 - lb) * sf * (1.0 - sf)).astype(MXU)
        dlb[...] += jnp.sum(dforget * (1.0 - sf), axis=0, keepdims=True)
        dl0 = dlb[...] * lb * (1.0 - lb)
        dlbl_ref[...] = jnp.where(_iota((2, LANES), 0) == 0, dl0, -dl0)

    def blk(j):
        return pl.BlockSpec((c, LANES), lambda h, i: (n - 1 - i, j * nh + h))

    head = pl.BlockSpec((c, LANES), lambda h, i: (n - 1 - i, h))
    vec = pl.BlockSpec((1, LANES), lambda h, i: (0, h))
    lg = pl.BlockSpec((2, LANES), lambda h, i: (0, h))
    grad = jax.ShapeDtypeStruct((t, WA), MXU)
    return pl.pallas_call(
        body, name=name, grid=(nh, n),
        in_specs=[blk(0), blk(1), blk(2), blk(3), lg, vec, head,
                  pl.BlockSpec((1, 1, LANES, LANES), lambda h, i: (h, n - 1 - i, 0, 0)), head],
        out_specs=[head, head, head, head, vec, lg],
        out_shape=[grad, grad, grad, grad, jax.ShapeDtypeStruct((1, WA), F32), jax.ShapeDtypeStruct((2, WA), F32)],
        scratch_shapes=[pltpu.VMEM((LANES, LANES), F32), pltpu.VMEM((1, LANES), F32)],
        compiler_params=_params(2))(p_all, p_all, p_all, p_all, lbl, onorm, oraw, states, doa)


def _lora_act(x):
    lane = _iota(x.shape, 1)
    n_w, n_a, n_g = LORA
    return jnp.where(lane < n_w, jnp.tanh(x),
                     jnp.where(lane < n_w + n_a, x, jnp.where(lane < n_w + n_a + n_g, _sig(x), 0.0)))


def _lora_dact(x):
    lane = _iota(x.shape, 1)
    n_w, n_a, n_g = LORA
    th, s = jnp.tanh(x), _sig(x)
    return jnp.where(lane < n_w, 1.0 - th * th,
                     jnp.where(lane < n_w + n_a, 1.0, jnp.where(lane < n_w + n_a + n_g, s * (1.0 - s), 0.0)))


def _shift_down(cur, prev8, first):
    rolled = pltpu.roll(cur, 1, 0)
    edge = prev8[7:8, :] * jnp.where(first, 0.0, 1.0)
    return jnp.where(_iota(cur.shape, 0) == 0, edge, rolled)


def _shift_up(cur, next8, last):
    rows = cur.shape[0]
    rolled = pltpu.roll(cur, rows - 1, 0)
    edge = next8[0:1, :] * jnp.where(last, 0.0, 1.0)
    return jnp.where(_iota(cur.shape, 0) == rows - 1, edge, rolled)


def _rwkv_inputs(refs, first, ones):
    (pr, pk, pv, plo, qr, qk, qv, qlo, mr, mk, mv, mlo, w2c, w0, a0, kk_w, ka_w) = refs
    mix = lambda cur, prev, mu: cur[...] + mu[...] * (_shift_down(cur[...], prev[...], first) - cur[...])
    r, k, v, lo = mix(pr, qr, mr), mix(pk, qk, mk), mix(pv, qv, mv), mix(plo, qlo, mlo)
    z = _lora_act(lo)
    lin = _dot(z.astype(MXU), w2c[...])
    sg = _sig(w0[...] + lin[:, :WB])
    decay = jnp.exp(-DECAY_C * sg)
    a = _sig(a0[...] + lin[:, WB:2 * WB])
    g = lin[:, 2 * WB:]
    kk0 = k * kk_w[...]
    nrm = jnp.sqrt(_split_dot(kk0 * kk0, ones, 3))
    den = jnp.maximum(nrm, L2_EPS)
    kk = kk0 / den
    k2 = k * (1.0 + (a - 1.0) * ka_w[...])
    return dict(r=r, k=k, v=v, lo=lo, z=z, sg=sg, decay=decay, a=a, g=g, kk=kk, den=den, nrm=nrm, k2=k2)


def _rwkv_in_specs(t, tb):
    nt8 = tb // 8

    def cur(w, j):
        return pl.BlockSpec((tb, w), lambda i: (i, j))

    def prev(w, j):
        return pl.BlockSpec((8, w), lambda i: (jnp.maximum(i * nt8 - 1, 0), j))

    def vec(w, j=0):
        return pl.BlockSpec((1, w), lambda i: (0, j))

    return [cur(WB, COL_R), cur(WB, COL_K), cur(WB, COL_V), cur(256, COL_L),
            prev(WB, COL_R), prev(WB, COL_K), prev(WB, COL_V), prev(256, COL_L),
            vec(WB, 0), vec(WB, 1), vec(WB, 2), vec(256, 6),
            pl.BlockSpec((256, 3 * WB), lambda i: (0, 0)), vec(WB), vec(WB), vec(WB), vec(WB)]


def _rwkv_in_args(p_all, mu_pad, w2cat, w0, a0, k_k, k_a):
    return (p_all,) * 8 + (mu_pad,) * 4 + (w2cat, w0, a0, k_k, k_a)


def _rwkv_prep(p_all, mu_pad, w2cat, w0, a0, k_k, k_a, name):
    t = p_all.shape[0]
    tb = _tile(t, 256)

    def body(*refs):
        ins, outs = refs[:17], refs[17:]
        q = _rwkv_inputs(ins, pl.program_id(0) == 0, _head_ones(WB, HD_B))
        for ref, val in zip(outs, (q["r"], q["decay"], q["k2"], q["v"], -q["kk"], q["kk"] * q["a"], q["g"])):
            ref[...] = val

    out = pl.BlockSpec((tb, WB), lambda i: (i, 0))
    return pl.pallas_call(
        body, name=name, grid=(t // tb,), in_specs=_rwkv_in_specs(t, tb), out_specs=[out] * 7,
        out_shape=[jax.ShapeDtypeStruct((t, WB), F32)] * 7, compiler_params=_params(1))(
            *_rwkv_in_args(p_all, mu_pad, w2cat, w0, a0, k_k, k_a))


def _pair_rows(x8, i):
    return jnp.concatenate([jnp.broadcast_to(x8[i:i + 1, p * LANES:(p + 1) * LANES], (HD_B, LANES))
                            for p in range(4)], axis=0)


def _pair_sums(x):
    return jnp.concatenate([jnp.sum(x[p * HD_B:(p + 1) * HD_B], axis=0, keepdims=True) for p in range(4)], axis=1)


def _put_row(buf, i, row):
    return jnp.where(_iota(buf.shape, 0) == i, row, buf)


def _pieces(x):
    hi = x.astype(jnp.bfloat16).astype(F32)
    lo = (x - hi).astype(jnp.bfloat16).astype(F32)
    upper = (_iota((x.shape[0], LANES), 1) & (HD_B // 2)) != 0
    swapped = [jnp.where(upper, pltpu.roll(lo[:, p * LANES:(p + 1) * LANES], HD_B // 2, 1),
                         pltpu.roll(lo[:, p * LANES:(p + 1) * LANES], LANES - HD_B // 2, 1)) for p in range(4)]
    return hi, jnp.concatenate(swapped, axis=1)


def _scan_consts():
    row, lane = _iota((HD_B, LANES), 0), _iota((HD_B, LANES), 1) & (HD_B - 1)
    return (row == lane).astype(jnp.bfloat16), (row == lane ^ (HD_B // 2)).astype(jnp.bfloat16), _head_ones(LANES, HD_B)


def _pair_cols(many, consts):
    diag_hi, diag_lo, ones = consts
    tiles = []
    for (hi8, lo8), i in many:
        for p in range(4):
            lanes = slice(p * LANES, (p + 1) * LANES)
            hi = jnp.broadcast_to(hi8[i:i + 1, lanes], (16, LANES)).astype(jnp.bfloat16)
            lo = jnp.broadcast_to(lo8[i:i + 1, lanes], (16, LANES)).astype(jnp.bfloat16)
            for g in range(HD_B // 16):
                rows = slice(g * 16, (g + 1) * 16)
                tiles.append(hi * diag_hi[rows] + lo * diag_lo[rows])
    out = _dot(jnp.concatenate(tiles, axis=0), ones)
    return [out[m * 4 * HD_B:(m + 1) * 4 * HD_B] for m in range(len(many))]


def _rwkv_scan_fwd(r, w, k, v, a, b, name):
    t = r.shape[0]
    cc = min(t, SCAN_CHUNK)

    def body(r_ref, w_ref, k_ref, v_ref, a_ref, b_ref, y_ref, sa_ref, sall_ref, state_k, state_v):
        @pl.when(pl.program_id(0) == 0)
        def _():
            state_k[...] = jnp.zeros_like(state_k)
            state_v[...] = jnp.zeros_like(state_v)

        consts = _scan_consts()

        def block(j, carry):
            sk, sv = carry
            base = pl.multiple_of(j * 8, 8)
            rows = pl.ds(base, 8)
            r8, w8, k8, v8, a8, b8 = (ref[rows, :] for ref in (r_ref, w_ref, k_ref, v_ref, a_ref, b_ref))
            rp, wp, kp, vp, ap, bp = (_pieces(x) for x in (r8, w8, k8, v8, a8, b8))
            y8 = jnp.zeros((8, WB), F32)
            sa8 = jnp.zeros((8, WB), F32)
            cols = _pair_cols([(x, i) for i in range(8) for x in (ap, wp, bp, kp, rp, vp)], consts)
            for i in range(8):
                a_c, w_c, b_c, k_c, r_c = cols[6 * i:6 * i + 5]
                sa = _pair_sums(sk * a_c)
                sk = sk * w_c + b_c * _pair_rows(sa, 0) + k_c * _pair_rows(v8, i)
                y8 = _put_row(y8, i, _pair_sums(sk * r_c))
                sa8 = _put_row(sa8, i, sa)
            sa_cols = _pair_cols([(_pieces(sa8), i) for i in range(8)], consts)
            for i in range(8):
                sv = sv * _pair_rows(w8, i) + sa_cols[i] * _pair_rows(b8, i) + cols[6 * i + 5] * _pair_rows(k8, i)
                sall_ref[base + i] = sv
            y_ref[rows, :] = y8
            sa_ref[rows, :] = sa8
            return sk, sv

        state_k[...], state_v[...] = lax.fori_loop(0, cc // 8, block, (state_k[...], state_v[...]))

    row = pl.BlockSpec((cc, WB), lambda i: (i, 0))
    tile = pltpu.VMEM((4 * HD_B, LANES), F32)
    return pl.pallas_call(
        body, name=name, grid=(t // cc,), in_specs=[row] * 6,
        out_specs=[row, row, pl.BlockSpec((cc, 4 * HD_B, LANES), lambda i: (i, 0, 0))],
        out_shape=[jax.ShapeDtypeStruct((t, WB), F32)] * 2 + [jax.ShapeDtypeStruct((t, 4 * HD_B, LANES), F32)],
        scratch_shapes=[tile, tile], compiler_params=_params(1))(r, w, k, v, a, b)


def _rwkv_scan_bwd(dy, r, w, k, v, a, b, sa, sall, name):
    t = r.shape[0]
    cc = min(t, SCAN_CHUNK)
    n = t // cc

    def body(dy_ref, r_ref, w_ref, k_ref, v_ref, a_ref, b_ref, sa_ref, sall_ref, sprev_ref,
             dr_ref, dw_ref, dk_ref, dv_ref, da_ref, db_ref, dstate_k, dstate_v):
        @pl.when(pl.program_id(0) == 0)
        def _():
            dstate_k[...] = jnp.zeros_like(dstate_k)
            dstate_v[...] = jnp.zeros_like(dstate_v)

        consts = _scan_consts()
        before_chunk = jnp.where(pl.program_id(0) == n - 1, 0.0, 1.0) * sprev_ref[0]

        def block(jj, carry):
            dk_s, dv_s, sc = carry
            j = cc // 8 - 1 - jj
            base = pl.multiple_of(j * 8, 8)
            rows = pl.ds(base, 8)
            dy8, r8, w8, k8, v8, a8, b8, sa8 = (ref[rows, :] for ref in
                                                (dy_ref, r_ref, w_ref, k_ref, v_ref, a_ref, b_ref, sa_ref))
            dyp, rp, wp, kp, vp, ap, bp, sap = (_pieces(x) for x in (dy8, r8, w8, k8, v8, a8, b8, sa8))
            dsa8, dv8 = jnp.zeros((8, WB), F32), jnp.zeros((8, WB), F32)
            steps = range(7, -1, -1)
            cols_k = _pair_cols([(x, i) for i in steps for x in (rp, bp, kp, wp, ap)], consts)
            cols_v = _pair_cols([(x, i) for i in steps for x in (dyp, vp, sap)], consts)
            for n_done, i in enumerate(steps):
                r_c, b_c, k_c, w_c, a_c = cols_k[5 * n_done:5 * n_done + 5]
                dk_s = dk_s + r_c * _pair_rows(dy8, i)
                dsa = _pair_sums(dk_s * b_c)
                dv8 = _put_row(dv8, i, _pair_sums(dk_s * k_c))
                dsa8 = _put_row(dsa8, i, dsa)
                dk_s = dk_s * w_c + a_c * _pair_rows(dsa, 0)
            dsa_cols = _pair_cols([(_pieces(dsa8), i) for i in steps], consts)
            outs = [jnp.zeros((8, WB), F32) for _ in range(5)]
            for n_done, i in enumerate(steps):
                if i > 0:
                    sp = sall_ref[base + i - 1]
                else:
                    sp = jnp.where(j == 0, before_chunk, sall_ref[jnp.maximum(base - 1, 0)])
                dy_c, v_c, sa_c = cols_v[3 * n_done:3 * n_done + 3]
                dsa_c = dsa_cols[n_done]
                dv_s = dv_s + dy_c * _pair_rows(r8, i)
                vals = (_pair_sums(sc * dy_c), _pair_sums(dv_s * sp), _pair_sums(dv_s * v_c),
                        _pair_sums(sp * dsa_c), _pair_sums(dv_s * sa_c))
                outs = [_put_row(o, i, val) for o, val in zip(outs, vals)]
                dv_s = dv_s * _pair_rows(w8, i) + dsa_c * _pair_rows(a8, i)
                sc = sp
            dr8, dw8, dk8, da8, db8 = outs
            for ref, o in zip((dr_ref, dw_ref, dk_ref, dv_ref, da_ref, db_ref), (dr8, dw8, dk8, dv8, da8, db8)):
                ref[rows, :] = o
            return dk_s, dv_s, sc

        dk_s, dv_s, _ = lax.fori_loop(0, cc // 8, block, (dstate_k[...], dstate_v[...], sall_ref[cc - 1]))
        dstate_k[...] = dk_s
        dstate_v[...] = dv_s

    row = pl.BlockSpec((cc, WB), lambda i: (n - 1 - i, 0))
    tile = pltpu.VMEM((4 * HD_B, LANES), F32)
    return pl.pallas_call(
        body, name=name, grid=(n,),
        in_specs=[row] * 8 + [pl.BlockSpec((cc, 4 * HD_B, LANES), lambda i: (n - 1 - i, 0, 0)),
                              pl.BlockSpec((1, 4 * HD_B, LANES), lambda i: (jnp.maximum((n - 1 - i) * cc - 1, 0), 0, 0))],
        out_specs=[row] * 6, out_shape=[jax.ShapeDtypeStruct((t, WB), F32)] * 6,
        scratch_shapes=[tile, tile], compiler_params=_params(1))(dy, r, w, k, v, a, b, sa, sall, sall)


def _rwkv_post(y, r, k2, v, g, r_k, gn_w, gn_b, name):
    t = y.shape[0]
    tb = _tile(t, 256)

    def body(y_ref, r_ref, k_ref, v_ref, g_ref, rk_ref, gw_ref, gb_ref, o_ref):
        ones = _head_ones(WB, HD_B)
        yv = y_ref[...]
        yc = yv - _split_dot(yv, ones, 3) * (1.0 / HD_B)
        rstd = lax.rsqrt(_split_dot(yc * yc, ones, 3) * (1.0 / HD_B) + GN_EPS)
        rk = _split_dot(r_ref[...] * k_ref[...] * rk_ref[...], ones, 3)
        o_ref[...] = ((yc * rstd * gw_ref[...] + gb_ref[...] + rk * v_ref[...]) * g_ref[...]).astype(MXU)

    row = pl.BlockSpec((tb, WB), lambda i: (i, 0))
    vec = pl.BlockSpec((1, WB), lambda i: (0, 0))
    return pl.pallas_call(
        body, name=name, grid=(t // tb,), in_specs=[row] * 5 + [vec] * 3, out_specs=row,
        out_shape=jax.ShapeDtypeStruct((t, WB), MXU), compiler_params=_params(1))(y, r, k2, v, g, r_k, gn_w, gn_b)


def _rwkv_post_bwd(dob, y, r, k2, v, g, r_k, gn_w, gn_b, name):
    t = y.shape[0]
    tb = _tile(t, 256)

    def body(do_ref, y_ref, r_ref, k_ref, v_ref, g_ref, rk_ref, gw_ref, gb_ref,
             dy_ref, dg_ref, dr_ref, dk_ref, dv_ref, dgw_ref, dgb_ref, drk_ref):
        @pl.when(pl.program_id(0) == 0)
        def _():
            dgw_ref[...] = jnp.zeros_like(dgw_ref)
            dgb_ref[...] = jnp.zeros_like(dgb_ref)
            drk_ref[...] = jnp.zeros_like(drk_ref)

        ones = _head_ones(WB, HD_B)
        seg = lambda x: _split_dot(x, ones, 3)
        yv, rv, kv, vv, gv = y_ref[...], r_ref[...], k_ref[...], v_ref[...], g_ref[...]
        yc = yv - seg(yv) * (1.0 / HD_B)
        rstd = lax.rsqrt(seg(yc * yc) * (1.0 / HD_B) + GN_EPS)
        yn = yc * rstd
        rk = seg(rv * kv * rk_ref[...])
        dob_v = do_ref[...]
        dg_ref[...] = dob_v * (yn * gw_ref[...] + gb_ref[...] + rk * vv)
        dyg = dob_v * gv
        dgw_ref[...] += jnp.sum(dyg * yn, axis=0, keepdims=True)
        dgb_ref[...] += jnp.sum(dyg, axis=0, keepdims=True)
        dyn = dyg * gw_ref[...]
        dy_ref[...] = rstd * (dyn - (seg(dyn) + yn * seg(dyn * yn)) * (1.0 / HD_B))
        drk = seg(dyg * vv)
        dv_ref[...] = dyg * rk
        dr_ref[...] = drk * kv * rk_ref[...]
        dk_ref[...] = drk * rv * rk_ref[...]
        drk_ref[...] += jnp.sum(drk * rv * kv, axis=0, keepdims=True)

    row = pl.BlockSpec((tb, WB), lambda i: (i, 0))
    vec = pl.BlockSpec((1, WB), lambda i: (0, 0))
    full, small = jax.ShapeDtypeStruct((t, WB), F32), jax.ShapeDtypeStruct((1, WB), F32)
    return pl.pallas_call(
        body, name=name, grid=(t // tb,), in_specs=[row] * 6 + [vec] * 3, out_specs=[row] * 5 + [vec] * 3,
        out_shape=[full] * 5 + [small] * 3, compiler_params=_params(1))(dob, y, r, k2, v, g, r_k, gn_w, gn_b)


def _rwkv_prep_bwd(grads, p_all, mu_pad, w2cat, w0, a0, k_k, k_a, name):
    t = p_all.shape[0]
    tb = _tile(t, 256)

    def body(*refs):
        g_refs, ins, outs = refs[:10], refs[10:27], refs[27:]
        dr_s, dw, dk2_s, dv_s, das, dbs, dg, dr_b, dk2_b, dv_b = (ref[...] for ref in g_refs)
        dr_ref, dk_ref, dv_ref, dlo_ref, dw2_ref, dw0_ref, da0_ref, dkk_ref, dka_ref = outs

        @pl.when(pl.program_id(0) == 0)
        def _():
            for ref in (dw2_ref, dw0_ref, da0_ref, dkk_ref, dka_ref):
                ref[...] = jnp.zeros_like(ref)

        ones = _head_ones(WB, HD_B)
        q = _rwkv_inputs(ins, pl.program_id(0) == 0, ones)
        kk_w, ka_w = ins[15][...], ins[16][...]
        a, kk, k = q["a"], q["kk"], q["k"]
        dk2 = dk2_s + dk2_b
        dkk = dbs * a - das
        da = dbs * kk + dk2 * k * ka_w
        dk = dk2 * (1.0 + (a - 1.0) * ka_w)
        dka_ref[...] += jnp.sum(dk2 * k * (a - 1.0), axis=0, keepdims=True)
        proj = jnp.where(q["nrm"] > L2_EPS, _split_dot(dkk * kk, ones, 3), 0.0)
        dkk0 = (dkk - kk * proj) / q["den"]
        dk = dk + dkk0 * kk_w
        dkk_ref[...] += jnp.sum(dkk0 * k, axis=0, keepdims=True)
        dal = da * a * (1.0 - a)
        da0_ref[...] += jnp.sum(dal, axis=0, keepdims=True)
        sg = q["sg"]
        dwl = dw * q["decay"] * (-DECAY_C) * sg * (1.0 - sg)
        dw0_ref[...] += jnp.sum(dwl, axis=0, keepdims=True)
        dlin = jnp.concatenate([dwl, dal, dg], axis=1).astype(MXU)
        dw2_ref[...] += _dot(q["z"].astype(MXU), dlin, _TN)
        dz = _dot(dlin, ins[12][...], _NT)
        dlo_ref[...] = dz * _lora_dact(q["lo"])
        dr_ref[...] = dr_s + dr_b
        dk_ref[...] = dk
        dv_ref[...] = dv_s + dv_b

    row = pl.BlockSpec((tb, WB), lambda i: (i, 0))
    vec = pl.BlockSpec((1, WB), lambda i: (0, 0))
    full, small = jax.ShapeDtypeStruct((t, WB), F32), jax.ShapeDtypeStruct((1, WB), F32)
    return pl.pallas_call(
        body, name=name, grid=(t // tb,), in_specs=[row] * 10 + _rwkv_in_specs(t, tb),
        out_specs=[row] * 3 + [pl.BlockSpec((tb, 256), lambda i: (i, 0)),
                               pl.BlockSpec((256, 3 * WB), lambda i: (0, 0))] + [vec] * 4,
        out_shape=[full] * 3 + [jax.ShapeDtypeStruct((t, 256), F32), jax.ShapeDtypeStruct((256, 3 * WB), F32)]
        + [small] * 4, compiler_params=_params(1))(*grads, *_rwkv_in_args(p_all, mu_pad, w2cat, w0, a0, k_k, k_a))


def _shift_bwd(dshifted, p_all, mu_pad, name):
    t = p_all.shape[0]
    tb = _tile(t, 256)
    nt, nt8 = t // tb, tb // 8
    widths, cols, mus = (WB, WB, WB, 256), (COL_R, COL_K, COL_V, COL_L), (0, 1, 2, 6)

    def body(*refs):
        d_refs, n_refs, p_refs, q_refs, m_refs = refs[0:4], refs[4:8], refs[8:12], refs[12:16], refs[16:20]
        o_refs, dmu_refs = refs[20:24], refs[24:28]
        i = pl.program_id(0)

        @pl.when(i == 0)
        def _():
            for ref in dmu_refs:
                ref[...] = jnp.zeros_like(ref)

        for d, nx, p, q, m, o, dmu in zip(d_refs, n_refs, p_refs, q_refs, m_refs, o_refs, dmu_refs):
            dv, pv, mu = d[...], p[...], m[...]
            o[...] = (dv * (1.0 - mu) + mu * _shift_up(dv, nx[...], i == nt - 1)).astype(MXU)
            dmu[...] += jnp.sum(dv * (_shift_down(pv, q[...], i == 0) - pv), axis=0, keepdims=True)

    cur_d = [pl.BlockSpec((tb, w), lambda i: (i, 0)) for w in widths]
    next_d = [pl.BlockSpec((8, w), lambda i: (jnp.minimum((i + 1) * nt8, t // 8 - 1), 0)) for w in widths]
    cur_p = [pl.BlockSpec((tb, w), lambda i, j=j: (i, j)) for w, j in zip(widths, cols)]
    prev_p = [pl.BlockSpec((8, w), lambda i, j=j: (jnp.maximum(i * nt8 - 1, 0), j)) for w, j in zip(widths, cols)]
    mu_s = [pl.BlockSpec((1, w), lambda i, j=j: (0, j)) for w, j in zip(widths, mus)]
    vecs = [pl.BlockSpec((1, w), lambda i: (0, 0)) for w in widths]
    return pl.pallas_call(
        body, name=name, grid=(nt,), in_specs=cur_d + next_d + cur_p + prev_p + mu_s, out_specs=cur_d + vecs,
        out_shape=[jax.ShapeDtypeStruct((t, w), MXU) for w in widths]
        + [jax.ShapeDtypeStruct((1, w), F32) for w in widths],
        compiler_params=_params(1))(*dshifted, *dshifted, *(p_all,) * 8, *(mu_pad,) * 4)


def _peer(k):
    x, y, c = (lax.axis_index(n) for n in AXES)
    px = 1 - x if k & 4 else x
    py = 1 - y if k & 2 else y
    pc = 1 - c if k & 1 else c
    return (px, py, pc), 4 * px + 2 * py + pc


def _exchange(srcs, per_peer, name):
    n = len(srcs)
    shapes = [tuple(s.shape[1:]) if pp else tuple(s.shape) for s, pp in zip(srcs, per_peer)]
    pairs = [(j, k) for k in range(1, N_DEV) for j in range(n)]

    def body(*refs):
        src_refs, out_refs = refs[:n], refs[n:2 * n]
        send_sems, recv_sems, local_sems = refs[2 * n:]
        _, me = _peer(0)

        def copy(j, k, arriving):
            peer, idx = _peer(k)
            sem = j * (N_DEV - 1) + k - 1
            return pltpu.make_async_remote_copy(
                src_ref=src_refs[j].at[idx] if per_peer[j] else src_refs[j],
                dst_ref=out_refs[j].at[idx if arriving else me],
                send_sem=send_sems.at[sem], recv_sem=recv_sems.at[sem],
                device_id=peer, device_id_type=pl.DeviceIdType.MESH)

        local = [pltpu.make_async_copy(src_refs[j].at[me] if per_peer[j] else src_refs[j], out_refs[j].at[me],
                                       local_sems.at[j]) for j in range(n)]
        for cp in local:
            cp.start()
        for j, k in pairs:
            copy(j, k, False).start()
        for j, k in pairs:
            copy(j, k, True).wait_recv()
        for j, k in pairs:
            copy(j, k, False).wait_send()
        for cp in local:
            cp.wait()

    hbm = pl.BlockSpec(memory_space=pltpu.HBM)
    return pl.pallas_call(
        body, name=name, in_specs=[hbm] * n, out_specs=[hbm] * n,
        out_shape=[jax.ShapeDtypeStruct((N_DEV,) + shp, s.dtype) for shp, s in zip(shapes, srcs)],
        scratch_shapes=[pltpu.SemaphoreType.DMA((n * (N_DEV - 1),)), pltpu.SemaphoreType.DMA((n * (N_DEV - 1),)),
                        pltpu.SemaphoreType.DMA((n,))])(*srcs)


def _adam_update(g, w, m, v):
    c1, c2 = 1.0 - ADAM_B1 ** ADAM_STEP, 1.0 - ADAM_B2 ** ADAM_STEP
    nm = ADAM_B1 * m + (1.0 - ADAM_B1) * g
    nv = ADAM_B2 * v + (1.0 - ADAM_B2) * (g * g)
    return -ADAM_LR * ((nm / c1) / (jnp.sqrt(nv / c2) + ADAM_EPS) + ADAM_WD * w), nm, nv


def _row_tile(rows, cols):
    padded = -(-cols // LANES) * LANES
    cap = max(8, ADAM_BLOCK_BYTES // (N_DEV * padded * 4))
    best = 8
    for t in range(8, min(rows, cap) + 1, 8):
        if rows % t == 0:
            best = t
    return best


def _adamw(parts, w, m, v, name):
    _, rows, cols = w.shape
    tb = _row_tile(rows, cols)

    def body(p_ref, w_ref, m_ref, v_ref, g_ref, d_ref, nm_ref, nv_ref):
        g = p_ref[0]
        for d in range(1, N_DEV):
            g = g + p_ref[d]
        g_ref[0] = g
        d_ref[0], nm_ref[0], nv_ref[0] = _adam_update(g, w_ref[0], m_ref[0], v_ref[0])

    row = pl.BlockSpec((1, tb, cols), lambda i: (0, i, 0))
    out = jax.ShapeDtypeStruct(w.shape, F32)
    return pl.pallas_call(
        body, name=name, grid=(rows // tb,),
        in_specs=[pl.BlockSpec((N_DEV, tb, cols), lambda i: (0, i, 0)), row, row, row], out_specs=[row] * 4,
        out_shape=[out] * 4, compiler_params=_params(1))(parts, w, m, v)


def _adamw_small(parts, ws, ms, vs, name):
    n = len(ws)

    def body(*refs):
        p_ref = refs[0]
        w_refs, m_refs, v_refs = refs[1:1 + n], refs[1 + n:1 + 2 * n], refs[1 + 2 * n:1 + 3 * n]
        outs = refs[1 + 3 * n:]
        base = 0
        for j in range(n):
            rows, cols = ws[j].shape
            size = rows * cols
            for ch in range(-(-size // LANES)):
                r, c0 = divmod(ch * LANES, cols)
                width = min(LANES, cols - c0)
                g = p_ref[0, base + ch:base + ch + 1, 0:width]
                for d in range(1, N_DEV):
                    g = g + p_ref[d, base + ch:base + ch + 1, 0:width]
                at = (slice(r, r + 1), slice(c0, c0 + width))
                delta, nm, nv = _adam_update(g, w_refs[j][at], m_refs[j][at], v_refs[j][at])
                for out, val in zip((outs[j], outs[n + j], outs[2 * n + j], outs[3 * n + j]), (g, delta, nm, nv)):
                    out[at] = val
            base += -(-size // (8 * LANES)) * 8

    vmem = pl.BlockSpec(memory_space=pltpu.VMEM)
    res = pl.pallas_call(
        body, name=name, in_specs=[vmem] * (1 + 3 * n), out_specs=[vmem] * (4 * n),
        out_shape=[jax.ShapeDtypeStruct(a.shape, F32) for a in ws] * 4)(parts, *ws, *ms, *vs)
    return res[:n], res[n:2 * n], res[2 * n:3 * n], res[3 * n:]


def _rows(a, multiple):
    flat = a.reshape(-1)
    pad = -flat.shape[0] % (multiple * LANES)
    if pad:
        flat = jnp.concatenate([flat, jnp.zeros((pad,), a.dtype)])
    return flat.reshape(-1, LANES)


def _pack(arrs, multiple):
    return jnp.concatenate([_rows(a, multiple) for a in arrs], axis=0)


def _gathered_to_full(g, name, shard_shape):
    g = g.reshape((N_DEV,) + shard_shape)
    if name in COL_SHARDED:
        return jnp.transpose(g, (1, 0, 2)).reshape(shard_shape[0], N_DEV * shard_shape[1])
    return g.reshape(N_DEV * shard_shape[0], shard_shape[1])


def _full_to_per_device(full, name):
    if name in COL_SHARDED:
        r, c = full.shape
        return jnp.transpose(full.reshape(r, N_DEV, c // N_DEV), (1, 0, 2))
    return full.reshape(N_DEV, full.shape[0] // N_DEV, full.shape[1])


def _w2cat(w2, a2, g2):
    n_w, n_a, n_g = LORA
    out = jnp.zeros((256, 3 * WB), w2.dtype)
    out = out.at[0:n_w, 0:WB].set(w2)
    out = out.at[n_w:n_w + n_a, WB:2 * WB].set(a2)
    return out.at[n_w + n_a:n_w + n_a + n_g, 2 * WB:].set(g2)


def _local_step(x, target, w):
    n_w, n_a, n_g = LORA
    w_in_pad = jnp.pad(w["w_in"], ((0, 0), (0, N_INP - N_IN)))
    mu_pad = jnp.pad(w["rwkv_shift_mu"], ((0, 0), (0, 1792 - 1696)))
    w2cat = _w2cat(w["rwkv_w2"], w["rwkv_a2"], w["rwkv_g2"])
    r_k = w["rwkv_r_k"].reshape(1, WB)
    rw = (mu_pad, w2cat, w["rwkv_w0"], w["rwkv_a0"], w["rwkv_k_k"], w["rwkv_k_a"])

    h1 = _rms_fwd(x, w["ffn1_norm"], "ffn1_norm")
    x1 = _ffn_fwd(x, h1, w["ffn1_w_gate"], w["ffn1_w_up"], w["ffn1_w_down"], "ffn1_fwd")
    h2 = _rms_fwd(x1, w["mix_norm"], "mix_norm")
    p_all = _matmul(h2, w_in_pad, name="in_proj")
    oa, oraw, states = _hgrn_fwd(p_all, w["hgrn_lb_logits"], w["hgrn_out_norm"], "hgrn_fwd")
    r, decay, k2, v, sa, sb, g = _rwkv_prep(p_all, *rw, "rwkv_prep")
    y, s_a, sall = _rwkv_scan_fwd(r, decay, k2, v, sa, sb, "rwkv_scan_fwd")
    post_w = (r_k, w["rwkv_gn_w"], w["rwkv_gn_b"])
    ob = _rwkv_post(y, r, k2, v, g, *post_w, "rwkv_post")
    o = jnp.concatenate([oa, ob], axis=1)
    x2 = _matmul(o, w["w_out"], res=x1, name="out_proj")
    h3 = _rms_fwd(x2, w["ffn2_norm"], "ffn2_norm")
    x3 = _ffn_fwd(x2, h3, w["ffn2_w_gate"], w["ffn2_w_up"], w["ffn2_w_down"], "ffn2_fwd")
    loss, dx3, d_final = _loss_head(x3, w["final_norm"].reshape(1, D), target, "loss_head")

    grads = {"final_norm": d_final.reshape(D)}

    def ffn_back(prefix, h, dy, x_in, norm):
        wg, wu, wd = (w[prefix + s] for s in ("_w_gate", "_w_up", "_w_down"))
        dh, act, dgate, dup, dout = _ffn_bwd(h, dy, wg, wu, wd, prefix + "_bwd")
        grads[prefix + "_w_gate"] = _matmul(h, dgate, ta=True, name=prefix + "_dwg")
        grads[prefix + "_w_up"] = _matmul(h, dup, ta=True, name=prefix + "_dwu")
        grads[prefix + "_w_down"] = _matmul(act, dout, ta=True, name=prefix + "_dwd")
        dx, grads[prefix + "_norm"] = _rms_bwd(x_in, norm, dh, dy, prefix + "_norm_bwd")
        return dx

    dx2 = ffn_back("ffn2", h3, dx3, x2, w["ffn2_norm"])
    grads["w_out"] = _matmul(o, dx2, ta=True, name="d_w_out")
    do = _matmul(dx2, w["w_out"], tb=True, name="d_mixed")
    dqa, dfa, dia, dga, grads["hgrn_out_norm"], grads["hgrn_lb_logits"] = _hgrn_bwd(
        p_all, w["hgrn_lb_logits"], w["hgrn_out_norm"], oraw, states, do[:, :WA], "hgrn_bwd")
    dy, dg, dr_b, dk2_b, dv_b, grads["rwkv_gn_w"], grads["rwkv_gn_b"], d_rk = _rwkv_post_bwd(
        do[:, WA:], y, r, k2, v, g, *post_w, "rwkv_post_bwd")
    grads["rwkv_r_k"] = d_rk.reshape(w["rwkv_r_k"].shape)
    dr, dw, dk2, dv, dsa, dsb = _rwkv_scan_bwd(dy, r, decay, k2, v, sa, sb, s_a, sall, "rwkv_scan_bwd")
    (dsr, dsk, dsv, dslo, dw2cat, grads["rwkv_w0"], grads["rwkv_a0"], grads["rwkv_k_k"],
     grads["rwkv_k_a"]) = _rwkv_prep_bwd((dr, dw, dk2, dv, dsa, dsb, dg, dr_b, dk2_b, dv_b), p_all, *rw,
                                         "rwkv_prep_bwd")
    grads["rwkv_w2"] = dw2cat[0:n_w, 0:WB]
    grads["rwkv_a2"] = dw2cat[n_w:n_w + n_a, WB:2 * WB]
    grads["rwkv_g2"] = dw2cat[n_w + n_a:n_w + n_a + n_g, 2 * WB:]
    dpr, dpk, dpv, dplo, dmu_r, dmu_k, dmu_v, dmu_lo = _shift_bwd((dsr, dsk, dsv, dslo), p_all, mu_pad, "shift_bwd")
    grads["rwkv_shift_mu"] = jnp.concatenate([dmu_r, dmu_k, dmu_v, dmu_lo], axis=1)[:, :1696]
    dp = jnp.concatenate([dqa, dfa, dia, dga, dpr, dpk, dpv, dplo], axis=1)
    grads["w_in"] = _matmul(h2, dp, ta=True, name="d_w_in")[:, :N_IN]
    dh2 = _matmul(dp, w_in_pad, tb=True, name="d_h2")
    dx1, grads["mix_norm"] = _rms_bwd(x1, w["mix_norm"], dh2, dx2, "mix_norm_bwd")
    dx0 = ffn_back("ffn1", h1, dx1, x, w["ffn1_norm"])
    return loss[0, 0], dx0, grads


def kernel(x, ffn1_norm, ffn1_w_gate, ffn1_w_up, ffn1_w_down, mix_norm, w_in, hgrn_lb_logits, hgrn_out_norm, rwkv_shift_mu, rwkv_w0, rwkv_w2, rwkv_a0, rwkv_a2, rwkv_g2, rwkv_k_k, rwkv_k_a, rwkv_r_k, rwkv_gn_w, rwkv_gn_b, w_out, ffn2_norm, ffn2_w_gate, ffn2_w_up, ffn2_w_down, final_norm, loss_target, m_ffn1_norm, m_ffn1_w_gate, m_ffn1_w_up, m_ffn1_w_down, m_mix_norm, m_w_in, m_hgrn_lb_logits, m_hgrn_out_norm, m_rwkv_shift_mu, m_rwkv_w0, m_rwkv_w2, m_rwkv_a0, m_rwkv_a2, m_rwkv_g2, m_rwkv_k_k, m_rwkv_k_a, m_rwkv_r_k, m_rwkv_gn_w, m_rwkv_gn_b, m_w_out, m_ffn2_norm, m_ffn2_w_gate, m_ffn2_w_up, m_ffn2_w_down, m_final_norm, v_ffn1_norm, v_ffn1_w_gate, v_ffn1_w_up, v_ffn1_w_down, v_mix_norm, v_w_in, v_hgrn_lb_logits, v_hgrn_out_norm, v_rwkv_shift_mu, v_rwkv_w0, v_rwkv_w2, v_rwkv_a0, v_rwkv_a2, v_rwkv_g2, v_rwkv_k_k, v_rwkv_k_a, v_rwkv_r_k, v_rwkv_gn_w, v_rwkv_gn_b, v_w_out, v_ffn2_norm, v_ffn2_w_gate, v_ffn2_w_up, v_ffn2_w_down, v_final_norm):
    args = dict(locals())
    wts = {n: args[n] for n in WEIGHTS}
    mom = {n: args["m_" + n] for n in WEIGHTS}
    var = {n: args["v_" + n] for n in WEIGHTS}
    shard_shapes = {n: wts[n].shape[1:] for n in SHARDED}

    gathered = _exchange([wts[n].astype(MXU) for n in SHARDED], [False] * len(SHARDED), "gather_weights")
    full = {n: _gathered_to_full(g, n, shard_shapes[n]) for n, g in zip(SHARDED, gathered)}
    for n in SMALL:
        full[n] = wts[n] if n in ("hgrn_lb_logits", "rwkv_r_k", "final_norm") else wts[n].reshape(1, -1)

    loss, grad_x, grads = _local_step(x[0], loss_target[0], full)
    loss = lax.psum(loss, AXES)

    contrib = [_full_to_per_device(grads[n], n) for n in SHARDED] + [_pack([grads[n] for n in SMALL], 8)]
    parts = _exchange(contrib, [True] * len(SHARDED) + [False], "scatter_grads")

    new = {}
    for n, p in zip(SHARDED, parts):
        new[n] = _adamw(p, wts[n], mom[n], var[n], "adamw_" + n)
    two_d = lambda a: a if a.ndim == 2 else a.reshape(1, -1)
    small = _adamw_small(parts[-1], *([two_d(src[n]) for n in SMALL] for src in (wts, mom, var)), "adamw_small")
    for j, n in enumerate(SMALL):
        new[n] = [res[j].reshape(wts[n].shape) for res in small]
    return (loss, grad_x[None], *[new[n][0] for n in WEIGHTS], *[new[n][1] for n in WEIGHTS],
            *[new[n][2] for n in WEIGHTS], *[new[n][3] for n in WEIGHTS])
```

```python
import functools
import math

import jax
import jax.numpy as jnp
from jax import lax
from jax.experimental import pallas as pl
from jax.experimental.pallas import tpu as pltpu

F32 = jnp.float32
MXU = jnp.bfloat16
WIRE = jnp.bfloat16
D = 1024
FF = 2816
WA = 512
WB = 512
HD_B = 64
N_IN = 3744
N_INP = 3840
COL_R, COL_K, COL_V = 4, 5, 6
COL_L = 14
LORA = (32, 32, 96)
HG_CHUNK = 64
SCAN_CHUNK = 64
NORM_EPS = 1e-6
GN_EPS = 64e-5
L2_EPS = 1e-12
DECAY_C = math.exp(-0.5)
N_DEV = 8
LANES = 128
ADAM_BLOCK_BYTES = 4 * 1024 * 1024
MATMUL_BLOCK_BYTES = 40 * 1024 * 1024
VMEM_LIMIT = 56 * 1024 * 1024
ADAM_LR, ADAM_B1, ADAM_B2, ADAM_EPS, ADAM_WD, ADAM_STEP = 0.001, 0.9, 0.999, 1e-08, 0.01, 10
AXES = ("x", "y", "c")

SHARDED = ("ffn1_w_gate", "ffn1_w_up", "ffn1_w_down", "w_in", "rwkv_w2", "rwkv_a2", "rwkv_g2", "w_out",
           "ffn2_w_gate", "ffn2_w_up", "ffn2_w_down")
COL_SHARDED = {"ffn1_w_gate", "ffn1_w_up", "w_in", "rwkv_w2", "rwkv_a2", "rwkv_g2", "ffn2_w_gate", "ffn2_w_up"}
SMALL = ("ffn1_norm", "mix_norm", "hgrn_lb_logits", "hgrn_out_norm", "rwkv_shift_mu", "rwkv_w0", "rwkv_a0",
         "rwkv_k_k", "rwkv_k_a", "rwkv_r_k", "rwkv_gn_w", "rwkv_gn_b", "ffn2_norm", "final_norm")
WEIGHTS = ("ffn1_norm", "ffn1_w_gate", "ffn1_w_up", "ffn1_w_down", "mix_norm", "w_in", "hgrn_lb_logits",
           "hgrn_out_norm", "rwkv_shift_mu", "rwkv_w0", "rwkv_w2", "rwkv_a0", "rwkv_a2", "rwkv_g2", "rwkv_k_k",
           "rwkv_k_a", "rwkv_r_k", "rwkv_gn_w", "rwkv_gn_b", "w_out", "ffn2_norm", "ffn2_w_gate", "ffn2_w_up",
           "ffn2_w_down", "final_norm")


def _tile(n, cap):
    if n <= cap:
        return n
    for t in range(cap - cap % LANES, 0, -LANES):
        if n % t == 0:
            return t
    raise ValueError((n, cap))


def _params(n_axes):
    return pltpu.CompilerParams(dimension_semantics=("arbitrary",) * n_axes, vmem_limit_bytes=VMEM_LIMIT)


def _sig(x):
    return jax.nn.sigmoid(x)


def _dsilu(z, s):
    return s * (1.0 + z * (1.0 - s))


def _dot(a, b, dims=((1,), (0,)), precision=None):
    return lax.dot_general(a, b, (dims, ((), ())), preferred_element_type=F32, precision=precision)


_NT = ((1,), (1,))
_TN = ((0,), (0,))
_HI = lax.Precision.HIGHEST


def _iota(shape, dim):
    return lax.broadcasted_iota(jnp.int32, shape, dim)


def _split_dot(x, ones, passes):
    hi = x.astype(jnp.bfloat16)
    acc = _dot(hi, ones)
    rem = x
    for _ in range(passes - 1):
        rem = rem - hi.astype(F32)
        hi = rem.astype(jnp.bfloat16)
        acc = acc + _dot(hi, ones)
    return acc


def _head_ones(n, width):
    shift = width.bit_length() - 1
    return (_iota((n, n), 0) >> shift == _iota((n, n), 1) >> shift).astype(jnp.bfloat16)


def _matmul(a, b, *, ta=False, tb=False, out_dtype=F32, res=None, name):
    m, k = (a.shape[1], a.shape[0]) if ta else a.shape
    n = b.shape[0] if tb else b.shape[1]
    tm, tn = _tile(m, 1408), _tile(n, 1408)
    in_bytes = max(a.dtype.itemsize, b.dtype.itemsize)
    for tk in (_tile(k, 1024), _tile(k, 512), _tile(k, 256)):
        if 2 * (tm + tn) * tk * in_bytes + 3 * tm * tn * 4 <= MATMUL_BLOCK_BYTES:
            break
    nk = k // tk
    dims = ((0 if ta else 1,), (1 if tb else 0,))

    def body(*refs):
        a_ref, b_ref = refs[:2]
        o_ref, acc = refs[-2:]
        kk = pl.program_id(2)

        @pl.when(kk == 0)
        def _():
            acc[...] = jnp.zeros_like(acc)

        acc[...] += _dot(a_ref[...].astype(MXU), b_ref[...].astype(MXU), dims)

        @pl.when(kk == nk - 1)
        def _():
            v = acc[...]
            if res is not None:
                v = v + refs[2][...]
            o_ref[...] = v.astype(out_dtype)

    a_spec = pl.BlockSpec((tk, tm), lambda i, j, kk: (kk, i)) if ta else pl.BlockSpec((tm, tk), lambda i, j, kk: (i, kk))
    b_spec = pl.BlockSpec((tn, tk), lambda i, j, kk: (j, kk)) if tb else pl.BlockSpec((tk, tn), lambda i, j, kk: (kk, j))
    o_spec = pl.BlockSpec((tm, tn), lambda i, j, kk: (i, j))
    ins, specs = [a, b], [a_spec, b_spec]
    if res is not None:
        ins.append(res)
        specs.append(o_spec)
    return pl.pallas_call(
        body, name=name, grid=(m // tm, n // tn, nk), in_specs=specs, out_specs=o_spec,
        out_shape=jax.ShapeDtypeStruct((m, n), out_dtype), scratch_shapes=[pltpu.VMEM((tm, tn), F32)],
        compiler_params=_params(3))(*ins)


def _rms_fwd(x, g, name):
    t = x.shape[0]
    tb = _tile(t, 512)

    def body(x_ref, g_ref, o_ref):
        xv = x_ref[...]
        rinv = lax.rsqrt(jnp.mean(xv * xv, axis=-1, keepdims=True) + NORM_EPS)
        o_ref[...] = (xv * rinv * g_ref[...]).astype(MXU)

    return pl.pallas_call(
        body, name=name, grid=(t // tb,),
        in_specs=[pl.BlockSpec((tb, D), lambda i: (i, 0)), pl.BlockSpec((1, D), lambda i: (0, 0))],
        out_specs=pl.BlockSpec((tb, D), lambda i: (i, 0)), out_shape=jax.ShapeDtypeStruct((t, D), MXU),
        compiler_params=_params(1))(x, g)


def _rms_bwd(x, g, dh, dres, name):
    t = x.shape[0]
    tb = _tile(t, 512)

    def body(x_ref, g_ref, dh_ref, dres_ref, dx_ref, dg_ref):
        @pl.when(pl.program_id(0) == 0)
        def _():
            dg_ref[...] = jnp.zeros_like(dg_ref)

        xv = x_ref[...]
        rinv = lax.rsqrt(jnp.mean(xv * xv, axis=-1, keepdims=True) + NORM_EPS)
        xhat = xv * rinv
        dhv = dh_ref[...]
        dg_ref[...] += jnp.sum(dhv * xhat, axis=0, keepdims=True)
        dxhat = dhv * g_ref[...]
        dx_ref[...] = dres_ref[...] + rinv * (dxhat - xhat * jnp.mean(dxhat * xhat, axis=-1, keepdims=True))

    row = pl.BlockSpec((tb, D), lambda i: (i, 0))
    vec = pl.BlockSpec((1, D), lambda i: (0, 0))
    return pl.pallas_call(
        body, name=name, grid=(t // tb,), in_specs=[row, vec, row, row], out_specs=[row, vec],
        out_shape=[jax.ShapeDtypeStruct((t, D), F32), jax.ShapeDtypeStruct((1, D), F32)],
        compiler_params=_params(1))(x, g, dh, dres)


def _loss_head(x, g, target, name):
    t = x.shape[0]
    tb = _tile(t, 512)

    def body(x_ref, g_ref, t_ref, loss_ref, dx_ref, dg_ref):
        @pl.when(pl.program_id(0) == 0)
        def _():
            dg_ref[...] = jnp.zeros_like(dg_ref)
            loss_ref[...] = jnp.zeros_like(loss_ref)

        xv = x_ref[...]
        gv = g_ref[...]
        rinv = lax.rsqrt(jnp.mean(xv * xv, axis=-1, keepdims=True) + NORM_EPS)
        xhat = xv * rinv
        err = xhat * gv - t_ref[...]
        per_tok = jnp.mean(err * err, axis=-1, keepdims=True)
        loss_ref[...] += jnp.broadcast_to(0.5 * jnp.sum(per_tok, axis=0, keepdims=True), loss_ref.shape)
        dy = err * (1.0 / D)
        dg_ref[...] += jnp.sum(dy * xhat, axis=0, keepdims=True)
        dxhat = dy * gv
        dx_ref[...] = rinv * (dxhat - xhat * jnp.mean(dxhat * xhat, axis=-1, keepdims=True))

    row = pl.BlockSpec((tb, D), lambda i: (i, 0))
    vec = pl.BlockSpec((1, D), lambda i: (0, 0))
    return pl.pallas_call(
        body, name=name, grid=(t // tb,), in_specs=[row, vec, row],
        out_specs=[pl.BlockSpec((1, LANES), lambda i: (0, 0)), row, vec],
        out_shape=[jax.ShapeDtypeStruct((1, LANES), F32), jax.ShapeDtypeStruct((t, D), F32),
                   jax.ShapeDtypeStruct((1, D), F32)],
        compiler_params=_params(1))(x, g, target)


def _ffn_fwd(x, h, wg, wu, wd, name):
    t = x.shape[0]
    tb, fb = _tile(t, 1024), 256
    nf = FF // fb

    def body(x_ref, h_ref, wg_ref, wu_ref, wd_ref, o_ref, acc):
        f = pl.program_id(1)

        @pl.when(f == 0)
        def _():
            acc[...] = jnp.zeros_like(acc)

        hv = h_ref[...]
        gate = _dot(hv, wg_ref[...])
        up = _dot(hv, wu_ref[...])
        act = (gate * _sig(gate) * up).astype(MXU)
        acc[...] += _dot(act, wd_ref[...])

        @pl.when(f == nf - 1)
        def _():
            o_ref[...] = x_ref[...] + 0.5 * acc[...]

    row = pl.BlockSpec((tb, D), lambda i, f: (i, 0))
    col = pl.BlockSpec((D, fb), lambda i, f: (0, f))
    return pl.pallas_call(
        body, name=name, grid=(t // tb, nf),
        in_specs=[row, row, col, col, pl.BlockSpec((fb, D), lambda i, f: (f, 0))], out_specs=row,
        out_shape=jax.ShapeDtypeStruct((t, D), F32), scratch_shapes=[pltpu.VMEM((tb, D), F32)],
        compiler_params=_params(2))(x, h, wg, wu, wd)


def _ffn_bwd(h, dy, wg, wu, wd, name):
    t = h.shape[0]
    tb, fb = _tile(t, 1024), 256
    nf = FF // fb

    def body(h_ref, dy_ref, wg_ref, wu_ref, wd_ref, dh_ref, act_ref, dg_ref, du_ref, dout_ref, acc):
        f = pl.program_id(1)

        @pl.when(f == 0)
        def _():
            acc[...] = jnp.zeros_like(acc)

        hv = h_ref[...]
        dout = (0.5 * dy_ref[...]).astype(MXU)
        dout_ref[...] = dout
        gate = _dot(hv, wg_ref[...])
        up = _dot(hv, wu_ref[...])
        dact = _dot(dout, wd_ref[...], _NT)
        s = _sig(gate)
        silu = gate * s
        act_ref[...] = (silu * up).astype(MXU)
        dup = (dact * silu).astype(MXU)
        dgate = (dact * up * _dsilu(gate, s)).astype(MXU)
        du_ref[...] = dup
        dg_ref[...] = dgate
        acc[...] += _dot(dgate, wg_ref[...], _NT) + _dot(dup, wu_ref[...], _NT)

        @pl.when(f == nf - 1)
        def _():
            dh_ref[...] = acc[...]

    row = pl.BlockSpec((tb, D), lambda i, f: (i, 0))
    col = pl.BlockSpec((D, fb), lambda i, f: (0, f))
    hid = pl.BlockSpec((tb, fb), lambda i, f: (i, f))
    hid_shape = jax.ShapeDtypeStruct((t, FF), MXU)
    return pl.pallas_call(
        body, name=name, grid=(t // tb, nf),
        in_specs=[row, row, col, col, pl.BlockSpec((fb, D), lambda i, f: (f, 0))],
        out_specs=[row, hid, hid, hid, row],
        out_shape=[jax.ShapeDtypeStruct((t, D), F32), hid_shape, hid_shape, hid_shape,
                   jax.ShapeDtypeStruct((t, D), MXU)],
        scratch_shapes=[pltpu.VMEM((tb, D), F32)], compiler_params=_params(2))(h, dy, wg, wu, wd)


def _hgrn_chunk(qa, fa, lbl):
    c = HG_CHUNK
    lb = _sig(lbl[0:1, :] - lbl[1:2, :])
    sf = _sig(fa)
    forget = lb + (1.0 - lb) * sf
    kh = 1.0 - forget
    row, col = _iota((c, c), 0), _iota((c, c), 1)
    b = _dot((col <= row).astype(F32), jnp.log(forget), precision=_HI)
    bref, blast = b[c // 2:c // 2 + 1, :], b[c - 1:c, :]
    sq = _sig(qa)
    q = qa * sq
    qt, kt = q * jnp.exp(b - bref), kh * jnp.exp(bref - b)
    qb, kl = q * jnp.exp(b), kh * jnp.exp(blast - b)
    causal = col <= row
    return dict(lb=lb, sf=sf, forget=forget, sq=sq, qt=qt, kt=kt, qb=qb, kl=kl, decay=jnp.exp(blast),
                causal=causal, e_q=jnp.exp(b), e_qt=jnp.exp(b - bref), e_kt=jnp.exp(bref - b),
                e_kl=jnp.exp(blast - b))


def _hgrn_specs(t):
    c = HG_CHUNK
    return c, t // c, WA // LANES


def _hgrn_fwd(p_all, lbl, onorm, name):
    t = p_all.shape[0]
    c, n, nh = _hgrn_specs(t)

    def body(q_ref, f_ref, i_ref, g_ref, lbl_ref, on_ref, oa_ref, oraw_ref, st_ref, state):
        @pl.when(pl.program_id(0) == 0)
        def _():
            state[...] = jnp.zeros_like(state)

        for h in range(nh):
            at = slice(h * LANES, (h + 1) * LANES)
            k = _hgrn_chunk(q_ref[:, at], f_ref[:, at], lbl_ref[:, at])
            v = i_ref[:, at]
            st = state[h]
            st_ref[h, 0] = st
            a = jnp.where(k["causal"], _dot(k["qt"], k["kt"], _NT, _HI), 0.0)
            o = _dot(a, v, precision=_HI) + _dot(k["qb"], st, _NT, _HI)
            state[h] = st * k["decay"] + _dot(v, k["kl"], _TN, _HI)
            oraw_ref[:, at] = o
            rinv = lax.rsqrt(jnp.mean(o * o, axis=-1, keepdims=True) + NORM_EPS)
            ga = g_ref[:, at]
            oa_ref[:, at] = (o * rinv * on_ref[:, at] * (ga * _sig(ga))).astype(MXU)

    def blk(j):
        return pl.BlockSpec((c, WA), lambda i: (i, j))

    return pl.pallas_call(
        body, name=name, grid=(n,),
        in_specs=[blk(0), blk(1), blk(2), blk(3), pl.BlockSpec((2, WA), lambda i: (0, 0)),
                  pl.BlockSpec((1, WA), lambda i: (0, 0))],
        out_specs=[blk(0), blk(0), pl.BlockSpec((nh, 1, LANES, LANES), lambda i: (0, i, 0, 0))],
        out_shape=[jax.ShapeDtypeStruct((t, WA), MXU), jax.ShapeDtypeStruct((t, WA), F32),
                   jax.ShapeDtypeStruct((nh, n, LANES, LANES), F32)],
        scratch_shapes=[pltpu.VMEM((nh, LANES, LANES), F32)], compiler_params=_params(1))(
            p_all, p_all, p_all, p_all, lbl, onorm)


def _hgrn_bwd(p_all, lbl, onorm, oraw, states, doa, name):
    t = p_all.shape[0]
    c, n, nh = _hgrn_specs(t)

    def body(q_ref, f_ref, i_ref, g_ref, lbl_ref, on_ref, oraw_ref, st_ref, doa_ref,
             dq_ref, df_ref, di_ref, dg_ref, don_ref, dlbl_ref, dstate, dlb):
        @pl.when(pl.program_id(0) == 0)
        def _():
            dstate[...] = jnp.zeros_like(dstate)
            dlb[...] = jnp.zeros_like(dlb)
            don_ref[...] = jnp.zeros_like(don_ref)

        for h in range(nh):
            at = slice(h * LANES, (h + 1) * LANES)
            qa, fa, v, ga = q_ref[:, at], f_ref[:, at], i_ref[:, at], g_ref[:, at]
            k = _hgrn_chunk(qa, fa, lbl_ref[:, at])
            st, dst_next = st_ref[h, 0], dstate[h]
            o = oraw_ref[:, at]
            gain = on_ref[:, at]
            rinv = lax.rsqrt(jnp.mean(o * o, axis=-1, keepdims=True) + NORM_EPS)
            on = o * rinv
            sg = _sig(ga)
            gate = ga * sg
            dout = doa_ref[:, at]
            don_ref[:, at] += jnp.sum(dout * on * gate, axis=0, keepdims=True)
            dg_ref[:, at] = (dout * on * gain * _dsilu(ga, sg)).astype(MXU)
            d_on = dout * gain * gate
            do = rinv * (d_on - on * jnp.mean(d_on * on, axis=-1, keepdims=True))

            a = jnp.where(k["causal"], _dot(k["qt"], k["kt"], _NT, _HI), 0.0)
            dqb = _dot(do, st, precision=_HI)
            dstate[h] = dst_next * k["decay"] + _dot(do, k["qb"], _TN, _HI)
            da = jnp.where(k["causal"], _dot(do, v, _NT, _HI), 0.0)
            dqt = _dot(da, k["kt"], precision=_HI)
            dkt = _dot(da, k["qt"], _TN, _HI)
            dv = _dot(a, do, _TN, _HI) + _dot(k["kl"], dst_next, _NT, _HI)
            dkl = _dot(v, dst_next, precision=_HI)
            ddecay = jnp.sum(dst_next * st, axis=0, keepdims=True)
            dq = dqb * k["e_q"] + dqt * k["e_qt"]
            dk = dkt * k["e_kt"] + dkl * k["e_kl"]
            tq, tk, tl = dqt * k["qt"], dkt * k["kt"], dkl * k["kl"]
            db = dqb * k["qb"] + tq - tk - tl
            dbref = jnp.sum(tk - tq, axis=0, keepdims=True)
            dblast = jnp.sum(tl, axis=0, keepdims=True) + ddecay * k["decay"]
            rows = _iota((c, LANES), 0)
            db = db + jnp.where(rows == c // 2, dbref, 0.0) + jnp.where(rows == c - 1, dblast, 0.0)
            row, col = _iota((c, c), 0), _iota((c, c), 1)
            dlogf = _dot((col >= row).astype(F32), db, precision=_HI)
            dq_ref[:, at] = (dq * _dsilu(qa, k["sq"])).astype(MXU)
            di_ref[:, at] = dv.astype(MXU)
            dforget = dlogf / k["forget"] - dk
            sf, lb = k["sf"], k["lb"]
            df_ref[:, at] = (dforget * (1.0 - lb) * sf * (1.0 - sf)).astype(MXU)
            dlb[:, at] += jnp.sum(dforget * (1.0 - sf), axis=0, keepdims=True)
            dl0 = dlb[:, at] * lb * (1.0 - lb)
            dlbl_ref[:, at] = jnp.where(_iota((2, LANES), 0) == 0, dl0, -dl0)

    def blk(j):
        return pl.BlockSpec((c, WA), lambda i: (n - 1 - i, j))

    vec = pl.BlockSpec((1, WA), lambda i: (0, 0))
    lg = pl.BlockSpec((2, WA), lambda i: (0, 0))
    grad = jax.ShapeDtypeStruct((t, WA), MXU)
    return pl.pallas_call(
        body, name=name, grid=(n,),
        in_specs=[blk(0), blk(1), blk(2), blk(3), lg, vec, blk(0),
                  pl.BlockSpec((nh, 1, LANES, LANES), lambda i: (0, n - 1 - i, 0, 0)), blk(0)],
        out_specs=[blk(0), blk(0), blk(0), blk(0), vec, lg],
        out_shape=[grad, grad, grad, grad, jax.ShapeDtypeStruct((1, WA), F32), jax.ShapeDtypeStruct((2, WA), F32)],
        scratch_shapes=[pltpu.VMEM((nh, LANES, LANES), F32), pltpu.VMEM((1, WA), F32)],
        compiler_params=_params(1))(p_all, p_all, p_all, p_all, lbl, onorm, oraw, states, doa)


def _lora_act(x):
    lane = _iota(x.shape, 1)
    n_w, n_a, n_g = LORA
    return jnp.where(lane < n_w, jnp.tanh(x),
                     jnp.where(lane < n_w + n_a, x, jnp.where(lane < n_w + n_a + n_g, _sig(x), 0.0)))


def _lora_dact(x):
    lane = _iota(x.shape, 1)
    n_w, n_a, n_g = LORA
    th, s = jnp.tanh(x), _sig(x)
    return jnp.where(lane < n_w, 1.0 - th * th,
                     jnp.where(lane < n_w + n_a, 1.0, jnp.where(lane < n_w + n_a + n_g, s * (1.0 - s), 0.0)))


def _shift_down(cur, prev8, first):
    rolled = pltpu.roll(cur, 1, 0)
    edge = prev8[7:8, :] * jnp.where(first, 0.0, 1.0)
    return jnp.where(_iota(cur.shape, 0) == 0, edge, rolled)


def _shift_up(cur, next8, last):
    rows = cur.shape[0]
    rolled = pltpu.roll(cur, rows - 1, 0)
    edge = next8[0:1, :] * jnp.where(last, 0.0, 1.0)
    return jnp.where(_iota(cur.shape, 0) == rows - 1, edge, rolled)


def _rwkv_inputs(refs, first, ones):
    (pr, pk, pv, plo, qr, qk, qv, qlo, mr, mk, mv, mlo, w2c, w0, a0, kk_w, ka_w) = refs
    mix = lambda cur, prev, mu: cur[...] + mu[...] * (_shift_down(cur[...], prev[...], first) - cur[...])
    r, k, v, lo = mix(pr, qr, mr), mix(pk, qk, mk), mix(pv, qv, mv), mix(plo, qlo, mlo)
    z = _lora_act(lo)
    lin = _dot(z.astype(MXU), w2c[...])
    sg = _sig(w0[...] + lin[:, :WB])
    decay = jnp.exp(-DECAY_C * sg)
    a = _sig(a0[...] + lin[:, WB:2 * WB])
    g = lin[:, 2 * WB:]
    kk0 = k * kk_w[...]
    nrm = jnp.sqrt(_split_dot(kk0 * kk0, ones, 3))
    den = jnp.maximum(nrm, L2_EPS)
    kk = kk0 / den
    k2 = k * (1.0 + (a - 1.0) * ka_w[...])
    return dict(r=r, k=k, v=v, lo=lo, z=z, sg=sg, decay=decay, a=a, g=g, kk=kk, den=den, nrm=nrm, k2=k2)


def _rwkv_in_specs(t, tb):
    nt8 = tb // 8

    def cur(w, j):
        return pl.BlockSpec((tb, w), lambda i: (i, j))

    def prev(w, j):
        return pl.BlockSpec((8, w), lambda i: (jnp.maximum(i * nt8 - 1, 0), j))

    def vec(w, j=0):
        return pl.BlockSpec((1, w), lambda i: (0, j))

    return [cur(WB, COL_R), cur(WB, COL_K), cur(WB, COL_V), cur(256, COL_L),
            prev(WB, COL_R), prev(WB, COL_K), prev(WB, COL_V), prev(256, COL_L),
            vec(WB, 0), vec(WB, 1), vec(WB, 2), vec(256, 6),
            pl.BlockSpec((256, 3 * WB), lambda i: (0, 0)), vec(WB), vec(WB), vec(WB), vec(WB)]


def _rwkv_in_args(p_all, mu_pad, w2cat, w0, a0, k_k, k_a):
    return (p_all,) * 8 + (mu_pad,) * 4 + (w2cat, w0, a0, k_k, k_a)


def _rwkv_prep(p_all, mu_pad, w2cat, w0, a0, k_k, k_a, name):
    t = p_all.shape[0]
    tb = _tile(t, 256)

    def body(*refs):
        ins, outs = refs[:17], refs[17:]
        q = _rwkv_inputs(ins, pl.program_id(0) == 0, _head_ones(WB, HD_B))
        for ref, val in zip(outs, (q["r"], q["decay"], q["k2"], q["v"], -q["kk"], q["kk"] * q["a"], q["g"])):
            ref[...] = val

    out = pl.BlockSpec((tb, WB), lambda i: (i, 0))
    return pl.pallas_call(
        body, name=name, grid=(t // tb,), in_specs=_rwkv_in_specs(t, tb), out_specs=[out] * 7,
        out_shape=[jax.ShapeDtypeStruct((t, WB), F32)] * 7, compiler_params=_params(1))(
            *_rwkv_in_args(p_all, mu_pad, w2cat, w0, a0, k_k, k_a))


def _pair_rows(x8, i):
    return jnp.concatenate([jnp.broadcast_to(x8[i:i + 1, p * LANES:(p + 1) * LANES], (HD_B, LANES))
                            for p in range(4)], axis=0)


def _pair_sums(x):
    return jnp.concatenate([jnp.sum(x[p * HD_B:(p + 1) * HD_B], axis=0, keepdims=True) for p in range(4)], axis=1)


def _put_row(buf, i, row):
    return jnp.where(_iota(buf.shape, 0) == i, row, buf)


def _pieces(x):
    hi = x.astype(jnp.bfloat16).astype(F32)
    lo = (x - hi).astype(jnp.bfloat16).astype(F32)
    upper = (_iota((x.shape[0], LANES), 1) & (HD_B // 2)) != 0
    swapped = [jnp.where(upper, pltpu.roll(lo[:, p * LANES:(p + 1) * LANES], HD_B // 2, 1),
                         pltpu.roll(lo[:, p * LANES:(p + 1) * LANES], LANES - HD_B // 2, 1)) for p in range(4)]
    return hi, jnp.concatenate(swapped, axis=1)


def _scan_consts():
    row, lane = _iota((HD_B, LANES), 0), _iota((HD_B, LANES), 1) & (HD_B - 1)
    return (row == lane).astype(jnp.bfloat16), (row == lane ^ (HD_B // 2)).astype(jnp.bfloat16), _head_ones(LANES, HD_B)


def _pair_cols(many, consts):
    diag_hi, diag_lo, ones = consts
    tiles = []
    for (hi8, lo8), i in many:
        for p in range(4):
            lanes = slice(p * LANES, (p + 1) * LANES)
            hi = jnp.broadcast_to(hi8[i:i + 1, lanes], (16, LANES)).astype(jnp.bfloat16)
            lo = jnp.broadcast_to(lo8[i:i + 1, lanes], (16, LANES)).astype(jnp.bfloat16)
            for g in range(HD_B // 16):
                rows = slice(g * 16, (g + 1) * 16)
                tiles.append(hi * diag_hi[rows] + lo * diag_lo[rows])
    out = _dot(jnp.concatenate(tiles, axis=0), ones)
    return [out[m * 4 * HD_B:(m + 1) * 4 * HD_B] for m in range(len(many))]


def _rwkv_scan_fwd(r, w, k, v, a, b, name):
    t = r.shape[0]
    cc = min(t, SCAN_CHUNK)

    def body(r_ref, w_ref, k_ref, v_ref, a_ref, b_ref, y_ref, sa_ref, sall_ref, state_k, state_v):
        @pl.when(pl.program_id(0) == 0)
        def _():
            state_k[...] = jnp.zeros_like(state_k)
            state_v[...] = jnp.zeros_like(state_v)

        consts = _scan_consts()

        def block(j, carry):
            sk, sv = carry
            base = pl.multiple_of(j * 8, 8)
            rows = pl.ds(base, 8)
            r8, w8, k8, v8, a8, b8 = (ref[rows, :] for ref in (r_ref, w_ref, k_ref, v_ref, a_ref, b_ref))
            rp, wp, kp, vp, ap, bp = (_pieces(x) for x in (r8, w8, k8, v8, a8, b8))
            y8 = jnp.zeros((8, WB), F32)
            sa8 = jnp.zeros((8, WB), F32)
            cols = _pair_cols([(x, i) for i in range(8) for x in (ap, wp, bp, kp, rp, vp)], consts)
            for i in range(8):
                a_c, w_c, b_c, k_c, r_c = cols[6 * i:6 * i + 5]
                sa = _pair_sums(sk * a_c)
                sk = sk * w_c + b_c * _pair_rows(sa, 0) + k_c * _pair_rows(v8, i)
                y8 = _put_row(y8, i, _pair_sums(sk * r_c))
                sa8 = _put_row(sa8, i, sa)
            sa_cols = _pair_cols([(_pieces(sa8), i) for i in range(8)], consts)
            for i in range(8):
                sv = sv * _pair_rows(w8, i) + sa_cols[i] * _pair_rows(b8, i) + cols[6 * i + 5] * _pair_rows(k8, i)
                sall_ref[base + i] = sv
            y_ref[rows, :] = y8
            sa_ref[rows, :] = sa8
            return sk, sv

        state_k[...], state_v[...] = lax.fori_loop(0, cc // 8, block, (state_k[...], state_v[...]))

    row = pl.BlockSpec((cc, WB), lambda i: (i, 0))
    tile = pltpu.VMEM((4 * HD_B, LANES), F32)
    return pl.pallas_call(
        body, name=name, grid=(t // cc,), in_specs=[row] * 6,
        out_specs=[row, row, pl.BlockSpec((cc, 4 * HD_B, LANES), lambda i: (i, 0, 0))],
        out_shape=[jax.ShapeDtypeStruct((t, WB), F32)] * 2 + [jax.ShapeDtypeStruct((t, 4 * HD_B, LANES), F32)],
        scratch_shapes=[tile, tile], compiler_params=_params(1))(r, w, k, v, a, b)


def _rwkv_scan_bwd(dy, r, w, k, v, a, b, sa, sall, name):
    t = r.shape[0]
    cc = min(t, SCAN_CHUNK)
    n = t // cc

    def body(dy_ref, r_ref, w_ref, k_ref, v_ref, a_ref, b_ref, sa_ref, sall_ref, sprev_ref,
             dr_ref, dw_ref, dk_ref, dv_ref, da_ref, db_ref, dstate_k, dstate_v):
        @pl.when(pl.program_id(0) == 0)
        def _():
            dstate_k[...] = jnp.zeros_like(dstate_k)
            dstate_v[...] = jnp.zeros_like(dstate_v)

        consts = _scan_consts()
        before_chunk = jnp.where(pl.program_id(0) == n - 1, 0.0, 1.0) * sprev_ref[0]

        def block(jj, carry):
            dk_s, dv_s, sc = carry
            j = cc // 8 - 1 - jj
            base = pl.multiple_of(j * 8, 8)
            rows = pl.ds(base, 8)
            dy8, r8, w8, k8, v8, a8, b8, sa8 = (ref[rows, :] for ref in
                                                (dy_ref, r_ref, w_ref, k_ref, v_ref, a_ref, b_ref, sa_ref))
            dyp, rp, wp, kp, vp, ap, bp, sap = (_pieces(x) for x in (dy8, r8, w8, k8, v8, a8, b8, sa8))
            dsa8, dv8 = jnp.zeros((8, WB), F32), jnp.zeros((8, WB), F32)
            steps = range(7, -1, -1)
            cols_k = _pair_cols([(x, i) for i in steps for x in (rp, bp, kp, wp, ap)], consts)
            cols_v = _pair_cols([(x, i) for i in steps for x in (dyp, vp, sap)], consts)
            for n_done, i in enumerate(steps):
                r_c, b_c, k_c, w_c, a_c = cols_k[5 * n_done:5 * n_done + 5]
                dk_s = dk_s + r_c * _pair_rows(dy8, i)
                dsa = _pair_sums(dk_s * b_c)
                dv8 = _put_row(dv8, i, _pair_sums(dk_s * k_c))
                dsa8 = _put_row(dsa8, i, dsa)
                dk_s = dk_s * w_c + a_c * _pair_rows(dsa, 0)
            dsa_cols = _pair_cols([(_pieces(dsa8), i) for i in steps], consts)
            outs = [jnp.zeros((8, WB), F32) for _ in range(5)]
            for n_done, i in enumerate(steps):
                if i > 0:
                    sp = sall_ref[base + i - 1]
                else:
                    sp = jnp.where(j == 0, before_chunk, sall_ref[jnp.maximum(base - 1, 0)])
                dy_c, v_c, sa_c = cols_v[3 * n_done:3 * n_done + 3]
                dsa_c = dsa_cols[n_done]
                dv_s = dv_s + dy_c * _pair_rows(r8, i)
                vals = (_pair_sums(sc * dy_c), _pair_sums(dv_s * sp), _pair_sums(dv_s * v_c),
                        _pair_sums(sp * dsa_c), _pair_sums(dv_s * sa_c))
                outs = [_put_row(o, i, val) for o, val in zip(outs, vals)]
                dv_s = dv_s * _pair_rows(w8, i) + dsa_c * _pair_rows(a8, i)
                sc = sp
            dr8, dw8, dk8, da8, db8 = outs
            for ref, o in zip((dr_ref, dw_ref, dk_ref, dv_ref, da_ref, db_ref), (dr8, dw8, dk8, dv8, da8, db8)):
                ref[rows, :] = o
            return dk_s, dv_s, sc

        dk_s, dv_s, _ = lax.fori_loop(0, cc // 8, block, (dstate_k[...], dstate_v[...], sall_ref[cc - 1]))
        dstate_k[...] = dk_s
        dstate_v[...] = dv_s

    row = pl.BlockSpec((cc, WB), lambda i: (n - 1 - i, 0))
    tile = pltpu.VMEM((4 * HD_B, LANES), F32)
    return pl.pallas_call(
        body, name=name, grid=(n,),
        in_specs=[row] * 8 + [pl.BlockSpec((cc, 4 * HD_B, LANES), lambda i: (n - 1 - i, 0, 0)),
                              pl.BlockSpec((1, 4 * HD_B, LANES), lambda i: (jnp.maximum((n - 1 - i) * cc - 1, 0), 0, 0))],
        out_specs=[row] * 6, out_shape=[jax.ShapeDtypeStruct((t, WB), F32)] * 6,
        scratch_shapes=[tile, tile], compiler_params=_params(1))(dy, r, w, k, v, a, b, sa, sall, sall)


def _rwkv_post(y, r, k2, v, g, r_k, gn_w, gn_b, name):
    t = y.shape[0]
    tb = _tile(t, 256)

    def body(y_ref, r_ref, k_ref, v_ref, g_ref, rk_ref, gw_ref, gb_ref, o_ref):
        ones = _head_ones(WB, HD_B)
        yv = y_ref[...]
        yc = yv - _split_dot(yv, ones, 3) * (1.0 / HD_B)
        rstd = lax.rsqrt(_split_dot(yc * yc, ones, 3) * (1.0 / HD_B) + GN_EPS)
        rk = _split_dot(r_ref[...] * k_ref[...] * rk_ref[...], ones, 3)
        o_ref[...] = ((yc * rstd * gw_ref[...] + gb_ref[...] + rk * v_ref[...]) * g_ref[...]).astype(MXU)

    row = pl.BlockSpec((tb, WB), lambda i: (i, 0))
    vec = pl.BlockSpec((1, WB), lambda i: (0, 0))
    return pl.pallas_call(
        body, name=name, grid=(t // tb,), in_specs=[row] * 5 + [vec] * 3, out_specs=row,
        out_shape=jax.ShapeDtypeStruct((t, WB), MXU), compiler_params=_params(1))(y, r, k2, v, g, r_k, gn_w, gn_b)


def _rwkv_post_bwd(dob, y, r, k2, v, g, r_k, gn_w, gn_b, name):
    t = y.shape[0]
    tb = _tile(t, 256)

    def body(do_ref, y_ref, r_ref, k_ref, v_ref, g_ref, rk_ref, gw_ref, gb_ref,
             dy_ref, dg_ref, dr_ref, dk_ref, dv_ref, dgw_ref, dgb_ref, drk_ref):
        @pl.when(pl.program_id(0) == 0)
        def _():
            dgw_ref[...] = jnp.zeros_like(dgw_ref)
            dgb_ref[...] = jnp.zeros_like(dgb_ref)
            drk_ref[...] = jnp.zeros_like(drk_ref)

        ones = _head_ones(WB, HD_B)
        seg = lambda x: _split_dot(x, ones, 3)
        yv, rv, kv, vv, gv = y_ref[...], r_ref[...], k_ref[...], v_ref[...], g_ref[...]
        yc = yv - seg(yv) * (1.0 / HD_B)
        rstd = lax.rsqrt(seg(yc * yc) * (1.0 / HD_B) + GN_EPS)
        yn = yc * rstd
        rk = seg(rv * kv * rk_ref[...])
        dob_v = do_ref[...]
        dg_ref[...] = dob_v * (yn * gw_ref[...] + gb_ref[...] + rk * vv)
        dyg = dob_v * gv
        dgw_ref[...] += jnp.sum(dyg * yn, axis=0, keepdims=True)
        dgb_ref[...] += jnp.sum(dyg, axis=0, keepdims=True)
        dyn = dyg * gw_ref[...]
        dy_ref[...] = rstd * (dyn - (seg(dyn) + yn * seg(dyn * yn)) * (1.0 / HD_B))
        drk = seg(dyg * vv)
        dv_ref[...] = dyg * rk
        dr_ref[...] = drk * kv * rk_ref[...]
        dk_ref[...] = drk * rv * rk_ref[...]
        drk_ref[...] += jnp.sum(drk * rv * kv, axis=0, keepdims=True)

    row = pl.BlockSpec((tb, WB), lambda i: (i, 0))
    vec = pl.BlockSpec((1, WB), lambda i: (0, 0))
    full, small = jax.ShapeDtypeStruct((t, WB), F32), jax.ShapeDtypeStruct((1, WB), F32)
    return pl.pallas_call(
        body, name=name, grid=(t // tb,), in_specs=[row] * 6 + [vec] * 3, out_specs=[row] * 5 + [vec] * 3,
        out_shape=[full] * 5 + [small] * 3, compiler_params=_params(1))(dob, y, r, k2, v, g, r_k, gn_w, gn_b)


def _rwkv_prep_bwd(grads, p_all, mu_pad, w2cat, w0, a0, k_k, k_a, name):
    t = p_all.shape[0]
    tb = _tile(t, 256)

    def body(*refs):
        g_refs, ins, outs = refs[:10], refs[10:27], refs[27:]
        dr_s, dw, dk2_s, dv_s, das, dbs, dg, dr_b, dk2_b, dv_b = (ref[...] for ref in g_refs)
        dr_ref, dk_ref, dv_ref, dlo_ref, dw2_ref, dw0_ref, da0_ref, dkk_ref, dka_ref = outs

        @pl.when(pl.program_id(0) == 0)
        def _():
            for ref in (dw2_ref, dw0_ref, da0_ref, dkk_ref, dka_ref):
                ref[...] = jnp.zeros_like(ref)

        ones = _head_ones(WB, HD_B)
        q = _rwkv_inputs(ins, pl.program_id(0) == 0, ones)
        kk_w, ka_w = ins[15][...], ins[16][...]
        a, kk, k = q["a"], q["kk"], q["k"]
        dk2 = dk2_s + dk2_b
        dkk = dbs * a - das
        da = dbs * kk + dk2 * k * ka_w
        dk = dk2 * (1.0 + (a - 1.0) * ka_w)
        dka_ref[...] += jnp.sum(dk2 * k * (a - 1.0), axis=0, keepdims=True)
        proj = jnp.where(q["nrm"] > L2_EPS, _split_dot(dkk * kk, ones, 3), 0.0)
        dkk0 = (dkk - kk * proj) / q["den"]
        dk = dk + dkk0 * kk_w
        dkk_ref[...] += jnp.sum(dkk0 * k, axis=0, keepdims=True)
        dal = da * a * (1.0 - a)
        da0_ref[...] += jnp.sum(dal, axis=0, keepdims=True)
        sg = q["sg"]
        dwl = dw * q["decay"] * (-DECAY_C) * sg * (1.0 - sg)
        dw0_ref[...] += jnp.sum(dwl, axis=0, keepdims=True)
        dlin = jnp.concatenate([dwl, dal, dg], axis=1).astype(MXU)
        dw2_ref[...] += _dot(q["z"].astype(MXU), dlin, _TN)
        dz = _dot(dlin, ins[12][...], _NT)
        dlo_ref[...] = dz * _lora_dact(q["lo"])
        dr_ref[...] = dr_s + dr_b
        dk_ref[...] = dk
        dv_ref[...] = dv_s + dv_b

    row = pl.BlockSpec((tb, WB), lambda i: (i, 0))
    vec = pl.BlockSpec((1, WB), lambda i: (0, 0))
    full, small = jax.ShapeDtypeStruct((t, WB), F32), jax.ShapeDtypeStruct((1, WB), F32)
    return pl.pallas_call(
        body, name=name, grid=(t // tb,), in_specs=[row] * 10 + _rwkv_in_specs(t, tb),
        out_specs=[row] * 3 + [pl.BlockSpec((tb, 256), lambda i: (i, 0)),
                               pl.BlockSpec((256, 3 * WB), lambda i: (0, 0))] + [vec] * 4,
        out_shape=[full] * 3 + [jax.ShapeDtypeStruct((t, 256), F32), jax.ShapeDtypeStruct((256, 3 * WB), F32)]
        + [small] * 4, compiler_params=_params(1))(*grads, *_rwkv_in_args(p_all, mu_pad, w2cat, w0, a0, k_k, k_a))


def _shift_bwd(dshifted, p_all, mu_pad, name):
    t = p_all.shape[0]
    tb = _tile(t, 256)
    nt, nt8 = t // tb, tb // 8
    widths, cols, mus = (WB, WB, WB, 256), (COL_R, COL_K, COL_V, COL_L), (0, 1, 2, 6)

    def body(*refs):
        d_refs, n_refs, p_refs, q_refs, m_refs = refs[0:4], refs[4:8], refs[8:12], refs[12:16], refs[16:20]
        o_refs, dmu_refs = refs[20:24], refs[24:28]
        i = pl.program_id(0)

        @pl.when(i == 0)
        def _():
            for ref in dmu_refs:
                ref[...] = jnp.zeros_like(ref)

        for d, nx, p, q, m, o, dmu in zip(d_refs, n_refs, p_refs, q_refs, m_refs, o_refs, dmu_refs):
            dv, pv, mu = d[...], p[...], m[...]
            o[...] = (dv * (1.0 - mu) + mu * _shift_up(dv, nx[...], i == nt - 1)).astype(MXU)
            dmu[...] += jnp.sum(dv * (_shift_down(pv, q[...], i == 0) - pv), axis=0, keepdims=True)

    cur_d = [pl.BlockSpec((tb, w), lambda i: (i, 0)) for w in widths]
    next_d = [pl.BlockSpec((8, w), lambda i: (jnp.minimum((i + 1) * nt8, t // 8 - 1), 0)) for w in widths]
    cur_p = [pl.BlockSpec((tb, w), lambda i, j=j: (i, j)) for w, j in zip(widths, cols)]
    prev_p = [pl.BlockSpec((8, w), lambda i, j=j: (jnp.maximum(i * nt8 - 1, 0), j)) for w, j in zip(widths, cols)]
    mu_s = [pl.BlockSpec((1, w), lambda i, j=j: (0, j)) for w, j in zip(widths, mus)]
    vecs = [pl.BlockSpec((1, w), lambda i: (0, 0)) for w in widths]
    return pl.pallas_call(
        body, name=name, grid=(nt,), in_specs=cur_d + next_d + cur_p + prev_p + mu_s, out_specs=cur_d + vecs,
        out_shape=[jax.ShapeDtypeStruct((t, w), MXU) for w in widths]
        + [jax.ShapeDtypeStruct((1, w), F32) for w in widths],
        compiler_params=_params(1))(*dshifted, *dshifted, *(p_all,) * 8, *(mu_pad,) * 4)


def _peer(k):
    x, y, c = (lax.axis_index(n) for n in AXES)
    px = 1 - x if k & 4 else x
    py = 1 - y if k & 2 else y
    pc = 1 - c if k & 1 else c
    return (px, py, pc), 4 * px + 2 * py + pc


def _exchange(srcs, per_peer, name):
    n = len(srcs)
    shapes = [tuple(s.shape[1:]) if pp else tuple(s.shape) for s, pp in zip(srcs, per_peer)]
    pairs = [(j, k) for k in range(1, N_DEV) for j in range(n)]

    def body(*refs):
        src_refs, out_refs = refs[:n], refs[n:2 * n]
        send_sems, recv_sems, local_sems = refs[2 * n:]
        _, me = _peer(0)

        def copy(j, k, arriving):
            peer, idx = _peer(k)
            sem = j * (N_DEV - 1) + k - 1
            return pltpu.make_async_remote_copy(
                src_ref=src_refs[j].at[idx] if per_peer[j] else src_refs[j],
                dst_ref=out_refs[j].at[idx if arriving else me],
                send_sem=send_sems.at[sem], recv_sem=recv_sems.at[sem],
                device_id=peer, device_id_type=pl.DeviceIdType.MESH)

        local = [pltpu.make_async_copy(src_refs[j].at[me] if per_peer[j] else src_refs[j], out_refs[j].at[me],
                                       local_sems.at[j]) for j in range(n)]
        for cp in local:
            cp.start()
        for j, k in pairs:
            copy(j, k, False).start()
        for j, k in pairs:
            copy(j, k, True).wait_recv()
        for j, k in pairs:
            copy(j, k, False).wait_send()
        for cp in local:
            cp.wait()

    hbm = pl.BlockSpec(memory_space=pltpu.HBM)
    return pl.pallas_call(
        body, name=name, in_specs=[hbm] * n, out_specs=[hbm] * n,
        out_shape=[jax.ShapeDtypeStruct((N_DEV,) + shp, s.dtype) for shp, s in zip(shapes, srcs)],
        scratch_shapes=[pltpu.SemaphoreType.DMA((n * (N_DEV - 1),)), pltpu.SemaphoreType.DMA((n * (N_DEV - 1),)),
                        pltpu.SemaphoreType.DMA((n,))])(*srcs)


def _adam_update(g, w, m, v):
    c1, c2 = 1.0 - ADAM_B1 ** ADAM_STEP, 1.0 - ADAM_B2 ** ADAM_STEP
    nm = ADAM_B1 * m + (1.0 - ADAM_B1) * g
    nv = ADAM_B2 * v + (1.0 - ADAM_B2) * (g * g)
    return -ADAM_LR * ((nm / c1) / (jnp.sqrt(nv / c2) + ADAM_EPS) + ADAM_WD * w), nm, nv


def _row_tile(rows, cols):
    padded = -(-cols // LANES) * LANES
    cap = max(16, ADAM_BLOCK_BYTES // (N_DEV * padded * 4))
    best = 16
    for t in range(16, min(rows, cap) + 1, 16):
        if rows % t == 0:
            best = t
    return best


def _adamw(parts, w, m, v, name):
    _, rows, cols = w.shape
    tb = _row_tile(rows, cols)

    def body(p_ref, w_ref, m_ref, v_ref, g_ref, d_ref, nm_ref, nv_ref):
        g = p_ref[0].astype(F32)
        for d in range(1, N_DEV):
            g = g + p_ref[d].astype(F32)
        g_ref[0] = g
        d_ref[0], nm_ref[0], nv_ref[0] = _adam_update(g, w_ref[0], m_ref[0], v_ref[0])

    row = pl.BlockSpec((1, tb, cols), lambda i: (0, i, 0))
    out = jax.ShapeDtypeStruct(w.shape, F32)
    return pl.pallas_call(
        body, name=name, grid=(rows // tb,),
        in_specs=[pl.BlockSpec((N_DEV, tb, cols), lambda i: (0, i, 0)), row, row, row], out_specs=[row] * 4,
        out_shape=[out] * 4, compiler_params=_params(1))(parts, w, m, v)


def _adamw_small(parts, ws, ms, vs, name):
    n = len(ws)

    def body(*refs):
        p_ref = refs[0]
        w_refs, m_refs, v_refs = refs[1:1 + n], refs[1 + n:1 + 2 * n], refs[1 + 2 * n:1 + 3 * n]
        outs = refs[1 + 3 * n:]
        base = 0
        for j in range(n):
            rows, cols = ws[j].shape
            size = rows * cols
            for ch in range(-(-size // LANES)):
                r, c0 = divmod(ch * LANES, cols)
                width = min(LANES, cols - c0)
                g = p_ref[0, base + ch:base + ch + 1, 0:width]
                for d in range(1, N_DEV):
                    g = g + p_ref[d, base + ch:base + ch + 1, 0:width]
                at = (slice(r, r + 1), slice(c0, c0 + width))
                delta, nm, nv = _adam_update(g, w_refs[j][at], m_refs[j][at], v_refs[j][at])
                for out, val in zip((outs[j], outs[n + j], outs[2 * n + j], outs[3 * n + j]), (g, delta, nm, nv)):
                    out[at] = val
            base += -(-size // (8 * LANES)) * 8

    vmem = pl.BlockSpec(memory_space=pltpu.VMEM)
    res = pl.pallas_call(
        body, name=name, in_specs=[vmem] * (1 + 3 * n), out_specs=[vmem] * (4 * n),
        out_shape=[jax.ShapeDtypeStruct(a.shape, F32) for a in ws] * 4)(parts, *ws, *ms, *vs)
    return res[:n], res[n:2 * n], res[2 * n:3 * n], res[3 * n:]


def _rows(a, multiple):
    flat = a.reshape(-1)
    pad = -flat.shape[0] % (multiple * LANES)
    if pad:
        flat = jnp.concatenate([flat, jnp.zeros((pad,), a.dtype)])
    return flat.reshape(-1, LANES)


def _pack(arrs, multiple):
    return jnp.concatenate([_rows(a, multiple) for a in arrs], axis=0)


def _gathered_to_full(g, name, shard_shape):
    g = g.reshape((N_DEV,) + shard_shape)
    if name in COL_SHARDED:
        return jnp.transpose(g, (1, 0, 2)).reshape(shard_shape[0], N_DEV * shard_shape[1])
    return g.reshape(N_DEV * shard_shape[0], shard_shape[1])


def _full_to_per_device(full, name):
    if name in COL_SHARDED:
        r, c = full.shape
        return jnp.transpose(full.reshape(r, N_DEV, c // N_DEV), (1, 0, 2))
    return full.reshape(N_DEV, full.shape[0] // N_DEV, full.shape[1])


def _w2cat(w2, a2, g2):
    n_w, n_a, n_g = LORA
    out = jnp.zeros((256, 3 * WB), w2.dtype)
    out = out.at[0:n_w, 0:WB].set(w2)
    out = out.at[n_w:n_w + n_a, WB:2 * WB].set(a2)
    return out.at[n_w + n_a:n_w + n_a + n_g, 2 * WB:].set(g2)


def _local_step(x, target, w):
    n_w, n_a, n_g = LORA
    w_in_pad = jnp.pad(w["w_in"], ((0, 0), (0, N_INP - N_IN)))
    mu_pad = jnp.pad(w["rwkv_shift_mu"], ((0, 0), (0, 1792 - 1696)))
    w2cat = _w2cat(w["rwkv_w2"], w["rwkv_a2"], w["rwkv_g2"])
    r_k = w["rwkv_r_k"].reshape(1, WB)
    rw = (mu_pad, w2cat, w["rwkv_w0"], w["rwkv_a0"], w["rwkv_k_k"], w["rwkv_k_a"])

    h1 = _rms_fwd(x, w["ffn1_norm"], "ffn1_norm")
    x1 = _ffn_fwd(x, h1, w["ffn1_w_gate"], w["ffn1_w_up"], w["ffn1_w_down"], "ffn1_fwd")
    h2 = _rms_fwd(x1, w["mix_norm"], "mix_norm")
    p_all = _matmul(h2, w_in_pad, name="in_proj")
    oa, oraw, states = _hgrn_fwd(p_all, w["hgrn_lb_logits"], w["hgrn_out_norm"], "hgrn_fwd")
    r, decay, k2, v, sa, sb, g = _rwkv_prep(p_all, *rw, "rwkv_prep")
    y, s_a, sall = _rwkv_scan_fwd(r, decay, k2, v, sa, sb, "rwkv_scan_fwd")
    post_w = (r_k, w["rwkv_gn_w"], w["rwkv_gn_b"])
    ob = _rwkv_post(y, r, k2, v, g, *post_w, "rwkv_post")
    o = jnp.concatenate([oa, ob], axis=1)
    x2 = _matmul(o, w["w_out"], res=x1, name="out_proj")
    h3 = _rms_fwd(x2, w["ffn2_norm"], "ffn2_norm")
    x3 = _ffn_fwd(x2, h3, w["ffn2_w_gate"], w["ffn2_w_up"], w["ffn2_w_down"], "ffn2_fwd")
    loss, dx3, d_final = _loss_head(x3, w["final_norm"].reshape(1, D), target, "loss_head")

    grads = {"final_norm": d_final.reshape(D)}

    def ffn_back(prefix, h, dy, x_in, norm):
        wg, wu, wd = (w[prefix + s] for s in ("_w_gate", "_w_up", "_w_down"))
        dh, act, dgate, dup, dout = _ffn_bwd(h, dy, wg, wu, wd, prefix + "_bwd")
        grads[prefix + "_w_gate"] = _matmul(h, dgate, ta=True, out_dtype=WIRE, name=prefix + "_dwg")
        grads[prefix + "_w_up"] = _matmul(h, dup, ta=True, out_dtype=WIRE, name=prefix + "_dwu")
        grads[prefix + "_w_down"] = _matmul(act, dout, ta=True, out_dtype=WIRE, name=prefix + "_dwd")
        dx, grads[prefix + "_norm"] = _rms_bwd(x_in, norm, dh, dy, prefix + "_norm_bwd")
        return dx

    dx2 = ffn_back("ffn2", h3, dx3, x2, w["ffn2_norm"])
    grads["w_out"] = _matmul(o, dx2, ta=True, out_dtype=WIRE, name="d_w_out")
    do = _matmul(dx2, w["w_out"], tb=True, name="d_mixed")
    dqa, dfa, dia, dga, grads["hgrn_out_norm"], grads["hgrn_lb_logits"] = _hgrn_bwd(
        p_all, w["hgrn_lb_logits"], w["hgrn_out_norm"], oraw, states, do[:, :WA], "hgrn_bwd")
    dy, dg, dr_b, dk2_b, dv_b, grads["rwkv_gn_w"], grads["rwkv_gn_b"], d_rk = _rwkv_post_bwd(
        do[:, WA:], y, r, k2, v, g, *post_w, "rwkv_post_bwd")
    grads["rwkv_r_k"] = d_rk.reshape(w["rwkv_r_k"].shape)
    dr, dw, dk2, dv, dsa, dsb = _rwkv_scan_bwd(dy, r, decay, k2, v, sa, sb, s_a, sall, "rwkv_scan_bwd")
    (dsr, dsk, dsv, dslo, dw2cat, grads["rwkv_w0"], grads["rwkv_a0"], grads["rwkv_k_k"],
     grads["rwkv_k_a"]) = _rwkv_prep_bwd((dr, dw, dk2, dv, dsa, dsb, dg, dr_b, dk2_b, dv_b), p_all, *rw,
                                         "rwkv_prep_bwd")
    grads["rwkv_w2"] = dw2cat[0:n_w, 0:WB]
    grads["rwkv_a2"] = dw2cat[n_w:n_w + n_a, WB:2 * WB]
    grads["rwkv_g2"] = dw2cat[n_w + n_a:n_w + n_a + n_g, 2 * WB:]
    dpr, dpk, dpv, dplo, dmu_r, dmu_k, dmu_v, dmu_lo = _shift_bwd((dsr, dsk, dsv, dslo), p_all, mu_pad, "shift_bwd")
    grads["rwkv_shift_mu"] = jnp.concatenate([dmu_r, dmu_k, dmu_v, dmu_lo], axis=1)[:, :1696]
    dp = jnp.concatenate([dqa, dfa, dia, dga, dpr, dpk, dpv, dplo], axis=1)
    grads["w_in"] = _matmul(h2, dp, ta=True, out_dtype=WIRE, name="d_w_in")[:, :N_IN]
    dh2 = _matmul(dp, w_in_pad, tb=True, name="d_h2")
    dx1, grads["mix_norm"] = _rms_bwd(x1, w["mix_norm"], dh2, dx2, "mix_norm_bwd")
    dx0 = ffn_back("ffn1", h1, dx1, x, w["ffn1_norm"])
    return loss[0, 0], dx0, grads


def kernel(x, ffn1_norm, ffn1_w_gate, ffn1_w_up, ffn1_w_down, mix_norm, w_in, hgrn_lb_logits, hgrn_out_norm, rwkv_shift_mu, rwkv_w0, rwkv_w2, rwkv_a0, rwkv_a2, rwkv_g2, rwkv_k_k, rwkv_k_a, rwkv_r_k, rwkv_gn_w, rwkv_gn_b, w_out, ffn2_norm, ffn2_w_gate, ffn2_w_up, ffn2_w_down, final_norm, loss_target, m_ffn1_norm, m_ffn1_w_gate, m_ffn1_w_up, m_ffn1_w_down, m_mix_norm, m_w_in, m_hgrn_lb_logits, m_hgrn_out_norm, m_rwkv_shift_mu, m_rwkv_w0, m_rwkv_w2, m_rwkv_a0, m_rwkv_a2, m_rwkv_g2, m_rwkv_k_k, m_rwkv_k_a, m_rwkv_r_k, m_rwkv_gn_w, m_rwkv_gn_b, m_w_out, m_ffn2_norm, m_ffn2_w_gate, m_ffn2_w_up, m_ffn2_w_down, m_final_norm, v_ffn1_norm, v_ffn1_w_gate, v_ffn1_w_up, v_ffn1_w_down, v_mix_norm, v_w_in, v_hgrn_lb_logits, v_hgrn_out_norm, v_rwkv_shift_mu, v_rwkv_w0, v_rwkv_w2, v_rwkv_a0, v_rwkv_a2, v_rwkv_g2, v_rwkv_k_k, v_rwkv_k_a, v_rwkv_r_k, v_rwkv_gn_w, v_rwkv_gn_b, v_w_out, v_ffn2_norm, v_ffn2_w_gate, v_ffn2_w_up, v_ffn2_w_down, v_final_norm):
    args = dict(locals())
    wts = {n: args[n] for n in WEIGHTS}
    mom = {n: args["m_" + n] for n in WEIGHTS}
    var = {n: args["v_" + n] for n in WEIGHTS}
    shard_shapes = {n: wts[n].shape[1:] for n in SHARDED}

    gathered = _exchange([wts[n].astype(MXU) for n in SHARDED], [False] * len(SHARDED), "gather_weights")
    full = {n: _gathered_to_full(g, n, shard_shapes[n]) for n, g in zip(SHARDED, gathered)}
    for n in SMALL:
        full[n] = wts[n] if n in ("hgrn_lb_logits", "rwkv_r_k", "final_norm") else wts[n].reshape(1, -1)

    loss, grad_x, grads = _local_step(x[0], loss_target[0], full)
    loss = lax.psum(loss, AXES)

    contrib = ([_full_to_per_device(grads[n], n).astype(WIRE) for n in SHARDED]
               + [_pack([grads[n] for n in SMALL], 8)])
    parts = _exchange(contrib, [True] * len(SHARDED) + [False], "scatter_grads")

    new = {}
    for n, p in zip(SHARDED, parts):
        new[n] = _adamw(p, wts[n], mom[n], var[n], "adamw_" + n)
    two_d = lambda a: a if a.ndim == 2 else a.reshape(1, -1)
    small = _adamw_small(parts[-1], *([two_d(src[n]) for n in SMALL] for src in (wts, mom, var)), "adamw_small")
    for j, n in enumerate(SMALL):
        new[n] = [res[j].reshape(wts[n].shape) for res in small]
    return (loss, grad_x[None], *[new[n][0] for n in WEIGHTS], *[new[n][1] for n in WEIGHTS],
            *[new[n][2] for n in WEIGHTS], *[new[n][3] for n in WEIGHTS])
```

```python
import functools
import math

import jax
import jax.numpy as jnp
from jax import lax
from jax.experimental import pallas as pl
from jax.experimental.pallas import tpu as pltpu

F32 = jnp.float32
MXU = jnp.bfloat16
WIRE = jnp.bfloat16
D = 1024
FF = 2816
WA = 512
WB = 512
HD_B = 64
N_IN = 3744
N_INP = 3840
COL_R, COL_K, COL_V = 4, 5, 6
COL_L = 14
LORA = (32, 32, 96)
HG_CHUNK = 64
SCAN_CHUNK = 64
NORM_EPS = 1e-6
GN_EPS = 64e-5
L2_EPS = 1e-12
DECAY_C = math.exp(-0.5)
N_DEV = 8
LANES = 128
ADAM_BLOCK_BYTES = 4 * 1024 * 1024
MATMUL_BLOCK_BYTES = 40 * 1024 * 1024
VMEM_LIMIT = 56 * 1024 * 1024
ADAM_LR, ADAM_B1, ADAM_B2, ADAM_EPS, ADAM_WD, ADAM_STEP = 0.001, 0.9, 0.999, 1e-08, 0.01, 10
AXES = ("x", "y", "c")

SHARDED = ("ffn1_w_gate", "ffn1_w_up", "ffn1_w_down", "w_in", "rwkv_w2", "rwkv_a2", "rwkv_g2", "w_out",
           "ffn2_w_gate", "ffn2_w_up", "ffn2_w_down")
COL_SHARDED = {"ffn1_w_gate", "ffn1_w_up", "w_in", "rwkv_w2", "rwkv_a2", "rwkv_g2", "ffn2_w_gate", "ffn2_w_up"}
SMALL = ("ffn1_norm", "mix_norm", "hgrn_lb_logits", "hgrn_out_norm", "rwkv_shift_mu", "rwkv_w0", "rwkv_a0",
         "rwkv_k_k", "rwkv_k_a", "rwkv_r_k", "rwkv_gn_w", "rwkv_gn_b", "ffn2_norm", "final_norm")
WEIGHTS = ("ffn1_norm", "ffn1_w_gate", "ffn1_w_up", "ffn1_w_down", "mix_norm", "w_in", "hgrn_lb_logits",
           "hgrn_out_norm", "rwkv_shift_mu", "rwkv_w0", "rwkv_w2", "rwkv_a0", "rwkv_a2", "rwkv_g2", "rwkv_k_k",
           "rwkv_k_a", "rwkv_r_k", "rwkv_gn_w", "rwkv_gn_b", "w_out", "ffn2_norm", "ffn2_w_gate", "ffn2_w_up",
           "ffn2_w_down", "final_norm")


def _tile(n, cap):
    if n <= cap:
        return n
    for t in range(cap - cap % LANES, 0, -LANES):
        if n % t == 0:
            return t
    raise ValueError((n, cap))


def _params(n_axes):
    return pltpu.CompilerParams(dimension_semantics=("arbitrary",) * n_axes, vmem_limit_bytes=VMEM_LIMIT)


def _sig(x):
    return jax.nn.sigmoid(x)


def _dsilu(z, s):
    return s * (1.0 + z * (1.0 - s))


def _dot(a, b, dims=((1,), (0,)), precision=None):
    return lax.dot_general(a, b, (dims, ((), ())), preferred_element_type=F32, precision=precision)


_NT = ((1,), (1,))
_TN = ((0,), (0,))
_HI = lax.Precision.HIGHEST


def _iota(shape, dim):
    return lax.broadcasted_iota(jnp.int32, shape, dim)


def _split_dot(x, ones, passes):
    hi = x.astype(jnp.bfloat16)
    acc = _dot(hi, ones)
    rem = x
    for _ in range(passes - 1):
        rem = rem - hi.astype(F32)
        hi = rem.astype(jnp.bfloat16)
        acc = acc + _dot(hi, ones)
    return acc


def _head_ones(n, width):
    shift = width.bit_length() - 1
    return (_iota((n, n), 0) >> shift == _iota((n, n), 1) >> shift).astype(jnp.bfloat16)


def _matmul(a, b, *, ta=False, tb=False, out_dtype=F32, res=None, after=(), name):
    m, k = (a.shape[1], a.shape[0]) if ta else a.shape
    n = b.shape[0] if tb else b.shape[1]
    tm, tn = _tile(m, 1408), _tile(n, 1408)
    in_bytes = max(a.dtype.itemsize, b.dtype.itemsize)
    for tk in (_tile(k, 1024), _tile(k, 512), _tile(k, 256)):
        if 2 * (tm + tn) * tk * in_bytes + 3 * tm * tn * 4 <= MATMUL_BLOCK_BYTES:
            break
    nk = k // tk
    dims = ((0 if ta else 1,), (1 if tb else 0,))

    def body(*refs):
        a_ref, b_ref = refs[:2]
        o_ref, acc = refs[-2:]
        kk = pl.program_id(2)

        @pl.when(kk == 0)
        def _():
            acc[...] = jnp.zeros_like(acc)

        acc[...] += _dot(a_ref[...].astype(MXU), b_ref[...].astype(MXU), dims)

        @pl.when(kk == nk - 1)
        def _():
            v = acc[...]
            if res is not None:
                v = v + refs[2][...]
            o_ref[...] = v.astype(out_dtype)

    a_spec = pl.BlockSpec((tk, tm), lambda i, j, kk: (kk, i)) if ta else pl.BlockSpec((tm, tk), lambda i, j, kk: (i, kk))
    b_spec = pl.BlockSpec((tn, tk), lambda i, j, kk: (j, kk)) if tb else pl.BlockSpec((tk, tn), lambda i, j, kk: (kk, j))
    o_spec = pl.BlockSpec((tm, tn), lambda i, j, kk: (i, j))
    ins, specs = [a, b], [a_spec, b_spec]
    if res is not None:
        ins.append(res)
        specs.append(o_spec)
    ins += list(after)
    specs += [pl.BlockSpec(memory_space=pl.ANY)] * len(after)
    return pl.pallas_call(
        body, name=name, grid=(m // tm, n // tn, nk), in_specs=specs, out_specs=o_spec,
        out_shape=jax.ShapeDtypeStruct((m, n), out_dtype), scratch_shapes=[pltpu.VMEM((tm, tn), F32)],
        compiler_params=_params(3))(*ins)


def _rms_fwd(x, g, name):
    t = x.shape[0]
    tb = _tile(t, 512)

    def body(x_ref, g_ref, o_ref):
        xv = x_ref[...]
        rinv = lax.rsqrt(jnp.mean(xv * xv, axis=-1, keepdims=True) + NORM_EPS)
        o_ref[...] = (xv * rinv * g_ref[...]).astype(MXU)

    return pl.pallas_call(
        body, name=name, grid=(t // tb,),
        in_specs=[pl.BlockSpec((tb, D), lambda i: (i, 0)), pl.BlockSpec((1, D), lambda i: (0, 0))],
        out_specs=pl.BlockSpec((tb, D), lambda i: (i, 0)), out_shape=jax.ShapeDtypeStruct((t, D), MXU),
        compiler_params=_params(1))(x, g)


def _rms_bwd(x, g, dh, dres, name):
    t = x.shape[0]
    tb = _tile(t, 512)

    def body(x_ref, g_ref, dh_ref, dres_ref, dx_ref, dg_ref):
        @pl.when(pl.program_id(0) == 0)
        def _():
            dg_ref[...] = jnp.zeros_like(dg_ref)

        xv = x_ref[...]
        rinv = lax.rsqrt(jnp.mean(xv * xv, axis=-1, keepdims=True) + NORM_EPS)
        xhat = xv * rinv
        dhv = dh_ref[...]
        dg_ref[...] += jnp.sum(dhv * xhat, axis=0, keepdims=True)
        dxhat = dhv * g_ref[...]
        dx_ref[...] = dres_ref[...] + rinv * (dxhat - xhat * jnp.mean(dxhat * xhat, axis=-1, keepdims=True))

    row = pl.BlockSpec((tb, D), lambda i: (i, 0))
    vec = pl.BlockSpec((1, D), lambda i: (0, 0))
    return pl.pallas_call(
        body, name=name, grid=(t // tb,), in_specs=[row, vec, row, row], out_specs=[row, vec],
        out_shape=[jax.ShapeDtypeStruct((t, D), F32), jax.ShapeDtypeStruct((1, D), F32)],
        compiler_params=_params(1))(x, g, dh, dres)


def _loss_head(x, g, target, name):
    t = x.shape[0]
    tb = _tile(t, 512)

    def body(x_ref, g_ref, t_ref, loss_ref, dx_ref, dg_ref):
        @pl.when(pl.program_id(0) == 0)
        def _():
            dg_ref[...] = jnp.zeros_like(dg_ref)
            loss_ref[...] = jnp.zeros_like(loss_ref)

        xv = x_ref[...]
        gv = g_ref[...]
        rinv = lax.rsqrt(jnp.mean(xv * xv, axis=-1, keepdims=True) + NORM_EPS)
        xhat = xv * rinv
        err = xhat * gv - t_ref[...]
        per_tok = jnp.mean(err * err, axis=-1, keepdims=True)
        loss_ref[...] += jnp.broadcast_to(0.5 * jnp.sum(per_tok, axis=0, keepdims=True), loss_ref.shape)
        dy = err * (1.0 / D)
        dg_ref[...] += jnp.sum(dy * xhat, axis=0, keepdims=True)
        dxhat = dy * gv
        dx_ref[...] = rinv * (dxhat - xhat * jnp.mean(dxhat * xhat, axis=-1, keepdims=True))

    row = pl.BlockSpec((tb, D), lambda i: (i, 0))
    vec = pl.BlockSpec((1, D), lambda i: (0, 0))
    return pl.pallas_call(
        body, name=name, grid=(t // tb,), in_specs=[row, vec, row],
        out_specs=[pl.BlockSpec((1, LANES), lambda i: (0, 0)), row, vec],
        out_shape=[jax.ShapeDtypeStruct((1, LANES), F32), jax.ShapeDtypeStruct((t, D), F32),
                   jax.ShapeDtypeStruct((1, D), F32)],
        compiler_params=_params(1))(x, g, target)


def _ffn_fwd(x, h, wg, wu, wd, name):
    t = x.shape[0]
    tb, fb = _tile(t, 1024), 256
    nf = FF // fb

    def body(x_ref, h_ref, wg_ref, wu_ref, wd_ref, o_ref, acc):
        f = pl.program_id(1)

        @pl.when(f == 0)
        def _():
            acc[...] = jnp.zeros_like(acc)

        hv = h_ref[...]
        gate = _dot(hv, wg_ref[...])
        up = _dot(hv, wu_ref[...])
        act = (gate * _sig(gate) * up).astype(MXU)
        acc[...] += _dot(act, wd_ref[...])

        @pl.when(f == nf - 1)
        def _():
            o_ref[...] = x_ref[...] + 0.5 * acc[...]

    row = pl.BlockSpec((tb, D), lambda i, f: (i, 0))
    col = pl.BlockSpec((D, fb), lambda i, f: (0, f))
    return pl.pallas_call(
        body, name=name, grid=(t // tb, nf),
        in_specs=[row, row, col, col, pl.BlockSpec((fb, D), lambda i, f: (f, 0))], out_specs=row,
        out_shape=jax.ShapeDtypeStruct((t, D), F32), scratch_shapes=[pltpu.VMEM((tb, D), F32)],
        compiler_params=_params(2))(x, h, wg, wu, wd)


def _ffn_bwd(h, dy, wg, wu, wd, name):
    t = h.shape[0]
    tb, fb = _tile(t, 1024), 256
    nf = FF // fb

    def body(h_ref, dy_ref, wg_ref, wu_ref, wd_ref, dh_ref, act_ref, dg_ref, du_ref, dout_ref, acc):
        f = pl.program_id(1)

        @pl.when(f == 0)
        def _():
            acc[...] = jnp.zeros_like(acc)

        hv = h_ref[...]
        dout = (0.5 * dy_ref[...]).astype(MXU)
        dout_ref[...] = dout
        gate = _dot(hv, wg_ref[...])
        up = _dot(hv, wu_ref[...])
        dact = _dot(dout, wd_ref[...], _NT)
        s = _sig(gate)
        silu = gate * s
        act_ref[...] = (silu * up).astype(MXU)
        dup = (dact * silu).astype(MXU)
        dgate = (dact * up * _dsilu(gate, s)).astype(MXU)
        du_ref[...] = dup
        dg_ref[...] = dgate
        acc[...] += _dot(dgate, wg_ref[...], _NT) + _dot(dup, wu_ref[...], _NT)

        @pl.when(f == nf - 1)
        def _():
            dh_ref[...] = acc[...]

    row = pl.BlockSpec((tb, D), lambda i, f: (i, 0))
    col = pl.BlockSpec((D, fb), lambda i, f: (0, f))
    hid = pl.BlockSpec((tb, fb), lambda i, f: (i, f))
    hid_shape = jax.ShapeDtypeStruct((t, FF), MXU)
    return pl.pallas_call(
        body, name=name, grid=(t // tb, nf),
        in_specs=[row, row, col, col, pl.BlockSpec((fb, D), lambda i, f: (f, 0))],
        out_specs=[row, hid, hid, hid, row],
        out_shape=[jax.ShapeDtypeStruct((t, D), F32), hid_shape, hid_shape, hid_shape,
                   jax.ShapeDtypeStruct((t, D), MXU)],
        scratch_shapes=[pltpu.VMEM((tb, D), F32)], compiler_params=_params(2))(h, dy, wg, wu, wd)


def _hgrn_chunk(qa, fa, lbl):
    c = HG_CHUNK
    lb = _sig(lbl[0:1, :] - lbl[1:2, :])
    sf = _sig(fa)
    forget = lb + (1.0 - lb) * sf
    kh = 1.0 - forget
    row, col = _iota((c, c), 0), _iota((c, c), 1)
    b = _dot((col <= row).astype(F32), jnp.log(forget), precision=_HI)
    bref, blast = b[c // 2:c // 2 + 1, :], b[c - 1:c, :]
    sq = _sig(qa)
    q = qa * sq
    qt, kt = q * jnp.exp(b - bref), kh * jnp.exp(bref - b)
    qb, kl = q * jnp.exp(b), kh * jnp.exp(blast - b)
    causal = col <= row
    return dict(lb=lb, sf=sf, forget=forget, sq=sq, qt=qt, kt=kt, qb=qb, kl=kl, decay=jnp.exp(blast),
                causal=causal, e_q=jnp.exp(b), e_qt=jnp.exp(b - bref), e_kt=jnp.exp(bref - b),
                e_kl=jnp.exp(blast - b))


def _hgrn_specs(t):
    c = HG_CHUNK
    return c, t // c, WA // LANES


def _hgrn_fwd(p_all, lbl, onorm, name):
    t = p_all.shape[0]
    c, n, nh = _hgrn_specs(t)

    def body(q_ref, f_ref, i_ref, g_ref, lbl_ref, on_ref, oa_ref, oraw_ref, st_ref, state):
        @pl.when(pl.program_id(0) == 0)
        def _():
            state[...] = jnp.zeros_like(state)

        for h in range(nh):
            at = slice(h * LANES, (h + 1) * LANES)
            k = _hgrn_chunk(q_ref[:, at], f_ref[:, at], lbl_ref[:, at])
            v = i_ref[:, at]
            st = state[h]
            st_ref[h, 0] = st
            a = jnp.where(k["causal"], _dot(k["qt"], k["kt"], _NT, _HI), 0.0)
            o = _dot(a, v, precision=_HI) + _dot(k["qb"], st, _NT, _HI)
            state[h] = st * k["decay"] + _dot(v, k["kl"], _TN, _HI)
            oraw_ref[:, at] = o
            rinv = lax.rsqrt(jnp.mean(o * o, axis=-1, keepdims=True) + NORM_EPS)
            ga = g_ref[:, at]
            oa_ref[:, at] = (o * rinv * on_ref[:, at] * (ga * _sig(ga))).astype(MXU)

    def blk(j):
        return pl.BlockSpec((c, WA), lambda i: (i, j))

    return pl.pallas_call(
        body, name=name, grid=(n,),
        in_specs=[blk(0), blk(1), blk(2), blk(3), pl.BlockSpec((2, WA), lambda i: (0, 0)),
                  pl.BlockSpec((1, WA), lambda i: (0, 0))],
        out_specs=[blk(0), blk(0), pl.BlockSpec((nh, 1, LANES, LANES), lambda i: (0, i, 0, 0))],
        out_shape=[jax.ShapeDtypeStruct((t, WA), MXU), jax.ShapeDtypeStruct((t, WA), F32),
                   jax.ShapeDtypeStruct((nh, n, LANES, LANES), F32)],
        scratch_shapes=[pltpu.VMEM((nh, LANES, LANES), F32)], compiler_params=_params(1))(
            p_all, p_all, p_all, p_all, lbl, onorm)


def _hgrn_bwd(p_all, lbl, onorm, oraw, states, doa, name):
    t = p_all.shape[0]
    c, n, nh = _hgrn_specs(t)

    def body(q_ref, f_ref, i_ref, g_ref, lbl_ref, on_ref, oraw_ref, st_ref, doa_ref,
             dq_ref, df_ref, di_ref, dg_ref, don_ref, dlbl_ref, dstate, dlb):
        @pl.when(pl.program_id(0) == 0)
        def _():
            dstate[...] = jnp.zeros_like(dstate)
            dlb[...] = jnp.zeros_like(dlb)
            don_ref[...] = jnp.zeros_like(don_ref)

        for h in range(nh):
            at = slice(h * LANES, (h + 1) * LANES)
            qa, fa, v, ga = q_ref[:, at], f_ref[:, at], i_ref[:, at], g_ref[:, at]
            k = _hgrn_chunk(qa, fa, lbl_ref[:, at])
            st, dst_next = st_ref[h, 0], dstate[h]
            o = oraw_ref[:, at]
            gain = on_ref[:, at]
            rinv = lax.rsqrt(jnp.mean(o * o, axis=-1, keepdims=True) + NORM_EPS)
            on = o * rinv
            sg = _sig(ga)
            gate = ga * sg
            dout = doa_ref[:, at]
            don_ref[:, at] += jnp.sum(dout * on * gate, axis=0, keepdims=True)
            dg_ref[:, at] = (dout * on * gain * _dsilu(ga, sg)).astype(MXU)
            d_on = dout * gain * gate
            do = rinv * (d_on - on * jnp.mean(d_on * on, axis=-1, keepdims=True))

            a = jnp.where(k["causal"], _dot(k["qt"], k["kt"], _NT, _HI), 0.0)
            dqb = _dot(do, st, precision=_HI)
            dstate[h] = dst_next * k["decay"] + _dot(do, k["qb"], _TN, _HI)
            da = jnp.where(k["causal"], _dot(do, v, _NT, _HI), 0.0)
            dqt = _dot(da, k["kt"], precision=_HI)
            dkt = _dot(da, k["qt"], _TN, _HI)
            dv = _dot(a, do, _TN, _HI) + _dot(k["kl"], dst_next, _NT, _HI)
            dkl = _dot(v, dst_next, precision=_HI)
            ddecay = jnp.sum(dst_next * st, axis=0, keepdims=True)
            dq = dqb * k["e_q"] + dqt * k["e_qt"]
            dk = dkt * k["e_kt"] + dkl * k["e_kl"]
            tq, tk, tl = dqt * k["qt"], dkt * k["kt"], dkl * k["kl"]
            db = dqb * k["qb"] + tq - tk - tl
            dbref = jnp.sum(tk - tq, axis=0, keepdims=True)
            dblast = jnp.sum(tl, axis=0, keepdims=True) + ddecay * k["decay"]
            rows = _iota((c, LANES), 0)
            db = db + jnp.where(rows == c // 2, dbref, 0.0) + jnp.where(rows == c - 1, dblast, 0.0)
            row, col = _iota((c, c), 0), _iota((c, c), 1)
            dlogf = _dot((col >= row).astype(F32), db, precision=_HI)
            dq_ref[:, at] = (dq * _dsilu(qa, k["sq"])).astype(MXU)
            di_ref[:, at] = dv.astype(MXU)
            dforget = dlogf / k["forget"] - dk
            sf, lb = k["sf"], k["lb"]
            df_ref[:, at] = (dforget * (1.0 - lb) * sf * (1.0 - sf)).astype(MXU)
            dlb[:, at] += jnp.sum(dforget * (1.0 - sf), axis=0, keepdims=True)
            dl0 = dlb[:, at] * lb * (1.0 - lb)
            dlbl_ref[:, at] = jnp.where(_iota((2, LANES), 0) == 0, dl0, -dl0)

    def blk(j):
        return pl.BlockSpec((c, WA), lambda i: (n - 1 - i, j))

    vec = pl.BlockSpec((1, WA), lambda i: (0, 0))
    lg = pl.BlockSpec((2, WA), lambda i: (0, 0))
    grad = jax.ShapeDtypeStruct((t, WA), MXU)
    return pl.pallas_call(
        body, name=name, grid=(n,),
        in_specs=[blk(0), blk(1), blk(2), blk(3), lg, vec, blk(0),
                  pl.BlockSpec((nh, 1, LANES, LANES), lambda i: (0, n - 1 - i, 0, 0)), blk(0)],
        out_specs=[blk(0), blk(0), blk(0), blk(0), vec, lg],
        out_shape=[grad, grad, grad, grad, jax.ShapeDtypeStruct((1, WA), F32), jax.ShapeDtypeStruct((2, WA), F32)],
        scratch_shapes=[pltpu.VMEM((nh, LANES, LANES), F32), pltpu.VMEM((1, WA), F32)],
        compiler_params=_params(1))(p_all, p_all, p_all, p_all, lbl, onorm, oraw, states, doa)


def _lora_act(x):
    lane = _iota(x.shape, 1)
    n_w, n_a, n_g = LORA
    return jnp.where(lane < n_w, jnp.tanh(x),
                     jnp.where(lane < n_w + n_a, x, jnp.where(lane < n_w + n_a + n_g, _sig(x), 0.0)))


def _lora_dact(x):
    lane = _iota(x.shape, 1)
    n_w, n_a, n_g = LORA
    th, s = jnp.tanh(x), _sig(x)
    return jnp.where(lane < n_w, 1.0 - th * th,
                     jnp.where(lane < n_w + n_a, 1.0, jnp.where(lane < n_w + n_a + n_g, s * (1.0 - s), 0.0)))


def _shift_down(cur, prev8, first):
    rolled = pltpu.roll(cur, 1, 0)
    edge = prev8[7:8, :] * jnp.where(first, 0.0, 1.0)
    return jnp.where(_iota(cur.shape, 0) == 0, edge, rolled)


def _shift_up(cur, next8, last):
    rows = cur.shape[0]
    rolled = pltpu.roll(cur, rows - 1, 0)
    edge = next8[0:1, :] * jnp.where(last, 0.0, 1.0)
    return jnp.where(_iota(cur.shape, 0) == rows - 1, edge, rolled)


def _rwkv_inputs(refs, first, ones):
    (pr, pk, pv, plo, qr, qk, qv, qlo, mr, mk, mv, mlo, w2c, w0, a0, kk_w, ka_w) = refs
    mix = lambda cur, prev, mu: cur[...] + mu[...] * (_shift_down(cur[...], prev[...], first) - cur[...])
    r, k, v, lo = mix(pr, qr, mr), mix(pk, qk, mk), mix(pv, qv, mv), mix(plo, qlo, mlo)
    z = _lora_act(lo)
    lin = _dot(z.astype(MXU), w2c[...])
    sg = _sig(w0[...] + lin[:, :WB])
    decay = jnp.exp(-DECAY_C * sg)
    a = _sig(a0[...] + lin[:, WB:2 * WB])
    g = lin[:, 2 * WB:]
    kk0 = k * kk_w[...]
    nrm = jnp.sqrt(_split_dot(kk0 * kk0, ones, 3))
    den = jnp.maximum(nrm, L2_EPS)
    kk = kk0 / den
    k2 = k * (1.0 + (a - 1.0) * ka_w[...])
    return dict(r=r, k=k, v=v, lo=lo, z=z, sg=sg, decay=decay, a=a, g=g, kk=kk, den=den, nrm=nrm, k2=k2)


def _rwkv_in_specs(t, tb):
    nt8 = tb // 8

    def cur(w, j):
        return pl.BlockSpec((tb, w), lambda i: (i, j))

    def prev(w, j):
        return pl.BlockSpec((8, w), lambda i: (jnp.maximum(i * nt8 - 1, 0), j))

    def vec(w, j=0):
        return pl.BlockSpec((1, w), lambda i: (0, j))

    return [cur(WB, COL_R), cur(WB, COL_K), cur(WB, COL_V), cur(256, COL_L),
            prev(WB, COL_R), prev(WB, COL_K), prev(WB, COL_V), prev(256, COL_L),
            vec(WB, 0), vec(WB, 1), vec(WB, 2), vec(256, 6),
            pl.BlockSpec((256, 3 * WB), lambda i: (0, 0)), vec(WB), vec(WB), vec(WB), vec(WB)]


def _rwkv_in_args(p_all, mu_pad, w2cat, w0, a0, k_k, k_a):
    return (p_all,) * 8 + (mu_pad,) * 4 + (w2cat, w0, a0, k_k, k_a)


def _rwkv_prep(p_all, mu_pad, w2cat, w0, a0, k_k, k_a, name):
    t = p_all.shape[0]
    tb = _tile(t, 256)

    def body(*refs):
        ins, outs = refs[:17], refs[17:]
        q = _rwkv_inputs(ins, pl.program_id(0) == 0, _head_ones(WB, HD_B))
        for ref, val in zip(outs, (q["r"], q["decay"], q["k2"], q["v"], -q["kk"], q["kk"] * q["a"], q["g"])):
            ref[...] = val

    out = pl.BlockSpec((tb, WB), lambda i: (i, 0))
    return pl.pallas_call(
        body, name=name, grid=(t // tb,), in_specs=_rwkv_in_specs(t, tb), out_specs=[out] * 7,
        out_shape=[jax.ShapeDtypeStruct((t, WB), F32)] * 7, compiler_params=_params(1))(
            *_rwkv_in_args(p_all, mu_pad, w2cat, w0, a0, k_k, k_a))


def _pair_rows(x8, i):
    return jnp.concatenate([jnp.broadcast_to(x8[i:i + 1, p * LANES:(p + 1) * LANES], (HD_B, LANES))
                            for p in range(4)], axis=0)


def _pair_sums(x):
    return jnp.concatenate([jnp.sum(x[p * HD_B:(p + 1) * HD_B], axis=0, keepdims=True) for p in range(4)], axis=1)


def _put_row(buf, i, row):
    return jnp.where(_iota(buf.shape, 0) == i, row, buf)


def _pieces(x):
    hi = x.astype(jnp.bfloat16).astype(F32)
    lo = (x - hi).astype(jnp.bfloat16).astype(F32)
    upper = (_iota((x.shape[0], LANES), 1) & (HD_B // 2)) != 0
    swapped = [jnp.where(upper, pltpu.roll(lo[:, p * LANES:(p + 1) * LANES], HD_B // 2, 1),
                         pltpu.roll(lo[:, p * LANES:(p + 1) * LANES], LANES - HD_B // 2, 1)) for p in range(4)]
    return hi, jnp.concatenate(swapped, axis=1)


def _scan_consts():
    row, lane = _iota((HD_B, LANES), 0), _iota((HD_B, LANES), 1) & (HD_B - 1)
    return (row == lane).astype(jnp.bfloat16), (row == lane ^ (HD_B // 2)).astype(jnp.bfloat16), _head_ones(LANES, HD_B)


def _pair_cols(many, consts):
    diag_hi, diag_lo, ones = consts
    tiles = []
    for (hi8, lo8), i in many:
        for p in range(4):
            lanes = slice(p * LANES, (p + 1) * LANES)
            hi = jnp.broadcast_to(hi8[i:i + 1, lanes], (16, LANES)).astype(jnp.bfloat16)
            lo = jnp.broadcast_to(lo8[i:i + 1, lanes], (16, LANES)).astype(jnp.bfloat16)
            for g in range(HD_B // 16):
                rows = slice(g * 16, (g + 1) * 16)
                tiles.append(hi * diag_hi[rows] + lo * diag_lo[rows])
    out = _dot(jnp.concatenate(tiles, axis=0), ones)
    return [out[m * 4 * HD_B:(m + 1) * 4 * HD_B] for m in range(len(many))]


def _rwkv_scan_fwd(r, w, k, v, a, b, name):
    t = r.shape[0]
    cc = min(t, SCAN_CHUNK)

    def body(r_ref, w_ref, k_ref, v_ref, a_ref, b_ref, y_ref, sa_ref, sall_ref, state_k, state_v):
        @pl.when(pl.program_id(0) == 0)
        def _():
            state_k[...] = jnp.zeros_like(state_k)
            state_v[...] = jnp.zeros_like(state_v)

        consts = _scan_consts()

        def block(j, carry):
            sk, sv = carry
            base = pl.multiple_of(j * 8, 8)
            rows = pl.ds(base, 8)
            r8, w8, k8, v8, a8, b8 = (ref[rows, :] for ref in (r_ref, w_ref, k_ref, v_ref, a_ref, b_ref))
            rp, wp, kp, vp, ap, bp = (_pieces(x) for x in (r8, w8, k8, v8, a8, b8))
            y8 = jnp.zeros((8, WB), F32)
            sa8 = jnp.zeros((8, WB), F32)
            cols = _pair_cols([(x, i) for i in range(8) for x in (ap, wp, bp, kp, rp, vp)], consts)
            for i in range(8):
                a_c, w_c, b_c, k_c, r_c = cols[6 * i:6 * i + 5]
                sa = _pair_sums(sk * a_c)
                sk = sk * w_c + b_c * _pair_rows(sa, 0) + k_c * _pair_rows(v8, i)
                y8 = _put_row(y8, i, _pair_sums(sk * r_c))
                sa8 = _put_row(sa8, i, sa)
            sa_cols = _pair_cols([(_pieces(sa8), i) for i in range(8)], consts)
            for i in range(8):
                sv = sv * _pair_rows(w8, i) + sa_cols[i] * _pair_rows(b8, i) + cols[6 * i + 5] * _pair_rows(k8, i)
                sall_ref[base + i] = sv
            y_ref[rows, :] = y8
            sa_ref[rows, :] = sa8
            return sk, sv

        state_k[...], state_v[...] = lax.fori_loop(0, cc // 8, block, (state_k[...], state_v[...]))

    row = pl.BlockSpec((cc, WB), lambda i: (i, 0))
    tile = pltpu.VMEM((4 * HD_B, LANES), F32)
    return pl.pallas_call(
        body, name=name, grid=(t // cc,), in_specs=[row] * 6,
        out_specs=[row, row, pl.BlockSpec((cc, 4 * HD_B, LANES), lambda i: (i, 0, 0))],
        out_shape=[jax.ShapeDtypeStruct((t, WB), F32)] * 2 + [jax.ShapeDtypeStruct((t, 4 * HD_B, LANES), F32)],
        scratch_shapes=[tile, tile], compiler_params=_params(1))(r, w, k, v, a, b)


def _rwkv_scan_bwd(dy, r, w, k, v, a, b, sa, sall, name):
    t = r.shape[0]
    cc = min(t, SCAN_CHUNK)
    n = t // cc

    def body(dy_ref, r_ref, w_ref, k_ref, v_ref, a_ref, b_ref, sa_ref, sall_ref, sprev_ref,
             dr_ref, dw_ref, dk_ref, dv_ref, da_ref, db_ref, dstate_k, dstate_v):
        @pl.when(pl.program_id(0) == 0)
        def _():
            dstate_k[...] = jnp.zeros_like(dstate_k)
            dstate_v[...] = jnp.zeros_like(dstate_v)

        consts = _scan_consts()
        before_chunk = jnp.where(pl.program_id(0) == n - 1, 0.0, 1.0) * sprev_ref[0]

        def block(jj, carry):
            dk_s, dv_s, sc = carry
            j = cc // 8 - 1 - jj
            base = pl.multiple_of(j * 8, 8)
            rows = pl.ds(base, 8)
            dy8, r8, w8, k8, v8, a8, b8, sa8 = (ref[rows, :] for ref in
                                                (dy_ref, r_ref, w_ref, k_ref, v_ref, a_ref, b_ref, sa_ref))
            dyp, rp, wp, kp, vp, ap, bp, sap = (_pieces(x) for x in (dy8, r8, w8, k8, v8, a8, b8, sa8))
            dsa8, dv8 = jnp.zeros((8, WB), F32), jnp.zeros((8, WB), F32)
            steps = range(7, -1, -1)
            cols_k = _pair_cols([(x, i) for i in steps for x in (rp, bp, kp, wp, ap)], consts)
            cols_v = _pair_cols([(x, i) for i in steps for x in (dyp, vp, sap)], consts)
            for n_done, i in enumerate(steps):
                r_c, b_c, k_c, w_c, a_c = cols_k[5 * n_done:5 * n_done + 5]
                dk_s = dk_s + r_c * _pair_rows(dy8, i)
                dsa = _pair_sums(dk_s * b_c)
                dv8 = _put_row(dv8, i, _pair_sums(dk_s * k_c))
                dsa8 = _put_row(dsa8, i, dsa)
                dk_s = dk_s * w_c + a_c * _pair_rows(dsa, 0)
            dsa_cols = _pair_cols([(_pieces(dsa8), i) for i in steps], consts)
            outs = [jnp.zeros((8, WB), F32) for _ in range(5)]
            for n_done, i in enumerate(steps):
                if i > 0:
                    sp = sall_ref[base + i - 1]
                else:
                    sp = jnp.where(j == 0, before_chunk, sall_ref[jnp.maximum(base - 1, 0)])
                dy_c, v_c, sa_c = cols_v[3 * n_done:3 * n_done + 3]
                dsa_c = dsa_cols[n_done]
                dv_s = dv_s + dy_c * _pair_rows(r8, i)
                vals = (_pair_sums(sc * dy_c), _pair_sums(dv_s * sp), _pair_sums(dv_s * v_c),
                        _pair_sums(sp * dsa_c), _pair_sums(dv_s * sa_c))
                outs = [_put_row(o, i, val) for o, val in zip(outs, vals)]
                dv_s = dv_s * _pair_rows(w8, i) + dsa_c * _pair_rows(a8, i)
                sc = sp
            dr8, dw8, dk8, da8, db8 = outs
            for ref, o in zip((dr_ref, dw_ref, dk_ref, dv_ref, da_ref, db_ref), (dr8, dw8, dk8, dv8, da8, db8)):
                ref[rows, :] = o
            return dk_s, dv_s, sc

        dk_s, dv_s, _ = lax.fori_loop(0, cc // 8, block, (dstate_k[...], dstate_v[...], sall_ref[cc - 1]))
        dstate_k[...] = dk_s
        dstate_v[...] = dv_s

    row = pl.BlockSpec((cc, WB), lambda i: (n - 1 - i, 0))
    tile = pltpu.VMEM((4 * HD_B, LANES), F32)
    return pl.pallas_call(
        body, name=name, grid=(n,),
        in_specs=[row] * 8 + [pl.BlockSpec((cc, 4 * HD_B, LANES), lambda i: (n - 1 - i, 0, 0)),
                              pl.BlockSpec((1, 4 * HD_B, LANES), lambda i: (jnp.maximum((n - 1 - i) * cc - 1, 0), 0, 0))],
        out_specs=[row] * 6, out_shape=[jax.ShapeDtypeStruct((t, WB), F32)] * 6,
        scratch_shapes=[tile, tile], compiler_params=_params(1))(dy, r, w, k, v, a, b, sa, sall, sall)


def _rwkv_post(y, r, k2, v, g, r_k, gn_w, gn_b, name):
    t = y.shape[0]
    tb = _tile(t, 256)

    def body(y_ref, r_ref, k_ref, v_ref, g_ref, rk_ref, gw_ref, gb_ref, o_ref):
        ones = _head_ones(WB, HD_B)
        yv = y_ref[...]
        yc = yv - _split_dot(yv, ones, 3) * (1.0 / HD_B)
        rstd = lax.rsqrt(_split_dot(yc * yc, ones, 3) * (1.0 / HD_B) + GN_EPS)
        rk = _split_dot(r_ref[...] * k_ref[...] * rk_ref[...], ones, 3)
        o_ref[...] = ((yc * rstd * gw_ref[...] + gb_ref[...] + rk * v_ref[...]) * g_ref[...]).astype(MXU)

    row = pl.BlockSpec((tb, WB), lambda i: (i, 0))
    vec = pl.BlockSpec((1, WB), lambda i: (0, 0))
    return pl.pallas_call(
        body, name=name, grid=(t // tb,), in_specs=[row] * 5 + [vec] * 3, out_specs=row,
        out_shape=jax.ShapeDtypeStruct((t, WB), MXU), compiler_params=_params(1))(y, r, k2, v, g, r_k, gn_w, gn_b)


def _rwkv_post_bwd(dob, y, r, k2, v, g, r_k, gn_w, gn_b, name):
    t = y.shape[0]
    tb = _tile(t, 256)

    def body(do_ref, y_ref, r_ref, k_ref, v_ref, g_ref, rk_ref, gw_ref, gb_ref,
             dy_ref, dg_ref, dr_ref, dk_ref, dv_ref, dgw_ref, dgb_ref, drk_ref):
        @pl.when(pl.program_id(0) == 0)
        def _():
            dgw_ref[...] = jnp.zeros_like(dgw_ref)
            dgb_ref[...] = jnp.zeros_like(dgb_ref)
            drk_ref[...] = jnp.zeros_like(drk_ref)

        ones = _head_ones(WB, HD_B)
        seg = lambda x: _split_dot(x, ones, 3)
        yv, rv, kv, vv, gv = y_ref[...], r_ref[...], k_ref[...], v_ref[...], g_ref[...]
        yc = yv - seg(yv) * (1.0 / HD_B)
        rstd = lax.rsqrt(seg(yc * yc) * (1.0 / HD_B) + GN_EPS)
        yn = yc * rstd
        rk = seg(rv * kv * rk_ref[...])
        dob_v = do_ref[...]
        dg_ref[...] = dob_v * (yn * gw_ref[...] + gb_ref[...] + rk * vv)
        dyg = dob_v * gv
        dgw_ref[...] += jnp.sum(dyg * yn, axis=0, keepdims=True)
        dgb_ref[...] += jnp.sum(dyg, axis=0, keepdims=True)
        dyn = dyg * gw_ref[...]
        dy_ref[...] = rstd * (dyn - (seg(dyn) + yn * seg(dyn * yn)) * (1.0 / HD_B))
        drk = seg(dyg * vv)
        dv_ref[...] = dyg * rk
        dr_ref[...] = drk * kv * rk_ref[...]
        dk_ref[...] = drk * rv * rk_ref[...]
        drk_ref[...] += jnp.sum(drk * rv * kv, axis=0, keepdims=True)

    row = pl.BlockSpec((tb, WB), lambda i: (i, 0))
    vec = pl.BlockSpec((1, WB), lambda i: (0, 0))
    full, small = jax.ShapeDtypeStruct((t, WB), F32), jax.ShapeDtypeStruct((1, WB), F32)
    return pl.pallas_call(
        body, name=name, grid=(t // tb,), in_specs=[row] * 6 + [vec] * 3, out_specs=[row] * 5 + [vec] * 3,
        out_shape=[full] * 5 + [small] * 3, compiler_params=_params(1))(dob, y, r, k2, v, g, r_k, gn_w, gn_b)


def _rwkv_prep_bwd(grads, p_all, mu_pad, w2cat, w0, a0, k_k, k_a, name):
    t = p_all.shape[0]
    tb = _tile(t, 256)

    def body(*refs):
        g_refs, ins, outs = refs[:10], refs[10:27], refs[27:]
        dr_s, dw, dk2_s, dv_s, das, dbs, dg, dr_b, dk2_b, dv_b = (ref[...] for ref in g_refs)
        dr_ref, dk_ref, dv_ref, dlo_ref, dw2_ref, dw0_ref, da0_ref, dkk_ref, dka_ref = outs

        @pl.when(pl.program_id(0) == 0)
        def _():
            for ref in (dw2_ref, dw0_ref, da0_ref, dkk_ref, dka_ref):
                ref[...] = jnp.zeros_like(ref)

        ones = _head_ones(WB, HD_B)
        q = _rwkv_inputs(ins, pl.program_id(0) == 0, ones)
        kk_w, ka_w = ins[15][...], ins[16][...]
        a, kk, k = q["a"], q["kk"], q["k"]
        dk2 = dk2_s + dk2_b
        dkk = dbs * a - das
        da = dbs * kk + dk2 * k * ka_w
        dk = dk2 * (1.0 + (a - 1.0) * ka_w)
        dka_ref[...] += jnp.sum(dk2 * k * (a - 1.0), axis=0, keepdims=True)
        proj = jnp.where(q["nrm"] > L2_EPS, _split_dot(dkk * kk, ones, 3), 0.0)
        dkk0 = (dkk - kk * proj) / q["den"]
        dk = dk + dkk0 * kk_w
        dkk_ref[...] += jnp.sum(dkk0 * k, axis=0, keepdims=True)
        dal = da * a * (1.0 - a)
        da0_ref[...] += jnp.sum(dal, axis=0, keepdims=True)
        sg = q["sg"]
        dwl = dw * q["decay"] * (-DECAY_C) * sg * (1.0 - sg)
        dw0_ref[...] += jnp.sum(dwl, axis=0, keepdims=True)
        dlin = jnp.concatenate([dwl, dal, dg], axis=1).astype(MXU)
        dw2_ref[...] += _dot(q["z"].astype(MXU), dlin, _TN)
        dz = _dot(dlin, ins[12][...], _NT)
        dlo_ref[...] = dz * _lora_dact(q["lo"])
        dr_ref[...] = dr_s + dr_b
        dk_ref[...] = dk
        dv_ref[...] = dv_s + dv_b

    row = pl.BlockSpec((tb, WB), lambda i: (i, 0))
    vec = pl.BlockSpec((1, WB), lambda i: (0, 0))
    full, small = jax.ShapeDtypeStruct((t, WB), F32), jax.ShapeDtypeStruct((1, WB), F32)
    return pl.pallas_call(
        body, name=name, grid=(t // tb,), in_specs=[row] * 10 + _rwkv_in_specs(t, tb),
        out_specs=[row] * 3 + [pl.BlockSpec((tb, 256), lambda i: (i, 0)),
                               pl.BlockSpec((256, 3 * WB), lambda i: (0, 0))] + [vec] * 4,
        out_shape=[full] * 3 + [jax.ShapeDtypeStruct((t, 256), F32), jax.ShapeDtypeStruct((256, 3 * WB), F32)]
        + [small] * 4, compiler_params=_params(1))(*grads, *_rwkv_in_args(p_all, mu_pad, w2cat, w0, a0, k_k, k_a))


def _shift_bwd(dshifted, p_all, mu_pad, name):
    t = p_all.shape[0]
    tb = _tile(t, 256)
    nt, nt8 = t // tb, tb // 8
    widths, cols, mus = (WB, WB, WB, 256), (COL_R, COL_K, COL_V, COL_L), (0, 1, 2, 6)

    def body(*refs):
        d_refs, n_refs, p_refs, q_refs, m_refs = refs[0:4], refs[4:8], refs[8:12], refs[12:16], refs[16:20]
        o_refs, dmu_refs = refs[20:24], refs[24:28]
        i = pl.program_id(0)

        @pl.when(i == 0)
        def _():
            for ref in dmu_refs:
                ref[...] = jnp.zeros_like(ref)

        for d, nx, p, q, m, o, dmu in zip(d_refs, n_refs, p_refs, q_refs, m_refs, o_refs, dmu_refs):
            dv, pv, mu = d[...], p[...], m[...]
            o[...] = (dv * (1.0 - mu) + mu * _shift_up(dv, nx[...], i == nt - 1)).astype(MXU)
            dmu[...] += jnp.sum(dv * (_shift_down(pv, q[...], i == 0) - pv), axis=0, keepdims=True)

    cur_d = [pl.BlockSpec((tb, w), lambda i: (i, 0)) for w in widths]
    next_d = [pl.BlockSpec((8, w), lambda i: (jnp.minimum((i + 1) * nt8, t // 8 - 1), 0)) for w in widths]
    cur_p = [pl.BlockSpec((tb, w), lambda i, j=j: (i, j)) for w, j in zip(widths, cols)]
    prev_p = [pl.BlockSpec((8, w), lambda i, j=j: (jnp.maximum(i * nt8 - 1, 0), j)) for w, j in zip(widths, cols)]
    mu_s = [pl.BlockSpec((1, w), lambda i, j=j: (0, j)) for w, j in zip(widths, mus)]
    vecs = [pl.BlockSpec((1, w), lambda i: (0, 0)) for w in widths]
    return pl.pallas_call(
        body, name=name, grid=(nt,), in_specs=cur_d + next_d + cur_p + prev_p + mu_s, out_specs=cur_d + vecs,
        out_shape=[jax.ShapeDtypeStruct((t, w), MXU) for w in widths]
        + [jax.ShapeDtypeStruct((1, w), F32) for w in widths],
        compiler_params=_params(1))(*dshifted, *dshifted, *(p_all,) * 8, *(mu_pad,) * 4)


def _peer(k):
    x, y, c = (lax.axis_index(n) for n in AXES)
    px = 1 - x if k & 4 else x
    py = 1 - y if k & 2 else y
    pc = 1 - c if k & 1 else c
    return (px, py, pc), 4 * px + 2 * py + pc


def _exchange_copy(src_refs, land_refs, send_sems, recv_sems, per_peer, j, k, arriving):
    _, me = _peer(0)
    peer, idx = _peer(k)
    sem = j * (N_DEV - 1) + k - 1
    return pltpu.make_async_remote_copy(
        src_ref=src_refs[j].at[idx] if per_peer[j] else src_refs[j],
        dst_ref=land_refs[j].at[idx if arriving else me],
        send_sem=send_sems.at[sem], recv_sem=recv_sems.at[sem],
        device_id=peer, device_id_type=pl.DeviceIdType.MESH)


def _exchange_start(srcs, per_peer, name):
    n = len(srcs)
    shapes = [tuple(s.shape[1:]) if pp else tuple(s.shape) for s, pp in zip(srcs, per_peer)]
    pairs = [(j, k) for k in range(1, N_DEV) for j in range(n)]

    def body(*refs):
        src_refs, land_refs, (send_sems, recv_sems), token = refs[:n], refs[n:2 * n], refs[2 * n:2 * n + 2], refs[-1]
        for j, k in pairs:
            _exchange_copy(src_refs, land_refs, send_sems, recv_sems, per_peer, j, k, False).start()
        token[...] = jnp.zeros_like(token)

    hbm, sem = pl.BlockSpec(memory_space=pltpu.HBM), pl.BlockSpec(memory_space=pltpu.SEMAPHORE)
    lands = [lax.empty((N_DEV,) + shp, s.dtype) for shp, s in zip(shapes, srcs)]
    operands = [pltpu.with_memory_space_constraint(a, pltpu.HBM) for a in list(srcs) + lands]
    n_sems = n * (N_DEV - 1)
    out = pl.pallas_call(
        body, name=name, in_specs=[hbm] * (2 * n),
        out_specs=[sem, sem] + [hbm] * (2 * n) + [pl.BlockSpec(memory_space=pltpu.VMEM)],
        out_shape=[pltpu.SemaphoreType.DMA((n_sems,)), pltpu.SemaphoreType.DMA((n_sems,))]
        + [pltpu.HBM(a.shape, a.dtype) for a in operands] + [jax.ShapeDtypeStruct((8, LANES), F32)],
        input_output_aliases={j: 2 + j for j in range(2 * n)},
        compiler_params=pltpu.CompilerParams(has_side_effects=pltpu.SideEffectType.DATAFLOW_SIDE_EFFECTING))(*operands)
    return (out[0], out[1], out[2:2 + n], out[2 + n:2 + 2 * n], per_peer), out[-1]


def _exchange_wait(handle, after, name):
    send_sems, recv_sems, srcs, lands, per_peer = handle
    n = len(srcs)
    pairs = [(j, k) for k in range(1, N_DEV) for j in range(n)]

    def body(*refs):
        src_refs, land_refs, (send_sems, recv_sems) = refs[:n], refs[n:2 * n], refs[2 * n:2 * n + 2]
        for j, k in pairs:
            _exchange_copy(src_refs, land_refs, send_sems, recv_sems, per_peer, j, k, False).wait_send()
            _exchange_copy(src_refs, land_refs, send_sems, recv_sems, per_peer, j, k, True).wait_recv()

    hbm, sem = pl.BlockSpec(memory_space=pltpu.HBM), pl.BlockSpec(memory_space=pltpu.SEMAPHORE)
    out = pl.pallas_call(
        body, name=name, in_specs=[hbm] * (2 * n) + [sem, sem, pl.BlockSpec(memory_space=pl.ANY)],
        out_specs=[hbm] * (2 * n), out_shape=[pltpu.HBM(a.shape, a.dtype) for a in list(srcs) + list(lands)],
        input_output_aliases={j: j for j in range(2 * n)},
        compiler_params=pltpu.CompilerParams(has_side_effects=pltpu.SideEffectType.DATAFLOW_SIDE_EFFECTING))(
            *srcs, *lands, send_sems, recv_sems, after)
    return out[n:]


def _adam_update(g, w, m, v):
    c1, c2 = 1.0 - ADAM_B1 ** ADAM_STEP, 1.0 - ADAM_B2 ** ADAM_STEP
    nm = ADAM_B1 * m + (1.0 - ADAM_B1) * g
    nv = ADAM_B2 * v + (1.0 - ADAM_B2) * (g * g)
    return -ADAM_LR * ((nm / c1) / (jnp.sqrt(nv / c2) + ADAM_EPS) + ADAM_WD * w), nm, nv


def _row_tile(rows, cols):
    padded = -(-cols // LANES) * LANES
    cap = max(16, ADAM_BLOCK_BYTES // (N_DEV * padded * 4))
    best = 16
    for t in range(16, min(rows, cap) + 1, 16):
        if rows % t == 0:
            best = t
    return best


def _adamw(parts, w, m, v, name):
    _, rows, cols = w.shape
    tb = _row_tile(rows, cols)

    def body(p_ref, w_ref, m_ref, v_ref, g_ref, d_ref, nm_ref, nv_ref):
        g = p_ref[0].astype(F32)
        for d in range(1, N_DEV):
            g = g + p_ref[d].astype(F32)
        g_ref[0] = g
        d_ref[0], nm_ref[0], nv_ref[0] = _adam_update(g, w_ref[0], m_ref[0], v_ref[0])

    row = pl.BlockSpec((1, tb, cols), lambda i: (0, i, 0))
    out = jax.ShapeDtypeStruct(w.shape, F32)
    return pl.pallas_call(
        body, name=name, grid=(rows // tb,),
        in_specs=[pl.BlockSpec((N_DEV, tb, cols), lambda i: (0, i, 0)), row, row, row], out_specs=[row] * 4,
        out_shape=[out] * 4, compiler_params=_params(1))(parts, w, m, v)


def _adamw_small(parts, ws, ms, vs, name):
    n = len(ws)

    def body(*refs):
        p_ref = refs[0]
        w_refs, m_refs, v_refs = refs[1:1 + n], refs[1 + n:1 + 2 * n], refs[1 + 2 * n:1 + 3 * n]
        outs = refs[1 + 3 * n:]
        base = 0
        for j in range(n):
            rows, cols = ws[j].shape
            size = rows * cols
            for ch in range(-(-size // LANES)):
                r, c0 = divmod(ch * LANES, cols)
                width = min(LANES, cols - c0)
                g = p_ref[0, base + ch:base + ch + 1, 0:width]
                for d in range(1, N_DEV):
                    g = g + p_ref[d, base + ch:base + ch + 1, 0:width]
                at = (slice(r, r + 1), slice(c0, c0 + width))
                delta, nm, nv = _adam_update(g, w_refs[j][at], m_refs[j][at], v_refs[j][at])
                for out, val in zip((outs[j], outs[n + j], outs[2 * n + j], outs[3 * n + j]), (g, delta, nm, nv)):
                    out[at] = val
            base += -(-size // (8 * LANES)) * 8

    vmem = pl.BlockSpec(memory_space=pltpu.VMEM)
    res = pl.pallas_call(
        body, name=name, in_specs=[vmem] * (1 + 3 * n), out_specs=[vmem] * (4 * n),
        out_shape=[jax.ShapeDtypeStruct(a.shape, F32) for a in ws] * 4)(parts, *ws, *ms, *vs)
    return res[:n], res[n:2 * n], res[2 * n:3 * n], res[3 * n:]


def _rows(a, multiple):
    flat = a.reshape(-1)
    pad = -flat.shape[0] % (multiple * LANES)
    if pad:
        flat = jnp.concatenate([flat, jnp.zeros((pad,), a.dtype)])
    return flat.reshape(-1, LANES)


def _pack(arrs, multiple):
    return jnp.concatenate([_rows(a, multiple) for a in arrs], axis=0)


def _gathered_to_full(g, name, shard_shape):
    g = g.reshape((N_DEV,) + shard_shape)
    if name in COL_SHARDED:
        return jnp.transpose(g, (1, 0, 2)).reshape(shard_shape[0], N_DEV * shard_shape[1])
    return g.reshape(N_DEV * shard_shape[0], shard_shape[1])


def _full_to_per_device(full, name):
    if name in COL_SHARDED:
        r, c = full.shape
        return jnp.transpose(full.reshape(r, N_DEV, c // N_DEV), (1, 0, 2))
    return full.reshape(N_DEV, full.shape[0] // N_DEV, full.shape[1])


def _w2cat(w2, a2, g2):
    n_w, n_a, n_g = LORA
    out = jnp.zeros((256, 3 * WB), w2.dtype)
    out = out.at[0:n_w, 0:WB].set(w2)
    out = out.at[n_w:n_w + n_a, WB:2 * WB].set(a2)
    return out.at[n_w + n_a:n_w + n_a + n_g, 2 * WB:].set(g2)


class _Local:
    def __init__(self, w):
        self.w = w

    def first_weights(self):
        return self.w

    def other_weights(self, after):
        return self.w

    def send(self, grads, names):
        return ()


class _Overlapped:
    FIRST = ("ffn1_w_gate", "ffn1_w_up", "ffn1_w_down")

    def __init__(self, wts):
        x, y, c = (lax.axis_index(n) for n in AXES)
        self.wts, self.me, self.gathers, self.sends = wts, 4 * x + 2 * y + c, {}, []
        for tag, names in (("first", self.FIRST), ("others", tuple(n for n in SHARDED if n not in self.FIRST))):
            shards = [wts[n].astype(MXU) for n in names]
            handle, token = _exchange_start(shards, [False] * len(names), "gather_" + tag)
            self.gathers[tag] = (names, shards, handle, token)

    def _own_slot(self, land, mine):
        return lax.dynamic_update_slice(land, mine[None], (self.me,) + (0,) * mine.ndim)

    def _gathered(self, tag, after):
        names, shards, handle, _ = self.gathers[tag]
        lands = _exchange_wait(handle, after, "gathered_" + tag)
        return {n: _gathered_to_full(self._own_slot(land, own), n, own.shape[1:])
                for n, own, land in zip(names, shards, lands)}

    def first_weights(self):
        w = self._gathered("first", self.gathers["others"][3])
        for n in SMALL:
            keep = n in ("hgrn_lb_logits", "rwkv_r_k", "final_norm")
            w[n] = self.wts[n] if keep else self.wts[n].reshape(1, -1)
        return w

    def other_weights(self, after):
        return self._gathered("others", after)

    def send(self, grads, names, small=None):
        contrib = [_full_to_per_device(grads[n], n).astype(WIRE) for n in names]
        per_peer = [True] * len(names)
        if small is not None:
            names, contrib, per_peer = names + ("small",), contrib + [small], per_peer + [False]
        handle, token = _exchange_start(contrib, per_peer, "scatter_" + names[0])
        self.sends.append((names, contrib, per_peer, handle))
        return (token,)

    def finish(self, grads):
        token, = self.send(grads, self.FIRST, _pack([grads[n] for n in SMALL], 8))
        parts = {}
        for names, contrib, per_peer, handle in self.sends:
            lands = _exchange_wait(handle, token, "scattered_" + names[0])
            for n, own, pp, land in zip(names, contrib, per_peer, lands):
                mine = lax.dynamic_index_in_dim(own, self.me, 0, keepdims=False) if pp else own
                parts[n] = self._own_slot(land, mine)
        return parts


def _local_step(x, target, net):
    n_w, n_a, n_g = LORA
    w = dict(net.first_weights())
    h1 = _rms_fwd(x, w["ffn1_norm"], "ffn1_norm")
    x1 = _ffn_fwd(x, h1, w["ffn1_w_gate"], w["ffn1_w_up"], w["ffn1_w_down"], "ffn1_fwd")
    w.update(net.other_weights(x1))
    w_in_pad = jnp.pad(w["w_in"], ((0, 0), (0, N_INP - N_IN)))
    mu_pad = jnp.pad(w["rwkv_shift_mu"], ((0, 0), (0, 1792 - 1696)))
    w2cat = _w2cat(w["rwkv_w2"], w["rwkv_a2"], w["rwkv_g2"])
    r_k = w["rwkv_r_k"].reshape(1, WB)
    rw = (mu_pad, w2cat, w["rwkv_w0"], w["rwkv_a0"], w["rwkv_k_k"], w["rwkv_k_a"])

    h2 = _rms_fwd(x1, w["mix_norm"], "mix_norm")
    p_all = _matmul(h2, w_in_pad, name="in_proj")
    oa, oraw, states = _hgrn_fwd(p_all, w["hgrn_lb_logits"], w["hgrn_out_norm"], "hgrn_fwd")
    r, decay, k2, v, sa, sb, g = _rwkv_prep(p_all, *rw, "rwkv_prep")
    y, s_a, sall = _rwkv_scan_fwd(r, decay, k2, v, sa, sb, "rwkv_scan_fwd")
    post_w = (r_k, w["rwkv_gn_w"], w["rwkv_gn_b"])
    ob = _rwkv_post(y, r, k2, v, g, *post_w, "rwkv_post")
    o = jnp.concatenate([oa, ob], axis=1)
    x2 = _matmul(o, w["w_out"], res=x1, name="out_proj")
    h3 = _rms_fwd(x2, w["ffn2_norm"], "ffn2_norm")
    x3 = _ffn_fwd(x2, h3, w["ffn2_w_gate"], w["ffn2_w_up"], w["ffn2_w_down"], "ffn2_fwd")
    loss, dx3, d_final = _loss_head(x3, w["final_norm"].reshape(1, D), target, "loss_head")

    grads = {"final_norm": d_final.reshape(D)}

    def ffn_back(prefix, h, dy, x_in, norm):
        wg, wu, wd = (w[prefix + s] for s in ("_w_gate", "_w_up", "_w_down"))
        dh, act, dgate, dup, dout = _ffn_bwd(h, dy, wg, wu, wd, prefix + "_bwd")
        grads[prefix + "_w_gate"] = _matmul(h, dgate, ta=True, out_dtype=WIRE, name=prefix + "_dwg")
        grads[prefix + "_w_up"] = _matmul(h, dup, ta=True, out_dtype=WIRE, name=prefix + "_dwu")
        grads[prefix + "_w_down"] = _matmul(act, dout, ta=True, out_dtype=WIRE, name=prefix + "_dwd")
        dx, grads[prefix + "_norm"] = _rms_bwd(x_in, norm, dh, dy, prefix + "_norm_bwd")
        return dx

    dx2 = ffn_back("ffn2", h3, dx3, x2, w["ffn2_norm"])
    grads["w_out"] = _matmul(o, dx2, ta=True, out_dtype=WIRE, name="d_w_out")
    sent = net.send(grads, ("ffn2_w_gate", "ffn2_w_up", "ffn2_w_down", "w_out"))
    do = _matmul(dx2, w["w_out"], tb=True, after=sent, name="d_mixed")
    dqa, dfa, dia, dga, grads["hgrn_out_norm"], grads["hgrn_lb_logits"] = _hgrn_bwd(
        p_all, w["hgrn_lb_logits"], w["hgrn_out_norm"], oraw, states, do[:, :WA], "hgrn_bwd")
    dy, dg, dr_b, dk2_b, dv_b, grads["rwkv_gn_w"], grads["rwkv_gn_b"], d_rk = _rwkv_post_bwd(
        do[:, WA:], y, r, k2, v, g, *post_w, "rwkv_post_bwd")
    grads["rwkv_r_k"] = d_rk.reshape(w["rwkv_r_k"].shape)
    dr, dw, dk2, dv, dsa, dsb = _rwkv_scan_bwd(dy, r, decay, k2, v, sa, sb, s_a, sall, "rwkv_scan_bwd")
    (dsr, dsk, dsv, dslo, dw2cat, grads["rwkv_w0"], grads["rwkv_a0"], grads["rwkv_k_k"],
     grads["rwkv_k_a"]) = _rwkv_prep_bwd((dr, dw, dk2, dv, dsa, dsb, dg, dr_b, dk2_b, dv_b), p_all, *rw,
                                         "rwkv_prep_bwd")
    grads["rwkv_w2"] = dw2cat[0:n_w, 0:WB]
    grads["rwkv_a2"] = dw2cat[n_w:n_w + n_a, WB:2 * WB]
    grads["rwkv_g2"] = dw2cat[n_w + n_a:n_w + n_a + n_g, 2 * WB:]
    dpr, dpk, dpv, dplo, dmu_r, dmu_k, dmu_v, dmu_lo = _shift_bwd((dsr, dsk, dsv, dslo), p_all, mu_pad, "shift_bwd")
    grads["rwkv_shift_mu"] = jnp.concatenate([dmu_r, dmu_k, dmu_v, dmu_lo], axis=1)[:, :1696]
    dp = jnp.concatenate([dqa, dfa, dia, dga, dpr, dpk, dpv, dplo], axis=1)
    grads["w_in"] = _matmul(h2, dp, ta=True, out_dtype=WIRE, name="d_w_in")[:, :N_IN]
    sent = net.send(grads, ("w_in", "rwkv_w2", "rwkv_a2", "rwkv_g2"))
    dh2 = _matmul(dp, w_in_pad, tb=True, after=sent, name="d_h2")
    dx1, grads["mix_norm"] = _rms_bwd(x1, w["mix_norm"], dh2, dx2, "mix_norm_bwd")
    dx0 = ffn_back("ffn1", h1, dx1, x, w["ffn1_norm"])
    return loss[0, 0], dx0, grads


def kernel(x, ffn1_norm, ffn1_w_gate, ffn1_w_up, ffn1_w_down, mix_norm, w_in, hgrn_lb_logits, hgrn_out_norm, rwkv_shift_mu, rwkv_w0, rwkv_w2, rwkv_a0, rwkv_a2, rwkv_g2, rwkv_k_k, rwkv_k_a, rwkv_r_k, rwkv_gn_w, rwkv_gn_b, w_out, ffn2_norm, ffn2_w_gate, ffn2_w_up, ffn2_w_down, final_norm, loss_target, m_ffn1_norm, m_ffn1_w_gate, m_ffn1_w_up, m_ffn1_w_down, m_mix_norm, m_w_in, m_hgrn_lb_logits, m_hgrn_out_norm, m_rwkv_shift_mu, m_rwkv_w0, m_rwkv_w2, m_rwkv_a0, m_rwkv_a2, m_rwkv_g2, m_rwkv_k_k, m_rwkv_k_a, m_rwkv_r_k, m_rwkv_gn_w, m_rwkv_gn_b, m_w_out, m_ffn2_norm, m_ffn2_w_gate, m_ffn2_w_up, m_ffn2_w_down, m_final_norm, v_ffn1_norm, v_ffn1_w_gate, v_ffn1_w_up, v_ffn1_w_down, v_mix_norm, v_w_in, v_hgrn_lb_logits, v_hgrn_out_norm, v_rwkv_shift_mu, v_rwkv_w0, v_rwkv_w2, v_rwkv_a0, v_rwkv_a2, v_rwkv_g2, v_rwkv_k_k, v_rwkv_k_a, v_rwkv_r_k, v_rwkv_gn_w, v_rwkv_gn_b, v_w_out, v_ffn2_norm, v_ffn2_w_gate, v_ffn2_w_up, v_ffn2_w_down, v_final_norm):
    args = dict(locals())
    wts = {n: args[n] for n in WEIGHTS}
    mom = {n: args["m_" + n] for n in WEIGHTS}
    var = {n: args["v_" + n] for n in WEIGHTS}
    net = _Overlapped(wts)
    loss, grad_x, grads = _local_step(x[0], loss_target[0], net)
    loss = lax.psum(loss, AXES)
    parts = net.finish(grads)

    new = {}
    for n in SHARDED:
        new[n] = _adamw(parts[n], wts[n], mom[n], var[n], "adamw_" + n)
    two_d = lambda a: a if a.ndim == 2 else a.reshape(1, -1)
    small = _adamw_small(parts["small"], *([two_d(src[n]) for n in SMALL] for src in (wts, mom, var)), "adamw_small")
    for j, n in enumerate(SMALL):
        new[n] = [res[j].reshape(wts[n].shape) for res in small]
    return (loss, grad_x[None], *[new[n][0] for n in WEIGHTS], *[new[n][1] for n in WEIGHTS],
            *[new[n][2] for n in WEIGHTS], *[new[n][3] for n in WEIGHTS])
```

```python
import functools
import math

import jax
import jax.numpy as jnp
from jax import lax
from jax.experimental import pallas as pl
from jax.experimental.pallas import tpu as pltpu

F32 = jnp.float32
MXU = jnp.bfloat16
WIRE = jnp.bfloat16
D = 1024
FF = 2816
WA = 512
WB = 512
HD_B = 64
N_IN = 3744
N_INP = 3840
COL_R, COL_K, COL_V = 4, 5, 6
COL_L = 14
LORA = (32, 32, 96)
HG_CHUNK = 64
SCAN_CHUNK = 64
NORM_EPS = 1e-6
GN_EPS = 64e-5
L2_EPS = 1e-12
DECAY_C = math.exp(-0.5)
N_DEV = 8
LANES = 128
ADAM_BLOCK_BYTES = 4 * 1024 * 1024
MATMUL_BLOCK_BYTES = 40 * 1024 * 1024
VMEM_LIMIT = 56 * 1024 * 1024
ADAM_LR, ADAM_B1, ADAM_B2, ADAM_EPS, ADAM_WD, ADAM_STEP = 0.001, 0.9, 0.999, 1e-08, 0.01, 10
AXES = ("x", "y", "c")

SHARDED = ("ffn1_w_gate", "ffn1_w_up", "ffn1_w_down", "w_in", "rwkv_w2", "rwkv_a2", "rwkv_g2", "w_out",
           "ffn2_w_gate", "ffn2_w_up", "ffn2_w_down")
COL_SHARDED = {"ffn1_w_gate", "ffn1_w_up", "w_in", "rwkv_w2", "rwkv_a2", "rwkv_g2", "ffn2_w_gate", "ffn2_w_up"}
SMALL = ("ffn1_norm", "mix_norm", "hgrn_lb_logits", "hgrn_out_norm", "rwkv_shift_mu", "rwkv_w0", "rwkv_a0",
         "rwkv_k_k", "rwkv_k_a", "rwkv_r_k", "rwkv_gn_w", "rwkv_gn_b", "ffn2_norm", "final_norm")
WEIGHTS = ("ffn1_norm", "ffn1_w_gate", "ffn1_w_up", "ffn1_w_down", "mix_norm", "w_in", "hgrn_lb_logits",
           "hgrn_out_norm", "rwkv_shift_mu", "rwkv_w0", "rwkv_w2", "rwkv_a0", "rwkv_a2", "rwkv_g2", "rwkv_k_k",
           "rwkv_k_a", "rwkv_r_k", "rwkv_gn_w", "rwkv_gn_b", "w_out", "ffn2_norm", "ffn2_w_gate", "ffn2_w_up",
           "ffn2_w_down", "final_norm")


def _tile(n, cap):
    if n <= cap:
        return n
    for t in range(cap - cap % LANES, 0, -LANES):
        if n % t == 0:
            return t
    raise ValueError((n, cap))


def _params(n_axes):
    return pltpu.CompilerParams(dimension_semantics=("arbitrary",) * n_axes, vmem_limit_bytes=VMEM_LIMIT)


def _sig(x):
    return jax.nn.sigmoid(x)


def _dsilu(z, s):
    return s * (1.0 + z * (1.0 - s))


def _dot(a, b, dims=((1,), (0,)), precision=None):
    return lax.dot_general(a, b, (dims, ((), ())), preferred_element_type=F32, precision=precision)


_NT = ((1,), (1,))
_TN = ((0,), (0,))
_HI = lax.Precision.HIGHEST


def _iota(shape, dim):
    return lax.broadcasted_iota(jnp.int32, shape, dim)


def _split_dot(x, ones, passes):
    hi = x.astype(jnp.bfloat16)
    acc = _dot(hi, ones)
    rem = x
    for _ in range(passes - 1):
        rem = rem - hi.astype(F32)
        hi = rem.astype(jnp.bfloat16)
        acc = acc + _dot(hi, ones)
    return acc


def _head_ones(n, width):
    shift = width.bit_length() - 1
    return (_iota((n, n), 0) >> shift == _iota((n, n), 1) >> shift).astype(jnp.bfloat16)


def _matmul(a, b, *, ta=False, tb=False, out_dtype=F32, res=None, after=(), name):
    m, k = (a.shape[1], a.shape[0]) if ta else a.shape
    n = b.shape[0] if tb else b.shape[1]
    tm, tn = _tile(m, 1408), _tile(n, 1408)
    in_bytes = max(a.dtype.itemsize, b.dtype.itemsize)
    for tk in (_tile(k, 1024), _tile(k, 512), _tile(k, 256)):
        if 2 * (tm + tn) * tk * in_bytes + 3 * tm * tn * 4 <= MATMUL_BLOCK_BYTES:
            break
    nk = k // tk
    dims = ((0 if ta else 1,), (1 if tb else 0,))

    def body(*refs):
        a_ref, b_ref = refs[:2]
        o_ref, acc = refs[-2:]
        kk = pl.program_id(2)

        @pl.when(kk == 0)
        def _():
            acc[...] = jnp.zeros_like(acc)

        acc[...] += _dot(a_ref[...].astype(MXU), b_ref[...].astype(MXU), dims)

        @pl.when(kk == nk - 1)
        def _():
            v = acc[...]
            if res is not None:
                v = v + refs[2][...]
            o_ref[...] = v.astype(out_dtype)

    a_spec = pl.BlockSpec((tk, tm), lambda i, j, kk: (kk, i)) if ta else pl.BlockSpec((tm, tk), lambda i, j, kk: (i, kk))
    b_spec = pl.BlockSpec((tn, tk), lambda i, j, kk: (j, kk)) if tb else pl.BlockSpec((tk, tn), lambda i, j, kk: (kk, j))
    o_spec = pl.BlockSpec((tm, tn), lambda i, j, kk: (i, j))
    ins, specs = [a, b], [a_spec, b_spec]
    if res is not None:
        ins.append(res)
        specs.append(o_spec)
    ins += list(after)
    specs += [pl.BlockSpec(memory_space=pl.ANY)] * len(after)
    return pl.pallas_call(
        body, name=name, grid=(m // tm, n // tn, nk), in_specs=specs, out_specs=o_spec,
        out_shape=jax.ShapeDtypeStruct((m, n), out_dtype), scratch_shapes=[pltpu.VMEM((tm, tn), F32)],
        compiler_params=_params(3))(*ins)


def _rms_fwd(x, g, name):
    t = x.shape[0]
    tb = _tile(t, 512)

    def body(x_ref, g_ref, o_ref):
        xv = x_ref[...]
        rinv = lax.rsqrt(jnp.mean(xv * xv, axis=-1, keepdims=True) + NORM_EPS)
        o_ref[...] = (xv * rinv * g_ref[...]).astype(MXU)

    return pl.pallas_call(
        body, name=name, grid=(t // tb,),
        in_specs=[pl.BlockSpec((tb, D), lambda i: (i, 0)), pl.BlockSpec((1, D), lambda i: (0, 0))],
        out_specs=pl.BlockSpec((tb, D), lambda i: (i, 0)), out_shape=jax.ShapeDtypeStruct((t, D), MXU),
        compiler_params=_params(1))(x, g)


def _rms_bwd(x, g, dh, dres, name, after=()):
    t = x.shape[0]
    tb = _tile(t, 512)

    def body(x_ref, g_ref, dh_ref, dres_ref, *rest):
        dx_ref, dg_ref = rest[-2:]

        @pl.when(pl.program_id(0) == 0)
        def _():
            dg_ref[...] = jnp.zeros_like(dg_ref)

        xv = x_ref[...]
        rinv = lax.rsqrt(jnp.mean(xv * xv, axis=-1, keepdims=True) + NORM_EPS)
        xhat = xv * rinv
        dhv = dh_ref[...]
        dg_ref[...] += jnp.sum(dhv * xhat, axis=0, keepdims=True)
        dxhat = dhv * g_ref[...]
        dx_ref[...] = dres_ref[...] + rinv * (dxhat - xhat * jnp.mean(dxhat * xhat, axis=-1, keepdims=True))

    row = pl.BlockSpec((tb, D), lambda i: (i, 0))
    vec = pl.BlockSpec((1, D), lambda i: (0, 0))
    return pl.pallas_call(
        body, name=name, grid=(t // tb,),
        in_specs=[row, vec, row, row] + [pl.BlockSpec(memory_space=pl.ANY)] * len(after), out_specs=[row, vec],
        out_shape=[jax.ShapeDtypeStruct((t, D), F32), jax.ShapeDtypeStruct((1, D), F32)],
        compiler_params=_params(1))(x, g, dh, dres, *after)


def _loss_head(x, g, target, name):
    t = x.shape[0]
    tb = _tile(t, 512)

    def body(x_ref, g_ref, t_ref, loss_ref, dx_ref, dg_ref):
        @pl.when(pl.program_id(0) == 0)
        def _():
            dg_ref[...] = jnp.zeros_like(dg_ref)
            loss_ref[...] = jnp.zeros_like(loss_ref)

        xv = x_ref[...]
        gv = g_ref[...]
        rinv = lax.rsqrt(jnp.mean(xv * xv, axis=-1, keepdims=True) + NORM_EPS)
        xhat = xv * rinv
        err = xhat * gv - t_ref[...]
        per_tok = jnp.mean(err * err, axis=-1, keepdims=True)
        loss_ref[...] += jnp.broadcast_to(0.5 * jnp.sum(per_tok, axis=0, keepdims=True), loss_ref.shape)
        dy = err * (1.0 / D)
        dg_ref[...] += jnp.sum(dy * xhat, axis=0, keepdims=True)
        dxhat = dy * gv
        dx_ref[...] = rinv * (dxhat - xhat * jnp.mean(dxhat * xhat, axis=-1, keepdims=True))

    row = pl.BlockSpec((tb, D), lambda i: (i, 0))
    vec = pl.BlockSpec((1, D), lambda i: (0, 0))
    return pl.pallas_call(
        body, name=name, grid=(t // tb,), in_specs=[row, vec, row],
        out_specs=[pl.BlockSpec((1, LANES), lambda i: (0, 0)), row, vec],
        out_shape=[jax.ShapeDtypeStruct((1, LANES), F32), jax.ShapeDtypeStruct((t, D), F32),
                   jax.ShapeDtypeStruct((1, D), F32)],
        compiler_params=_params(1))(x, g, target)


def _ffn_fwd(x, h, wg, wu, wd, name):
    t = x.shape[0]
    tb, fb = _tile(t, 1024), 256
    nf = FF // fb

    def body(x_ref, h_ref, wg_ref, wu_ref, wd_ref, o_ref, acc):
        f = pl.program_id(1)

        @pl.when(f == 0)
        def _():
            acc[...] = jnp.zeros_like(acc)

        hv = h_ref[...]
        gate = _dot(hv, wg_ref[...])
        up = _dot(hv, wu_ref[...])
        act = (gate * _sig(gate) * up).astype(MXU)
        acc[...] += _dot(act, wd_ref[...])

        @pl.when(f == nf - 1)
        def _():
            o_ref[...] = x_ref[...] + 0.5 * acc[...]

    row = pl.BlockSpec((tb, D), lambda i, f: (i, 0))
    col = pl.BlockSpec((D, fb), lambda i, f: (0, f))
    return pl.pallas_call(
        body, name=name, grid=(t // tb, nf),
        in_specs=[row, row, col, col, pl.BlockSpec((fb, D), lambda i, f: (f, 0))], out_specs=row,
        out_shape=jax.ShapeDtypeStruct((t, D), F32), scratch_shapes=[pltpu.VMEM((tb, D), F32)],
        compiler_params=_params(2))(x, h, wg, wu, wd)


def _ffn_bwd(h, dy, wg, wu, wd, name):
    t = h.shape[0]
    tb, fb = _tile(t, 1024), 256
    nf = FF // fb

    def body(h_ref, dy_ref, wg_ref, wu_ref, wd_ref, dh_ref, act_ref, dg_ref, du_ref, dout_ref, acc):
        f = pl.program_id(1)

        @pl.when(f == 0)
        def _():
            acc[...] = jnp.zeros_like(acc)

        hv = h_ref[...]
        dout = (0.5 * dy_ref[...]).astype(MXU)
        dout_ref[...] = dout
        gate = _dot(hv, wg_ref[...])
        up = _dot(hv, wu_ref[...])
        dact = _dot(dout, wd_ref[...], _NT)
        s = _sig(gate)
        silu = gate * s
        act_ref[...] = (silu * up).astype(MXU)
        dup = (dact * silu).astype(MXU)
        dgate = (dact * up * _dsilu(gate, s)).astype(MXU)
        du_ref[...] = dup
        dg_ref[...] = dgate
        acc[...] += _dot(dgate, wg_ref[...], _NT) + _dot(dup, wu_ref[...], _NT)

        @pl.when(f == nf - 1)
        def _():
            dh_ref[...] = acc[...]

    row = pl.BlockSpec((tb, D), lambda i, f: (i, 0))
    col = pl.BlockSpec((D, fb), lambda i, f: (0, f))
    hid = pl.BlockSpec((tb, fb), lambda i, f: (i, f))
    hid_shape = jax.ShapeDtypeStruct((t, FF), MXU)
    return pl.pallas_call(
        body, name=name, grid=(t // tb, nf),
        in_specs=[row, row, col, col, pl.BlockSpec((fb, D), lambda i, f: (f, 0))],
        out_specs=[row, hid, hid, hid, row],
        out_shape=[jax.ShapeDtypeStruct((t, D), F32), hid_shape, hid_shape, hid_shape,
                   jax.ShapeDtypeStruct((t, D), MXU)],
        scratch_shapes=[pltpu.VMEM((tb, D), F32)], compiler_params=_params(2))(h, dy, wg, wu, wd)


def _hgrn_chunk(qa, fa, lbl):
    c = HG_CHUNK
    lb = _sig(lbl[0:1, :] - lbl[1:2, :])
    sf = _sig(fa)
    forget = lb + (1.0 - lb) * sf
    kh = 1.0 - forget
    row, col = _iota((c, c), 0), _iota((c, c), 1)
    b = _dot((col <= row).astype(F32), jnp.log(forget), precision=_HI)
    bref, blast = b[c // 2:c // 2 + 1, :], b[c - 1:c, :]
    sq = _sig(qa)
    q = qa * sq
    qt, kt = q * jnp.exp(b - bref), kh * jnp.exp(bref - b)
    qb, kl = q * jnp.exp(b), kh * jnp.exp(blast - b)
    causal = col <= row
    return dict(lb=lb, sf=sf, forget=forget, sq=sq, qt=qt, kt=kt, qb=qb, kl=kl, decay=jnp.exp(blast),
                causal=causal, e_q=jnp.exp(b), e_qt=jnp.exp(b - bref), e_kt=jnp.exp(bref - b),
                e_kl=jnp.exp(blast - b))


def _hgrn_specs(t):
    c = HG_CHUNK
    return c, t // c, WA // LANES


def _hgrn_fwd(p_all, lbl, onorm, name):
    t = p_all.shape[0]
    c, n, nh = _hgrn_specs(t)

    def body(q_ref, f_ref, i_ref, g_ref, lbl_ref, on_ref, oa_ref, oraw_ref, st_ref, state):
        @pl.when(pl.program_id(0) == 0)
        def _():
            state[...] = jnp.zeros_like(state)

        for h in range(nh):
            at = slice(h * LANES, (h + 1) * LANES)
            k = _hgrn_chunk(q_ref[:, at], f_ref[:, at], lbl_ref[:, at])
            v = i_ref[:, at]
            st = state[h]
            st_ref[h, 0] = st
            a = jnp.where(k["causal"], _dot(k["qt"], k["kt"], _NT, _HI), 0.0)
            o = _dot(a, v, precision=_HI) + _dot(k["qb"], st, _NT, _HI)
            state[h] = st * k["decay"] + _dot(v, k["kl"], _TN, _HI)
            oraw_ref[:, at] = o
            rinv = lax.rsqrt(jnp.mean(o * o, axis=-1, keepdims=True) + NORM_EPS)
            ga = g_ref[:, at]
            oa_ref[:, at] = (o * rinv * on_ref[:, at] * (ga * _sig(ga))).astype(MXU)

    def blk(j):
        return pl.BlockSpec((c, WA), lambda i: (i, j))

    return pl.pallas_call(
        body, name=name, grid=(n,),
        in_specs=[blk(0), blk(1), blk(2), blk(3), pl.BlockSpec((2, WA), lambda i: (0, 0)),
                  pl.BlockSpec((1, WA), lambda i: (0, 0))],
        out_specs=[blk(0), blk(0), pl.BlockSpec((nh, 1, LANES, LANES), lambda i: (0, i, 0, 0))],
        out_shape=[jax.ShapeDtypeStruct((t, WA), MXU), jax.ShapeDtypeStruct((t, WA), F32),
                   jax.ShapeDtypeStruct((nh, n, LANES, LANES), F32)],
        scratch_shapes=[pltpu.VMEM((nh, LANES, LANES), F32)], compiler_params=_params(1))(
            p_all, p_all, p_all, p_all, lbl, onorm)


def _hgrn_bwd(p_all, lbl, onorm, oraw, states, doa, name):
    t = p_all.shape[0]
    c, n, nh = _hgrn_specs(t)

    def body(q_ref, f_ref, i_ref, g_ref, lbl_ref, on_ref, oraw_ref, st_ref, doa_ref,
             dq_ref, df_ref, di_ref, dg_ref, don_ref, dlbl_ref, dstate, dlb):
        @pl.when(pl.program_id(0) == 0)
        def _():
            dstate[...] = jnp.zeros_like(dstate)
            dlb[...] = jnp.zeros_like(dlb)
            don_ref[...] = jnp.zeros_like(don_ref)

        for h in range(nh):
            at = slice(h * LANES, (h + 1) * LANES)
            qa, fa, v, ga = q_ref[:, at], f_ref[:, at], i_ref[:, at], g_ref[:, at]
            k = _hgrn_chunk(qa, fa, lbl_ref[:, at])
            st, dst_next = st_ref[h, 0], dstate[h]
            o = oraw_ref[:, at]
            gain = on_ref[:, at]
            rinv = lax.rsqrt(jnp.mean(o * o, axis=-1, keepdims=True) + NORM_EPS)
            on = o * rinv
            sg = _sig(ga)
            gate = ga * sg
            dout = doa_ref[:, at]
            don_ref[:, at] += jnp.sum(dout * on * gate, axis=0, keepdims=True)
            dg_ref[:, at] = (dout * on * gain * _dsilu(ga, sg)).astype(MXU)
            d_on = dout * gain * gate
            do = rinv * (d_on - on * jnp.mean(d_on * on, axis=-1, keepdims=True))

            a = jnp.where(k["causal"], _dot(k["qt"], k["kt"], _NT, _HI), 0.0)
            dqb = _dot(do, st, precision=_HI)
            dstate[h] = dst_next * k["decay"] + _dot(do, k["qb"], _TN, _HI)
            da = jnp.where(k["causal"], _dot(do, v, _NT, _HI), 0.0)
            dqt = _dot(da, k["kt"], precision=_HI)
            dkt = _dot(da, k["qt"], _TN, _HI)
            dv = _dot(a, do, _TN, _HI) + _dot(k["kl"], dst_next, _NT, _HI)
            dkl = _dot(v, dst_next, precision=_HI)
            ddecay = jnp.sum(dst_next * st, axis=0, keepdims=True)
            dq = dqb * k["e_q"] + dqt * k["e_qt"]
            dk = dkt * k["e_kt"] + dkl * k["e_kl"]
            tq, tk, tl = dqt * k["qt"], dkt * k["kt"], dkl * k["kl"]
            db = dqb * k["qb"] + tq - tk - tl
            dbref = jnp.sum(tk - tq, axis=0, keepdims=True)
            dblast = jnp.sum(tl, axis=0, keepdims=True) + ddecay * k["decay"]
            rows = _iota((c, LANES), 0)
            db = db + jnp.where(rows == c // 2, dbref, 0.0) + jnp.where(rows == c - 1, dblast, 0.0)
            row, col = _iota((c, c), 0), _iota((c, c), 1)
            dlogf = _dot((col >= row).astype(F32), db, precision=_HI)
            dq_ref[:, at] = (dq * _dsilu(qa, k["sq"])).astype(MXU)
            di_ref[:, at] = dv.astype(MXU)
            dforget = dlogf / k["forget"] - dk
            sf, lb = k["sf"], k["lb"]
            df_ref[:, at] = (dforget * (1.0 - lb) * sf * (1.0 - sf)).astype(MXU)
            dlb[:, at] += jnp.sum(dforget * (1.0 - sf), axis=0, keepdims=True)
            dl0 = dlb[:, at] * lb * (1.0 - lb)
            dlbl_ref[:, at] = jnp.where(_iota((2, LANES), 0) == 0, dl0, -dl0)

    def blk(j):
        return pl.BlockSpec((c, WA), lambda i: (n - 1 - i, j))

    vec = pl.BlockSpec((1, WA), lambda i: (0, 0))
    lg = pl.BlockSpec((2, WA), lambda i: (0, 0))
    grad = jax.ShapeDtypeStruct((t, WA), MXU)
    return pl.pallas_call(
        body, name=name, grid=(n,),
        in_specs=[blk(0), blk(1), blk(2), blk(3), lg, vec, blk(0),
                  pl.BlockSpec((nh, 1, LANES, LANES), lambda i: (0, n - 1 - i, 0, 0)), blk(0)],
        out_specs=[blk(0), blk(0), blk(0), blk(0), vec, lg],
        out_shape=[grad, grad, grad, grad, jax.ShapeDtypeStruct((1, WA), F32), jax.ShapeDtypeStruct((2, WA), F32)],
        scratch_shapes=[pltpu.VMEM((nh, LANES, LANES), F32), pltpu.VMEM((1, WA), F32)],
        compiler_params=_params(1))(p_all, p_all, p_all, p_all, lbl, onorm, oraw, states, doa)


def _lora_act(x):
    lane = _iota(x.shape, 1)
    n_w, n_a, n_g = LORA
    return jnp.where(lane < n_w, jnp.tanh(x),
                     jnp.where(lane < n_w + n_a, x, jnp.where(lane < n_w + n_a + n_g, _sig(x), 0.0)))


def _lora_dact(x):
    lane = _iota(x.shape, 1)
    n_w, n_a, n_g = LORA
    th, s = jnp.tanh(x), _sig(x)
    return jnp.where(lane < n_w, 1.0 - th * th,
                     jnp.where(lane < n_w + n_a, 1.0, jnp.where(lane < n_w + n_a + n_g, s * (1.0 - s), 0.0)))


def _shift_down(cur, prev8, first):
    rolled = pltpu.roll(cur, 1, 0)
    edge = prev8[7:8, :] * jnp.where(first, 0.0, 1.0)
    return jnp.where(_iota(cur.shape, 0) == 0, edge, rolled)


def _shift_up(cur, next8, last):
    rows = cur.shape[0]
    rolled = pltpu.roll(cur, rows - 1, 0)
    edge = next8[0:1, :] * jnp.where(last, 0.0, 1.0)
    return jnp.where(_iota(cur.shape, 0) == rows - 1, edge, rolled)


def _rwkv_inputs(refs, first, ones):
    (pr, pk, pv, plo, qr, qk, qv, qlo, mr, mk, mv, mlo, w2c, w0, a0, kk_w, ka_w) = refs
    mix = lambda cur, prev, mu: cur[...] + mu[...] * (_shift_down(cur[...], prev[...], first) - cur[...])
    r, k, v, lo = mix(pr, qr, mr), mix(pk, qk, mk), mix(pv, qv, mv), mix(plo, qlo, mlo)
    z = _lora_act(lo)
    lin = _dot(z.astype(MXU), w2c[...])
    sg = _sig(w0[...] + lin[:, :WB])
    decay = jnp.exp(-DECAY_C * sg)
    a = _sig(a0[...] + lin[:, WB:2 * WB])
    g = lin[:, 2 * WB:]
    kk0 = k * kk_w[...]
    nrm = jnp.sqrt(_split_dot(kk0 * kk0, ones, 3))
    den = jnp.maximum(nrm, L2_EPS)
    kk = kk0 / den
    k2 = k * (1.0 + (a - 1.0) * ka_w[...])
    return dict(r=r, k=k, v=v, lo=lo, z=z, sg=sg, decay=decay, a=a, g=g, kk=kk, den=den, nrm=nrm, k2=k2)


def _rwkv_in_specs(t, tb):
    nt8 = tb // 8

    def cur(w, j):
        return pl.BlockSpec((tb, w), lambda i: (i, j))

    def prev(w, j):
        return pl.BlockSpec((8, w), lambda i: (jnp.maximum(i * nt8 - 1, 0), j))

    def vec(w, j=0):
        return pl.BlockSpec((1, w), lambda i: (0, j))

    return [cur(WB, COL_R), cur(WB, COL_K), cur(WB, COL_V), cur(256, COL_L),
            prev(WB, COL_R), prev(WB, COL_K), prev(WB, COL_V), prev(256, COL_L),
            vec(WB, 0), vec(WB, 1), vec(WB, 2), vec(256, 6),
            pl.BlockSpec((256, 3 * WB), lambda i: (0, 0)), vec(WB), vec(WB), vec(WB), vec(WB)]


def _rwkv_in_args(p_all, mu_pad, w2cat, w0, a0, k_k, k_a):
    return (p_all,) * 8 + (mu_pad,) * 4 + (w2cat, w0, a0, k_k, k_a)


def _rwkv_prep(p_all, mu_pad, w2cat, w0, a0, k_k, k_a, name):
    t = p_all.shape[0]
    tb = _tile(t, 256)

    def body(*refs):
        ins, outs = refs[:17], refs[17:]
        q = _rwkv_inputs(ins, pl.program_id(0) == 0, _head_ones(WB, HD_B))
        for ref, val in zip(outs, (q["r"], q["decay"], q["k2"], q["v"], -q["kk"], q["kk"] * q["a"], q["g"])):
            ref[...] = val

    out = pl.BlockSpec((tb, WB), lambda i: (i, 0))
    return pl.pallas_call(
        body, name=name, grid=(t // tb,), in_specs=_rwkv_in_specs(t, tb), out_specs=[out] * 7,
        out_shape=[jax.ShapeDtypeStruct((t, WB), F32)] * 7, compiler_params=_params(1))(
            *_rwkv_in_args(p_all, mu_pad, w2cat, w0, a0, k_k, k_a))


def _pair_rows(x8, i):
    return jnp.concatenate([jnp.broadcast_to(x8[i:i + 1, p * LANES:(p + 1) * LANES], (HD_B, LANES))
                            for p in range(4)], axis=0)


def _pair_sums(x):
    return jnp.concatenate([jnp.sum(x[p * HD_B:(p + 1) * HD_B], axis=0, keepdims=True) for p in range(4)], axis=1)


def _put_row(buf, i, row):
    return jnp.where(_iota(buf.shape, 0) == i, row, buf)


def _pieces(x):
    hi = x.astype(jnp.bfloat16).astype(F32)
    lo = (x - hi).astype(jnp.bfloat16).astype(F32)
    upper = (_iota((x.shape[0], LANES), 1) & (HD_B // 2)) != 0
    swapped = [jnp.where(upper, pltpu.roll(lo[:, p * LANES:(p + 1) * LANES], HD_B // 2, 1),
                         pltpu.roll(lo[:, p * LANES:(p + 1) * LANES], LANES - HD_B // 2, 1)) for p in range(4)]
    return hi, jnp.concatenate(swapped, axis=1)


def _scan_consts():
    row, lane = _iota((HD_B, LANES), 0), _iota((HD_B, LANES), 1) & (HD_B - 1)
    return (row == lane).astype(jnp.bfloat16), (row == lane ^ (HD_B // 2)).astype(jnp.bfloat16), _head_ones(LANES, HD_B)


def _pair_cols(many, consts):
    diag_hi, diag_lo, ones = consts
    tiles = []
    for (hi8, lo8), i in many:
        for p in range(4):
            lanes = slice(p * LANES, (p + 1) * LANES)
            hi = jnp.broadcast_to(hi8[i:i + 1, lanes], (16, LANES)).astype(jnp.bfloat16)
            lo = jnp.broadcast_to(lo8[i:i + 1, lanes], (16, LANES)).astype(jnp.bfloat16)
            for g in range(HD_B // 16):
                rows = slice(g * 16, (g + 1) * 16)
                tiles.append(hi * diag_hi[rows] + lo * diag_lo[rows])
    out = _dot(jnp.concatenate(tiles, axis=0), ones)
    return [out[m * 4 * HD_B:(m + 1) * 4 * HD_B] for m in range(len(many))]


def _rwkv_scan_fwd(r, w, k, v, a, b, name):
    t = r.shape[0]
    cc = min(t, SCAN_CHUNK)

    def body(r_ref, w_ref, k_ref, v_ref, a_ref, b_ref, y_ref, sa_ref, sall_ref, state_k, state_v):
        @pl.when(pl.program_id(0) == 0)
        def _():
            state_k[...] = jnp.zeros_like(state_k)
            state_v[...] = jnp.zeros_like(state_v)

        consts = _scan_consts()

        def block(j, carry):
            sk, sv = carry
            base = pl.multiple_of(j * 8, 8)
            rows = pl.ds(base, 8)
            r8, w8, k8, v8, a8, b8 = (ref[rows, :] for ref in (r_ref, w_ref, k_ref, v_ref, a_ref, b_ref))
            rp, wp, kp, vp, ap, bp = (_pieces(x) for x in (r8, w8, k8, v8, a8, b8))
            y8 = jnp.zeros((8, WB), F32)
            sa8 = jnp.zeros((8, WB), F32)
            cols = _pair_cols([(x, i) for i in range(8) for x in (ap, wp, bp, kp, rp, vp)], consts)
            for i in range(8):
                a_c, w_c, b_c, k_c, r_c = cols[6 * i:6 * i + 5]
                sa = _pair_sums(sk * a_c)
                sk = sk * w_c + b_c * _pair_rows(sa, 0) + k_c * _pair_rows(v8, i)
                y8 = _put_row(y8, i, _pair_sums(sk * r_c))
                sa8 = _put_row(sa8, i, sa)
            sa_cols = _pair_cols([(_pieces(sa8), i) for i in range(8)], consts)
            for i in range(8):
                sv = sv * _pair_rows(w8, i) + sa_cols[i] * _pair_rows(b8, i) + cols[6 * i + 5] * _pair_rows(k8, i)
                sall_ref[base + i] = sv
            y_ref[rows, :] = y8
            sa_ref[rows, :] = sa8
            return sk, sv

        state_k[...], state_v[...] = lax.fori_loop(0, cc // 8, block, (state_k[...], state_v[...]))

    row = pl.BlockSpec((cc, WB), lambda i: (i, 0))
    tile = pltpu.VMEM((4 * HD_B, LANES), F32)
    return pl.pallas_call(
        body, name=name, grid=(t // cc,), in_specs=[row] * 6,
        out_specs=[row, row, pl.BlockSpec((cc, 4 * HD_B, LANES), lambda i: (i, 0, 0))],
        out_shape=[jax.ShapeDtypeStruct((t, WB), F32)] * 2 + [jax.ShapeDtypeStruct((t, 4 * HD_B, LANES), F32)],
        scratch_shapes=[tile, tile], compiler_params=_params(1))(r, w, k, v, a, b)


def _rwkv_scan_bwd(dy, r, w, k, v, a, b, sa, sall, name):
    t = r.shape[0]
    cc = min(t, SCAN_CHUNK)
    n = t // cc

    def body(dy_ref, r_ref, w_ref, k_ref, v_ref, a_ref, b_ref, sa_ref, sall_ref, sprev_ref,
             dr_ref, dw_ref, dk_ref, dv_ref, da_ref, db_ref, dstate_k, dstate_v):
        @pl.when(pl.program_id(0) == 0)
        def _():
            dstate_k[...] = jnp.zeros_like(dstate_k)
            dstate_v[...] = jnp.zeros_like(dstate_v)

        consts = _scan_consts()
        before_chunk = jnp.where(pl.program_id(0) == n - 1, 0.0, 1.0) * sprev_ref[0]

        def block(jj, carry):
            dk_s, dv_s, sc = carry
            j = cc // 8 - 1 - jj
            base = pl.multiple_of(j * 8, 8)
            rows = pl.ds(base, 8)
            dy8, r8, w8, k8, v8, a8, b8, sa8 = (ref[rows, :] for ref in
                                                (dy_ref, r_ref, w_ref, k_ref, v_ref, a_ref, b_ref, sa_ref))
            dyp, rp, wp, kp, vp, ap, bp, sap = (_pieces(x) for x in (dy8, r8, w8, k8, v8, a8, b8, sa8))
            dsa8, dv8 = jnp.zeros((8, WB), F32), jnp.zeros((8, WB), F32)
            steps = range(7, -1, -1)
            cols_k = _pair_cols([(x, i) for i in steps for x in (rp, bp, kp, wp, ap)], consts)
            cols_v = _pair_cols([(x, i) for i in steps for x in (dyp, vp, sap)], consts)
            for n_done, i in enumerate(steps):
                r_c, b_c, k_c, w_c, a_c = cols_k[5 * n_done:5 * n_done + 5]
                dk_s = dk_s + r_c * _pair_rows(dy8, i)
                dsa = _pair_sums(dk_s * b_c)
                dv8 = _put_row(dv8, i, _pair_sums(dk_s * k_c))
                dsa8 = _put_row(dsa8, i, dsa)
                dk_s = dk_s * w_c + a_c * _pair_rows(dsa, 0)
            dsa_cols = _pair_cols([(_pieces(dsa8), i) for i in steps], consts)
            outs = [jnp.zeros((8, WB), F32) for _ in range(5)]
            for n_done, i in enumerate(steps):
                if i > 0:
                    sp = sall_ref[base + i - 1]
                else:
                    sp = jnp.where(j == 0, before_chunk, sall_ref[jnp.maximum(base - 1, 0)])
                dy_c, v_c, sa_c = cols_v[3 * n_done:3 * n_done + 3]
                dsa_c = dsa_cols[n_done]
                dv_s = dv_s + dy_c * _pair_rows(r8, i)
                vals = (_pair_sums(sc * dy_c), _pair_sums(dv_s * sp), _pair_sums(dv_s * v_c),
                        _pair_sums(sp * dsa_c), _pair_sums(dv_s * sa_c))
                outs = [_put_row(o, i, val) for o, val in zip(outs, vals)]
                dv_s = dv_s * _pair_rows(w8, i) + dsa_c * _pair_rows(a8, i)
                sc = sp
            dr8, dw8, dk8, da8, db8 = outs
            for ref, o in zip((dr_ref, dw_ref, dk_ref, dv_ref, da_ref, db_ref), (dr8, dw8, dk8, dv8, da8, db8)):
                ref[rows, :] = o
            return dk_s, dv_s, sc

        dk_s, dv_s, _ = lax.fori_loop(0, cc // 8, block, (dstate_k[...], dstate_v[...], sall_ref[cc - 1]))
        dstate_k[...] = dk_s
        dstate_v[...] = dv_s

    row = pl.BlockSpec((cc, WB), lambda i: (n - 1 - i, 0))
    tile = pltpu.VMEM((4 * HD_B, LANES), F32)
    return pl.pallas_call(
        body, name=name, grid=(n,),
        in_specs=[row] * 8 + [pl.BlockSpec((cc, 4 * HD_B, LANES), lambda i: (n - 1 - i, 0, 0)),
                              pl.BlockSpec((1, 4 * HD_B, LANES), lambda i: (jnp.maximum((n - 1 - i) * cc - 1, 0), 0, 0))],
        out_specs=[row] * 6, out_shape=[jax.ShapeDtypeStruct((t, WB), F32)] * 6,
        scratch_shapes=[tile, tile], compiler_params=_params(1))(dy, r, w, k, v, a, b, sa, sall, sall)


def _rwkv_post(y, r, k2, v, g, r_k, gn_w, gn_b, name):
    t = y.shape[0]
    tb = _tile(t, 256)

    def body(y_ref, r_ref, k_ref, v_ref, g_ref, rk_ref, gw_ref, gb_ref, o_ref):
        ones = _head_ones(WB, HD_B)
        yv = y_ref[...]
        yc = yv - _split_dot(yv, ones, 3) * (1.0 / HD_B)
        rstd = lax.rsqrt(_split_dot(yc * yc, ones, 3) * (1.0 / HD_B) + GN_EPS)
        rk = _split_dot(r_ref[...] * k_ref[...] * rk_ref[...], ones, 3)
        o_ref[...] = ((yc * rstd * gw_ref[...] + gb_ref[...] + rk * v_ref[...]) * g_ref[...]).astype(MXU)

    row = pl.BlockSpec((tb, WB), lambda i: (i, 0))
    vec = pl.BlockSpec((1, WB), lambda i: (0, 0))
    return pl.pallas_call(
        body, name=name, grid=(t // tb,), in_specs=[row] * 5 + [vec] * 3, out_specs=row,
        out_shape=jax.ShapeDtypeStruct((t, WB), MXU), compiler_params=_params(1))(y, r, k2, v, g, r_k, gn_w, gn_b)


def _rwkv_post_bwd(dob, y, r, k2, v, g, r_k, gn_w, gn_b, name):
    t = y.shape[0]
    tb = _tile(t, 256)

    def body(do_ref, y_ref, r_ref, k_ref, v_ref, g_ref, rk_ref, gw_ref, gb_ref,
             dy_ref, dg_ref, dr_ref, dk_ref, dv_ref, dgw_ref, dgb_ref, drk_ref):
        @pl.when(pl.program_id(0) == 0)
        def _():
            dgw_ref[...] = jnp.zeros_like(dgw_ref)
            dgb_ref[...] = jnp.zeros_like(dgb_ref)
            drk_ref[...] = jnp.zeros_like(drk_ref)

        ones = _head_ones(WB, HD_B)
        seg = lambda x: _split_dot(x, ones, 3)
        yv, rv, kv, vv, gv = y_ref[...], r_ref[...], k_ref[...], v_ref[...], g_ref[...]
        yc = yv - seg(yv) * (1.0 / HD_B)
        rstd = lax.rsqrt(seg(yc * yc) * (1.0 / HD_B) + GN_EPS)
        yn = yc * rstd
        rk = seg(rv * kv * rk_ref[...])
        dob_v = do_ref[...]
        dg_ref[...] = dob_v * (yn * gw_ref[...] + gb_ref[...] + rk * vv)
        dyg = dob_v * gv
        dgw_ref[...] += jnp.sum(dyg * yn, axis=0, keepdims=True)
        dgb_ref[...] += jnp.sum(dyg, axis=0, keepdims=True)
        dyn = dyg * gw_ref[...]
        dy_ref[...] = rstd * (dyn - (seg(dyn) + yn * seg(dyn * yn)) * (1.0 / HD_B))
        drk = seg(dyg * vv)
        dv_ref[...] = dyg * rk
        dr_ref[...] = drk * kv * rk_ref[...]
        dk_ref[...] = drk * rv * rk_ref[...]
        drk_ref[...] += jnp.sum(drk * rv * kv, axis=0, keepdims=True)

    row = pl.BlockSpec((tb, WB), lambda i: (i, 0))
    vec = pl.BlockSpec((1, WB), lambda i: (0, 0))
    full, small = jax.ShapeDtypeStruct((t, WB), F32), jax.ShapeDtypeStruct((1, WB), F32)
    return pl.pallas_call(
        body, name=name, grid=(t // tb,), in_specs=[row] * 6 + [vec] * 3, out_specs=[row] * 5 + [vec] * 3,
        out_shape=[full] * 5 + [small] * 3, compiler_params=_params(1))(dob, y, r, k2, v, g, r_k, gn_w, gn_b)


def _rwkv_prep_bwd(grads, p_all, mu_pad, w2cat, w0, a0, k_k, k_a, name):
    t = p_all.shape[0]
    tb = _tile(t, 256)

    def body(*refs):
        g_refs, ins, outs = refs[:10], refs[10:27], refs[27:]
        dr_s, dw, dk2_s, dv_s, das, dbs, dg, dr_b, dk2_b, dv_b = (ref[...] for ref in g_refs)
        dr_ref, dk_ref, dv_ref, dlo_ref, dw2_ref, dw0_ref, da0_ref, dkk_ref, dka_ref = outs

        @pl.when(pl.program_id(0) == 0)
        def _():
            for ref in (dw2_ref, dw0_ref, da0_ref, dkk_ref, dka_ref):
                ref[...] = jnp.zeros_like(ref)

        ones = _head_ones(WB, HD_B)
        q = _rwkv_inputs(ins, pl.program_id(0) == 0, ones)
        kk_w, ka_w = ins[15][...], ins[16][...]
        a, kk, k = q["a"], q["kk"], q["k"]
        dk2 = dk2_s + dk2_b
        dkk = dbs * a - das
        da = dbs * kk + dk2 * k * ka_w
        dk = dk2 * (1.0 + (a - 1.0) * ka_w)
        dka_ref[...] += jnp.sum(dk2 * k * (a - 1.0), axis=0, keepdims=True)
        proj = jnp.where(q["nrm"] > L2_EPS, _split_dot(dkk * kk, ones, 3), 0.0)
        dkk0 = (dkk - kk * proj) / q["den"]
        dk = dk + dkk0 * kk_w
        dkk_ref[...] += jnp.sum(dkk0 * k, axis=0, keepdims=True)
        dal = da * a * (1.0 - a)
        da0_ref[...] += jnp.sum(dal, axis=0, keepdims=True)
        sg = q["sg"]
        dwl = dw * q["decay"] * (-DECAY_C) * sg * (1.0 - sg)
        dw0_ref[...] += jnp.sum(dwl, axis=0, keepdims=True)
        dlin = jnp.concatenate([dwl, dal, dg], axis=1).astype(MXU)
        dw2_ref[...] += _dot(q["z"].astype(MXU), dlin, _TN)
        dz = _dot(dlin, ins[12][...], _NT)
        dlo_ref[...] = dz * _lora_dact(q["lo"])
        dr_ref[...] = dr_s + dr_b
        dk_ref[...] = dk
        dv_ref[...] = dv_s + dv_b

    row = pl.BlockSpec((tb, WB), lambda i: (i, 0))
    vec = pl.BlockSpec((1, WB), lambda i: (0, 0))
    full, small = jax.ShapeDtypeStruct((t, WB), F32), jax.ShapeDtypeStruct((1, WB), F32)
    return pl.pallas_call(
        body, name=name, grid=(t // tb,), in_specs=[row] * 10 + _rwkv_in_specs(t, tb),
        out_specs=[row] * 3 + [pl.BlockSpec((tb, 256), lambda i: (i, 0)),
                               pl.BlockSpec((256, 3 * WB), lambda i: (0, 0))] + [vec] * 4,
        out_shape=[full] * 3 + [jax.ShapeDtypeStruct((t, 256), F32), jax.ShapeDtypeStruct((256, 3 * WB), F32)]
        + [small] * 4, compiler_params=_params(1))(*grads, *_rwkv_in_args(p_all, mu_pad, w2cat, w0, a0, k_k, k_a))


def _shift_bwd(dshifted, p_all, mu_pad, name):
    t = p_all.shape[0]
    tb = _tile(t, 256)
    nt, nt8 = t // tb, tb // 8
    widths, cols, mus = (WB, WB, WB, 256), (COL_R, COL_K, COL_V, COL_L), (0, 1, 2, 6)

    def body(*refs):
        d_refs, n_refs, p_refs, q_refs, m_refs = refs[0:4], refs[4:8], refs[8:12], refs[12:16], refs[16:20]
        o_refs, dmu_refs = refs[20:24], refs[24:28]
        i = pl.program_id(0)

        @pl.when(i == 0)
        def _():
            for ref in dmu_refs:
                ref[...] = jnp.zeros_like(ref)

        for d, nx, p, q, m, o, dmu in zip(d_refs, n_refs, p_refs, q_refs, m_refs, o_refs, dmu_refs):
            dv, pv, mu = d[...], p[...], m[...]
            o[...] = (dv * (1.0 - mu) + mu * _shift_up(dv, nx[...], i == nt - 1)).astype(MXU)
            dmu[...] += jnp.sum(dv * (_shift_down(pv, q[...], i == 0) - pv), axis=0, keepdims=True)

    cur_d = [pl.BlockSpec((tb, w), lambda i: (i, 0)) for w in widths]
    next_d = [pl.BlockSpec((8, w), lambda i: (jnp.minimum((i + 1) * nt8, t // 8 - 1), 0)) for w in widths]
    cur_p = [pl.BlockSpec((tb, w), lambda i, j=j: (i, j)) for w, j in zip(widths, cols)]
    prev_p = [pl.BlockSpec((8, w), lambda i, j=j: (jnp.maximum(i * nt8 - 1, 0), j)) for w, j in zip(widths, cols)]
    mu_s = [pl.BlockSpec((1, w), lambda i, j=j: (0, j)) for w, j in zip(widths, mus)]
    vecs = [pl.BlockSpec((1, w), lambda i: (0, 0)) for w in widths]
    return pl.pallas_call(
        body, name=name, grid=(nt,), in_specs=cur_d + next_d + cur_p + prev_p + mu_s, out_specs=cur_d + vecs,
        out_shape=[jax.ShapeDtypeStruct((t, w), MXU) for w in widths]
        + [jax.ShapeDtypeStruct((1, w), F32) for w in widths],
        compiler_params=_params(1))(*dshifted, *dshifted, *(p_all,) * 8, *(mu_pad,) * 4)


def _peer(k):
    x, y, c = (lax.axis_index(n) for n in AXES)
    px = 1 - x if k & 4 else x
    py = 1 - y if k & 2 else y
    pc = 1 - c if k & 1 else c
    return (px, py, pc), 4 * px + 2 * py + pc


def _exchange_copy(src_refs, land_refs, send_sems, recv_sems, per_peer, j, k, arriving):
    _, me = _peer(0)
    peer, idx = _peer(k)
    sem = j * (N_DEV - 1) + k - 1
    return pltpu.make_async_remote_copy(
        src_ref=src_refs[j].at[idx] if per_peer[j] else src_refs[j],
        dst_ref=land_refs[j].at[idx if arriving else me],
        send_sem=send_sems.at[sem], recv_sem=recv_sems.at[sem],
        device_id=peer, device_id_type=pl.DeviceIdType.MESH)


def _exchange_start(srcs, per_peer, name):
    n = len(srcs)
    shapes = [tuple(s.shape[1:]) if pp else tuple(s.shape) for s, pp in zip(srcs, per_peer)]
    pairs = [(j, k) for k in range(1, N_DEV) for j in range(n)]

    def body(*refs):
        src_refs, land_refs, (send_sems, recv_sems), token = refs[:n], refs[n:2 * n], refs[2 * n:2 * n + 2], refs[-1]
        for j, k in pairs:
            _exchange_copy(src_refs, land_refs, send_sems, recv_sems, per_peer, j, k, False).start()
        token[...] = jnp.zeros_like(token)

    hbm, sem = pl.BlockSpec(memory_space=pltpu.HBM), pl.BlockSpec(memory_space=pltpu.SEMAPHORE)
    lands = [lax.empty((N_DEV,) + shp, s.dtype) for shp, s in zip(shapes, srcs)]
    operands = [pltpu.with_memory_space_constraint(a, pltpu.HBM) for a in list(srcs) + lands]
    n_sems = n * (N_DEV - 1)
    out = pl.pallas_call(
        body, name=name, in_specs=[hbm] * (2 * n),
        out_specs=[sem, sem] + [hbm] * (2 * n) + [pl.BlockSpec(memory_space=pltpu.VMEM)],
        out_shape=[pltpu.SemaphoreType.DMA((n_sems,)), pltpu.SemaphoreType.DMA((n_sems,))]
        + [pltpu.HBM(a.shape, a.dtype) for a in operands] + [jax.ShapeDtypeStruct((8, LANES), F32)],
        input_output_aliases={j: 2 + j for j in range(2 * n)},
        compiler_params=pltpu.CompilerParams(has_side_effects=pltpu.SideEffectType.DATAFLOW_SIDE_EFFECTING))(*operands)
    return (out[0], out[1], out[2:2 + n], out[2 + n:2 + 2 * n], per_peer), out[-1]


def _exchange_wait(handle, after, name):
    send_sems, recv_sems, srcs, lands, per_peer = handle
    n = len(srcs)
    pairs = [(j, k) for k in range(1, N_DEV) for j in range(n)]

    def body(*refs):
        src_refs, land_refs, (send_sems, recv_sems) = refs[:n], refs[n:2 * n], refs[2 * n:2 * n + 2]
        for j, k in pairs:
            _exchange_copy(src_refs, land_refs, send_sems, recv_sems, per_peer, j, k, False).wait_send()
            _exchange_copy(src_refs, land_refs, send_sems, recv_sems, per_peer, j, k, True).wait_recv()

    hbm, sem = pl.BlockSpec(memory_space=pltpu.HBM), pl.BlockSpec(memory_space=pltpu.SEMAPHORE)
    out = pl.pallas_call(
        body, name=name, in_specs=[hbm] * (2 * n) + [sem, sem, pl.BlockSpec(memory_space=pl.ANY)],
        out_specs=[hbm] * (2 * n), out_shape=[pltpu.HBM(a.shape, a.dtype) for a in list(srcs) + list(lands)],
        input_output_aliases={j: j for j in range(2 * n)},
        compiler_params=pltpu.CompilerParams(has_side_effects=pltpu.SideEffectType.DATAFLOW_SIDE_EFFECTING))(
            *srcs, *lands, send_sems, recv_sems, after)
    return out[n:]


def _adam_update(g, w, m, v):
    c1, c2 = 1.0 - ADAM_B1 ** ADAM_STEP, 1.0 - ADAM_B2 ** ADAM_STEP
    nm = ADAM_B1 * m + (1.0 - ADAM_B1) * g
    nv = ADAM_B2 * v + (1.0 - ADAM_B2) * (g * g)
    return -ADAM_LR * ((nm / c1) / (jnp.sqrt(nv / c2) + ADAM_EPS) + ADAM_WD * w), nm, nv


def _row_tile(rows, cols):
    padded = -(-cols // LANES) * LANES
    cap = max(16, ADAM_BLOCK_BYTES // (N_DEV * padded * 4))
    best = 16
    for t in range(16, min(rows, cap) + 1, 16):
        if rows % t == 0:
            best = t
    return best


def _adamw(parts, w, m, v, name):
    _, rows, cols = w.shape
    tb = _row_tile(rows, cols)

    def body(p_ref, w_ref, m_ref, v_ref, g_ref, d_ref, nm_ref, nv_ref):
        g = p_ref[0].astype(F32)
        for d in range(1, N_DEV):
            g = g + p_ref[d].astype(F32)
        g_ref[0] = g
        d_ref[0], nm_ref[0], nv_ref[0] = _adam_update(g, w_ref[0], m_ref[0], v_ref[0])

    row = pl.BlockSpec((1, tb, cols), lambda i: (0, i, 0))
    out = jax.ShapeDtypeStruct(w.shape, F32)
    return pl.pallas_call(
        body, name=name, grid=(rows // tb,),
        in_specs=[pl.BlockSpec((N_DEV, tb, cols), lambda i: (0, i, 0)), row, row, row], out_specs=[row] * 4,
        out_shape=[out] * 4, compiler_params=_params(1))(parts, w, m, v)


def _adamw_small(parts, ws, ms, vs, name):
    n = len(ws)

    def body(*refs):
        p_ref = refs[0]
        w_refs, m_refs, v_refs = refs[1:1 + n], refs[1 + n:1 + 2 * n], refs[1 + 2 * n:1 + 3 * n]
        outs = refs[1 + 3 * n:]
        base = 0
        for j in range(n):
            rows, cols = ws[j].shape
            size = rows * cols
            for ch in range(-(-size // LANES)):
                r, c0 = divmod(ch * LANES, cols)
                width = min(LANES, cols - c0)
                g = p_ref[0, base + ch:base + ch + 1, 0:width]
                for d in range(1, N_DEV):
                    g = g + p_ref[d, base + ch:base + ch + 1, 0:width]
                at = (slice(r, r + 1), slice(c0, c0 + width))
                delta, nm, nv = _adam_update(g, w_refs[j][at], m_refs[j][at], v_refs[j][at])
                for out, val in zip((outs[j], outs[n + j], outs[2 * n + j], outs[3 * n + j]), (g, delta, nm, nv)):
                    out[at] = val
            base += -(-size // (8 * LANES)) * 8

    vmem = pl.BlockSpec(memory_space=pltpu.VMEM)
    res = pl.pallas_call(
        body, name=name, in_specs=[vmem] * (1 + 3 * n), out_specs=[vmem] * (4 * n),
        out_shape=[jax.ShapeDtypeStruct(a.shape, F32) for a in ws] * 4)(parts, *ws, *ms, *vs)
    return res[:n], res[n:2 * n], res[2 * n:3 * n], res[3 * n:]


def _rows(a, multiple):
    flat = a.reshape(-1)
    pad = -flat.shape[0] % (multiple * LANES)
    if pad:
        flat = jnp.concatenate([flat, jnp.zeros((pad,), a.dtype)])
    return flat.reshape(-1, LANES)


def _pack(arrs, multiple):
    return jnp.concatenate([_rows(a, multiple) for a in arrs], axis=0)


def _gathered_to_full(g, name, shard_shape):
    g = g.reshape((N_DEV,) + shard_shape)
    if name in COL_SHARDED:
        return jnp.transpose(g, (1, 0, 2)).reshape(shard_shape[0], N_DEV * shard_shape[1])
    return g.reshape(N_DEV * shard_shape[0], shard_shape[1])


def _full_to_per_device(full, name):
    if name in COL_SHARDED:
        r, c = full.shape
        return jnp.transpose(full.reshape(r, N_DEV, c // N_DEV), (1, 0, 2))
    return full.reshape(N_DEV, full.shape[0] // N_DEV, full.shape[1])


def _w2cat(w2, a2, g2):
    n_w, n_a, n_g = LORA
    out = jnp.zeros((256, 3 * WB), w2.dtype)
    out = out.at[0:n_w, 0:WB].set(w2)
    out = out.at[n_w:n_w + n_a, WB:2 * WB].set(a2)
    return out.at[n_w + n_a:n_w + n_a + n_g, 2 * WB:].set(g2)


class _Local:
    def __init__(self, w):
        self.w = w

    def weights(self, group, after=None):
        return self.w

    def send(self, grads, names):
        return ()


class _Overlapped:
    GROUPS = {"ffn1": ("ffn1_w_gate", "ffn1_w_up", "ffn1_w_down"),
              "mixer_in": ("w_in", "rwkv_w2", "rwkv_a2", "rwkv_g2"),
              "late": ("w_out", "ffn2_w_gate", "ffn2_w_up", "ffn2_w_down")}

    def __init__(self, wts):
        x, y, c = (lax.axis_index(n) for n in AXES)
        self.wts, self.me, self.gathers, self.sends = wts, 4 * x + 2 * y + c, {}, []
        for group, names in self.GROUPS.items():
            shards = [wts[n].astype(MXU) for n in names]
            handle, token = _exchange_start(shards, [False] * len(names), "gather_" + group)
            self.gathers[group] = (names, shards, handle, token)

    def _own_slot(self, land, mine):
        return lax.dynamic_update_slice(land, mine[None], (self.me,) + (0,) * mine.ndim)

    def weights(self, group, after=None):
        names, shards, handle, _ = self.gathers[group]
        if after is None:
            after = sum(token for _, _, _, token in self.gathers.values())
        lands = _exchange_wait(handle, after, "gathered_" + group)
        w = {n: _gathered_to_full(self._own_slot(land, own), n, own.shape[1:])
             for n, own, land in zip(names, shards, lands)}
        if group == "ffn1":
            for n in SMALL:
                keep = n in ("hgrn_lb_logits", "rwkv_r_k", "final_norm")
                w[n] = self.wts[n] if keep else self.wts[n].reshape(1, -1)
        return w

    def send(self, grads, names, small=None):
        contrib = [_full_to_per_device(grads[n], n).astype(WIRE) for n in names]
        per_peer = [True] * len(names)
        if small is not None:
            names, contrib, per_peer = names + ("small",), contrib + [small], per_peer + [False]
        handle, token = _exchange_start(contrib, per_peer, "scatter_" + names[0])
        self.sends.append((names, contrib, per_peer, handle))
        self.last_token = token
        return (token,)

    def received(self, which, after):
        names, contrib, per_peer, handle = self.sends[which]
        lands = _exchange_wait(handle, after, "scattered_" + names[0])
        parts = {}
        for n, own, pp, land in zip(names, contrib, per_peer, lands):
            mine = lax.dynamic_index_in_dim(own, self.me, 0, keepdims=False) if pp else own
            parts[n] = self._own_slot(land, mine)
        return parts


def _local_step(x, target, net):
    n_w, n_a, n_g = LORA
    w = dict(net.weights("ffn1"))
    h1 = _rms_fwd(x, w["ffn1_norm"], "ffn1_norm")
    x1 = _ffn_fwd(x, h1, w["ffn1_w_gate"], w["ffn1_w_up"], w["ffn1_w_down"], "ffn1_fwd")
    w.update(net.weights("mixer_in", x1))
    w_in_pad = jnp.pad(w["w_in"], ((0, 0), (0, N_INP - N_IN)))
    mu_pad = jnp.pad(w["rwkv_shift_mu"], ((0, 0), (0, 1792 - 1696)))
    w2cat = _w2cat(w["rwkv_w2"], w["rwkv_a2"], w["rwkv_g2"])
    r_k = w["rwkv_r_k"].reshape(1, WB)
    rw = (mu_pad, w2cat, w["rwkv_w0"], w["rwkv_a0"], w["rwkv_k_k"], w["rwkv_k_a"])

    h2 = _rms_fwd(x1, w["mix_norm"], "mix_norm")
    p_all = _matmul(h2, w_in_pad, name="in_proj")
    oa, oraw, states = _hgrn_fwd(p_all, w["hgrn_lb_logits"], w["hgrn_out_norm"], "hgrn_fwd")
    r, decay, k2, v, sa, sb, g = _rwkv_prep(p_all, *rw, "rwkv_prep")
    y, s_a, sall = _rwkv_scan_fwd(r, decay, k2, v, sa, sb, "rwkv_scan_fwd")
    post_w = (r_k, w["rwkv_gn_w"], w["rwkv_gn_b"])
    ob = _rwkv_post(y, r, k2, v, g, *post_w, "rwkv_post")
    w.update(net.weights("late", ob))
    o = jnp.concatenate([oa, ob], axis=1)
    x2 = _matmul(o, w["w_out"], res=x1, name="out_proj")
    h3 = _rms_fwd(x2, w["ffn2_norm"], "ffn2_norm")
    x3 = _ffn_fwd(x2, h3, w["ffn2_w_gate"], w["ffn2_w_up"], w["ffn2_w_down"], "ffn2_fwd")
    loss, dx3, d_final = _loss_head(x3, w["final_norm"].reshape(1, D), target, "loss_head")

    grads = {"final_norm": d_final.reshape(D)}

    def ffn_back(prefix, h, dy, x_in, norm):
        wg, wu, wd = (w[prefix + s] for s in ("_w_gate", "_w_up", "_w_down"))
        dh, act, dgate, dup, dout = _ffn_bwd(h, dy, wg, wu, wd, prefix + "_bwd")
        sent = ()
        for which, a_op, b_op in (("_w_gate", h, dgate), ("_w_up", h, dup), ("_w_down", act, dout)):
            grads[prefix + which] = _matmul(a_op, b_op, ta=True, out_dtype=WIRE, after=sent, name=prefix + "_d" + which)
            sent = net.send(grads, (prefix + which,))
        dx, grads[prefix + "_norm"] = _rms_bwd(x_in, norm, dh, dy, prefix + "_norm_bwd", after=sent)
        return dx

    dx2 = ffn_back("ffn2", h3, dx3, x2, w["ffn2_norm"])
    grads["w_out"] = _matmul(o, dx2, ta=True, out_dtype=WIRE, name="d_w_out")
    sent = net.send(grads, ("w_out",))
    do = _matmul(dx2, w["w_out"], tb=True, after=sent, name="d_mixed")
    dqa, dfa, dia, dga, grads["hgrn_out_norm"], grads["hgrn_lb_logits"] = _hgrn_bwd(
        p_all, w["hgrn_lb_logits"], w["hgrn_out_norm"], oraw, states, do[:, :WA], "hgrn_bwd")
    dy, dg, dr_b, dk2_b, dv_b, grads["rwkv_gn_w"], grads["rwkv_gn_b"], d_rk = _rwkv_post_bwd(
        do[:, WA:], y, r, k2, v, g, *post_w, "rwkv_post_bwd")
    grads["rwkv_r_k"] = d_rk.reshape(w["rwkv_r_k"].shape)
    dr, dw, dk2, dv, dsa, dsb = _rwkv_scan_bwd(dy, r, decay, k2, v, sa, sb, s_a, sall, "rwkv_scan_bwd")
    (dsr, dsk, dsv, dslo, dw2cat, grads["rwkv_w0"], grads["rwkv_a0"], grads["rwkv_k_k"],
     grads["rwkv_k_a"]) = _rwkv_prep_bwd((dr, dw, dk2, dv, dsa, dsb, dg, dr_b, dk2_b, dv_b), p_all, *rw,
                                         "rwkv_prep_bwd")
    grads["rwkv_w2"] = dw2cat[0:n_w, 0:WB]
    grads["rwkv_a2"] = dw2cat[n_w:n_w + n_a, WB:2 * WB]
    grads["rwkv_g2"] = dw2cat[n_w + n_a:n_w + n_a + n_g, 2 * WB:]
    dpr, dpk, dpv, dplo, dmu_r, dmu_k, dmu_v, dmu_lo = _shift_bwd((dsr, dsk, dsv, dslo), p_all, mu_pad, "shift_bwd")
    grads["rwkv_shift_mu"] = jnp.concatenate([dmu_r, dmu_k, dmu_v, dmu_lo], axis=1)[:, :1696]
    dp = jnp.concatenate([dqa, dfa, dia, dga, dpr, dpk, dpv, dplo], axis=1)
    grads["w_in"] = _matmul(h2, dp, ta=True, out_dtype=WIRE, name="d_w_in")[:, :N_IN]
    sent = net.send(grads, ("w_in", "rwkv_w2", "rwkv_a2", "rwkv_g2"))
    dh2 = _matmul(dp, w_in_pad, tb=True, after=sent, name="d_h2")
    dx1, grads["mix_norm"] = _rms_bwd(x1, w["mix_norm"], dh2, dx2, "mix_norm_bwd")
    dx0 = ffn_back("ffn1", h1, dx1, x, w["ffn1_norm"])
    return loss[0, 0], dx0, grads


def kernel(x, ffn1_norm, ffn1_w_gate, ffn1_w_up, ffn1_w_down, mix_norm, w_in, hgrn_lb_logits, hgrn_out_norm, rwkv_shift_mu, rwkv_w0, rwkv_w2, rwkv_a0, rwkv_a2, rwkv_g2, rwkv_k_k, rwkv_k_a, rwkv_r_k, rwkv_gn_w, rwkv_gn_b, w_out, ffn2_norm, ffn2_w_gate, ffn2_w_up, ffn2_w_down, final_norm, loss_target, m_ffn1_norm, m_ffn1_w_gate, m_ffn1_w_up, m_ffn1_w_down, m_mix_norm, m_w_in, m_hgrn_lb_logits, m_hgrn_out_norm, m_rwkv_shift_mu, m_rwkv_w0, m_rwkv_w2, m_rwkv_a0, m_rwkv_a2, m_rwkv_g2, m_rwkv_k_k, m_rwkv_k_a, m_rwkv_r_k, m_rwkv_gn_w, m_rwkv_gn_b, m_w_out, m_ffn2_norm, m_ffn2_w_gate, m_ffn2_w_up, m_ffn2_w_down, m_final_norm, v_ffn1_norm, v_ffn1_w_gate, v_ffn1_w_up, v_ffn1_w_down, v_mix_norm, v_w_in, v_hgrn_lb_logits, v_hgrn_out_norm, v_rwkv_shift_mu, v_rwkv_w0, v_rwkv_w2, v_rwkv_a0, v_rwkv_a2, v_rwkv_g2, v_rwkv_k_k, v_rwkv_k_a, v_rwkv_r_k, v_rwkv_gn_w, v_rwkv_gn_b, v_w_out, v_ffn2_norm, v_ffn2_w_gate, v_ffn2_w_up, v_ffn2_w_down, v_final_norm):
    args = dict(locals())
    wts = {n: args[n] for n in WEIGHTS}
    mom = {n: args["m_" + n] for n in WEIGHTS}
    var = {n: args["v_" + n] for n in WEIGHTS}
    net = _Overlapped(wts)
    loss, grad_x, grads = _local_step(x[0], loss_target[0], net)
    loss = lax.psum(loss, AXES)
    after, = net.send(grads, (), _pack([grads[n] for n in SMALL], 8))

    new = {}
    two_d = lambda a: a if a.ndim == 2 else a.reshape(1, -1)
    for which in range(len(net.sends)):
        for n, part in net.received(which, after).items():
            if n == "small":
                small = _adamw_small(part, *([two_d(src[k]) for k in SMALL] for src in (wts, mom, var)), "adamw_small")
                for j, k in enumerate(SMALL):
                    new[k] = [res[j].reshape(wts[k].shape) for res in small]
            else:
                new[n] = _adamw(part, wts[n], mom[n], var[n], "adamw_" + n)
                after = new[n][1]
    return (loss, grad_x[None], *[new[n][0] for n in WEIGHTS], *[new[n][1] for n in WEIGHTS],
            *[new[n][2] for n in WEIGHTS], *[new[n][3] for n in WEIGHTS])
```

```python
import functools
import math

import jax
import jax.numpy as jnp
from jax import lax
from jax.experimental import pallas as pl
from jax.experimental.pallas import tpu as pltpu

F32 = jnp.float32
MXU = jnp.bfloat16
WIRE = jnp.bfloat16
D = 1024
FF = 2816
WA = 512
WB = 512
HD_B = 64
N_IN = 3744
N_INP = 3840
COL_R, COL_K, COL_V = 4, 5, 6
COL_L = 14
LORA = (32, 32, 96)
HG_CHUNK = 64
SCAN_CHUNK = 64
NORM_EPS = 1e-6
GN_EPS = 64e-5
L2_EPS = 1e-12
DECAY_C = math.exp(-0.5)
N_DEV = 8
LANES = 128
ADAM_BLOCK_BYTES = 4 * 1024 * 1024
MATMUL_BLOCK_BYTES = 40 * 1024 * 1024
VMEM_LIMIT = 56 * 1024 * 1024
ADAM_LR, ADAM_B1, ADAM_B2, ADAM_EPS, ADAM_WD, ADAM_STEP = 0.001, 0.9, 0.999, 1e-08, 0.01, 10
AXES = ("x", "y", "c")

SHARDED = ("ffn1_w_gate", "ffn1_w_up", "ffn1_w_down", "w_in", "rwkv_w2", "rwkv_a2", "rwkv_g2", "w_out",
           "ffn2_w_gate", "ffn2_w_up", "ffn2_w_down")
COL_SHARDED = {"ffn1_w_gate", "ffn1_w_up", "w_in", "rwkv_w2", "rwkv_a2", "rwkv_g2", "ffn2_w_gate", "ffn2_w_up"}
SMALL = ("ffn1_norm", "mix_norm", "hgrn_lb_logits", "hgrn_out_norm", "rwkv_shift_mu", "rwkv_w0", "rwkv_a0",
         "rwkv_k_k", "rwkv_k_a", "rwkv_r_k", "rwkv_gn_w", "rwkv_gn_b", "ffn2_norm", "final_norm")
WEIGHTS = ("ffn1_norm", "ffn1_w_gate", "ffn1_w_up", "ffn1_w_down", "mix_norm", "w_in", "hgrn_lb_logits",
           "hgrn_out_norm", "rwkv_shift_mu", "rwkv_w0", "rwkv_w2", "rwkv_a0", "rwkv_a2", "rwkv_g2", "rwkv_k_k",
           "rwkv_k_a", "rwkv_r_k", "rwkv_gn_w", "rwkv_gn_b", "w_out", "ffn2_norm", "ffn2_w_gate", "ffn2_w_up",
           "ffn2_w_down", "final_norm")


def _tile(n, cap):
    if n <= cap:
        return n
    for t in range(cap - cap % LANES, 0, -LANES):
        if n % t == 0:
            return t
    raise ValueError((n, cap))


def _params(n_axes):
    return pltpu.CompilerParams(dimension_semantics=("arbitrary",) * n_axes, vmem_limit_bytes=VMEM_LIMIT)


def _sig(x):
    return jax.nn.sigmoid(x)


def _dsilu(z, s):
    return s * (1.0 + z * (1.0 - s))


def _dot(a, b, dims=((1,), (0,)), precision=None):
    return lax.dot_general(a, b, (dims, ((), ())), preferred_element_type=F32, precision=precision)


_NT = ((1,), (1,))
_TN = ((0,), (0,))
_HI = lax.Precision.HIGHEST


def _iota(shape, dim):
    return lax.broadcasted_iota(jnp.int32, shape, dim)


def _split_dot(x, ones, passes):
    hi = x.astype(jnp.bfloat16)
    acc = _dot(hi, ones)
    rem = x
    for _ in range(passes - 1):
        rem = rem - hi.astype(F32)
        hi = rem.astype(jnp.bfloat16)
        acc = acc + _dot(hi, ones)
    return acc


def _head_ones(n, width):
    shift = width.bit_length() - 1
    return (_iota((n, n), 0) >> shift == _iota((n, n), 1) >> shift).astype(jnp.bfloat16)


def _matmul(a, b, *, ta=False, tb=False, out_dtype=F32, res=None, after=(), name):
    m, k = (a.shape[1], a.shape[0]) if ta else a.shape
    n = b.shape[0] if tb else b.shape[1]
    tm, tn = _tile(m, 1408), _tile(n, 1408)
    in_bytes = max(a.dtype.itemsize, b.dtype.itemsize)
    for tk in (_tile(k, 1024), _tile(k, 512), _tile(k, 256)):
        if 2 * (tm + tn) * tk * in_bytes + 3 * tm * tn * 4 <= MATMUL_BLOCK_BYTES:
            break
    nk = k // tk
    dims = ((0 if ta else 1,), (1 if tb else 0,))

    def body(*refs):
        a_ref, b_ref = refs[:2]
        o_ref, acc = refs[-2:]
        kk = pl.program_id(2)

        @pl.when(kk == 0)
        def _():
            acc[...] = jnp.zeros_like(acc)

        acc[...] += _dot(a_ref[...].astype(MXU), b_ref[...].astype(MXU), dims)

        @pl.when(kk == nk - 1)
        def _():
            v = acc[...]
            if res is not None:
                v = v + refs[2][...]
            o_ref[...] = v.astype(out_dtype)

    a_spec = pl.BlockSpec((tk, tm), lambda i, j, kk: (kk, i)) if ta else pl.BlockSpec((tm, tk), lambda i, j, kk: (i, kk))
    b_spec = pl.BlockSpec((tn, tk), lambda i, j, kk: (j, kk)) if tb else pl.BlockSpec((tk, tn), lambda i, j, kk: (kk, j))
    o_spec = pl.BlockSpec((tm, tn), lambda i, j, kk: (i, j))
    ins, specs = [a, b], [a_spec, b_spec]
    if res is not None:
        ins.append(res)
        specs.append(o_spec)
    ins += list(after)
    specs += [pl.BlockSpec(memory_space=pl.ANY)] * len(after)
    return pl.pallas_call(
        body, name=name, grid=(m // tm, n // tn, nk), in_specs=specs, out_specs=o_spec,
        out_shape=jax.ShapeDtypeStruct((m, n), out_dtype), scratch_shapes=[pltpu.VMEM((tm, tn), F32)],
        compiler_params=_params(3))(*ins)


def _rms_fwd(x, g, name):
    t = x.shape[0]
    tb = _tile(t, 512)

    def body(x_ref, g_ref, o_ref):
        xv = x_ref[...]
        rinv = lax.rsqrt(jnp.mean(xv * xv, axis=-1, keepdims=True) + NORM_EPS)
        o_ref[...] = (xv * rinv * g_ref[...]).astype(MXU)

    return pl.pallas_call(
        body, name=name, grid=(t // tb,),
        in_specs=[pl.BlockSpec((tb, D), lambda i: (i, 0)), pl.BlockSpec((1, D), lambda i: (0, 0))],
        out_specs=pl.BlockSpec((tb, D), lambda i: (i, 0)), out_shape=jax.ShapeDtypeStruct((t, D), MXU),
        compiler_params=_params(1))(x, g)


def _rms_bwd(x, g, dh, dres, name, after=()):
    t = x.shape[0]
    tb = _tile(t, 512)

    def body(x_ref, g_ref, dh_ref, dres_ref, *rest):
        dx_ref, dg_ref = rest[-2:]

        @pl.when(pl.program_id(0) == 0)
        def _():
            dg_ref[...] = jnp.zeros_like(dg_ref)

        xv = x_ref[...]
        rinv = lax.rsqrt(jnp.mean(xv * xv, axis=-1, keepdims=True) + NORM_EPS)
        xhat = xv * rinv
        dhv = dh_ref[...]
        dg_ref[...] += jnp.sum(dhv * xhat, axis=0, keepdims=True)
        dxhat = dhv * g_ref[...]
        dx_ref[...] = dres_ref[...] + rinv * (dxhat - xhat * jnp.mean(dxhat * xhat, axis=-1, keepdims=True))

    row = pl.BlockSpec((tb, D), lambda i: (i, 0))
    vec = pl.BlockSpec((1, D), lambda i: (0, 0))
    return pl.pallas_call(
        body, name=name, grid=(t // tb,),
        in_specs=[row, vec, row, row] + [pl.BlockSpec(memory_space=pl.ANY)] * len(after), out_specs=[row, vec],
        out_shape=[jax.ShapeDtypeStruct((t, D), F32), jax.ShapeDtypeStruct((1, D), F32)],
        compiler_params=_params(1))(x, g, dh, dres, *after)


def _loss_head(x, g, target, name):
    t = x.shape[0]
    tb = _tile(t, 512)

    def body(x_ref, g_ref, t_ref, loss_ref, dx_ref, dg_ref):
        @pl.when(pl.program_id(0) == 0)
        def _():
            dg_ref[...] = jnp.zeros_like(dg_ref)
            loss_ref[...] = jnp.zeros_like(loss_ref)

        xv = x_ref[...]
        gv = g_ref[...]
        rinv = lax.rsqrt(jnp.mean(xv * xv, axis=-1, keepdims=True) + NORM_EPS)
        xhat = xv * rinv
        err = xhat * gv - t_ref[...]
        per_tok = jnp.mean(err * err, axis=-1, keepdims=True)
        loss_ref[...] += jnp.broadcast_to(0.5 * jnp.sum(per_tok, axis=0, keepdims=True), loss_ref.shape)
        dy = err * (1.0 / D)
        dg_ref[...] += jnp.sum(dy * xhat, axis=0, keepdims=True)
        dxhat = dy * gv
        dx_ref[...] = rinv * (dxhat - xhat * jnp.mean(dxhat * xhat, axis=-1, keepdims=True))

    row = pl.BlockSpec((tb, D), lambda i: (i, 0))
    vec = pl.BlockSpec((1, D), lambda i: (0, 0))
    return pl.pallas_call(
        body, name=name, grid=(t // tb,), in_specs=[row, vec, row],
        out_specs=[pl.BlockSpec((1, LANES), lambda i: (0, 0)), row, vec],
        out_shape=[jax.ShapeDtypeStruct((1, LANES), F32), jax.ShapeDtypeStruct((t, D), F32),
                   jax.ShapeDtypeStruct((1, D), F32)],
        compiler_params=_params(1))(x, g, target)


def _ffn_fwd(x, h, wg, wu, wd, name):
    t = x.shape[0]
    tb, fb = _tile(t, 1024), 256
    nf = FF // fb

    def body(x_ref, h_ref, wg_ref, wu_ref, wd_ref, o_ref, acc):
        f = pl.program_id(1)

        @pl.when(f == 0)
        def _():
            acc[...] = jnp.zeros_like(acc)

        hv = h_ref[...]
        gate = _dot(hv, wg_ref[...])
        up = _dot(hv, wu_ref[...])
        act = (gate * _sig(gate) * up).astype(MXU)
        acc[...] += _dot(act, wd_ref[...])

        @pl.when(f == nf - 1)
        def _():
            o_ref[...] = x_ref[...] + 0.5 * acc[...]

    row = pl.BlockSpec((tb, D), lambda i, f: (i, 0))
    col = pl.BlockSpec((D, fb), lambda i, f: (0, f))
    return pl.pallas_call(
        body, name=name, grid=(t // tb, nf),
        in_specs=[row, row, col, col, pl.BlockSpec((fb, D), lambda i, f: (f, 0))], out_specs=row,
        out_shape=jax.ShapeDtypeStruct((t, D), F32), scratch_shapes=[pltpu.VMEM((tb, D), F32)],
        compiler_params=_params(2))(x, h, wg, wu, wd)


def _ffn_bwd(h, dy, wg, wu, wd, name):
    t = h.shape[0]
    tb, fb = _tile(t, 1024), 256
    nf = FF // fb

    def body(h_ref, dy_ref, wg_ref, wu_ref, wd_ref, dh_ref, act_ref, dg_ref, du_ref, dout_ref, acc):
        f = pl.program_id(1)

        @pl.when(f == 0)
        def _():
            acc[...] = jnp.zeros_like(acc)

        hv = h_ref[...]
        dout = (0.5 * dy_ref[...]).astype(MXU)
        dout_ref[...] = dout
        gate = _dot(hv, wg_ref[...])
        up = _dot(hv, wu_ref[...])
        dact = _dot(dout, wd_ref[...], _NT)
        s = _sig(gate)
        silu = gate * s
        act_ref[...] = (silu * up).astype(MXU)
        dup = (dact * silu).astype(MXU)
        dgate = (dact * up * _dsilu(gate, s)).astype(MXU)
        du_ref[...] = dup
        dg_ref[...] = dgate
        acc[...] += _dot(dgate, wg_ref[...], _NT) + _dot(dup, wu_ref[...], _NT)

        @pl.when(f == nf - 1)
        def _():
            dh_ref[...] = acc[...]

    row = pl.BlockSpec((tb, D), lambda i, f: (i, 0))
    col = pl.BlockSpec((D, fb), lambda i, f: (0, f))
    hid = pl.BlockSpec((tb, fb), lambda i, f: (i, f))
    hid_shape = jax.ShapeDtypeStruct((t, FF), MXU)
    return pl.pallas_call(
        body, name=name, grid=(t // tb, nf),
        in_specs=[row, row, col, col, pl.BlockSpec((fb, D), lambda i, f: (f, 0))],
        out_specs=[row, hid, hid, hid, row],
        out_shape=[jax.ShapeDtypeStruct((t, D), F32), hid_shape, hid_shape, hid_shape,
                   jax.ShapeDtypeStruct((t, D), MXU)],
        scratch_shapes=[pltpu.VMEM((tb, D), F32)], compiler_params=_params(2))(h, dy, wg, wu, wd)


def _hgrn_chunk(qa, fa, lbl):
    c = HG_CHUNK
    lb = _sig(lbl[0:1, :] - lbl[1:2, :])
    sf = _sig(fa)
    forget = lb + (1.0 - lb) * sf
    kh = 1.0 - forget
    row, col = _iota((c, c), 0), _iota((c, c), 1)
    b = _dot((col <= row).astype(F32), jnp.log(forget), precision=_HI)
    bref, blast = b[c // 2:c // 2 + 1, :], b[c - 1:c, :]
    sq = _sig(qa)
    q = qa * sq
    qt, kt = q * jnp.exp(b - bref), kh * jnp.exp(bref - b)
    qb, kl = q * jnp.exp(b), kh * jnp.exp(blast - b)
    causal = col <= row
    return dict(lb=lb, sf=sf, forget=forget, sq=sq, qt=qt, kt=kt, qb=qb, kl=kl, decay=jnp.exp(blast),
                causal=causal, e_q=jnp.exp(b), e_qt=jnp.exp(b - bref), e_kt=jnp.exp(bref - b),
                e_kl=jnp.exp(blast - b))


def _hgrn_specs(t):
    c = HG_CHUNK
    return c, t // c, WA // LANES


def _hgrn_fwd(p_all, lbl, onorm, name):
    t = p_all.shape[0]
    c, n, nh = _hgrn_specs(t)

    def body(q_ref, f_ref, i_ref, g_ref, lbl_ref, on_ref, oa_ref, oraw_ref, st_ref, state):
        @pl.when(pl.program_id(0) == 0)
        def _():
            state[...] = jnp.zeros_like(state)

        for h in range(nh):
            at = slice(h * LANES, (h + 1) * LANES)
            k = _hgrn_chunk(q_ref[:, at], f_ref[:, at], lbl_ref[:, at])
            v = i_ref[:, at]
            st = state[h]
            st_ref[h, 0] = st
            a = jnp.where(k["causal"], _dot(k["qt"], k["kt"], _NT, _HI), 0.0)
            o = _dot(a, v, precision=_HI) + _dot(k["qb"], st, _NT, _HI)
            state[h] = st * k["decay"] + _dot(v, k["kl"], _TN, _HI)
            oraw_ref[:, at] = o
            rinv = lax.rsqrt(jnp.mean(o * o, axis=-1, keepdims=True) + NORM_EPS)
            ga = g_ref[:, at]
            oa_ref[:, at] = (o * rinv * on_ref[:, at] * (ga * _sig(ga))).astype(MXU)

    def blk(j):
        return pl.BlockSpec((c, WA), lambda i: (i, j))

    return pl.pallas_call(
        body, name=name, grid=(n,),
        in_specs=[blk(0), blk(1), blk(2), blk(3), pl.BlockSpec((2, WA), lambda i: (0, 0)),
                  pl.BlockSpec((1, WA), lambda i: (0, 0))],
        out_specs=[blk(0), blk(0), pl.BlockSpec((nh, 1, LANES, LANES), lambda i: (0, i, 0, 0))],
        out_shape=[jax.ShapeDtypeStruct((t, WA), MXU), jax.ShapeDtypeStruct((t, WA), F32),
                   jax.ShapeDtypeStruct((nh, n, LANES, LANES), F32)],
        scratch_shapes=[pltpu.VMEM((nh, LANES, LANES), F32)], compiler_params=_params(1))(
            p_all, p_all, p_all, p_all, lbl, onorm)


def _hgrn_bwd(p_all, lbl, onorm, oraw, states, doa, name):
    t = p_all.shape[0]
    c, n, nh = _hgrn_specs(t)

    def body(q_ref, f_ref, i_ref, g_ref, lbl_ref, on_ref, oraw_ref, st_ref, doa_ref,
             dq_ref, df_ref, di_ref, dg_ref, don_ref, dlbl_ref, dstate, dlb):
        @pl.when(pl.program_id(0) == 0)
        def _():
            dstate[...] = jnp.zeros_like(dstate)
            dlb[...] = jnp.zeros_like(dlb)
            don_ref[...] = jnp.zeros_like(don_ref)

        for h in range(nh):
            at = slice(h * LANES, (h + 1) * LANES)
            qa, fa, v, ga = q_ref[:, at], f_ref[:, at], i_ref[:, at], g_ref[:, at]
            k = _hgrn_chunk(qa, fa, lbl_ref[:, at])
            st, dst_next = st_ref[h, 0], dstate[h]
            o = oraw_ref[:, at]
            gain = on_ref[:, at]
            rinv = lax.rsqrt(jnp.mean(o * o, axis=-1, keepdims=True) + NORM_EPS)
            on = o * rinv
            sg = _sig(ga)
            gate = ga * sg
            dout = doa_ref[:, at]
            don_ref[:, at] += jnp.sum(dout * on * gate, axis=0, keepdims=True)
            dg_ref[:, at] = (dout * on * gain * _dsilu(ga, sg)).astype(MXU)
            d_on = dout * gain * gate
            do = rinv * (d_on - on * jnp.mean(d_on * on, axis=-1, keepdims=True))

            a = jnp.where(k["causal"], _dot(k["qt"], k["kt"], _NT, _HI), 0.0)
            dqb = _dot(do, st, precision=_HI)
            dstate[h] = dst_next * k["decay"] + _dot(do, k["qb"], _TN, _HI)
            da = jnp.where(k["causal"], _dot(do, v, _NT, _HI), 0.0)
            dqt = _dot(da, k["kt"], precision=_HI)
            dkt = _dot(da, k["qt"], _TN, _HI)
            dv = _dot(a, do, _TN, _HI) + _dot(k["kl"], dst_next, _NT, _HI)
            dkl = _dot(v, dst_next, precision=_HI)
            ddecay = jnp.sum(dst_next * st, axis=0, keepdims=True)
            dq = dqb * k["e_q"] + dqt * k["e_qt"]
            dk = dkt * k["e_kt"] + dkl * k["e_kl"]
            tq, tk, tl = dqt * k["qt"], dkt * k["kt"], dkl * k["kl"]
            db = dqb * k["qb"] + tq - tk - tl
            dbref = jnp.sum(tk - tq, axis=0, keepdims=True)
            dblast = jnp.sum(tl, axis=0, keepdims=True) + ddecay * k["decay"]
            rows = _iota((c, LANES), 0)
            db = db + jnp.where(rows == c // 2, dbref, 0.0) + jnp.where(rows == c - 1, dblast, 0.0)
            row, col = _iota((c, c), 0), _iota((c, c), 1)
            dlogf = _dot((col >= row).astype(F32), db, precision=_HI)
            dq_ref[:, at] = (dq * _dsilu(qa, k["sq"])).astype(MXU)
            di_ref[:, at] = dv.astype(MXU)
            dforget = dlogf / k["forget"] - dk
            sf, lb = k["sf"], k["lb"]
            df_ref[:, at] = (dforget * (1.0 - lb) * sf * (1.0 - sf)).astype(MXU)
            dlb[:, at] += jnp.sum(dforget * (1.0 - sf), axis=0, keepdims=True)
            dl0 = dlb[:, at] * lb * (1.0 - lb)
            dlbl_ref[:, at] = jnp.where(_iota((2, LANES), 0) == 0, dl0, -dl0)

    def blk(j):
        return pl.BlockSpec((c, WA), lambda i: (n - 1 - i, j))

    vec = pl.BlockSpec((1, WA), lambda i: (0, 0))
    lg = pl.BlockSpec((2, WA), lambda i: (0, 0))
    grad = jax.ShapeDtypeStruct((t, WA), MXU)
    return pl.pallas_call(
        body, name=name, grid=(n,),
        in_specs=[blk(0), blk(1), blk(2), blk(3), lg, vec, blk(0),
                  pl.BlockSpec((nh, 1, LANES, LANES), lambda i: (0, n - 1 - i, 0, 0)), blk(0)],
        out_specs=[blk(0), blk(0), blk(0), blk(0), vec, lg],
        out_shape=[grad, grad, grad, grad, jax.ShapeDtypeStruct((1, WA), F32), jax.ShapeDtypeStruct((2, WA), F32)],
        scratch_shapes=[pltpu.VMEM((nh, LANES, LANES), F32), pltpu.VMEM((1, WA), F32)],
        compiler_params=_params(1))(p_all, p_all, p_all, p_all, lbl, onorm, oraw, states, doa)


def _lora_act(x):
    lane = _iota(x.shape, 1)
    n_w, n_a, n_g = LORA
    return jnp.where(lane < n_w, jnp.tanh(x),
                     jnp.where(lane < n_w + n_a, x, jnp.where(lane < n_w + n_a + n_g, _sig(x), 0.0)))


def _lora_dact(x):
    lane = _iota(x.shape, 1)
    n_w, n_a, n_g = LORA
    th, s = jnp.tanh(x), _sig(x)
    return jnp.where(lane < n_w, 1.0 - th * th,
                     jnp.where(lane < n_w + n_a, 1.0, jnp.where(lane < n_w + n_a + n_g, s * (1.0 - s), 0.0)))


def _shift_down(cur, prev8, first):
    rolled = pltpu.roll(cur, 1, 0)
    edge = prev8[7:8, :] * jnp.where(first, 0.0, 1.0)
    return jnp.where(_iota(cur.shape, 0) == 0, edge, rolled)


def _shift_up(cur, next8, last):
    rows = cur.shape[0]
    rolled = pltpu.roll(cur, rows - 1, 0)
    edge = next8[0:1, :] * jnp.where(last, 0.0, 1.0)
    return jnp.where(_iota(cur.shape, 0) == rows - 1, edge, rolled)


def _rwkv_inputs(refs, first, ones):
    (pr, pk, pv, plo, qr, qk, qv, qlo, mr, mk, mv, mlo, w2c, w0, a0, kk_w, ka_w) = refs
    mix = lambda cur, prev, mu: cur[...] + mu[...] * (_shift_down(cur[...], prev[...], first) - cur[...])
    r, k, v, lo = mix(pr, qr, mr), mix(pk, qk, mk), mix(pv, qv, mv), mix(plo, qlo, mlo)
    z = _lora_act(lo)
    lin = _dot(z.astype(MXU), w2c[...])
    sg = _sig(w0[...] + lin[:, :WB])
    decay = jnp.exp(-DECAY_C * sg)
    a = _sig(a0[...] + lin[:, WB:2 * WB])
    g = lin[:, 2 * WB:]
    kk0 = k * kk_w[...]
    nrm = jnp.sqrt(_split_dot(kk0 * kk0, ones, 3))
    den = jnp.maximum(nrm, L2_EPS)
    kk = kk0 / den
    k2 = k * (1.0 + (a - 1.0) * ka_w[...])
    return dict(r=r, k=k, v=v, lo=lo, z=z, sg=sg, decay=decay, a=a, g=g, kk=kk, den=den, nrm=nrm, k2=k2)


def _rwkv_in_specs(t, tb):
    nt8 = tb // 8

    def cur(w, j):
        return pl.BlockSpec((tb, w), lambda i: (i, j))

    def prev(w, j):
        return pl.BlockSpec((8, w), lambda i: (jnp.maximum(i * nt8 - 1, 0), j))

    def vec(w, j=0):
        return pl.BlockSpec((1, w), lambda i: (0, j))

    return [cur(WB, COL_R), cur(WB, COL_K), cur(WB, COL_V), cur(256, COL_L),
            prev(WB, COL_R), prev(WB, COL_K), prev(WB, COL_V), prev(256, COL_L),
            vec(WB, 0), vec(WB, 1), vec(WB, 2), vec(256, 6),
            pl.BlockSpec((256, 3 * WB), lambda i: (0, 0)), vec(WB), vec(WB), vec(WB), vec(WB)]


def _rwkv_in_args(p_all, mu_pad, w2cat, w0, a0, k_k, k_a):
    return (p_all,) * 8 + (mu_pad,) * 4 + (w2cat, w0, a0, k_k, k_a)


def _rwkv_prep(p_all, mu_pad, w2cat, w0, a0, k_k, k_a, name):
    t = p_all.shape[0]
    tb = _tile(t, 256)

    def body(*refs):
        ins, outs = refs[:17], refs[17:]
        q = _rwkv_inputs(ins, pl.program_id(0) == 0, _head_ones(WB, HD_B))
        for ref, val in zip(outs, (q["r"], q["decay"], q["k2"], q["v"], -q["kk"], q["kk"] * q["a"], q["g"])):
            ref[...] = val

    out = pl.BlockSpec((tb, WB), lambda i: (i, 0))
    return pl.pallas_call(
        body, name=name, grid=(t // tb,), in_specs=_rwkv_in_specs(t, tb), out_specs=[out] * 7,
        out_shape=[jax.ShapeDtypeStruct((t, WB), F32)] * 7, compiler_params=_params(1))(
            *_rwkv_in_args(p_all, mu_pad, w2cat, w0, a0, k_k, k_a))


def _pair_rows(x8, i):
    return jnp.concatenate([jnp.broadcast_to(x8[i:i + 1, p * LANES:(p + 1) * LANES], (HD_B, LANES))
                            for p in range(4)], axis=0)


def _pair_sums(x):
    return jnp.concatenate([jnp.sum(x[p * HD_B:(p + 1) * HD_B], axis=0, keepdims=True) for p in range(4)], axis=1)


def _put_row(buf, i, row):
    return jnp.where(_iota(buf.shape, 0) == i, row, buf)


def _pieces(x):
    hi = x.astype(jnp.bfloat16).astype(F32)
    lo = (x - hi).astype(jnp.bfloat16).astype(F32)
    upper = (_iota((x.shape[0], LANES), 1) & (HD_B // 2)) != 0
    swapped = [jnp.where(upper, pltpu.roll(lo[:, p * LANES:(p + 1) * LANES], HD_B // 2, 1),
                         pltpu.roll(lo[:, p * LANES:(p + 1) * LANES], LANES - HD_B // 2, 1)) for p in range(4)]
    return hi, jnp.concatenate(swapped, axis=1)


def _scan_consts():
    row, lane = _iota((HD_B, LANES), 0), _iota((HD_B, LANES), 1) & (HD_B - 1)
    return (row == lane).astype(jnp.bfloat16), (row == lane ^ (HD_B // 2)).astype(jnp.bfloat16), _head_ones(LANES, HD_B)


def _pair_cols(many, consts):
    diag_hi, diag_lo, ones = consts
    tiles = []
    for (hi8, lo8), i in many:
        for p in range(4):
            lanes = slice(p * LANES, (p + 1) * LANES)
            hi = jnp.broadcast_to(hi8[i:i + 1, lanes], (16, LANES)).astype(jnp.bfloat16)
            lo = jnp.broadcast_to(lo8[i:i + 1, lanes], (16, LANES)).astype(jnp.bfloat16)
            for g in range(HD_B // 16):
                rows = slice(g * 16, (g + 1) * 16)
                tiles.append(hi * diag_hi[rows] + lo * diag_lo[rows])
    out = _dot(jnp.concatenate(tiles, axis=0), ones)
    return [out[m * 4 * HD_B:(m + 1) * 4 * HD_B] for m in range(len(many))]


def _rwkv_scan_fwd(r, w, k, v, a, b, name):
    t = r.shape[0]
    cc = min(t, SCAN_CHUNK)

    def body(r_ref, w_ref, k_ref, v_ref, a_ref, b_ref, y_ref, sa_ref, sall_ref, state_k, state_v):
        @pl.when(pl.program_id(0) == 0)
        def _():
            state_k[...] = jnp.zeros_like(state_k)
            state_v[...] = jnp.zeros_like(state_v)

        consts = _scan_consts()

        def block(j, carry):
            sk, sv = carry
            base = pl.multiple_of(j * 8, 8)
            rows = pl.ds(base, 8)
            r8, w8, k8, v8, a8, b8 = (ref[rows, :] for ref in (r_ref, w_ref, k_ref, v_ref, a_ref, b_ref))
            rp, wp, kp, vp, ap, bp = (_pieces(x) for x in (r8, w8, k8, v8, a8, b8))
            y8 = jnp.zeros((8, WB), F32)
            sa8 = jnp.zeros((8, WB), F32)
            cols = _pair_cols([(x, i) for i in range(8) for x in (ap, wp, bp, kp, rp, vp)], consts)
            for i in range(8):
                a_c, w_c, b_c, k_c, r_c = cols[6 * i:6 * i + 5]
                sa = _pair_sums(sk * a_c)
                sk = sk * w_c + b_c * _pair_rows(sa, 0) + k_c * _pair_rows(v8, i)
                y8 = _put_row(y8, i, _pair_sums(sk * r_c))
                sa8 = _put_row(sa8, i, sa)
            sa_cols = _pair_cols([(_pieces(sa8), i) for i in range(8)], consts)
            for i in range(8):
                sv = sv * _pair_rows(w8, i) + sa_cols[i] * _pair_rows(b8, i) + cols[6 * i + 5] * _pair_rows(k8, i)
                sall_ref[base + i] = sv
            y_ref[rows, :] = y8
            sa_ref[rows, :] = sa8
            return sk, sv

        state_k[...], state_v[...] = lax.fori_loop(0, cc // 8, block, (state_k[...], state_v[...]))

    row = pl.BlockSpec((cc, WB), lambda i: (i, 0))
    tile = pltpu.VMEM((4 * HD_B, LANES), F32)
    return pl.pallas_call(
        body, name=name, grid=(t // cc,), in_specs=[row] * 6,
        out_specs=[row, row, pl.BlockSpec((cc, 4 * HD_B, LANES), lambda i: (i, 0, 0))],
        out_shape=[jax.ShapeDtypeStruct((t, WB), F32)] * 2 + [jax.ShapeDtypeStruct((t, 4 * HD_B, LANES), F32)],
        scratch_shapes=[tile, tile], compiler_params=_params(1))(r, w, k, v, a, b)


def _rwkv_scan_bwd(dy, r, w, k, v, a, b, sa, sall, name):
    t = r.shape[0]
    cc = min(t, SCAN_CHUNK)
    n = t // cc

    def body(dy_ref, r_ref, w_ref, k_ref, v_ref, a_ref, b_ref, sa_ref, sall_ref, sprev_ref,
             dr_ref, dw_ref, dk_ref, dv_ref, da_ref, db_ref, dstate_k, dstate_v):
        @pl.when(pl.program_id(0) == 0)
        def _():
            dstate_k[...] = jnp.zeros_like(dstate_k)
            dstate_v[...] = jnp.zeros_like(dstate_v)

        consts = _scan_consts()
        before_chunk = jnp.where(pl.program_id(0) == n - 1, 0.0, 1.0) * sprev_ref[0]

        def block(jj, carry):
            dk_s, dv_s, sc = carry
            j = cc // 8 - 1 - jj
            base = pl.multiple_of(j * 8, 8)
            rows = pl.ds(base, 8)
            dy8, r8, w8, k8, v8, a8, b8, sa8 = (ref[rows, :] for ref in
                                                (dy_ref, r_ref, w_ref, k_ref, v_ref, a_ref, b_ref, sa_ref))
            dyp, rp, wp, kp, vp, ap, bp, sap = (_pieces(x) for x in (dy8, r8, w8, k8, v8, a8, b8, sa8))
            dsa8, dv8 = jnp.zeros((8, WB), F32), jnp.zeros((8, WB), F32)
            steps = range(7, -1, -1)
            cols_k = _pair_cols([(x, i) for i in steps for x in (rp, bp, kp, wp, ap)], consts)
            cols_v = _pair_cols([(x, i) for i in steps for x in (dyp, vp, sap)], consts)
            for n_done, i in enumerate(steps):
                r_c, b_c, k_c, w_c, a_c = cols_k[5 * n_done:5 * n_done + 5]
                dk_s = dk_s + r_c * _pair_rows(dy8, i)
                dsa = _pair_sums(dk_s * b_c)
                dv8 = _put_row(dv8, i, _pair_sums(dk_s * k_c))
                dsa8 = _put_row(dsa8, i, dsa)
                dk_s = dk_s * w_c + a_c * _pair_rows(dsa, 0)
            dsa_cols = _pair_cols([(_pieces(dsa8), i) for i in steps], consts)
            outs = [jnp.zeros((8, WB), F32) for _ in range(5)]
            for n_done, i in enumerate(steps):
                if i > 0:
                    sp = sall_ref[base + i - 1]
                else:
                    sp = jnp.where(j == 0, before_chunk, sall_ref[jnp.maximum(base - 1, 0)])
                dy_c, v_c, sa_c = cols_v[3 * n_done:3 * n_done + 3]
                dsa_c = dsa_cols[n_done]
                dv_s = dv_s + dy_c * _pair_rows(r8, i)
                vals = (_pair_sums(sc * dy_c), _pair_sums(dv_s * sp), _pair_sums(dv_s * v_c),
                        _pair_sums(sp * dsa_c), _pair_sums(dv_s * sa_c))
                outs = [_put_row(o, i, val) for o, val in zip(outs, vals)]
                dv_s = dv_s * _pair_rows(w8, i) + dsa_c * _pair_rows(a8, i)
                sc = sp
            dr8, dw8, dk8, da8, db8 = outs
            for ref, o in zip((dr_ref, dw_ref, dk_ref, dv_ref, da_ref, db_ref), (dr8, dw8, dk8, dv8, da8, db8)):
                ref[rows, :] = o
            return dk_s, dv_s, sc

        dk_s, dv_s, _ = lax.fori_loop(0, cc // 8, block, (dstate_k[...], dstate_v[...], sall_ref[cc - 1]))
        dstate_k[...] = dk_s
        dstate_v[...] = dv_s

    row = pl.BlockSpec((cc, WB), lambda i: (n - 1 - i, 0))
    tile = pltpu.VMEM((4 * HD_B, LANES), F32)
    return pl.pallas_call(
        body, name=name, grid=(n,),
        in_specs=[row] * 8 + [pl.BlockSpec((cc, 4 * HD_B, LANES), lambda i: (n - 1 - i, 0, 0)),
                              pl.BlockSpec((1, 4 * HD_B, LANES), lambda i: (jnp.maximum((n - 1 - i) * cc - 1, 0), 0, 0))],
        out_specs=[row] * 6, out_shape=[jax.ShapeDtypeStruct((t, WB), F32)] * 6,
        scratch_shapes=[tile, tile], compiler_params=_params(1))(dy, r, w, k, v, a, b, sa, sall, sall)


def _rwkv_post(y, r, k2, v, g, r_k, gn_w, gn_b, name):
    t = y.shape[0]
    tb = _tile(t, 256)

    def body(y_ref, r_ref, k_ref, v_ref, g_ref, rk_ref, gw_ref, gb_ref, o_ref):
        ones = _head_ones(WB, HD_B)
        yv = y_ref[...]
        yc = yv - _split_dot(yv, ones, 3) * (1.0 / HD_B)
        rstd = lax.rsqrt(_split_dot(yc * yc, ones, 3) * (1.0 / HD_B) + GN_EPS)
        rk = _split_dot(r_ref[...] * k_ref[...] * rk_ref[...], ones, 3)
        o_ref[...] = ((yc * rstd * gw_ref[...] + gb_ref[...] + rk * v_ref[...]) * g_ref[...]).astype(MXU)

    row = pl.BlockSpec((tb, WB), lambda i: (i, 0))
    vec = pl.BlockSpec((1, WB), lambda i: (0, 0))
    return pl.pallas_call(
        body, name=name, grid=(t // tb,), in_specs=[row] * 5 + [vec] * 3, out_specs=row,
        out_shape=jax.ShapeDtypeStruct((t, WB), MXU), compiler_params=_params(1))(y, r, k2, v, g, r_k, gn_w, gn_b)


def _rwkv_post_bwd(dob, y, r, k2, v, g, r_k, gn_w, gn_b, name):
    t = y.shape[0]
    tb = _tile(t, 256)

    def body(do_ref, y_ref, r_ref, k_ref, v_ref, g_ref, rk_ref, gw_ref, gb_ref,
             dy_ref, dg_ref, dr_ref, dk_ref, dv_ref, dgw_ref, dgb_ref, drk_ref):
        @pl.when(pl.program_id(0) == 0)
        def _():
            dgw_ref[...] = jnp.zeros_like(dgw_ref)
            dgb_ref[...] = jnp.zeros_like(dgb_ref)
            drk_ref[...] = jnp.zeros_like(drk_ref)

        ones = _head_ones(WB, HD_B)
        seg = lambda x: _split_dot(x, ones, 3)
        yv, rv, kv, vv, gv = y_ref[...], r_ref[...], k_ref[...], v_ref[...], g_ref[...]
        yc = yv - seg(yv) * (1.0 / HD_B)
        rstd = lax.rsqrt(seg(yc * yc) * (1.0 / HD_B) + GN_EPS)
        yn = yc * rstd
        rk = seg(rv * kv * rk_ref[...])
        dob_v = do_ref[...]
        dg_ref[...] = dob_v * (yn * gw_ref[...] + gb_ref[...] + rk * vv)
        dyg = dob_v * gv
        dgw_ref[...] += jnp.sum(dyg * yn, axis=0, keepdims=True)
        dgb_ref[...] += jnp.sum(dyg, axis=0, keepdims=True)
        dyn = dyg * gw_ref[...]
        dy_ref[...] = rstd * (dyn - (seg(dyn) + yn * seg(dyn * yn)) * (1.0 / HD_B))
        drk = seg(dyg * vv)
        dv_ref[...] = dyg * rk
        dr_ref[...] = drk * kv * rk_ref[...]
        dk_ref[...] = drk * rv * rk_ref[...]
        drk_ref[...] += jnp.sum(drk * rv * kv, axis=0, keepdims=True)

    row = pl.BlockSpec((tb, WB), lambda i: (i, 0))
    vec = pl.BlockSpec((1, WB), lambda i: (0, 0))
    full, small = jax.ShapeDtypeStruct((t, WB), F32), jax.ShapeDtypeStruct((1, WB), F32)
    return pl.pallas_call(
        body, name=name, grid=(t // tb,), in_specs=[row] * 6 + [vec] * 3, out_specs=[row] * 5 + [vec] * 3,
        out_shape=[full] * 5 + [small] * 3, compiler_params=_params(1))(dob, y, r, k2, v, g, r_k, gn_w, gn_b)


def _rwkv_prep_bwd(grads, p_all, mu_pad, w2cat, w0, a0, k_k, k_a, name):
    t = p_all.shape[0]
    tb = _tile(t, 256)

    def body(*refs):
        g_refs, ins, outs = refs[:10], refs[10:27], refs[27:]
        dr_s, dw, dk2_s, dv_s, das, dbs, dg, dr_b, dk2_b, dv_b = (ref[...] for ref in g_refs)
        dr_ref, dk_ref, dv_ref, dlo_ref, dw2_ref, dw0_ref, da0_ref, dkk_ref, dka_ref = outs

        @pl.when(pl.program_id(0) == 0)
        def _():
            for ref in (dw2_ref, dw0_ref, da0_ref, dkk_ref, dka_ref):
                ref[...] = jnp.zeros_like(ref)

        ones = _head_ones(WB, HD_B)
        q = _rwkv_inputs(ins, pl.program_id(0) == 0, ones)
        kk_w, ka_w = ins[15][...], ins[16][...]
        a, kk, k = q["a"], q["kk"], q["k"]
        dk2 = dk2_s + dk2_b
        dkk = dbs * a - das
        da = dbs * kk + dk2 * k * ka_w
        dk = dk2 * (1.0 + (a - 1.0) * ka_w)
        dka_ref[...] += jnp.sum(dk2 * k * (a - 1.0), axis=0, keepdims=True)
        proj = jnp.where(q["nrm"] > L2_EPS, _split_dot(dkk * kk, ones, 3), 0.0)
        dkk0 = (dkk - kk * proj) / q["den"]
        dk = dk + dkk0 * kk_w
        dkk_ref[...] += jnp.sum(dkk0 * k, axis=0, keepdims=True)
        dal = da * a * (1.0 - a)
        da0_ref[...] += jnp.sum(dal, axis=0, keepdims=True)
        sg = q["sg"]
        dwl = dw * q["decay"] * (-DECAY_C) * sg * (1.0 - sg)
        dw0_ref[...] += jnp.sum(dwl, axis=0, keepdims=True)
        dlin = jnp.concatenate([dwl, dal, dg], axis=1).astype(MXU)
        dw2_ref[...] += _dot(q["z"].astype(MXU), dlin, _TN)
        dz = _dot(dlin, ins[12][...], _NT)
        dlo_ref[...] = dz * _lora_dact(q["lo"])
        dr_ref[...] = dr_s + dr_b
        dk_ref[...] = dk
        dv_ref[...] = dv_s + dv_b

    row = pl.BlockSpec((tb, WB), lambda i: (i, 0))
    vec = pl.BlockSpec((1, WB), lambda i: (0, 0))
    full, small = jax.ShapeDtypeStruct((t, WB), F32), jax.ShapeDtypeStruct((1, WB), F32)
    return pl.pallas_call(
        body, name=name, grid=(t // tb,), in_specs=[row] * 10 + _rwkv_in_specs(t, tb),
        out_specs=[row] * 3 + [pl.BlockSpec((tb, 256), lambda i: (i, 0)),
                               pl.BlockSpec((256, 3 * WB), lambda i: (0, 0))] + [vec] * 4,
        out_shape=[full] * 3 + [jax.ShapeDtypeStruct((t, 256), F32), jax.ShapeDtypeStruct((256, 3 * WB), F32)]
        + [small] * 4, compiler_params=_params(1))(*grads, *_rwkv_in_args(p_all, mu_pad, w2cat, w0, a0, k_k, k_a))


def _shift_bwd(dshifted, p_all, mu_pad, name):
    t = p_all.shape[0]
    tb = _tile(t, 256)
    nt, nt8 = t // tb, tb // 8
    widths, cols, mus = (WB, WB, WB, 256), (COL_R, COL_K, COL_V, COL_L), (0, 1, 2, 6)

    def body(*refs):
        d_refs, n_refs, p_refs, q_refs, m_refs = refs[0:4], refs[4:8], refs[8:12], refs[12:16], refs[16:20]
        o_refs, dmu_refs = refs[20:24], refs[24:28]
        i = pl.program_id(0)

        @pl.when(i == 0)
        def _():
            for ref in dmu_refs:
                ref[...] = jnp.zeros_like(ref)

        for d, nx, p, q, m, o, dmu in zip(d_refs, n_refs, p_refs, q_refs, m_refs, o_refs, dmu_refs):
            dv, pv, mu = d[...], p[...], m[...]
            o[...] = (dv * (1.0 - mu) + mu * _shift_up(dv, nx[...], i == nt - 1)).astype(MXU)
            dmu[...] += jnp.sum(dv * (_shift_down(pv, q[...], i == 0) - pv), axis=0, keepdims=True)

    cur_d = [pl.BlockSpec((tb, w), lambda i: (i, 0)) for w in widths]
    next_d = [pl.BlockSpec((8, w), lambda i: (jnp.minimum((i + 1) * nt8, t // 8 - 1), 0)) for w in widths]
    cur_p = [pl.BlockSpec((tb, w), lambda i, j=j: (i, j)) for w, j in zip(widths, cols)]
    prev_p = [pl.BlockSpec((8, w), lambda i, j=j: (jnp.maximum(i * nt8 - 1, 0), j)) for w, j in zip(widths, cols)]
    mu_s = [pl.BlockSpec((1, w), lambda i, j=j: (0, j)) for w, j in zip(widths, mus)]
    vecs = [pl.BlockSpec((1, w), lambda i: (0, 0)) for w in widths]
    return pl.pallas_call(
        body, name=name, grid=(nt,), in_specs=cur_d + next_d + cur_p + prev_p + mu_s, out_specs=cur_d + vecs,
        out_shape=[jax.ShapeDtypeStruct((t, w), MXU) for w in widths]
        + [jax.ShapeDtypeStruct((1, w), F32) for w in widths],
        compiler_params=_params(1))(*dshifted, *dshifted, *(p_all,) * 8, *(mu_pad,) * 4)


def _peer(k):
    x, y, c = (lax.axis_index(n) for n in AXES)
    px = 1 - x if k & 4 else x
    py = 1 - y if k & 2 else y
    pc = 1 - c if k & 1 else c
    return (px, py, pc), 4 * px + 2 * py + pc


def _exchange_copy(src_refs, land_refs, send_sems, recv_sems, per_peer, j, k, arriving):
    _, me = _peer(0)
    peer, idx = _peer(k)
    sem = j * (N_DEV - 1) + k - 1
    return pltpu.make_async_remote_copy(
        src_ref=src_refs[j].at[idx] if per_peer[j] else src_refs[j],
        dst_ref=land_refs[j].at[idx if arriving else me],
        send_sem=send_sems.at[sem], recv_sem=recv_sems.at[sem],
        device_id=peer, device_id_type=pl.DeviceIdType.MESH)


def _exchange_start(srcs, per_peer, name, after=()):
    n = len(srcs)
    shapes = [tuple(s.shape[1:]) if pp else tuple(s.shape) for s, pp in zip(srcs, per_peer)]
    pairs = [(j, k) for k in range(1, N_DEV) for j in range(n)]
    first_out = 2 * n + len(after)

    def body(*refs):
        src_refs, land_refs, (send_sems, recv_sems), token = refs[:n], refs[n:2 * n], refs[first_out:first_out + 2], refs[-1]
        for j, k in pairs:
            _exchange_copy(src_refs, land_refs, send_sems, recv_sems, per_peer, j, k, False).start()
        token[...] = jnp.zeros_like(token)

    hbm, sem = pl.BlockSpec(memory_space=pltpu.HBM), pl.BlockSpec(memory_space=pltpu.SEMAPHORE)
    lands = [lax.empty((N_DEV,) + shp, s.dtype) for shp, s in zip(shapes, srcs)]
    operands = [pltpu.with_memory_space_constraint(a, pltpu.HBM) for a in list(srcs) + lands]
    n_sems = n * (N_DEV - 1)
    out = pl.pallas_call(
        body, name=name, in_specs=[hbm] * (2 * n) + [pl.BlockSpec(memory_space=pl.ANY)] * len(after),
        out_specs=[sem, sem] + [hbm] * (2 * n) + [pl.BlockSpec(memory_space=pltpu.VMEM)],
        out_shape=[pltpu.SemaphoreType.DMA((n_sems,)), pltpu.SemaphoreType.DMA((n_sems,))]
        + [pltpu.HBM(a.shape, a.dtype) for a in operands] + [jax.ShapeDtypeStruct((8, LANES), F32)],
        input_output_aliases={j: 2 + j for j in range(2 * n)},
        compiler_params=pltpu.CompilerParams(has_side_effects=pltpu.SideEffectType.DATAFLOW_SIDE_EFFECTING))(
            *operands, *after)
    return (out[0], out[1], out[2:2 + n], out[2 + n:2 + 2 * n], per_peer), out[-1]


def _exchange_wait(handle, after, name):
    send_sems, recv_sems, srcs, lands, per_peer = handle
    n = len(srcs)
    pairs = [(j, k) for k in range(1, N_DEV) for j in range(n)]

    def body(*refs):
        src_refs, land_refs, (send_sems, recv_sems) = refs[:n], refs[n:2 * n], refs[2 * n:2 * n + 2]
        for j, k in pairs:
            _exchange_copy(src_refs, land_refs, send_sems, recv_sems, per_peer, j, k, False).wait_send()
            _exchange_copy(src_refs, land_refs, send_sems, recv_sems, per_peer, j, k, True).wait_recv()

    hbm, sem = pl.BlockSpec(memory_space=pltpu.HBM), pl.BlockSpec(memory_space=pltpu.SEMAPHORE)
    out = pl.pallas_call(
        body, name=name, in_specs=[hbm] * (2 * n) + [sem, sem, pl.BlockSpec(memory_space=pl.ANY)],
        out_specs=[hbm] * (2 * n), out_shape=[pltpu.HBM(a.shape, a.dtype) for a in list(srcs) + list(lands)],
        input_output_aliases={j: j for j in range(2 * n)},
        compiler_params=pltpu.CompilerParams(has_side_effects=pltpu.SideEffectType.DATAFLOW_SIDE_EFFECTING))(
            *srcs, *lands, send_sems, recv_sems, after)
    return out[n:]


def _adam_update(g, w, m, v):
    c1, c2 = 1.0 - ADAM_B1 ** ADAM_STEP, 1.0 - ADAM_B2 ** ADAM_STEP
    nm = ADAM_B1 * m + (1.0 - ADAM_B1) * g
    nv = ADAM_B2 * v + (1.0 - ADAM_B2) * (g * g)
    return -ADAM_LR * ((nm / c1) / (jnp.sqrt(nv / c2) + ADAM_EPS) + ADAM_WD * w), nm, nv


def _row_tile(rows, cols):
    padded = -(-cols // LANES) * LANES
    cap = max(16, ADAM_BLOCK_BYTES // (N_DEV * padded * 4))
    best = 16
    for t in range(16, min(rows, cap) + 1, 16):
        if rows % t == 0:
            best = t
    return best


def _adamw(parts, w, m, v, name):
    _, rows, cols = w.shape
    tb = _row_tile(rows, cols)

    def body(p_ref, w_ref, m_ref, v_ref, g_ref, d_ref, nm_ref, nv_ref):
        g = p_ref[0].astype(F32)
        for d in range(1, N_DEV):
            g = g + p_ref[d].astype(F32)
        g_ref[0] = g
        d_ref[0], nm_ref[0], nv_ref[0] = _adam_update(g, w_ref[0], m_ref[0], v_ref[0])

    row = pl.BlockSpec((1, tb, cols), lambda i: (0, i, 0))
    out = jax.ShapeDtypeStruct(w.shape, F32)
    return pl.pallas_call(
        body, name=name, grid=(rows // tb,),
        in_specs=[pl.BlockSpec((N_DEV, tb, cols), lambda i: (0, i, 0)), row, row, row], out_specs=[row] * 4,
        out_shape=[out] * 4, compiler_params=_params(1))(parts, w, m, v)


def _adamw_small(parts, ws, ms, vs, name):
    n = len(ws)

    def body(*refs):
        p_ref = refs[0]
        w_refs, m_refs, v_refs = refs[1:1 + n], refs[1 + n:1 + 2 * n], refs[1 + 2 * n:1 + 3 * n]
        outs = refs[1 + 3 * n:]
        base = 0
        for j in range(n):
            rows, cols = ws[j].shape
            size = rows * cols
            for ch in range(-(-size // LANES)):
                r, c0 = divmod(ch * LANES, cols)
                width = min(LANES, cols - c0)
                g = p_ref[0, base + ch:base + ch + 1, 0:width]
                for d in range(1, N_DEV):
                    g = g + p_ref[d, base + ch:base + ch + 1, 0:width]
                at = (slice(r, r + 1), slice(c0, c0 + width))
                delta, nm, nv = _adam_update(g, w_refs[j][at], m_refs[j][at], v_refs[j][at])
                for out, val in zip((outs[j], outs[n + j], outs[2 * n + j], outs[3 * n + j]), (g, delta, nm, nv)):
                    out[at] = val
            base += -(-size // (8 * LANES)) * 8

    vmem = pl.BlockSpec(memory_space=pltpu.VMEM)
    res = pl.pallas_call(
        body, name=name, in_specs=[vmem] * (1 + 3 * n), out_specs=[vmem] * (4 * n),
        out_shape=[jax.ShapeDtypeStruct(a.shape, F32) for a in ws] * 4)(parts, *ws, *ms, *vs)
    return res[:n], res[n:2 * n], res[2 * n:3 * n], res[3 * n:]


def _rows(a, multiple):
    flat = a.reshape(-1)
    pad = -flat.shape[0] % (multiple * LANES)
    if pad:
        flat = jnp.concatenate([flat, jnp.zeros((pad,), a.dtype)])
    return flat.reshape(-1, LANES)


def _pack(arrs, multiple):
    return jnp.concatenate([_rows(a, multiple) for a in arrs], axis=0)


def _gathered_to_full(g, name, shard_shape):
    g = g.reshape((N_DEV,) + shard_shape)
    if name in COL_SHARDED:
        return jnp.transpose(g, (1, 0, 2)).reshape(shard_shape[0], N_DEV * shard_shape[1])
    return g.reshape(N_DEV * shard_shape[0], shard_shape[1])


def _full_to_per_device(full, name):
    if name in COL_SHARDED:
        r, c = full.shape
        return jnp.transpose(full.reshape(r, N_DEV, c // N_DEV), (1, 0, 2))
    return full.reshape(N_DEV, full.shape[0] // N_DEV, full.shape[1])


def _w2cat(w2, a2, g2):
    n_w, n_a, n_g = LORA
    out = jnp.zeros((256, 3 * WB), w2.dtype)
    out = out.at[0:n_w, 0:WB].set(w2)
    out = out.at[n_w:n_w + n_a, WB:2 * WB].set(a2)
    return out.at[n_w + n_a:n_w + n_a + n_g, 2 * WB:].set(g2)


class _Local:
    def __init__(self, w):
        self.w = w

    def weights(self, group, after=None):
        return self.w

    def send(self, grads, names):
        return ()


class _Overlapped:
    GROUPS = {"ffn1": ("ffn1_w_gate", "ffn1_w_up", "ffn1_w_down"),
              "mixer_in": ("w_in", "rwkv_w2", "rwkv_a2", "rwkv_g2"),
              "late": ("w_out", "ffn2_w_gate", "ffn2_w_up", "ffn2_w_down")}

    def __init__(self, wts):
        x, y, c = (lax.axis_index(n) for n in AXES)
        self.wts, self.me, self.gathers, self.sends = wts, 4 * x + 2 * y + c, {}, []
        self._gather("ffn1", ())

    def _gather(self, group, after):
        names = self.GROUPS[group]
        shards = [self.wts[n].astype(MXU) for n in names]
        handle, token = _exchange_start(shards, [False] * len(names), "gather_" + group, after)
        self.gathers[group] = (names, shards, handle, token)

    def _own_slot(self, land, mine):
        return lax.dynamic_update_slice(land, mine[None], (self.me,) + (0,) * mine.ndim)

    def weights(self, group, after=None):
        names, shards, handle, token = self.gathers[group]
        lands = _exchange_wait(handle, token if after is None else after, "gathered_" + group)
        w = {n: _gathered_to_full(self._own_slot(land, own), n, own.shape[1:])
             for n, own, land in zip(names, shards, lands)}
        order = list(self.GROUPS)
        if group != order[-1]:
            self._gather(order[order.index(group) + 1], (w[names[0]],))
        if group == "ffn1":
            for n in SMALL:
                keep = n in ("hgrn_lb_logits", "rwkv_r_k", "final_norm")
                w[n] = self.wts[n] if keep else self.wts[n].reshape(1, -1)
        return w

    def send(self, grads, names, small=None):
        contrib = [_full_to_per_device(grads[n], n).astype(WIRE) for n in names]
        per_peer = [True] * len(names)
        if small is not None:
            names, contrib, per_peer = names + ("small",), contrib + [small], per_peer + [False]
        handle, token = _exchange_start(contrib, per_peer, "scatter_" + names[0])
        self.sends.append((names, contrib, per_peer, handle))
        self.last_token = token
        return (token,)

    def received(self, which, after):
        names, contrib, per_peer, handle = self.sends[which]
        lands = _exchange_wait(handle, after, "scattered_" + names[0])
        parts = {}
        for n, own, pp, land in zip(names, contrib, per_peer, lands):
            mine = lax.dynamic_index_in_dim(own, self.me, 0, keepdims=False) if pp else own
            parts[n] = self._own_slot(land, mine)
        return parts


def _local_step(x, target, net):
    n_w, n_a, n_g = LORA
    w = dict(net.weights("ffn1"))
    h1 = _rms_fwd(x, w["ffn1_norm"], "ffn1_norm")
    x1 = _ffn_fwd(x, h1, w["ffn1_w_gate"], w["ffn1_w_up"], w["ffn1_w_down"], "ffn1_fwd")
    w.update(net.weights("mixer_in", x1))
    w_in_pad = jnp.pad(w["w_in"], ((0, 0), (0, N_INP - N_IN)))
    mu_pad = jnp.pad(w["rwkv_shift_mu"], ((0, 0), (0, 1792 - 1696)))
    w2cat = _w2cat(w["rwkv_w2"], w["rwkv_a2"], w["rwkv_g2"])
    r_k = w["rwkv_r_k"].reshape(1, WB)
    rw = (mu_pad, w2cat, w["rwkv_w0"], w["rwkv_a0"], w["rwkv_k_k"], w["rwkv_k_a"])

    h2 = _rms_fwd(x1, w["mix_norm"], "mix_norm")
    p_all = _matmul(h2, w_in_pad, name="in_proj")
    oa, oraw, states = _hgrn_fwd(p_all, w["hgrn_lb_logits"], w["hgrn_out_norm"], "hgrn_fwd")
    r, decay, k2, v, sa, sb, g = _rwkv_prep(p_all, *rw, "rwkv_prep")
    y, s_a, sall = _rwkv_scan_fwd(r, decay, k2, v, sa, sb, "rwkv_scan_fwd")
    post_w = (r_k, w["rwkv_gn_w"], w["rwkv_gn_b"])
    ob = _rwkv_post(y, r, k2, v, g, *post_w, "rwkv_post")
    w.update(net.weights("late", ob))
    o = jnp.concatenate([oa, ob], axis=1)
    x2 = _matmul(o, w["w_out"], res=x1, name="out_proj")
    h3 = _rms_fwd(x2, w["ffn2_norm"], "ffn2_norm")
    x3 = _ffn_fwd(x2, h3, w["ffn2_w_gate"], w["ffn2_w_up"], w["ffn2_w_down"], "ffn2_fwd")
    loss, dx3, d_final = _loss_head(x3, w["final_norm"].reshape(1, D), target, "loss_head")

    grads = {"final_norm": d_final.reshape(D)}

    def ffn_back(prefix, h, dy, x_in, norm):
        wg, wu, wd = (w[prefix + s] for s in ("_w_gate", "_w_up", "_w_down"))
        dh, act, dgate, dup, dout = _ffn_bwd(h, dy, wg, wu, wd, prefix + "_bwd")
        sent = ()
        for which, a_op, b_op in (("_w_gate", h, dgate), ("_w_up", h, dup), ("_w_down", act, dout)):
            grads[prefix + which] = _matmul(a_op, b_op, ta=True, out_dtype=WIRE, after=sent, name=prefix + "_d" + which)
            sent = net.send(grads, (prefix + which,))
        dx, grads[prefix + "_norm"] = _rms_bwd(x_in, norm, dh, dy, prefix + "_norm_bwd", after=sent)
        return dx

    dx2 = ffn_back("ffn2", h3, dx3, x2, w["ffn2_norm"])
    grads["w_out"] = _matmul(o, dx2, ta=True, out_dtype=WIRE, name="d_w_out")
    sent = net.send(grads, ("w_out",))
    do = _matmul(dx2, w["w_out"], tb=True, after=sent, name="d_mixed")
    dqa, dfa, dia, dga, grads["hgrn_out_norm"], grads["hgrn_lb_logits"] = _hgrn_bwd(
        p_all, w["hgrn_lb_logits"], w["hgrn_out_norm"], oraw, states, do[:, :WA], "hgrn_bwd")
    dy, dg, dr_b, dk2_b, dv_b, grads["rwkv_gn_w"], grads["rwkv_gn_b"], d_rk = _rwkv_post_bwd(
        do[:, WA:], y, r, k2, v, g, *post_w, "rwkv_post_bwd")
    grads["rwkv_r_k"] = d_rk.reshape(w["rwkv_r_k"].shape)
    dr, dw, dk2, dv, dsa, dsb = _rwkv_scan_bwd(dy, r, decay, k2, v, sa, sb, s_a, sall, "rwkv_scan_bwd")
    (dsr, dsk, dsv, dslo, dw2cat, grads["rwkv_w0"], grads["rwkv_a0"], grads["rwkv_k_k"],
     grads["rwkv_k_a"]) = _rwkv_prep_bwd((dr, dw, dk2, dv, dsa, dsb, dg, dr_b, dk2_b, dv_b), p_all, *rw,
                                         "rwkv_prep_bwd")
    grads["rwkv_w2"] = dw2cat[0:n_w, 0:WB]
    grads["rwkv_a2"] = dw2cat[n_w:n_w + n_a, WB:2 * WB]
    grads["rwkv_g2"] = dw2cat[n_w + n_a:n_w + n_a + n_g, 2 * WB:]
    dpr, dpk, dpv, dplo, dmu_r, dmu_k, dmu_v, dmu_lo = _shift_bwd((dsr, dsk, dsv, dslo), p_all, mu_pad, "shift_bwd")
    grads["rwkv_shift_mu"] = jnp.concatenate([dmu_r, dmu_k, dmu_v, dmu_lo], axis=1)[:, :1696]
    dp = jnp.concatenate([dqa, dfa, dia, dga, dpr, dpk, dpv, dplo], axis=1)
    grads["w_in"] = _matmul(h2, dp, ta=True, out_dtype=WIRE, name="d_w_in")[:, :N_IN]
    sent = net.send(grads, ("w_in", "rwkv_w2", "rwkv_a2", "rwkv_g2"))
    dh2 = _matmul(dp, w_in_pad, tb=True, after=sent, name="d_h2")
    dx1, grads["mix_norm"] = _rms_bwd(x1, w["mix_norm"], dh2, dx2, "mix_norm_bwd")
    dx0 = ffn_back("ffn1", h1, dx1, x, w["ffn1_norm"])
    return loss[0, 0], dx0, grads


def kernel(x, ffn1_norm, ffn1_w_gate, ffn1_w_up, ffn1_w_down, mix_norm, w_in, hgrn_lb_logits, hgrn_out_norm, rwkv_shift_mu, rwkv_w0, rwkv_w2, rwkv_a0, rwkv_a2, rwkv_g2, rwkv_k_k, rwkv_k_a, rwkv_r_k, rwkv_gn_w, rwkv_gn_b, w_out, ffn2_norm, ffn2_w_gate, ffn2_w_up, ffn2_w_down, final_norm, loss_target, m_ffn1_norm, m_ffn1_w_gate, m_ffn1_w_up, m_ffn1_w_down, m_mix_norm, m_w_in, m_hgrn_lb_logits, m_hgrn_out_norm, m_rwkv_shift_mu, m_rwkv_w0, m_rwkv_w2, m_rwkv_a0, m_rwkv_a2, m_rwkv_g2, m_rwkv_k_k, m_rwkv_k_a, m_rwkv_r_k, m_rwkv_gn_w, m_rwkv_gn_b, m_w_out, m_ffn2_norm, m_ffn2_w_gate, m_ffn2_w_up, m_ffn2_w_down, m_final_norm, v_ffn1_norm, v_ffn1_w_gate, v_ffn1_w_up, v_ffn1_w_down, v_mix_norm, v_w_in, v_hgrn_lb_logits, v_hgrn_out_norm, v_rwkv_shift_mu, v_rwkv_w0, v_rwkv_w2, v_rwkv_a0, v_rwkv_a2, v_rwkv_g2, v_rwkv_k_k, v_rwkv_k_a, v_rwkv_r_k, v_rwkv_gn_w, v_rwkv_gn_b, v_w_out, v_ffn2_norm, v_ffn2_w_gate, v_ffn2_w_up, v_ffn2_w_down, v_final_norm):
    args = dict(locals())
    wts = {n: args[n] for n in WEIGHTS}
    mom = {n: args["m_" + n] for n in WEIGHTS}
    var = {n: args["v_" + n] for n in WEIGHTS}
    net = _Overlapped(wts)
    loss, grad_x, grads = _local_step(x[0], loss_target[0], net)
    loss = lax.psum(loss, AXES)
    after, = net.send(grads, (), _pack([grads[n] for n in SMALL], 8))

    new = {}
    two_d = lambda a: a if a.ndim == 2 else a.reshape(1, -1)
    for which in range(len(net.sends)):
        for n, part in net.received(which, after).items():
            if n == "small":
                small = _adamw_small(part, *([two_d(src[k]) for k in SMALL] for src in (wts, mom, var)), "adamw_small")
                for j, k in enumerate(SMALL):
                    new[k] = [res[j].reshape(wts[k].shape) for res in small]
            else:
                new[n] = _adamw(part, wts[n], mom[n], var[n], "adamw_" + n)
                after = new[n][1]
    return (loss, grad_x[None], *[new[n][0] for n in WEIGHTS], *[new[n][1] for n in WEIGHTS],
            *[new[n][2] for n in WEIGHTS], *[new[n][3] for n in WEIGHTS])
```

```python
import functools
import math

import jax
import jax.numpy as jnp
from jax import lax
from jax.experimental import pallas as pl
from jax.experimental.pallas import tpu as pltpu

F32 = jnp.float32
MXU = jnp.bfloat16
WIRE = jnp.bfloat16
D = 1024
FF = 2816
WA = 512
WB = 512
HD_B = 64
N_IN = 3744
N_INP = 3840
COL_R, COL_K, COL_V = 4, 5, 6
COL_L = 14
LORA = (32, 32, 96)
HG_CHUNK = 64
SCAN_CHUNK = 64
NORM_EPS = 1e-6
GN_EPS = 64e-5
L2_EPS = 1e-12
DECAY_C = math.exp(-0.5)
N_DEV = 8
LANES = 128
ADAM_BLOCK_BYTES = 4 * 1024 * 1024
MATMUL_BLOCK_BYTES = 40 * 1024 * 1024
VMEM_LIMIT = 56 * 1024 * 1024
ADAM_LR, ADAM_B1, ADAM_B2, ADAM_EPS, ADAM_WD, ADAM_STEP = 0.001, 0.9, 0.999, 1e-08, 0.01, 10
AXES = ("x", "y", "c")

SHARDED = ("ffn1_w_gate", "ffn1_w_up", "ffn1_w_down", "w_in", "rwkv_w2", "rwkv_a2", "rwkv_g2", "w_out",
           "ffn2_w_gate", "ffn2_w_up", "ffn2_w_down")
COL_SHARDED = {"ffn1_w_gate", "ffn1_w_up", "w_in", "rwkv_w2", "rwkv_a2", "rwkv_g2", "ffn2_w_gate", "ffn2_w_up"}
SMALL = ("ffn1_norm", "mix_norm", "hgrn_lb_logits", "hgrn_out_norm", "rwkv_shift_mu", "rwkv_w0", "rwkv_a0",
         "rwkv_k_k", "rwkv_k_a", "rwkv_r_k", "rwkv_gn_w", "rwkv_gn_b", "ffn2_norm", "final_norm")
WEIGHTS = ("ffn1_norm", "ffn1_w_gate", "ffn1_w_up", "ffn1_w_down", "mix_norm", "w_in", "hgrn_lb_logits",
           "hgrn_out_norm", "rwkv_shift_mu", "rwkv_w0", "rwkv_w2", "rwkv_a0", "rwkv_a2", "rwkv_g2", "rwkv_k_k",
           "rwkv_k_a", "rwkv_r_k", "rwkv_gn_w", "rwkv_gn_b", "w_out", "ffn2_norm", "ffn2_w_gate", "ffn2_w_up",
           "ffn2_w_down", "final_norm")


def _tile(n, cap):
    if n <= cap:
        return n
    for t in range(cap - cap % LANES, 0, -LANES):
        if n % t == 0:
            return t
    raise ValueError((n, cap))


def _params(n_axes):
    return pltpu.CompilerParams(dimension_semantics=("arbitrary",) * n_axes, vmem_limit_bytes=VMEM_LIMIT)


def _sig(x):
    return jax.nn.sigmoid(x)


def _dsilu(z, s):
    return s * (1.0 + z * (1.0 - s))


def _dot(a, b, dims=((1,), (0,)), precision=None):
    return lax.dot_general(a, b, (dims, ((), ())), preferred_element_type=F32, precision=precision)


_NT = ((1,), (1,))
_TN = ((0,), (0,))
_HI = lax.Precision.HIGHEST


def _iota(shape, dim):
    return lax.broadcasted_iota(jnp.int32, shape, dim)


def _split_dot(x, ones, passes):
    hi = x.astype(jnp.bfloat16)
    acc = _dot(hi, ones)
    rem = x
    for _ in range(passes - 1):
        rem = rem - hi.astype(F32)
        hi = rem.astype(jnp.bfloat16)
        acc = acc + _dot(hi, ones)
    return acc


def _head_ones(n, width):
    shift = width.bit_length() - 1
    return (_iota((n, n), 0) >> shift == _iota((n, n), 1) >> shift).astype(jnp.bfloat16)


def _matmul(a, b, *, ta=False, tb=False, out_dtype=F32, res=None, after=(), name):
    m, k = (a.shape[1], a.shape[0]) if ta else a.shape
    n = b.shape[0] if tb else b.shape[1]
    tm, tn = _tile(m, 1408), _tile(n, 1408)
    in_bytes = max(a.dtype.itemsize, b.dtype.itemsize)
    for tk in (_tile(k, 1024), _tile(k, 512), _tile(k, 256)):
        if 2 * (tm + tn) * tk * in_bytes + 3 * tm * tn * 4 <= MATMUL_BLOCK_BYTES:
            break
    nk = k // tk
    dims = ((0 if ta else 1,), (1 if tb else 0,))

    def body(*refs):
        a_ref, b_ref = refs[:2]
        o_ref, acc = refs[-2:]
        kk = pl.program_id(2)

        @pl.when(kk == 0)
        def _():
            acc[...] = jnp.zeros_like(acc)

        acc[...] += _dot(a_ref[...].astype(MXU), b_ref[...].astype(MXU), dims)

        @pl.when(kk == nk - 1)
        def _():
            v = acc[...]
            if res is not None:
                v = v + refs[2][...]
            o_ref[...] = v.astype(out_dtype)

    a_spec = pl.BlockSpec((tk, tm), lambda i, j, kk: (kk, i)) if ta else pl.BlockSpec((tm, tk), lambda i, j, kk: (i, kk))
    b_spec = pl.BlockSpec((tn, tk), lambda i, j, kk: (j, kk)) if tb else pl.BlockSpec((tk, tn), lambda i, j, kk: (kk, j))
    o_spec = pl.BlockSpec((tm, tn), lambda i, j, kk: (i, j))
    ins, specs = [a, b], [a_spec, b_spec]
    if res is not None:
        ins.append(res)
        specs.append(o_spec)
    ins += list(after)
    specs += [pl.BlockSpec(memory_space=pl.ANY)] * len(after)
    return pl.pallas_call(
        body, name=name, grid=(m // tm, n // tn, nk), in_specs=specs, out_specs=o_spec,
        out_shape=jax.ShapeDtypeStruct((m, n), out_dtype), scratch_shapes=[pltpu.VMEM((tm, tn), F32)],
        compiler_params=_params(3))(*ins)


def _rms_fwd(x, g, name):
    t = x.shape[0]
    tb = _tile(t, 512)

    def body(x_ref, g_ref, o_ref):
        xv = x_ref[...]
        rinv = lax.rsqrt(jnp.mean(xv * xv, axis=-1, keepdims=True) + NORM_EPS)
        o_ref[...] = (xv * rinv * g_ref[...]).astype(MXU)

    return pl.pallas_call(
        body, name=name, grid=(t // tb,),
        in_specs=[pl.BlockSpec((tb, D), lambda i: (i, 0)), pl.BlockSpec((1, D), lambda i: (0, 0))],
        out_specs=pl.BlockSpec((tb, D), lambda i: (i, 0)), out_shape=jax.ShapeDtypeStruct((t, D), MXU),
        compiler_params=_params(1))(x, g)


def _rms_bwd(x, g, dh, dres, name, after=()):
    t = x.shape[0]
    tb = _tile(t, 512)

    def body(x_ref, g_ref, dh_ref, dres_ref, *rest):
        dx_ref, dg_ref = rest[-2:]

        @pl.when(pl.program_id(0) == 0)
        def _():
            dg_ref[...] = jnp.zeros_like(dg_ref)

        xv = x_ref[...]
        rinv = lax.rsqrt(jnp.mean(xv * xv, axis=-1, keepdims=True) + NORM_EPS)
        xhat = xv * rinv
        dhv = dh_ref[...]
        dg_ref[...] += jnp.sum(dhv * xhat, axis=0, keepdims=True)
        dxhat = dhv * g_ref[...]
        dx_ref[...] = dres_ref[...] + rinv * (dxhat - xhat * jnp.mean(dxhat * xhat, axis=-1, keepdims=True))

    row = pl.BlockSpec((tb, D), lambda i: (i, 0))
    vec = pl.BlockSpec((1, D), lambda i: (0, 0))
    return pl.pallas_call(
        body, name=name, grid=(t // tb,),
        in_specs=[row, vec, row, row] + [pl.BlockSpec(memory_space=pl.ANY)] * len(after), out_specs=[row, vec],
        out_shape=[jax.ShapeDtypeStruct((t, D), F32), jax.ShapeDtypeStruct((1, D), F32)],
        compiler_params=_params(1))(x, g, dh, dres, *after)


def _loss_head(x, g, target, name):
    t = x.shape[0]
    tb = _tile(t, 512)

    def body(x_ref, g_ref, t_ref, loss_ref, dx_ref, dg_ref):
        @pl.when(pl.program_id(0) == 0)
        def _():
            dg_ref[...] = jnp.zeros_like(dg_ref)
            loss_ref[...] = jnp.zeros_like(loss_ref)

        xv = x_ref[...]
        gv = g_ref[...]
        rinv = lax.rsqrt(jnp.mean(xv * xv, axis=-1, keepdims=True) + NORM_EPS)
        xhat = xv * rinv
        err = xhat * gv - t_ref[...]
        per_tok = jnp.mean(err * err, axis=-1, keepdims=True)
        loss_ref[...] += jnp.broadcast_to(0.5 * jnp.sum(per_tok, axis=0, keepdims=True), loss_ref.shape)
        dy = err * (1.0 / D)
        dg_ref[...] += jnp.sum(dy * xhat, axis=0, keepdims=True)
        dxhat = dy * gv
        dx_ref[...] = rinv * (dxhat - xhat * jnp.mean(dxhat * xhat, axis=-1, keepdims=True))

    row = pl.BlockSpec((tb, D), lambda i: (i, 0))
    vec = pl.BlockSpec((1, D), lambda i: (0, 0))
    return pl.pallas_call(
        body, name=name, grid=(t // tb,), in_specs=[row, vec, row],
        out_specs=[pl.BlockSpec((1, LANES), lambda i: (0, 0)), row, vec],
        out_shape=[jax.ShapeDtypeStruct((1, LANES), F32), jax.ShapeDtypeStruct((t, D), F32),
                   jax.ShapeDtypeStruct((1, D), F32)],
        compiler_params=_params(1))(x, g, target)


def _ffn_fwd(x, h, wg, wu, wd, name, after=()):
    t = x.shape[0]
    tb, fb = _tile(t, 1024), 256
    nf = FF // fb

    def body(x_ref, h_ref, wg_ref, wu_ref, wd_ref, *rest):
        o_ref, acc = rest[-2:]
        f = pl.program_id(1)

        @pl.when(f == 0)
        def _():
            acc[...] = jnp.zeros_like(acc)

        hv = h_ref[...]
        gate = _dot(hv, wg_ref[...])
        up = _dot(hv, wu_ref[...])
        act = (gate * _sig(gate) * up).astype(MXU)
        acc[...] += _dot(act, wd_ref[...])

        @pl.when(f == nf - 1)
        def _():
            o_ref[...] = x_ref[...] + 0.5 * acc[...]

    row = pl.BlockSpec((tb, D), lambda i, f: (i, 0))
    col = pl.BlockSpec((D, fb), lambda i, f: (0, f))
    return pl.pallas_call(
        body, name=name, grid=(t // tb, nf),
        in_specs=[row, row, col, col, pl.BlockSpec((fb, D), lambda i, f: (f, 0))]
        + [pl.BlockSpec(memory_space=pl.ANY)] * len(after), out_specs=row,
        out_shape=jax.ShapeDtypeStruct((t, D), F32), scratch_shapes=[pltpu.VMEM((tb, D), F32)],
        compiler_params=_params(2))(x, h, wg, wu, wd, *after)


def _ffn_bwd(h, dy, wg, wu, wd, name):
    t = h.shape[0]
    tb, fb = _tile(t, 1024), 256
    nf = FF // fb

    def body(h_ref, dy_ref, wg_ref, wu_ref, wd_ref, dh_ref, act_ref, dg_ref, du_ref, dout_ref, acc):
        f = pl.program_id(1)

        @pl.when(f == 0)
        def _():
            acc[...] = jnp.zeros_like(acc)

        hv = h_ref[...]
        dout = (0.5 * dy_ref[...]).astype(MXU)
        dout_ref[...] = dout
        gate = _dot(hv, wg_ref[...])
        up = _dot(hv, wu_ref[...])
        dact = _dot(dout, wd_ref[...], _NT)
        s = _sig(gate)
        silu = gate * s
        act_ref[...] = (silu * up).astype(MXU)
        dup = (dact * silu).astype(MXU)
        dgate = (dact * up * _dsilu(gate, s)).astype(MXU)
        du_ref[...] = dup
        dg_ref[...] = dgate
        acc[...] += _dot(dgate, wg_ref[...], _NT) + _dot(dup, wu_ref[...], _NT)

        @pl.when(f == nf - 1)
        def _():
            dh_ref[...] = acc[...]

    row = pl.BlockSpec((tb, D), lambda i, f: (i, 0))
    col = pl.BlockSpec((D, fb), lambda i, f: (0, f))
    hid = pl.BlockSpec((tb, fb), lambda i, f: (i, f))
    hid_shape = jax.ShapeDtypeStruct((t, FF), MXU)
    return pl.pallas_call(
        body, name=name, grid=(t // tb, nf),
        in_specs=[row, row, col, col, pl.BlockSpec((fb, D), lambda i, f: (f, 0))],
        out_specs=[row, hid, hid, hid, row],
        out_shape=[jax.ShapeDtypeStruct((t, D), F32), hid_shape, hid_shape, hid_shape,
                   jax.ShapeDtypeStruct((t, D), MXU)],
        scratch_shapes=[pltpu.VMEM((tb, D), F32)], compiler_params=_params(2))(h, dy, wg, wu, wd)


def _hgrn_chunk(qa, fa, lbl):
    c = HG_CHUNK
    lb = _sig(lbl[0:1, :] - lbl[1:2, :])
    sf = _sig(fa)
    forget = lb + (1.0 - lb) * sf
    kh = 1.0 - forget
    row, col = _iota((c, c), 0), _iota((c, c), 1)
    b = _dot((col <= row).astype(F32), jnp.log(forget), precision=_HI)
    bref, blast = b[c // 2:c // 2 + 1, :], b[c - 1:c, :]
    sq = _sig(qa)
    q = qa * sq
    qt, kt = q * jnp.exp(b - bref), kh * jnp.exp(bref - b)
    qb, kl = q * jnp.exp(b), kh * jnp.exp(blast - b)
    causal = col <= row
    return dict(lb=lb, sf=sf, forget=forget, sq=sq, qt=qt, kt=kt, qb=qb, kl=kl, decay=jnp.exp(blast),
                causal=causal, e_q=jnp.exp(b), e_qt=jnp.exp(b - bref), e_kt=jnp.exp(bref - b),
                e_kl=jnp.exp(blast - b))


def _hgrn_specs(t):
    c = HG_CHUNK
    return c, t // c, WA // LANES


def _hgrn_fwd(p_all, lbl, onorm, name):
    t = p_all.shape[0]
    c, n, nh = _hgrn_specs(t)

    def body(q_ref, f_ref, i_ref, g_ref, lbl_ref, on_ref, oa_ref, oraw_ref, st_ref, state):
        @pl.when(pl.program_id(0) == 0)
        def _():
            state[...] = jnp.zeros_like(state)

        for h in range(nh):
            at = slice(h * LANES, (h + 1) * LANES)
            k = _hgrn_chunk(q_ref[:, at], f_ref[:, at], lbl_ref[:, at])
            v = i_ref[:, at]
            st = state[h]
            st_ref[h, 0] = st
            a = jnp.where(k["causal"], _dot(k["qt"], k["kt"], _NT, _HI), 0.0)
            o = _dot(a, v, precision=_HI) + _dot(k["qb"], st, _NT, _HI)
            state[h] = st * k["decay"] + _dot(v, k["kl"], _TN, _HI)
            oraw_ref[:, at] = o
            rinv = lax.rsqrt(jnp.mean(o * o, axis=-1, keepdims=True) + NORM_EPS)
            ga = g_ref[:, at]
            oa_ref[:, at] = (o * rinv * on_ref[:, at] * (ga * _sig(ga))).astype(MXU)

    def blk(j):
        return pl.BlockSpec((c, WA), lambda i: (i, j))

    return pl.pallas_call(
        body, name=name, grid=(n,),
        in_specs=[blk(0), blk(1), blk(2), blk(3), pl.BlockSpec((2, WA), lambda i: (0, 0)),
                  pl.BlockSpec((1, WA), lambda i: (0, 0))],
        out_specs=[blk(0), blk(0), pl.BlockSpec((nh, 1, LANES, LANES), lambda i: (0, i, 0, 0))],
        out_shape=[jax.ShapeDtypeStruct((t, WA), MXU), jax.ShapeDtypeStruct((t, WA), F32),
                   jax.ShapeDtypeStruct((nh, n, LANES, LANES), F32)],
        scratch_shapes=[pltpu.VMEM((nh, LANES, LANES), F32)], compiler_params=_params(1))(
            p_all, p_all, p_all, p_all, lbl, onorm)


def _hgrn_bwd(p_all, lbl, onorm, oraw, states, doa, name):
    t = p_all.shape[0]
    c, n, nh = _hgrn_specs(t)

    def body(q_ref, f_ref, i_ref, g_ref, lbl_ref, on_ref, oraw_ref, st_ref, doa_ref,
             dq_ref, df_ref, di_ref, dg_ref, don_ref, dlbl_ref, dstate, dlb):
        @pl.when(pl.program_id(0) == 0)
        def _():
            dstate[...] = jnp.zeros_like(dstate)
            dlb[...] = jnp.zeros_like(dlb)
            don_ref[...] = jnp.zeros_like(don_ref)

        for h in range(nh):
            at = slice(h * LANES, (h + 1) * LANES)
            qa, fa, v, ga = q_ref[:, at], f_ref[:, at], i_ref[:, at], g_ref[:, at]
            k = _hgrn_chunk(qa, fa, lbl_ref[:, at])
            st, dst_next = st_ref[h, 0], dstate[h]
            o = oraw_ref[:, at]
            gain = on_ref[:, at]
            rinv = lax.rsqrt(jnp.mean(o * o, axis=-1, keepdims=True) + NORM_EPS)
            on = o * rinv
            sg = _sig(ga)
            gate = ga * sg
            dout = doa_ref[:, at]
            don_ref[:, at] += jnp.sum(dout * on * gate, axis=0, keepdims=True)
            dg_ref[:, at] = (dout * on * gain * _dsilu(ga, sg)).astype(MXU)
            d_on = dout * gain * gate
            do = rinv * (d_on - on * jnp.mean(d_on * on, axis=-1, keepdims=True))

            a = jnp.where(k["causal"], _dot(k["qt"], k["kt"], _NT, _HI), 0.0)
            dqb = _dot(do, st, precision=_HI)
            dstate[h] = dst_next * k["decay"] + _dot(do, k["qb"], _TN, _HI)
            da = jnp.where(k["causal"], _dot(do, v, _NT, _HI), 0.0)
            dqt = _dot(da, k["kt"], precision=_HI)
            dkt = _dot(da, k["qt"], _TN, _HI)
            dv = _dot(a, do, _TN, _HI) + _dot(k["kl"], dst_next, _NT, _HI)
            dkl = _dot(v, dst_next, precision=_HI)
            ddecay = jnp.sum(dst_next * st, axis=0, keepdims=True)
            dq = dqb * k["e_q"] + dqt * k["e_qt"]
            dk = dkt * k["e_kt"] + dkl * k["e_kl"]
            tq, tk, tl = dqt * k["qt"], dkt * k["kt"], dkl * k["kl"]
            db = dqb * k["qb"] + tq - tk - tl
            dbref = jnp.sum(tk - tq, axis=0, keepdims=True)
            dblast = jnp.sum(tl, axis=0, keepdims=True) + ddecay * k["decay"]
            rows = _iota((c, LANES), 0)
            db = db + jnp.where(rows == c // 2, dbref, 0.0) + jnp.where(rows == c - 1, dblast, 0.0)
            row, col = _iota((c, c), 0), _iota((c, c), 1)
            dlogf = _dot((col >= row).astype(F32), db, precision=_HI)
            dq_ref[:, at] = (dq * _dsilu(qa, k["sq"])).astype(MXU)
            di_ref[:, at] = dv.astype(MXU)
            dforget = dlogf / k["forget"] - dk
            sf, lb = k["sf"], k["lb"]
            df_ref[:, at] = (dforget * (1.0 - lb) * sf * (1.0 - sf)).astype(MXU)
            dlb[:, at] += jnp.sum(dforget * (1.0 - sf), axis=0, keepdims=True)
            dl0 = dlb[:, at] * lb * (1.0 - lb)
            dlbl_ref[:, at] = jnp.where(_iota((2, LANES), 0) == 0, dl0, -dl0)

    def blk(j):
        return pl.BlockSpec((c, WA), lambda i: (n - 1 - i, j))

    vec = pl.BlockSpec((1, WA), lambda i: (0, 0))
    lg = pl.BlockSpec((2, WA), lambda i: (0, 0))
    grad = jax.ShapeDtypeStruct((t, WA), MXU)
    return pl.pallas_call(
        body, name=name, grid=(n,),
        in_specs=[blk(0), blk(1), blk(2), blk(3), lg, vec, blk(0),
                  pl.BlockSpec((nh, 1, LANES, LANES), lambda i: (0, n - 1 - i, 0, 0)), blk(0)],
        out_specs=[blk(0), blk(0), blk(0), blk(0), vec, lg],
        out_shape=[grad, grad, grad, grad, jax.ShapeDtypeStruct((1, WA), F32), jax.ShapeDtypeStruct((2, WA), F32)],
        scratch_shapes=[pltpu.VMEM((nh, LANES, LANES), F32), pltpu.VMEM((1, WA), F32)],
        compiler_params=_params(1))(p_all, p_all, p_all, p_all, lbl, onorm, oraw, states, doa)


def _lora_act(x):
    lane = _iota(x.shape, 1)
    n_w, n_a, n_g = LORA
    return jnp.where(lane < n_w, jnp.tanh(x),
                     jnp.where(lane < n_w + n_a, x, jnp.where(lane < n_w + n_a + n_g, _sig(x), 0.0)))


def _lora_dact(x):
    lane = _iota(x.shape, 1)
    n_w, n_a, n_g = LORA
    th, s = jnp.tanh(x), _sig(x)
    return jnp.where(lane < n_w, 1.0 - th * th,
                     jnp.where(lane < n_w + n_a, 1.0, jnp.where(lane < n_w + n_a + n_g, s * (1.0 - s), 0.0)))


def _shift_down(cur, prev8, first):
    rolled = pltpu.roll(cur, 1, 0)
    edge = prev8[7:8, :] * jnp.where(first, 0.0, 1.0)
    return jnp.where(_iota(cur.shape, 0) == 0, edge, rolled)


def _shift_up(cur, next8, last):
    rows = cur.shape[0]
    rolled = pltpu.roll(cur, rows - 1, 0)
    edge = next8[0:1, :] * jnp.where(last, 0.0, 1.0)
    return jnp.where(_iota(cur.shape, 0) == rows - 1, edge, rolled)


def _rwkv_inputs(refs, first, ones):
    (pr, pk, pv, plo, qr, qk, qv, qlo, mr, mk, mv, mlo, w2c, w0, a0, kk_w, ka_w) = refs
    mix = lambda cur, prev, mu: cur[...] + mu[...] * (_shift_down(cur[...], prev[...], first) - cur[...])
    r, k, v, lo = mix(pr, qr, mr), mix(pk, qk, mk), mix(pv, qv, mv), mix(plo, qlo, mlo)
    z = _lora_act(lo)
    lin = _dot(z.astype(MXU), w2c[...])
    sg = _sig(w0[...] + lin[:, :WB])
    decay = jnp.exp(-DECAY_C * sg)
    a = _sig(a0[...] + lin[:, WB:2 * WB])
    g = lin[:, 2 * WB:]
    kk0 = k * kk_w[...]
    nrm = jnp.sqrt(_split_dot(kk0 * kk0, ones, 3))
    den = jnp.maximum(nrm, L2_EPS)
    kk = kk0 / den
    k2 = k * (1.0 + (a - 1.0) * ka_w[...])
    return dict(r=r, k=k, v=v, lo=lo, z=z, sg=sg, decay=decay, a=a, g=g, kk=kk, den=den, nrm=nrm, k2=k2)


def _rwkv_in_specs(t, tb):
    nt8 = tb // 8

    def cur(w, j):
        return pl.BlockSpec((tb, w), lambda i: (i, j))

    def prev(w, j):
        return pl.BlockSpec((8, w), lambda i: (jnp.maximum(i * nt8 - 1, 0), j))

    def vec(w, j=0):
        return pl.BlockSpec((1, w), lambda i: (0, j))

    return [cur(WB, COL_R), cur(WB, COL_K), cur(WB, COL_V), cur(256, COL_L),
            prev(WB, COL_R), prev(WB, COL_K), prev(WB, COL_V), prev(256, COL_L),
            vec(WB, 0), vec(WB, 1), vec(WB, 2), vec(256, 6),
            pl.BlockSpec((256, 3 * WB), lambda i: (0, 0)), vec(WB), vec(WB), vec(WB), vec(WB)]


def _rwkv_in_args(p_all, mu_pad, w2cat, w0, a0, k_k, k_a):
    return (p_all,) * 8 + (mu_pad,) * 4 + (w2cat, w0, a0, k_k, k_a)


def _rwkv_prep(p_all, mu_pad, w2cat, w0, a0, k_k, k_a, name):
    t = p_all.shape[0]
    tb = _tile(t, 256)

    def body(*refs):
        ins, outs = refs[:17], refs[17:]
        q = _rwkv_inputs(ins, pl.program_id(0) == 0, _head_ones(WB, HD_B))
        for ref, val in zip(outs, (q["r"], q["decay"], q["k2"], q["v"], -q["kk"], q["kk"] * q["a"], q["g"])):
            ref[...] = val

    out = pl.BlockSpec((tb, WB), lambda i: (i, 0))
    return pl.pallas_call(
        body, name=name, grid=(t // tb,), in_specs=_rwkv_in_specs(t, tb), out_specs=[out] * 7,
        out_shape=[jax.ShapeDtypeStruct((t, WB), F32)] * 7, compiler_params=_params(1))(
            *_rwkv_in_args(p_all, mu_pad, w2cat, w0, a0, k_k, k_a))


def _pair_rows(x8, i):
    return jnp.concatenate([jnp.broadcast_to(x8[i:i + 1, p * LANES:(p + 1) * LANES], (HD_B, LANES))
                            for p in range(4)], axis=0)


def _pair_sums(x):
    return jnp.concatenate([jnp.sum(x[p * HD_B:(p + 1) * HD_B], axis=0, keepdims=True) for p in range(4)], axis=1)


def _put_row(buf, i, row):
    return jnp.where(_iota(buf.shape, 0) == i, row, buf)


def _pieces(x):
    hi = x.astype(jnp.bfloat16).astype(F32)
    lo = (x - hi).astype(jnp.bfloat16).astype(F32)
    upper = (_iota((x.shape[0], LANES), 1) & (HD_B // 2)) != 0
    swapped = [jnp.where(upper, pltpu.roll(lo[:, p * LANES:(p + 1) * LANES], HD_B // 2, 1),
                         pltpu.roll(lo[:, p * LANES:(p + 1) * LANES], LANES - HD_B // 2, 1)) for p in range(4)]
    return hi, jnp.concatenate(swapped, axis=1)


def _scan_consts():
    row, lane = _iota((HD_B, LANES), 0), _iota((HD_B, LANES), 1) & (HD_B - 1)
    return (row == lane).astype(jnp.bfloat16), (row == lane ^ (HD_B // 2)).astype(jnp.bfloat16), _head_ones(LANES, HD_B)


def _pair_cols(many, consts):
    diag_hi, diag_lo, ones = consts
    tiles = []
    for (hi8, lo8), i in many:
        for p in range(4):
            lanes = slice(p * LANES, (p + 1) * LANES)
            hi = jnp.broadcast_to(hi8[i:i + 1, lanes], (16, LANES)).astype(jnp.bfloat16)
            lo = jnp.broadcast_to(lo8[i:i + 1, lanes], (16, LANES)).astype(jnp.bfloat16)
            for g in range(HD_B // 16):
                rows = slice(g * 16, (g + 1) * 16)
                tiles.append(hi * diag_hi[rows] + lo * diag_lo[rows])
    out = _dot(jnp.concatenate(tiles, axis=0), ones)
    return [out[m * 4 * HD_B:(m + 1) * 4 * HD_B] for m in range(len(many))]


def _rwkv_scan_fwd(r, w, k, v, a, b, name):
    t = r.shape[0]
    cc = min(t, SCAN_CHUNK)

    def body(r_ref, w_ref, k_ref, v_ref, a_ref, b_ref, y_ref, sa_ref, sall_ref, state_k, state_v):
        @pl.when(pl.program_id(0) == 0)
        def _():
            state_k[...] = jnp.zeros_like(state_k)
            state_v[...] = jnp.zeros_like(state_v)

        consts = _scan_consts()

        def block(j, carry):
            sk, sv = carry
            base = pl.multiple_of(j * 8, 8)
            rows = pl.ds(base, 8)
            r8, w8, k8, v8, a8, b8 = (ref[rows, :] for ref in (r_ref, w_ref, k_ref, v_ref, a_ref, b_ref))
            rp, wp, kp, vp, ap, bp = (_pieces(x) for x in (r8, w8, k8, v8, a8, b8))
            y8 = jnp.zeros((8, WB), F32)
            sa8 = jnp.zeros((8, WB), F32)
            cols = _pair_cols([(x, i) for i in range(8) for x in (ap, wp, bp, kp, rp, vp)], consts)
            for i in range(8):
                a_c, w_c, b_c, k_c, r_c = cols[6 * i:6 * i + 5]
                sa = _pair_sums(sk * a_c)
                sk = sk * w_c + b_c * _pair_rows(sa, 0) + k_c * _pair_rows(v8, i)
                y8 = _put_row(y8, i, _pair_sums(sk * r_c))
                sa8 = _put_row(sa8, i, sa)
            sa_cols = _pair_cols([(_pieces(sa8), i) for i in range(8)], consts)
            for i in range(8):
                sv = sv * _pair_rows(w8, i) + sa_cols[i] * _pair_rows(b8, i) + cols[6 * i + 5] * _pair_rows(k8, i)
                sall_ref[base + i] = sv
            y_ref[rows, :] = y8
            sa_ref[rows, :] = sa8
            return sk, sv

        state_k[...], state_v[...] = lax.fori_loop(0, cc // 8, block, (state_k[...], state_v[...]))

    row = pl.BlockSpec((cc, WB), lambda i: (i, 0))
    tile = pltpu.VMEM((4 * HD_B, LANES), F32)
    return pl.pallas_call(
        body, name=name, grid=(t // cc,), in_specs=[row] * 6,
        out_specs=[row, row, pl.BlockSpec((cc, 4 * HD_B, LANES), lambda i: (i, 0, 0))],
        out_shape=[jax.ShapeDtypeStruct((t, WB), F32)] * 2 + [jax.ShapeDtypeStruct((t, 4 * HD_B, LANES), F32)],
        scratch_shapes=[tile, tile], compiler_params=_params(1))(r, w, k, v, a, b)


def _rwkv_scan_bwd(dy, r, w, k, v, a, b, sa, sall, name):
    t = r.shape[0]
    cc = min(t, SCAN_CHUNK)
    n = t // cc

    def body(dy_ref, r_ref, w_ref, k_ref, v_ref, a_ref, b_ref, sa_ref, sall_ref, sprev_ref,
             dr_ref, dw_ref, dk_ref, dv_ref, da_ref, db_ref, dstate_k, dstate_v):
        @pl.when(pl.program_id(0) == 0)
        def _():
            dstate_k[...] = jnp.zeros_like(dstate_k)
            dstate_v[...] = jnp.zeros_like(dstate_v)

        consts = _scan_consts()
        before_chunk = jnp.where(pl.program_id(0) == n - 1, 0.0, 1.0) * sprev_ref[0]

        def block(jj, carry):
            dk_s, dv_s, sc = carry
            j = cc // 8 - 1 - jj
            base = pl.multiple_of(j * 8, 8)
            rows = pl.ds(base, 8)
            dy8, r8, w8, k8, v8, a8, b8, sa8 = (ref[rows, :] for ref in
                                                (dy_ref, r_ref, w_ref, k_ref, v_ref, a_ref, b_ref, sa_ref))
            dyp, rp, wp, kp, vp, ap, bp, sap = (_pieces(x) for x in (dy8, r8, w8, k8, v8, a8, b8, sa8))
            dsa8, dv8 = jnp.zeros((8, WB), F32), jnp.zeros((8, WB), F32)
            steps = range(7, -1, -1)
            cols_k = _pair_cols([(x, i) for i in steps for x in (rp, bp, kp, wp, ap)], consts)
            cols_v = _pair_cols([(x, i) for i in steps for x in (dyp, vp, sap)], consts)
            for n_done, i in enumerate(steps):
                r_c, b_c, k_c, w_c, a_c = cols_k[5 * n_done:5 * n_done + 5]
                dk_s = dk_s + r_c * _pair_rows(dy8, i)
                dsa = _pair_sums(dk_s * b_c)
                dv8 = _put_row(dv8, i, _pair_sums(dk_s * k_c))
                dsa8 = _put_row(dsa8, i, dsa)
                dk_s = dk_s * w_c + a_c * _pair_rows(dsa, 0)
            dsa_cols = _pair_cols([(_pieces(dsa8), i) for i in steps], consts)
            outs = [jnp.zeros((8, WB), F32) for _ in range(5)]
            for n_done, i in enumerate(steps):
                if i > 0:
                    sp = sall_ref[base + i - 1]
                else:
                    sp = jnp.where(j == 0, before_chunk, sall_ref[jnp.maximum(base - 1, 0)])
                dy_c, v_c, sa_c = cols_v[3 * n_done:3 * n_done + 3]
                dsa_c = dsa_cols[n_done]
                dv_s = dv_s + dy_c * _pair_rows(r8, i)
                vals = (_pair_sums(sc * dy_c), _pair_sums(dv_s * sp), _pair_sums(dv_s * v_c),
                        _pair_sums(sp * dsa_c), _pair_sums(dv_s * sa_c))
                outs = [_put_row(o, i, val) for o, val in zip(outs, vals)]
                dv_s = dv_s * _pair_rows(w8, i) + dsa_c * _pair_rows(a8, i)
                sc = sp
            dr8, dw8, dk8, da8, db8 = outs
            for ref, o in zip((dr_ref, dw_ref, dk_ref, dv_ref, da_ref, db_ref), (dr8, dw8, dk8, dv8, da8, db8)):
                ref[rows, :] = o
            return dk_s, dv_s, sc

        dk_s, dv_s, _ = lax.fori_loop(0, cc // 8, block, (dstate_k[...], dstate_v[...], sall_ref[cc - 1]))
        dstate_k[...] = dk_s
        dstate_v[...] = dv_s

    row = pl.BlockSpec((cc, WB), lambda i: (n - 1 - i, 0))
    tile = pltpu.VMEM((4 * HD_B, LANES), F32)
    return pl.pallas_call(
        body, name=name, grid=(n,),
        in_specs=[row] * 8 + [pl.BlockSpec((cc, 4 * HD_B, LANES), lambda i: (n - 1 - i, 0, 0)),
                              pl.BlockSpec((1, 4 * HD_B, LANES), lambda i: (jnp.maximum((n - 1 - i) * cc - 1, 0), 0, 0))],
        out_specs=[row] * 6, out_shape=[jax.ShapeDtypeStruct((t, WB), F32)] * 6,
        scratch_shapes=[tile, tile], compiler_params=_params(1))(dy, r, w, k, v, a, b, sa, sall, sall)


def _rwkv_post(y, r, k2, v, g, r_k, gn_w, gn_b, name):
    t = y.shape[0]
    tb = _tile(t, 256)

    def body(y_ref, r_ref, k_ref, v_ref, g_ref, rk_ref, gw_ref, gb_ref, o_ref):
        ones = _head_ones(WB, HD_B)
        yv = y_ref[...]
        yc = yv - _split_dot(yv, ones, 3) * (1.0 / HD_B)
        rstd = lax.rsqrt(_split_dot(yc * yc, ones, 3) * (1.0 / HD_B) + GN_EPS)
        rk = _split_dot(r_ref[...] * k_ref[...] * rk_ref[...], ones, 3)
        o_ref[...] = ((yc * rstd * gw_ref[...] + gb_ref[...] + rk * v_ref[...]) * g_ref[...]).astype(MXU)

    row = pl.BlockSpec((tb, WB), lambda i: (i, 0))
    vec = pl.BlockSpec((1, WB), lambda i: (0, 0))
    return pl.pallas_call(
        body, name=name, grid=(t // tb,), in_specs=[row] * 5 + [vec] * 3, out_specs=row,
        out_shape=jax.ShapeDtypeStruct((t, WB), MXU), compiler_params=_params(1))(y, r, k2, v, g, r_k, gn_w, gn_b)


def _rwkv_post_bwd(dob, y, r, k2, v, g, r_k, gn_w, gn_b, name):
    t = y.shape[0]
    tb = _tile(t, 256)

    def body(do_ref, y_ref, r_ref, k_ref, v_ref, g_ref, rk_ref, gw_ref, gb_ref,
             dy_ref, dg_ref, dr_ref, dk_ref, dv_ref, dgw_ref, dgb_ref, drk_ref):
        @pl.when(pl.program_id(0) == 0)
        def _():
            dgw_ref[...] = jnp.zeros_like(dgw_ref)
            dgb_ref[...] = jnp.zeros_like(dgb_ref)
            drk_ref[...] = jnp.zeros_like(drk_ref)

        ones = _head_ones(WB, HD_B)
        seg = lambda x: _split_dot(x, ones, 3)
        yv, rv, kv, vv, gv = y_ref[...], r_ref[...], k_ref[...], v_ref[...], g_ref[...]
        yc = yv - seg(yv) * (1.0 / HD_B)
        rstd = lax.rsqrt(seg(yc * yc) * (1.0 / HD_B) + GN_EPS)
        yn = yc * rstd
        rk = seg(rv * kv * rk_ref[...])
        dob_v = do_ref[...]
        dg_ref[...] = dob_v * (yn * gw_ref[...] + gb_ref[...] + rk * vv)
        dyg = dob_v * gv
        dgw_ref[...] += jnp.sum(dyg * yn, axis=0, keepdims=True)
        dgb_ref[...] += jnp.sum(dyg, axis=0, keepdims=True)
        dyn = dyg * gw_ref[...]
        dy_ref[...] = rstd * (dyn - (seg(dyn) + yn * seg(dyn * yn)) * (1.0 / HD_B))
        drk = seg(dyg * vv)
        dv_ref[...] = dyg * rk
        dr_ref[...] = drk * kv * rk_ref[...]
        dk_ref[...] = drk * rv * rk_ref[...]
        drk_ref[...] += jnp.sum(drk * rv * kv, axis=0, keepdims=True)

    row = pl.BlockSpec((tb, WB), lambda i: (i, 0))
    vec = pl.BlockSpec((1, WB), lambda i: (0, 0))
    full, small = jax.ShapeDtypeStruct((t, WB), F32), jax.ShapeDtypeStruct((1, WB), F32)
    return pl.pallas_call(
        body, name=name, grid=(t // tb,), in_specs=[row] * 6 + [vec] * 3, out_specs=[row] * 5 + [vec] * 3,
        out_shape=[full] * 5 + [small] * 3, compiler_params=_params(1))(dob, y, r, k2, v, g, r_k, gn_w, gn_b)


def _rwkv_prep_bwd(grads, p_all, mu_pad, w2cat, w0, a0, k_k, k_a, name):
    t = p_all.shape[0]
    tb = _tile(t, 256)

    def body(*refs):
        g_refs, ins, outs = refs[:10], refs[10:27], refs[27:]
        dr_s, dw, dk2_s, dv_s, das, dbs, dg, dr_b, dk2_b, dv_b = (ref[...] for ref in g_refs)
        dr_ref, dk_ref, dv_ref, dlo_ref, dw2_ref, dw0_ref, da0_ref, dkk_ref, dka_ref = outs

        @pl.when(pl.program_id(0) == 0)
        def _():
            for ref in (dw2_ref, dw0_ref, da0_ref, dkk_ref, dka_ref):
                ref[...] = jnp.zeros_like(ref)

        ones = _head_ones(WB, HD_B)
        q = _rwkv_inputs(ins, pl.program_id(0) == 0, ones)
        kk_w, ka_w = ins[15][...], ins[16][...]
        a, kk, k = q["a"], q["kk"], q["k"]
        dk2 = dk2_s + dk2_b
        dkk = dbs * a - das
        da = dbs * kk + dk2 * k * ka_w
        dk = dk2 * (1.0 + (a - 1.0) * ka_w)
        dka_ref[...] += jnp.sum(dk2 * k * (a - 1.0), axis=0, keepdims=True)
        proj = jnp.where(q["nrm"] > L2_EPS, _split_dot(dkk * kk, ones, 3), 0.0)
        dkk0 = (dkk - kk * proj) / q["den"]
        dk = dk + dkk0 * kk_w
        dkk_ref[...] += jnp.sum(dkk0 * k, axis=0, keepdims=True)
        dal = da * a * (1.0 - a)
        da0_ref[...] += jnp.sum(dal, axis=0, keepdims=True)
        sg = q["sg"]
        dwl = dw * q["decay"] * (-DECAY_C) * sg * (1.0 - sg)
        dw0_ref[...] += jnp.sum(dwl, axis=0, keepdims=True)
        dlin = jnp.concatenate([dwl, dal, dg], axis=1).astype(MXU)
        dw2_ref[...] += _dot(q["z"].astype(MXU), dlin, _TN)
        dz = _dot(dlin, ins[12][...], _NT)
        dlo_ref[...] = dz * _lora_dact(q["lo"])
        dr_ref[...] = dr_s + dr_b
        dk_ref[...] = dk
        dv_ref[...] = dv_s + dv_b

    row = pl.BlockSpec((tb, WB), lambda i: (i, 0))
    vec = pl.BlockSpec((1, WB), lambda i: (0, 0))
    full, small = jax.ShapeDtypeStruct((t, WB), F32), jax.ShapeDtypeStruct((1, WB), F32)
    return pl.pallas_call(
        body, name=name, grid=(t // tb,), in_specs=[row] * 10 + _rwkv_in_specs(t, tb),
        out_specs=[row] * 3 + [pl.BlockSpec((tb, 256), lambda i: (i, 0)),
                               pl.BlockSpec((256, 3 * WB), lambda i: (0, 0))] + [vec] * 4,
        out_shape=[full] * 3 + [jax.ShapeDtypeStruct((t, 256), F32), jax.ShapeDtypeStruct((256, 3 * WB), F32)]
        + [small] * 4, compiler_params=_params(1))(*grads, *_rwkv_in_args(p_all, mu_pad, w2cat, w0, a0, k_k, k_a))


def _shift_bwd(dshifted, p_all, mu_pad, name):
    t = p_all.shape[0]
    tb = _tile(t, 256)
    nt, nt8 = t // tb, tb // 8
    widths, cols, mus = (WB, WB, WB, 256), (COL_R, COL_K, COL_V, COL_L), (0, 1, 2, 6)

    def body(*refs):
        d_refs, n_refs, p_refs, q_refs, m_refs = refs[0:4], refs[4:8], refs[8:12], refs[12:16], refs[16:20]
        o_refs, dmu_refs = refs[20:24], refs[24:28]
        i = pl.program_id(0)

        @pl.when(i == 0)
        def _():
            for ref in dmu_refs:
                ref[...] = jnp.zeros_like(ref)

        for d, nx, p, q, m, o, dmu in zip(d_refs, n_refs, p_refs, q_refs, m_refs, o_refs, dmu_refs):
            dv, pv, mu = d[...], p[...], m[...]
            o[...] = (dv * (1.0 - mu) + mu * _shift_up(dv, nx[...], i == nt - 1)).astype(MXU)
            dmu[...] += jnp.sum(dv * (_shift_down(pv, q[...], i == 0) - pv), axis=0, keepdims=True)

    cur_d = [pl.BlockSpec((tb, w), lambda i: (i, 0)) for w in widths]
    next_d = [pl.BlockSpec((8, w), lambda i: (jnp.minimum((i + 1) * nt8, t // 8 - 1), 0)) for w in widths]
    cur_p = [pl.BlockSpec((tb, w), lambda i, j=j: (i, j)) for w, j in zip(widths, cols)]
    prev_p = [pl.BlockSpec((8, w), lambda i, j=j: (jnp.maximum(i * nt8 - 1, 0), j)) for w, j in zip(widths, cols)]
    mu_s = [pl.BlockSpec((1, w), lambda i, j=j: (0, j)) for w, j in zip(widths, mus)]
    vecs = [pl.BlockSpec((1, w), lambda i: (0, 0)) for w in widths]
    return pl.pallas_call(
        body, name=name, grid=(nt,), in_specs=cur_d + next_d + cur_p + prev_p + mu_s, out_specs=cur_d + vecs,
        out_shape=[jax.ShapeDtypeStruct((t, w), MXU) for w in widths]
        + [jax.ShapeDtypeStruct((1, w), F32) for w in widths],
        compiler_params=_params(1))(*dshifted, *dshifted, *(p_all,) * 8, *(mu_pad,) * 4)


def _peer(k):
    x, y, c = (lax.axis_index(n) for n in AXES)
    px = 1 - x if k & 4 else x
    py = 1 - y if k & 2 else y
    pc = 1 - c if k & 1 else c
    return (px, py, pc), 4 * px + 2 * py + pc


def _exchange_copy(src_refs, land_refs, send_sems, recv_sems, per_peer, j, k, arriving):
    _, me = _peer(0)
    peer, idx = _peer(k)
    sem = j * (N_DEV - 1) + k - 1
    return pltpu.make_async_remote_copy(
        src_ref=src_refs[j].at[idx] if per_peer[j] else src_refs[j],
        dst_ref=land_refs[j].at[idx if arriving else me],
        send_sem=send_sems.at[sem], recv_sem=recv_sems.at[sem],
        device_id=peer, device_id_type=pl.DeviceIdType.MESH)


def _exchange_start(srcs, per_peer, name, after=()):
    n = len(srcs)
    shapes = [tuple(s.shape[1:]) if pp else tuple(s.shape) for s, pp in zip(srcs, per_peer)]
    pairs = [(j, k) for k in range(1, N_DEV) for j in range(n)]
    first_out = 2 * n + len(after)

    def body(*refs):
        src_refs, land_refs, (send_sems, recv_sems), token = refs[:n], refs[n:2 * n], refs[first_out:first_out + 2], refs[-1]
        for j, k in pairs:
            _exchange_copy(src_refs, land_refs, send_sems, recv_sems, per_peer, j, k, False).start()
        token[...] = jnp.zeros_like(token)

    hbm, sem = pl.BlockSpec(memory_space=pltpu.HBM), pl.BlockSpec(memory_space=pltpu.SEMAPHORE)
    lands = [lax.empty((N_DEV,) + shp, s.dtype) for shp, s in zip(shapes, srcs)]
    operands = [pltpu.with_memory_space_constraint(a, pltpu.HBM) for a in list(srcs) + lands]
    n_sems = n * (N_DEV - 1)
    out = pl.pallas_call(
        body, name=name, in_specs=[hbm] * (2 * n) + [pl.BlockSpec(memory_space=pl.ANY)] * len(after),
        out_specs=[sem, sem] + [hbm] * (2 * n) + [pl.BlockSpec(memory_space=pltpu.VMEM)],
        out_shape=[pltpu.SemaphoreType.DMA((n_sems,)), pltpu.SemaphoreType.DMA((n_sems,))]
        + [pltpu.HBM(a.shape, a.dtype) for a in operands] + [jax.ShapeDtypeStruct((8, LANES), F32)],
        input_output_aliases={j: 2 + j for j in range(2 * n)},
        compiler_params=pltpu.CompilerParams(has_side_effects=pltpu.SideEffectType.DATAFLOW_SIDE_EFFECTING))(
            *operands, *after)
    return (out[0], out[1], out[2:2 + n], out[2 + n:2 + 2 * n], per_peer), out[-1]


def _exchange_wait(handle, after, name):
    send_sems, recv_sems, srcs, lands, per_peer = handle
    n = len(srcs)
    pairs = [(j, k) for k in range(1, N_DEV) for j in range(n)]

    def body(*refs):
        src_refs, land_refs, (send_sems, recv_sems) = refs[:n], refs[n:2 * n], refs[2 * n:2 * n + 2]
        for j, k in pairs:
            _exchange_copy(src_refs, land_refs, send_sems, recv_sems, per_peer, j, k, False).wait_send()
            _exchange_copy(src_refs, land_refs, send_sems, recv_sems, per_peer, j, k, True).wait_recv()

    hbm, sem = pl.BlockSpec(memory_space=pltpu.HBM), pl.BlockSpec(memory_space=pltpu.SEMAPHORE)
    out = pl.pallas_call(
        body, name=name, in_specs=[hbm] * (2 * n) + [sem, sem, pl.BlockSpec(memory_space=pl.ANY)],
        out_specs=[hbm] * (2 * n), out_shape=[pltpu.HBM(a.shape, a.dtype) for a in list(srcs) + list(lands)],
        input_output_aliases={j: j for j in range(2 * n)},
        compiler_params=pltpu.CompilerParams(has_side_effects=pltpu.SideEffectType.DATAFLOW_SIDE_EFFECTING))(
            *srcs, *lands, send_sems, recv_sems, after)
    return out[n:]


def _adam_update(g, w, m, v):
    c1, c2 = 1.0 - ADAM_B1 ** ADAM_STEP, 1.0 - ADAM_B2 ** ADAM_STEP
    nm = ADAM_B1 * m + (1.0 - ADAM_B1) * g
    nv = ADAM_B2 * v + (1.0 - ADAM_B2) * (g * g)
    return -ADAM_LR * ((nm / c1) / (jnp.sqrt(nv / c2) + ADAM_EPS) + ADAM_WD * w), nm, nv


def _row_tile(rows, cols):
    padded = -(-cols // LANES) * LANES
    cap = max(16, ADAM_BLOCK_BYTES // (N_DEV * padded * 4))
    best = 16
    for t in range(16, min(rows, cap) + 1, 16):
        if rows % t == 0:
            best = t
    return best


def _adamw(parts, w, m, v, name):
    _, rows, cols = w.shape
    tb = _row_tile(rows, cols)

    def body(p_ref, w_ref, m_ref, v_ref, g_ref, d_ref, nm_ref, nv_ref):
        g = p_ref[0].astype(F32)
        for d in range(1, N_DEV):
            g = g + p_ref[d].astype(F32)
        g_ref[0] = g
        d_ref[0], nm_ref[0], nv_ref[0] = _adam_update(g, w_ref[0], m_ref[0], v_ref[0])

    row = pl.BlockSpec((1, tb, cols), lambda i: (0, i, 0))
    out = jax.ShapeDtypeStruct(w.shape, F32)
    return pl.pallas_call(
        body, name=name, grid=(rows // tb,),
        in_specs=[pl.BlockSpec((N_DEV, tb, cols), lambda i: (0, i, 0)), row, row, row], out_specs=[row] * 4,
        out_shape=[out] * 4, compiler_params=_params(1))(parts, w, m, v)


def _adamw_small(parts, ws, ms, vs, name):
    n = len(ws)

    def body(*refs):
        p_ref = refs[0]
        w_refs, m_refs, v_refs = refs[1:1 + n], refs[1 + n:1 + 2 * n], refs[1 + 2 * n:1 + 3 * n]
        outs = refs[1 + 3 * n:]
        base = 0
        for j in range(n):
            rows, cols = ws[j].shape
            size = rows * cols
            for ch in range(-(-size // LANES)):
                r, c0 = divmod(ch * LANES, cols)
                width = min(LANES, cols - c0)
                g = p_ref[0, base + ch:base + ch + 1, 0:width]
                for d in range(1, N_DEV):
                    g = g + p_ref[d, base + ch:base + ch + 1, 0:width]
                at = (slice(r, r + 1), slice(c0, c0 + width))
                delta, nm, nv = _adam_update(g, w_refs[j][at], m_refs[j][at], v_refs[j][at])
                for out, val in zip((outs[j], outs[n + j], outs[2 * n + j], outs[3 * n + j]), (g, delta, nm, nv)):
                    out[at] = val
            base += -(-size // (8 * LANES)) * 8

    vmem = pl.BlockSpec(memory_space=pltpu.VMEM)
    res = pl.pallas_call(
        body, name=name, in_specs=[vmem] * (1 + 3 * n), out_specs=[vmem] * (4 * n),
        out_shape=[jax.ShapeDtypeStruct(a.shape, F32) for a in ws] * 4)(parts, *ws, *ms, *vs)
    return res[:n], res[n:2 * n], res[2 * n:3 * n], res[3 * n:]


def _rows(a, multiple):
    flat = a.reshape(-1)
    pad = -flat.shape[0] % (multiple * LANES)
    if pad:
        flat = jnp.concatenate([flat, jnp.zeros((pad,), a.dtype)])
    return flat.reshape(-1, LANES)


def _pack(arrs, multiple):
    return jnp.concatenate([_rows(a, multiple) for a in arrs], axis=0)


def _gathered_to_full(g, name, shard_shape):
    g = g.reshape((N_DEV,) + shard_shape)
    if name in COL_SHARDED:
        return jnp.transpose(g, (1, 0, 2)).reshape(shard_shape[0], N_DEV * shard_shape[1])
    return g.reshape(N_DEV * shard_shape[0], shard_shape[1])


def _full_to_per_device(full, name):
    if name in COL_SHARDED:
        r, c = full.shape
        return jnp.transpose(full.reshape(r, N_DEV, c // N_DEV), (1, 0, 2))
    return full.reshape(N_DEV, full.shape[0] // N_DEV, full.shape[1])


def _w2cat(w2, a2, g2):
    n_w, n_a, n_g = LORA
    out = jnp.zeros((256, 3 * WB), w2.dtype)
    out = out.at[0:n_w, 0:WB].set(w2)
    out = out.at[n_w:n_w + n_a, WB:2 * WB].set(a2)
    return out.at[n_w + n_a:n_w + n_a + n_g, 2 * WB:].set(g2)


class _Local:
    def __init__(self, w):
        self.w = w

    def weights(self, group, after=None):
        return self.w

    def started(self):
        return ()

    def send(self, grads, names):
        return ()


class _Overlapped:
    GROUPS = {"ffn1": ("ffn1_w_gate", "ffn1_w_up", "ffn1_w_down"),
              "mixer_in": ("w_in", "rwkv_w2", "rwkv_a2", "rwkv_g2"),
              "late": ("w_out", "ffn2_w_gate", "ffn2_w_up", "ffn2_w_down")}

    def __init__(self, wts):
        x, y, c = (lax.axis_index(n) for n in AXES)
        self.wts, self.me, self.gathers, self.sends = wts, 4 * x + 2 * y + c, {}, []
        self._gather("ffn1", ())

    def _gather(self, group, after):
        names = self.GROUPS[group]
        shards = [self.wts[n].astype(MXU) for n in names]
        handle, token = _exchange_start(shards, [False] * len(names), "gather_" + group, after)
        self.gathers[group] = (names, shards, handle, token)
        self.newest = token

    def started(self):
        return (self.newest,)

    def _own_slot(self, land, mine):
        return lax.dynamic_update_slice(land, mine[None], (self.me,) + (0,) * mine.ndim)

    def weights(self, group, after=None):
        names, shards, handle, token = self.gathers[group]
        lands = _exchange_wait(handle, token if after is None else after, "gathered_" + group)
        w = {n: _gathered_to_full(self._own_slot(land, own), n, own.shape[1:])
             for n, own, land in zip(names, shards, lands)}
        order = list(self.GROUPS)
        if group != order[-1]:
            self._gather(order[order.index(group) + 1], (w[names[0]],))
        if group == "ffn1":
            for n in SMALL:
                keep = n in ("hgrn_lb_logits", "rwkv_r_k", "final_norm")
                w[n] = self.wts[n] if keep else self.wts[n].reshape(1, -1)
        return w

    def send(self, grads, names, small=None):
        contrib = [_full_to_per_device(grads[n], n).astype(WIRE) for n in names]
        per_peer = [True] * len(names)
        if small is not None:
            names, contrib, per_peer = names + ("small",), contrib + [small], per_peer + [False]
        handle, token = _exchange_start(contrib, per_peer, "scatter_" + names[0])
        self.sends.append((names, contrib, per_peer, handle))
        self.last_token = token
        return (token,)

    def received(self, which, after):
        names, contrib, per_peer, handle = self.sends[which]
        lands = _exchange_wait(handle, after, "scattered_" + names[0])
        parts = {}
        for n, own, pp, land in zip(names, contrib, per_peer, lands):
            mine = lax.dynamic_index_in_dim(own, self.me, 0, keepdims=False) if pp else own
            parts[n] = self._own_slot(land, mine)
        return parts


def _local_step(x, target, net):
    n_w, n_a, n_g = LORA
    w = dict(net.weights("ffn1"))
    h1 = _rms_fwd(x, w["ffn1_norm"], "ffn1_norm")
    x1 = _ffn_fwd(x, h1, w["ffn1_w_gate"], w["ffn1_w_up"], w["ffn1_w_down"], "ffn1_fwd", after=net.started())
    w.update(net.weights("mixer_in", x1))
    w_in_pad = jnp.pad(w["w_in"], ((0, 0), (0, N_INP - N_IN)))
    mu_pad = jnp.pad(w["rwkv_shift_mu"], ((0, 0), (0, 1792 - 1696)))
    w2cat = _w2cat(w["rwkv_w2"], w["rwkv_a2"], w["rwkv_g2"])
    r_k = w["rwkv_r_k"].reshape(1, WB)
    rw = (mu_pad, w2cat, w["rwkv_w0"], w["rwkv_a0"], w["rwkv_k_k"], w["rwkv_k_a"])

    h2 = _rms_fwd(x1, w["mix_norm"], "mix_norm")
    p_all = _matmul(h2, w_in_pad, after=net.started(), name="in_proj")
    oa, oraw, states = _hgrn_fwd(p_all, w["hgrn_lb_logits"], w["hgrn_out_norm"], "hgrn_fwd")
    r, decay, k2, v, sa, sb, g = _rwkv_prep(p_all, *rw, "rwkv_prep")
    y, s_a, sall = _rwkv_scan_fwd(r, decay, k2, v, sa, sb, "rwkv_scan_fwd")
    post_w = (r_k, w["rwkv_gn_w"], w["rwkv_gn_b"])
    ob = _rwkv_post(y, r, k2, v, g, *post_w, "rwkv_post")
    w.update(net.weights("late", ob))
    o = jnp.concatenate([oa, ob], axis=1)
    x2 = _matmul(o, w["w_out"], res=x1, name="out_proj")
    h3 = _rms_fwd(x2, w["ffn2_norm"], "ffn2_norm")
    x3 = _ffn_fwd(x2, h3, w["ffn2_w_gate"], w["ffn2_w_up"], w["ffn2_w_down"], "ffn2_fwd")
    loss, dx3, d_final = _loss_head(x3, w["final_norm"].reshape(1, D), target, "loss_head")

    grads = {"final_norm": d_final.reshape(D)}

    def ffn_back(prefix, h, dy, x_in, norm):
        wg, wu, wd = (w[prefix + s] for s in ("_w_gate", "_w_up", "_w_down"))
        dh, act, dgate, dup, dout = _ffn_bwd(h, dy, wg, wu, wd, prefix + "_bwd")
        sent = ()
        for which, a_op, b_op in (("_w_gate", h, dgate), ("_w_up", h, dup), ("_w_down", act, dout)):
            grads[prefix + which] = _matmul(a_op, b_op, ta=True, out_dtype=WIRE, after=sent, name=prefix + "_d" + which)
            sent = net.send(grads, (prefix + which,))
        dx, grads[prefix + "_norm"] = _rms_bwd(x_in, norm, dh, dy, prefix + "_norm_bwd", after=sent)
        return dx

    dx2 = ffn_back("ffn2", h3, dx3, x2, w["ffn2_norm"])
    grads["w_out"] = _matmul(o, dx2, ta=True, out_dtype=WIRE, name="d_w_out")
    sent = net.send(grads, ("w_out",))
    do = _matmul(dx2, w["w_out"], tb=True, after=sent, name="d_mixed")
    dqa, dfa, dia, dga, grads["hgrn_out_norm"], grads["hgrn_lb_logits"] = _hgrn_bwd(
        p_all, w["hgrn_lb_logits"], w["hgrn_out_norm"], oraw, states, do[:, :WA], "hgrn_bwd")
    dy, dg, dr_b, dk2_b, dv_b, grads["rwkv_gn_w"], grads["rwkv_gn_b"], d_rk = _rwkv_post_bwd(
        do[:, WA:], y, r, k2, v, g, *post_w, "rwkv_post_bwd")
    grads["rwkv_r_k"] = d_rk.reshape(w["rwkv_r_k"].shape)
    dr, dw, dk2, dv, dsa, dsb = _rwkv_scan_bwd(dy, r, decay, k2, v, sa, sb, s_a, sall, "rwkv_scan_bwd")
    (dsr, dsk, dsv, dslo, dw2cat, grads["rwkv_w0"], grads["rwkv_a0"], grads["rwkv_k_k"],
     grads["rwkv_k_a"]) = _rwkv_prep_bwd((dr, dw, dk2, dv, dsa, dsb, dg, dr_b, dk2_b, dv_b), p_all, *rw,
                                         "rwkv_prep_bwd")
    grads["rwkv_w2"] = dw2cat[0:n_w, 0:WB]
    grads["rwkv_a2"] = dw2cat[n_w:n_w + n_a, WB:2 * WB]
    grads["rwkv_g2"] = dw2cat[n_w + n_a:n_w + n_a + n_g, 2 * WB:]
    dpr, dpk, dpv, dplo, dmu_r, dmu_k, dmu_v, dmu_lo = _shift_bwd((dsr, dsk, dsv, dslo), p_all, mu_pad, "shift_bwd")
    grads["rwkv_shift_mu"] = jnp.concatenate([dmu_r, dmu_k, dmu_v, dmu_lo], axis=1)[:, :1696]
    dp = jnp.concatenate([dqa, dfa, dia, dga, dpr, dpk, dpv, dplo], axis=1)
    grads["w_in"] = _matmul(h2, dp, ta=True, out_dtype=WIRE, name="d_w_in")[:, :N_IN]
    sent = net.send(grads, ("w_in", "rwkv_w2", "rwkv_a2", "rwkv_g2"))
    dh2 = _matmul(dp, w_in_pad, tb=True, after=sent, name="d_h2")
    dx1, grads["mix_norm"] = _rms_bwd(x1, w["mix_norm"], dh2, dx2, "mix_norm_bwd")
    dx0 = ffn_back("ffn1", h1, dx1, x, w["ffn1_norm"])
    return loss[0, 0], dx0, grads


def kernel(x, ffn1_norm, ffn1_w_gate, ffn1_w_up, ffn1_w_down, mix_norm, w_in, hgrn_lb_logits, hgrn_out_norm, rwkv_shift_mu, rwkv_w0, rwkv_w2, rwkv_a0, rwkv_a2, rwkv_g2, rwkv_k_k, rwkv_k_a, rwkv_r_k, rwkv_gn_w, rwkv_gn_b, w_out, ffn2_norm, ffn2_w_gate, ffn2_w_up, ffn2_w_down, final_norm, loss_target, m_ffn1_norm, m_ffn1_w_gate, m_ffn1_w_up, m_ffn1_w_down, m_mix_norm, m_w_in, m_hgrn_lb_logits, m_hgrn_out_norm, m_rwkv_shift_mu, m_rwkv_w0, m_rwkv_w2, m_rwkv_a0, m_rwkv_a2, m_rwkv_g2, m_rwkv_k_k, m_rwkv_k_a, m_rwkv_r_k, m_rwkv_gn_w, m_rwkv_gn_b, m_w_out, m_ffn2_norm, m_ffn2_w_gate, m_ffn2_w_up, m_ffn2_w_down, m_final_norm, v_ffn1_norm, v_ffn1_w_gate, v_ffn1_w_up, v_ffn1_w_down, v_mix_norm, v_w_in, v_hgrn_lb_logits, v_hgrn_out_norm, v_rwkv_shift_mu, v_rwkv_w0, v_rwkv_w2, v_rwkv_a0, v_rwkv_a2, v_rwkv_g2, v_rwkv_k_k, v_rwkv_k_a, v_rwkv_r_k, v_rwkv_gn_w, v_rwkv_gn_b, v_w_out, v_ffn2_norm, v_ffn2_w_gate, v_ffn2_w_up, v_ffn2_w_down, v_final_norm):
    args = dict(locals())
    wts = {n: args[n] for n in WEIGHTS}
    mom = {n: args["m_" + n] for n in WEIGHTS}
    var = {n: args["v_" + n] for n in WEIGHTS}
    net = _Overlapped(wts)
    loss, grad_x, grads = _local_step(x[0], loss_target[0], net)
    loss = lax.psum(loss, AXES)
    after, = net.send(grads, (), _pack([grads[n] for n in SMALL], 8))

    new = {}
    two_d = lambda a: a if a.ndim == 2 else a.reshape(1, -1)
    for which in range(len(net.sends)):
        for n, part in net.received(which, after).items():
            if n == "small":
                small = _adamw_small(part, *([two_d(src[k]) for k in SMALL] for src in (wts, mom, var)), "adamw_small")
                for j, k in enumerate(SMALL):
                    new[k] = [res[j].reshape(wts[k].shape) for res in small]
            else:
                new[n] = _adamw(part, wts[n], mom[n], var[n], "adamw_" + n)
                after = new[n][1]
    return (loss, grad_x[None], *[new[n][0] for n in WEIGHTS], *[new[n][1] for n in WEIGHTS],
            *[new[n][2] for n in WEIGHTS], *[new[n][3] for n in WEIGHTS])
```

```python
import functools
import math

import jax
import jax.numpy as jnp
from jax import lax
from jax.experimental import pallas as pl
from jax.experimental.pallas import tpu as pltpu

F32 = jnp.float32
MXU = jnp.bfloat16
WIRE = jnp.bfloat16
D = 1024
FF = 2816
WA = 512
WB = 512
HD_B = 64
N_IN = 3744
N_INP = 3840
COL_R, COL_K, COL_V = 4, 5, 6
COL_L = 14
LORA = (32, 32, 96)
HG_CHUNK = 64
SCAN_CHUNK = 64
SCAN_UNROLL = 4
NORM_EPS = 1e-6
GN_EPS = 64e-5
L2_EPS = 1e-12
DECAY_C = math.exp(-0.5)
N_DEV = 8
LANES = 128
ADAM_BLOCK_BYTES = 4 * 1024 * 1024
MATMUL_BLOCK_BYTES = 40 * 1024 * 1024
VMEM_LIMIT = 56 * 1024 * 1024
ADAM_LR, ADAM_B1, ADAM_B2, ADAM_EPS, ADAM_WD, ADAM_STEP = 0.001, 0.9, 0.999, 1e-08, 0.01, 10
AXES = ("x", "y", "c")

SHARDED = ("ffn1_w_gate", "ffn1_w_up", "ffn1_w_down", "w_in", "rwkv_w2", "rwkv_a2", "rwkv_g2", "w_out",
           "ffn2_w_gate", "ffn2_w_up", "ffn2_w_down")
COL_SHARDED = {"ffn1_w_gate", "ffn1_w_up", "w_in", "rwkv_w2", "rwkv_a2", "rwkv_g2", "ffn2_w_gate", "ffn2_w_up"}
SMALL = ("ffn1_norm", "mix_norm", "hgrn_lb_logits", "hgrn_out_norm", "rwkv_shift_mu", "rwkv_w0", "rwkv_a0",
         "rwkv_k_k", "rwkv_k_a", "rwkv_r_k", "rwkv_gn_w", "rwkv_gn_b", "ffn2_norm", "final_norm")
WEIGHTS = ("ffn1_norm", "ffn1_w_gate", "ffn1_w_up", "ffn1_w_down", "mix_norm", "w_in", "hgrn_lb_logits",
           "hgrn_out_norm", "rwkv_shift_mu", "rwkv_w0", "rwkv_w2", "rwkv_a0", "rwkv_a2", "rwkv_g2", "rwkv_k_k",
           "rwkv_k_a", "rwkv_r_k", "rwkv_gn_w", "rwkv_gn_b", "w_out", "ffn2_norm", "ffn2_w_gate", "ffn2_w_up",
           "ffn2_w_down", "final_norm")


def _tile(n, cap):
    if n <= cap:
        return n
    for t in range(cap - cap % LANES, 0, -LANES):
        if n % t == 0:
            return t
    raise ValueError((n, cap))


def _params(n_axes):
    return pltpu.CompilerParams(dimension_semantics=("arbitrary",) * n_axes, vmem_limit_bytes=VMEM_LIMIT)


def _sig(x):
    return jax.nn.sigmoid(x)


def _dsilu(z, s):
    return s * (1.0 + z * (1.0 - s))


def _dot(a, b, dims=((1,), (0,)), precision=None):
    return lax.dot_general(a, b, (dims, ((), ())), preferred_element_type=F32, precision=precision)


_NT = ((1,), (1,))
_TN = ((0,), (0,))
_HI = lax.Precision.HIGHEST


def _iota(shape, dim):
    return lax.broadcasted_iota(jnp.int32, shape, dim)


def _split_dot(x, ones, passes):
    hi = x.astype(jnp.bfloat16)
    acc = _dot(hi, ones)
    rem = x
    for _ in range(passes - 1):
        rem = rem - hi.astype(F32)
        hi = rem.astype(jnp.bfloat16)
        acc = acc + _dot(hi, ones)
    return acc


def _head_ones(n, width):
    shift = width.bit_length() - 1
    return (_iota((n, n), 0) >> shift == _iota((n, n), 1) >> shift).astype(jnp.bfloat16)


def _matmul(a, b, *, ta=False, tb=False, out_dtype=F32, res=None, after=(), name):
    m, k = (a.shape[1], a.shape[0]) if ta else a.shape
    n = b.shape[0] if tb else b.shape[1]
    tm, tn = _tile(m, 1408), _tile(n, 1408)
    in_bytes = max(a.dtype.itemsize, b.dtype.itemsize)
    for tk in (_tile(k, 1024), _tile(k, 512), _tile(k, 256)):
        if 2 * (tm + tn) * tk * in_bytes + 3 * tm * tn * 4 <= MATMUL_BLOCK_BYTES:
            break
    nk = k // tk
    dims = ((0 if ta else 1,), (1 if tb else 0,))

    def body(*refs):
        a_ref, b_ref = refs[:2]
        o_ref, acc = refs[-2:]
        kk = pl.program_id(2)

        @pl.when(kk == 0)
        def _():
            acc[...] = jnp.zeros_like(acc)

        acc[...] += _dot(a_ref[...].astype(MXU), b_ref[...].astype(MXU), dims)

        @pl.when(kk == nk - 1)
        def _():
            v = acc[...]
            if res is not None:
                v = v + refs[2][...]
            o_ref[...] = v.astype(out_dtype)

    a_spec = pl.BlockSpec((tk, tm), lambda i, j, kk: (kk, i)) if ta else pl.BlockSpec((tm, tk), lambda i, j, kk: (i, kk))
    b_spec = pl.BlockSpec((tn, tk), lambda i, j, kk: (j, kk)) if tb else pl.BlockSpec((tk, tn), lambda i, j, kk: (kk, j))
    o_spec = pl.BlockSpec((tm, tn), lambda i, j, kk: (i, j))
    ins, specs = [a, b], [a_spec, b_spec]
    if res is not None:
        ins.append(res)
        specs.append(o_spec)
    ins += list(after)
    specs += [pl.BlockSpec(memory_space=pl.ANY)] * len(after)
    return pl.pallas_call(
        body, name=name, grid=(m // tm, n // tn, nk), in_specs=specs, out_specs=o_spec,
        out_shape=jax.ShapeDtypeStruct((m, n), out_dtype), scratch_shapes=[pltpu.VMEM((tm, tn), F32)],
        compiler_params=_params(3))(*ins)


def _rms_fwd(x, g, name):
    t = x.shape[0]
    tb = _tile(t, 512)

    def body(x_ref, g_ref, o_ref):
        xv = x_ref[...]
        rinv = lax.rsqrt(jnp.mean(xv * xv, axis=-1, keepdims=True) + NORM_EPS)
        o_ref[...] = (xv * rinv * g_ref[...]).astype(MXU)

    return pl.pallas_call(
        body, name=name, grid=(t // tb,),
        in_specs=[pl.BlockSpec((tb, D), lambda i: (i, 0)), pl.BlockSpec((1, D), lambda i: (0, 0))],
        out_specs=pl.BlockSpec((tb, D), lambda i: (i, 0)), out_shape=jax.ShapeDtypeStruct((t, D), MXU),
        compiler_params=_params(1))(x, g)


def _rms_bwd(x, g, dh, dres, name, after=()):
    t = x.shape[0]
    tb = _tile(t, 512)

    def body(x_ref, g_ref, dh_ref, dres_ref, *rest):
        dx_ref, dg_ref = rest[-2:]

        @pl.when(pl.program_id(0) == 0)
        def _():
            dg_ref[...] = jnp.zeros_like(dg_ref)

        xv = x_ref[...]
        rinv = lax.rsqrt(jnp.mean(xv * xv, axis=-1, keepdims=True) + NORM_EPS)
        xhat = xv * rinv
        dhv = dh_ref[...]
        dg_ref[...] += jnp.sum(dhv * xhat, axis=0, keepdims=True)
        dxhat = dhv * g_ref[...]
        dx_ref[...] = dres_ref[...] + rinv * (dxhat - xhat * jnp.mean(dxhat * xhat, axis=-1, keepdims=True))

    row = pl.BlockSpec((tb, D), lambda i: (i, 0))
    vec = pl.BlockSpec((1, D), lambda i: (0, 0))
    return pl.pallas_call(
        body, name=name, grid=(t // tb,),
        in_specs=[row, vec, row, row] + [pl.BlockSpec(memory_space=pl.ANY)] * len(after), out_specs=[row, vec],
        out_shape=[jax.ShapeDtypeStruct((t, D), F32), jax.ShapeDtypeStruct((1, D), F32)],
        compiler_params=_params(1))(x, g, dh, dres, *after)


def _loss_head(x, g, target, name):
    t = x.shape[0]
    tb = _tile(t, 512)

    def body(x_ref, g_ref, t_ref, loss_ref, dx_ref, dg_ref):
        @pl.when(pl.program_id(0) == 0)
        def _():
            dg_ref[...] = jnp.zeros_like(dg_ref)
            loss_ref[...] = jnp.zeros_like(loss_ref)

        xv = x_ref[...]
        gv = g_ref[...]
        rinv = lax.rsqrt(jnp.mean(xv * xv, axis=-1, keepdims=True) + NORM_EPS)
        xhat = xv * rinv
        err = xhat * gv - t_ref[...]
        per_tok = jnp.mean(err * err, axis=-1, keepdims=True)
        loss_ref[...] += jnp.broadcast_to(0.5 * jnp.sum(per_tok, axis=0, keepdims=True), loss_ref.shape)
        dy = err * (1.0 / D)
        dg_ref[...] += jnp.sum(dy * xhat, axis=0, keepdims=True)
        dxhat = dy * gv
        dx_ref[...] = rinv * (dxhat - xhat * jnp.mean(dxhat * xhat, axis=-1, keepdims=True))

    row = pl.BlockSpec((tb, D), lambda i: (i, 0))
    vec = pl.BlockSpec((1, D), lambda i: (0, 0))
    return pl.pallas_call(
        body, name=name, grid=(t // tb,), in_specs=[row, vec, row],
        out_specs=[pl.BlockSpec((1, LANES), lambda i: (0, 0)), row, vec],
        out_shape=[jax.ShapeDtypeStruct((1, LANES), F32), jax.ShapeDtypeStruct((t, D), F32),
                   jax.ShapeDtypeStruct((1, D), F32)],
        compiler_params=_params(1))(x, g, target)


def _ffn_fwd(x, h, wg, wu, wd, name, after=()):
    t = x.shape[0]
    tb, fb = _tile(t, 1024), 256
    nf = FF // fb

    def body(x_ref, h_ref, wg_ref, wu_ref, wd_ref, *rest):
        o_ref, acc = rest[-2:]
        f = pl.program_id(1)

        @pl.when(f == 0)
        def _():
            acc[...] = jnp.zeros_like(acc)

        hv = h_ref[...]
        gate = _dot(hv, wg_ref[...])
        up = _dot(hv, wu_ref[...])
        act = (gate * _sig(gate) * up).astype(MXU)
        acc[...] += _dot(act, wd_ref[...])

        @pl.when(f == nf - 1)
        def _():
            o_ref[...] = x_ref[...] + 0.5 * acc[...]

    row = pl.BlockSpec((tb, D), lambda i, f: (i, 0))
    col = pl.BlockSpec((D, fb), lambda i, f: (0, f))
    return pl.pallas_call(
        body, name=name, grid=(t // tb, nf),
        in_specs=[row, row, col, col, pl.BlockSpec((fb, D), lambda i, f: (f, 0))]
        + [pl.BlockSpec(memory_space=pl.ANY)] * len(after), out_specs=row,
        out_shape=jax.ShapeDtypeStruct((t, D), F32), scratch_shapes=[pltpu.VMEM((tb, D), F32)],
        compiler_params=_params(2))(x, h, wg, wu, wd, *after)


def _ffn_bwd(h, dy, wg, wu, wd, name):
    t = h.shape[0]
    tb, fb = _tile(t, 1024), 256
    nf = FF // fb

    def body(h_ref, dy_ref, wg_ref, wu_ref, wd_ref, dh_ref, act_ref, dg_ref, du_ref, dout_ref, acc):
        f = pl.program_id(1)

        @pl.when(f == 0)
        def _():
            acc[...] = jnp.zeros_like(acc)

        hv = h_ref[...]
        dout = (0.5 * dy_ref[...]).astype(MXU)
        dout_ref[...] = dout
        gate = _dot(hv, wg_ref[...])
        up = _dot(hv, wu_ref[...])
        dact = _dot(dout, wd_ref[...], _NT)
        s = _sig(gate)
        silu = gate * s
        act_ref[...] = (silu * up).astype(MXU)
        dup = (dact * silu).astype(MXU)
        dgate = (dact * up * _dsilu(gate, s)).astype(MXU)
        du_ref[...] = dup
        dg_ref[...] = dgate
        acc[...] += _dot(dgate, wg_ref[...], _NT) + _dot(dup, wu_ref[...], _NT)

        @pl.when(f == nf - 1)
        def _():
            dh_ref[...] = acc[...]

    row = pl.BlockSpec((tb, D), lambda i, f: (i, 0))
    col = pl.BlockSpec((D, fb), lambda i, f: (0, f))
    hid = pl.BlockSpec((tb, fb), lambda i, f: (i, f))
    hid_shape = jax.ShapeDtypeStruct((t, FF), MXU)
    return pl.pallas_call(
        body, name=name, grid=(t // tb, nf),
        in_specs=[row, row, col, col, pl.BlockSpec((fb, D), lambda i, f: (f, 0))],
        out_specs=[row, hid, hid, hid, row],
        out_shape=[jax.ShapeDtypeStruct((t, D), F32), hid_shape, hid_shape, hid_shape,
                   jax.ShapeDtypeStruct((t, D), MXU)],
        scratch_shapes=[pltpu.VMEM((tb, D), F32)], compiler_params=_params(2))(h, dy, wg, wu, wd)


def _hgrn_chunk(qa, fa, lbl):
    c = HG_CHUNK
    lb = _sig(lbl[0:1, :] - lbl[1:2, :])
    sf = _sig(fa)
    forget = lb + (1.0 - lb) * sf
    kh = 1.0 - forget
    row, col = _iota((c, c), 0), _iota((c, c), 1)
    b = _dot((col <= row).astype(F32), jnp.log(forget), precision=_HI)
    bref, blast = b[c // 2:c // 2 + 1, :], b[c - 1:c, :]
    sq = _sig(qa)
    q = qa * sq
    qt, kt = q * jnp.exp(b - bref), kh * jnp.exp(bref - b)
    qb, kl = q * jnp.exp(b), kh * jnp.exp(blast - b)
    causal = col <= row
    return dict(lb=lb, sf=sf, forget=forget, sq=sq, qt=qt, kt=kt, qb=qb, kl=kl, decay=jnp.exp(blast),
                causal=causal, e_q=jnp.exp(b), e_qt=jnp.exp(b - bref), e_kt=jnp.exp(bref - b),
                e_kl=jnp.exp(blast - b))


def _hgrn_specs(t):
    c = HG_CHUNK
    return c, t // c, WA // LANES


def _hgrn_fwd(p_all, lbl, onorm, name):
    t = p_all.shape[0]
    c, n, nh = _hgrn_specs(t)

    def body(q_ref, f_ref, i_ref, g_ref, lbl_ref, on_ref, oa_ref, oraw_ref, st_ref, state):
        @pl.when(pl.program_id(0) == 0)
        def _():
            state[...] = jnp.zeros_like(state)

        for h in range(nh):
            at = slice(h * LANES, (h + 1) * LANES)
            k = _hgrn_chunk(q_ref[:, at], f_ref[:, at], lbl_ref[:, at])
            v = i_ref[:, at]
            st = state[h]
            st_ref[h, 0] = st
            a = jnp.where(k["causal"], _dot(k["qt"], k["kt"], _NT, _HI), 0.0)
            o = _dot(a, v, precision=_HI) + _dot(k["qb"], st, _NT, _HI)
            state[h] = st * k["decay"] + _dot(v, k["kl"], _TN, _HI)
            oraw_ref[:, at] = o
            rinv = lax.rsqrt(jnp.mean(o * o, axis=-1, keepdims=True) + NORM_EPS)
            ga = g_ref[:, at]
            oa_ref[:, at] = (o * rinv * on_ref[:, at] * (ga * _sig(ga))).astype(MXU)

    def blk(j):
        return pl.BlockSpec((c, WA), lambda i: (i, j))

    return pl.pallas_call(
        body, name=name, grid=(n,),
        in_specs=[blk(0), blk(1), blk(2), blk(3), pl.BlockSpec((2, WA), lambda i: (0, 0)),
                  pl.BlockSpec((1, WA), lambda i: (0, 0))],
        out_specs=[blk(0), blk(0), pl.BlockSpec((nh, 1, LANES, LANES), lambda i: (0, i, 0, 0))],
        out_shape=[jax.ShapeDtypeStruct((t, WA), MXU), jax.ShapeDtypeStruct((t, WA), F32),
                   jax.ShapeDtypeStruct((nh, n, LANES, LANES), F32)],
        scratch_shapes=[pltpu.VMEM((nh, LANES, LANES), F32)], compiler_params=_params(1))(
            p_all, p_all, p_all, p_all, lbl, onorm)


def _hgrn_bwd(p_all, lbl, onorm, oraw, states, doa, name):
    t = p_all.shape[0]
    c, n, nh = _hgrn_specs(t)

    def body(q_ref, f_ref, i_ref, g_ref, lbl_ref, on_ref, oraw_ref, st_ref, doa_ref,
             dq_ref, df_ref, di_ref, dg_ref, don_ref, dlbl_ref, dstate, dlb):
        @pl.when(pl.program_id(0) == 0)
        def _():
            dstate[...] = jnp.zeros_like(dstate)
            dlb[...] = jnp.zeros_like(dlb)
            don_ref[...] = jnp.zeros_like(don_ref)

        for h in range(nh):
            at = slice(h * LANES, (h + 1) * LANES)
            qa, fa, v, ga = q_ref[:, at], f_ref[:, at], i_ref[:, at], g_ref[:, at]
            k = _hgrn_chunk(qa, fa, lbl_ref[:, at])
            st, dst_next = st_ref[h, 0], dstate[h]
            o = oraw_ref[:, at]
            gain = on_ref[:, at]
            rinv = lax.rsqrt(jnp.mean(o * o, axis=-1, keepdims=True) + NORM_EPS)
            on = o * rinv
            sg = _sig(ga)
            gate = ga * sg
            dout = doa_ref[:, at]
            don_ref[:, at] += jnp.sum(dout * on * gate, axis=0, keepdims=True)
            dg_ref[:, at] = (dout * on * gain * _dsilu(ga, sg)).astype(MXU)
            d_on = dout * gain * gate
            do = rinv * (d_on - on * jnp.mean(d_on * on, axis=-1, keepdims=True))

            a = jnp.where(k["causal"], _dot(k["qt"], k["kt"], _NT, _HI), 0.0)
            dqb = _dot(do, st, precision=_HI)
            dstate[h] = dst_next * k["decay"] + _dot(do, k["qb"], _TN, _HI)
            da = jnp.where(k["causal"], _dot(do, v, _NT, _HI), 0.0)
            dqt = _dot(da, k["kt"], precision=_HI)
            dkt = _dot(da, k["qt"], _TN, _HI)
            dv = _dot(a, do, _TN, _HI) + _dot(k["kl"], dst_next, _NT, _HI)
            dkl = _dot(v, dst_next, precision=_HI)
            ddecay = jnp.sum(dst_next * st, axis=0, keepdims=True)
            dq = dqb * k["e_q"] + dqt * k["e_qt"]
            dk = dkt * k["e_kt"] + dkl * k["e_kl"]
            tq, tk, tl = dqt * k["qt"], dkt * k["kt"], dkl * k["kl"]
            db = dqb * k["qb"] + tq - tk - tl
            dbref = jnp.sum(tk - tq, axis=0, keepdims=True)
            dblast = jnp.sum(tl, axis=0, keepdims=True) + ddecay * k["decay"]
            rows = _iota((c, LANES), 0)
            db = db + jnp.where(rows == c // 2, dbref, 0.0) + jnp.where(rows == c - 1, dblast, 0.0)
            row, col = _iota((c, c), 0), _iota((c, c), 1)
            dlogf = _dot((col >= row).astype(F32), db, precision=_HI)
            dq_ref[:, at] = (dq * _dsilu(qa, k["sq"])).astype(MXU)
            di_ref[:, at] = dv.astype(MXU)
            dforget = dlogf / k["forget"] - dk
            sf, lb = k["sf"], k["lb"]
            df_ref[:, at] = (dforget * (1.0 - lb) * sf * (1.0 - sf)).astype(MXU)
            dlb[:, at] += jnp.sum(dforget * (1.0 - sf), axis=0, keepdims=True)
            dl0 = dlb[:, at] * lb * (1.0 - lb)
            dlbl_ref[:, at] = jnp.where(_iota((2, LANES), 0) == 0, dl0, -dl0)

    def blk(j):
        return pl.BlockSpec((c, WA), lambda i: (n - 1 - i, j))

    vec = pl.BlockSpec((1, WA), lambda i: (0, 0))
    lg = pl.BlockSpec((2, WA), lambda i: (0, 0))
    grad = jax.ShapeDtypeStruct((t, WA), MXU)
    return pl.pallas_call(
        body, name=name, grid=(n,),
        in_specs=[blk(0), blk(1), blk(2), blk(3), lg, vec, blk(0),
                  pl.BlockSpec((nh, 1, LANES, LANES), lambda i: (0, n - 1 - i, 0, 0)), blk(0)],
        out_specs=[blk(0), blk(0), blk(0), blk(0), vec, lg],
        out_shape=[grad, grad, grad, grad, jax.ShapeDtypeStruct((1, WA), F32), jax.ShapeDtypeStruct((2, WA), F32)],
        scratch_shapes=[pltpu.VMEM((nh, LANES, LANES), F32), pltpu.VMEM((1, WA), F32)],
        compiler_params=_params(1))(p_all, p_all, p_all, p_all, lbl, onorm, oraw, states, doa)


def _lora_act(x):
    lane = _iota(x.shape, 1)
    n_w, n_a, n_g = LORA
    return jnp.where(lane < n_w, jnp.tanh(x),
                     jnp.where(lane < n_w + n_a, x, jnp.where(lane < n_w + n_a + n_g, _sig(x), 0.0)))


def _lora_dact(x):
    lane = _iota(x.shape, 1)
    n_w, n_a, n_g = LORA
    th, s = jnp.tanh(x), _sig(x)
    return jnp.where(lane < n_w, 1.0 - th * th,
                     jnp.where(lane < n_w + n_a, 1.0, jnp.where(lane < n_w + n_a + n_g, s * (1.0 - s), 0.0)))


def _shift_down(cur, prev8, first):
    rolled = pltpu.roll(cur, 1, 0)
    edge = prev8[7:8, :] * jnp.where(first, 0.0, 1.0)
    return jnp.where(_iota(cur.shape, 0) == 0, edge, rolled)


def _shift_up(cur, next8, last):
    rows = cur.shape[0]
    rolled = pltpu.roll(cur, rows - 1, 0)
    edge = next8[0:1, :] * jnp.where(last, 0.0, 1.0)
    return jnp.where(_iota(cur.shape, 0) == rows - 1, edge, rolled)


def _rwkv_inputs(refs, first, ones):
    (pr, pk, pv, plo, qr, qk, qv, qlo, mr, mk, mv, mlo, w2c, w0, a0, kk_w, ka_w) = refs
    mix = lambda cur, prev, mu: cur[...] + mu[...] * (_shift_down(cur[...], prev[...], first) - cur[...])
    r, k, v, lo = mix(pr, qr, mr), mix(pk, qk, mk), mix(pv, qv, mv), mix(plo, qlo, mlo)
    z = _lora_act(lo)
    lin = _dot(z.astype(MXU), w2c[...])
    sg = _sig(w0[...] + lin[:, :WB])
    decay = jnp.exp(-DECAY_C * sg)
    a = _sig(a0[...] + lin[:, WB:2 * WB])
    g = lin[:, 2 * WB:]
    kk0 = k * kk_w[...]
    nrm = jnp.sqrt(_split_dot(kk0 * kk0, ones, 3))
    den = jnp.maximum(nrm, L2_EPS)
    kk = kk0 / den
    k2 = k * (1.0 + (a - 1.0) * ka_w[...])
    return dict(r=r, k=k, v=v, lo=lo, z=z, sg=sg, decay=decay, a=a, g=g, kk=kk, den=den, nrm=nrm, k2=k2)


def _rwkv_in_specs(t, tb):
    nt8 = tb // 8

    def cur(w, j):
        return pl.BlockSpec((tb, w), lambda i: (i, j))

    def prev(w, j):
        return pl.BlockSpec((8, w), lambda i: (jnp.maximum(i * nt8 - 1, 0), j))

    def vec(w, j=0):
        return pl.BlockSpec((1, w), lambda i: (0, j))

    return [cur(WB, COL_R), cur(WB, COL_K), cur(WB, COL_V), cur(256, COL_L),
            prev(WB, COL_R), prev(WB, COL_K), prev(WB, COL_V), prev(256, COL_L),
            vec(WB, 0), vec(WB, 1), vec(WB, 2), vec(256, 6),
            pl.BlockSpec((256, 3 * WB), lambda i: (0, 0)), vec(WB), vec(WB), vec(WB), vec(WB)]


def _rwkv_in_args(p_all, mu_pad, w2cat, w0, a0, k_k, k_a):
    return (p_all,) * 8 + (mu_pad,) * 4 + (w2cat, w0, a0, k_k, k_a)


def _rwkv_prep(p_all, mu_pad, w2cat, w0, a0, k_k, k_a, name):
    t = p_all.shape[0]
    tb = _tile(t, 256)

    def body(*refs):
        ins, outs = refs[:17], refs[17:]
        q = _rwkv_inputs(ins, pl.program_id(0) == 0, _head_ones(WB, HD_B))
        for ref, val in zip(outs, (q["r"], q["decay"], q["k2"], q["v"], -q["kk"], q["kk"] * q["a"], q["g"])):
            ref[...] = val

    out = pl.BlockSpec((tb, WB), lambda i: (i, 0))
    return pl.pallas_call(
        body, name=name, grid=(t // tb,), in_specs=_rwkv_in_specs(t, tb), out_specs=[out] * 7,
        out_shape=[jax.ShapeDtypeStruct((t, WB), F32)] * 7, compiler_params=_params(1))(
            *_rwkv_in_args(p_all, mu_pad, w2cat, w0, a0, k_k, k_a))


def _pair_rows(x8, i):
    return jnp.concatenate([jnp.broadcast_to(x8[i:i + 1, p * LANES:(p + 1) * LANES], (HD_B, LANES))
                            for p in range(4)], axis=0)


def _pair_sums(x):
    return jnp.concatenate([jnp.sum(x[p * HD_B:(p + 1) * HD_B], axis=0, keepdims=True) for p in range(4)], axis=1)


def _put_row(buf, i, row):
    return jnp.where(_iota(buf.shape, 0) == i, row, buf)


def _pieces(x):
    hi = x.astype(jnp.bfloat16).astype(F32)
    lo = (x - hi).astype(jnp.bfloat16).astype(F32)
    upper = (_iota((x.shape[0], LANES), 1) & (HD_B // 2)) != 0
    swapped = [jnp.where(upper, pltpu.roll(lo[:, p * LANES:(p + 1) * LANES], HD_B // 2, 1),
                         pltpu.roll(lo[:, p * LANES:(p + 1) * LANES], LANES - HD_B // 2, 1)) for p in range(4)]
    return hi, jnp.concatenate(swapped, axis=1)


def _scan_consts():
    row, lane = _iota((HD_B, LANES), 0), _iota((HD_B, LANES), 1) & (HD_B - 1)
    either = ((row ^ lane) & (HD_B // 2 - 1)) == 0
    return ((row ^ lane) & (HD_B // 2)) != 0, either.astype(jnp.bfloat16), _head_ones(LANES, HD_B)


def _pair_cols(many, consts):
    swapped, either, ones = consts
    tiles = []
    for (hi8, lo8), i in many:
        for p in range(4):
            lanes = slice(p * LANES, (p + 1) * LANES)
            hi = jnp.broadcast_to(hi8[i:i + 1, lanes], (16, LANES)).astype(jnp.bfloat16)
            lo = jnp.broadcast_to(lo8[i:i + 1, lanes], (16, LANES)).astype(jnp.bfloat16)
            for g in range(HD_B // 16):
                rows = slice(g * 16, (g + 1) * 16)
                tiles.append(jnp.where(swapped[rows], lo, hi) * either[rows])
    out = _dot(jnp.concatenate(tiles, axis=0), ones)
    return [out[m * 4 * HD_B:(m + 1) * 4 * HD_B] for m in range(len(many))]


def _blocked_loop(n_blocks, prepare, advance, init):
    unroll = SCAN_UNROLL if n_blocks % SCAN_UNROLL == 0 else 1

    def trip(g, carry):
        prepared = [prepare(g * unroll + i) for i in range(unroll)]
        for p in prepared:
            carry = advance(p, carry)
        return carry

    return lax.fori_loop(0, n_blocks // unroll, trip, init)


def _rwkv_scan_fwd(r, w, k, v, a, b, name):
    t = r.shape[0]
    cc = min(t, SCAN_CHUNK)

    def body(r_ref, w_ref, k_ref, v_ref, a_ref, b_ref, y_ref, sa_ref, state):
        @pl.when(pl.program_id(0) == 0)
        def _():
            state[...] = jnp.zeros_like(state)

        consts = _scan_consts()

        def prepare(j):
            rows = pl.ds(pl.multiple_of(j * 8, 8), 8)
            r8, w8, k8, v8, a8, b8 = (ref[rows, :] for ref in (r_ref, w_ref, k_ref, v_ref, a_ref, b_ref))
            rp, wp, kp, ap, bp = (_pieces(x) for x in (r8, w8, k8, a8, b8))
            return rows, v8, _pair_cols([(x, i) for i in range(8) for x in (ap, wp, bp, kp, rp)], consts)

        def advance(prepared, sk):
            rows, v8, cols = prepared
            y8 = jnp.zeros((8, WB), F32)
            sa8 = jnp.zeros((8, WB), F32)
            for i in range(8):
                a_c, w_c, b_c, k_c, r_c = cols[5 * i:5 * i + 5]
                sa = _pair_sums(sk * a_c)
                sk = sk * w_c + b_c * _pair_rows(sa, 0) + k_c * _pair_rows(v8, i)
                y8 = _put_row(y8, i, _pair_sums(sk * r_c))
                sa8 = _put_row(sa8, i, sa)
            y_ref[rows, :] = y8
            sa_ref[rows, :] = sa8
            return sk

        state[...] = _blocked_loop(cc // 8, prepare, advance, state[...])

    row = pl.BlockSpec((cc, WB), lambda i: (i, 0))
    return pl.pallas_call(
        body, name=name, grid=(t // cc,), in_specs=[row] * 6, out_specs=[row, row],
        out_shape=[jax.ShapeDtypeStruct((t, WB), F32)] * 2,
        scratch_shapes=[pltpu.VMEM((4 * HD_B, LANES), F32)], compiler_params=_params(1))(r, w, k, v, a, b)


def _rwkv_states(sa, w, k, v, b, name):
    t = sa.shape[0]
    cc = min(t, SCAN_CHUNK)

    def body(sa_ref, w_ref, k_ref, v_ref, b_ref, sall_ref, state):
        @pl.when(pl.program_id(0) == 0)
        def _():
            state[...] = jnp.zeros_like(state)

        consts = _scan_consts()

        def prepare(j):
            base = pl.multiple_of(j * 8, 8)
            sa8, w8, k8, v8, b8 = (ref[pl.ds(base, 8), :] for ref in (sa_ref, w_ref, k_ref, v_ref, b_ref))
            sap, vp = _pieces(sa8), _pieces(v8)
            return base, w8, k8, b8, _pair_cols([(x, i) for i in range(8) for x in (sap, vp)], consts)

        def advance(prepared, sv):
            base, w8, k8, b8, cols = prepared
            for i in range(8):
                sv = sv * _pair_rows(w8, i) + cols[2 * i] * _pair_rows(b8, i) + cols[2 * i + 1] * _pair_rows(k8, i)
                sall_ref[base + i] = sv
            return sv

        state[...] = _blocked_loop(cc // 8, prepare, advance, state[...])

    row = pl.BlockSpec((cc, WB), lambda i: (i, 0))
    return pl.pallas_call(
        body, name=name, grid=(t // cc,), in_specs=[row] * 5,
        out_specs=pl.BlockSpec((cc, 4 * HD_B, LANES), lambda i: (i, 0, 0)),
        out_shape=jax.ShapeDtypeStruct((t, 4 * HD_B, LANES), F32),
        scratch_shapes=[pltpu.VMEM((4 * HD_B, LANES), F32)], compiler_params=_params(1))(sa, w, k, v, b)


def _rwkv_scan_bwd(dy, r, w, k, a, b, name):
    t = r.shape[0]
    cc = min(t, SCAN_CHUNK)
    n = t // cc

    def body(dy_ref, r_ref, w_ref, k_ref, a_ref, b_ref, dsa_ref, dv_ref, dstate):
        @pl.when(pl.program_id(0) == 0)
        def _():
            dstate[...] = jnp.zeros_like(dstate)

        consts = _scan_consts()

        steps = range(7, -1, -1)

        def prepare(jj):
            rows = pl.ds(pl.multiple_of((cc // 8 - 1 - jj) * 8, 8), 8)
            dy8, r8, w8, k8, a8, b8 = (ref[rows, :] for ref in (dy_ref, r_ref, w_ref, k_ref, a_ref, b_ref))
            rp, wp, kp, ap, bp = (_pieces(x) for x in (r8, w8, k8, a8, b8))
            return rows, dy8, _pair_cols([(x, i) for i in steps for x in (rp, bp, kp, wp, ap)], consts)

        def advance(prepared, ds):
            rows, dy8, cols = prepared
            dsa8, dv8 = jnp.zeros((8, WB), F32), jnp.zeros((8, WB), F32)
            for n_done, i in enumerate(steps):
                r_c, b_c, k_c, w_c, a_c = cols[5 * n_done:5 * n_done + 5]
                ds = ds + r_c * _pair_rows(dy8, i)
                dsa = _pair_sums(ds * b_c)
                dv8 = _put_row(dv8, i, _pair_sums(ds * k_c))
                dsa8 = _put_row(dsa8, i, dsa)
                ds = ds * w_c + a_c * _pair_rows(dsa, 0)
            dsa_ref[rows, :] = dsa8
            dv_ref[rows, :] = dv8
            return ds

        dstate[...] = _blocked_loop(cc // 8, prepare, advance, dstate[...])

    row = pl.BlockSpec((cc, WB), lambda i: (n - 1 - i, 0))
    return pl.pallas_call(
        body, name=name, grid=(n,), in_specs=[row] * 6, out_specs=[row] * 2,
        out_shape=[jax.ShapeDtypeStruct((t, WB), F32)] * 2,
        scratch_shapes=[pltpu.VMEM((4 * HD_B, LANES), F32)], compiler_params=_params(1))(dy, r, w, k, a, b)


def _rwkv_scan_bwd_values(dy, r, w, v, a, sa, dsa, sall, name):
    t = r.shape[0]
    cc = min(t, SCAN_CHUNK)
    n = t // cc

    def body(dy_ref, r_ref, w_ref, v_ref, a_ref, sa_ref, dsa_ref, sall_ref, sprev_ref,
             dr_ref, dw_ref, dk_ref, da_ref, db_ref, dstate):
        @pl.when(pl.program_id(0) == 0)
        def _():
            dstate[...] = jnp.zeros_like(dstate)

        consts = _scan_consts()
        before_chunk = jnp.where(pl.program_id(0) == n - 1, 0.0, 1.0) * sprev_ref[0]

        steps = range(7, -1, -1)

        def prepare(jj):
            j = cc // 8 - 1 - jj
            base = pl.multiple_of(j * 8, 8)
            dy8, r8, w8, v8, a8, sa8, dsa8 = (ref[pl.ds(base, 8), :] for ref in
                                              (dy_ref, r_ref, w_ref, v_ref, a_ref, sa_ref, dsa_ref))
            dyp, vp, sap, dsap = (_pieces(x) for x in (dy8, v8, sa8, dsa8))
            return j, base, r8, w8, a8, _pair_cols([(x, i) for i in steps for x in (dyp, vp, sap, dsap)], consts)

        def advance(prepared, carry):
            ds, sc = carry
            j, base, r8, w8, a8, cols = prepared
            rows = pl.ds(base, 8)
            outs = [jnp.zeros((8, WB), F32) for _ in range(5)]
            for n_done, i in enumerate(steps):
                if i > 0:
                    sp = sall_ref[base + i - 1]
                else:
                    sp = jnp.where(j == 0, before_chunk, sall_ref[jnp.maximum(base - 1, 0)])
                dy_c, v_c, sa_c, dsa_c = cols[4 * n_done:4 * n_done + 4]
                ds = ds + dy_c * _pair_rows(r8, i)
                vals = (_pair_sums(sc * dy_c), _pair_sums(ds * sp), _pair_sums(ds * v_c),
                        _pair_sums(sp * dsa_c), _pair_sums(ds * sa_c))
                outs = [_put_row(o, i, val) for o, val in zip(outs, vals)]
                ds = ds * _pair_rows(w8, i) + dsa_c * _pair_rows(a8, i)
                sc = sp
            for ref, o in zip((dr_ref, dw_ref, dk_ref, da_ref, db_ref), outs):
                ref[rows, :] = o
            return ds, sc

        ds, _ = _blocked_loop(cc // 8, prepare, advance, (dstate[...], sall_ref[cc - 1]))
        dstate[...] = ds

    row = pl.BlockSpec((cc, WB), lambda i: (n - 1 - i, 0))
    return pl.pallas_call(
        body, name=name, grid=(n,),
        in_specs=[row] * 7 + [pl.BlockSpec((cc, 4 * HD_B, LANES), lambda i: (n - 1 - i, 0, 0)),
                              pl.BlockSpec((1, 4 * HD_B, LANES), lambda i: (jnp.maximum((n - 1 - i) * cc - 1, 0), 0, 0))],
        out_specs=[row] * 5, out_shape=[jax.ShapeDtypeStruct((t, WB), F32)] * 5,
        scratch_shapes=[pltpu.VMEM((4 * HD_B, LANES), F32)], compiler_params=_params(1))(
            dy, r, w, v, a, sa, dsa, sall, sall)


def _rwkv_post(y, r, k2, v, g, r_k, gn_w, gn_b, name):
    t = y.shape[0]
    tb = _tile(t, 256)

    def body(y_ref, r_ref, k_ref, v_ref, g_ref, rk_ref, gw_ref, gb_ref, o_ref):
        ones = _head_ones(WB, HD_B)
        yv = y_ref[...]
        yc = yv - _split_dot(yv, ones, 3) * (1.0 / HD_B)
        rstd = lax.rsqrt(_split_dot(yc * yc, ones, 3) * (1.0 / HD_B) + GN_EPS)
        rk = _split_dot(r_ref[...] * k_ref[...] * rk_ref[...], ones, 3)
        o_ref[...] = ((yc * rstd * gw_ref[...] + gb_ref[...] + rk * v_ref[...]) * g_ref[...]).astype(MXU)

    row = pl.BlockSpec((tb, WB), lambda i: (i, 0))
    vec = pl.BlockSpec((1, WB), lambda i: (0, 0))
    return pl.pallas_call(
        body, name=name, grid=(t // tb,), in_specs=[row] * 5 + [vec] * 3, out_specs=row,
        out_shape=jax.ShapeDtypeStruct((t, WB), MXU), compiler_params=_params(1))(y, r, k2, v, g, r_k, gn_w, gn_b)


def _rwkv_post_bwd(dob, y, r, k2, v, g, r_k, gn_w, gn_b, name):
    t = y.shape[0]
    tb = _tile(t, 256)

    def body(do_ref, y_ref, r_ref, k_ref, v_ref, g_ref, rk_ref, gw_ref, gb_ref,
             dy_ref, dg_ref, dr_ref, dk_ref, dv_ref, dgw_ref, dgb_ref, drk_ref):
        @pl.when(pl.program_id(0) == 0)
        def _():
            dgw_ref[...] = jnp.zeros_like(dgw_ref)
            dgb_ref[...] = jnp.zeros_like(dgb_ref)
            drk_ref[...] = jnp.zeros_like(drk_ref)

        ones = _head_ones(WB, HD_B)
        seg = lambda x: _split_dot(x, ones, 3)
        yv, rv, kv, vv, gv = y_ref[...], r_ref[...], k_ref[...], v_ref[...], g_ref[...]
        yc = yv - seg(yv) * (1.0 / HD_B)
        rstd = lax.rsqrt(seg(yc * yc) * (1.0 / HD_B) + GN_EPS)
        yn = yc * rstd
        rk = seg(rv * kv * rk_ref[...])
        dob_v = do_ref[...]
        dg_ref[...] = dob_v * (yn * gw_ref[...] + gb_ref[...] + rk * vv)
        dyg = dob_v * gv
        dgw_ref[...] += jnp.sum(dyg * yn, axis=0, keepdims=True)
        dgb_ref[...] += jnp.sum(dyg, axis=0, keepdims=True)
        dyn = dyg * gw_ref[...]
        dy_ref[...] = rstd * (dyn - (seg(dyn) + yn * seg(dyn * yn)) * (1.0 / HD_B))
        drk = seg(dyg * vv)
        dv_ref[...] = dyg * rk
        dr_ref[...] = drk * kv * rk_ref[...]
        dk_ref[...] = drk * rv * rk_ref[...]
        drk_ref[...] += jnp.sum(drk * rv * kv, axis=0, keepdims=True)

    row = pl.BlockSpec((tb, WB), lambda i: (i, 0))
    vec = pl.BlockSpec((1, WB), lambda i: (0, 0))
    full, small = jax.ShapeDtypeStruct((t, WB), F32), jax.ShapeDtypeStruct((1, WB), F32)
    return pl.pallas_call(
        body, name=name, grid=(t // tb,), in_specs=[row] * 6 + [vec] * 3, out_specs=[row] * 5 + [vec] * 3,
        out_shape=[full] * 5 + [small] * 3, compiler_params=_params(1))(dob, y, r, k2, v, g, r_k, gn_w, gn_b)


def _rwkv_prep_bwd(grads, p_all, mu_pad, w2cat, w0, a0, k_k, k_a, name):
    t = p_all.shape[0]
    tb = _tile(t, 256)

    def body(*refs):
        g_refs, ins, outs = refs[:10], refs[10:27], refs[27:]
        dr_s, dw, dk2_s, dv_s, das, dbs, dg, dr_b, dk2_b, dv_b = (ref[...] for ref in g_refs)
        dr_ref, dk_ref, dv_ref, dlo_ref, dw2_ref, dw0_ref, da0_ref, dkk_ref, dka_ref = outs

        @pl.when(pl.program_id(0) == 0)
        def _():
            for ref in (dw2_ref, dw0_ref, da0_ref, dkk_ref, dka_ref):
                ref[...] = jnp.zeros_like(ref)

        ones = _head_ones(WB, HD_B)
        q = _rwkv_inputs(ins, pl.program_id(0) == 0, ones)
        kk_w, ka_w = ins[15][...], ins[16][...]
        a, kk, k = q["a"], q["kk"], q["k"]
        dk2 = dk2_s + dk2_b
        dkk = dbs * a - das
        da = dbs * kk + dk2 * k * ka_w
        dk = dk2 * (1.0 + (a - 1.0) * ka_w)
        dka_ref[...] += jnp.sum(dk2 * k * (a - 1.0), axis=0, keepdims=True)
        proj = jnp.where(q["nrm"] > L2_EPS, _split_dot(dkk * kk, ones, 3), 0.0)
        dkk0 = (dkk - kk * proj) / q["den"]
        dk = dk + dkk0 * kk_w
        dkk_ref[...] += jnp.sum(dkk0 * k, axis=0, keepdims=True)
        dal = da * a * (1.0 - a)
        da0_ref[...] += jnp.sum(dal, axis=0, keepdims=True)
        sg = q["sg"]
        dwl = dw * q["decay"] * (-DECAY_C) * sg * (1.0 - sg)
        dw0_ref[...] += jnp.sum(dwl, axis=0, keepdims=True)
        dlin = jnp.concatenate([dwl, dal, dg], axis=1).astype(MXU)
        dw2_ref[...] += _dot(q["z"].astype(MXU), dlin, _TN)
        dz = _dot(dlin, ins[12][...], _NT)
        dlo_ref[...] = dz * _lora_dact(q["lo"])
        dr_ref[...] = dr_s + dr_b
        dk_ref[...] = dk
        dv_ref[...] = dv_s + dv_b

    row = pl.BlockSpec((tb, WB), lambda i: (i, 0))
    vec = pl.BlockSpec((1, WB), lambda i: (0, 0))
    full, small = jax.ShapeDtypeStruct((t, WB), F32), jax.ShapeDtypeStruct((1, WB), F32)
    return pl.pallas_call(
        body, name=name, grid=(t // tb,), in_specs=[row] * 10 + _rwkv_in_specs(t, tb),
        out_specs=[row] * 3 + [pl.BlockSpec((tb, 256), lambda i: (i, 0)),
                               pl.BlockSpec((256, 3 * WB), lambda i: (0, 0))] + [vec] * 4,
        out_shape=[full] * 3 + [jax.ShapeDtypeStruct((t, 256), F32), jax.ShapeDtypeStruct((256, 3 * WB), F32)]
        + [small] * 4, compiler_params=_params(1))(*grads, *_rwkv_in_args(p_all, mu_pad, w2cat, w0, a0, k_k, k_a))


def _shift_bwd(dshifted, p_all, mu_pad, name):
    t = p_all.shape[0]
    tb = _tile(t, 256)
    nt, nt8 = t // tb, tb // 8
    widths, cols, mus = (WB, WB, WB, 256), (COL_R, COL_K, COL_V, COL_L), (0, 1, 2, 6)

    def body(*refs):
        d_refs, n_refs, p_refs, q_refs, m_refs = refs[0:4], refs[4:8], refs[8:12], refs[12:16], refs[16:20]
        o_refs, dmu_refs = refs[20:24], refs[24:28]
        i = pl.program_id(0)

        @pl.when(i == 0)
        def _():
            for ref in dmu_refs:
                ref[...] = jnp.zeros_like(ref)

        for d, nx, p, q, m, o, dmu in zip(d_refs, n_refs, p_refs, q_refs, m_refs, o_refs, dmu_refs):
            dv, pv, mu = d[...], p[...], m[...]
            o[...] = (dv * (1.0 - mu) + mu * _shift_up(dv, nx[...], i == nt - 1)).astype(MXU)
            dmu[...] += jnp.sum(dv * (_shift_down(pv, q[...], i == 0) - pv), axis=0, keepdims=True)

    cur_d = [pl.BlockSpec((tb, w), lambda i: (i, 0)) for w in widths]
    next_d = [pl.BlockSpec((8, w), lambda i: (jnp.minimum((i + 1) * nt8, t // 8 - 1), 0)) for w in widths]
    cur_p = [pl.BlockSpec((tb, w), lambda i, j=j: (i, j)) for w, j in zip(widths, cols)]
    prev_p = [pl.BlockSpec((8, w), lambda i, j=j: (jnp.maximum(i * nt8 - 1, 0), j)) for w, j in zip(widths, cols)]
    mu_s = [pl.BlockSpec((1, w), lambda i, j=j: (0, j)) for w, j in zip(widths, mus)]
    vecs = [pl.BlockSpec((1, w), lambda i: (0, 0)) for w in widths]
    return pl.pallas_call(
        body, name=name, grid=(nt,), in_specs=cur_d + next_d + cur_p + prev_p + mu_s, out_specs=cur_d + vecs,
        out_shape=[jax.ShapeDtypeStruct((t, w), MXU) for w in widths]
        + [jax.ShapeDtypeStruct((1, w), F32) for w in widths],
        compiler_params=_params(1))(*dshifted, *dshifted, *(p_all,) * 8, *(mu_pad,) * 4)


def _peer(k):
    x, y, c = (lax.axis_index(n) for n in AXES)
    px = 1 - x if k & 4 else x
    py = 1 - y if k & 2 else y
    pc = 1 - c if k & 1 else c
    return (px, py, pc), 4 * px + 2 * py + pc


def _exchange_copy(src_refs, land_refs, send_sems, recv_sems, per_peer, j, k, arriving):
    _, me = _peer(0)
    peer, idx = _peer(k)
    sem = j * (N_DEV - 1) + k - 1
    return pltpu.make_async_remote_copy(
        src_ref=src_refs[j].at[idx] if per_peer[j] else src_refs[j],
        dst_ref=land_refs[j].at[idx if arriving else me],
        send_sem=send_sems.at[sem], recv_sem=recv_sems.at[sem],
        device_id=peer, device_id_type=pl.DeviceIdType.MESH)


def _exchange_start(srcs, per_peer, name, after=()):
    n = len(srcs)
    shapes = [tuple(s.shape[1:]) if pp else tuple(s.shape) for s, pp in zip(srcs, per_peer)]
    pairs = [(j, k) for k in range(1, N_DEV) for j in range(n)]
    first_out = 2 * n + len(after)

    def body(*refs):
        src_refs, land_refs, (send_sems, recv_sems), token = refs[:n], refs[n:2 * n], refs[first_out:first_out + 2], refs[-1]
        for j, k in pairs:
            _exchange_copy(src_refs, land_refs, send_sems, recv_sems, per_peer, j, k, False).start()
        token[...] = jnp.zeros_like(token)

    hbm, sem = pl.BlockSpec(memory_space=pltpu.HBM), pl.BlockSpec(memory_space=pltpu.SEMAPHORE)
    lands = [lax.empty((N_DEV,) + shp, s.dtype) for shp, s in zip(shapes, srcs)]
    operands = [pltpu.with_memory_space_constraint(a, pltpu.HBM) for a in list(srcs) + lands]
    n_sems = n * (N_DEV - 1)
    out = pl.pallas_call(
        body, name=name, in_specs=[hbm] * (2 * n) + [pl.BlockSpec(memory_space=pl.ANY)] * len(after),
        out_specs=[sem, sem] + [hbm] * (2 * n) + [pl.BlockSpec(memory_space=pltpu.VMEM)],
        out_shape=[pltpu.SemaphoreType.DMA((n_sems,)), pltpu.SemaphoreType.DMA((n_sems,))]
        + [pltpu.HBM(a.shape, a.dtype) for a in operands] + [jax.ShapeDtypeStruct((8, LANES), F32)],
        input_output_aliases={j: 2 + j for j in range(2 * n)},
        compiler_params=pltpu.CompilerParams(has_side_effects=pltpu.SideEffectType.DATAFLOW_SIDE_EFFECTING))(
            *operands, *after)
    return (out[0], out[1], out[2:2 + n], out[2 + n:2 + 2 * n], per_peer), out[-1]


def _exchange_wait(handle, after, name):
    send_sems, recv_sems, srcs, lands, per_peer = handle
    n = len(srcs)
    pairs = [(j, k) for k in range(1, N_DEV) for j in range(n)]

    def body(*refs):
        src_refs, land_refs, (send_sems, recv_sems) = refs[:n], refs[n:2 * n], refs[2 * n:2 * n + 2]
        for j, k in pairs:
            _exchange_copy(src_refs, land_refs, send_sems, recv_sems, per_peer, j, k, False).wait_send()
            _exchange_copy(src_refs, land_refs, send_sems, recv_sems, per_peer, j, k, True).wait_recv()

    hbm, sem = pl.BlockSpec(memory_space=pltpu.HBM), pl.BlockSpec(memory_space=pltpu.SEMAPHORE)
    out = pl.pallas_call(
        body, name=name, in_specs=[hbm] * (2 * n) + [sem, sem, pl.BlockSpec(memory_space=pl.ANY)],
        out_specs=[hbm] * (2 * n), out_shape=[pltpu.HBM(a.shape, a.dtype) for a in list(srcs) + list(lands)],
        input_output_aliases={j: j for j in range(2 * n)},
        compiler_params=pltpu.CompilerParams(has_side_effects=pltpu.SideEffectType.DATAFLOW_SIDE_EFFECTING))(
            *srcs, *lands, send_sems, recv_sems, after)
    return out[n:]


def _adam_update(g, w, m, v):
    c1, c2 = 1.0 - ADAM_B1 ** ADAM_STEP, 1.0 - ADAM_B2 ** ADAM_STEP
    nm = ADAM_B1 * m + (1.0 - ADAM_B1) * g
    nv = ADAM_B2 * v + (1.0 - ADAM_B2) * (g * g)
    return -ADAM_LR * ((nm / c1) / (jnp.sqrt(nv / c2) + ADAM_EPS) + ADAM_WD * w), nm, nv


def _row_tile(rows, cols):
    padded = -(-cols // LANES) * LANES
    cap = max(16, ADAM_BLOCK_BYTES // (N_DEV * padded * 4))
    best = 16
    for t in range(16, min(rows, cap) + 1, 16):
        if rows % t == 0:
            best = t
    return best


def _adamw(parts, w, m, v, name):
    _, rows, cols = w.shape
    tb = _row_tile(rows, cols)

    def body(p_ref, w_ref, m_ref, v_ref, g_ref, d_ref, nm_ref, nv_ref):
        g = p_ref[0].astype(F32)
        for d in range(1, N_DEV):
            g = g + p_ref[d].astype(F32)
        g_ref[0] = g
        d_ref[0], nm_ref[0], nv_ref[0] = _adam_update(g, w_ref[0], m_ref[0], v_ref[0])

    row = pl.BlockSpec((1, tb, cols), lambda i: (0, i, 0))
    out = jax.ShapeDtypeStruct(w.shape, F32)
    return pl.pallas_call(
        body, name=name, grid=(rows // tb,),
        in_specs=[pl.BlockSpec((N_DEV, tb, cols), lambda i: (0, i, 0)), row, row, row], out_specs=[row] * 4,
        out_shape=[out] * 4, compiler_params=_params(1))(parts, w, m, v)


def _adamw_small(parts, ws, ms, vs, name):
    n = len(ws)

    def body(*refs):
        p_ref = refs[0]
        w_refs, m_refs, v_refs = refs[1:1 + n], refs[1 + n:1 + 2 * n], refs[1 + 2 * n:1 + 3 * n]
        outs = refs[1 + 3 * n:]
        base = 0
        for j in range(n):
            rows, cols = ws[j].shape
            size = rows * cols
            for ch in range(-(-size // LANES)):
                r, c0 = divmod(ch * LANES, cols)
                width = min(LANES, cols - c0)
                g = p_ref[0, base + ch:base + ch + 1, 0:width]
                for d in range(1, N_DEV):
                    g = g + p_ref[d, base + ch:base + ch + 1, 0:width]
                at = (slice(r, r + 1), slice(c0, c0 + width))
                delta, nm, nv = _adam_update(g, w_refs[j][at], m_refs[j][at], v_refs[j][at])
                for out, val in zip((outs[j], outs[n + j], outs[2 * n + j], outs[3 * n + j]), (g, delta, nm, nv)):
                    out[at] = val
            base += -(-size // (8 * LANES)) * 8

    vmem = pl.BlockSpec(memory_space=pltpu.VMEM)
    res = pl.pallas_call(
        body, name=name, in_specs=[vmem] * (1 + 3 * n), out_specs=[vmem] * (4 * n),
        out_shape=[jax.ShapeDtypeStruct(a.shape, F32) for a in ws] * 4)(parts, *ws, *ms, *vs)
    return res[:n], res[n:2 * n], res[2 * n:3 * n], res[3 * n:]


def _rows(a, multiple):
    flat = a.reshape(-1)
    pad = -flat.shape[0] % (multiple * LANES)
    if pad:
        flat = jnp.concatenate([flat, jnp.zeros((pad,), a.dtype)])
    return flat.reshape(-1, LANES)


def _pack(arrs, multiple):
    return jnp.concatenate([_rows(a, multiple) for a in arrs], axis=0)


def _gathered_to_full(g, name, shard_shape):
    g = g.reshape((N_DEV,) + shard_shape)
    if name in COL_SHARDED:
        return jnp.transpose(g, (1, 0, 2)).reshape(shard_shape[0], N_DEV * shard_shape[1])
    return g.reshape(N_DEV * shard_shape[0], shard_shape[1])


def _full_to_per_device(full, name):
    if name in COL_SHARDED:
        r, c = full.shape
        return jnp.transpose(full.reshape(r, N_DEV, c // N_DEV), (1, 0, 2))
    return full.reshape(N_DEV, full.shape[0] // N_DEV, full.shape[1])


def _w2cat(w2, a2, g2):
    n_w, n_a, n_g = LORA
    out = jnp.zeros((256, 3 * WB), w2.dtype)
    out = out.at[0:n_w, 0:WB].set(w2)
    out = out.at[n_w:n_w + n_a, WB:2 * WB].set(a2)
    return out.at[n_w + n_a:n_w + n_a + n_g, 2 * WB:].set(g2)


class _Local:
    def __init__(self, w):
        self.w = w

    def weights(self, group, after=None):
        return self.w

    def started(self):
        return ()

    def send(self, grads, names):
        return ()


class _Overlapped:
    GROUPS = {"ffn1": ("ffn1_w_gate", "ffn1_w_up", "ffn1_w_down"),
              "mixer_in": ("w_in", "rwkv_w2", "rwkv_a2", "rwkv_g2"),
              "late": ("w_out", "ffn2_w_gate", "ffn2_w_up", "ffn2_w_down")}

    def __init__(self, wts):
        x, y, c = (lax.axis_index(n) for n in AXES)
        self.wts, self.me, self.gathers, self.sends = wts, 4 * x + 2 * y + c, {}, []
        self._gather("ffn1", ())

    def _gather(self, group, after):
        names = self.GROUPS[group]
        shards = [self.wts[n].astype(MXU) for n in names]
        handle, token = _exchange_start(shards, [False] * len(names), "gather_" + group, after)
        self.gathers[group] = (names, shards, handle, token)
        self.newest = token

    def started(self):
        return (self.newest,)

    def _own_slot(self, land, mine):
        return lax.dynamic_update_slice(land, mine[None], (self.me,) + (0,) * mine.ndim)

    def weights(self, group, after=None):
        names, shards, handle, token = self.gathers[group]
        lands = _exchange_wait(handle, token if after is None else after, "gathered_" + group)
        w = {n: _gathered_to_full(self._own_slot(land, own), n, own.shape[1:])
             for n, own, land in zip(names, shards, lands)}
        order = list(self.GROUPS)
        if group != order[-1]:
            self._gather(order[order.index(group) + 1], (w[names[0]],))
        if group == "ffn1":
            for n in SMALL:
                keep = n in ("hgrn_lb_logits", "rwkv_r_k", "final_norm")
                w[n] = self.wts[n] if keep else self.wts[n].reshape(1, -1)
        return w

    def send(self, grads, names, small=None):
        contrib = [_full_to_per_device(grads[n], n).astype(WIRE) for n in names]
        per_peer = [True] * len(names)
        if small is not None:
            names, contrib, per_peer = names + ("small",), contrib + [small], per_peer + [False]
        handle, token = _exchange_start(contrib, per_peer, "scatter_" + names[0])
        self.sends.append((names, contrib, per_peer, handle))
        self.last_token = token
        return (token,)

    def received(self, which, after):
        names, contrib, per_peer, handle = self.sends[which]
        lands = _exchange_wait(handle, after, "scattered_" + names[0])
        parts = {}
        for n, own, pp, land in zip(names, contrib, per_peer, lands):
            mine = lax.dynamic_index_in_dim(own, self.me, 0, keepdims=False) if pp else own
            parts[n] = self._own_slot(land, mine)
        return parts


def _local_step(x, target, net):
    n_w, n_a, n_g = LORA
    w = dict(net.weights("ffn1"))
    h1 = _rms_fwd(x, w["ffn1_norm"], "ffn1_norm")
    x1 = _ffn_fwd(x, h1, w["ffn1_w_gate"], w["ffn1_w_up"], w["ffn1_w_down"], "ffn1_fwd", after=net.started())
    w.update(net.weights("mixer_in", x1))
    w_in_pad = jnp.pad(w["w_in"], ((0, 0), (0, N_INP - N_IN)))
    mu_pad = jnp.pad(w["rwkv_shift_mu"], ((0, 0), (0, 1792 - 1696)))
    w2cat = _w2cat(w["rwkv_w2"], w["rwkv_a2"], w["rwkv_g2"])
    r_k = w["rwkv_r_k"].reshape(1, WB)
    rw = (mu_pad, w2cat, w["rwkv_w0"], w["rwkv_a0"], w["rwkv_k_k"], w["rwkv_k_a"])

    h2 = _rms_fwd(x1, w["mix_norm"], "mix_norm")
    p_all = _matmul(h2, w_in_pad, after=net.started(), name="in_proj")
    oa, oraw, states = _hgrn_fwd(p_all, w["hgrn_lb_logits"], w["hgrn_out_norm"], "hgrn_fwd")
    r, decay, k2, v, sa, sb, g = _rwkv_prep(p_all, *rw, "rwkv_prep")
    y, s_a = _rwkv_scan_fwd(r, decay, k2, v, sa, sb, "rwkv_scan_fwd")
    sall = _rwkv_states(s_a, decay, k2, v, sb, "rwkv_states")
    post_w = (r_k, w["rwkv_gn_w"], w["rwkv_gn_b"])
    ob = _rwkv_post(y, r, k2, v, g, *post_w, "rwkv_post")
    w.update(net.weights("late", ob))
    o = jnp.concatenate([oa, ob], axis=1)
    x2 = _matmul(o, w["w_out"], res=x1, name="out_proj")
    h3 = _rms_fwd(x2, w["ffn2_norm"], "ffn2_norm")
    x3 = _ffn_fwd(x2, h3, w["ffn2_w_gate"], w["ffn2_w_up"], w["ffn2_w_down"], "ffn2_fwd")
    loss, dx3, d_final = _loss_head(x3, w["final_norm"].reshape(1, D), target, "loss_head")

    grads = {"final_norm": d_final.reshape(D)}

    def ffn_back(prefix, h, dy, x_in, norm):
        wg, wu, wd = (w[prefix + s] for s in ("_w_gate", "_w_up", "_w_down"))
        dh, act, dgate, dup, dout = _ffn_bwd(h, dy, wg, wu, wd, prefix + "_bwd")
        sent = ()
        for which, a_op, b_op in (("_w_gate", h, dgate), ("_w_up", h, dup), ("_w_down", act, dout)):
            grads[prefix + which] = _matmul(a_op, b_op, ta=True, out_dtype=WIRE, after=sent, name=prefix + "_d" + which)
            sent = net.send(grads, (prefix + which,))
        dx, grads[prefix + "_norm"] = _rms_bwd(x_in, norm, dh, dy, prefix + "_norm_bwd", after=sent)
        return dx

    dx2 = ffn_back("ffn2", h3, dx3, x2, w["ffn2_norm"])
    grads["w_out"] = _matmul(o, dx2, ta=True, out_dtype=WIRE, name="d_w_out")
    sent = net.send(grads, ("w_out",))
    do = _matmul(dx2, w["w_out"], tb=True, after=sent, name="d_mixed")
    dqa, dfa, dia, dga, grads["hgrn_out_norm"], grads["hgrn_lb_logits"] = _hgrn_bwd(
        p_all, w["hgrn_lb_logits"], w["hgrn_out_norm"], oraw, states, do[:, :WA], "hgrn_bwd")
    dy, dg, dr_b, dk2_b, dv_b, grads["rwkv_gn_w"], grads["rwkv_gn_b"], d_rk = _rwkv_post_bwd(
        do[:, WA:], y, r, k2, v, g, *post_w, "rwkv_post_bwd")
    grads["rwkv_r_k"] = d_rk.reshape(w["rwkv_r_k"].shape)
    d_sa, dv = _rwkv_scan_bwd(dy, r, decay, k2, sa, sb, "rwkv_scan_bwd")
    dr, dw, dk2, dsa, dsb = _rwkv_scan_bwd_values(dy, r, decay, v, sa, s_a, d_sa, sall, "rwkv_scan_bwd_values")
    (dsr, dsk, dsv, dslo, dw2cat, grads["rwkv_w0"], grads["rwkv_a0"], grads["rwkv_k_k"],
     grads["rwkv_k_a"]) = _rwkv_prep_bwd((dr, dw, dk2, dv, dsa, dsb, dg, dr_b, dk2_b, dv_b), p_all, *rw,
                                         "rwkv_prep_bwd")
    grads["rwkv_w2"] = dw2cat[0:n_w, 0:WB]
    grads["rwkv_a2"] = dw2cat[n_w:n_w + n_a, WB:2 * WB]
    grads["rwkv_g2"] = dw2cat[n_w + n_a:n_w + n_a + n_g, 2 * WB:]
    dpr, dpk, dpv, dplo, dmu_r, dmu_k, dmu_v, dmu_lo = _shift_bwd((dsr, dsk, dsv, dslo), p_all, mu_pad, "shift_bwd")
    grads["rwkv_shift_mu"] = jnp.concatenate([dmu_r, dmu_k, dmu_v, dmu_lo], axis=1)[:, :1696]
    dp = jnp.concatenate([dqa, dfa, dia, dga, dpr, dpk, dpv, dplo], axis=1)
    grads["w_in"] = _matmul(h2, dp, ta=True, out_dtype=WIRE, name="d_w_in")[:, :N_IN]
    sent = net.send(grads, ("w_in", "rwkv_w2", "rwkv_a2", "rwkv_g2"))
    dh2 = _matmul(dp, w_in_pad, tb=True, after=sent, name="d_h2")
    dx1, grads["mix_norm"] = _rms_bwd(x1, w["mix_norm"], dh2, dx2, "mix_norm_bwd")
    dx0 = ffn_back("ffn1", h1, dx1, x, w["ffn1_norm"])
    return loss[0, 0], dx0, grads


def kernel(x, ffn1_norm, ffn1_w_gate, ffn1_w_up, ffn1_w_down, mix_norm, w_in, hgrn_lb_logits, hgrn_out_norm, rwkv_shift_mu, rwkv_w0, rwkv_w2, rwkv_a0, rwkv_a2, rwkv_g2, rwkv_k_k, rwkv_k_a, rwkv_r_k, rwkv_gn_w, rwkv_gn_b, w_out, ffn2_norm, ffn2_w_gate, ffn2_w_up, ffn2_w_down, final_norm, loss_target, m_ffn1_norm, m_ffn1_w_gate, m_ffn1_w_up, m_ffn1_w_down, m_mix_norm, m_w_in, m_hgrn_lb_logits, m_hgrn_out_norm, m_rwkv_shift_mu, m_rwkv_w0, m_rwkv_w2, m_rwkv_a0, m_rwkv_a2, m_rwkv_g2, m_rwkv_k_k, m_rwkv_k_a, m_rwkv_r_k, m_rwkv_gn_w, m_rwkv_gn_b, m_w_out, m_ffn2_norm, m_ffn2_w_gate, m_ffn2_w_up, m_ffn2_w_down, m_final_norm, v_ffn1_norm, v_ffn1_w_gate, v_ffn1_w_up, v_ffn1_w_down, v_mix_norm, v_w_in, v_hgrn_lb_logits, v_hgrn_out_norm, v_rwkv_shift_mu, v_rwkv_w0, v_rwkv_w2, v_rwkv_a0, v_rwkv_a2, v_rwkv_g2, v_rwkv_k_k, v_rwkv_k_a, v_rwkv_r_k, v_rwkv_gn_w, v_rwkv_gn_b, v_w_out, v_ffn2_norm, v_ffn2_w_gate, v_ffn2_w_up, v_ffn2_w_down, v_final_norm):
    args = dict(locals())
    wts = {n: args[n] for n in WEIGHTS}
    mom = {n: args["m_" + n] for n in WEIGHTS}
    var = {n: args["v_" + n] for n in WEIGHTS}
    net = _Overlapped(wts)
    loss, grad_x, grads = _local_step(x[0], loss_target[0], net)
    loss = lax.psum(loss, AXES)
    after, = net.send(grads, (), _pack([grads[n] for n in SMALL], 8))

    new = {}
    two_d = lambda a: a if a.ndim == 2 else a.reshape(1, -1)
    for which in range(len(net.sends)):
        for n, part in net.received(which, after).items():
            if n == "small":
                small = _adamw_small(part, *([two_d(src[k]) for k in SMALL] for src in (wts, mom, var)), "adamw_small")
                for j, k in enumerate(SMALL):
                    new[k] = [res[j].reshape(wts[k].shape) for res in small]
            else:
                new[n] = _adamw(part, wts[n], mom[n], var[n], "adamw_" + n)
                after = new[n][1]
    return (loss, grad_x[None], *[new[n][0] for n in WEIGHTS], *[new[n][1] for n in WEIGHTS],
            *[new[n][2] for n in WEIGHTS], *[new[n][3] for n in WEIGHTS])
```

```python
import functools
import math

import jax
import jax.numpy as jnp
from jax import lax
from jax.experimental import pallas as pl
from jax.experimental.pallas import tpu as pltpu

F32 = jnp.float32
MXU = jnp.bfloat16
WIRE = jnp.bfloat16
D = 1024
FF = 2816
WA = 512
WB = 512
HD_B = 64
N_IN = 3744
N_INP = 3840
COL_R, COL_K, COL_V = 4, 5, 6
COL_L = 14
LORA = (32, 32, 96)
HG_CHUNK = 64
SCAN_CHUNK = 64
SCAN_UNROLL = 4
NORM_EPS = 1e-6
GN_EPS = 64e-5
L2_EPS = 1e-12
DECAY_C = math.exp(-0.5)
N_DEV = 8
LANES = 128
ADAM_BLOCK_BYTES = 4 * 1024 * 1024
MATMUL_BLOCK_BYTES = 40 * 1024 * 1024
VMEM_LIMIT = 56 * 1024 * 1024
ADAM_LR, ADAM_B1, ADAM_B2, ADAM_EPS, ADAM_WD, ADAM_STEP = 0.001, 0.9, 0.999, 1e-08, 0.01, 10
AXES = ("x", "y", "c")

SHARDED = ("ffn1_w_gate", "ffn1_w_up", "ffn1_w_down", "w_in", "rwkv_w2", "rwkv_a2", "rwkv_g2", "w_out",
           "ffn2_w_gate", "ffn2_w_up", "ffn2_w_down")
COL_SHARDED = {"ffn1_w_gate", "ffn1_w_up", "w_in", "rwkv_w2", "rwkv_a2", "rwkv_g2", "ffn2_w_gate", "ffn2_w_up"}
SMALL = ("ffn1_norm", "mix_norm", "hgrn_lb_logits", "hgrn_out_norm", "rwkv_shift_mu", "rwkv_w0", "rwkv_a0",
         "rwkv_k_k", "rwkv_k_a", "rwkv_r_k", "rwkv_gn_w", "rwkv_gn_b", "ffn2_norm", "final_norm")
WEIGHTS = ("ffn1_norm", "ffn1_w_gate", "ffn1_w_up", "ffn1_w_down", "mix_norm", "w_in", "hgrn_lb_logits",
           "hgrn_out_norm", "rwkv_shift_mu", "rwkv_w0", "rwkv_w2", "rwkv_a0", "rwkv_a2", "rwkv_g2", "rwkv_k_k",
           "rwkv_k_a", "rwkv_r_k", "rwkv_gn_w", "rwkv_gn_b", "w_out", "ffn2_norm", "ffn2_w_gate", "ffn2_w_up",
           "ffn2_w_down", "final_norm")


def _tile(n, cap):
    if n <= cap:
        return n
    for t in range(cap - cap % LANES, 0, -LANES):
        if n % t == 0:
            return t
    raise ValueError((n, cap))


def _params(n_axes):
    return pltpu.CompilerParams(dimension_semantics=("arbitrary",) * n_axes, vmem_limit_bytes=VMEM_LIMIT)


def _sig(x):
    return jax.nn.sigmoid(x)


def _dsilu(z, s):
    return s * (1.0 + z * (1.0 - s))


def _dot(a, b, dims=((1,), (0,)), precision=None):
    return lax.dot_general(a, b, (dims, ((), ())), preferred_element_type=F32, precision=precision)


_NT = ((1,), (1,))
_TN = ((0,), (0,))
_HI = lax.Precision.HIGHEST


def _iota(shape, dim):
    return lax.broadcasted_iota(jnp.int32, shape, dim)


def _split_dot(x, ones, passes):
    hi = x.astype(jnp.bfloat16)
    acc = _dot(hi, ones)
    rem = x
    for _ in range(passes - 1):
        rem = rem - hi.astype(F32)
        hi = rem.astype(jnp.bfloat16)
        acc = acc + _dot(hi, ones)
    return acc


def _head_ones(n, width):
    shift = width.bit_length() - 1
    return (_iota((n, n), 0) >> shift == _iota((n, n), 1) >> shift).astype(jnp.bfloat16)


def _matmul(a, b, *, ta=False, tb=False, out_dtype=F32, res=None, after=(), name):
    m, k = (a.shape[1], a.shape[0]) if ta else a.shape
    n = b.shape[0] if tb else b.shape[1]
    tm, tn = _tile(m, 1408), _tile(n, 1408)
    in_bytes = max(a.dtype.itemsize, b.dtype.itemsize)
    for tk in (_tile(k, 1408), _tile(k, 1024), _tile(k, 512), _tile(k, 256)):
        if 2 * (tm + tn) * tk * in_bytes + 3 * tm * tn * 4 <= MATMUL_BLOCK_BYTES:
            break
    nk = k // tk
    dims = ((0 if ta else 1,), (1 if tb else 0,))

    def body(*refs):
        a_ref, b_ref = refs[:2]
        o_ref, acc = refs[-2:]
        kk = pl.program_id(2)

        @pl.when(kk == 0)
        def _():
            acc[...] = jnp.zeros_like(acc)

        acc[...] += _dot(a_ref[...].astype(MXU), b_ref[...].astype(MXU), dims)

        @pl.when(kk == nk - 1)
        def _():
            v = acc[...]
            if res is not None:
                v = v + refs[2][...]
            o_ref[...] = v.astype(out_dtype)

    a_spec = pl.BlockSpec((tk, tm), lambda i, j, kk: (kk, i)) if ta else pl.BlockSpec((tm, tk), lambda i, j, kk: (i, kk))
    b_spec = pl.BlockSpec((tn, tk), lambda i, j, kk: (j, kk)) if tb else pl.BlockSpec((tk, tn), lambda i, j, kk: (kk, j))
    o_spec = pl.BlockSpec((tm, tn), lambda i, j, kk: (i, j))
    ins, specs = [a, b], [a_spec, b_spec]
    if res is not None:
        ins.append(res)
        specs.append(o_spec)
    ins += list(after)
    specs += [pl.BlockSpec(memory_space=pl.ANY)] * len(after)
    return pl.pallas_call(
        body, name=name, grid=(m // tm, n // tn, nk), in_specs=specs, out_specs=o_spec,
        out_shape=jax.ShapeDtypeStruct((m, n), out_dtype), scratch_shapes=[pltpu.VMEM((tm, tn), F32)],
        compiler_params=_params(3))(*ins)


def _rms_fwd(x, g, name):
    t = x.shape[0]
    tb = _tile(t, 512)

    def body(x_ref, g_ref, o_ref):
        xv = x_ref[...]
        rinv = lax.rsqrt(jnp.mean(xv * xv, axis=-1, keepdims=True) + NORM_EPS)
        o_ref[...] = (xv * rinv * g_ref[...]).astype(MXU)

    return pl.pallas_call(
        body, name=name, grid=(t // tb,),
        in_specs=[pl.BlockSpec((tb, D), lambda i: (i, 0)), pl.BlockSpec((1, D), lambda i: (0, 0))],
        out_specs=pl.BlockSpec((tb, D), lambda i: (i, 0)), out_shape=jax.ShapeDtypeStruct((t, D), MXU),
        compiler_params=_params(1))(x, g)


def _rms_bwd(x, g, dh, dres, name, after=()):
    t = x.shape[0]
    tb = _tile(t, 512)

    def body(x_ref, g_ref, dh_ref, dres_ref, *rest):
        dx_ref, dg_ref = rest[-2:]

        @pl.when(pl.program_id(0) == 0)
        def _():
            dg_ref[...] = jnp.zeros_like(dg_ref)

        xv = x_ref[...]
        rinv = lax.rsqrt(jnp.mean(xv * xv, axis=-1, keepdims=True) + NORM_EPS)
        xhat = xv * rinv
        dhv = dh_ref[...]
        dg_ref[...] += jnp.sum(dhv * xhat, axis=0, keepdims=True)
        dxhat = dhv * g_ref[...]
        dx_ref[...] = dres_ref[...] + rinv * (dxhat - xhat * jnp.mean(dxhat * xhat, axis=-1, keepdims=True))

    row = pl.BlockSpec((tb, D), lambda i: (i, 0))
    vec = pl.BlockSpec((1, D), lambda i: (0, 0))
    return pl.pallas_call(
        body, name=name, grid=(t // tb,),
        in_specs=[row, vec, row, row] + [pl.BlockSpec(memory_space=pl.ANY)] * len(after), out_specs=[row, vec],
        out_shape=[jax.ShapeDtypeStruct((t, D), F32), jax.ShapeDtypeStruct((1, D), F32)],
        compiler_params=_params(1))(x, g, dh, dres, *after)


def _loss_head(x, g, target, name):
    t = x.shape[0]
    tb = _tile(t, 512)

    def body(x_ref, g_ref, t_ref, loss_ref, dx_ref, dg_ref):
        @pl.when(pl.program_id(0) == 0)
        def _():
            dg_ref[...] = jnp.zeros_like(dg_ref)
            loss_ref[...] = jnp.zeros_like(loss_ref)

        xv = x_ref[...]
        gv = g_ref[...]
        rinv = lax.rsqrt(jnp.mean(xv * xv, axis=-1, keepdims=True) + NORM_EPS)
        xhat = xv * rinv
        err = xhat * gv - t_ref[...]
        per_tok = jnp.mean(err * err, axis=-1, keepdims=True)
        loss_ref[...] += jnp.broadcast_to(0.5 * jnp.sum(per_tok, axis=0, keepdims=True), loss_ref.shape)
        dy = err * (1.0 / D)
        dg_ref[...] += jnp.sum(dy * xhat, axis=0, keepdims=True)
        dxhat = dy * gv
        dx_ref[...] = rinv * (dxhat - xhat * jnp.mean(dxhat * xhat, axis=-1, keepdims=True))

    row = pl.BlockSpec((tb, D), lambda i: (i, 0))
    vec = pl.BlockSpec((1, D), lambda i: (0, 0))
    return pl.pallas_call(
        body, name=name, grid=(t // tb,), in_specs=[row, vec, row],
        out_specs=[pl.BlockSpec((1, LANES), lambda i: (0, 0)), row, vec],
        out_shape=[jax.ShapeDtypeStruct((1, LANES), F32), jax.ShapeDtypeStruct((t, D), F32),
                   jax.ShapeDtypeStruct((1, D), F32)],
        compiler_params=_params(1))(x, g, target)


def _ffn_fwd(x, h, wg, wu, wd, name, after=()):
    t = x.shape[0]
    tb, fb = _tile(t, 1024), 256
    nf = FF // fb

    def body(x_ref, h_ref, wg_ref, wu_ref, wd_ref, *rest):
        o_ref, acc = rest[-2:]
        f = pl.program_id(1)

        @pl.when(f == 0)
        def _():
            acc[...] = jnp.zeros_like(acc)

        hv = h_ref[...]
        gate = _dot(hv, wg_ref[...])
        up = _dot(hv, wu_ref[...])
        act = (gate * _sig(gate) * up).astype(MXU)
        acc[...] += _dot(act, wd_ref[...])

        @pl.when(f == nf - 1)
        def _():
            o_ref[...] = x_ref[...] + 0.5 * acc[...]

    row = pl.BlockSpec((tb, D), lambda i, f: (i, 0))
    col = pl.BlockSpec((D, fb), lambda i, f: (0, f))
    return pl.pallas_call(
        body, name=name, grid=(t // tb, nf),
        in_specs=[row, row, col, col, pl.BlockSpec((fb, D), lambda i, f: (f, 0))]
        + [pl.BlockSpec(memory_space=pl.ANY)] * len(after), out_specs=row,
        out_shape=jax.ShapeDtypeStruct((t, D), F32), scratch_shapes=[pltpu.VMEM((tb, D), F32)],
        compiler_params=_params(2))(x, h, wg, wu, wd, *after)


def _ffn_bwd(h, dy, wg, wu, wd, name):
    t = h.shape[0]
    tb, fb = _tile(t, 1024), 256
    nf = FF // fb

    def body(h_ref, dy_ref, wg_ref, wu_ref, wd_ref, act_ref, dg_ref, du_ref, dout_ref):
        hv = h_ref[...]
        dout = (0.5 * dy_ref[...]).astype(MXU)
        dout_ref[...] = dout
        gate = _dot(hv, wg_ref[...])
        up = _dot(hv, wu_ref[...])
        dact = _dot(dout, wd_ref[...], _NT)
        s = _sig(gate)
        silu = gate * s
        act_ref[...] = (silu * up).astype(MXU)
        du_ref[...] = (dact * silu).astype(MXU)
        dg_ref[...] = (dact * up * _dsilu(gate, s)).astype(MXU)

    row = pl.BlockSpec((tb, D), lambda i, f: (i, 0))
    col = pl.BlockSpec((D, fb), lambda i, f: (0, f))
    hid = pl.BlockSpec((tb, fb), lambda i, f: (i, f))
    hid_shape = jax.ShapeDtypeStruct((t, FF), MXU)
    return pl.pallas_call(
        body, name=name, grid=(t // tb, nf),
        in_specs=[row, row, col, col, pl.BlockSpec((fb, D), lambda i, f: (f, 0))],
        out_specs=[hid, hid, hid, row],
        out_shape=[hid_shape, hid_shape, hid_shape, jax.ShapeDtypeStruct((t, D), MXU)],
        compiler_params=_params(2))(h, dy, wg, wu, wd)


def _hgrn_chunk(qa, fa, lbl):
    c = HG_CHUNK
    lb = _sig(lbl[0:1, :] - lbl[1:2, :])
    sf = _sig(fa)
    forget = lb + (1.0 - lb) * sf
    kh = 1.0 - forget
    row, col = _iota((c, c), 0), _iota((c, c), 1)
    b = _dot((col <= row).astype(F32), jnp.log(forget), precision=_HI)
    bref, blast = b[c // 2:c // 2 + 1, :], b[c - 1:c, :]
    sq = _sig(qa)
    q = qa * sq
    qt, kt = q * jnp.exp(b - bref), kh * jnp.exp(bref - b)
    qb, kl = q * jnp.exp(b), kh * jnp.exp(blast - b)
    causal = col <= row
    return dict(lb=lb, sf=sf, forget=forget, sq=sq, qt=qt, kt=kt, qb=qb, kl=kl, decay=jnp.exp(blast),
                causal=causal, e_q=jnp.exp(b), e_qt=jnp.exp(b - bref), e_kt=jnp.exp(bref - b),
                e_kl=jnp.exp(blast - b))


def _hgrn_specs(t):
    c = HG_CHUNK
    return c, t // c, WA // LANES


def _hgrn_fwd(p_all, lbl, onorm, name):
    t = p_all.shape[0]
    c, n, nh = _hgrn_specs(t)

    def body(q_ref, f_ref, i_ref, g_ref, lbl_ref, on_ref, oa_ref, oraw_ref, st_ref, state):
        @pl.when(pl.program_id(0) == 0)
        def _():
            state[...] = jnp.zeros_like(state)

        for h in range(nh):
            at = slice(h * LANES, (h + 1) * LANES)
            k = _hgrn_chunk(q_ref[:, at], f_ref[:, at], lbl_ref[:, at])
            v = i_ref[:, at]
            st = state[h]
            st_ref[h, 0] = st
            a = jnp.where(k["causal"], _dot(k["qt"], k["kt"], _NT, _HI), 0.0)
            o = _dot(a, v, precision=_HI) + _dot(k["qb"], st, _NT, _HI)
            state[h] = st * k["decay"] + _dot(v, k["kl"], _TN, _HI)
            oraw_ref[:, at] = o
            rinv = lax.rsqrt(jnp.mean(o * o, axis=-1, keepdims=True) + NORM_EPS)
            ga = g_ref[:, at]
            oa_ref[:, at] = (o * rinv * on_ref[:, at] * (ga * _sig(ga))).astype(MXU)

    def blk(j):
        return pl.BlockSpec((c, WA), lambda i: (i, j))

    return pl.pallas_call(
        body, name=name, grid=(n,),
        in_specs=[blk(0), blk(1), blk(2), blk(3), pl.BlockSpec((2, WA), lambda i: (0, 0)),
                  pl.BlockSpec((1, WA), lambda i: (0, 0))],
        out_specs=[blk(0), blk(0), pl.BlockSpec((nh, 1, LANES, LANES), lambda i: (0, i, 0, 0))],
        out_shape=[jax.ShapeDtypeStruct((t, WA), MXU), jax.ShapeDtypeStruct((t, WA), F32),
                   jax.ShapeDtypeStruct((nh, n, LANES, LANES), F32)],
        scratch_shapes=[pltpu.VMEM((nh, LANES, LANES), F32)], compiler_params=_params(1))(
            p_all, p_all, p_all, p_all, lbl, onorm)


def _hgrn_bwd(p_all, lbl, onorm, oraw, states, doa, name):
    t = p_all.shape[0]
    c, n, nh = _hgrn_specs(t)

    def body(q_ref, f_ref, i_ref, g_ref, lbl_ref, on_ref, oraw_ref, st_ref, doa_ref,
             dq_ref, df_ref, di_ref, dg_ref, don_ref, dlbl_ref, dstate, dlb):
        @pl.when(pl.program_id(0) == 0)
        def _():
            dstate[...] = jnp.zeros_like(dstate)
            dlb[...] = jnp.zeros_like(dlb)
            don_ref[...] = jnp.zeros_like(don_ref)

        for h in range(nh):
            at = slice(h * LANES, (h + 1) * LANES)
            qa, fa, v, ga = q_ref[:, at], f_ref[:, at], i_ref[:, at], g_ref[:, at]
            k = _hgrn_chunk(qa, fa, lbl_ref[:, at])
            st, dst_next = st_ref[h, 0], dstate[h]
            o = oraw_ref[:, at]
            gain = on_ref[:, at]
            rinv = lax.rsqrt(jnp.mean(o * o, axis=-1, keepdims=True) + NORM_EPS)
            on = o * rinv
            sg = _sig(ga)
            gate = ga * sg
            dout = doa_ref[:, at]
            don_ref[:, at] += jnp.sum(dout * on * gate, axis=0, keepdims=True)
            dg_ref[:, at] = (dout * on * gain * _dsilu(ga, sg)).astype(MXU)
            d_on = dout * gain * gate
            do = rinv * (d_on - on * jnp.mean(d_on * on, axis=-1, keepdims=True))

            a = jnp.where(k["causal"], _dot(k["qt"], k["kt"], _NT, _HI), 0.0)
            dqb = _dot(do, st, precision=_HI)
            dstate[h] = dst_next * k["decay"] + _dot(do, k["qb"], _TN, _HI)
            da = jnp.where(k["causal"], _dot(do, v, _NT, _HI), 0.0)
            dqt = _dot(da, k["kt"], precision=_HI)
            dkt = _dot(da, k["qt"], _TN, _HI)
            dv = _dot(a, do, _TN, _HI) + _dot(k["kl"], dst_next, _NT, _HI)
            dkl = _dot(v, dst_next, precision=_HI)
            ddecay = jnp.sum(dst_next * st, axis=0, keepdims=True)
            dq = dqb * k["e_q"] + dqt * k["e_qt"]
            dk = dkt * k["e_kt"] + dkl * k["e_kl"]
            tq, tk, tl = dqt * k["qt"], dkt * k["kt"], dkl * k["kl"]
            db = dqb * k["qb"] + tq - tk - tl
            dbref = jnp.sum(tk - tq, axis=0, keepdims=True)
            dblast = jnp.sum(tl, axis=0, keepdims=True) + ddecay * k["decay"]
            rows = _iota((c, LANES), 0)
            db = db + jnp.where(rows == c // 2, dbref, 0.0) + jnp.where(rows == c - 1, dblast, 0.0)
            row, col = _iota((c, c), 0), _iota((c, c), 1)
            dlogf = _dot((col >= row).astype(F32), db, precision=_HI)
            dq_ref[:, at] = (dq * _dsilu(qa, k["sq"])).astype(MXU)
            di_ref[:, at] = dv.astype(MXU)
            dforget = dlogf / k["forget"] - dk
            sf, lb = k["sf"], k["lb"]
            df_ref[:, at] = (dforget * (1.0 - lb) * sf * (1.0 - sf)).astype(MXU)
            dlb[:, at] += jnp.sum(dforget * (1.0 - sf), axis=0, keepdims=True)
            dl0 = dlb[:, at] * lb * (1.0 - lb)
            dlbl_ref[:, at] = jnp.where(_iota((2, LANES), 0) == 0, dl0, -dl0)

    def blk(j):
        return pl.BlockSpec((c, WA), lambda i: (n - 1 - i, j))

    vec = pl.BlockSpec((1, WA), lambda i: (0, 0))
    lg = pl.BlockSpec((2, WA), lambda i: (0, 0))
    grad = jax.ShapeDtypeStruct((t, WA), MXU)
    return pl.pallas_call(
        body, name=name, grid=(n,),
        in_specs=[blk(0), blk(1), blk(2), blk(3), lg, vec, blk(0),
                  pl.BlockSpec((nh, 1, LANES, LANES), lambda i: (0, n - 1 - i, 0, 0)), blk(0)],
        out_specs=[blk(0), blk(0), blk(0), blk(0), vec, lg],
        out_shape=[grad, grad, grad, grad, jax.ShapeDtypeStruct((1, WA), F32), jax.ShapeDtypeStruct((2, WA), F32)],
        scratch_shapes=[pltpu.VMEM((nh, LANES, LANES), F32), pltpu.VMEM((1, WA), F32)],
        compiler_params=_params(1))(p_all, p_all, p_all, p_all, lbl, onorm, oraw, states, doa)


def _lora_act(x):
    lane = _iota(x.shape, 1)
    n_w, n_a, n_g = LORA
    return jnp.where(lane < n_w, jnp.tanh(x),
                     jnp.where(lane < n_w + n_a, x, jnp.where(lane < n_w + n_a + n_g, _sig(x), 0.0)))


def _lora_dact(x):
    lane = _iota(x.shape, 1)
    n_w, n_a, n_g = LORA
    th, s = jnp.tanh(x), _sig(x)
    return jnp.where(lane < n_w, 1.0 - th * th,
                     jnp.where(lane < n_w + n_a, 1.0, jnp.where(lane < n_w + n_a + n_g, s * (1.0 - s), 0.0)))


def _shift_down(cur, prev8, first):
    rolled = pltpu.roll(cur, 1, 0)
    edge = prev8[7:8, :] * jnp.where(first, 0.0, 1.0)
    return jnp.where(_iota(cur.shape, 0) == 0, edge, rolled)


def _shift_up(cur, next8, last):
    rows = cur.shape[0]
    rolled = pltpu.roll(cur, rows - 1, 0)
    edge = next8[0:1, :] * jnp.where(last, 0.0, 1.0)
    return jnp.where(_iota(cur.shape, 0) == rows - 1, edge, rolled)


def _rwkv_inputs(refs, first, ones):
    (pr, pk, pv, plo, qr, qk, qv, qlo, mr, mk, mv, mlo, w2c, w0, a0, kk_w, ka_w) = refs
    mix = lambda cur, prev, mu: cur[...] + mu[...] * (_shift_down(cur[...], prev[...], first) - cur[...])
    r, k, v, lo = mix(pr, qr, mr), mix(pk, qk, mk), mix(pv, qv, mv), mix(plo, qlo, mlo)
    z = _lora_act(lo)
    lin = _dot(z.astype(MXU), w2c[...])
    sg = _sig(w0[...] + lin[:, :WB])
    decay = jnp.exp(-DECAY_C * sg)
    a = _sig(a0[...] + lin[:, WB:2 * WB])
    g = lin[:, 2 * WB:]
    kk0 = k * kk_w[...]
    nrm = jnp.sqrt(_split_dot(kk0 * kk0, ones, 3))
    den = jnp.maximum(nrm, L2_EPS)
    kk = kk0 / den
    k2 = k * (1.0 + (a - 1.0) * ka_w[...])
    return dict(r=r, k=k, v=v, lo=lo, z=z, sg=sg, decay=decay, a=a, g=g, kk=kk, den=den, nrm=nrm, k2=k2)


def _rwkv_in_specs(t, tb):
    nt8 = tb // 8

    def cur(w, j):
        return pl.BlockSpec((tb, w), lambda i: (i, j))

    def prev(w, j):
        return pl.BlockSpec((8, w), lambda i: (jnp.maximum(i * nt8 - 1, 0), j))

    def vec(w, j=0):
        return pl.BlockSpec((1, w), lambda i: (0, j))

    return [cur(WB, COL_R), cur(WB, COL_K), cur(WB, COL_V), cur(256, COL_L),
            prev(WB, COL_R), prev(WB, COL_K), prev(WB, COL_V), prev(256, COL_L),
            vec(WB, 0), vec(WB, 1), vec(WB, 2), vec(256, 6),
            pl.BlockSpec((256, 3 * WB), lambda i: (0, 0)), vec(WB), vec(WB), vec(WB), vec(WB)]


def _rwkv_in_args(p_all, mu_pad, w2cat, w0, a0, k_k, k_a):
    return (p_all,) * 8 + (mu_pad,) * 4 + (w2cat, w0, a0, k_k, k_a)


def _rwkv_prep(p_all, mu_pad, w2cat, w0, a0, k_k, k_a, name):
    t = p_all.shape[0]
    tb = _tile(t, 256)

    def body(*refs):
        ins, outs = refs[:17], refs[17:]
        q = _rwkv_inputs(ins, pl.program_id(0) == 0, _head_ones(WB, HD_B))
        for ref, val in zip(outs, (q["r"], q["decay"], q["k2"], q["v"], -q["kk"], q["kk"] * q["a"], q["g"])):
            ref[...] = val

    out = pl.BlockSpec((tb, WB), lambda i: (i, 0))
    return pl.pallas_call(
        body, name=name, grid=(t // tb,), in_specs=_rwkv_in_specs(t, tb), out_specs=[out] * 7,
        out_shape=[jax.ShapeDtypeStruct((t, WB), F32)] * 7, compiler_params=_params(1))(
            *_rwkv_in_args(p_all, mu_pad, w2cat, w0, a0, k_k, k_a))


def _pair_rows(x8, i):
    return jnp.concatenate([jnp.broadcast_to(x8[i:i + 1, p * LANES:(p + 1) * LANES], (HD_B, LANES))
                            for p in range(4)], axis=0)


def _pair_sums(x):
    return jnp.concatenate([jnp.sum(x[p * HD_B:(p + 1) * HD_B], axis=0, keepdims=True) for p in range(4)], axis=1)


def _put_row(buf, i, row):
    return jnp.where(_iota(buf.shape, 0) == i, row, buf)


def _pieces(x):
    hi = x.astype(jnp.bfloat16).astype(F32)
    lo = (x - hi).astype(jnp.bfloat16).astype(F32)
    upper = (_iota((x.shape[0], LANES), 1) & (HD_B // 2)) != 0
    swapped = [jnp.where(upper, pltpu.roll(lo[:, p * LANES:(p + 1) * LANES], HD_B // 2, 1),
                         pltpu.roll(lo[:, p * LANES:(p + 1) * LANES], LANES - HD_B // 2, 1)) for p in range(4)]
    return hi, jnp.concatenate(swapped, axis=1)


def _scan_consts():
    row, lane = _iota((HD_B, LANES), 0), _iota((HD_B, LANES), 1) & (HD_B - 1)
    either = ((row ^ lane) & (HD_B // 2 - 1)) == 0
    return ((row ^ lane) & (HD_B // 2)) != 0, either.astype(jnp.bfloat16), _head_ones(LANES, HD_B)


def _pair_cols(many, consts):
    swapped, either, ones = consts
    tiles = []
    for (hi8, lo8), i in many:
        for p in range(4):
            lanes = slice(p * LANES, (p + 1) * LANES)
            hi = jnp.broadcast_to(hi8[i:i + 1, lanes], (16, LANES)).astype(jnp.bfloat16)
            lo = jnp.broadcast_to(lo8[i:i + 1, lanes], (16, LANES)).astype(jnp.bfloat16)
            for g in range(HD_B // 16):
                rows = slice(g * 16, (g + 1) * 16)
                tiles.append(jnp.where(swapped[rows], lo, hi) * either[rows])
    out = _dot(jnp.concatenate(tiles, axis=0), ones)
    return [out[m * 4 * HD_B:(m + 1) * 4 * HD_B] for m in range(len(many))]


def _block_products(w8):
    rows = _iota(w8.shape, 0)
    down, up = w8, w8
    for shift in (1, 2, 4):
        down = down * jnp.where(rows >= shift, pltpu.roll(down, shift, 0), 1.0)
        up = up * jnp.where(rows < 8 - shift, pltpu.roll(up, 8 - shift, 0), 1.0)
    return down, up


def _blocked_loop(n_blocks, prepare, advance, init):
    unroll = SCAN_UNROLL if n_blocks % SCAN_UNROLL == 0 else 1

    def trip(g, carry):
        prepared = [prepare(g * unroll + i) for i in range(unroll)]
        for p in prepared:
            carry = advance(p, carry)
        return carry

    return lax.fori_loop(0, n_blocks // unroll, trip, init)


def _rwkv_scan_fwd(r, w, k, v, a, b, name):
    t = r.shape[0]
    cc = min(t, SCAN_CHUNK)

    def body(r_ref, w_ref, k_ref, v_ref, a_ref, b_ref, y_ref, sa_ref, state):
        @pl.when(pl.program_id(0) == 0)
        def _():
            state[...] = jnp.zeros_like(state)

        consts = _scan_consts()

        def prepare(j):
            rows = pl.ds(pl.multiple_of(j * 8, 8), 8)
            r8, w8, k8, v8, a8, b8 = (ref[rows, :] for ref in (r_ref, w_ref, k_ref, v_ref, a_ref, b_ref))
            decay, _ = _block_products(w8)
            before = jnp.where(_iota(w8.shape, 0) == 0, 1.0, pltpu.roll(decay, 1, 0))
            inv = 1.0 / decay
            scaled = [_pieces(x) for x in (a8 * before, b8 * inv, k8 * inv, r8 * decay)]
            return rows, v8, _pair_cols([(x, i) for i in range(8) for x in scaled] + [(_pieces(decay), 7)], consts)

        def advance(prepared, sk):
            rows, v8, cols = prepared
            y8 = jnp.zeros((8, WB), F32)
            sa8 = jnp.zeros((8, WB), F32)
            for i in range(8):
                a_c, b_c, k_c, r_c = cols[4 * i:4 * i + 4]
                sa = _pair_sums(sk * a_c)
                sk = sk + b_c * _pair_rows(sa, 0) + k_c * _pair_rows(v8, i)
                y8 = _put_row(y8, i, _pair_sums(sk * r_c))
                sa8 = _put_row(sa8, i, sa)
            y_ref[rows, :] = y8
            sa_ref[rows, :] = sa8
            return sk * cols[-1]

        state[...] = _blocked_loop(cc // 8, prepare, advance, state[...])

    row = pl.BlockSpec((cc, WB), lambda i: (i, 0))
    return pl.pallas_call(
        body, name=name, grid=(t // cc,), in_specs=[row] * 6, out_specs=[row, row],
        out_shape=[jax.ShapeDtypeStruct((t, WB), F32)] * 2,
        scratch_shapes=[pltpu.VMEM((4 * HD_B, LANES), F32)], compiler_params=_params(1))(r, w, k, v, a, b)


def _rwkv_states(sa, w, k, v, b, name):
    t = sa.shape[0]
    cc = min(t, SCAN_CHUNK)

    def body(sa_ref, w_ref, k_ref, v_ref, b_ref, sall_ref, state):
        @pl.when(pl.program_id(0) == 0)
        def _():
            state[...] = jnp.zeros_like(state)

        consts = _scan_consts()

        def prepare(j):
            base = pl.multiple_of(j * 8, 8)
            sa8, w8, k8, v8, b8 = (ref[pl.ds(base, 8), :] for ref in (sa_ref, w_ref, k_ref, v_ref, b_ref))
            sap, vp = _pieces(sa8), _pieces(v8)
            return base, w8, k8, b8, _pair_cols([(x, i) for i in range(8) for x in (sap, vp)], consts)

        def advance(prepared, sv):
            base, w8, k8, b8, cols = prepared
            for i in range(8):
                sv = sv * _pair_rows(w8, i) + cols[2 * i] * _pair_rows(b8, i) + cols[2 * i + 1] * _pair_rows(k8, i)
                sall_ref[base + i] = sv
            return sv

        state[...] = _blocked_loop(cc // 8, prepare, advance, state[...])

    row = pl.BlockSpec((cc, WB), lambda i: (i, 0))
    return pl.pallas_call(
        body, name=name, grid=(t // cc,), in_specs=[row] * 5,
        out_specs=pl.BlockSpec((cc, 4 * HD_B, LANES), lambda i: (i, 0, 0)),
        out_shape=jax.ShapeDtypeStruct((t, 4 * HD_B, LANES), F32),
        scratch_shapes=[pltpu.VMEM((4 * HD_B, LANES), F32)], compiler_params=_params(1))(sa, w, k, v, b)


def _rwkv_scan_bwd(dy, r, w, k, a, b, name):
    t = r.shape[0]
    cc = min(t, SCAN_CHUNK)
    n = t // cc

    def body(dy_ref, r_ref, w_ref, k_ref, a_ref, b_ref, dsa_ref, dv_ref, dstate):
        @pl.when(pl.program_id(0) == 0)
        def _():
            dstate[...] = jnp.zeros_like(dstate)

        consts = _scan_consts()

        steps = range(7, -1, -1)

        def prepare(jj):
            rows = pl.ds(pl.multiple_of((cc // 8 - 1 - jj) * 8, 8), 8)
            dy8, r8, w8, k8, a8, b8 = (ref[rows, :] for ref in (dy_ref, r_ref, w_ref, k_ref, a_ref, b_ref))
            _, upto = _block_products(w8)
            later = jnp.where(_iota(w8.shape, 0) == 7, 1.0, pltpu.roll(upto, 7, 0))
            scaled = [_pieces(x) for x in (r8 / later, b8 * later, k8 * later, a8 / upto)]
            return rows, dy8, _pair_cols([(x, i) for i in steps for x in scaled] + [(_pieces(upto), 0)], consts)

        def advance(prepared, ds):
            rows, dy8, cols = prepared
            dsa8, dv8 = jnp.zeros((8, WB), F32), jnp.zeros((8, WB), F32)
            for n_done, i in enumerate(steps):
                r_c, b_c, k_c, a_c = cols[4 * n_done:4 * n_done + 4]
                ds = ds + r_c * _pair_rows(dy8, i)
                dsa = _pair_sums(ds * b_c)
                dv8 = _put_row(dv8, i, _pair_sums(ds * k_c))
                dsa8 = _put_row(dsa8, i, dsa)
                ds = ds + a_c * _pair_rows(dsa, 0)
            dsa_ref[rows, :] = dsa8
            dv_ref[rows, :] = dv8
            return ds * cols[-1]

        dstate[...] = _blocked_loop(cc // 8, prepare, advance, dstate[...])

    row = pl.BlockSpec((cc, WB), lambda i: (n - 1 - i, 0))
    return pl.pallas_call(
        body, name=name, grid=(n,), in_specs=[row] * 6, out_specs=[row] * 2,
        out_shape=[jax.ShapeDtypeStruct((t, WB), F32)] * 2,
        scratch_shapes=[pltpu.VMEM((4 * HD_B, LANES), F32)], compiler_params=_params(1))(dy, r, w, k, a, b)


def _rwkv_scan_bwd_values(dy, r, w, v, a, sa, dsa, sall, name):
    t = r.shape[0]
    cc = min(t, SCAN_CHUNK)
    n = t // cc

    def body(dy_ref, r_ref, w_ref, v_ref, a_ref, sa_ref, dsa_ref, sall_ref, sprev_ref,
             dr_ref, dw_ref, dk_ref, da_ref, db_ref, dstate):
        @pl.when(pl.program_id(0) == 0)
        def _():
            dstate[...] = jnp.zeros_like(dstate)

        consts = _scan_consts()
        before_chunk = jnp.where(pl.program_id(0) == n - 1, 0.0, 1.0) * sprev_ref[0]

        steps = range(7, -1, -1)

        def prepare(jj):
            j = cc // 8 - 1 - jj
            base = pl.multiple_of(j * 8, 8)
            dy8, r8, w8, v8, a8, sa8, dsa8 = (ref[pl.ds(base, 8), :] for ref in
                                              (dy_ref, r_ref, w_ref, v_ref, a_ref, sa_ref, dsa_ref))
            dyp, vp, sap, dsap = (_pieces(x) for x in (dy8, v8, sa8, dsa8))
            return j, base, r8, w8, a8, _pair_cols([(x, i) for i in steps for x in (dyp, vp, sap, dsap)], consts)

        def advance(prepared, carry):
            ds, sc = carry
            j, base, r8, w8, a8, cols = prepared
            rows = pl.ds(base, 8)
            outs = [jnp.zeros((8, WB), F32) for _ in range(5)]
            for n_done, i in enumerate(steps):
                if i > 0:
                    sp = sall_ref[base + i - 1]
                else:
                    sp = jnp.where(j == 0, before_chunk, sall_ref[jnp.maximum(base - 1, 0)])
                dy_c, v_c, sa_c, dsa_c = cols[4 * n_done:4 * n_done + 4]
                ds = ds + dy_c * _pair_rows(r8, i)
                vals = (_pair_sums(sc * dy_c), _pair_sums(ds * sp), _pair_sums(ds * v_c),
                        _pair_sums(sp * dsa_c), _pair_sums(ds * sa_c))
                outs = [_put_row(o, i, val) for o, val in zip(outs, vals)]
                ds = ds * _pair_rows(w8, i) + dsa_c * _pair_rows(a8, i)
                sc = sp
            for ref, o in zip((dr_ref, dw_ref, dk_ref, da_ref, db_ref), outs):
                ref[rows, :] = o
            return ds, sc

        ds, _ = _blocked_loop(cc // 8, prepare, advance, (dstate[...], sall_ref[cc - 1]))
        dstate[...] = ds

    row = pl.BlockSpec((cc, WB), lambda i: (n - 1 - i, 0))
    return pl.pallas_call(
        body, name=name, grid=(n,),
        in_specs=[row] * 7 + [pl.BlockSpec((cc, 4 * HD_B, LANES), lambda i: (n - 1 - i, 0, 0)),
                              pl.BlockSpec((1, 4 * HD_B, LANES), lambda i: (jnp.maximum((n - 1 - i) * cc - 1, 0), 0, 0))],
        out_specs=[row] * 5, out_shape=[jax.ShapeDtypeStruct((t, WB), F32)] * 5,
        scratch_shapes=[pltpu.VMEM((4 * HD_B, LANES), F32)], compiler_params=_params(1))(
            dy, r, w, v, a, sa, dsa, sall, sall)


def _rwkv_post(y, r, k2, v, g, r_k, gn_w, gn_b, name):
    t = y.shape[0]
    tb = _tile(t, 256)

    def body(y_ref, r_ref, k_ref, v_ref, g_ref, rk_ref, gw_ref, gb_ref, o_ref):
        ones = _head_ones(WB, HD_B)
        yv = y_ref[...]
        yc = yv - _split_dot(yv, ones, 3) * (1.0 / HD_B)
        rstd = lax.rsqrt(_split_dot(yc * yc, ones, 3) * (1.0 / HD_B) + GN_EPS)
        rk = _split_dot(r_ref[...] * k_ref[...] * rk_ref[...], ones, 3)
        o_ref[...] = ((yc * rstd * gw_ref[...] + gb_ref[...] + rk * v_ref[...]) * g_ref[...]).astype(MXU)

    row = pl.BlockSpec((tb, WB), lambda i: (i, 0))
    vec = pl.BlockSpec((1, WB), lambda i: (0, 0))
    return pl.pallas_call(
        body, name=name, grid=(t // tb,), in_specs=[row] * 5 + [vec] * 3, out_specs=row,
        out_shape=jax.ShapeDtypeStruct((t, WB), MXU), compiler_params=_params(1))(y, r, k2, v, g, r_k, gn_w, gn_b)


def _rwkv_post_bwd(dob, y, r, k2, v, g, r_k, gn_w, gn_b, name):
    t = y.shape[0]
    tb = _tile(t, 256)

    def body(do_ref, y_ref, r_ref, k_ref, v_ref, g_ref, rk_ref, gw_ref, gb_ref,
             dy_ref, dg_ref, dr_ref, dk_ref, dv_ref, dgw_ref, dgb_ref, drk_ref):
        @pl.when(pl.program_id(0) == 0)
        def _():
            dgw_ref[...] = jnp.zeros_like(dgw_ref)
            dgb_ref[...] = jnp.zeros_like(dgb_ref)
            drk_ref[...] = jnp.zeros_like(drk_ref)

        ones = _head_ones(WB, HD_B)
        seg = lambda x: _split_dot(x, ones, 3)
        yv, rv, kv, vv, gv = y_ref[...], r_ref[...], k_ref[...], v_ref[...], g_ref[...]
        yc = yv - seg(yv) * (1.0 / HD_B)
        rstd = lax.rsqrt(seg(yc * yc) * (1.0 / HD_B) + GN_EPS)
        yn = yc * rstd
        rk = seg(rv * kv * rk_ref[...])
        dob_v = do_ref[...]
        dg_ref[...] = dob_v * (yn * gw_ref[...] + gb_ref[...] + rk * vv)
        dyg = dob_v * gv
        dgw_ref[...] += jnp.sum(dyg * yn, axis=0, keepdims=True)
        dgb_ref[...] += jnp.sum(dyg, axis=0, keepdims=True)
        dyn = dyg * gw_ref[...]
        dy_ref[...] = rstd * (dyn - (seg(dyn) + yn * seg(dyn * yn)) * (1.0 / HD_B))
        drk = seg(dyg * vv)
        dv_ref[...] = dyg * rk
        dr_ref[...] = drk * kv * rk_ref[...]
        dk_ref[...] = drk * rv * rk_ref[...]
        drk_ref[...] += jnp.sum(drk * rv * kv, axis=0, keepdims=True)

    row = pl.BlockSpec((tb, WB), lambda i: (i, 0))
    vec = pl.BlockSpec((1, WB), lambda i: (0, 0))
    full, small = jax.ShapeDtypeStruct((t, WB), F32), jax.ShapeDtypeStruct((1, WB), F32)
    return pl.pallas_call(
        body, name=name, grid=(t // tb,), in_specs=[row] * 6 + [vec] * 3, out_specs=[row] * 5 + [vec] * 3,
        out_shape=[full] * 5 + [small] * 3, compiler_params=_params(1))(dob, y, r, k2, v, g, r_k, gn_w, gn_b)


def _rwkv_prep_bwd(grads, p_all, mu_pad, w2cat, w0, a0, k_k, k_a, name):
    t = p_all.shape[0]
    tb = _tile(t, 256)

    def body(*refs):
        g_refs, ins, outs = refs[:10], refs[10:27], refs[27:]
        dr_s, dw, dk2_s, dv_s, das, dbs, dg, dr_b, dk2_b, dv_b = (ref[...] for ref in g_refs)
        dr_ref, dk_ref, dv_ref, dlo_ref, dw2_ref, dw0_ref, da0_ref, dkk_ref, dka_ref = outs

        @pl.when(pl.program_id(0) == 0)
        def _():
            for ref in (dw2_ref, dw0_ref, da0_ref, dkk_ref, dka_ref):
                ref[...] = jnp.zeros_like(ref)

        ones = _head_ones(WB, HD_B)
        q = _rwkv_inputs(ins, pl.program_id(0) == 0, ones)
        kk_w, ka_w = ins[15][...], ins[16][...]
        a, kk, k = q["a"], q["kk"], q["k"]
        dk2 = dk2_s + dk2_b
        dkk = dbs * a - das
        da = dbs * kk + dk2 * k * ka_w
        dk = dk2 * (1.0 + (a - 1.0) * ka_w)
        dka_ref[...] += jnp.sum(dk2 * k * (a - 1.0), axis=0, keepdims=True)
        proj = jnp.where(q["nrm"] > L2_EPS, _split_dot(dkk * kk, ones, 3), 0.0)
        dkk0 = (dkk - kk * proj) / q["den"]
        dk = dk + dkk0 * kk_w
        dkk_ref[...] += jnp.sum(dkk0 * k, axis=0, keepdims=True)
        dal = da * a * (1.0 - a)
        da0_ref[...] += jnp.sum(dal, axis=0, keepdims=True)
        sg = q["sg"]
        dwl = dw * q["decay"] * (-DECAY_C) * sg * (1.0 - sg)
        dw0_ref[...] += jnp.sum(dwl, axis=0, keepdims=True)
        dlin = jnp.concatenate([dwl, dal, dg], axis=1).astype(MXU)
        dw2_ref[...] += _dot(q["z"].astype(MXU), dlin, _TN)
        dz = _dot(dlin, ins[12][...], _NT)
        dlo_ref[...] = dz * _lora_dact(q["lo"])
        dr_ref[...] = dr_s + dr_b
        dk_ref[...] = dk
        dv_ref[...] = dv_s + dv_b

    row = pl.BlockSpec((tb, WB), lambda i: (i, 0))
    vec = pl.BlockSpec((1, WB), lambda i: (0, 0))
    full, small = jax.ShapeDtypeStruct((t, WB), F32), jax.ShapeDtypeStruct((1, WB), F32)
    return pl.pallas_call(
        body, name=name, grid=(t // tb,), in_specs=[row] * 10 + _rwkv_in_specs(t, tb),
        out_specs=[row] * 3 + [pl.BlockSpec((tb, 256), lambda i: (i, 0)),
                               pl.BlockSpec((256, 3 * WB), lambda i: (0, 0))] + [vec] * 4,
        out_shape=[full] * 3 + [jax.ShapeDtypeStruct((t, 256), F32), jax.ShapeDtypeStruct((256, 3 * WB), F32)]
        + [small] * 4, compiler_params=_params(1))(*grads, *_rwkv_in_args(p_all, mu_pad, w2cat, w0, a0, k_k, k_a))


def _shift_bwd(dshifted, p_all, mu_pad, name):
    t = p_all.shape[0]
    tb = _tile(t, 256)
    nt, nt8 = t // tb, tb // 8
    widths, cols, mus = (WB, WB, WB, 256), (COL_R, COL_K, COL_V, COL_L), (0, 1, 2, 6)

    def body(*refs):
        d_refs, n_refs, p_refs, q_refs, m_refs = refs[0:4], refs[4:8], refs[8:12], refs[12:16], refs[16:20]
        o_refs, dmu_refs = refs[20:24], refs[24:28]
        i = pl.program_id(0)

        @pl.when(i == 0)
        def _():
            for ref in dmu_refs:
                ref[...] = jnp.zeros_like(ref)

        for d, nx, p, q, m, o, dmu in zip(d_refs, n_refs, p_refs, q_refs, m_refs, o_refs, dmu_refs):
            dv, pv, mu = d[...], p[...], m[...]
            o[...] = (dv * (1.0 - mu) + mu * _shift_up(dv, nx[...], i == nt - 1)).astype(MXU)
            dmu[...] += jnp.sum(dv * (_shift_down(pv, q[...], i == 0) - pv), axis=0, keepdims=True)

    cur_d = [pl.BlockSpec((tb, w), lambda i: (i, 0)) for w in widths]
    next_d = [pl.BlockSpec((8, w), lambda i: (jnp.minimum((i + 1) * nt8, t // 8 - 1), 0)) for w in widths]
    cur_p = [pl.BlockSpec((tb, w), lambda i, j=j: (i, j)) for w, j in zip(widths, cols)]
    prev_p = [pl.BlockSpec((8, w), lambda i, j=j: (jnp.maximum(i * nt8 - 1, 0), j)) for w, j in zip(widths, cols)]
    mu_s = [pl.BlockSpec((1, w), lambda i, j=j: (0, j)) for w, j in zip(widths, mus)]
    vecs = [pl.BlockSpec((1, w), lambda i: (0, 0)) for w in widths]
    return pl.pallas_call(
        body, name=name, grid=(nt,), in_specs=cur_d + next_d + cur_p + prev_p + mu_s, out_specs=cur_d + vecs,
        out_shape=[jax.ShapeDtypeStruct((t, w), MXU) for w in widths]
        + [jax.ShapeDtypeStruct((1, w), F32) for w in widths],
        compiler_params=_params(1))(*dshifted, *dshifted, *(p_all,) * 8, *(mu_pad,) * 4)


def _peer(k):
    x, y, c = (lax.axis_index(n) for n in AXES)
    px = 1 - x if k & 4 else x
    py = 1 - y if k & 2 else y
    pc = 1 - c if k & 1 else c
    return (px, py, pc), 4 * px + 2 * py + pc


def _exchange_copy(src_refs, land_refs, send_sems, recv_sems, per_peer, j, k, arriving):
    _, me = _peer(0)
    peer, idx = _peer(k)
    sem = j * (N_DEV - 1) + k - 1
    return pltpu.make_async_remote_copy(
        src_ref=src_refs[j].at[idx] if per_peer[j] else src_refs[j],
        dst_ref=land_refs[j].at[idx if arriving else me],
        send_sem=send_sems.at[sem], recv_sem=recv_sems.at[sem],
        device_id=peer, device_id_type=pl.DeviceIdType.MESH)


def _exchange_start(srcs, per_peer, name, after=()):
    n = len(srcs)
    shapes = [tuple(s.shape[1:]) if pp else tuple(s.shape) for s, pp in zip(srcs, per_peer)]
    pairs = [(j, k) for k in range(1, N_DEV) for j in range(n)]
    first_out = 2 * n + len(after)

    def body(*refs):
        src_refs, land_refs, (send_sems, recv_sems), token = refs[:n], refs[n:2 * n], refs[first_out:first_out + 2], refs[-1]
        for j, k in pairs:
            _exchange_copy(src_refs, land_refs, send_sems, recv_sems, per_peer, j, k, False).start()
        token[...] = jnp.zeros_like(token)

    hbm, sem = pl.BlockSpec(memory_space=pltpu.HBM), pl.BlockSpec(memory_space=pltpu.SEMAPHORE)
    lands = [lax.empty((N_DEV,) + shp, s.dtype) for shp, s in zip(shapes, srcs)]
    operands = [pltpu.with_memory_space_constraint(a, pltpu.HBM) for a in list(srcs) + lands]
    n_sems = n * (N_DEV - 1)
    out = pl.pallas_call(
        body, name=name, in_specs=[hbm] * (2 * n) + [pl.BlockSpec(memory_space=pl.ANY)] * len(after),
        out_specs=[sem, sem] + [hbm] * (2 * n) + [pl.BlockSpec(memory_space=pltpu.VMEM)],
        out_shape=[pltpu.SemaphoreType.DMA((n_sems,)), pltpu.SemaphoreType.DMA((n_sems,))]
        + [pltpu.HBM(a.shape, a.dtype) for a in operands] + [jax.ShapeDtypeStruct((8, LANES), F32)],
        input_output_aliases={j: 2 + j for j in range(2 * n)},
        compiler_params=pltpu.CompilerParams(has_side_effects=pltpu.SideEffectType.DATAFLOW_SIDE_EFFECTING))(
            *operands, *after)
    return (out[0], out[1], out[2:2 + n], out[2 + n:2 + 2 * n], per_peer), out[-1]


def _exchange_wait(handle, after, name):
    send_sems, recv_sems, srcs, lands, per_peer = handle
    n = len(srcs)
    pairs = [(j, k) for k in range(1, N_DEV) for j in range(n)]

    def body(*refs):
        src_refs, land_refs, (send_sems, recv_sems) = refs[:n], refs[n:2 * n], refs[2 * n:2 * n + 2]
        for j, k in pairs:
            _exchange_copy(src_refs, land_refs, send_sems, recv_sems, per_peer, j, k, False).wait_send()
            _exchange_copy(src_refs, land_refs, send_sems, recv_sems, per_peer, j, k, True).wait_recv()

    hbm, sem = pl.BlockSpec(memory_space=pltpu.HBM), pl.BlockSpec(memory_space=pltpu.SEMAPHORE)
    out = pl.pallas_call(
        body, name=name, in_specs=[hbm] * (2 * n) + [sem, sem, pl.BlockSpec(memory_space=pl.ANY)],
        out_specs=[hbm] * (2 * n), out_shape=[pltpu.HBM(a.shape, a.dtype) for a in list(srcs) + list(lands)],
        input_output_aliases={j: j for j in range(2 * n)},
        compiler_params=pltpu.CompilerParams(has_side_effects=pltpu.SideEffectType.DATAFLOW_SIDE_EFFECTING))(
            *srcs, *lands, send_sems, recv_sems, after)
    return out[n:]


def _adam_update(g, w, m, v):
    c1, c2 = 1.0 - ADAM_B1 ** ADAM_STEP, 1.0 - ADAM_B2 ** ADAM_STEP
    nm = ADAM_B1 * m + (1.0 - ADAM_B1) * g
    nv = ADAM_B2 * v + (1.0 - ADAM_B2) * (g * g)
    return -ADAM_LR * ((nm / c1) / (jnp.sqrt(nv / c2) + ADAM_EPS) + ADAM_WD * w), nm, nv


def _row_tile(rows, cols):
    padded = -(-cols // LANES) * LANES
    cap = max(16, ADAM_BLOCK_BYTES // (N_DEV * padded * 4))
    best = 16
    for t in range(16, min(rows, cap) + 1, 16):
        if rows % t == 0:
            best = t
    return best


def _adamw(parts, w, m, v, name):
    _, rows, cols = w.shape
    tb = _row_tile(rows, cols)

    def body(p_ref, w_ref, m_ref, v_ref, g_ref, d_ref, nm_ref, nv_ref):
        g = p_ref[0].astype(F32)
        for d in range(1, N_DEV):
            g = g + p_ref[d].astype(F32)
        g_ref[0] = g
        d_ref[0], nm_ref[0], nv_ref[0] = _adam_update(g, w_ref[0], m_ref[0], v_ref[0])

    row = pl.BlockSpec((1, tb, cols), lambda i: (0, i, 0))
    out = jax.ShapeDtypeStruct(w.shape, F32)
    return pl.pallas_call(
        body, name=name, grid=(rows // tb,),
        in_specs=[pl.BlockSpec((N_DEV, tb, cols), lambda i: (0, i, 0)), row, row, row], out_specs=[row] * 4,
        out_shape=[out] * 4, compiler_params=_params(1))(parts, w, m, v)


def _adamw_small(parts, ws, ms, vs, name):
    n = len(ws)

    def body(*refs):
        p_ref = refs[0]
        w_refs, m_refs, v_refs = refs[1:1 + n], refs[1 + n:1 + 2 * n], refs[1 + 2 * n:1 + 3 * n]
        outs = refs[1 + 3 * n:]
        base = 0
        for j in range(n):
            rows, cols = ws[j].shape
            size = rows * cols
            for ch in range(-(-size // LANES)):
                r, c0 = divmod(ch * LANES, cols)
                width = min(LANES, cols - c0)
                g = p_ref[0, base + ch:base + ch + 1, 0:width]
                for d in range(1, N_DEV):
                    g = g + p_ref[d, base + ch:base + ch + 1, 0:width]
                at = (slice(r, r + 1), slice(c0, c0 + width))
                delta, nm, nv = _adam_update(g, w_refs[j][at], m_refs[j][at], v_refs[j][at])
                for out, val in zip((outs[j], outs[n + j], outs[2 * n + j], outs[3 * n + j]), (g, delta, nm, nv)):
                    out[at] = val
            base += -(-size // (8 * LANES)) * 8

    vmem = pl.BlockSpec(memory_space=pltpu.VMEM)
    res = pl.pallas_call(
        body, name=name, in_specs=[vmem] * (1 + 3 * n), out_specs=[vmem] * (4 * n),
        out_shape=[jax.ShapeDtypeStruct(a.shape, F32) for a in ws] * 4)(parts, *ws, *ms, *vs)
    return res[:n], res[n:2 * n], res[2 * n:3 * n], res[3 * n:]


def _rows(a, multiple):
    flat = a.reshape(-1)
    pad = -flat.shape[0] % (multiple * LANES)
    if pad:
        flat = jnp.concatenate([flat, jnp.zeros((pad,), a.dtype)])
    return flat.reshape(-1, LANES)


def _pack(arrs, multiple):
    return jnp.concatenate([_rows(a, multiple) for a in arrs], axis=0)


def _gathered_to_full(g, name, shard_shape):
    g = g.reshape((N_DEV,) + shard_shape)
    if name in COL_SHARDED:
        return jnp.transpose(g, (1, 0, 2)).reshape(shard_shape[0], N_DEV * shard_shape[1])
    return g.reshape(N_DEV * shard_shape[0], shard_shape[1])


def _full_to_per_device(full, name):
    if name in COL_SHARDED:
        r, c = full.shape
        return jnp.transpose(full.reshape(r, N_DEV, c // N_DEV), (1, 0, 2))
    return full.reshape(N_DEV, full.shape[0] // N_DEV, full.shape[1])


def _w2cat(w2, a2, g2):
    n_w, n_a, n_g = LORA
    out = jnp.zeros((256, 3 * WB), w2.dtype)
    out = out.at[0:n_w, 0:WB].set(w2)
    out = out.at[n_w:n_w + n_a, WB:2 * WB].set(a2)
    return out.at[n_w + n_a:n_w + n_a + n_g, 2 * WB:].set(g2)


class _Local:
    def __init__(self, w):
        self.w = w

    def weights(self, group, after=None):
        return self.w

    def started(self):
        return ()

    def send(self, grads, names):
        return ()


class _Overlapped:
    GROUPS = {"ffn1": ("ffn1_w_gate", "ffn1_w_up", "ffn1_w_down"),
              "mixer_in": ("w_in", "rwkv_w2", "rwkv_a2", "rwkv_g2"),
              "late": ("w_out", "ffn2_w_gate", "ffn2_w_up", "ffn2_w_down")}

    def __init__(self, wts):
        x, y, c = (lax.axis_index(n) for n in AXES)
        self.wts, self.me, self.gathers, self.sends = wts, 4 * x + 2 * y + c, {}, []
        self._gather("ffn1", ())

    def _gather(self, group, after):
        names = self.GROUPS[group]
        shards = [self.wts[n].astype(MXU) for n in names]
        handle, token = _exchange_start(shards, [False] * len(names), "gather_" + group, after)
        self.gathers[group] = (names, shards, handle, token)
        self.newest = token

    def started(self):
        return (self.newest,)

    def _own_slot(self, land, mine):
        return lax.dynamic_update_slice(land, mine[None], (self.me,) + (0,) * mine.ndim)

    def weights(self, group, after=None):
        names, shards, handle, token = self.gathers[group]
        lands = _exchange_wait(handle, token if after is None else after, "gathered_" + group)
        w = {n: _gathered_to_full(self._own_slot(land, own), n, own.shape[1:])
             for n, own, land in zip(names, shards, lands)}
        order = list(self.GROUPS)
        if group != order[-1]:
            self._gather(order[order.index(group) + 1], (w[names[0]],))
        if group == "ffn1":
            for n in SMALL:
                keep = n in ("hgrn_lb_logits", "rwkv_r_k", "final_norm")
                w[n] = self.wts[n] if keep else self.wts[n].reshape(1, -1)
        return w

    def send(self, grads, names, small=None):
        contrib = [_full_to_per_device(grads[n], n).astype(WIRE) for n in names]
        per_peer = [True] * len(names)
        if small is not None:
            names, contrib, per_peer = names + ("small",), contrib + [small], per_peer + [False]
        handle, token = _exchange_start(contrib, per_peer, "scatter_" + names[0])
        self.sends.append((names, contrib, per_peer, handle))
        self.last_token = token
        return (token,)

    def received(self, which, after):
        names, contrib, per_peer, handle = self.sends[which]
        lands = _exchange_wait(handle, after, "scattered_" + names[0])
        parts = {}
        for n, own, pp, land in zip(names, contrib, per_peer, lands):
            mine = lax.dynamic_index_in_dim(own, self.me, 0, keepdims=False) if pp else own
            parts[n] = self._own_slot(land, mine)
        return parts


def _local_step(x, target, net):
    n_w, n_a, n_g = LORA
    w = dict(net.weights("ffn1"))
    h1 = _rms_fwd(x, w["ffn1_norm"], "ffn1_norm")
    x1 = _ffn_fwd(x, h1, w["ffn1_w_gate"], w["ffn1_w_up"], w["ffn1_w_down"], "ffn1_fwd", after=net.started())
    w.update(net.weights("mixer_in", x1))
    w_in_pad = jnp.pad(w["w_in"], ((0, 0), (0, N_INP - N_IN)))
    mu_pad = jnp.pad(w["rwkv_shift_mu"], ((0, 0), (0, 1792 - 1696)))
    w2cat = _w2cat(w["rwkv_w2"], w["rwkv_a2"], w["rwkv_g2"])
    r_k = w["rwkv_r_k"].reshape(1, WB)
    rw = (mu_pad, w2cat, w["rwkv_w0"], w["rwkv_a0"], w["rwkv_k_k"], w["rwkv_k_a"])

    h2 = _rms_fwd(x1, w["mix_norm"], "mix_norm")
    p_all = _matmul(h2, w_in_pad, after=net.started(), name="in_proj")
    oa, oraw, states = _hgrn_fwd(p_all, w["hgrn_lb_logits"], w["hgrn_out_norm"], "hgrn_fwd")
    r, decay, k2, v, sa, sb, g = _rwkv_prep(p_all, *rw, "rwkv_prep")
    y, s_a = _rwkv_scan_fwd(r, decay, k2, v, sa, sb, "rwkv_scan_fwd")
    sall = _rwkv_states(s_a, decay, k2, v, sb, "rwkv_states")
    post_w = (r_k, w["rwkv_gn_w"], w["rwkv_gn_b"])
    ob = _rwkv_post(y, r, k2, v, g, *post_w, "rwkv_post")
    w.update(net.weights("late", ob))
    o = jnp.concatenate([oa, ob], axis=1)
    x2 = _matmul(o, w["w_out"], res=x1, name="out_proj")
    h3 = _rms_fwd(x2, w["ffn2_norm"], "ffn2_norm")
    x3 = _ffn_fwd(x2, h3, w["ffn2_w_gate"], w["ffn2_w_up"], w["ffn2_w_down"], "ffn2_fwd")
    loss, dx3, d_final = _loss_head(x3, w["final_norm"].reshape(1, D), target, "loss_head")

    grads = {"final_norm": d_final.reshape(D)}

    def ffn_back(prefix, h, dy, x_in, norm):
        wg, wu, wd = (w[prefix + s] for s in ("_w_gate", "_w_up", "_w_down"))
        act, dgate, dup, dout = _ffn_bwd(h, dy, wg, wu, wd, prefix + "_bwd")
        dh = _matmul(dup, wu, tb=True, res=_matmul(dgate, wg, tb=True, name=prefix + "_dh_gate"), name=prefix + "_dh")
        sent = ()
        for which, a_op, b_op in (("_w_gate", h, dgate), ("_w_up", h, dup), ("_w_down", act, dout)):
            grads[prefix + which] = _matmul(a_op, b_op, ta=True, out_dtype=WIRE, after=sent, name=prefix + "_d" + which)
            sent = net.send(grads, (prefix + which,))
        dx, grads[prefix + "_norm"] = _rms_bwd(x_in, norm, dh, dy, prefix + "_norm_bwd", after=sent)
        return dx

    dx2 = ffn_back("ffn2", h3, dx3, x2, w["ffn2_norm"])
    grads["w_out"] = _matmul(o, dx2, ta=True, out_dtype=WIRE, name="d_w_out")
    sent = net.send(grads, ("w_out",))
    do = _matmul(dx2, w["w_out"], tb=True, after=sent, name="d_mixed")
    dqa, dfa, dia, dga, grads["hgrn_out_norm"], grads["hgrn_lb_logits"] = _hgrn_bwd(
        p_all, w["hgrn_lb_logits"], w["hgrn_out_norm"], oraw, states, do[:, :WA], "hgrn_bwd")
    dy, dg, dr_b, dk2_b, dv_b, grads["rwkv_gn_w"], grads["rwkv_gn_b"], d_rk = _rwkv_post_bwd(
        do[:, WA:], y, r, k2, v, g, *post_w, "rwkv_post_bwd")
    grads["rwkv_r_k"] = d_rk.reshape(w["rwkv_r_k"].shape)
    d_sa, dv = _rwkv_scan_bwd(dy, r, decay, k2, sa, sb, "rwkv_scan_bwd")
    dr, dw, dk2, dsa, dsb = _rwkv_scan_bwd_values(dy, r, decay, v, sa, s_a, d_sa, sall, "rwkv_scan_bwd_values")
    (dsr, dsk, dsv, dslo, dw2cat, grads["rwkv_w0"], grads["rwkv_a0"], grads["rwkv_k_k"],
     grads["rwkv_k_a"]) = _rwkv_prep_bwd((dr, dw, dk2, dv, dsa, dsb, dg, dr_b, dk2_b, dv_b), p_all, *rw,
                                         "rwkv_prep_bwd")
    grads["rwkv_w2"] = dw2cat[0:n_w, 0:WB]
    grads["rwkv_a2"] = dw2cat[n_w:n_w + n_a, WB:2 * WB]
    grads["rwkv_g2"] = dw2cat[n_w + n_a:n_w + n_a + n_g, 2 * WB:]
    dpr, dpk, dpv, dplo, dmu_r, dmu_k, dmu_v, dmu_lo = _shift_bwd((dsr, dsk, dsv, dslo), p_all, mu_pad, "shift_bwd")
    grads["rwkv_shift_mu"] = jnp.concatenate([dmu_r, dmu_k, dmu_v, dmu_lo], axis=1)[:, :1696]
    dp = jnp.concatenate([dqa, dfa, dia, dga, dpr, dpk, dpv, dplo], axis=1)
    grads["w_in"] = _matmul(h2, dp, ta=True, out_dtype=WIRE, name="d_w_in")[:, :N_IN]
    sent = net.send(grads, ("w_in", "rwkv_w2", "rwkv_a2", "rwkv_g2"))
    dh2 = _matmul(dp, w_in_pad, tb=True, after=sent, name="d_h2")
    dx1, grads["mix_norm"] = _rms_bwd(x1, w["mix_norm"], dh2, dx2, "mix_norm_bwd")
    dx0 = ffn_back("ffn1", h1, dx1, x, w["ffn1_norm"])
    return loss[0, 0], dx0, grads


def kernel(x, ffn1_norm, ffn1_w_gate, ffn1_w_up, ffn1_w_down, mix_norm, w_in, hgrn_lb_logits, hgrn_out_norm, rwkv_shift_mu, rwkv_w0, rwkv_w2, rwkv_a0, rwkv_a2, rwkv_g2, rwkv_k_k, rwkv_k_a, rwkv_r_k, rwkv_gn_w, rwkv_gn_b, w_out, ffn2_norm, ffn2_w_gate, ffn2_w_up, ffn2_w_down, final_norm, loss_target, m_ffn1_norm, m_ffn1_w_gate, m_ffn1_w_up, m_ffn1_w_down, m_mix_norm, m_w_in, m_hgrn_lb_logits, m_hgrn_out_norm, m_rwkv_shift_mu, m_rwkv_w0, m_rwkv_w2, m_rwkv_a0, m_rwkv_a2, m_rwkv_g2, m_rwkv_k_k, m_rwkv_k_a, m_rwkv_r_k, m_rwkv_gn_w, m_rwkv_gn_b, m_w_out, m_ffn2_norm, m_ffn2_w_gate, m_ffn2_w_up, m_ffn2_w_down, m_final_norm, v_ffn1_norm, v_ffn1_w_gate, v_ffn1_w_up, v_ffn1_w_down, v_mix_norm, v_w_in, v_hgrn_lb_logits, v_hgrn_out_norm, v_rwkv_shift_mu, v_rwkv_w0, v_rwkv_w2, v_rwkv_a0, v_rwkv_a2, v_rwkv_g2, v_rwkv_k_k, v_rwkv_k_a, v_rwkv_r_k, v_rwkv_gn_w, v_rwkv_gn_b, v_w_out, v_ffn2_norm, v_ffn2_w_gate, v_ffn2_w_up, v_ffn2_w_down, v_final_norm):
    args = dict(locals())
    wts = {n: args[n] for n in WEIGHTS}
    mom = {n: args["m_" + n] for n in WEIGHTS}
    var = {n: args["v_" + n] for n in WEIGHTS}
    net = _Overlapped(wts)
    loss, grad_x, grads = _local_step(x[0], loss_target[0], net)
    loss = lax.psum(loss, AXES)
    after, = net.send(grads, (), _pack([grads[n] for n in SMALL], 8))

    new = {}
    two_d = lambda a: a if a.ndim == 2 else a.reshape(1, -1)
    for which in range(len(net.sends)):
        for n, part in net.received(which, after).items():
            if n == "small":
                small = _adamw_small(part, *([two_d(src[k]) for k in SMALL] for src in (wts, mom, var)), "adamw_small")
                for j, k in enumerate(SMALL):
                    new[k] = [res[j].reshape(wts[k].shape) for res in small]
            else:
                new[n] = _adamw(part, wts[n], mom[n], var[n], "adamw_" + n)
                after = new[n][1]
    return (loss, grad_x[None], *[new[n][0] for n in WEIGHTS], *[new[n][1] for n in WEIGHTS],
            *[new[n][2] for n in WEIGHTS], *[new[n][3] for n in WEIGHTS])
```

```python
import functools
import math

import jax
import jax.numpy as jnp
from jax import lax
from jax.experimental import pallas as pl
from jax.experimental.pallas import tpu as pltpu

F32 = jnp.float32
MXU = jnp.bfloat16
WIRE = jnp.bfloat16
D = 1024
FF = 2816
WA = 512
WB = 512
HD_B = 64
N_IN = 3744
N_INP = 3840
COL_R, COL_K, COL_V = 4, 5, 6
COL_L = 14
LORA = (32, 32, 96)
HG_CHUNK = 64
HG_STEP_CHUNKS = 4
SCAN_CHUNK = 64
SCAN_UNROLL = 4
NORM_EPS = 1e-6
GN_EPS = 64e-5
L2_EPS = 1e-12
DECAY_C = math.exp(-0.5)
N_DEV = 8
LANES = 128
ADAM_BLOCK_BYTES = 4 * 1024 * 1024
MATMUL_BLOCK_BYTES = 40 * 1024 * 1024
VMEM_LIMIT = 56 * 1024 * 1024
ADAM_LR, ADAM_B1, ADAM_B2, ADAM_EPS, ADAM_WD, ADAM_STEP = 0.001, 0.9, 0.999, 1e-08, 0.01, 10
AXES = ("x", "y", "c")

SHARDED = ("ffn1_w_gate", "ffn1_w_up", "ffn1_w_down", "w_in", "rwkv_w2", "rwkv_a2", "rwkv_g2", "w_out",
           "ffn2_w_gate", "ffn2_w_up", "ffn2_w_down")
COL_SHARDED = {"ffn1_w_gate", "ffn1_w_up", "w_in", "rwkv_w2", "rwkv_a2", "rwkv_g2", "ffn2_w_gate", "ffn2_w_up"}
SMALL = ("ffn1_norm", "mix_norm", "hgrn_lb_logits", "hgrn_out_norm", "rwkv_shift_mu", "rwkv_w0", "rwkv_a0",
         "rwkv_k_k", "rwkv_k_a", "rwkv_r_k", "rwkv_gn_w", "rwkv_gn_b", "ffn2_norm", "final_norm")
WEIGHTS = ("ffn1_norm", "ffn1_w_gate", "ffn1_w_up", "ffn1_w_down", "mix_norm", "w_in", "hgrn_lb_logits",
           "hgrn_out_norm", "rwkv_shift_mu", "rwkv_w0", "rwkv_w2", "rwkv_a0", "rwkv_a2", "rwkv_g2", "rwkv_k_k",
           "rwkv_k_a", "rwkv_r_k", "rwkv_gn_w", "rwkv_gn_b", "w_out", "ffn2_norm", "ffn2_w_gate", "ffn2_w_up",
           "ffn2_w_down", "final_norm")


def _tile(n, cap):
    if n <= cap:
        return n
    for t in range(cap - cap % LANES, 0, -LANES):
        if n % t == 0:
            return t
    raise ValueError((n, cap))


def _params(n_axes):
    return pltpu.CompilerParams(dimension_semantics=("arbitrary",) * n_axes, vmem_limit_bytes=VMEM_LIMIT)


def _sig(x):
    return jax.nn.sigmoid(x)


def _dsilu(z, s):
    return s * (1.0 + z * (1.0 - s))


def _dot(a, b, dims=((1,), (0,)), precision=None):
    return lax.dot_general(a, b, (dims, ((), ())), preferred_element_type=F32, precision=precision)


_NT = ((1,), (1,))
_TN = ((0,), (0,))
_HI = lax.Precision.HIGH


def _iota(shape, dim):
    return lax.broadcasted_iota(jnp.int32, shape, dim)


def _split_dot(x, ones, passes):
    hi = x.astype(jnp.bfloat16)
    acc = _dot(hi, ones)
    rem = x
    for _ in range(passes - 1):
        rem = rem - hi.astype(F32)
        hi = rem.astype(jnp.bfloat16)
        acc = acc + _dot(hi, ones)
    return acc


def _head_ones(n, width):
    shift = width.bit_length() - 1
    return (_iota((n, n), 0) >> shift == _iota((n, n), 1) >> shift).astype(jnp.bfloat16)


def _matmul(a, b, *, ta=False, tb=False, out_dtype=F32, res=None, after=(), name):
    m, k = (a.shape[1], a.shape[0]) if ta else a.shape
    n = b.shape[0] if tb else b.shape[1]
    tm, tn = _tile(m, 1408), _tile(n, 1408)
    in_bytes = max(a.dtype.itemsize, b.dtype.itemsize)
    for tk in (_tile(k, 1408), _tile(k, 1024), _tile(k, 512), _tile(k, 256)):
        if 2 * (tm + tn) * tk * in_bytes + 3 * tm * tn * 4 <= MATMUL_BLOCK_BYTES:
            break
    nk = k // tk
    dims = ((0 if ta else 1,), (1 if tb else 0,))

    def body(*refs):
        a_ref, b_ref = refs[:2]
        o_ref, acc = refs[-2:]
        kk = pl.program_id(2)

        @pl.when(kk == 0)
        def _():
            acc[...] = jnp.zeros_like(acc)

        acc[...] += _dot(a_ref[...].astype(MXU), b_ref[...].astype(MXU), dims)

        @pl.when(kk == nk - 1)
        def _():
            v = acc[...]
            if res is not None:
                v = v + refs[2][...]
            o_ref[...] = v.astype(out_dtype)

    a_spec = pl.BlockSpec((tk, tm), lambda i, j, kk: (kk, i)) if ta else pl.BlockSpec((tm, tk), lambda i, j, kk: (i, kk))
    b_spec = pl.BlockSpec((tn, tk), lambda i, j, kk: (j, kk)) if tb else pl.BlockSpec((tk, tn), lambda i, j, kk: (kk, j))
    o_spec = pl.BlockSpec((tm, tn), lambda i, j, kk: (i, j))
    ins, specs = [a, b], [a_spec, b_spec]
    if res is not None:
        ins.append(res)
        specs.append(o_spec)
    ins += list(after)
    specs += [pl.BlockSpec(memory_space=pl.ANY)] * len(after)
    return pl.pallas_call(
        body, name=name, grid=(m // tm, n // tn, nk), in_specs=specs, out_specs=o_spec,
        out_shape=jax.ShapeDtypeStruct((m, n), out_dtype), scratch_shapes=[pltpu.VMEM((tm, tn), F32)],
        compiler_params=_params(3))(*ins)


def _rms_fwd(x, g, name):
    t = x.shape[0]
    tb = _tile(t, 512)

    def body(x_ref, g_ref, o_ref):
        xv = x_ref[...]
        rinv = lax.rsqrt(jnp.mean(xv * xv, axis=-1, keepdims=True) + NORM_EPS)
        o_ref[...] = (xv * rinv * g_ref[...]).astype(MXU)

    return pl.pallas_call(
        body, name=name, grid=(t // tb,),
        in_specs=[pl.BlockSpec((tb, D), lambda i: (i, 0)), pl.BlockSpec((1, D), lambda i: (0, 0))],
        out_specs=pl.BlockSpec((tb, D), lambda i: (i, 0)), out_shape=jax.ShapeDtypeStruct((t, D), MXU),
        compiler_params=_params(1))(x, g)


def _rms_bwd(x, g, dh, dres, name, after=()):
    t = x.shape[0]
    tb = _tile(t, 512)

    def body(x_ref, g_ref, dh_ref, dres_ref, *rest):
        dx_ref, dg_ref = rest[-2:]

        @pl.when(pl.program_id(0) == 0)
        def _():
            dg_ref[...] = jnp.zeros_like(dg_ref)

        xv = x_ref[...]
        rinv = lax.rsqrt(jnp.mean(xv * xv, axis=-1, keepdims=True) + NORM_EPS)
        xhat = xv * rinv
        dhv = dh_ref[...]
        dg_ref[...] += jnp.sum(dhv * xhat, axis=0, keepdims=True)
        dxhat = dhv * g_ref[...]
        dx_ref[...] = dres_ref[...] + rinv * (dxhat - xhat * jnp.mean(dxhat * xhat, axis=-1, keepdims=True))

    row = pl.BlockSpec((tb, D), lambda i: (i, 0))
    vec = pl.BlockSpec((1, D), lambda i: (0, 0))
    return pl.pallas_call(
        body, name=name, grid=(t // tb,),
        in_specs=[row, vec, row, row] + [pl.BlockSpec(memory_space=pl.ANY)] * len(after), out_specs=[row, vec],
        out_shape=[jax.ShapeDtypeStruct((t, D), F32), jax.ShapeDtypeStruct((1, D), F32)],
        compiler_params=_params(1))(x, g, dh, dres, *after)


def _loss_head(x, g, target, name):
    t = x.shape[0]
    tb = _tile(t, 512)

    def body(x_ref, g_ref, t_ref, loss_ref, dx_ref, dg_ref):
        @pl.when(pl.program_id(0) == 0)
        def _():
            dg_ref[...] = jnp.zeros_like(dg_ref)
            loss_ref[...] = jnp.zeros_like(loss_ref)

        xv = x_ref[...]
        gv = g_ref[...]
        rinv = lax.rsqrt(jnp.mean(xv * xv, axis=-1, keepdims=True) + NORM_EPS)
        xhat = xv * rinv
        err = xhat * gv - t_ref[...]
        per_tok = jnp.mean(err * err, axis=-1, keepdims=True)
        loss_ref[...] += jnp.broadcast_to(0.5 * jnp.sum(per_tok, axis=0, keepdims=True), loss_ref.shape)
        dy = err * (1.0 / D)
        dg_ref[...] += jnp.sum(dy * xhat, axis=0, keepdims=True)
        dxhat = dy * gv
        dx_ref[...] = rinv * (dxhat - xhat * jnp.mean(dxhat * xhat, axis=-1, keepdims=True))

    row = pl.BlockSpec((tb, D), lambda i: (i, 0))
    vec = pl.BlockSpec((1, D), lambda i: (0, 0))
    return pl.pallas_call(
        body, name=name, grid=(t // tb,), in_specs=[row, vec, row],
        out_specs=[pl.BlockSpec((1, LANES), lambda i: (0, 0)), row, vec],
        out_shape=[jax.ShapeDtypeStruct((1, LANES), F32), jax.ShapeDtypeStruct((t, D), F32),
                   jax.ShapeDtypeStruct((1, D), F32)],
        compiler_params=_params(1))(x, g, target)


def _ffn_fwd(x, h, wg, wu, wd, name, after=()):
    t = x.shape[0]
    tb, fb = _tile(t, 1024), 256
    nf = FF // fb

    def body(x_ref, h_ref, wg_ref, wu_ref, wd_ref, *rest):
        o_ref, acc = rest[-2:]
        f = pl.program_id(1)

        @pl.when(f == 0)
        def _():
            acc[...] = jnp.zeros_like(acc)

        hv = h_ref[...]
        gate = _dot(hv, wg_ref[...])
        up = _dot(hv, wu_ref[...])
        act = (gate * _sig(gate) * up).astype(MXU)
        acc[...] += _dot(act, wd_ref[...])

        @pl.when(f == nf - 1)
        def _():
            o_ref[...] = x_ref[...] + 0.5 * acc[...]

    row = pl.BlockSpec((tb, D), lambda i, f: (i, 0))
    col = pl.BlockSpec((D, fb), lambda i, f: (0, f))
    return pl.pallas_call(
        body, name=name, grid=(t // tb, nf),
        in_specs=[row, row, col, col, pl.BlockSpec((fb, D), lambda i, f: (f, 0))]
        + [pl.BlockSpec(memory_space=pl.ANY)] * len(after), out_specs=row,
        out_shape=jax.ShapeDtypeStruct((t, D), F32), scratch_shapes=[pltpu.VMEM((tb, D), F32)],
        compiler_params=_params(2))(x, h, wg, wu, wd, *after)


def _ffn_bwd(h, dy, wg, wu, wd, name):
    t = h.shape[0]
    tb, fb = _tile(t, 1024), 256
    nf = FF // fb

    def body(h_ref, dy_ref, wg_ref, wu_ref, wd_ref, act_ref, dg_ref, du_ref, dout_ref):
        hv = h_ref[...]
        dout = (0.5 * dy_ref[...]).astype(MXU)
        dout_ref[...] = dout
        gate = _dot(hv, wg_ref[...])
        up = _dot(hv, wu_ref[...])
        dact = _dot(dout, wd_ref[...], _NT)
        s = _sig(gate)
        silu = gate * s
        act_ref[...] = (silu * up).astype(MXU)
        du_ref[...] = (dact * silu).astype(MXU)
        dg_ref[...] = (dact * up * _dsilu(gate, s)).astype(MXU)

    row = pl.BlockSpec((tb, D), lambda i, f: (i, 0))
    col = pl.BlockSpec((D, fb), lambda i, f: (0, f))
    hid = pl.BlockSpec((tb, fb), lambda i, f: (i, f))
    hid_shape = jax.ShapeDtypeStruct((t, FF), MXU)
    return pl.pallas_call(
        body, name=name, grid=(t // tb, nf),
        in_specs=[row, row, col, col, pl.BlockSpec((fb, D), lambda i, f: (f, 0))],
        out_specs=[hid, hid, hid, row],
        out_shape=[hid_shape, hid_shape, hid_shape, jax.ShapeDtypeStruct((t, D), MXU)],
        compiler_params=_params(2))(h, dy, wg, wu, wd)


def _hgrn_chunk(qa, fa, lbl):
    c = HG_CHUNK
    lb = _sig(lbl[0:1, :] - lbl[1:2, :])
    sf = _sig(fa)
    forget = lb + (1.0 - lb) * sf
    kh = 1.0 - forget
    row, col = _iota((c, c), 0), _iota((c, c), 1)
    b = _dot((col <= row).astype(F32), jnp.log(forget), precision=_HI)
    bref, blast = b[c // 2:c // 2 + 1, :], b[c - 1:c, :]
    sq = _sig(qa)
    q = qa * sq
    qt, kt = q * jnp.exp(b - bref), kh * jnp.exp(bref - b)
    qb, kl = q * jnp.exp(b), kh * jnp.exp(blast - b)
    causal = col <= row
    return dict(lb=lb, sf=sf, forget=forget, sq=sq, qt=qt, kt=kt, qb=qb, kl=kl, decay=jnp.exp(blast),
                causal=causal, e_q=jnp.exp(b), e_qt=jnp.exp(b - bref), e_kt=jnp.exp(bref - b),
                e_kl=jnp.exp(blast - b))


def _hgrn_specs(t):
    c = HG_CHUNK
    n = t // c
    return c, n, WA // LANES, HG_STEP_CHUNKS if n % HG_STEP_CHUNKS == 0 else 1


def _hgrn_fwd(p_all, lbl, onorm, name):
    t = p_all.shape[0]
    c, n, nh, m = _hgrn_specs(t)

    def body(q_ref, f_ref, i_ref, g_ref, lbl_ref, on_ref, oa_ref, oraw_ref, st_ref, state):
        @pl.when(pl.program_id(0) == 0)
        def _():
            state[...] = jnp.zeros_like(state)

        heads = [slice(h * LANES, (h + 1) * LANES) for h in range(nh)]
        sts = [state[h] for h in range(nh)]
        for sub in range(m):
            rows = slice(sub * c, (sub + 1) * c)
            ks = [_hgrn_chunk(q_ref[rows, at], f_ref[rows, at], lbl_ref[:, at]) for at in heads]
            vs = [i_ref[rows, at] for at in heads]
            for h in range(nh):
                st_ref[h, sub] = sts[h]
            scores = [jnp.where(k["causal"], _dot(k["qt"], k["kt"], _NT, _HI), 0.0) for k in ks]
            outs = [_dot(a, v, precision=_HI) + _dot(k["qb"], st, _NT, _HI) for a, v, k, st in zip(scores, vs, ks, sts)]
            sts = [st * k["decay"] + _dot(v, k["kl"], _TN, _HI) for st, k, v in zip(sts, ks, vs)]
            for at, o in zip(heads, outs):
                oraw_ref[rows, at] = o
                rinv = lax.rsqrt(jnp.mean(o * o, axis=-1, keepdims=True) + NORM_EPS)
                ga = g_ref[rows, at]
                oa_ref[rows, at] = (o * rinv * on_ref[:, at] * (ga * _sig(ga))).astype(MXU)
        for h in range(nh):
            state[h] = sts[h]

    def blk(j):
        return pl.BlockSpec((m * c, WA), lambda i: (i, j))

    return pl.pallas_call(
        body, name=name, grid=(n // m,),
        in_specs=[blk(0), blk(1), blk(2), blk(3), pl.BlockSpec((2, WA), lambda i: (0, 0)),
                  pl.BlockSpec((1, WA), lambda i: (0, 0))],
        out_specs=[blk(0), blk(0), pl.BlockSpec((nh, m, LANES, LANES), lambda i: (0, i, 0, 0))],
        out_shape=[jax.ShapeDtypeStruct((t, WA), MXU), jax.ShapeDtypeStruct((t, WA), F32),
                   jax.ShapeDtypeStruct((nh, n, LANES, LANES), F32)],
        scratch_shapes=[pltpu.VMEM((nh, LANES, LANES), F32)], compiler_params=_params(1))(
            p_all, p_all, p_all, p_all, lbl, onorm)


def _hgrn_bwd(p_all, lbl, onorm, oraw, states, doa, name):
    t = p_all.shape[0]
    c, n, nh, m = _hgrn_specs(t)

    def body(q_ref, f_ref, i_ref, g_ref, lbl_ref, on_ref, oraw_ref, st_ref, doa_ref,
             dq_ref, df_ref, di_ref, dg_ref, don_ref, dlbl_ref, dstate, dlb):
        @pl.when(pl.program_id(0) == 0)
        def _():
            dstate[...] = jnp.zeros_like(dstate)
            dlb[...] = jnp.zeros_like(dlb)
            don_ref[...] = jnp.zeros_like(don_ref)

        heads = [slice(h * LANES, (h + 1) * LANES) for h in range(nh)]
        dsts = [dstate[h] for h in range(nh)]
        step = _iota((c, LANES), 0)
        row, col = _iota((c, c), 0), _iota((c, c), 1)
        for sub in reversed(range(m)):
            rows = slice(sub * c, (sub + 1) * c)
            work = []
            for h, at in enumerate(heads):
                qa, fa, v, ga = q_ref[rows, at], f_ref[rows, at], i_ref[rows, at], g_ref[rows, at]
                k = _hgrn_chunk(qa, fa, lbl_ref[:, at])
                o = oraw_ref[rows, at]
                gain = on_ref[:, at]
                rinv = lax.rsqrt(jnp.mean(o * o, axis=-1, keepdims=True) + NORM_EPS)
                on = o * rinv
                sg = _sig(ga)
                gate = ga * sg
                dout = doa_ref[rows, at]
                don_ref[:, at] += jnp.sum(dout * on * gate, axis=0, keepdims=True)
                dg_ref[rows, at] = (dout * on * gain * _dsilu(ga, sg)).astype(MXU)
                d_on = dout * gain * gate
                do = rinv * (d_on - on * jnp.mean(d_on * on, axis=-1, keepdims=True))
                work.append(dict(at=at, qa=qa, v=v, k=k, do=do, st=st_ref[h, sub]))
            for x, dst_next in zip(work, dsts):
                k, do = x["k"], x["do"]
                x["a"] = jnp.where(k["causal"], _dot(k["qt"], k["kt"], _NT, _HI), 0.0)
                x["dqb"] = _dot(do, x["st"], precision=_HI)
                x["dst"] = dst_next * k["decay"] + _dot(do, k["qb"], _TN, _HI)
                x["da"] = jnp.where(k["causal"], _dot(do, x["v"], _NT, _HI), 0.0)
            for x, dst_next in zip(work, dsts):
                k = x["k"]
                x["dqt"] = _dot(x["da"], k["kt"], precision=_HI)
                x["dkt"] = _dot(x["da"], k["qt"], _TN, _HI)
                x["dv"] = _dot(x["a"], x["do"], _TN, _HI) + _dot(k["kl"], dst_next, _NT, _HI)
                x["dkl"] = _dot(x["v"], dst_next, precision=_HI)
            for x, dst_next in zip(work, dsts):
                k, at, dqt, dkt, dkl, dqb = x["k"], x["at"], x["dqt"], x["dkt"], x["dkl"], x["dqb"]
                ddecay = jnp.sum(dst_next * x["st"], axis=0, keepdims=True)
                dq = dqb * k["e_q"] + dqt * k["e_qt"]
                dk = dkt * k["e_kt"] + dkl * k["e_kl"]
                tq, tk, tl = dqt * k["qt"], dkt * k["kt"], dkl * k["kl"]
                db = dqb * k["qb"] + tq - tk - tl
                dbref = jnp.sum(tk - tq, axis=0, keepdims=True)
                dblast = jnp.sum(tl, axis=0, keepdims=True) + ddecay * k["decay"]
                db = db + jnp.where(step == c // 2, dbref, 0.0) + jnp.where(step == c - 1, dblast, 0.0)
                dlogf = _dot((col >= row).astype(F32), db, precision=_HI)
                dq_ref[rows, at] = (dq * _dsilu(x["qa"], k["sq"])).astype(MXU)
                di_ref[rows, at] = x["dv"].astype(MXU)
                dforget = dlogf / k["forget"] - dk
                sf, lb = k["sf"], k["lb"]
                df_ref[rows, at] = (dforget * (1.0 - lb) * sf * (1.0 - sf)).astype(MXU)
                dlb[:, at] += jnp.sum(dforget * (1.0 - sf), axis=0, keepdims=True)
                dl0 = dlb[:, at] * lb * (1.0 - lb)
                dlbl_ref[:, at] = jnp.where(_iota((2, LANES), 0) == 0, dl0, -dl0)
            dsts = [x["dst"] for x in work]
        for h in range(nh):
            dstate[h] = dsts[h]

    last = n // m - 1

    def blk(j):
        return pl.BlockSpec((m * c, WA), lambda i: (last - i, j))

    vec = pl.BlockSpec((1, WA), lambda i: (0, 0))
    lg = pl.BlockSpec((2, WA), lambda i: (0, 0))
    grad = jax.ShapeDtypeStruct((t, WA), MXU)
    return pl.pallas_call(
        body, name=name, grid=(n // m,),
        in_specs=[blk(0), blk(1), blk(2), blk(3), lg, vec, blk(0),
                  pl.BlockSpec((nh, m, LANES, LANES), lambda i: (0, last - i, 0, 0)), blk(0)],
        out_specs=[blk(0), blk(0), blk(0), blk(0), vec, lg],
        out_shape=[grad, grad, grad, grad, jax.ShapeDtypeStruct((1, WA), F32), jax.ShapeDtypeStruct((2, WA), F32)],
        scratch_shapes=[pltpu.VMEM((nh, LANES, LANES), F32), pltpu.VMEM((1, WA), F32)],
        compiler_params=_params(1))(p_all, p_all, p_all, p_all, lbl, onorm, oraw, states, doa)


def _lora_act(x):
    lane = _iota(x.shape, 1)
    n_w, n_a, n_g = LORA
    return jnp.where(lane < n_w, jnp.tanh(x),
                     jnp.where(lane < n_w + n_a, x, jnp.where(lane < n_w + n_a + n_g, _sig(x), 0.0)))


def _lora_dact(x):
    lane = _iota(x.shape, 1)
    n_w, n_a, n_g = LORA
    th, s = jnp.tanh(x), _sig(x)
    return jnp.where(lane < n_w, 1.0 - th * th,
                     jnp.where(lane < n_w + n_a, 1.0, jnp.where(lane < n_w + n_a + n_g, s * (1.0 - s), 0.0)))


def _shift_down(cur, prev8, first):
    rolled = pltpu.roll(cur, 1, 0)
    edge = prev8[7:8, :] * jnp.where(first, 0.0, 1.0)
    return jnp.where(_iota(cur.shape, 0) == 0, edge, rolled)


def _shift_up(cur, next8, last):
    rows = cur.shape[0]
    rolled = pltpu.roll(cur, rows - 1, 0)
    edge = next8[0:1, :] * jnp.where(last, 0.0, 1.0)
    return jnp.where(_iota(cur.shape, 0) == rows - 1, edge, rolled)


def _rwkv_inputs(refs, first, ones):
    (pr, pk, pv, plo, qr, qk, qv, qlo, mr, mk, mv, mlo, w2c, w0, a0, kk_w, ka_w) = refs
    mix = lambda cur, prev, mu: cur[...] + mu[...] * (_shift_down(cur[...], prev[...], first) - cur[...])
    r, k, v, lo = mix(pr, qr, mr), mix(pk, qk, mk), mix(pv, qv, mv), mix(plo, qlo, mlo)
    z = _lora_act(lo)
    lin = _dot(z.astype(MXU), w2c[...])
    sg = _sig(w0[...] + lin[:, :WB])
    decay = jnp.exp(-DECAY_C * sg)
    a = _sig(a0[...] + lin[:, WB:2 * WB])
    g = lin[:, 2 * WB:]
    kk0 = k * kk_w[...]
    nrm = jnp.sqrt(_split_dot(kk0 * kk0, ones, 3))
    den = jnp.maximum(nrm, L2_EPS)
    kk = kk0 / den
    k2 = k * (1.0 + (a - 1.0) * ka_w[...])
    return dict(r=r, k=k, v=v, lo=lo, z=z, sg=sg, decay=decay, a=a, g=g, kk=kk, den=den, nrm=nrm, k2=k2)


def _rwkv_in_specs(t, tb):
    nt8 = tb // 8

    def cur(w, j):
        return pl.BlockSpec((tb, w), lambda i: (i, j))

    def prev(w, j):
        return pl.BlockSpec((8, w), lambda i: (jnp.maximum(i * nt8 - 1, 0), j))

    def vec(w, j=0):
        return pl.BlockSpec((1, w), lambda i: (0, j))

    return [cur(WB, COL_R), cur(WB, COL_K), cur(WB, COL_V), cur(256, COL_L),
            prev(WB, COL_R), prev(WB, COL_K), prev(WB, COL_V), prev(256, COL_L),
            vec(WB, 0), vec(WB, 1), vec(WB, 2), vec(256, 6),
            pl.BlockSpec((256, 3 * WB), lambda i: (0, 0)), vec(WB), vec(WB), vec(WB), vec(WB)]


def _rwkv_in_args(p_all, mu_pad, w2cat, w0, a0, k_k, k_a):
    return (p_all,) * 8 + (mu_pad,) * 4 + (w2cat, w0, a0, k_k, k_a)


def _rwkv_prep(p_all, mu_pad, w2cat, w0, a0, k_k, k_a, name):
    t = p_all.shape[0]
    tb = _tile(t, 256)

    def body(*refs):
        ins, outs = refs[:17], refs[17:]
        q = _rwkv_inputs(ins, pl.program_id(0) == 0, _head_ones(WB, HD_B))
        for ref, val in zip(outs, (q["r"], q["decay"], q["k2"], q["v"], -q["kk"], q["kk"] * q["a"], q["g"])):
            ref[...] = val

    out = pl.BlockSpec((tb, WB), lambda i: (i, 0))
    return pl.pallas_call(
        body, name=name, grid=(t // tb,), in_specs=_rwkv_in_specs(t, tb), out_specs=[out] * 7,
        out_shape=[jax.ShapeDtypeStruct((t, WB), F32)] * 7, compiler_params=_params(1))(
            *_rwkv_in_args(p_all, mu_pad, w2cat, w0, a0, k_k, k_a))


def _pair_rows(x8, i):
    return jnp.concatenate([jnp.broadcast_to(x8[i:i + 1, p * LANES:(p + 1) * LANES], (HD_B, LANES))
                            for p in range(4)], axis=0)


def _pair_sums(x):
    return jnp.concatenate([jnp.sum(x[p * HD_B:(p + 1) * HD_B], axis=0, keepdims=True) for p in range(4)], axis=1)


def _put_row(buf, i, row):
    return jnp.where(_iota(buf.shape, 0) == i, row, buf)


def _pieces(x):
    hi = x.astype(jnp.bfloat16).astype(F32)
    lo = (x - hi).astype(jnp.bfloat16).astype(F32)
    upper = (_iota((x.shape[0], LANES), 1) & (HD_B // 2)) != 0
    swapped = [jnp.where(upper, pltpu.roll(lo[:, p * LANES:(p + 1) * LANES], HD_B // 2, 1),
                         pltpu.roll(lo[:, p * LANES:(p + 1) * LANES], LANES - HD_B // 2, 1)) for p in range(4)]
    return hi, jnp.concatenate(swapped, axis=1)


def _scan_consts():
    row, lane = _iota((HD_B, LANES), 0), _iota((HD_B, LANES), 1) & (HD_B - 1)
    either = ((row ^ lane) & (HD_B // 2 - 1)) == 0
    return ((row ^ lane) & (HD_B // 2)) != 0, either.astype(jnp.bfloat16), _head_ones(LANES, HD_B)


def _pair_cols(many, consts):
    swapped, either, ones = consts
    tiles = []
    for (hi8, lo8), i in many:
        for p in range(4):
            lanes = slice(p * LANES, (p + 1) * LANES)
            hi = jnp.broadcast_to(hi8[i:i + 1, lanes], (16, LANES)).astype(jnp.bfloat16)
            lo = jnp.broadcast_to(lo8[i:i + 1, lanes], (16, LANES)).astype(jnp.bfloat16)
            for g in range(HD_B // 16):
                rows = slice(g * 16, (g + 1) * 16)
                tiles.append(jnp.where(swapped[rows], lo, hi) * either[rows])
    out = _dot(jnp.concatenate(tiles, axis=0), ones)
    return [out[m * 4 * HD_B:(m + 1) * 4 * HD_B] for m in range(len(many))]


def _block_products(w8):
    rows = _iota(w8.shape, 0)
    down, up = w8, w8
    for shift in (1, 2, 4):
        down = down * jnp.where(rows >= shift, pltpu.roll(down, shift, 0), 1.0)
        up = up * jnp.where(rows < 8 - shift, pltpu.roll(up, 8 - shift, 0), 1.0)
    return down, up


def _blocked_loop(n_blocks, prepare, advance, init):
    unroll = SCAN_UNROLL if n_blocks % SCAN_UNROLL == 0 else 1

    def trip(g, carry):
        prepared = [prepare(g * unroll + i) for i in range(unroll)]
        for p in prepared:
            carry = advance(p, carry)
        return carry

    return lax.fori_loop(0, n_blocks // unroll, trip, init)


def _rwkv_scan_fwd(r, w, k, v, a, b, name):
    t = r.shape[0]
    cc = min(t, SCAN_CHUNK)

    def body(r_ref, w_ref, k_ref, v_ref, a_ref, b_ref, y_ref, sa_ref, state):
        @pl.when(pl.program_id(0) == 0)
        def _():
            state[...] = jnp.zeros_like(state)

        consts = _scan_consts()

        def prepare(j):
            rows = pl.ds(pl.multiple_of(j * 8, 8), 8)
            r8, w8, k8, v8, a8, b8 = (ref[rows, :] for ref in (r_ref, w_ref, k_ref, v_ref, a_ref, b_ref))
            decay, _ = _block_products(w8)
            before = jnp.where(_iota(w8.shape, 0) == 0, 1.0, pltpu.roll(decay, 1, 0))
            inv = 1.0 / decay
            scaled = [_pieces(x) for x in (a8 * before, b8 * inv, k8 * inv, r8 * decay)]
            return rows, v8, _pair_cols([(x, i) for i in range(8) for x in scaled] + [(_pieces(decay), 7)], consts)

        def advance(prepared, sk):
            rows, v8, cols = prepared
            y8 = jnp.zeros((8, WB), F32)
            sa8 = jnp.zeros((8, WB), F32)
            for i in range(8):
                a_c, b_c, k_c, r_c = cols[4 * i:4 * i + 4]
                sa = _pair_sums(sk * a_c)
                sk = sk + b_c * _pair_rows(sa, 0) + k_c * _pair_rows(v8, i)
                y8 = _put_row(y8, i, _pair_sums(sk * r_c))
                sa8 = _put_row(sa8, i, sa)
            y_ref[rows, :] = y8
            sa_ref[rows, :] = sa8
            return sk * cols[-1]

        state[...] = _blocked_loop(cc // 8, prepare, advance, state[...])

    row = pl.BlockSpec((cc, WB), lambda i: (i, 0))
    return pl.pallas_call(
        body, name=name, grid=(t // cc,), in_specs=[row] * 6, out_specs=[row, row],
        out_shape=[jax.ShapeDtypeStruct((t, WB), F32)] * 2,
        scratch_shapes=[pltpu.VMEM((4 * HD_B, LANES), F32)], compiler_params=_params(1))(r, w, k, v, a, b)


def _rwkv_states(sa, w, k, v, b, name):
    t = sa.shape[0]
    cc = min(t, SCAN_CHUNK)

    def body(sa_ref, w_ref, k_ref, v_ref, b_ref, sall_ref, state):
        @pl.when(pl.program_id(0) == 0)
        def _():
            state[...] = jnp.zeros_like(state)

        consts = _scan_consts()

        def prepare(j):
            base = pl.multiple_of(j * 8, 8)
            sa8, w8, k8, v8, b8 = (ref[pl.ds(base, 8), :] for ref in (sa_ref, w_ref, k_ref, v_ref, b_ref))
            sap, vp = _pieces(sa8), _pieces(v8)
            return base, w8, k8, b8, _pair_cols([(x, i) for i in range(8) for x in (sap, vp)], consts)

        def advance(prepared, sv):
            base, w8, k8, b8, cols = prepared
            for i in range(8):
                sv = sv * _pair_rows(w8, i) + cols[2 * i] * _pair_rows(b8, i) + cols[2 * i + 1] * _pair_rows(k8, i)
                sall_ref[base + i] = sv
            return sv

        state[...] = _blocked_loop(cc // 8, prepare, advance, state[...])

    row = pl.BlockSpec((cc, WB), lambda i: (i, 0))
    return pl.pallas_call(
        body, name=name, grid=(t // cc,), in_specs=[row] * 5,
        out_specs=pl.BlockSpec((cc, 4 * HD_B, LANES), lambda i: (i, 0, 0)),
        out_shape=jax.ShapeDtypeStruct((t, 4 * HD_B, LANES), F32),
        scratch_shapes=[pltpu.VMEM((4 * HD_B, LANES), F32)], compiler_params=_params(1))(sa, w, k, v, b)


def _rwkv_scan_bwd(dy, r, w, k, a, b, name):
    t = r.shape[0]
    cc = min(t, SCAN_CHUNK)
    n = t // cc

    def body(dy_ref, r_ref, w_ref, k_ref, a_ref, b_ref, dsa_ref, dv_ref, dstate):
        @pl.when(pl.program_id(0) == 0)
        def _():
            dstate[...] = jnp.zeros_like(dstate)

        consts = _scan_consts()

        steps = range(7, -1, -1)

        def prepare(jj):
            rows = pl.ds(pl.multiple_of((cc // 8 - 1 - jj) * 8, 8), 8)
            dy8, r8, w8, k8, a8, b8 = (ref[rows, :] for ref in (dy_ref, r_ref, w_ref, k_ref, a_ref, b_ref))
            _, upto = _block_products(w8)
            later = jnp.where(_iota(w8.shape, 0) == 7, 1.0, pltpu.roll(upto, 7, 0))
            scaled = [_pieces(x) for x in (r8 / later, b8 * later, k8 * later, a8 / upto)]
            return rows, dy8, _pair_cols([(x, i) for i in steps for x in scaled] + [(_pieces(upto), 0)], consts)

        def advance(prepared, ds):
            rows, dy8, cols = prepared
            dsa8, dv8 = jnp.zeros((8, WB), F32), jnp.zeros((8, WB), F32)
            for n_done, i in enumerate(steps):
                r_c, b_c, k_c, a_c = cols[4 * n_done:4 * n_done + 4]
                ds = ds + r_c * _pair_rows(dy8, i)
                dsa = _pair_sums(ds * b_c)
                dv8 = _put_row(dv8, i, _pair_sums(ds * k_c))
                dsa8 = _put_row(dsa8, i, dsa)
                ds = ds + a_c * _pair_rows(dsa, 0)
            dsa_ref[rows, :] = dsa8
            dv_ref[rows, :] = dv8
            return ds * cols[-1]

        dstate[...] = _blocked_loop(cc // 8, prepare, advance, dstate[...])

    row = pl.BlockSpec((cc, WB), lambda i: (n - 1 - i, 0))
    return pl.pallas_call(
        body, name=name, grid=(n,), in_specs=[row] * 6, out_specs=[row] * 2,
        out_shape=[jax.ShapeDtypeStruct((t, WB), F32)] * 2,
        scratch_shapes=[pltpu.VMEM((4 * HD_B, LANES), F32)], compiler_params=_params(1))(dy, r, w, k, a, b)


def _rwkv_scan_bwd_values(dy, r, w, v, a, sa, dsa, sall, name):
    t = r.shape[0]
    cc = min(t, SCAN_CHUNK)
    n = t // cc

    def body(dy_ref, r_ref, w_ref, v_ref, a_ref, sa_ref, dsa_ref, sall_ref, sprev_ref,
             dr_ref, dw_ref, dk_ref, da_ref, db_ref, dstate):
        @pl.when(pl.program_id(0) == 0)
        def _():
            dstate[...] = jnp.zeros_like(dstate)

        consts = _scan_consts()
        before_chunk = jnp.where(pl.program_id(0) == n - 1, 0.0, 1.0) * sprev_ref[0]

        steps = range(7, -1, -1)

        def prepare(jj):
            j = cc // 8 - 1 - jj
            base = pl.multiple_of(j * 8, 8)
            dy8, r8, w8, v8, a8, sa8, dsa8 = (ref[pl.ds(base, 8), :] for ref in
                                              (dy_ref, r_ref, w_ref, v_ref, a_ref, sa_ref, dsa_ref))
            dyp, vp, sap, dsap = (_pieces(x) for x in (dy8, v8, sa8, dsa8))
            return j, base, r8, w8, a8, _pair_cols([(x, i) for i in steps for x in (dyp, vp, sap, dsap)], consts)

        def advance(prepared, carry):
            ds, sc = carry
            j, base, r8, w8, a8, cols = prepared
            rows = pl.ds(base, 8)
            outs = [jnp.zeros((8, WB), F32) for _ in range(5)]
            for n_done, i in enumerate(steps):
                if i > 0:
                    sp = sall_ref[base + i - 1]
                else:
                    sp = jnp.where(j == 0, before_chunk, sall_ref[jnp.maximum(base - 1, 0)])
                dy_c, v_c, sa_c, dsa_c = cols[4 * n_done:4 * n_done + 4]
                ds = ds + dy_c * _pair_rows(r8, i)
                vals = (_pair_sums(sc * dy_c), _pair_sums(ds * sp), _pair_sums(ds * v_c),
                        _pair_sums(sp * dsa_c), _pair_sums(ds * sa_c))
                outs = [_put_row(o, i, val) for o, val in zip(outs, vals)]
                ds = ds * _pair_rows(w8, i) + dsa_c * _pair_rows(a8, i)
                sc = sp
            for ref, o in zip((dr_ref, dw_ref, dk_ref, da_ref, db_ref), outs):
                ref[rows, :] = o
            return ds, sc

        ds, _ = _blocked_loop(cc // 8, prepare, advance, (dstate[...], sall_ref[cc - 1]))
        dstate[...] = ds

    row = pl.BlockSpec((cc, WB), lambda i: (n - 1 - i, 0))
    return pl.pallas_call(
        body, name=name, grid=(n,),
        in_specs=[row] * 7 + [pl.BlockSpec((cc, 4 * HD_B, LANES), lambda i: (n - 1 - i, 0, 0)),
                              pl.BlockSpec((1, 4 * HD_B, LANES), lambda i: (jnp.maximum((n - 1 - i) * cc - 1, 0), 0, 0))],
        out_specs=[row] * 5, out_shape=[jax.ShapeDtypeStruct((t, WB), F32)] * 5,
        scratch_shapes=[pltpu.VMEM((4 * HD_B, LANES), F32)], compiler_params=_params(1))(
            dy, r, w, v, a, sa, dsa, sall, sall)


def _rwkv_post(y, r, k2, v, g, r_k, gn_w, gn_b, name):
    t = y.shape[0]
    tb = _tile(t, 256)

    def body(y_ref, r_ref, k_ref, v_ref, g_ref, rk_ref, gw_ref, gb_ref, o_ref):
        ones = _head_ones(WB, HD_B)
        yv = y_ref[...]
        yc = yv - _split_dot(yv, ones, 3) * (1.0 / HD_B)
        rstd = lax.rsqrt(_split_dot(yc * yc, ones, 3) * (1.0 / HD_B) + GN_EPS)
        rk = _split_dot(r_ref[...] * k_ref[...] * rk_ref[...], ones, 3)
        o_ref[...] = ((yc * rstd * gw_ref[...] + gb_ref[...] + rk * v_ref[...]) * g_ref[...]).astype(MXU)

    row = pl.BlockSpec((tb, WB), lambda i: (i, 0))
    vec = pl.BlockSpec((1, WB), lambda i: (0, 0))
    return pl.pallas_call(
        body, name=name, grid=(t // tb,), in_specs=[row] * 5 + [vec] * 3, out_specs=row,
        out_shape=jax.ShapeDtypeStruct((t, WB), MXU), compiler_params=_params(1))(y, r, k2, v, g, r_k, gn_w, gn_b)


def _rwkv_post_bwd(dob, y, r, k2, v, g, r_k, gn_w, gn_b, name):
    t = y.shape[0]
    tb = _tile(t, 256)

    def body(do_ref, y_ref, r_ref, k_ref, v_ref, g_ref, rk_ref, gw_ref, gb_ref,
             dy_ref, dg_ref, dr_ref, dk_ref, dv_ref, dgw_ref, dgb_ref, drk_ref):
        @pl.when(pl.program_id(0) == 0)
        def _():
            dgw_ref[...] = jnp.zeros_like(dgw_ref)
            dgb_ref[...] = jnp.zeros_like(dgb_ref)
            drk_ref[...] = jnp.zeros_like(drk_ref)

        ones = _head_ones(WB, HD_B)
        seg = lambda x: _split_dot(x, ones, 3)
        yv, rv, kv, vv, gv = y_ref[...], r_ref[...], k_ref[...], v_ref[...], g_ref[...]
        yc = yv - seg(yv) * (1.0 / HD_B)
        rstd = lax.rsqrt(seg(yc * yc) * (1.0 / HD_B) + GN_EPS)
        yn = yc * rstd
        rk = seg(rv * kv * rk_ref[...])
        dob_v = do_ref[...]
        dg_ref[...] = dob_v * (yn * gw_ref[...] + gb_ref[...] + rk * vv)
        dyg = dob_v * gv
        dgw_ref[...] += jnp.sum(dyg * yn, axis=0, keepdims=True)
        dgb_ref[...] += jnp.sum(dyg, axis=0, keepdims=True)
        dyn = dyg * gw_ref[...]
        dy_ref[...] = rstd * (dyn - (seg(dyn) + yn * seg(dyn * yn)) * (1.0 / HD_B))
        drk = seg(dyg * vv)
        dv_ref[...] = dyg * rk
        dr_ref[...] = drk * kv * rk_ref[...]
        dk_ref[...] = drk * rv * rk_ref[...]
        drk_ref[...] += jnp.sum(drk * rv * kv, axis=0, keepdims=True)

    row = pl.BlockSpec((tb, WB), lambda i: (i, 0))
    vec = pl.BlockSpec((1, WB), lambda i: (0, 0))
    full, small = jax.ShapeDtypeStruct((t, WB), F32), jax.ShapeDtypeStruct((1, WB), F32)
    return pl.pallas_call(
        body, name=name, grid=(t // tb,),
        in_specs=[pl.BlockSpec((tb, WB), lambda i: (i, dob.shape[1] // WB - 1))] + [row] * 5 + [vec] * 3,
        out_specs=[row] * 5 + [vec] * 3,
        out_shape=[full] * 5 + [small] * 3, compiler_params=_params(1))(dob, y, r, k2, v, g, r_k, gn_w, gn_b)


def _rwkv_prep_bwd(grads, p_all, mu_pad, w2cat, w0, a0, k_k, k_a, name):
    t = p_all.shape[0]
    tb = _tile(t, 256)

    def body(*refs):
        g_refs, ins, outs = refs[:10], refs[10:27], refs[27:]
        dr_s, dw, dk2_s, dv_s, das, dbs, dg, dr_b, dk2_b, dv_b = (ref[...] for ref in g_refs)
        dr_ref, dk_ref, dv_ref, dlo_ref, dw2_ref, dw0_ref, da0_ref, dkk_ref, dka_ref = outs

        @pl.when(pl.program_id(0) == 0)
        def _():
            for ref in (dw2_ref, dw0_ref, da0_ref, dkk_ref, dka_ref):
                ref[...] = jnp.zeros_like(ref)

        ones = _head_ones(WB, HD_B)
        q = _rwkv_inputs(ins, pl.program_id(0) == 0, ones)
        kk_w, ka_w = ins[15][...], ins[16][...]
        a, kk, k = q["a"], q["kk"], q["k"]
        dk2 = dk2_s + dk2_b
        dkk = dbs * a - das
        da = dbs * kk + dk2 * k * ka_w
        dk = dk2 * (1.0 + (a - 1.0) * ka_w)
        dka_ref[...] += jnp.sum(dk2 * k * (a - 1.0), axis=0, keepdims=True)
        proj = jnp.where(q["nrm"] > L2_EPS, _split_dot(dkk * kk, ones, 3), 0.0)
        dkk0 = (dkk - kk * proj) / q["den"]
        dk = dk + dkk0 * kk_w
        dkk_ref[...] += jnp.sum(dkk0 * k, axis=0, keepdims=True)
        dal = da * a * (1.0 - a)
        da0_ref[...] += jnp.sum(dal, axis=0, keepdims=True)
        sg = q["sg"]
        dwl = dw * q["decay"] * (-DECAY_C) * sg * (1.0 - sg)
        dw0_ref[...] += jnp.sum(dwl, axis=0, keepdims=True)
        dlin = jnp.concatenate([dwl, dal, dg], axis=1).astype(MXU)
        dw2_ref[...] += _dot(q["z"].astype(MXU), dlin, _TN)
        dz = _dot(dlin, ins[12][...], _NT)
        dlo_ref[...] = dz * _lora_dact(q["lo"])
        dr_ref[...] = dr_s + dr_b
        dk_ref[...] = dk
        dv_ref[...] = dv_s + dv_b

    row = pl.BlockSpec((tb, WB), lambda i: (i, 0))
    vec = pl.BlockSpec((1, WB), lambda i: (0, 0))
    full, small = jax.ShapeDtypeStruct((t, WB), F32), jax.ShapeDtypeStruct((1, WB), F32)
    return pl.pallas_call(
        body, name=name, grid=(t // tb,), in_specs=[row] * 10 + _rwkv_in_specs(t, tb),
        out_specs=[row] * 3 + [pl.BlockSpec((tb, 256), lambda i: (i, 0)),
                               pl.BlockSpec((256, 3 * WB), lambda i: (0, 0))] + [vec] * 4,
        out_shape=[full] * 3 + [jax.ShapeDtypeStruct((t, 256), F32), jax.ShapeDtypeStruct((256, 3 * WB), F32)]
        + [small] * 4, compiler_params=_params(1))(*grads, *_rwkv_in_args(p_all, mu_pad, w2cat, w0, a0, k_k, k_a))


def _shift_bwd(dshifted, p_all, mu_pad, name):
    t = p_all.shape[0]
    tb = _tile(t, 256)
    nt, nt8 = t // tb, tb // 8
    widths, cols, mus = (WB, WB, WB, 256), (COL_R, COL_K, COL_V, COL_L), (0, 1, 2, 6)

    def body(*refs):
        d_refs, n_refs, p_refs, q_refs, m_refs = refs[0:4], refs[4:8], refs[8:12], refs[12:16], refs[16:20]
        o_refs, dmu_refs = refs[20:24], refs[24:28]
        i = pl.program_id(0)

        @pl.when(i == 0)
        def _():
            for ref in dmu_refs:
                ref[...] = jnp.zeros_like(ref)

        for d, nx, p, q, m, o, dmu in zip(d_refs, n_refs, p_refs, q_refs, m_refs, o_refs, dmu_refs):
            dv, pv, mu = d[...], p[...], m[...]
            o[...] = (dv * (1.0 - mu) + mu * _shift_up(dv, nx[...], i == nt - 1)).astype(MXU)
            dmu[...] += jnp.sum(dv * (_shift_down(pv, q[...], i == 0) - pv), axis=0, keepdims=True)

    cur_d = [pl.BlockSpec((tb, w), lambda i: (i, 0)) for w in widths]
    next_d = [pl.BlockSpec((8, w), lambda i: (jnp.minimum((i + 1) * nt8, t // 8 - 1), 0)) for w in widths]
    cur_p = [pl.BlockSpec((tb, w), lambda i, j=j: (i, j)) for w, j in zip(widths, cols)]
    prev_p = [pl.BlockSpec((8, w), lambda i, j=j: (jnp.maximum(i * nt8 - 1, 0), j)) for w, j in zip(widths, cols)]
    mu_s = [pl.BlockSpec((1, w), lambda i, j=j: (0, j)) for w, j in zip(widths, mus)]
    vecs = [pl.BlockSpec((1, w), lambda i: (0, 0)) for w in widths]
    return pl.pallas_call(
        body, name=name, grid=(nt,), in_specs=cur_d + next_d + cur_p + prev_p + mu_s, out_specs=cur_d + vecs,
        out_shape=[jax.ShapeDtypeStruct((t, w), MXU) for w in widths]
        + [jax.ShapeDtypeStruct((1, w), F32) for w in widths],
        compiler_params=_params(1))(*dshifted, *dshifted, *(p_all,) * 8, *(mu_pad,) * 4)


def _peer(k):
    x, y, c = (lax.axis_index(n) for n in AXES)
    px = 1 - x if k & 4 else x
    py = 1 - y if k & 2 else y
    pc = 1 - c if k & 1 else c
    return (px, py, pc), 4 * px + 2 * py + pc


def _exchange_copy(src_refs, land_refs, send_sems, recv_sems, per_peer, j, k, arriving):
    _, me = _peer(0)
    peer, idx = _peer(k)
    sem = j * (N_DEV - 1) + k - 1
    return pltpu.make_async_remote_copy(
        src_ref=src_refs[j].at[idx] if per_peer[j] else src_refs[j],
        dst_ref=land_refs[j].at[idx if arriving else me],
        send_sem=send_sems.at[sem], recv_sem=recv_sems.at[sem],
        device_id=peer, device_id_type=pl.DeviceIdType.MESH)


def _exchange_start(srcs, per_peer, name, after=()):
    n = len(srcs)
    shapes = [tuple(s.shape[1:]) if pp else tuple(s.shape) for s, pp in zip(srcs, per_peer)]
    pairs = [(j, k) for k in range(1, N_DEV) for j in range(n)]
    first_out = 2 * n + len(after)

    def body(*refs):
        src_refs, land_refs, (send_sems, recv_sems), token = refs[:n], refs[n:2 * n], refs[first_out:first_out + 2], refs[-1]
        for j, k in pairs:
            _exchange_copy(src_refs, land_refs, send_sems, recv_sems, per_peer, j, k, False).start()
        token[...] = jnp.zeros_like(token)

    hbm, sem = pl.BlockSpec(memory_space=pltpu.HBM), pl.BlockSpec(memory_space=pltpu.SEMAPHORE)
    lands = [lax.empty((N_DEV,) + shp, s.dtype) for shp, s in zip(shapes, srcs)]
    operands = [pltpu.with_memory_space_constraint(a, pltpu.HBM) for a in list(srcs) + lands]
    n_sems = n * (N_DEV - 1)
    out = pl.pallas_call(
        body, name=name, in_specs=[hbm] * (2 * n) + [pl.BlockSpec(memory_space=pl.ANY)] * len(after),
        out_specs=[sem, sem] + [hbm] * (2 * n) + [pl.BlockSpec(memory_space=pltpu.VMEM)],
        out_shape=[pltpu.SemaphoreType.DMA((n_sems,)), pltpu.SemaphoreType.DMA((n_sems,))]
        + [pltpu.HBM(a.shape, a.dtype) for a in operands] + [jax.ShapeDtypeStruct((8, LANES), F32)],
        input_output_aliases={j: 2 + j for j in range(2 * n)},
        compiler_params=pltpu.CompilerParams(has_side_effects=pltpu.SideEffectType.DATAFLOW_SIDE_EFFECTING))(
            *operands, *after)
    return (out[0], out[1], out[2:2 + n], out[2 + n:2 + 2 * n], per_peer), out[-1]


def _exchange_wait(handle, after, name):
    send_sems, recv_sems, srcs, lands, per_peer = handle
    n = len(srcs)
    pairs = [(j, k) for k in range(1, N_DEV) for j in range(n)]

    def body(*refs):
        src_refs, land_refs, (send_sems, recv_sems) = refs[:n], refs[n:2 * n], refs[2 * n:2 * n + 2]
        for j, k in pairs:
            _exchange_copy(src_refs, land_refs, send_sems, recv_sems, per_peer, j, k, False).wait_send()
            _exchange_copy(src_refs, land_refs, send_sems, recv_sems, per_peer, j, k, True).wait_recv()

    hbm, sem = pl.BlockSpec(memory_space=pltpu.HBM), pl.BlockSpec(memory_space=pltpu.SEMAPHORE)
    out = pl.pallas_call(
        body, name=name, in_specs=[hbm] * (2 * n) + [sem, sem, pl.BlockSpec(memory_space=pl.ANY)],
        out_specs=[hbm] * (2 * n), out_shape=[pltpu.HBM(a.shape, a.dtype) for a in list(srcs) + list(lands)],
        input_output_aliases={j: j for j in range(2 * n)},
        compiler_params=pltpu.CompilerParams(has_side_effects=pltpu.SideEffectType.DATAFLOW_SIDE_EFFECTING))(
            *srcs, *lands, send_sems, recv_sems, after)
    return out[n:]


def _adam_update(g, w, m, v):
    c1, c2 = 1.0 - ADAM_B1 ** ADAM_STEP, 1.0 - ADAM_B2 ** ADAM_STEP
    nm = ADAM_B1 * m + (1.0 - ADAM_B1) * g
    nv = ADAM_B2 * v + (1.0 - ADAM_B2) * (g * g)
    return -ADAM_LR * ((nm / c1) / (jnp.sqrt(nv / c2) + ADAM_EPS) + ADAM_WD * w), nm, nv


def _row_tile(rows, cols):
    padded = -(-cols // LANES) * LANES
    cap = max(16, ADAM_BLOCK_BYTES // (N_DEV * padded * 4))
    best = 16
    for t in range(16, min(rows, cap) + 1, 16):
        if rows % t == 0:
            best = t
    return best


def _adamw(parts, w, m, v, name):
    _, rows, cols = w.shape
    tb = _row_tile(rows, cols)

    def body(p_ref, w_ref, m_ref, v_ref, g_ref, d_ref, nm_ref, nv_ref):
        g = p_ref[0].astype(F32)
        for d in range(1, N_DEV):
            g = g + p_ref[d].astype(F32)
        g_ref[0] = g
        d_ref[0], nm_ref[0], nv_ref[0] = _adam_update(g, w_ref[0], m_ref[0], v_ref[0])

    row = pl.BlockSpec((1, tb, cols), lambda i: (0, i, 0))
    out = jax.ShapeDtypeStruct(w.shape, F32)
    return pl.pallas_call(
        body, name=name, grid=(rows // tb,),
        in_specs=[pl.BlockSpec((N_DEV, tb, cols), lambda i: (0, i, 0)), row, row, row], out_specs=[row] * 4,
        out_shape=[out] * 4, compiler_params=_params(1))(parts, w, m, v)


def _adamw_small(parts, ws, ms, vs, name):
    n = len(ws)

    def body(*refs):
        p_ref = refs[0]
        w_refs, m_refs, v_refs = refs[1:1 + n], refs[1 + n:1 + 2 * n], refs[1 + 2 * n:1 + 3 * n]
        outs = refs[1 + 3 * n:]
        base = 0
        for j in range(n):
            rows, cols = ws[j].shape
            size = rows * cols
            for ch in range(-(-size // LANES)):
                r, c0 = divmod(ch * LANES, cols)
                width = min(LANES, cols - c0)
                g = p_ref[0, base + ch:base + ch + 1, 0:width]
                for d in range(1, N_DEV):
                    g = g + p_ref[d, base + ch:base + ch + 1, 0:width]
                at = (slice(r, r + 1), slice(c0, c0 + width))
                delta, nm, nv = _adam_update(g, w_refs[j][at], m_refs[j][at], v_refs[j][at])
                for out, val in zip((outs[j], outs[n + j], outs[2 * n + j], outs[3 * n + j]), (g, delta, nm, nv)):
                    out[at] = val
            base += -(-size // (8 * LANES)) * 8

    vmem = pl.BlockSpec(memory_space=pltpu.VMEM)
    res = pl.pallas_call(
        body, name=name, in_specs=[vmem] * (1 + 3 * n), out_specs=[vmem] * (4 * n),
        out_shape=[jax.ShapeDtypeStruct(a.shape, F32) for a in ws] * 4)(parts, *ws, *ms, *vs)
    return res[:n], res[n:2 * n], res[2 * n:3 * n], res[3 * n:]


def _rows(a, multiple):
    flat = a.reshape(-1)
    pad = -flat.shape[0] % (multiple * LANES)
    if pad:
        flat = jnp.concatenate([flat, jnp.zeros((pad,), a.dtype)])
    return flat.reshape(-1, LANES)


def _pack(arrs, multiple):
    return jnp.concatenate([_rows(a, multiple) for a in arrs], axis=0)


def _gathered_to_full(g, name, shard_shape):
    g = g.reshape((N_DEV,) + shard_shape)
    if name in COL_SHARDED:
        return jnp.transpose(g, (1, 0, 2)).reshape(shard_shape[0], N_DEV * shard_shape[1])
    return g.reshape(N_DEV * shard_shape[0], shard_shape[1])


def _full_to_per_device(full, name):
    if name in COL_SHARDED:
        r, c = full.shape
        return jnp.transpose(full.reshape(r, N_DEV, c // N_DEV), (1, 0, 2))
    return full.reshape(N_DEV, full.shape[0] // N_DEV, full.shape[1])


def _w2cat(w2, a2, g2):
    n_w, n_a, n_g = LORA
    out = jnp.zeros((256, 3 * WB), w2.dtype)
    out = out.at[0:n_w, 0:WB].set(w2)
    out = out.at[n_w:n_w + n_a, WB:2 * WB].set(a2)
    return out.at[n_w + n_a:n_w + n_a + n_g, 2 * WB:].set(g2)


class _Local:
    def __init__(self, w):
        self.w = w

    def weights(self, group, after=None):
        return self.w

    def started(self):
        return ()

    def send(self, grads, names):
        return ()


class _Overlapped:
    GROUPS = {"ffn1": ("ffn1_w_gate", "ffn1_w_up", "ffn1_w_down"),
              "mixer_in": ("w_in", "rwkv_w2", "rwkv_a2", "rwkv_g2"),
              "late": ("w_out", "ffn2_w_gate", "ffn2_w_up", "ffn2_w_down")}

    def __init__(self, wts):
        x, y, c = (lax.axis_index(n) for n in AXES)
        self.wts, self.me, self.gathers, self.sends = wts, 4 * x + 2 * y + c, {}, []
        self._gather("ffn1", ())

    def _gather(self, group, after):
        names = self.GROUPS[group]
        shards = [self.wts[n].astype(MXU) for n in names]
        handle, token = _exchange_start(shards, [False] * len(names), "gather_" + group, after)
        self.gathers[group] = (names, shards, handle, token)
        self.newest = token

    def started(self):
        return (self.newest,)

    def _own_slot(self, land, mine):
        return lax.dynamic_update_slice(land, mine[None], (self.me,) + (0,) * mine.ndim)

    def weights(self, group, after=None):
        names, shards, handle, token = self.gathers[group]
        lands = _exchange_wait(handle, token if after is None else after, "gathered_" + group)
        w = {n: _gathered_to_full(self._own_slot(land, own), n, own.shape[1:])
             for n, own, land in zip(names, shards, lands)}
        order = list(self.GROUPS)
        if group != order[-1]:
            self._gather(order[order.index(group) + 1], (w[names[0]],))
        if group == "ffn1":
            for n in SMALL:
                keep = n in ("hgrn_lb_logits", "rwkv_r_k", "final_norm")
                w[n] = self.wts[n] if keep else self.wts[n].reshape(1, -1)
        return w

    def send(self, grads, names, small=None):
        contrib = [_full_to_per_device(grads[n], n).astype(WIRE) for n in names]
        per_peer = [True] * len(names)
        if small is not None:
            names, contrib, per_peer = names + ("small",), contrib + [small], per_peer + [False]
        handle, token = _exchange_start(contrib, per_peer, "scatter_" + names[0])
        self.sends.append((names, contrib, per_peer, handle))
        self.last_token = token
        return (token,)

    def received(self, which, after):
        names, contrib, per_peer, handle = self.sends[which]
        lands = _exchange_wait(handle, after, "scattered_" + names[0])
        parts = {}
        for n, own, pp, land in zip(names, contrib, per_peer, lands):
            mine = lax.dynamic_index_in_dim(own, self.me, 0, keepdims=False) if pp else own
            parts[n] = self._own_slot(land, mine)
        return parts


def _local_step(x, target, net):
    n_w, n_a, n_g = LORA
    w = dict(net.weights("ffn1"))
    h1 = _rms_fwd(x, w["ffn1_norm"], "ffn1_norm")
    x1 = _ffn_fwd(x, h1, w["ffn1_w_gate"], w["ffn1_w_up"], w["ffn1_w_down"], "ffn1_fwd", after=net.started())
    w.update(net.weights("mixer_in", x1))
    w_in_pad = jnp.pad(w["w_in"], ((0, 0), (0, N_INP - N_IN)))
    mu_pad = jnp.pad(w["rwkv_shift_mu"], ((0, 0), (0, 1792 - 1696)))
    w2cat = _w2cat(w["rwkv_w2"], w["rwkv_a2"], w["rwkv_g2"])
    r_k = w["rwkv_r_k"].reshape(1, WB)
    rw = (mu_pad, w2cat, w["rwkv_w0"], w["rwkv_a0"], w["rwkv_k_k"], w["rwkv_k_a"])

    h2 = _rms_fwd(x1, w["mix_norm"], "mix_norm")
    p_all = _matmul(h2, w_in_pad, after=net.started(), name="in_proj")
    oa, oraw, states = _hgrn_fwd(p_all, w["hgrn_lb_logits"], w["hgrn_out_norm"], "hgrn_fwd")
    r, decay, k2, v, sa, sb, g = _rwkv_prep(p_all, *rw, "rwkv_prep")
    y, s_a = _rwkv_scan_fwd(r, decay, k2, v, sa, sb, "rwkv_scan_fwd")
    sall = _rwkv_states(s_a, decay, k2, v, sb, "rwkv_states")
    post_w = (r_k, w["rwkv_gn_w"], w["rwkv_gn_b"])
    ob = _rwkv_post(y, r, k2, v, g, *post_w, "rwkv_post")
    w.update(net.weights("late", ob))
    o = jnp.concatenate([oa, ob], axis=1)
    x2 = _matmul(o, w["w_out"], res=x1, name="out_proj")
    h3 = _rms_fwd(x2, w["ffn2_norm"], "ffn2_norm")
    x3 = _ffn_fwd(x2, h3, w["ffn2_w_gate"], w["ffn2_w_up"], w["ffn2_w_down"], "ffn2_fwd")
    loss, dx3, d_final = _loss_head(x3, w["final_norm"].reshape(1, D), target, "loss_head")

    grads = {"final_norm": d_final.reshape(D)}

    def ffn_back(prefix, h, dy, x_in, norm):
        wg, wu, wd = (w[prefix + s] for s in ("_w_gate", "_w_up", "_w_down"))
        act, dgate, dup, dout = _ffn_bwd(h, dy, wg, wu, wd, prefix + "_bwd")
        dh = _matmul(dup, wu, tb=True, res=_matmul(dgate, wg, tb=True, name=prefix + "_dh_gate"), name=prefix + "_dh")
        sent = ()
        for which, a_op, b_op in (("_w_gate", h, dgate), ("_w_up", h, dup), ("_w_down", act, dout)):
            grads[prefix + which] = _matmul(a_op, b_op, ta=True, out_dtype=WIRE, after=sent, name=prefix + "_d" + which)
            sent = net.send(grads, (prefix + which,))
        dx, grads[prefix + "_norm"] = _rms_bwd(x_in, norm, dh, dy, prefix + "_norm_bwd", after=sent)
        return dx

    dx2 = ffn_back("ffn2", h3, dx3, x2, w["ffn2_norm"])
    grads["w_out"] = _matmul(o, dx2, ta=True, out_dtype=WIRE, name="d_w_out")
    sent = net.send(grads, ("w_out",))
    do = _matmul(dx2, w["w_out"], tb=True, after=sent, name="d_mixed")
    dqa, dfa, dia, dga, grads["hgrn_out_norm"], grads["hgrn_lb_logits"] = _hgrn_bwd(
        p_all, w["hgrn_lb_logits"], w["hgrn_out_norm"], oraw, states, do, "hgrn_bwd")
    dy, dg, dr_b, dk2_b, dv_b, grads["rwkv_gn_w"], grads["rwkv_gn_b"], d_rk = _rwkv_post_bwd(
        do, y, r, k2, v, g, *post_w, "rwkv_post_bwd")
    grads["rwkv_r_k"] = d_rk.reshape(w["rwkv_r_k"].shape)
    d_sa, dv = _rwkv_scan_bwd(dy, r, decay, k2, sa, sb, "rwkv_scan_bwd")
    dr, dw, dk2, dsa, dsb = _rwkv_scan_bwd_values(dy, r, decay, v, sa, s_a, d_sa, sall, "rwkv_scan_bwd_values")
    (dsr, dsk, dsv, dslo, dw2cat, grads["rwkv_w0"], grads["rwkv_a0"], grads["rwkv_k_k"],
     grads["rwkv_k_a"]) = _rwkv_prep_bwd((dr, dw, dk2, dv, dsa, dsb, dg, dr_b, dk2_b, dv_b), p_all, *rw,
                                         "rwkv_prep_bwd")
    grads["rwkv_w2"] = dw2cat[0:n_w, 0:WB]
    grads["rwkv_a2"] = dw2cat[n_w:n_w + n_a, WB:2 * WB]
    grads["rwkv_g2"] = dw2cat[n_w + n_a:n_w + n_a + n_g, 2 * WB:]
    dpr, dpk, dpv, dplo, dmu_r, dmu_k, dmu_v, dmu_lo = _shift_bwd((dsr, dsk, dsv, dslo), p_all, mu_pad, "shift_bwd")
    grads["rwkv_shift_mu"] = jnp.concatenate([dmu_r, dmu_k, dmu_v, dmu_lo], axis=1)[:, :1696]
    dp = jnp.concatenate([dqa, dfa, dia, dga, dpr, dpk, dpv, dplo], axis=1)
    grads["w_in"] = _matmul(h2, dp, ta=True, out_dtype=WIRE, name="d_w_in")[:, :N_IN]
    sent = net.send(grads, ("w_in", "rwkv_w2", "rwkv_a2", "rwkv_g2"))
    dh2 = _matmul(dp, w_in_pad, tb=True, after=sent, name="d_h2")
    dx1, grads["mix_norm"] = _rms_bwd(x1, w["mix_norm"], dh2, dx2, "mix_norm_bwd")
    dx0 = ffn_back("ffn1", h1, dx1, x, w["ffn1_norm"])
    return loss[0, 0], dx0, grads


def kernel(x, ffn1_norm, ffn1_w_gate, ffn1_w_up, ffn1_w_down, mix_norm, w_in, hgrn_lb_logits, hgrn_out_norm, rwkv_shift_mu, rwkv_w0, rwkv_w2, rwkv_a0, rwkv_a2, rwkv_g2, rwkv_k_k, rwkv_k_a, rwkv_r_k, rwkv_gn_w, rwkv_gn_b, w_out, ffn2_norm, ffn2_w_gate, ffn2_w_up, ffn2_w_down, final_norm, loss_target, m_ffn1_norm, m_ffn1_w_gate, m_ffn1_w_up, m_ffn1_w_down, m_mix_norm, m_w_in, m_hgrn_lb_logits, m_hgrn_out_norm, m_rwkv_shift_mu, m_rwkv_w0, m_rwkv_w2, m_rwkv_a0, m_rwkv_a2, m_rwkv_g2, m_rwkv_k_k, m_rwkv_k_a, m_rwkv_r_k, m_rwkv_gn_w, m_rwkv_gn_b, m_w_out, m_ffn2_norm, m_ffn2_w_gate, m_ffn2_w_up, m_ffn2_w_down, m_final_norm, v_ffn1_norm, v_ffn1_w_gate, v_ffn1_w_up, v_ffn1_w_down, v_mix_norm, v_w_in, v_hgrn_lb_logits, v_hgrn_out_norm, v_rwkv_shift_mu, v_rwkv_w0, v_rwkv_w2, v_rwkv_a0, v_rwkv_a2, v_rwkv_g2, v_rwkv_k_k, v_rwkv_k_a, v_rwkv_r_k, v_rwkv_gn_w, v_rwkv_gn_b, v_w_out, v_ffn2_norm, v_ffn2_w_gate, v_ffn2_w_up, v_ffn2_w_down, v_final_norm):
    args = dict(locals())
    wts = {n: args[n] for n in WEIGHTS}
    mom = {n: args["m_" + n] for n in WEIGHTS}
    var = {n: args["v_" + n] for n in WEIGHTS}
    net = _Overlapped(wts)
    loss, grad_x, grads = _local_step(x[0], loss_target[0], net)
    loss = lax.psum(loss, AXES)
    after, = net.send(grads, (), _pack([grads[n] for n in SMALL], 8))

    new = {}
    two_d = lambda a: a if a.ndim == 2 else a.reshape(1, -1)
    for which in range(len(net.sends)):
        for n, part in net.received(which, after).items():
            if n == "small":
                small = _adamw_small(part, *([two_d(src[k]) for k in SMALL] for src in (wts, mom, var)), "adamw_small")
                for j, k in enumerate(SMALL):
                    new[k] = [res[j].reshape(wts[k].shape) for res in small]
            else:
                new[n] = _adamw(part, wts[n], mom[n], var[n], "adamw_" + n)
                after = new[n][1]
    return (loss, grad_x[None], *[new[n][0] for n in WEIGHTS], *[new[n][1] for n in WEIGHTS],
            *[new[n][2] for n in WEIGHTS], *[new[n][3] for n in WEIGHTS])
```

```python
import functools
import math

import jax
import jax.numpy as jnp
from jax import lax
from jax.experimental import pallas as pl
from jax.experimental.pallas import tpu as pltpu

F32 = jnp.float32
MXU = jnp.bfloat16
WIRE = jnp.bfloat16
D = 1024
FF = 2816
WA = 512
WB = 512
HD_B = 64
N_IN = 3744
N_INP = 3840
COL_R, COL_K, COL_V = 4, 5, 6
COL_L = 14
LORA = (32, 32, 96)
HG_CHUNK = 64
HG_STEP_CHUNKS = 4
SCAN_CHUNK = 64
SCAN_UNROLL = 8
NORM_EPS = 1e-6
GN_EPS = 64e-5
L2_EPS = 1e-12
DECAY_C = math.exp(-0.5)
N_DEV = 8
LANES = 128
ADAM_BLOCK_BYTES = 4 * 1024 * 1024
MATMUL_BLOCK_BYTES = 40 * 1024 * 1024
VMEM_LIMIT = 56 * 1024 * 1024
ADAM_LR, ADAM_B1, ADAM_B2, ADAM_EPS, ADAM_WD, ADAM_STEP = 0.001, 0.9, 0.999, 1e-08, 0.01, 10
AXES = ("x", "y", "c")

SHARDED = ("ffn1_w_gate", "ffn1_w_up", "ffn1_w_down", "w_in", "rwkv_w2", "rwkv_a2", "rwkv_g2", "w_out",
           "ffn2_w_gate", "ffn2_w_up", "ffn2_w_down")
COL_SHARDED = {"ffn1_w_gate", "ffn1_w_up", "w_in", "rwkv_w2", "rwkv_a2", "rwkv_g2", "ffn2_w_gate", "ffn2_w_up"}
SMALL = ("ffn1_norm", "mix_norm", "hgrn_lb_logits", "hgrn_out_norm", "rwkv_shift_mu", "rwkv_w0", "rwkv_a0",
         "rwkv_k_k", "rwkv_k_a", "rwkv_r_k", "rwkv_gn_w", "rwkv_gn_b", "ffn2_norm", "final_norm")
WEIGHTS = ("ffn1_norm", "ffn1_w_gate", "ffn1_w_up", "ffn1_w_down", "mix_norm", "w_in", "hgrn_lb_logits",
           "hgrn_out_norm", "rwkv_shift_mu", "rwkv_w0", "rwkv_w2", "rwkv_a0", "rwkv_a2", "rwkv_g2", "rwkv_k_k",
           "rwkv_k_a", "rwkv_r_k", "rwkv_gn_w", "rwkv_gn_b", "w_out", "ffn2_norm", "ffn2_w_gate", "ffn2_w_up",
           "ffn2_w_down", "final_norm")


def _tile(n, cap):
    if n <= cap:
        return n
    for t in range(cap - cap % LANES, 0, -LANES):
        if n % t == 0:
            return t
    raise ValueError((n, cap))


def _params(n_axes):
    return pltpu.CompilerParams(dimension_semantics=("arbitrary",) * n_axes, vmem_limit_bytes=VMEM_LIMIT)


def _sig(x):
    return jax.nn.sigmoid(x)


def _dsilu(z, s):
    return s * (1.0 + z * (1.0 - s))


def _dot(a, b, dims=((1,), (0,)), precision=None):
    return lax.dot_general(a, b, (dims, ((), ())), preferred_element_type=F32, precision=precision)


_NT = ((1,), (1,))
_TN = ((0,), (0,))
_HI = lax.Precision.HIGH


def _iota(shape, dim):
    return lax.broadcasted_iota(jnp.int32, shape, dim)


def _split_dot(x, ones, passes):
    hi = x.astype(jnp.bfloat16)
    acc = _dot(hi, ones)
    rem = x
    for _ in range(passes - 1):
        rem = rem - hi.astype(F32)
        hi = rem.astype(jnp.bfloat16)
        acc = acc + _dot(hi, ones)
    return acc


def _head_ones(n, width):
    shift = width.bit_length() - 1
    return (_iota((n, n), 0) >> shift == _iota((n, n), 1) >> shift).astype(jnp.bfloat16)


def _matmul(a, b, *, ta=False, tb=False, out_dtype=F32, res=None, after=(), name):
    m, k = (a.shape[1], a.shape[0]) if ta else a.shape
    n = b.shape[0] if tb else b.shape[1]
    tm, tn = _tile(m, 1408), _tile(n, 1408)
    in_bytes = max(a.dtype.itemsize, b.dtype.itemsize)
    for tk in (_tile(k, 1408), _tile(k, 1024), _tile(k, 512), _tile(k, 256)):
        if 2 * (tm + tn) * tk * in_bytes + 3 * tm * tn * 4 <= MATMUL_BLOCK_BYTES:
            break
    nk = k // tk
    dims = ((0 if ta else 1,), (1 if tb else 0,))

    def body(*refs):
        a_ref, b_ref = refs[:2]
        o_ref, acc = refs[-2:]
        kk = pl.program_id(2)

        @pl.when(kk == 0)
        def _():
            acc[...] = jnp.zeros_like(acc)

        acc[...] += _dot(a_ref[...].astype(MXU), b_ref[...].astype(MXU), dims)

        @pl.when(kk == nk - 1)
        def _():
            v = acc[...]
            if res is not None:
                v = v + refs[2][...]
            o_ref[...] = v.astype(out_dtype)

    a_spec = pl.BlockSpec((tk, tm), lambda i, j, kk: (kk, i)) if ta else pl.BlockSpec((tm, tk), lambda i, j, kk: (i, kk))
    b_spec = pl.BlockSpec((tn, tk), lambda i, j, kk: (j, kk)) if tb else pl.BlockSpec((tk, tn), lambda i, j, kk: (kk, j))
    o_spec = pl.BlockSpec((tm, tn), lambda i, j, kk: (i, j))
    ins, specs = [a, b], [a_spec, b_spec]
    if res is not None:
        ins.append(res)
        specs.append(o_spec)
    ins += list(after)
    specs += [pl.BlockSpec(memory_space=pl.ANY)] * len(after)
    return pl.pallas_call(
        body, name=name, grid=(m // tm, n // tn, nk), in_specs=specs, out_specs=o_spec,
        out_shape=jax.ShapeDtypeStruct((m, n), out_dtype), scratch_shapes=[pltpu.VMEM((tm, tn), F32)],
        compiler_params=_params(3))(*ins)


def _rms_fwd(x, g, name):
    t = x.shape[0]
    tb = _tile(t, 512)

    def body(x_ref, g_ref, o_ref):
        xv = x_ref[...]
        rinv = lax.rsqrt(jnp.mean(xv * xv, axis=-1, keepdims=True) + NORM_EPS)
        o_ref[...] = (xv * rinv * g_ref[...]).astype(MXU)

    return pl.pallas_call(
        body, name=name, grid=(t // tb,),
        in_specs=[pl.BlockSpec((tb, D), lambda i: (i, 0)), pl.BlockSpec((1, D), lambda i: (0, 0))],
        out_specs=pl.BlockSpec((tb, D), lambda i: (i, 0)), out_shape=jax.ShapeDtypeStruct((t, D), MXU),
        compiler_params=_params(1))(x, g)


def _rms_bwd(x, g, dh, dres, name, after=()):
    t = x.shape[0]
    tb = _tile(t, 512)

    def body(x_ref, g_ref, dh_ref, dres_ref, *rest):
        dx_ref, dg_ref = rest[-2:]

        @pl.when(pl.program_id(0) == 0)
        def _():
            dg_ref[...] = jnp.zeros_like(dg_ref)

        xv = x_ref[...]
        rinv = lax.rsqrt(jnp.mean(xv * xv, axis=-1, keepdims=True) + NORM_EPS)
        xhat = xv * rinv
        dhv = dh_ref[...]
        dg_ref[...] += jnp.sum(dhv * xhat, axis=0, keepdims=True)
        dxhat = dhv * g_ref[...]
        dx_ref[...] = dres_ref[...] + rinv * (dxhat - xhat * jnp.mean(dxhat * xhat, axis=-1, keepdims=True))

    row = pl.BlockSpec((tb, D), lambda i: (i, 0))
    vec = pl.BlockSpec((1, D), lambda i: (0, 0))
    return pl.pallas_call(
        body, name=name, grid=(t // tb,),
        in_specs=[row, vec, row, row] + [pl.BlockSpec(memory_space=pl.ANY)] * len(after), out_specs=[row, vec],
        out_shape=[jax.ShapeDtypeStruct((t, D), F32), jax.ShapeDtypeStruct((1, D), F32)],
        compiler_params=_params(1))(x, g, dh, dres, *after)


def _loss_head(x, g, target, name):
    t = x.shape[0]
    tb = _tile(t, 512)

    def body(x_ref, g_ref, t_ref, loss_ref, dx_ref, dg_ref):
        @pl.when(pl.program_id(0) == 0)
        def _():
            dg_ref[...] = jnp.zeros_like(dg_ref)
            loss_ref[...] = jnp.zeros_like(loss_ref)

        xv = x_ref[...]
        gv = g_ref[...]
        rinv = lax.rsqrt(jnp.mean(xv * xv, axis=-1, keepdims=True) + NORM_EPS)
        xhat = xv * rinv
        err = xhat * gv - t_ref[...]
        per_tok = jnp.mean(err * err, axis=-1, keepdims=True)
        loss_ref[...] += jnp.broadcast_to(0.5 * jnp.sum(per_tok, axis=0, keepdims=True), loss_ref.shape)
        dy = err * (1.0 / D)
        dg_ref[...] += jnp.sum(dy * xhat, axis=0, keepdims=True)
        dxhat = dy * gv
        dx_ref[...] = rinv * (dxhat - xhat * jnp.mean(dxhat * xhat, axis=-1, keepdims=True))

    row = pl.BlockSpec((tb, D), lambda i: (i, 0))
    vec = pl.BlockSpec((1, D), lambda i: (0, 0))
    return pl.pallas_call(
        body, name=name, grid=(t // tb,), in_specs=[row, vec, row],
        out_specs=[pl.BlockSpec((1, LANES), lambda i: (0, 0)), row, vec],
        out_shape=[jax.ShapeDtypeStruct((1, LANES), F32), jax.ShapeDtypeStruct((t, D), F32),
                   jax.ShapeDtypeStruct((1, D), F32)],
        compiler_params=_params(1))(x, g, target)


def _ffn_fwd(x, h, wg, wu, wd, name, after=()):
    t = x.shape[0]
    tb, fb = _tile(t, 1024), 256
    nf = FF // fb

    def body(x_ref, h_ref, wg_ref, wu_ref, wd_ref, *rest):
        o_ref, acc = rest[-2:]
        f = pl.program_id(1)

        @pl.when(f == 0)
        def _():
            acc[...] = jnp.zeros_like(acc)

        hv = h_ref[...]
        gate = _dot(hv, wg_ref[...])
        up = _dot(hv, wu_ref[...])
        act = (gate * _sig(gate) * up).astype(MXU)
        acc[...] += _dot(act, wd_ref[...])

        @pl.when(f == nf - 1)
        def _():
            o_ref[...] = x_ref[...] + 0.5 * acc[...]

    row = pl.BlockSpec((tb, D), lambda i, f: (i, 0))
    col = pl.BlockSpec((D, fb), lambda i, f: (0, f))
    return pl.pallas_call(
        body, name=name, grid=(t // tb, nf),
        in_specs=[row, row, col, col, pl.BlockSpec((fb, D), lambda i, f: (f, 0))]
        + [pl.BlockSpec(memory_space=pl.ANY)] * len(after), out_specs=row,
        out_shape=jax.ShapeDtypeStruct((t, D), F32), scratch_shapes=[pltpu.VMEM((tb, D), F32)],
        compiler_params=_params(2))(x, h, wg, wu, wd, *after)


def _ffn_bwd(h, dy, wg, wu, wd, name):
    t = h.shape[0]
    tb, fb = _tile(t, 1024), 256
    nf = FF // fb

    def body(h_ref, dy_ref, wg_ref, wu_ref, wd_ref, act_ref, dg_ref, du_ref, dout_ref):
        hv = h_ref[...]
        dout = (0.5 * dy_ref[...]).astype(MXU)
        dout_ref[...] = dout
        gate = _dot(hv, wg_ref[...])
        up = _dot(hv, wu_ref[...])
        dact = _dot(dout, wd_ref[...], _NT)
        s = _sig(gate)
        silu = gate * s
        act_ref[...] = (silu * up).astype(MXU)
        du_ref[...] = (dact * silu).astype(MXU)
        dg_ref[...] = (dact * up * _dsilu(gate, s)).astype(MXU)

    row = pl.BlockSpec((tb, D), lambda i, f: (i, 0))
    col = pl.BlockSpec((D, fb), lambda i, f: (0, f))
    hid = pl.BlockSpec((tb, fb), lambda i, f: (i, f))
    hid_shape = jax.ShapeDtypeStruct((t, FF), MXU)
    return pl.pallas_call(
        body, name=name, grid=(t // tb, nf),
        in_specs=[row, row, col, col, pl.BlockSpec((fb, D), lambda i, f: (f, 0))],
        out_specs=[hid, hid, hid, row],
        out_shape=[hid_shape, hid_shape, hid_shape, jax.ShapeDtypeStruct((t, D), MXU)],
        compiler_params=_params(2))(h, dy, wg, wu, wd)


def _hgrn_chunk(qa, fa, lbl):
    c = HG_CHUNK
    lb = _sig(lbl[0:1, :] - lbl[1:2, :])
    sf = _sig(fa)
    forget = lb + (1.0 - lb) * sf
    kh = 1.0 - forget
    row, col = _iota((c, c), 0), _iota((c, c), 1)
    b = _dot((col <= row).astype(F32), jnp.log(forget), precision=_HI)
    bref, blast = b[c // 2:c // 2 + 1, :], b[c - 1:c, :]
    sq = _sig(qa)
    q = qa * sq
    qt, kt = q * jnp.exp(b - bref), kh * jnp.exp(bref - b)
    qb, kl = q * jnp.exp(b), kh * jnp.exp(blast - b)
    causal = col <= row
    return dict(lb=lb, sf=sf, forget=forget, sq=sq, qt=qt, kt=kt, qb=qb, kl=kl, decay=jnp.exp(blast),
                causal=causal, e_q=jnp.exp(b), e_qt=jnp.exp(b - bref), e_kt=jnp.exp(bref - b),
                e_kl=jnp.exp(blast - b))


def _hgrn_specs(t):
    c = HG_CHUNK
    n = t // c
    return c, n, WA // LANES, HG_STEP_CHUNKS if n % HG_STEP_CHUNKS == 0 else 1


def _hgrn_fwd(p_all, lbl, onorm, name):
    t = p_all.shape[0]
    c, n, nh, m = _hgrn_specs(t)

    def body(q_ref, f_ref, i_ref, g_ref, lbl_ref, on_ref, oa_ref, oraw_ref, st_ref, state):
        @pl.when(pl.program_id(0) == 0)
        def _():
            state[...] = jnp.zeros_like(state)

        heads = [slice(h * LANES, (h + 1) * LANES) for h in range(nh)]
        sts = [state[h] for h in range(nh)]
        for sub in range(m):
            rows = slice(sub * c, (sub + 1) * c)
            ks = [_hgrn_chunk(q_ref[rows, at], f_ref[rows, at], lbl_ref[:, at]) for at in heads]
            vs = [i_ref[rows, at] for at in heads]
            for h in range(nh):
                st_ref[h, sub] = sts[h]
            scores = [jnp.where(k["causal"], _dot(k["qt"], k["kt"], _NT, _HI), 0.0) for k in ks]
            outs = [_dot(a, v, precision=_HI) + _dot(k["qb"], st, _NT, _HI) for a, v, k, st in zip(scores, vs, ks, sts)]
            sts = [st * k["decay"] + _dot(v, k["kl"], _TN, _HI) for st, k, v in zip(sts, ks, vs)]
            for at, o in zip(heads, outs):
                oraw_ref[rows, at] = o
                rinv = lax.rsqrt(jnp.mean(o * o, axis=-1, keepdims=True) + NORM_EPS)
                ga = g_ref[rows, at]
                oa_ref[rows, at] = (o * rinv * on_ref[:, at] * (ga * _sig(ga))).astype(MXU)
        for h in range(nh):
            state[h] = sts[h]

    def blk(j):
        return pl.BlockSpec((m * c, WA), lambda i: (i, j))

    return pl.pallas_call(
        body, name=name, grid=(n // m,),
        in_specs=[blk(0), blk(1), blk(2), blk(3), pl.BlockSpec((2, WA), lambda i: (0, 0)),
                  pl.BlockSpec((1, WA), lambda i: (0, 0))],
        out_specs=[blk(0), blk(0), pl.BlockSpec((nh, m, LANES, LANES), lambda i: (0, i, 0, 0))],
        out_shape=[jax.ShapeDtypeStruct((t, WA), MXU), jax.ShapeDtypeStruct((t, WA), F32),
                   jax.ShapeDtypeStruct((nh, n, LANES, LANES), F32)],
        scratch_shapes=[pltpu.VMEM((nh, LANES, LANES), F32)], compiler_params=_params(1))(
            p_all, p_all, p_all, p_all, lbl, onorm)


def _hgrn_bwd(p_all, lbl, onorm, oraw, states, doa, name):
    t = p_all.shape[0]
    c, n, nh, m = _hgrn_specs(t)

    def body(q_ref, f_ref, i_ref, g_ref, lbl_ref, on_ref, oraw_ref, st_ref, doa_ref,
             dq_ref, df_ref, di_ref, dg_ref, don_ref, dlbl_ref, dstate, dlb):
        @pl.when(pl.program_id(0) == 0)
        def _():
            dstate[...] = jnp.zeros_like(dstate)
            dlb[...] = jnp.zeros_like(dlb)
            don_ref[...] = jnp.zeros_like(don_ref)

        heads = [slice(h * LANES, (h + 1) * LANES) for h in range(nh)]
        dsts = [dstate[h] for h in range(nh)]
        step = _iota((c, LANES), 0)
        row, col = _iota((c, c), 0), _iota((c, c), 1)
        for sub in reversed(range(m)):
            rows = slice(sub * c, (sub + 1) * c)
            work = []
            for h, at in enumerate(heads):
                qa, fa, v, ga = q_ref[rows, at], f_ref[rows, at], i_ref[rows, at], g_ref[rows, at]
                k = _hgrn_chunk(qa, fa, lbl_ref[:, at])
                o = oraw_ref[rows, at]
                gain = on_ref[:, at]
                rinv = lax.rsqrt(jnp.mean(o * o, axis=-1, keepdims=True) + NORM_EPS)
                on = o * rinv
                sg = _sig(ga)
                gate = ga * sg
                dout = doa_ref[rows, at]
                don_ref[:, at] += jnp.sum(dout * on * gate, axis=0, keepdims=True)
                dg_ref[rows, at] = (dout * on * gain * _dsilu(ga, sg)).astype(MXU)
                d_on = dout * gain * gate
                do = rinv * (d_on - on * jnp.mean(d_on * on, axis=-1, keepdims=True))
                work.append(dict(at=at, qa=qa, v=v, k=k, do=do, st=st_ref[h, sub]))
            for x, dst_next in zip(work, dsts):
                k, do = x["k"], x["do"]
                x["a"] = jnp.where(k["causal"], _dot(k["qt"], k["kt"], _NT, _HI), 0.0)
                x["dqb"] = _dot(do, x["st"], precision=_HI)
                x["dst"] = dst_next * k["decay"] + _dot(do, k["qb"], _TN, _HI)
                x["da"] = jnp.where(k["causal"], _dot(do, x["v"], _NT, _HI), 0.0)
            for x, dst_next in zip(work, dsts):
                k = x["k"]
                x["dqt"] = _dot(x["da"], k["kt"], precision=_HI)
                x["dkt"] = _dot(x["da"], k["qt"], _TN, _HI)
                x["dv"] = _dot(x["a"], x["do"], _TN, _HI) + _dot(k["kl"], dst_next, _NT, _HI)
                x["dkl"] = _dot(x["v"], dst_next, precision=_HI)
            for x, dst_next in zip(work, dsts):
                k, at, dqt, dkt, dkl, dqb = x["k"], x["at"], x["dqt"], x["dkt"], x["dkl"], x["dqb"]
                ddecay = jnp.sum(dst_next * x["st"], axis=0, keepdims=True)
                dq = dqb * k["e_q"] + dqt * k["e_qt"]
                dk = dkt * k["e_kt"] + dkl * k["e_kl"]
                tq, tk, tl = dqt * k["qt"], dkt * k["kt"], dkl * k["kl"]
                db = dqb * k["qb"] + tq - tk - tl
                dbref = jnp.sum(tk - tq, axis=0, keepdims=True)
                dblast = jnp.sum(tl, axis=0, keepdims=True) + ddecay * k["decay"]
                db = db + jnp.where(step == c // 2, dbref, 0.0) + jnp.where(step == c - 1, dblast, 0.0)
                dlogf = _dot((col >= row).astype(F32), db, precision=_HI)
                dq_ref[rows, at] = (dq * _dsilu(x["qa"], k["sq"])).astype(MXU)
                di_ref[rows, at] = x["dv"].astype(MXU)
                dforget = dlogf / k["forget"] - dk
                sf, lb = k["sf"], k["lb"]
                df_ref[rows, at] = (dforget * (1.0 - lb) * sf * (1.0 - sf)).astype(MXU)
                dlb[:, at] += jnp.sum(dforget * (1.0 - sf), axis=0, keepdims=True)
                dl0 = dlb[:, at] * lb * (1.0 - lb)
                dlbl_ref[:, at] = jnp.where(_iota((2, LANES), 0) == 0, dl0, -dl0)
            dsts = [x["dst"] for x in work]
        for h in range(nh):
            dstate[h] = dsts[h]

    last = n // m - 1

    def blk(j):
        return pl.BlockSpec((m * c, WA), lambda i: (last - i, j))

    vec = pl.BlockSpec((1, WA), lambda i: (0, 0))
    lg = pl.BlockSpec((2, WA), lambda i: (0, 0))
    grad = jax.ShapeDtypeStruct((t, WA), MXU)
    return pl.pallas_call(
        body, name=name, grid=(n // m,),
        in_specs=[blk(0), blk(1), blk(2), blk(3), lg, vec, blk(0),
                  pl.BlockSpec((nh, m, LANES, LANES), lambda i: (0, last - i, 0, 0)), blk(0)],
        out_specs=[blk(0), blk(0), blk(0), blk(0), vec, lg],
        out_shape=[grad, grad, grad, grad, jax.ShapeDtypeStruct((1, WA), F32), jax.ShapeDtypeStruct((2, WA), F32)],
        scratch_shapes=[pltpu.VMEM((nh, LANES, LANES), F32), pltpu.VMEM((1, WA), F32)],
        compiler_params=_params(1))(p_all, p_all, p_all, p_all, lbl, onorm, oraw, states, doa)


def _lora_act(x):
    lane = _iota(x.shape, 1)
    n_w, n_a, n_g = LORA
    return jnp.where(lane < n_w, jnp.tanh(x),
                     jnp.where(lane < n_w + n_a, x, jnp.where(lane < n_w + n_a + n_g, _sig(x), 0.0)))


def _lora_dact(x):
    lane = _iota(x.shape, 1)
    n_w, n_a, n_g = LORA
    th, s = jnp.tanh(x), _sig(x)
    return jnp.where(lane < n_w, 1.0 - th * th,
                     jnp.where(lane < n_w + n_a, 1.0, jnp.where(lane < n_w + n_a + n_g, s * (1.0 - s), 0.0)))


def _shift_down(cur, prev8, first):
    rolled = pltpu.roll(cur, 1, 0)
    edge = prev8[7:8, :] * jnp.where(first, 0.0, 1.0)
    return jnp.where(_iota(cur.shape, 0) == 0, edge, rolled)


def _shift_up(cur, next8, last):
    rows = cur.shape[0]
    rolled = pltpu.roll(cur, rows - 1, 0)
    edge = next8[0:1, :] * jnp.where(last, 0.0, 1.0)
    return jnp.where(_iota(cur.shape, 0) == rows - 1, edge, rolled)


def _rwkv_inputs(refs, first, ones):
    (pr, pk, pv, plo, qr, qk, qv, qlo, mr, mk, mv, mlo, w2c, w0, a0, kk_w, ka_w) = refs
    mix = lambda cur, prev, mu: cur[...] + mu[...] * (_shift_down(cur[...], prev[...], first) - cur[...])
    r, k, v, lo = mix(pr, qr, mr), mix(pk, qk, mk), mix(pv, qv, mv), mix(plo, qlo, mlo)
    z = _lora_act(lo)
    lin = _dot(z.astype(MXU), w2c[...])
    sg = _sig(w0[...] + lin[:, :WB])
    decay = jnp.exp(-DECAY_C * sg)
    a = _sig(a0[...] + lin[:, WB:2 * WB])
    g = lin[:, 2 * WB:]
    kk0 = k * kk_w[...]
    nrm = jnp.sqrt(_split_dot(kk0 * kk0, ones, 3))
    den = jnp.maximum(nrm, L2_EPS)
    kk = kk0 / den
    k2 = k * (1.0 + (a - 1.0) * ka_w[...])
    return dict(r=r, k=k, v=v, lo=lo, z=z, sg=sg, decay=decay, a=a, g=g, kk=kk, den=den, nrm=nrm, k2=k2)


def _rwkv_in_specs(t, tb):
    nt8 = tb // 8

    def cur(w, j):
        return pl.BlockSpec((tb, w), lambda i: (i, j))

    def prev(w, j):
        return pl.BlockSpec((8, w), lambda i: (jnp.maximum(i * nt8 - 1, 0), j))

    def vec(w, j=0):
        return pl.BlockSpec((1, w), lambda i: (0, j))

    return [cur(WB, COL_R), cur(WB, COL_K), cur(WB, COL_V), cur(256, COL_L),
            prev(WB, COL_R), prev(WB, COL_K), prev(WB, COL_V), prev(256, COL_L),
            vec(WB, 0), vec(WB, 1), vec(WB, 2), vec(256, 6),
            pl.BlockSpec((256, 3 * WB), lambda i: (0, 0)), vec(WB), vec(WB), vec(WB), vec(WB)]


def _rwkv_in_args(p_all, mu_pad, w2cat, w0, a0, k_k, k_a):
    return (p_all,) * 8 + (mu_pad,) * 4 + (w2cat, w0, a0, k_k, k_a)


def _rwkv_prep(p_all, mu_pad, w2cat, w0, a0, k_k, k_a, name):
    t = p_all.shape[0]
    tb = _tile(t, 256)

    def body(*refs):
        ins, outs = refs[:17], refs[17:]
        q = _rwkv_inputs(ins, pl.program_id(0) == 0, _head_ones(WB, HD_B))
        for ref, val in zip(outs, (q["r"], q["decay"], q["k2"], q["v"], -q["kk"], q["kk"] * q["a"], q["g"])):
            ref[...] = val

    out = pl.BlockSpec((tb, WB), lambda i: (i, 0))
    return pl.pallas_call(
        body, name=name, grid=(t // tb,), in_specs=_rwkv_in_specs(t, tb), out_specs=[out] * 7,
        out_shape=[jax.ShapeDtypeStruct((t, WB), F32)] * 7, compiler_params=_params(1))(
            *_rwkv_in_args(p_all, mu_pad, w2cat, w0, a0, k_k, k_a))


def _pair_rows(x8, i):
    return jnp.concatenate([jnp.broadcast_to(x8[i:i + 1, p * LANES:(p + 1) * LANES], (HD_B, LANES))
                            for p in range(4)], axis=0)


def _pair_sums(x):
    return jnp.concatenate([jnp.sum(x[p * HD_B:(p + 1) * HD_B], axis=0, keepdims=True) for p in range(4)], axis=1)


def _put_row(buf, i, row):
    return jnp.where(_iota(buf.shape, 0) == i, row, buf)


def _pieces(x):
    hi = x.astype(jnp.bfloat16).astype(F32)
    lo = (x - hi).astype(jnp.bfloat16).astype(F32)
    upper = (_iota((x.shape[0], LANES), 1) & (HD_B // 2)) != 0
    swapped = [jnp.where(upper, pltpu.roll(lo[:, p * LANES:(p + 1) * LANES], HD_B // 2, 1),
                         pltpu.roll(lo[:, p * LANES:(p + 1) * LANES], LANES - HD_B // 2, 1)) for p in range(4)]
    return hi, jnp.concatenate(swapped, axis=1)


def _scan_consts():
    row, lane = _iota((HD_B, LANES), 0), _iota((HD_B, LANES), 1) & (HD_B - 1)
    either = ((row ^ lane) & (HD_B // 2 - 1)) == 0
    return ((row ^ lane) & (HD_B // 2)) != 0, either.astype(jnp.bfloat16), _head_ones(LANES, HD_B)


def _pair_cols(many, consts):
    swapped, either, ones = consts
    tiles = []
    for (hi8, lo8), i in many:
        for p in range(4):
            lanes = slice(p * LANES, (p + 1) * LANES)
            hi = jnp.broadcast_to(hi8[i:i + 1, lanes], (16, LANES)).astype(jnp.bfloat16)
            lo = jnp.broadcast_to(lo8[i:i + 1, lanes], (16, LANES)).astype(jnp.bfloat16)
            for g in range(HD_B // 16):
                rows = slice(g * 16, (g + 1) * 16)
                tiles.append(jnp.where(swapped[rows], lo, hi) * either[rows])
    out = _dot(jnp.concatenate(tiles, axis=0), ones)
    return [out[m * 4 * HD_B:(m + 1) * 4 * HD_B] for m in range(len(many))]


def _block_products(w8):
    rows = _iota(w8.shape, 0)
    down, up = w8, w8
    for shift in (1, 2, 4):
        down = down * jnp.where(rows >= shift, pltpu.roll(down, shift, 0), 1.0)
        up = up * jnp.where(rows < 8 - shift, pltpu.roll(up, 8 - shift, 0), 1.0)
    return down, up


def _blocked_loop(n_blocks, prepare, advance, init):
    unroll = SCAN_UNROLL if n_blocks % SCAN_UNROLL == 0 else 1

    def trip(g, carry):
        prepared = [prepare(g * unroll + i) for i in range(unroll)]
        for p in prepared:
            carry = advance(p, carry)
        return carry

    return lax.fori_loop(0, n_blocks // unroll, trip, init)


def _rwkv_scan_fwd(r, w, k, v, a, b, name):
    t = r.shape[0]
    cc = min(t, SCAN_CHUNK)

    def body(r_ref, w_ref, k_ref, v_ref, a_ref, b_ref, y_ref, sa_ref, state):
        @pl.when(pl.program_id(0) == 0)
        def _():
            state[...] = jnp.zeros_like(state)

        consts = _scan_consts()

        def prepare(j):
            rows = pl.ds(pl.multiple_of(j * 8, 8), 8)
            r8, w8, k8, v8, a8, b8 = (ref[rows, :] for ref in (r_ref, w_ref, k_ref, v_ref, a_ref, b_ref))
            decay, _ = _block_products(w8)
            before = jnp.where(_iota(w8.shape, 0) == 0, 1.0, pltpu.roll(decay, 1, 0))
            inv = 1.0 / decay
            scaled = [_pieces(x) for x in (a8 * before, b8 * inv, k8 * inv, r8 * decay)]
            return rows, v8, _pair_cols([(x, i) for i in range(8) for x in scaled] + [(_pieces(decay), 7)], consts)

        def advance(prepared, sk):
            rows, v8, cols = prepared
            y8 = jnp.zeros((8, WB), F32)
            sa8 = jnp.zeros((8, WB), F32)
            for i in range(8):
                a_c, b_c, k_c, r_c = cols[4 * i:4 * i + 4]
                sa = _pair_sums(sk * a_c)
                sk = sk + b_c * _pair_rows(sa, 0) + k_c * _pair_rows(v8, i)
                y8 = _put_row(y8, i, _pair_sums(sk * r_c))
                sa8 = _put_row(sa8, i, sa)
            y_ref[rows, :] = y8
            sa_ref[rows, :] = sa8
            return sk * cols[-1]

        state[...] = _blocked_loop(cc // 8, prepare, advance, state[...])

    row = pl.BlockSpec((cc, WB), lambda i: (i, 0))
    return pl.pallas_call(
        body, name=name, grid=(t // cc,), in_specs=[row] * 6, out_specs=[row, row],
        out_shape=[jax.ShapeDtypeStruct((t, WB), F32)] * 2,
        scratch_shapes=[pltpu.VMEM((4 * HD_B, LANES), F32)], compiler_params=_params(1))(r, w, k, v, a, b)


def _rwkv_states(sa, w, k, v, b, name):
    t = sa.shape[0]
    cc = min(t, SCAN_CHUNK)

    def body(sa_ref, w_ref, k_ref, v_ref, b_ref, sall_ref, state):
        @pl.when(pl.program_id(0) == 0)
        def _():
            state[...] = jnp.zeros_like(state)

        consts = _scan_consts()

        def prepare(j):
            base = pl.multiple_of(j * 8, 8)
            sa8, w8, k8, v8, b8 = (ref[pl.ds(base, 8), :] for ref in (sa_ref, w_ref, k_ref, v_ref, b_ref))
            sap, vp = _pieces(sa8), _pieces(v8)
            return base, w8, k8, b8, _pair_cols([(x, i) for i in range(8) for x in (sap, vp)], consts)

        def advance(prepared, sv):
            base, w8, k8, b8, cols = prepared
            for i in range(8):
                sv = sv * _pair_rows(w8, i) + cols[2 * i] * _pair_rows(b8, i) + cols[2 * i + 1] * _pair_rows(k8, i)
                sall_ref[base + i] = sv
            return sv

        state[...] = _blocked_loop(cc // 8, prepare, advance, state[...])

    row = pl.BlockSpec((cc, WB), lambda i: (i, 0))
    return pl.pallas_call(
        body, name=name, grid=(t // cc,), in_specs=[row] * 5,
        out_specs=pl.BlockSpec((cc, 4 * HD_B, LANES), lambda i: (i, 0, 0)),
        out_shape=jax.ShapeDtypeStruct((t, 4 * HD_B, LANES), F32),
        scratch_shapes=[pltpu.VMEM((4 * HD_B, LANES), F32)], compiler_params=_params(1))(sa, w, k, v, b)


def _rwkv_scan_bwd(dy, r, w, k, a, b, name):
    t = r.shape[0]
    cc = min(t, SCAN_CHUNK)
    n = t // cc

    def body(dy_ref, r_ref, w_ref, k_ref, a_ref, b_ref, dsa_ref, dv_ref, dstate):
        @pl.when(pl.program_id(0) == 0)
        def _():
            dstate[...] = jnp.zeros_like(dstate)

        consts = _scan_consts()

        steps = range(7, -1, -1)

        def prepare(jj):
            rows = pl.ds(pl.multiple_of((cc // 8 - 1 - jj) * 8, 8), 8)
            dy8, r8, w8, k8, a8, b8 = (ref[rows, :] for ref in (dy_ref, r_ref, w_ref, k_ref, a_ref, b_ref))
            _, upto = _block_products(w8)
            later = jnp.where(_iota(w8.shape, 0) == 7, 1.0, pltpu.roll(upto, 7, 0))
            scaled = [_pieces(x) for x in (r8 / later, b8 * later, k8 * later, a8 / upto)]
            return rows, dy8, _pair_cols([(x, i) for i in steps for x in scaled] + [(_pieces(upto), 0)], consts)

        def advance(prepared, ds):
            rows, dy8, cols = prepared
            dsa8, dv8 = jnp.zeros((8, WB), F32), jnp.zeros((8, WB), F32)
            for n_done, i in enumerate(steps):
                r_c, b_c, k_c, a_c = cols[4 * n_done:4 * n_done + 4]
                ds = ds + r_c * _pair_rows(dy8, i)
                dsa = _pair_sums(ds * b_c)
                dv8 = _put_row(dv8, i, _pair_sums(ds * k_c))
                dsa8 = _put_row(dsa8, i, dsa)
                ds = ds + a_c * _pair_rows(dsa, 0)
            dsa_ref[rows, :] = dsa8
            dv_ref[rows, :] = dv8
            return ds * cols[-1]

        dstate[...] = _blocked_loop(cc // 8, prepare, advance, dstate[...])

    row = pl.BlockSpec((cc, WB), lambda i: (n - 1 - i, 0))
    return pl.pallas_call(
        body, name=name, grid=(n,), in_specs=[row] * 6, out_specs=[row] * 2,
        out_shape=[jax.ShapeDtypeStruct((t, WB), F32)] * 2,
        scratch_shapes=[pltpu.VMEM((4 * HD_B, LANES), F32)], compiler_params=_params(1))(dy, r, w, k, a, b)


def _rwkv_scan_bwd_values(dy, r, w, v, a, sa, dsa, sall, name):
    t = r.shape[0]
    cc = min(t, SCAN_CHUNK)
    n = t // cc

    def body(dy_ref, r_ref, w_ref, v_ref, a_ref, sa_ref, dsa_ref, sall_ref, sprev_ref,
             dr_ref, dw_ref, dk_ref, da_ref, db_ref, dstate):
        @pl.when(pl.program_id(0) == 0)
        def _():
            dstate[...] = jnp.zeros_like(dstate)

        consts = _scan_consts()
        before_chunk = jnp.where(pl.program_id(0) == n - 1, 0.0, 1.0) * sprev_ref[0]

        steps = range(7, -1, -1)

        def prepare(jj):
            j = cc // 8 - 1 - jj
            base = pl.multiple_of(j * 8, 8)
            dy8, r8, w8, v8, a8, sa8, dsa8 = (ref[pl.ds(base, 8), :] for ref in
                                              (dy_ref, r_ref, w_ref, v_ref, a_ref, sa_ref, dsa_ref))
            dyp, vp, sap, dsap = (_pieces(x) for x in (dy8, v8, sa8, dsa8))
            return j, base, r8, w8, a8, _pair_cols([(x, i) for i in steps for x in (dyp, vp, sap, dsap)], consts)

        def advance(prepared, carry):
            ds, sc = carry
            j, base, r8, w8, a8, cols = prepared
            rows = pl.ds(base, 8)
            outs = [jnp.zeros((8, WB), F32) for _ in range(5)]
            for n_done, i in enumerate(steps):
                if i > 0:
                    sp = sall_ref[base + i - 1]
                else:
                    sp = jnp.where(j == 0, before_chunk, sall_ref[jnp.maximum(base - 1, 0)])
                dy_c, v_c, sa_c, dsa_c = cols[4 * n_done:4 * n_done + 4]
                ds = ds + dy_c * _pair_rows(r8, i)
                vals = (_pair_sums(sc * dy_c), _pair_sums(ds * sp), _pair_sums(ds * v_c),
                        _pair_sums(sp * dsa_c), _pair_sums(ds * sa_c))
                outs = [_put_row(o, i, val) for o, val in zip(outs, vals)]
                ds = ds * _pair_rows(w8, i) + dsa_c * _pair_rows(a8, i)
                sc = sp
            for ref, o in zip((dr_ref, dw_ref, dk_ref, da_ref, db_ref), outs):
                ref[rows, :] = o
            return ds, sc

        ds, _ = _blocked_loop(cc // 8, prepare, advance, (dstate[...], sall_ref[cc - 1]))
        dstate[...] = ds

    row = pl.BlockSpec((cc, WB), lambda i: (n - 1 - i, 0))
    return pl.pallas_call(
        body, name=name, grid=(n,),
        in_specs=[row] * 7 + [pl.BlockSpec((cc, 4 * HD_B, LANES), lambda i: (n - 1 - i, 0, 0)),
                              pl.BlockSpec((1, 4 * HD_B, LANES), lambda i: (jnp.maximum((n - 1 - i) * cc - 1, 0), 0, 0))],
        out_specs=[row] * 5, out_shape=[jax.ShapeDtypeStruct((t, WB), F32)] * 5,
        scratch_shapes=[pltpu.VMEM((4 * HD_B, LANES), F32)], compiler_params=_params(1))(
            dy, r, w, v, a, sa, dsa, sall, sall)


def _rwkv_post(y, r, k2, v, g, r_k, gn_w, gn_b, name):
    t = y.shape[0]
    tb = _tile(t, 256)

    def body(y_ref, r_ref, k_ref, v_ref, g_ref, rk_ref, gw_ref, gb_ref, o_ref):
        ones = _head_ones(WB, HD_B)
        yv = y_ref[...]
        yc = yv - _split_dot(yv, ones, 3) * (1.0 / HD_B)
        rstd = lax.rsqrt(_split_dot(yc * yc, ones, 3) * (1.0 / HD_B) + GN_EPS)
        rk = _split_dot(r_ref[...] * k_ref[...] * rk_ref[...], ones, 3)
        o_ref[...] = ((yc * rstd * gw_ref[...] + gb_ref[...] + rk * v_ref[...]) * g_ref[...]).astype(MXU)

    row = pl.BlockSpec((tb, WB), lambda i: (i, 0))
    vec = pl.BlockSpec((1, WB), lambda i: (0, 0))
    return pl.pallas_call(
        body, name=name, grid=(t // tb,), in_specs=[row] * 5 + [vec] * 3, out_specs=row,
        out_shape=jax.ShapeDtypeStruct((t, WB), MXU), compiler_params=_params(1))(y, r, k2, v, g, r_k, gn_w, gn_b)


def _rwkv_post_bwd(dob, y, r, k2, v, g, r_k, gn_w, gn_b, name):
    t = y.shape[0]
    tb = _tile(t, 256)

    def body(do_ref, y_ref, r_ref, k_ref, v_ref, g_ref, rk_ref, gw_ref, gb_ref,
             dy_ref, dg_ref, dr_ref, dk_ref, dv_ref, dgw_ref, dgb_ref, drk_ref):
        @pl.when(pl.program_id(0) == 0)
        def _():
            dgw_ref[...] = jnp.zeros_like(dgw_ref)
            dgb_ref[...] = jnp.zeros_like(dgb_ref)
            drk_ref[...] = jnp.zeros_like(drk_ref)

        ones = _head_ones(WB, HD_B)
        seg = lambda x: _split_dot(x, ones, 3)
        yv, rv, kv, vv, gv = y_ref[...], r_ref[...], k_ref[...], v_ref[...], g_ref[...]
        yc = yv - seg(yv) * (1.0 / HD_B)
        rstd = lax.rsqrt(seg(yc * yc) * (1.0 / HD_B) + GN_EPS)
        yn = yc * rstd
        rk = seg(rv * kv * rk_ref[...])
        dob_v = do_ref[...]
        dg_ref[...] = dob_v * (yn * gw_ref[...] + gb_ref[...] + rk * vv)
        dyg = dob_v * gv
        dgw_ref[...] += jnp.sum(dyg * yn, axis=0, keepdims=True)
        dgb_ref[...] += jnp.sum(dyg, axis=0, keepdims=True)
        dyn = dyg * gw_ref[...]
        dy_ref[...] = rstd * (dyn - (seg(dyn) + yn * seg(dyn * yn)) * (1.0 / HD_B))
        drk = seg(dyg * vv)
        dv_ref[...] = dyg * rk
        dr_ref[...] = drk * kv * rk_ref[...]
        dk_ref[...] = drk * rv * rk_ref[...]
        drk_ref[...] += jnp.sum(drk * rv * kv, axis=0, keepdims=True)

    row = pl.BlockSpec((tb, WB), lambda i: (i, 0))
    vec = pl.BlockSpec((1, WB), lambda i: (0, 0))
    full, small = jax.ShapeDtypeStruct((t, WB), F32), jax.ShapeDtypeStruct((1, WB), F32)
    return pl.pallas_call(
        body, name=name, grid=(t // tb,),
        in_specs=[pl.BlockSpec((tb, WB), lambda i: (i, dob.shape[1] // WB - 1))] + [row] * 5 + [vec] * 3,
        out_specs=[row] * 5 + [vec] * 3,
        out_shape=[full] * 5 + [small] * 3, compiler_params=_params(1))(dob, y, r, k2, v, g, r_k, gn_w, gn_b)


def _rwkv_prep_bwd(grads, p_all, mu_pad, w2cat, w0, a0, k_k, k_a, name):
    t = p_all.shape[0]
    tb = _tile(t, 256)

    def body(*refs):
        g_refs, ins, outs = refs[:10], refs[10:27], refs[27:]
        dr_s, dw, dk2_s, dv_s, das, dbs, dg, dr_b, dk2_b, dv_b = (ref[...] for ref in g_refs)
        dr_ref, dk_ref, dv_ref, dlo_ref, dw2_ref, dw0_ref, da0_ref, dkk_ref, dka_ref = outs

        @pl.when(pl.program_id(0) == 0)
        def _():
            for ref in (dw2_ref, dw0_ref, da0_ref, dkk_ref, dka_ref):
                ref[...] = jnp.zeros_like(ref)

        ones = _head_ones(WB, HD_B)
        q = _rwkv_inputs(ins, pl.program_id(0) == 0, ones)
        kk_w, ka_w = ins[15][...], ins[16][...]
        a, kk, k = q["a"], q["kk"], q["k"]
        dk2 = dk2_s + dk2_b
        dkk = dbs * a - das
        da = dbs * kk + dk2 * k * ka_w
        dk = dk2 * (1.0 + (a - 1.0) * ka_w)
        dka_ref[...] += jnp.sum(dk2 * k * (a - 1.0), axis=0, keepdims=True)
        proj = jnp.where(q["nrm"] > L2_EPS, _split_dot(dkk * kk, ones, 3), 0.0)
        dkk0 = (dkk - kk * proj) / q["den"]
        dk = dk + dkk0 * kk_w
        dkk_ref[...] += jnp.sum(dkk0 * k, axis=0, keepdims=True)
        dal = da * a * (1.0 - a)
        da0_ref[...] += jnp.sum(dal, axis=0, keepdims=True)
        sg = q["sg"]
        dwl = dw * q["decay"] * (-DECAY_C) * sg * (1.0 - sg)
        dw0_ref[...] += jnp.sum(dwl, axis=0, keepdims=True)
        dlin = jnp.concatenate([dwl, dal, dg], axis=1).astype(MXU)
        dw2_ref[...] += _dot(q["z"].astype(MXU), dlin, _TN)
        dz = _dot(dlin, ins[12][...], _NT)
        dlo_ref[...] = dz * _lora_dact(q["lo"])
        dr_ref[...] = dr_s + dr_b
        dk_ref[...] = dk
        dv_ref[...] = dv_s + dv_b

    row = pl.BlockSpec((tb, WB), lambda i: (i, 0))
    vec = pl.BlockSpec((1, WB), lambda i: (0, 0))
    full, small = jax.ShapeDtypeStruct((t, WB), F32), jax.ShapeDtypeStruct((1, WB), F32)
    return pl.pallas_call(
        body, name=name, grid=(t // tb,), in_specs=[row] * 10 + _rwkv_in_specs(t, tb),
        out_specs=[row] * 3 + [pl.BlockSpec((tb, 256), lambda i: (i, 0)),
                               pl.BlockSpec((256, 3 * WB), lambda i: (0, 0))] + [vec] * 4,
        out_shape=[full] * 3 + [jax.ShapeDtypeStruct((t, 256), F32), jax.ShapeDtypeStruct((256, 3 * WB), F32)]
        + [small] * 4, compiler_params=_params(1))(*grads, *_rwkv_in_args(p_all, mu_pad, w2cat, w0, a0, k_k, k_a))


def _shift_bwd(dshifted, p_all, mu_pad, name):
    t = p_all.shape[0]
    tb = _tile(t, 256)
    nt, nt8 = t // tb, tb // 8
    widths, cols, mus = (WB, WB, WB, 256), (COL_R, COL_K, COL_V, COL_L), (0, 1, 2, 6)

    def body(*refs):
        d_refs, n_refs, p_refs, q_refs, m_refs = refs[0:4], refs[4:8], refs[8:12], refs[12:16], refs[16:20]
        o_refs, dmu_refs = refs[20:24], refs[24:28]
        i = pl.program_id(0)

        @pl.when(i == 0)
        def _():
            for ref in dmu_refs:
                ref[...] = jnp.zeros_like(ref)

        for d, nx, p, q, m, o, dmu in zip(d_refs, n_refs, p_refs, q_refs, m_refs, o_refs, dmu_refs):
            dv, pv, mu = d[...], p[...], m[...]
            o[...] = (dv * (1.0 - mu) + mu * _shift_up(dv, nx[...], i == nt - 1)).astype(MXU)
            dmu[...] += jnp.sum(dv * (_shift_down(pv, q[...], i == 0) - pv), axis=0, keepdims=True)

    cur_d = [pl.BlockSpec((tb, w), lambda i: (i, 0)) for w in widths]
    next_d = [pl.BlockSpec((8, w), lambda i: (jnp.minimum((i + 1) * nt8, t // 8 - 1), 0)) for w in widths]
    cur_p = [pl.BlockSpec((tb, w), lambda i, j=j: (i, j)) for w, j in zip(widths, cols)]
    prev_p = [pl.BlockSpec((8, w), lambda i, j=j: (jnp.maximum(i * nt8 - 1, 0), j)) for w, j in zip(widths, cols)]
    mu_s = [pl.BlockSpec((1, w), lambda i, j=j: (0, j)) for w, j in zip(widths, mus)]
    vecs = [pl.BlockSpec((1, w), lambda i: (0, 0)) for w in widths]
    return pl.pallas_call(
        body, name=name, grid=(nt,), in_specs=cur_d + next_d + cur_p + prev_p + mu_s, out_specs=cur_d + vecs,
        out_shape=[jax.ShapeDtypeStruct((t, w), MXU) for w in widths]
        + [jax.ShapeDtypeStruct((1, w), F32) for w in widths],
        compiler_params=_params(1))(*dshifted, *dshifted, *(p_all,) * 8, *(mu_pad,) * 4)


def _peer(k):
    x, y, c = (lax.axis_index(n) for n in AXES)
    px = 1 - x if k & 4 else x
    py = 1 - y if k & 2 else y
    pc = 1 - c if k & 1 else c
    return (px, py, pc), 4 * px + 2 * py + pc


def _exchange_copy(src_refs, land_refs, send_sems, recv_sems, per_peer, j, k, arriving):
    _, me = _peer(0)
    peer, idx = _peer(k)
    sem = j * (N_DEV - 1) + k - 1
    return pltpu.make_async_remote_copy(
        src_ref=src_refs[j].at[idx] if per_peer[j] else src_refs[j],
        dst_ref=land_refs[j].at[idx if arriving else me],
        send_sem=send_sems.at[sem], recv_sem=recv_sems.at[sem],
        device_id=peer, device_id_type=pl.DeviceIdType.MESH)


def _exchange_start(srcs, per_peer, name, after=()):
    n = len(srcs)
    shapes = [tuple(s.shape[1:]) if pp else tuple(s.shape) for s, pp in zip(srcs, per_peer)]
    pairs = [(j, k) for k in range(1, N_DEV) for j in range(n)]
    first_out = 2 * n + len(after)

    def body(*refs):
        src_refs, land_refs, (send_sems, recv_sems), token = refs[:n], refs[n:2 * n], refs[first_out:first_out + 2], refs[-1]
        for j, k in pairs:
            _exchange_copy(src_refs, land_refs, send_sems, recv_sems, per_peer, j, k, False).start()
        token[...] = jnp.zeros_like(token)

    hbm, sem = pl.BlockSpec(memory_space=pltpu.HBM), pl.BlockSpec(memory_space=pltpu.SEMAPHORE)
    lands = [lax.empty((N_DEV,) + shp, s.dtype) for shp, s in zip(shapes, srcs)]
    operands = [pltpu.with_memory_space_constraint(a, pltpu.HBM) for a in list(srcs) + lands]
    n_sems = n * (N_DEV - 1)
    out = pl.pallas_call(
        body, name=name, in_specs=[hbm] * (2 * n) + [pl.BlockSpec(memory_space=pl.ANY)] * len(after),
        out_specs=[sem, sem] + [hbm] * (2 * n) + [pl.BlockSpec(memory_space=pltpu.VMEM)],
        out_shape=[pltpu.SemaphoreType.DMA((n_sems,)), pltpu.SemaphoreType.DMA((n_sems,))]
        + [pltpu.HBM(a.shape, a.dtype) for a in operands] + [jax.ShapeDtypeStruct((8, LANES), F32)],
        input_output_aliases={j: 2 + j for j in range(2 * n)},
        compiler_params=pltpu.CompilerParams(has_side_effects=pltpu.SideEffectType.DATAFLOW_SIDE_EFFECTING))(
            *operands, *after)
    return (out[0], out[1], out[2:2 + n], out[2 + n:2 + 2 * n], per_peer), out[-1]


def _exchange_wait(handle, after, name):
    send_sems, recv_sems, srcs, lands, per_peer = handle
    n = len(srcs)
    pairs = [(j, k) for k in range(1, N_DEV) for j in range(n)]

    def body(*refs):
        src_refs, land_refs, (send_sems, recv_sems) = refs[:n], refs[n:2 * n], refs[2 * n:2 * n + 2]
        for j, k in pairs:
            _exchange_copy(src_refs, land_refs, send_sems, recv_sems, per_peer, j, k, False).wait_send()
            _exchange_copy(src_refs, land_refs, send_sems, recv_sems, per_peer, j, k, True).wait_recv()

    hbm, sem = pl.BlockSpec(memory_space=pltpu.HBM), pl.BlockSpec(memory_space=pltpu.SEMAPHORE)
    out = pl.pallas_call(
        body, name=name, in_specs=[hbm] * (2 * n) + [sem, sem, pl.BlockSpec(memory_space=pl.ANY)],
        out_specs=[hbm] * (2 * n), out_shape=[pltpu.HBM(a.shape, a.dtype) for a in list(srcs) + list(lands)],
        input_output_aliases={j: j for j in range(2 * n)},
        compiler_params=pltpu.CompilerParams(has_side_effects=pltpu.SideEffectType.DATAFLOW_SIDE_EFFECTING))(
            *srcs, *lands, send_sems, recv_sems, after)
    return out[n:]


def _adam_update(g, w, m, v):
    c1, c2 = 1.0 - ADAM_B1 ** ADAM_STEP, 1.0 - ADAM_B2 ** ADAM_STEP
    nm = ADAM_B1 * m + (1.0 - ADAM_B1) * g
    nv = ADAM_B2 * v + (1.0 - ADAM_B2) * (g * g)
    return -ADAM_LR * ((nm / c1) / (jnp.sqrt(nv / c2) + ADAM_EPS) + ADAM_WD * w), nm, nv


def _row_tile(rows, cols):
    padded = -(-cols // LANES) * LANES
    cap = max(16, ADAM_BLOCK_BYTES // (N_DEV * padded * 4))
    best = 16
    for t in range(16, min(rows, cap) + 1, 16):
        if rows % t == 0:
            best = t
    return best


def _adamw(parts, w, m, v, name):
    _, rows, cols = w.shape
    tb = _row_tile(rows, cols)

    def body(p_ref, w_ref, m_ref, v_ref, g_ref, d_ref, nm_ref, nv_ref):
        g = p_ref[0].astype(F32)
        for d in range(1, N_DEV):
            g = g + p_ref[d].astype(F32)
        g_ref[0] = g
        d_ref[0], nm_ref[0], nv_ref[0] = _adam_update(g, w_ref[0], m_ref[0], v_ref[0])

    row = pl.BlockSpec((1, tb, cols), lambda i: (0, i, 0))
    out = jax.ShapeDtypeStruct(w.shape, F32)
    return pl.pallas_call(
        body, name=name, grid=(rows // tb,),
        in_specs=[pl.BlockSpec((N_DEV, tb, cols), lambda i: (0, i, 0)), row, row, row], out_specs=[row] * 4,
        out_shape=[out] * 4, compiler_params=_params(1))(parts, w, m, v)


def _adamw_small(parts, ws, ms, vs, name):
    n = len(ws)

    def body(*refs):
        p_ref = refs[0]
        w_refs, m_refs, v_refs = refs[1:1 + n], refs[1 + n:1 + 2 * n], refs[1 + 2 * n:1 + 3 * n]
        outs = refs[1 + 3 * n:]
        base = 0
        for j in range(n):
            rows, cols = ws[j].shape
            size = rows * cols
            for ch in range(-(-size // LANES)):
                r, c0 = divmod(ch * LANES, cols)
                width = min(LANES, cols - c0)
                g = p_ref[0, base + ch:base + ch + 1, 0:width]
                for d in range(1, N_DEV):
                    g = g + p_ref[d, base + ch:base + ch + 1, 0:width]
                at = (slice(r, r + 1), slice(c0, c0 + width))
                delta, nm, nv = _adam_update(g, w_refs[j][at], m_refs[j][at], v_refs[j][at])
                for out, val in zip((outs[j], outs[n + j], outs[2 * n + j], outs[3 * n + j]), (g, delta, nm, nv)):
                    out[at] = val
            base += -(-size // (8 * LANES)) * 8

    vmem = pl.BlockSpec(memory_space=pltpu.VMEM)
    res = pl.pallas_call(
        body, name=name, in_specs=[vmem] * (1 + 3 * n), out_specs=[vmem] * (4 * n),
        out_shape=[jax.ShapeDtypeStruct(a.shape, F32) for a in ws] * 4)(parts, *ws, *ms, *vs)
    return res[:n], res[n:2 * n], res[2 * n:3 * n], res[3 * n:]


def _rows(a, multiple):
    flat = a.reshape(-1)
    pad = -flat.shape[0] % (multiple * LANES)
    if pad:
        flat = jnp.concatenate([flat, jnp.zeros((pad,), a.dtype)])
    return flat.reshape(-1, LANES)


def _pack(arrs, multiple):
    return jnp.concatenate([_rows(a, multiple) for a in arrs], axis=0)


def _gathered_to_full(g, name, shard_shape):
    g = g.reshape((N_DEV,) + shard_shape)
    if name in COL_SHARDED:
        return jnp.transpose(g, (1, 0, 2)).reshape(shard_shape[0], N_DEV * shard_shape[1])
    return g.reshape(N_DEV * shard_shape[0], shard_shape[1])


def _full_to_per_device(full, name):
    if name in COL_SHARDED:
        r, c = full.shape
        return jnp.transpose(full.reshape(r, N_DEV, c // N_DEV), (1, 0, 2))
    return full.reshape(N_DEV, full.shape[0] // N_DEV, full.shape[1])


def _w2cat(w2, a2, g2):
    n_w, n_a, n_g = LORA
    out = jnp.zeros((256, 3 * WB), w2.dtype)
    out = out.at[0:n_w, 0:WB].set(w2)
    out = out.at[n_w:n_w + n_a, WB:2 * WB].set(a2)
    return out.at[n_w + n_a:n_w + n_a + n_g, 2 * WB:].set(g2)


class _Local:
    def __init__(self, w):
        self.w = w

    def weights(self, group, after=None):
        return self.w

    def started(self):
        return ()

    def send(self, grads, names):
        return ()


class _Overlapped:
    GROUPS = {"ffn1": ("ffn1_w_gate", "ffn1_w_up", "ffn1_w_down"),
              "mixer_in": ("w_in", "rwkv_w2", "rwkv_a2", "rwkv_g2"),
              "late": ("w_out", "ffn2_w_gate", "ffn2_w_up", "ffn2_w_down")}

    def __init__(self, wts):
        x, y, c = (lax.axis_index(n) for n in AXES)
        self.wts, self.me, self.gathers, self.sends = wts, 4 * x + 2 * y + c, {}, []
        self._gather("ffn1", ())

    def _gather(self, group, after):
        names = self.GROUPS[group]
        shards = [self.wts[n].astype(MXU) for n in names]
        handle, token = _exchange_start(shards, [False] * len(names), "gather_" + group, after)
        self.gathers[group] = (names, shards, handle, token)
        self.newest = token

    def started(self):
        return (self.newest,)

    def _own_slot(self, land, mine):
        return lax.dynamic_update_slice(land, mine[None], (self.me,) + (0,) * mine.ndim)

    def weights(self, group, after=None):
        names, shards, handle, token = self.gathers[group]
        lands = _exchange_wait(handle, token if after is None else after, "gathered_" + group)
        w = {n: _gathered_to_full(self._own_slot(land, own), n, own.shape[1:])
             for n, own, land in zip(names, shards, lands)}
        order = list(self.GROUPS)
        if group != order[-1]:
            self._gather(order[order.index(group) + 1], (w[names[0]],))
        if group == "ffn1":
            for n in SMALL:
                keep = n in ("hgrn_lb_logits", "rwkv_r_k", "final_norm")
                w[n] = self.wts[n] if keep else self.wts[n].reshape(1, -1)
        return w

    def send(self, grads, names, small=None):
        contrib = [_full_to_per_device(grads[n], n).astype(WIRE) for n in names]
        per_peer = [True] * len(names)
        if small is not None:
            names, contrib, per_peer = names + ("small",), contrib + [small], per_peer + [False]
        handle, token = _exchange_start(contrib, per_peer, "scatter_" + names[0])
        self.sends.append((names, contrib, per_peer, handle))
        self.last_token = token
        return (token,)

    def received(self, which, after):
        names, contrib, per_peer, handle = self.sends[which]
        lands = _exchange_wait(handle, after, "scattered_" + names[0])
        parts = {}
        for n, own, pp, land in zip(names, contrib, per_peer, lands):
            mine = lax.dynamic_index_in_dim(own, self.me, 0, keepdims=False) if pp else own
            parts[n] = self._own_slot(land, mine)
        return parts


def _local_step(x, target, net):
    n_w, n_a, n_g = LORA
    w = dict(net.weights("ffn1"))
    h1 = _rms_fwd(x, w["ffn1_norm"], "ffn1_norm")
    x1 = _ffn_fwd(x, h1, w["ffn1_w_gate"], w["ffn1_w_up"], w["ffn1_w_down"], "ffn1_fwd", after=net.started())
    w.update(net.weights("mixer_in", x1))
    w_in_pad = jnp.pad(w["w_in"], ((0, 0), (0, N_INP - N_IN)))
    mu_pad = jnp.pad(w["rwkv_shift_mu"], ((0, 0), (0, 1792 - 1696)))
    w2cat = _w2cat(w["rwkv_w2"], w["rwkv_a2"], w["rwkv_g2"])
    r_k = w["rwkv_r_k"].reshape(1, WB)
    rw = (mu_pad, w2cat, w["rwkv_w0"], w["rwkv_a0"], w["rwkv_k_k"], w["rwkv_k_a"])

    h2 = _rms_fwd(x1, w["mix_norm"], "mix_norm")
    p_all = _matmul(h2, w_in_pad, after=net.started(), name="in_proj")
    oa, oraw, states = _hgrn_fwd(p_all, w["hgrn_lb_logits"], w["hgrn_out_norm"], "hgrn_fwd")
    r, decay, k2, v, sa, sb, g = _rwkv_prep(p_all, *rw, "rwkv_prep")
    y, s_a = _rwkv_scan_fwd(r, decay, k2, v, sa, sb, "rwkv_scan_fwd")
    sall = _rwkv_states(s_a, decay, k2, v, sb, "rwkv_states")
    post_w = (r_k, w["rwkv_gn_w"], w["rwkv_gn_b"])
    ob = _rwkv_post(y, r, k2, v, g, *post_w, "rwkv_post")
    w.update(net.weights("late", ob))
    o = jnp.concatenate([oa, ob], axis=1)
    x2 = _matmul(o, w["w_out"], res=x1, name="out_proj")
    h3 = _rms_fwd(x2, w["ffn2_norm"], "ffn2_norm")
    x3 = _ffn_fwd(x2, h3, w["ffn2_w_gate"], w["ffn2_w_up"], w["ffn2_w_down"], "ffn2_fwd")
    loss, dx3, d_final = _loss_head(x3, w["final_norm"].reshape(1, D), target, "loss_head")

    grads = {"final_norm": d_final.reshape(D)}

    def ffn_back(prefix, h, dy, x_in, norm):
        wg, wu, wd = (w[prefix + s] for s in ("_w_gate", "_w_up", "_w_down"))
        act, dgate, dup, dout = _ffn_bwd(h, dy, wg, wu, wd, prefix + "_bwd")
        dh = _matmul(dup, wu, tb=True, res=_matmul(dgate, wg, tb=True, name=prefix + "_dh_gate"), name=prefix + "_dh")
        sent = ()
        for which, a_op, b_op in (("_w_gate", h, dgate), ("_w_up", h, dup), ("_w_down", act, dout)):
            grads[prefix + which] = _matmul(a_op, b_op, ta=True, out_dtype=WIRE, after=sent, name=prefix + "_d" + which)
            sent = net.send(grads, (prefix + which,))
        dx, grads[prefix + "_norm"] = _rms_bwd(x_in, norm, dh, dy, prefix + "_norm_bwd", after=sent)
        return dx

    dx2 = ffn_back("ffn2", h3, dx3, x2, w["ffn2_norm"])
    grads["w_out"] = _matmul(o, dx2, ta=True, out_dtype=WIRE, name="d_w_out")
    sent = net.send(grads, ("w_out",))
    do = _matmul(dx2, w["w_out"], tb=True, after=sent, name="d_mixed")
    dqa, dfa, dia, dga, grads["hgrn_out_norm"], grads["hgrn_lb_logits"] = _hgrn_bwd(
        p_all, w["hgrn_lb_logits"], w["hgrn_out_norm"], oraw, states, do, "hgrn_bwd")
    dy, dg, dr_b, dk2_b, dv_b, grads["rwkv_gn_w"], grads["rwkv_gn_b"], d_rk = _rwkv_post_bwd(
        do, y, r, k2, v, g, *post_w, "rwkv_post_bwd")
    grads["rwkv_r_k"] = d_rk.reshape(w["rwkv_r_k"].shape)
    d_sa, dv = _rwkv_scan_bwd(dy, r, decay, k2, sa, sb, "rwkv_scan_bwd")
    dr, dw, dk2, dsa, dsb = _rwkv_scan_bwd_values(dy, r, decay, v, sa, s_a, d_sa, sall, "rwkv_scan_bwd_values")
    (dsr, dsk, dsv, dslo, dw2cat, grads["rwkv_w0"], grads["rwkv_a0"], grads["rwkv_k_k"],
     grads["rwkv_k_a"]) = _rwkv_prep_bwd((dr, dw, dk2, dv, dsa, dsb, dg, dr_b, dk2_b, dv_b), p_all, *rw,
                                         "rwkv_prep_bwd")
    grads["rwkv_w2"] = dw2cat[0:n_w, 0:WB]
    grads["rwkv_a2"] = dw2cat[n_w:n_w + n_a, WB:2 * WB]
    grads["rwkv_g2"] = dw2cat[n_w + n_a:n_w + n_a + n_g, 2 * WB:]
    dpr, dpk, dpv, dplo, dmu_r, dmu_k, dmu_v, dmu_lo = _shift_bwd((dsr, dsk, dsv, dslo), p_all, mu_pad, "shift_bwd")
    grads["rwkv_shift_mu"] = jnp.concatenate([dmu_r, dmu_k, dmu_v, dmu_lo], axis=1)[:, :1696]
    dp = jnp.concatenate([dqa, dfa, dia, dga, dpr, dpk, dpv, dplo], axis=1)
    grads["w_in"] = _matmul(h2, dp, ta=True, out_dtype=WIRE, name="d_w_in")[:, :N_IN]
    sent = net.send(grads, ("w_in", "rwkv_w2", "rwkv_a2", "rwkv_g2"))
    dh2 = _matmul(dp, w_in_pad, tb=True, after=sent, name="d_h2")
    dx1, grads["mix_norm"] = _rms_bwd(x1, w["mix_norm"], dh2, dx2, "mix_norm_bwd")
    dx0 = ffn_back("ffn1", h1, dx1, x, w["ffn1_norm"])
    return loss[0, 0], dx0, grads


def kernel(x, ffn1_norm, ffn1_w_gate, ffn1_w_up, ffn1_w_down, mix_norm, w_in, hgrn_lb_logits, hgrn_out_norm, rwkv_shift_mu, rwkv_w0, rwkv_w2, rwkv_a0, rwkv_a2, rwkv_g2, rwkv_k_k, rwkv_k_a, rwkv_r_k, rwkv_gn_w, rwkv_gn_b, w_out, ffn2_norm, ffn2_w_gate, ffn2_w_up, ffn2_w_down, final_norm, loss_target, m_ffn1_norm, m_ffn1_w_gate, m_ffn1_w_up, m_ffn1_w_down, m_mix_norm, m_w_in, m_hgrn_lb_logits, m_hgrn_out_norm, m_rwkv_shift_mu, m_rwkv_w0, m_rwkv_w2, m_rwkv_a0, m_rwkv_a2, m_rwkv_g2, m_rwkv_k_k, m_rwkv_k_a, m_rwkv_r_k, m_rwkv_gn_w, m_rwkv_gn_b, m_w_out, m_ffn2_norm, m_ffn2_w_gate, m_ffn2_w_up, m_ffn2_w_down, m_final_norm, v_ffn1_norm, v_ffn1_w_gate, v_ffn1_w_up, v_ffn1_w_down, v_mix_norm, v_w_in, v_hgrn_lb_logits, v_hgrn_out_norm, v_rwkv_shift_mu, v_rwkv_w0, v_rwkv_w2, v_rwkv_a0, v_rwkv_a2, v_rwkv_g2, v_rwkv_k_k, v_rwkv_k_a, v_rwkv_r_k, v_rwkv_gn_w, v_rwkv_gn_b, v_w_out, v_ffn2_norm, v_ffn2_w_gate, v_ffn2_w_up, v_ffn2_w_down, v_final_norm):
    args = dict(locals())
    wts = {n: args[n] for n in WEIGHTS}
    mom = {n: args["m_" + n] for n in WEIGHTS}
    var = {n: args["v_" + n] for n in WEIGHTS}
    net = _Overlapped(wts)
    loss, grad_x, grads = _local_step(x[0], loss_target[0], net)
    loss = lax.psum(loss, AXES)
    after, = net.send(grads, (), _pack([grads[n] for n in SMALL], 8))

    new = {}
    two_d = lambda a: a if a.ndim == 2 else a.reshape(1, -1)
    for which in range(len(net.sends)):
        for n, part in net.received(which, after).items():
            if n == "small":
                small = _adamw_small(part, *([two_d(src[k]) for k in SMALL] for src in (wts, mom, var)), "adamw_small")
                for j, k in enumerate(SMALL):
                    new[k] = [res[j].reshape(wts[k].shape) for res in small]
            else:
                new[n] = _adamw(part, wts[n], mom[n], var[n], "adamw_" + n)
                after = new[n][1]
    return (loss, grad_x[None], *[new[n][0] for n in WEIGHTS], *[new[n][1] for n in WEIGHTS],
            *[new[n][2] for n in WEIGHTS], *[new[n][3] for n in WEIGHTS])
```

```python
import functools
import math

import jax
import jax.numpy as jnp
from jax import lax
from jax.experimental import pallas as pl
from jax.experimental.pallas import tpu as pltpu

F32 = jnp.float32
MXU = jnp.bfloat16
WIRE = jnp.bfloat16
D = 1024
FF = 2816
WA = 512
WB = 512
HD_B = 64
N_IN = 3744
N_INP = 3840
COL_R, COL_K, COL_V = 4, 5, 6
COL_L = 14
LORA = (32, 32, 96)
HG_CHUNK = 64
HG_STEP_CHUNKS = 4
SCAN_CHUNK = 64
SCAN_UNROLL = 8
NORM_EPS = 1e-6
GN_EPS = 64e-5
L2_EPS = 1e-12
DECAY_C = math.exp(-0.5)
N_DEV = 8
LANES = 128
ADAM_BLOCK_BYTES = 4 * 1024 * 1024
MATMUL_BLOCK_BYTES = 40 * 1024 * 1024
VMEM_LIMIT = 56 * 1024 * 1024
ADAM_LR, ADAM_B1, ADAM_B2, ADAM_EPS, ADAM_WD, ADAM_STEP = 0.001, 0.9, 0.999, 1e-08, 0.01, 10
AXES = ("x", "y", "c")

SHARDED = ("ffn1_w_gate", "ffn1_w_up", "ffn1_w_down", "w_in", "rwkv_w2", "rwkv_a2", "rwkv_g2", "w_out",
           "ffn2_w_gate", "ffn2_w_up", "ffn2_w_down")
COL_SHARDED = {"rwkv_w2", "rwkv_a2", "rwkv_g2"}
SENT_TRANSPOSED = {"ffn1_w_gate", "ffn1_w_up", "w_in", "ffn2_w_gate", "ffn2_w_up"}
SMALL = ("ffn1_norm", "mix_norm", "hgrn_lb_logits", "hgrn_out_norm", "rwkv_shift_mu", "rwkv_w0", "rwkv_a0",
         "rwkv_k_k", "rwkv_k_a", "rwkv_r_k", "rwkv_gn_w", "rwkv_gn_b", "ffn2_norm", "final_norm")
WEIGHTS = ("ffn1_norm", "ffn1_w_gate", "ffn1_w_up", "ffn1_w_down", "mix_norm", "w_in", "hgrn_lb_logits",
           "hgrn_out_norm", "rwkv_shift_mu", "rwkv_w0", "rwkv_w2", "rwkv_a0", "rwkv_a2", "rwkv_g2", "rwkv_k_k",
           "rwkv_k_a", "rwkv_r_k", "rwkv_gn_w", "rwkv_gn_b", "w_out", "ffn2_norm", "ffn2_w_gate", "ffn2_w_up",
           "ffn2_w_down", "final_norm")


def _tile(n, cap):
    if n <= cap:
        return n
    for t in range(cap - cap % LANES, 0, -LANES):
        if n % t == 0:
            return t
    raise ValueError((n, cap))


def _params(n_axes):
    return pltpu.CompilerParams(dimension_semantics=("arbitrary",) * n_axes, vmem_limit_bytes=VMEM_LIMIT)


def _sig(x):
    return jax.nn.sigmoid(x)


def _dsilu(z, s):
    return s * (1.0 + z * (1.0 - s))


def _dot(a, b, dims=((1,), (0,)), precision=None):
    return lax.dot_general(a, b, (dims, ((), ())), preferred_element_type=F32, precision=precision)


_NT = ((1,), (1,))
_TN = ((0,), (0,))
_HI = lax.Precision.HIGH


def _iota(shape, dim):
    return lax.broadcasted_iota(jnp.int32, shape, dim)


def _split_dot(x, ones, passes):
    hi = x.astype(jnp.bfloat16)
    acc = _dot(hi, ones)
    rem = x
    for _ in range(passes - 1):
        rem = rem - hi.astype(F32)
        hi = rem.astype(jnp.bfloat16)
        acc = acc + _dot(hi, ones)
    return acc


def _head_ones(n, width):
    shift = width.bit_length() - 1
    return (_iota((n, n), 0) >> shift == _iota((n, n), 1) >> shift).astype(jnp.bfloat16)


def _matmul(a, b, *, ta=False, tb=False, out_dtype=F32, res=None, after=(), name):
    m, k = (a.shape[1], a.shape[0]) if ta else a.shape
    n = b.shape[0] if tb else b.shape[1]
    tm, tn = _tile(m, 1408), _tile(n, 1408)
    in_bytes = max(a.dtype.itemsize, b.dtype.itemsize)
    for tk in (_tile(k, 1408), _tile(k, 1024), _tile(k, 512), _tile(k, 256)):
        if 2 * (tm + tn) * tk * in_bytes + 3 * tm * tn * 4 <= MATMUL_BLOCK_BYTES:
            break
    nk = k // tk
    dims = ((0 if ta else 1,), (1 if tb else 0,))

    def body(*refs):
        a_ref, b_ref = refs[:2]
        o_ref, acc = refs[-2:]
        kk = pl.program_id(2)

        @pl.when(kk == 0)
        def _():
            acc[...] = jnp.zeros_like(acc)

        acc[...] += _dot(a_ref[...].astype(MXU), b_ref[...].astype(MXU), dims)

        @pl.when(kk == nk - 1)
        def _():
            v = acc[...]
            if res is not None:
                v = v + refs[2][...]
            o_ref[...] = v.astype(out_dtype)

    a_spec = pl.BlockSpec((tk, tm), lambda i, j, kk: (kk, i)) if ta else pl.BlockSpec((tm, tk), lambda i, j, kk: (i, kk))
    b_spec = pl.BlockSpec((tn, tk), lambda i, j, kk: (j, kk)) if tb else pl.BlockSpec((tk, tn), lambda i, j, kk: (kk, j))
    o_spec = pl.BlockSpec((tm, tn), lambda i, j, kk: (i, j))
    ins, specs = [a, b], [a_spec, b_spec]
    if res is not None:
        ins.append(res)
        specs.append(o_spec)
    ins += list(after)
    specs += [pl.BlockSpec(memory_space=pl.ANY)] * len(after)
    return pl.pallas_call(
        body, name=name, grid=(m // tm, n // tn, nk), in_specs=specs, out_specs=o_spec,
        out_shape=jax.ShapeDtypeStruct((m, n), out_dtype), scratch_shapes=[pltpu.VMEM((tm, tn), F32)],
        compiler_params=_params(3))(*ins)


def _rms_fwd(x, g, name):
    t = x.shape[0]
    tb = _tile(t, 512)

    def body(x_ref, g_ref, o_ref):
        xv = x_ref[...]
        rinv = lax.rsqrt(jnp.mean(xv * xv, axis=-1, keepdims=True) + NORM_EPS)
        o_ref[...] = (xv * rinv * g_ref[...]).astype(MXU)

    return pl.pallas_call(
        body, name=name, grid=(t // tb,),
        in_specs=[pl.BlockSpec((tb, D), lambda i: (i, 0)), pl.BlockSpec((1, D), lambda i: (0, 0))],
        out_specs=pl.BlockSpec((tb, D), lambda i: (i, 0)), out_shape=jax.ShapeDtypeStruct((t, D), MXU),
        compiler_params=_params(1))(x, g)


def _rms_bwd(x, g, dh, dres, name, after=()):
    t = x.shape[0]
    tb = _tile(t, 512)

    def body(x_ref, g_ref, dh_ref, dres_ref, *rest):
        dx_ref, dg_ref = rest[-2:]

        @pl.when(pl.program_id(0) == 0)
        def _():
            dg_ref[...] = jnp.zeros_like(dg_ref)

        xv = x_ref[...]
        rinv = lax.rsqrt(jnp.mean(xv * xv, axis=-1, keepdims=True) + NORM_EPS)
        xhat = xv * rinv
        dhv = dh_ref[...]
        dg_ref[...] += jnp.sum(dhv * xhat, axis=0, keepdims=True)
        dxhat = dhv * g_ref[...]
        dx_ref[...] = dres_ref[...] + rinv * (dxhat - xhat * jnp.mean(dxhat * xhat, axis=-1, keepdims=True))

    row = pl.BlockSpec((tb, D), lambda i: (i, 0))
    vec = pl.BlockSpec((1, D), lambda i: (0, 0))
    return pl.pallas_call(
        body, name=name, grid=(t // tb,),
        in_specs=[row, vec, row, row] + [pl.BlockSpec(memory_space=pl.ANY)] * len(after), out_specs=[row, vec],
        out_shape=[jax.ShapeDtypeStruct((t, D), F32), jax.ShapeDtypeStruct((1, D), F32)],
        compiler_params=_params(1))(x, g, dh, dres, *after)


def _loss_head(x, g, target, name):
    t = x.shape[0]
    tb = _tile(t, 512)

    def body(x_ref, g_ref, t_ref, loss_ref, dx_ref, dg_ref):
        @pl.when(pl.program_id(0) == 0)
        def _():
            dg_ref[...] = jnp.zeros_like(dg_ref)
            loss_ref[...] = jnp.zeros_like(loss_ref)

        xv = x_ref[...]
        gv = g_ref[...]
        rinv = lax.rsqrt(jnp.mean(xv * xv, axis=-1, keepdims=True) + NORM_EPS)
        xhat = xv * rinv
        err = xhat * gv - t_ref[...]
        per_tok = jnp.mean(err * err, axis=-1, keepdims=True)
        loss_ref[...] += jnp.broadcast_to(0.5 * jnp.sum(per_tok, axis=0, keepdims=True), loss_ref.shape)
        dy = err * (1.0 / D)
        dg_ref[...] += jnp.sum(dy * xhat, axis=0, keepdims=True)
        dxhat = dy * gv
        dx_ref[...] = rinv * (dxhat - xhat * jnp.mean(dxhat * xhat, axis=-1, keepdims=True))

    row = pl.BlockSpec((tb, D), lambda i: (i, 0))
    vec = pl.BlockSpec((1, D), lambda i: (0, 0))
    return pl.pallas_call(
        body, name=name, grid=(t // tb,), in_specs=[row, vec, row],
        out_specs=[pl.BlockSpec((1, LANES), lambda i: (0, 0)), row, vec],
        out_shape=[jax.ShapeDtypeStruct((1, LANES), F32), jax.ShapeDtypeStruct((t, D), F32),
                   jax.ShapeDtypeStruct((1, D), F32)],
        compiler_params=_params(1))(x, g, target)


def _ffn_fwd(x, h, wg, wu, wd, name, after=()):
    t = x.shape[0]
    tb, fb = _tile(t, 1024), 256
    nf = FF // fb

    def body(x_ref, h_ref, wg_ref, wu_ref, wd_ref, *rest):
        o_ref, acc = rest[-2:]
        f = pl.program_id(1)

        @pl.when(f == 0)
        def _():
            acc[...] = jnp.zeros_like(acc)

        hv = h_ref[...]
        gate = _dot(hv, wg_ref[...], _NT)
        up = _dot(hv, wu_ref[...], _NT)
        act = (gate * _sig(gate) * up).astype(MXU)
        acc[...] += _dot(act, wd_ref[...])

        @pl.when(f == nf - 1)
        def _():
            o_ref[...] = x_ref[...] + 0.5 * acc[...]

    row = pl.BlockSpec((tb, D), lambda i, f: (i, 0))
    hidden = pl.BlockSpec((fb, D), lambda i, f: (f, 0))
    return pl.pallas_call(
        body, name=name, grid=(t // tb, nf),
        in_specs=[row, row, hidden, hidden, hidden]
        + [pl.BlockSpec(memory_space=pl.ANY)] * len(after), out_specs=row,
        out_shape=jax.ShapeDtypeStruct((t, D), F32), scratch_shapes=[pltpu.VMEM((tb, D), F32)],
        compiler_params=_params(2))(x, h, wg, wu, wd, *after)


def _ffn_bwd(h, dy, wg, wu, wd, name):
    t = h.shape[0]
    tb, fb = _tile(t, 1024), 256
    nf = FF // fb

    def body(h_ref, dy_ref, wg_ref, wu_ref, wd_ref, act_ref, dg_ref, du_ref, dout_ref):
        hv = h_ref[...]
        dout = (0.5 * dy_ref[...]).astype(MXU)
        dout_ref[...] = dout
        gate = _dot(hv, wg_ref[...], _NT)
        up = _dot(hv, wu_ref[...], _NT)
        dact = _dot(dout, wd_ref[...], _NT)
        s = _sig(gate)
        silu = gate * s
        act_ref[...] = (silu * up).astype(MXU)
        du_ref[...] = (dact * silu).astype(MXU)
        dg_ref[...] = (dact * up * _dsilu(gate, s)).astype(MXU)

    row = pl.BlockSpec((tb, D), lambda i, f: (i, 0))
    hidden = pl.BlockSpec((fb, D), lambda i, f: (f, 0))
    hid = pl.BlockSpec((tb, fb), lambda i, f: (i, f))
    hid_shape = jax.ShapeDtypeStruct((t, FF), MXU)
    return pl.pallas_call(
        body, name=name, grid=(t // tb, nf),
        in_specs=[row, row, hidden, hidden, hidden],
        out_specs=[hid, hid, hid, row],
        out_shape=[hid_shape, hid_shape, hid_shape, jax.ShapeDtypeStruct((t, D), MXU)],
        compiler_params=_params(2))(h, dy, wg, wu, wd)


def _hgrn_chunk(qa, fa, lbl):
    c = HG_CHUNK
    lb = _sig(lbl[0:1, :] - lbl[1:2, :])
    sf = _sig(fa)
    forget = lb + (1.0 - lb) * sf
    kh = 1.0 - forget
    row, col = _iota((c, c), 0), _iota((c, c), 1)
    b = _dot((col <= row).astype(F32), jnp.log(forget), precision=_HI)
    bref, blast = b[c // 2:c // 2 + 1, :], b[c - 1:c, :]
    sq = _sig(qa)
    q = qa * sq
    qt, kt = q * jnp.exp(b - bref), kh * jnp.exp(bref - b)
    qb, kl = q * jnp.exp(b), kh * jnp.exp(blast - b)
    causal = col <= row
    return dict(lb=lb, sf=sf, forget=forget, sq=sq, qt=qt, kt=kt, qb=qb, kl=kl, decay=jnp.exp(blast),
                causal=causal, e_q=jnp.exp(b), e_qt=jnp.exp(b - bref), e_kt=jnp.exp(bref - b),
                e_kl=jnp.exp(blast - b))


def _hgrn_specs(t):
    c = HG_CHUNK
    n = t // c
    return c, n, WA // LANES, HG_STEP_CHUNKS if n % HG_STEP_CHUNKS == 0 else 1


def _hgrn_fwd(p_all, lbl, onorm, name):
    t = p_all.shape[0]
    c, n, nh, m = _hgrn_specs(t)

    def body(q_ref, f_ref, i_ref, g_ref, lbl_ref, on_ref, oa_ref, oraw_ref, st_ref, state):
        @pl.when(pl.program_id(0) == 0)
        def _():
            state[...] = jnp.zeros_like(state)

        heads = [slice(h * LANES, (h + 1) * LANES) for h in range(nh)]
        sts = [state[h] for h in range(nh)]
        for sub in range(m):
            rows = slice(sub * c, (sub + 1) * c)
            ks = [_hgrn_chunk(q_ref[rows, at], f_ref[rows, at], lbl_ref[:, at]) for at in heads]
            vs = [i_ref[rows, at] for at in heads]
            for h in range(nh):
                st_ref[h, sub] = sts[h]
            scores = [jnp.where(k["causal"], _dot(k["qt"], k["kt"], _NT, _HI), 0.0) for k in ks]
            outs = [_dot(a, v, precision=_HI) + _dot(k["qb"], st, _NT, _HI) for a, v, k, st in zip(scores, vs, ks, sts)]
            sts = [st * k["decay"] + _dot(v, k["kl"], _TN, _HI) for st, k, v in zip(sts, ks, vs)]
            for at, o in zip(heads, outs):
                oraw_ref[rows, at] = o
                rinv = lax.rsqrt(jnp.mean(o * o, axis=-1, keepdims=True) + NORM_EPS)
                ga = g_ref[rows, at]
                oa_ref[rows, at] = (o * rinv * on_ref[:, at] * (ga * _sig(ga))).astype(MXU)
        for h in range(nh):
            state[h] = sts[h]

    def blk(j):
        return pl.BlockSpec((m * c, WA), lambda i: (i, j))

    return pl.pallas_call(
        body, name=name, grid=(n // m,),
        in_specs=[blk(0), blk(1), blk(2), blk(3), pl.BlockSpec((2, WA), lambda i: (0, 0)),
                  pl.BlockSpec((1, WA), lambda i: (0, 0))],
        out_specs=[blk(0), blk(0), pl.BlockSpec((nh, m, LANES, LANES), lambda i: (0, i, 0, 0))],
        out_shape=[jax.ShapeDtypeStruct((t, WA), MXU), jax.ShapeDtypeStruct((t, WA), F32),
                   jax.ShapeDtypeStruct((nh, n, LANES, LANES), F32)],
        scratch_shapes=[pltpu.VMEM((nh, LANES, LANES), F32)], compiler_params=_params(1))(
            p_all, p_all, p_all, p_all, lbl, onorm)


def _hgrn_bwd(p_all, lbl, onorm, oraw, states, doa, name):
    t = p_all.shape[0]
    c, n, nh, m = _hgrn_specs(t)

    def body(q_ref, f_ref, i_ref, g_ref, lbl_ref, on_ref, oraw_ref, st_ref, doa_ref,
             dq_ref, df_ref, di_ref, dg_ref, don_ref, dlbl_ref, dstate, dlb):
        @pl.when(pl.program_id(0) == 0)
        def _():
            dstate[...] = jnp.zeros_like(dstate)
            dlb[...] = jnp.zeros_like(dlb)
            don_ref[...] = jnp.zeros_like(don_ref)

        heads = [slice(h * LANES, (h + 1) * LANES) for h in range(nh)]
        dsts = [dstate[h] for h in range(nh)]
        step = _iota((c, LANES), 0)
        row, col = _iota((c, c), 0), _iota((c, c), 1)
        for sub in reversed(range(m)):
            rows = slice(sub * c, (sub + 1) * c)
            work = []
            for h, at in enumerate(heads):
                qa, fa, v, ga = q_ref[rows, at], f_ref[rows, at], i_ref[rows, at], g_ref[rows, at]
                k = _hgrn_chunk(qa, fa, lbl_ref[:, at])
                o = oraw_ref[rows, at]
                gain = on_ref[:, at]
                rinv = lax.rsqrt(jnp.mean(o * o, axis=-1, keepdims=True) + NORM_EPS)
                on = o * rinv
                sg = _sig(ga)
                gate = ga * sg
                dout = doa_ref[rows, at]
                don_ref[:, at] += jnp.sum(dout * on * gate, axis=0, keepdims=True)
                dg_ref[rows, at] = (dout * on * gain * _dsilu(ga, sg)).astype(MXU)
                d_on = dout * gain * gate
                do = rinv * (d_on - on * jnp.mean(d_on * on, axis=-1, keepdims=True))
                work.append(dict(at=at, qa=qa, v=v, k=k, do=do, st=st_ref[h, sub]))
            for x, dst_next in zip(work, dsts):
                k, do = x["k"], x["do"]
                x["a"] = jnp.where(k["causal"], _dot(k["qt"], k["kt"], _NT, _HI), 0.0)
                x["dqb"] = _dot(do, x["st"], precision=_HI)
                x["dst"] = dst_next * k["decay"] + _dot(do, k["qb"], _TN, _HI)
                x["da"] = jnp.where(k["causal"], _dot(do, x["v"], _NT, _HI), 0.0)
            for x, dst_next in zip(work, dsts):
                k = x["k"]
                x["dqt"] = _dot(x["da"], k["kt"], precision=_HI)
                x["dkt"] = _dot(x["da"], k["qt"], _TN, _HI)
                x["dv"] = _dot(x["a"], x["do"], _TN, _HI) + _dot(k["kl"], dst_next, _NT, _HI)
                x["dkl"] = _dot(x["v"], dst_next, precision=_HI)
            for x, dst_next in zip(work, dsts):
                k, at, dqt, dkt, dkl, dqb = x["k"], x["at"], x["dqt"], x["dkt"], x["dkl"], x["dqb"]
                ddecay = jnp.sum(dst_next * x["st"], axis=0, keepdims=True)
                dq = dqb * k["e_q"] + dqt * k["e_qt"]
                dk = dkt * k["e_kt"] + dkl * k["e_kl"]
                tq, tk, tl = dqt * k["qt"], dkt * k["kt"], dkl * k["kl"]
                db = dqb * k["qb"] + tq - tk - tl
                dbref = jnp.sum(tk - tq, axis=0, keepdims=True)
                dblast = jnp.sum(tl, axis=0, keepdims=True) + ddecay * k["decay"]
                db = db + jnp.where(step == c // 2, dbref, 0.0) + jnp.where(step == c - 1, dblast, 0.0)
                dlogf = _dot((col >= row).astype(F32), db, precision=_HI)
                dq_ref[rows, at] = (dq * _dsilu(x["qa"], k["sq"])).astype(MXU)
                di_ref[rows, at] = x["dv"].astype(MXU)
                dforget = dlogf / k["forget"] - dk
                sf, lb = k["sf"], k["lb"]
                df_ref[rows, at] = (dforget * (1.0 - lb) * sf * (1.0 - sf)).astype(MXU)
                dlb[:, at] += jnp.sum(dforget * (1.0 - sf), axis=0, keepdims=True)
                dl0 = dlb[:, at] * lb * (1.0 - lb)
                dlbl_ref[:, at] = jnp.where(_iota((2, LANES), 0) == 0, dl0, -dl0)
            dsts = [x["dst"] for x in work]
        for h in range(nh):
            dstate[h] = dsts[h]

    last = n // m - 1

    def blk(j):
        return pl.BlockSpec((m * c, WA), lambda i: (last - i, j))

    vec = pl.BlockSpec((1, WA), lambda i: (0, 0))
    lg = pl.BlockSpec((2, WA), lambda i: (0, 0))
    grad = jax.ShapeDtypeStruct((t, WA), MXU)
    return pl.pallas_call(
        body, name=name, grid=(n // m,),
        in_specs=[blk(0), blk(1), blk(2), blk(3), lg, vec, blk(0),
                  pl.BlockSpec((nh, m, LANES, LANES), lambda i: (0, last - i, 0, 0)), blk(0)],
        out_specs=[blk(0), blk(0), blk(0), blk(0), vec, lg],
        out_shape=[grad, grad, grad, grad, jax.ShapeDtypeStruct((1, WA), F32), jax.ShapeDtypeStruct((2, WA), F32)],
        scratch_shapes=[pltpu.VMEM((nh, LANES, LANES), F32), pltpu.VMEM((1, WA), F32)],
        compiler_params=_params(1))(p_all, p_all, p_all, p_all, lbl, onorm, oraw, states, doa)


def _lora_act(x):
    lane = _iota(x.shape, 1)
    n_w, n_a, n_g = LORA
    return jnp.where(lane < n_w, jnp.tanh(x),
                     jnp.where(lane < n_w + n_a, x, jnp.where(lane < n_w + n_a + n_g, _sig(x), 0.0)))


def _lora_dact(x):
    lane = _iota(x.shape, 1)
    n_w, n_a, n_g = LORA
    th, s = jnp.tanh(x), _sig(x)
    return jnp.where(lane < n_w, 1.0 - th * th,
                     jnp.where(lane < n_w + n_a, 1.0, jnp.where(lane < n_w + n_a + n_g, s * (1.0 - s), 0.0)))


def _shift_down(cur, prev8, first):
    rolled = pltpu.roll(cur, 1, 0)
    edge = prev8[7:8, :] * jnp.where(first, 0.0, 1.0)
    return jnp.where(_iota(cur.shape, 0) == 0, edge, rolled)


def _shift_up(cur, next8, last):
    rows = cur.shape[0]
    rolled = pltpu.roll(cur, rows - 1, 0)
    edge = next8[0:1, :] * jnp.where(last, 0.0, 1.0)
    return jnp.where(_iota(cur.shape, 0) == rows - 1, edge, rolled)


def _rwkv_inputs(refs, first, ones):
    (pr, pk, pv, plo, qr, qk, qv, qlo, mr, mk, mv, mlo, w2c, w0, a0, kk_w, ka_w) = refs
    mix = lambda cur, prev, mu: cur[...] + mu[...] * (_shift_down(cur[...], prev[...], first) - cur[...])
    r, k, v, lo = mix(pr, qr, mr), mix(pk, qk, mk), mix(pv, qv, mv), mix(plo, qlo, mlo)
    z = _lora_act(lo)
    lin = _dot(z.astype(MXU), w2c[...])
    sg = _sig(w0[...] + lin[:, :WB])
    decay = jnp.exp(-DECAY_C * sg)
    a = _sig(a0[...] + lin[:, WB:2 * WB])
    g = lin[:, 2 * WB:]
    kk0 = k * kk_w[...]
    nrm = jnp.sqrt(_split_dot(kk0 * kk0, ones, 3))
    den = jnp.maximum(nrm, L2_EPS)
    kk = kk0 / den
    k2 = k * (1.0 + (a - 1.0) * ka_w[...])
    return dict(r=r, k=k, v=v, lo=lo, z=z, sg=sg, decay=decay, a=a, g=g, kk=kk, den=den, nrm=nrm, k2=k2)


def _rwkv_in_specs(t, tb):
    nt8 = tb // 8

    def cur(w, j):
        return pl.BlockSpec((tb, w), lambda i: (i, j))

    def prev(w, j):
        return pl.BlockSpec((8, w), lambda i: (jnp.maximum(i * nt8 - 1, 0), j))

    def vec(w, j=0):
        return pl.BlockSpec((1, w), lambda i: (0, j))

    return [cur(WB, COL_R), cur(WB, COL_K), cur(WB, COL_V), cur(256, COL_L),
            prev(WB, COL_R), prev(WB, COL_K), prev(WB, COL_V), prev(256, COL_L),
            vec(WB, 0), vec(WB, 1), vec(WB, 2), vec(256, 6),
            pl.BlockSpec((256, 3 * WB), lambda i: (0, 0)), vec(WB), vec(WB), vec(WB), vec(WB)]


def _rwkv_in_args(p_all, mu_pad, w2cat, w0, a0, k_k, k_a):
    return (p_all,) * 8 + (mu_pad,) * 4 + (w2cat, w0, a0, k_k, k_a)


def _rwkv_prep(p_all, mu_pad, w2cat, w0, a0, k_k, k_a, name):
    t = p_all.shape[0]
    tb = _tile(t, 256)

    def body(*refs):
        ins, outs = refs[:17], refs[17:]
        q = _rwkv_inputs(ins, pl.program_id(0) == 0, _head_ones(WB, HD_B))
        for ref, val in zip(outs, (q["r"], q["decay"], q["k2"], q["v"], -q["kk"], q["kk"] * q["a"], q["g"])):
            ref[...] = val

    out = pl.BlockSpec((tb, WB), lambda i: (i, 0))
    return pl.pallas_call(
        body, name=name, grid=(t // tb,), in_specs=_rwkv_in_specs(t, tb), out_specs=[out] * 7,
        out_shape=[jax.ShapeDtypeStruct((t, WB), F32)] * 7, compiler_params=_params(1))(
            *_rwkv_in_args(p_all, mu_pad, w2cat, w0, a0, k_k, k_a))


def _pair_rows(x8, i):
    return jnp.concatenate([jnp.broadcast_to(x8[i:i + 1, p * LANES:(p + 1) * LANES], (HD_B, LANES))
                            for p in range(4)], axis=0)


def _pair_sums(x):
    return jnp.concatenate([jnp.sum(x[p * HD_B:(p + 1) * HD_B], axis=0, keepdims=True) for p in range(4)], axis=1)


def _put_row(buf, i, row):
    return jnp.where(_iota(buf.shape, 0) == i, row, buf)


def _pieces(x):
    hi = x.astype(jnp.bfloat16).astype(F32)
    lo = (x - hi).astype(jnp.bfloat16).astype(F32)
    upper = (_iota((x.shape[0], LANES), 1) & (HD_B // 2)) != 0
    swapped = [jnp.where(upper, pltpu.roll(lo[:, p * LANES:(p + 1) * LANES], HD_B // 2, 1),
                         pltpu.roll(lo[:, p * LANES:(p + 1) * LANES], LANES - HD_B // 2, 1)) for p in range(4)]
    return hi, jnp.concatenate(swapped, axis=1)


def _scan_consts():
    row, lane = _iota((HD_B, LANES), 0), _iota((HD_B, LANES), 1) & (HD_B - 1)
    either = ((row ^ lane) & (HD_B // 2 - 1)) == 0
    return ((row ^ lane) & (HD_B // 2)) != 0, either.astype(jnp.bfloat16), _head_ones(LANES, HD_B)


def _pair_cols(many, consts):
    swapped, either, ones = consts
    tiles = []
    for (hi8, lo8), i in many:
        for p in range(4):
            lanes = slice(p * LANES, (p + 1) * LANES)
            hi = jnp.broadcast_to(hi8[i:i + 1, lanes], (16, LANES)).astype(jnp.bfloat16)
            lo = jnp.broadcast_to(lo8[i:i + 1, lanes], (16, LANES)).astype(jnp.bfloat16)
            for g in range(HD_B // 16):
                rows = slice(g * 16, (g + 1) * 16)
                tiles.append(jnp.where(swapped[rows], lo, hi) * either[rows])
    out = _dot(jnp.concatenate(tiles, axis=0), ones)
    return [out[m * 4 * HD_B:(m + 1) * 4 * HD_B] for m in range(len(many))]


def _block_products(w8):
    rows = _iota(w8.shape, 0)
    down, up = w8, w8
    for shift in (1, 2, 4):
        down = down * jnp.where(rows >= shift, pltpu.roll(down, shift, 0), 1.0)
        up = up * jnp.where(rows < 8 - shift, pltpu.roll(up, 8 - shift, 0), 1.0)
    return down, up


def _blocked_loop(n_blocks, prepare, advance, init):
    unroll = SCAN_UNROLL if n_blocks % SCAN_UNROLL == 0 else 1

    def trip(g, carry):
        prepared = [prepare(g * unroll + i) for i in range(unroll)]
        for p in prepared:
            carry = advance(p, carry)
        return carry

    return lax.fori_loop(0, n_blocks // unroll, trip, init)


def _rwkv_scan_fwd(r, w, k, v, a, b, name):
    t = r.shape[0]
    cc = min(t, SCAN_CHUNK)

    def body(r_ref, w_ref, k_ref, v_ref, a_ref, b_ref, y_ref, sa_ref, state):
        @pl.when(pl.program_id(0) == 0)
        def _():
            state[...] = jnp.zeros_like(state)

        consts = _scan_consts()

        def prepare(j):
            rows = pl.ds(pl.multiple_of(j * 8, 8), 8)
            r8, w8, k8, v8, a8, b8 = (ref[rows, :] for ref in (r_ref, w_ref, k_ref, v_ref, a_ref, b_ref))
            decay, _ = _block_products(w8)
            before = jnp.where(_iota(w8.shape, 0) == 0, 1.0, pltpu.roll(decay, 1, 0))
            inv = 1.0 / decay
            scaled = [_pieces(x) for x in (a8 * before, b8 * inv, k8 * inv, r8 * decay)]
            return rows, v8, _pair_cols([(x, i) for i in range(8) for x in scaled] + [(_pieces(decay), 7)], consts)

        def advance(prepared, sk):
            rows, v8, cols = prepared
            y8 = jnp.zeros((8, WB), F32)
            sa8 = jnp.zeros((8, WB), F32)
            for i in range(8):
                a_c, b_c, k_c, r_c = cols[4 * i:4 * i + 4]
                sa = _pair_sums(sk * a_c)
                sk = sk + b_c * _pair_rows(sa, 0) + k_c * _pair_rows(v8, i)
                y8 = _put_row(y8, i, _pair_sums(sk * r_c))
                sa8 = _put_row(sa8, i, sa)
            y_ref[rows, :] = y8
            sa_ref[rows, :] = sa8
            return sk * cols[-1]

        state[...] = _blocked_loop(cc // 8, prepare, advance, state[...])

    row = pl.BlockSpec((cc, WB), lambda i: (i, 0))
    return pl.pallas_call(
        body, name=name, grid=(t // cc,), in_specs=[row] * 6, out_specs=[row, row],
        out_shape=[jax.ShapeDtypeStruct((t, WB), F32)] * 2,
        scratch_shapes=[pltpu.VMEM((4 * HD_B, LANES), F32)], compiler_params=_params(1))(r, w, k, v, a, b)


def _rwkv_states(sa, w, k, v, b, name):
    t = sa.shape[0]
    cc = min(t, SCAN_CHUNK)

    def body(sa_ref, w_ref, k_ref, v_ref, b_ref, sall_ref, state):
        @pl.when(pl.program_id(0) == 0)
        def _():
            state[...] = jnp.zeros_like(state)

        consts = _scan_consts()

        def prepare(j):
            base = pl.multiple_of(j * 8, 8)
            sa8, w8, k8, v8, b8 = (ref[pl.ds(base, 8), :] for ref in (sa_ref, w_ref, k_ref, v_ref, b_ref))
            sap, vp = _pieces(sa8), _pieces(v8)
            return base, w8, k8, b8, _pair_cols([(x, i) for i in range(8) for x in (sap, vp)], consts)

        def advance(prepared, sv):
            base, w8, k8, b8, cols = prepared
            for i in range(8):
                sv = sv * _pair_rows(w8, i) + cols[2 * i] * _pair_rows(b8, i) + cols[2 * i + 1] * _pair_rows(k8, i)
                sall_ref[base + i] = sv
            return sv

        state[...] = _blocked_loop(cc // 8, prepare, advance, state[...])

    row = pl.BlockSpec((cc, WB), lambda i: (i, 0))
    return pl.pallas_call(
        body, name=name, grid=(t // cc,), in_specs=[row] * 5,
        out_specs=pl.BlockSpec((cc, 4 * HD_B, LANES), lambda i: (i, 0, 0)),
        out_shape=jax.ShapeDtypeStruct((t, 4 * HD_B, LANES), F32),
        scratch_shapes=[pltpu.VMEM((4 * HD_B, LANES), F32)], compiler_params=_params(1))(sa, w, k, v, b)


def _rwkv_scan_bwd(dy, r, w, k, a, b, name):
    t = r.shape[0]
    cc = min(t, SCAN_CHUNK)
    n = t // cc

    def body(dy_ref, r_ref, w_ref, k_ref, a_ref, b_ref, dsa_ref, dv_ref, dstate):
        @pl.when(pl.program_id(0) == 0)
        def _():
            dstate[...] = jnp.zeros_like(dstate)

        consts = _scan_consts()

        steps = range(7, -1, -1)

        def prepare(jj):
            rows = pl.ds(pl.multiple_of((cc // 8 - 1 - jj) * 8, 8), 8)
            dy8, r8, w8, k8, a8, b8 = (ref[rows, :] for ref in (dy_ref, r_ref, w_ref, k_ref, a_ref, b_ref))
            _, upto = _block_products(w8)
            later = jnp.where(_iota(w8.shape, 0) == 7, 1.0, pltpu.roll(upto, 7, 0))
            scaled = [_pieces(x) for x in (r8 / later, b8 * later, k8 * later, a8 / upto)]
            return rows, dy8, _pair_cols([(x, i) for i in steps for x in scaled] + [(_pieces(upto), 0)], consts)

        def advance(prepared, ds):
            rows, dy8, cols = prepared
            dsa8, dv8 = jnp.zeros((8, WB), F32), jnp.zeros((8, WB), F32)
            for n_done, i in enumerate(steps):
                r_c, b_c, k_c, a_c = cols[4 * n_done:4 * n_done + 4]
                ds = ds + r_c * _pair_rows(dy8, i)
                dsa = _pair_sums(ds * b_c)
                dv8 = _put_row(dv8, i, _pair_sums(ds * k_c))
                dsa8 = _put_row(dsa8, i, dsa)
                ds = ds + a_c * _pair_rows(dsa, 0)
            dsa_ref[rows, :] = dsa8
            dv_ref[rows, :] = dv8
            return ds * cols[-1]

        dstate[...] = _blocked_loop(cc // 8, prepare, advance, dstate[...])

    row = pl.BlockSpec((cc, WB), lambda i: (n - 1 - i, 0))
    return pl.pallas_call(
        body, name=name, grid=(n,), in_specs=[row] * 6, out_specs=[row] * 2,
        out_shape=[jax.ShapeDtypeStruct((t, WB), F32)] * 2,
        scratch_shapes=[pltpu.VMEM((4 * HD_B, LANES), F32)], compiler_params=_params(1))(dy, r, w, k, a, b)


def _rwkv_scan_bwd_values(dy, r, w, v, a, sa, dsa, sall, name):
    t = r.shape[0]
    cc = min(t, SCAN_CHUNK)
    n = t // cc

    def body(dy_ref, r_ref, w_ref, v_ref, a_ref, sa_ref, dsa_ref, sall_ref, sprev_ref,
             dr_ref, dw_ref, dk_ref, da_ref, db_ref, dstate):
        @pl.when(pl.program_id(0) == 0)
        def _():
            dstate[...] = jnp.zeros_like(dstate)

        consts = _scan_consts()
        before_chunk = jnp.where(pl.program_id(0) == n - 1, 0.0, 1.0) * sprev_ref[0]

        steps = range(7, -1, -1)

        def prepare(jj):
            j = cc // 8 - 1 - jj
            base = pl.multiple_of(j * 8, 8)
            dy8, r8, w8, v8, a8, sa8, dsa8 = (ref[pl.ds(base, 8), :] for ref in
                                              (dy_ref, r_ref, w_ref, v_ref, a_ref, sa_ref, dsa_ref))
            dyp, vp, sap, dsap = (_pieces(x) for x in (dy8, v8, sa8, dsa8))
            return j, base, r8, w8, a8, _pair_cols([(x, i) for i in steps for x in (dyp, vp, sap, dsap)], consts)

        def advance(prepared, carry):
            ds, sc = carry
            j, base, r8, w8, a8, cols = prepared
            rows = pl.ds(base, 8)
            outs = [jnp.zeros((8, WB), F32) for _ in range(5)]
            for n_done, i in enumerate(steps):
                if i > 0:
                    sp = sall_ref[base + i - 1]
                else:
                    sp = jnp.where(j == 0, before_chunk, sall_ref[jnp.maximum(base - 1, 0)])
                dy_c, v_c, sa_c, dsa_c = cols[4 * n_done:4 * n_done + 4]
                ds = ds + dy_c * _pair_rows(r8, i)
                vals = (_pair_sums(sc * dy_c), _pair_sums(ds * sp), _pair_sums(ds * v_c),
                        _pair_sums(sp * dsa_c), _pair_sums(ds * sa_c))
                outs = [_put_row(o, i, val) for o, val in zip(outs, vals)]
                ds = ds * _pair_rows(w8, i) + dsa_c * _pair_rows(a8, i)
                sc = sp
            for ref, o in zip((dr_ref, dw_ref, dk_ref, da_ref, db_ref), outs):
                ref[rows, :] = o
            return ds, sc

        ds, _ = _blocked_loop(cc // 8, prepare, advance, (dstate[...], sall_ref[cc - 1]))
        dstate[...] = ds

    row = pl.BlockSpec((cc, WB), lambda i: (n - 1 - i, 0))
    return pl.pallas_call(
        body, name=name, grid=(n,),
        in_specs=[row] * 7 + [pl.BlockSpec((cc, 4 * HD_B, LANES), lambda i: (n - 1 - i, 0, 0)),
                              pl.BlockSpec((1, 4 * HD_B, LANES), lambda i: (jnp.maximum((n - 1 - i) * cc - 1, 0), 0, 0))],
        out_specs=[row] * 5, out_shape=[jax.ShapeDtypeStruct((t, WB), F32)] * 5,
        scratch_shapes=[pltpu.VMEM((4 * HD_B, LANES), F32)], compiler_params=_params(1))(
            dy, r, w, v, a, sa, dsa, sall, sall)


def _rwkv_post(y, r, k2, v, g, r_k, gn_w, gn_b, name):
    t = y.shape[0]
    tb = _tile(t, 256)

    def body(y_ref, r_ref, k_ref, v_ref, g_ref, rk_ref, gw_ref, gb_ref, o_ref):
        ones = _head_ones(WB, HD_B)
        yv = y_ref[...]
        yc = yv - _split_dot(yv, ones, 3) * (1.0 / HD_B)
        rstd = lax.rsqrt(_split_dot(yc * yc, ones, 3) * (1.0 / HD_B) + GN_EPS)
        rk = _split_dot(r_ref[...] * k_ref[...] * rk_ref[...], ones, 3)
        o_ref[...] = ((yc * rstd * gw_ref[...] + gb_ref[...] + rk * v_ref[...]) * g_ref[...]).astype(MXU)

    row = pl.BlockSpec((tb, WB), lambda i: (i, 0))
    vec = pl.BlockSpec((1, WB), lambda i: (0, 0))
    return pl.pallas_call(
        body, name=name, grid=(t // tb,), in_specs=[row] * 5 + [vec] * 3, out_specs=row,
        out_shape=jax.ShapeDtypeStruct((t, WB), MXU), compiler_params=_params(1))(y, r, k2, v, g, r_k, gn_w, gn_b)


def _rwkv_post_bwd(dob, y, r, k2, v, g, r_k, gn_w, gn_b, name):
    t = y.shape[0]
    tb = _tile(t, 256)

    def body(do_ref, y_ref, r_ref, k_ref, v_ref, g_ref, rk_ref, gw_ref, gb_ref,
             dy_ref, dg_ref, dr_ref, dk_ref, dv_ref, dgw_ref, dgb_ref, drk_ref):
        @pl.when(pl.program_id(0) == 0)
        def _():
            dgw_ref[...] = jnp.zeros_like(dgw_ref)
            dgb_ref[...] = jnp.zeros_like(dgb_ref)
            drk_ref[...] = jnp.zeros_like(drk_ref)

        ones = _head_ones(WB, HD_B)
        seg = lambda x: _split_dot(x, ones, 3)
        yv, rv, kv, vv, gv = y_ref[...], r_ref[...], k_ref[...], v_ref[...], g_ref[...]
        yc = yv - seg(yv) * (1.0 / HD_B)
        rstd = lax.rsqrt(seg(yc * yc) * (1.0 / HD_B) + GN_EPS)
        yn = yc * rstd
        rk = seg(rv * kv * rk_ref[...])
        dob_v = do_ref[...]
        dg_ref[...] = dob_v * (yn * gw_ref[...] + gb_ref[...] + rk * vv)
        dyg = dob_v * gv
        dgw_ref[...] += jnp.sum(dyg * yn, axis=0, keepdims=True)
        dgb_ref[...] += jnp.sum(dyg, axis=0, keepdims=True)
        dyn = dyg * gw_ref[...]
        dy_ref[...] = rstd * (dyn - (seg(dyn) + yn * seg(dyn * yn)) * (1.0 / HD_B))
        drk = seg(dyg * vv)
        dv_ref[...] = dyg * rk
        dr_ref[...] = drk * kv * rk_ref[...]
        dk_ref[...] = drk * rv * rk_ref[...]
        drk_ref[...] += jnp.sum(drk * rv * kv, axis=0, keepdims=True)

    row = pl.BlockSpec((tb, WB), lambda i: (i, 0))
    vec = pl.BlockSpec((1, WB), lambda i: (0, 0))
    full, small = jax.ShapeDtypeStruct((t, WB), F32), jax.ShapeDtypeStruct((1, WB), F32)
    return pl.pallas_call(
        body, name=name, grid=(t // tb,),
        in_specs=[pl.BlockSpec((tb, WB), lambda i: (i, dob.shape[1] // WB - 1))] + [row] * 5 + [vec] * 3,
        out_specs=[row] * 5 + [vec] * 3,
        out_shape=[full] * 5 + [small] * 3, compiler_params=_params(1))(dob, y, r, k2, v, g, r_k, gn_w, gn_b)


def _rwkv_prep_bwd(grads, p_all, mu_pad, w2cat, w0, a0, k_k, k_a, name):
    t = p_all.shape[0]
    tb = _tile(t, 256)

    def body(*refs):
        g_refs, ins, outs = refs[:10], refs[10:27], refs[27:]
        dr_s, dw, dk2_s, dv_s, das, dbs, dg, dr_b, dk2_b, dv_b = (ref[...] for ref in g_refs)
        dr_ref, dk_ref, dv_ref, dlo_ref, dw2_ref, dw0_ref, da0_ref, dkk_ref, dka_ref = outs

        @pl.when(pl.program_id(0) == 0)
        def _():
            for ref in (dw2_ref, dw0_ref, da0_ref, dkk_ref, dka_ref):
                ref[...] = jnp.zeros_like(ref)

        ones = _head_ones(WB, HD_B)
        q = _rwkv_inputs(ins, pl.program_id(0) == 0, ones)
        kk_w, ka_w = ins[15][...], ins[16][...]
        a, kk, k = q["a"], q["kk"], q["k"]
        dk2 = dk2_s + dk2_b
        dkk = dbs * a - das
        da = dbs * kk + dk2 * k * ka_w
        dk = dk2 * (1.0 + (a - 1.0) * ka_w)
        dka_ref[...] += jnp.sum(dk2 * k * (a - 1.0), axis=0, keepdims=True)
        proj = jnp.where(q["nrm"] > L2_EPS, _split_dot(dkk * kk, ones, 3), 0.0)
        dkk0 = (dkk - kk * proj) / q["den"]
        dk = dk + dkk0 * kk_w
        dkk_ref[...] += jnp.sum(dkk0 * k, axis=0, keepdims=True)
        dal = da * a * (1.0 - a)
        da0_ref[...] += jnp.sum(dal, axis=0, keepdims=True)
        sg = q["sg"]
        dwl = dw * q["decay"] * (-DECAY_C) * sg * (1.0 - sg)
        dw0_ref[...] += jnp.sum(dwl, axis=0, keepdims=True)
        dlin = jnp.concatenate([dwl, dal, dg], axis=1).astype(MXU)
        dw2_ref[...] += _dot(q["z"].astype(MXU), dlin, _TN)
        dz = _dot(dlin, ins[12][...], _NT)
        dlo_ref[...] = dz * _lora_dact(q["lo"])
        dr_ref[...] = dr_s + dr_b
        dk_ref[...] = dk
        dv_ref[...] = dv_s + dv_b

    row = pl.BlockSpec((tb, WB), lambda i: (i, 0))
    vec = pl.BlockSpec((1, WB), lambda i: (0, 0))
    full, small = jax.ShapeDtypeStruct((t, WB), F32), jax.ShapeDtypeStruct((1, WB), F32)
    return pl.pallas_call(
        body, name=name, grid=(t // tb,), in_specs=[row] * 10 + _rwkv_in_specs(t, tb),
        out_specs=[row] * 3 + [pl.BlockSpec((tb, 256), lambda i: (i, 0)),
                               pl.BlockSpec((256, 3 * WB), lambda i: (0, 0))] + [vec] * 4,
        out_shape=[full] * 3 + [jax.ShapeDtypeStruct((t, 256), F32), jax.ShapeDtypeStruct((256, 3 * WB), F32)]
        + [small] * 4, compiler_params=_params(1))(*grads, *_rwkv_in_args(p_all, mu_pad, w2cat, w0, a0, k_k, k_a))


def _shift_bwd(dshifted, p_all, mu_pad, name):
    t = p_all.shape[0]
    tb = _tile(t, 256)
    nt, nt8 = t // tb, tb // 8
    widths, cols, mus = (WB, WB, WB, 256), (COL_R, COL_K, COL_V, COL_L), (0, 1, 2, 6)

    def body(*refs):
        d_refs, n_refs, p_refs, q_refs, m_refs = refs[0:4], refs[4:8], refs[8:12], refs[12:16], refs[16:20]
        o_refs, dmu_refs = refs[20:24], refs[24:28]
        i = pl.program_id(0)

        @pl.when(i == 0)
        def _():
            for ref in dmu_refs:
                ref[...] = jnp.zeros_like(ref)

        for d, nx, p, q, m, o, dmu in zip(d_refs, n_refs, p_refs, q_refs, m_refs, o_refs, dmu_refs):
            dv, pv, mu = d[...], p[...], m[...]
            o[...] = (dv * (1.0 - mu) + mu * _shift_up(dv, nx[...], i == nt - 1)).astype(MXU)
            dmu[...] += jnp.sum(dv * (_shift_down(pv, q[...], i == 0) - pv), axis=0, keepdims=True)

    cur_d = [pl.BlockSpec((tb, w), lambda i: (i, 0)) for w in widths]
    next_d = [pl.BlockSpec((8, w), lambda i: (jnp.minimum((i + 1) * nt8, t // 8 - 1), 0)) for w in widths]
    cur_p = [pl.BlockSpec((tb, w), lambda i, j=j: (i, j)) for w, j in zip(widths, cols)]
    prev_p = [pl.BlockSpec((8, w), lambda i, j=j: (jnp.maximum(i * nt8 - 1, 0), j)) for w, j in zip(widths, cols)]
    mu_s = [pl.BlockSpec((1, w), lambda i, j=j: (0, j)) for w, j in zip(widths, mus)]
    vecs = [pl.BlockSpec((1, w), lambda i: (0, 0)) for w in widths]
    return pl.pallas_call(
        body, name=name, grid=(nt,), in_specs=cur_d + next_d + cur_p + prev_p + mu_s, out_specs=cur_d + vecs,
        out_shape=[jax.ShapeDtypeStruct((t, w), MXU) for w in widths]
        + [jax.ShapeDtypeStruct((1, w), F32) for w in widths],
        compiler_params=_params(1))(*dshifted, *dshifted, *(p_all,) * 8, *(mu_pad,) * 4)


def _peer(k):
    x, y, c = (lax.axis_index(n) for n in AXES)
    px = 1 - x if k & 4 else x
    py = 1 - y if k & 2 else y
    pc = 1 - c if k & 1 else c
    return (px, py, pc), 4 * px + 2 * py + pc


def _exchange_copy(src_refs, land_refs, send_sems, recv_sems, per_peer, j, k, arriving):
    _, me = _peer(0)
    peer, idx = _peer(k)
    sem = j * (N_DEV - 1) + k - 1
    return pltpu.make_async_remote_copy(
        src_ref=src_refs[j].at[idx] if per_peer[j] else src_refs[j],
        dst_ref=land_refs[j].at[idx if arriving else me],
        send_sem=send_sems.at[sem], recv_sem=recv_sems.at[sem],
        device_id=peer, device_id_type=pl.DeviceIdType.MESH)


def _exchange_start(srcs, per_peer, name, after=()):
    n = len(srcs)
    shapes = [tuple(s.shape[1:]) if pp else tuple(s.shape) for s, pp in zip(srcs, per_peer)]
    pairs = [(j, k) for k in range(1, N_DEV) for j in range(n)]
    first_out = 2 * n + len(after)

    def body(*refs):
        src_refs, land_refs, (send_sems, recv_sems), token = refs[:n], refs[n:2 * n], refs[first_out:first_out + 2], refs[-1]
        for j, k in pairs:
            _exchange_copy(src_refs, land_refs, send_sems, recv_sems, per_peer, j, k, False).start()
        token[...] = jnp.zeros_like(token)

    hbm, sem = pl.BlockSpec(memory_space=pltpu.HBM), pl.BlockSpec(memory_space=pltpu.SEMAPHORE)
    lands = [lax.empty((N_DEV,) + shp, s.dtype) for shp, s in zip(shapes, srcs)]
    operands = [pltpu.with_memory_space_constraint(a, pltpu.HBM) for a in list(srcs) + lands]
    n_sems = n * (N_DEV - 1)
    out = pl.pallas_call(
        body, name=name, in_specs=[hbm] * (2 * n) + [pl.BlockSpec(memory_space=pl.ANY)] * len(after),
        out_specs=[sem, sem] + [hbm] * (2 * n) + [pl.BlockSpec(memory_space=pltpu.VMEM)],
        out_shape=[pltpu.SemaphoreType.DMA((n_sems,)), pltpu.SemaphoreType.DMA((n_sems,))]
        + [pltpu.HBM(a.shape, a.dtype) for a in operands] + [jax.ShapeDtypeStruct((8, LANES), F32)],
        input_output_aliases={j: 2 + j for j in range(2 * n)},
        compiler_params=pltpu.CompilerParams(has_side_effects=pltpu.SideEffectType.DATAFLOW_SIDE_EFFECTING))(
            *operands, *after)
    return (out[0], out[1], out[2:2 + n], out[2 + n:2 + 2 * n], per_peer), out[-1]


def _exchange_wait(handle, after, name):
    send_sems, recv_sems, srcs, lands, per_peer = handle
    n = len(srcs)
    pairs = [(j, k) for k in range(1, N_DEV) for j in range(n)]

    def body(*refs):
        src_refs, land_refs, (send_sems, recv_sems) = refs[:n], refs[n:2 * n], refs[2 * n:2 * n + 2]
        for j, k in pairs:
            _exchange_copy(src_refs, land_refs, send_sems, recv_sems, per_peer, j, k, False).wait_send()
            _exchange_copy(src_refs, land_refs, send_sems, recv_sems, per_peer, j, k, True).wait_recv()

    hbm, sem = pl.BlockSpec(memory_space=pltpu.HBM), pl.BlockSpec(memory_space=pltpu.SEMAPHORE)
    out = pl.pallas_call(
        body, name=name, in_specs=[hbm] * (2 * n) + [sem, sem, pl.BlockSpec(memory_space=pl.ANY)],
        out_specs=[hbm] * (2 * n), out_shape=[pltpu.HBM(a.shape, a.dtype) for a in list(srcs) + list(lands)],
        input_output_aliases={j: j for j in range(2 * n)},
        compiler_params=pltpu.CompilerParams(has_side_effects=pltpu.SideEffectType.DATAFLOW_SIDE_EFFECTING))(
            *srcs, *lands, send_sems, recv_sems, after)
    return out[n:]


def _adam_update(g, w, m, v):
    c1, c2 = 1.0 - ADAM_B1 ** ADAM_STEP, 1.0 - ADAM_B2 ** ADAM_STEP
    nm = ADAM_B1 * m + (1.0 - ADAM_B1) * g
    nv = ADAM_B2 * v + (1.0 - ADAM_B2) * (g * g)
    return -ADAM_LR * ((nm / c1) / (jnp.sqrt(nv / c2) + ADAM_EPS) + ADAM_WD * w), nm, nv


def _row_tile(rows, cols):
    padded = -(-cols // LANES) * LANES
    cap = max(16, ADAM_BLOCK_BYTES // (N_DEV * padded * 4))
    best = rows
    for t in range(16, min(rows, cap) + 1, 16):
        if rows % t == 0:
            best = t
    return best


def _sum_parts(parts, name):
    _, rows, cols = parts.shape
    tb = _row_tile(rows, cols)

    def body(p_ref, g_ref):
        g = p_ref[0].astype(F32)
        for d in range(1, N_DEV):
            g = g + p_ref[d].astype(F32)
        g_ref[...] = g

    return pl.pallas_call(
        body, name=name, grid=(rows // tb,), in_specs=[pl.BlockSpec((N_DEV, tb, cols), lambda i: (0, i, 0))],
        out_specs=pl.BlockSpec((tb, cols), lambda i: (i, 0)), out_shape=jax.ShapeDtypeStruct((rows, cols), F32),
        compiler_params=_params(1))(parts)


def _adamw(parts, w, m, v, name):
    _, rows, cols = w.shape
    tb = _row_tile(rows, cols)
    n_parts = parts.shape[0]

    def body(p_ref, w_ref, m_ref, v_ref, g_ref, d_ref, nm_ref, nv_ref):
        g = p_ref[0].astype(F32)
        for d in range(1, n_parts):
            g = g + p_ref[d].astype(F32)
        g_ref[0] = g
        d_ref[0], nm_ref[0], nv_ref[0] = _adam_update(g, w_ref[0], m_ref[0], v_ref[0])

    row = pl.BlockSpec((1, tb, cols), lambda i: (0, i, 0))
    out = jax.ShapeDtypeStruct(w.shape, F32)
    return pl.pallas_call(
        body, name=name, grid=(rows // tb,),
        in_specs=[pl.BlockSpec((n_parts, tb, cols), lambda i: (0, i, 0)), row, row, row], out_specs=[row] * 4,
        out_shape=[out] * 4, compiler_params=_params(1))(parts, w, m, v)


def _adamw_small(parts, ws, ms, vs, name):
    n = len(ws)

    def body(*refs):
        p_ref = refs[0]
        w_refs, m_refs, v_refs = refs[1:1 + n], refs[1 + n:1 + 2 * n], refs[1 + 2 * n:1 + 3 * n]
        outs = refs[1 + 3 * n:]
        base = 0
        for j in range(n):
            rows, cols = ws[j].shape
            size = rows * cols
            for ch in range(-(-size // LANES)):
                r, c0 = divmod(ch * LANES, cols)
                width = min(LANES, cols - c0)
                g = p_ref[0, base + ch:base + ch + 1, 0:width]
                for d in range(1, N_DEV):
                    g = g + p_ref[d, base + ch:base + ch + 1, 0:width]
                at = (slice(r, r + 1), slice(c0, c0 + width))
                delta, nm, nv = _adam_update(g, w_refs[j][at], m_refs[j][at], v_refs[j][at])
                for out, val in zip((outs[j], outs[n + j], outs[2 * n + j], outs[3 * n + j]), (g, delta, nm, nv)):
                    out[at] = val
            base += -(-size // (8 * LANES)) * 8

    vmem = pl.BlockSpec(memory_space=pltpu.VMEM)
    res = pl.pallas_call(
        body, name=name, in_specs=[vmem] * (1 + 3 * n), out_specs=[vmem] * (4 * n),
        out_shape=[jax.ShapeDtypeStruct(a.shape, F32) for a in ws] * 4)(parts, *ws, *ms, *vs)
    return res[:n], res[n:2 * n], res[2 * n:3 * n], res[3 * n:]


def _rows(a, multiple):
    flat = a.reshape(-1)
    pad = -flat.shape[0] % (multiple * LANES)
    if pad:
        flat = jnp.concatenate([flat, jnp.zeros((pad,), a.dtype)])
    return flat.reshape(-1, LANES)


def _pack(arrs, multiple):
    return jnp.concatenate([_rows(a, multiple) for a in arrs], axis=0)


def _gathered_to_full(g, name, shard_shape):
    g = g.reshape((N_DEV,) + shard_shape)
    if name in COL_SHARDED:
        return jnp.transpose(g, (1, 0, 2)).reshape(shard_shape[0], N_DEV * shard_shape[1])
    return g.reshape(N_DEV * shard_shape[0], shard_shape[1])


def _full_to_per_device(full, name):
    if name in COL_SHARDED:
        r, c = full.shape
        return jnp.transpose(full.reshape(r, N_DEV, c // N_DEV), (1, 0, 2))
    return full.reshape(N_DEV, full.shape[0] // N_DEV, full.shape[1])


def _w2cat(w2, a2, g2):
    n_w, n_a, n_g = LORA
    out = jnp.zeros((256, 3 * WB), w2.dtype)
    out = out.at[0:n_w, 0:WB].set(w2)
    out = out.at[n_w:n_w + n_a, WB:2 * WB].set(a2)
    return out.at[n_w + n_a:n_w + n_a + n_g, 2 * WB:].set(g2)


class _Local:
    def __init__(self, w):
        self.w = w

    def weights(self, group, after=None):
        return self.w

    def started(self):
        return ()

    def send(self, grads, names):
        return ()


class _Overlapped:
    GROUPS = {"ffn1": ("ffn1_w_gate", "ffn1_w_up", "ffn1_w_down"),
              "mixer_in": ("w_in", "rwkv_w2", "rwkv_a2", "rwkv_g2"),
              "late": ("w_out", "ffn2_w_gate", "ffn2_w_up", "ffn2_w_down")}

    def __init__(self, wts):
        x, y, c = (lax.axis_index(n) for n in AXES)
        self.wts, self.me, self.gathers, self.sends = wts, 4 * x + 2 * y + c, {}, []
        self._gather("ffn1", ())

    def _gather(self, group, after):
        names = self.GROUPS[group]
        shards = [(jnp.swapaxes(self.wts[n], 1, 2) if n in SENT_TRANSPOSED else self.wts[n]).astype(MXU) for n in names]
        handle, token = _exchange_start(shards, [False] * len(names), "gather_" + group, after)
        self.gathers[group] = (names, shards, handle, token)
        self.newest = token

    def started(self):
        return (self.newest,)

    def _own_slot(self, land, mine):
        return lax.dynamic_update_slice(land, mine[None], (self.me,) + (0,) * mine.ndim)

    def weights(self, group, after=None):
        names, shards, handle, token = self.gathers[group]
        lands = _exchange_wait(handle, token if after is None else after, "gathered_" + group)
        w = {n: _gathered_to_full(self._own_slot(land, own), n, own.shape[1:])
             for n, own, land in zip(names, shards, lands)}
        order = list(self.GROUPS)
        if group != order[-1]:
            self._gather(order[order.index(group) + 1], (w[names[0]],))
        if group == "ffn1":
            for n in SMALL:
                keep = n in ("hgrn_lb_logits", "rwkv_r_k", "final_norm")
                w[n] = self.wts[n] if keep else self.wts[n].reshape(1, -1)
        return w

    def send(self, grads, names, small=None):
        contrib = [_full_to_per_device(grads[n], n).astype(WIRE) for n in names]
        per_peer = [True] * len(names)
        if small is not None:
            names, contrib, per_peer = names + ("small",), contrib + [small], per_peer + [False]
        handle, token = _exchange_start(contrib, per_peer, "scatter_" + names[0])
        self.sends.append((names, contrib, per_peer, handle))
        self.last_token = token
        return (token,)

    def received(self, which, after):
        names, contrib, per_peer, handle = self.sends[which]
        lands = _exchange_wait(handle, after, "scattered_" + names[0])
        parts = {}
        for n, own, pp, land in zip(names, contrib, per_peer, lands):
            mine = lax.dynamic_index_in_dim(own, self.me, 0, keepdims=False) if pp else own
            parts[n] = self._own_slot(land, mine)
        return parts


def _local_step(x, target, net):
    n_w, n_a, n_g = LORA
    w = dict(net.weights("ffn1"))
    h1 = _rms_fwd(x, w["ffn1_norm"], "ffn1_norm")
    x1 = _ffn_fwd(x, h1, w["ffn1_w_gate"], w["ffn1_w_up"], w["ffn1_w_down"], "ffn1_fwd", after=net.started())
    w.update(net.weights("mixer_in", x1))
    w_in_pad = jnp.pad(w["w_in"], ((0, N_INP - N_IN), (0, 0)))
    mu_pad = jnp.pad(w["rwkv_shift_mu"], ((0, 0), (0, 1792 - 1696)))
    w2cat = _w2cat(w["rwkv_w2"], w["rwkv_a2"], w["rwkv_g2"])
    r_k = w["rwkv_r_k"].reshape(1, WB)
    rw = (mu_pad, w2cat, w["rwkv_w0"], w["rwkv_a0"], w["rwkv_k_k"], w["rwkv_k_a"])

    h2 = _rms_fwd(x1, w["mix_norm"], "mix_norm")
    p_all = _matmul(h2, w_in_pad, tb=True, after=net.started(), name="in_proj")
    oa, oraw, states = _hgrn_fwd(p_all, w["hgrn_lb_logits"], w["hgrn_out_norm"], "hgrn_fwd")
    r, decay, k2, v, sa, sb, g = _rwkv_prep(p_all, *rw, "rwkv_prep")
    y, s_a = _rwkv_scan_fwd(r, decay, k2, v, sa, sb, "rwkv_scan_fwd")
    sall = _rwkv_states(s_a, decay, k2, v, sb, "rwkv_states")
    post_w = (r_k, w["rwkv_gn_w"], w["rwkv_gn_b"])
    ob = _rwkv_post(y, r, k2, v, g, *post_w, "rwkv_post")
    w.update(net.weights("late", ob))
    o = jnp.concatenate([oa, ob], axis=1)
    x2 = _matmul(o, w["w_out"], res=x1, name="out_proj")
    h3 = _rms_fwd(x2, w["ffn2_norm"], "ffn2_norm")
    x3 = _ffn_fwd(x2, h3, w["ffn2_w_gate"], w["ffn2_w_up"], w["ffn2_w_down"], "ffn2_fwd")
    loss, dx3, d_final = _loss_head(x3, w["final_norm"].reshape(1, D), target, "loss_head")

    grads = {"final_norm": d_final.reshape(D)}

    def ffn_back(prefix, h, dy, x_in, norm):
        wg, wu, wd = (w[prefix + s] for s in ("_w_gate", "_w_up", "_w_down"))
        act, dgate, dup, dout = _ffn_bwd(h, dy, wg, wu, wd, prefix + "_bwd")
        dh = _matmul(dup, wu, res=_matmul(dgate, wg, name=prefix + "_dh_gate"), name=prefix + "_dh")
        sent = ()
        for which, a_op, b_op in (("_w_gate", dgate, h), ("_w_up", dup, h), ("_w_down", act, dout)):
            grads[prefix + which] = _matmul(a_op, b_op, ta=True, out_dtype=WIRE, after=sent, name=prefix + "_d" + which)
            sent = net.send(grads, (prefix + which,))
        dx, grads[prefix + "_norm"] = _rms_bwd(x_in, norm, dh, dy, prefix + "_norm_bwd", after=sent)
        return dx

    dx2 = ffn_back("ffn2", h3, dx3, x2, w["ffn2_norm"])
    grads["w_out"] = _matmul(o, dx2, ta=True, out_dtype=WIRE, name="d_w_out")
    sent = net.send(grads, ("w_out",))
    do = _matmul(dx2, w["w_out"], tb=True, after=sent, name="d_mixed")
    dqa, dfa, dia, dga, grads["hgrn_out_norm"], grads["hgrn_lb_logits"] = _hgrn_bwd(
        p_all, w["hgrn_lb_logits"], w["hgrn_out_norm"], oraw, states, do, "hgrn_bwd")
    dy, dg, dr_b, dk2_b, dv_b, grads["rwkv_gn_w"], grads["rwkv_gn_b"], d_rk = _rwkv_post_bwd(
        do, y, r, k2, v, g, *post_w, "rwkv_post_bwd")
    grads["rwkv_r_k"] = d_rk.reshape(w["rwkv_r_k"].shape)
    d_sa, dv = _rwkv_scan_bwd(dy, r, decay, k2, sa, sb, "rwkv_scan_bwd")
    dr, dw, dk2, dsa, dsb = _rwkv_scan_bwd_values(dy, r, decay, v, sa, s_a, d_sa, sall, "rwkv_scan_bwd_values")
    (dsr, dsk, dsv, dslo, dw2cat, grads["rwkv_w0"], grads["rwkv_a0"], grads["rwkv_k_k"],
     grads["rwkv_k_a"]) = _rwkv_prep_bwd((dr, dw, dk2, dv, dsa, dsb, dg, dr_b, dk2_b, dv_b), p_all, *rw,
                                         "rwkv_prep_bwd")
    grads["rwkv_w2"] = dw2cat[0:n_w, 0:WB]
    grads["rwkv_a2"] = dw2cat[n_w:n_w + n_a, WB:2 * WB]
    grads["rwkv_g2"] = dw2cat[n_w + n_a:n_w + n_a + n_g, 2 * WB:]
    dpr, dpk, dpv, dplo, dmu_r, dmu_k, dmu_v, dmu_lo = _shift_bwd((dsr, dsk, dsv, dslo), p_all, mu_pad, "shift_bwd")
    grads["rwkv_shift_mu"] = jnp.concatenate([dmu_r, dmu_k, dmu_v, dmu_lo], axis=1)[:, :1696]
    dp = jnp.concatenate([dqa, dfa, dia, dga, dpr, dpk, dpv, dplo], axis=1)
    grads["w_in"] = _matmul(dp, h2, ta=True, out_dtype=WIRE, name="d_w_in")[:N_IN]
    sent = net.send(grads, ("w_in", "rwkv_w2", "rwkv_a2", "rwkv_g2"))
    dh2 = _matmul(dp, w_in_pad, after=sent, name="d_h2")
    dx1, grads["mix_norm"] = _rms_bwd(x1, w["mix_norm"], dh2, dx2, "mix_norm_bwd")
    dx0 = ffn_back("ffn1", h1, dx1, x, w["ffn1_norm"])
    return loss[0, 0], dx0, grads


def kernel(x, ffn1_norm, ffn1_w_gate, ffn1_w_up, ffn1_w_down, mix_norm, w_in, hgrn_lb_logits, hgrn_out_norm, rwkv_shift_mu, rwkv_w0, rwkv_w2, rwkv_a0, rwkv_a2, rwkv_g2, rwkv_k_k, rwkv_k_a, rwkv_r_k, rwkv_gn_w, rwkv_gn_b, w_out, ffn2_norm, ffn2_w_gate, ffn2_w_up, ffn2_w_down, final_norm, loss_target, m_ffn1_norm, m_ffn1_w_gate, m_ffn1_w_up, m_ffn1_w_down, m_mix_norm, m_w_in, m_hgrn_lb_logits, m_hgrn_out_norm, m_rwkv_shift_mu, m_rwkv_w0, m_rwkv_w2, m_rwkv_a0, m_rwkv_a2, m_rwkv_g2, m_rwkv_k_k, m_rwkv_k_a, m_rwkv_r_k, m_rwkv_gn_w, m_rwkv_gn_b, m_w_out, m_ffn2_norm, m_ffn2_w_gate, m_ffn2_w_up, m_ffn2_w_down, m_final_norm, v_ffn1_norm, v_ffn1_w_gate, v_ffn1_w_up, v_ffn1_w_down, v_mix_norm, v_w_in, v_hgrn_lb_logits, v_hgrn_out_norm, v_rwkv_shift_mu, v_rwkv_w0, v_rwkv_w2, v_rwkv_a0, v_rwkv_a2, v_rwkv_g2, v_rwkv_k_k, v_rwkv_k_a, v_rwkv_r_k, v_rwkv_gn_w, v_rwkv_gn_b, v_w_out, v_ffn2_norm, v_ffn2_w_gate, v_ffn2_w_up, v_ffn2_w_down, v_final_norm):
    args = dict(locals())
    wts = {n: args[n] for n in WEIGHTS}
    mom = {n: args["m_" + n] for n in WEIGHTS}
    var = {n: args["v_" + n] for n in WEIGHTS}
    net = _Overlapped(wts)
    loss, grad_x, grads = _local_step(x[0], loss_target[0], net)
    loss = lax.psum(loss, AXES)
    after, = net.send(grads, (), _pack([grads[n] for n in SMALL], 8))

    new = {}
    two_d = lambda a: a if a.ndim == 2 else a.reshape(1, -1)
    for which in range(len(net.sends)):
        for n, part in net.received(which, after).items():
            if n == "small":
                small = _adamw_small(part, *([two_d(src[k]) for k in SMALL] for src in (wts, mom, var)), "adamw_small")
                for j, k in enumerate(SMALL):
                    new[k] = [res[j].reshape(wts[k].shape) for res in small]
            else:
                if n in SENT_TRANSPOSED:
                    part = jnp.swapaxes(_sum_parts(part, "grad_" + n), 0, 1)[None]
                new[n] = _adamw(part, wts[n], mom[n], var[n], "adamw_" + n)
                after = new[n][1]
    return (loss, grad_x[None], *[new[n][0] for n in WEIGHTS], *[new[n][1] for n in WEIGHTS],
            *[new[n][2] for n in WEIGHTS], *[new[n][3] for n in WEIGHTS])
```

```python
import functools
import math

import jax
import jax.numpy as jnp
from jax import lax
from jax.experimental import pallas as pl
from jax.experimental.pallas import tpu as pltpu

F32 = jnp.float32
MXU = jnp.bfloat16
WIRE = jnp.bfloat16
D = 1024
FF = 2816
WA = 512
WB = 512
HD_B = 64
N_IN = 3744
N_INP = 3840
COL_R, COL_K, COL_V = 4, 5, 6
COL_L = 14
LORA = (32, 32, 96)
HG_CHUNK = 64
HG_STEP_CHUNKS = 4
SCAN_CHUNK = 64
SCAN_UNROLL = 8
NORM_EPS = 1e-6
GN_EPS = 64e-5
L2_EPS = 1e-12
DECAY_C = math.exp(-0.5)
N_DEV = 8
LANES = 128
ADAM_BLOCK_BYTES = 4 * 1024 * 1024
MATMUL_BLOCK_BYTES = 40 * 1024 * 1024
VMEM_LIMIT = 56 * 1024 * 1024
ADAM_LR, ADAM_B1, ADAM_B2, ADAM_EPS, ADAM_WD, ADAM_STEP = 0.001, 0.9, 0.999, 1e-08, 0.01, 10
AXES = ("x", "y", "c")

SHARDED = ("ffn1_w_gate", "ffn1_w_up", "ffn1_w_down", "w_in", "rwkv_w2", "rwkv_a2", "rwkv_g2", "w_out",
           "ffn2_w_gate", "ffn2_w_up", "ffn2_w_down")
COL_SHARDED = {"rwkv_w2", "rwkv_a2", "rwkv_g2"}
SENT_TRANSPOSED = {"ffn1_w_gate", "ffn1_w_up", "w_in", "ffn2_w_gate", "ffn2_w_up"}
SMALL = ("ffn1_norm", "mix_norm", "hgrn_lb_logits", "hgrn_out_norm", "rwkv_shift_mu", "rwkv_w0", "rwkv_a0",
         "rwkv_k_k", "rwkv_k_a", "rwkv_r_k", "rwkv_gn_w", "rwkv_gn_b", "ffn2_norm", "final_norm")
WEIGHTS = ("ffn1_norm", "ffn1_w_gate", "ffn1_w_up", "ffn1_w_down", "mix_norm", "w_in", "hgrn_lb_logits",
           "hgrn_out_norm", "rwkv_shift_mu", "rwkv_w0", "rwkv_w2", "rwkv_a0", "rwkv_a2", "rwkv_g2", "rwkv_k_k",
           "rwkv_k_a", "rwkv_r_k", "rwkv_gn_w", "rwkv_gn_b", "w_out", "ffn2_norm", "ffn2_w_gate", "ffn2_w_up",
           "ffn2_w_down", "final_norm")


def _tile(n, cap):
    if n <= cap:
        return n
    for t in range(cap - cap % LANES, 0, -LANES):
        if n % t == 0:
            return t
    raise ValueError((n, cap))


def _params(n_axes):
    return pltpu.CompilerParams(dimension_semantics=("arbitrary",) * n_axes, vmem_limit_bytes=VMEM_LIMIT)


def _sig(x):
    return jax.nn.sigmoid(x)


def _dsilu(z, s):
    return s * (1.0 + z * (1.0 - s))


def _dot(a, b, dims=((1,), (0,)), precision=None):
    return lax.dot_general(a, b, (dims, ((), ())), preferred_element_type=F32, precision=precision)


_NT = ((1,), (1,))
_TN = ((0,), (0,))
_HI = lax.Precision.HIGH


def _iota(shape, dim):
    return lax.broadcasted_iota(jnp.int32, shape, dim)


def _split_dot(x, ones, passes):
    hi = x.astype(jnp.bfloat16)
    acc = _dot(hi, ones)
    rem = x
    for _ in range(passes - 1):
        rem = rem - hi.astype(F32)
        hi = rem.astype(jnp.bfloat16)
        acc = acc + _dot(hi, ones)
    return acc


def _head_ones(n, width):
    shift = width.bit_length() - 1
    return (_iota((n, n), 0) >> shift == _iota((n, n), 1) >> shift).astype(jnp.bfloat16)


def _matmul(a, b, *, ta=False, tb=False, out_dtype=F32, res=None, after=(), name):
    m, k = (a.shape[1], a.shape[0]) if ta else a.shape
    n = b.shape[0] if tb else b.shape[1]
    tm, tn = _tile(m, 1408), _tile(n, 1408)
    in_bytes = max(a.dtype.itemsize, b.dtype.itemsize)
    for tk in (_tile(k, 1408), _tile(k, 1024), _tile(k, 512), _tile(k, 256)):
        if 2 * (tm + tn) * tk * in_bytes + 3 * tm * tn * 4 <= MATMUL_BLOCK_BYTES:
            break
    nk = k // tk
    dims = ((0 if ta else 1,), (1 if tb else 0,))

    def body(*refs):
        a_ref, b_ref = refs[:2]
        o_ref, acc = refs[-2:]
        kk = pl.program_id(2)

        @pl.when(kk == 0)
        def _():
            acc[...] = jnp.zeros_like(acc)

        acc[...] += _dot(a_ref[...].astype(MXU), b_ref[...].astype(MXU), dims)

        @pl.when(kk == nk - 1)
        def _():
            v = acc[...]
            if res is not None:
                v = v + refs[2][...]
            o_ref[...] = v.astype(out_dtype)

    a_spec = pl.BlockSpec((tk, tm), lambda i, j, kk: (kk, i)) if ta else pl.BlockSpec((tm, tk), lambda i, j, kk: (i, kk))
    b_spec = pl.BlockSpec((tn, tk), lambda i, j, kk: (j, kk)) if tb else pl.BlockSpec((tk, tn), lambda i, j, kk: (kk, j))
    o_spec = pl.BlockSpec((tm, tn), lambda i, j, kk: (i, j))
    ins, specs = [a, b], [a_spec, b_spec]
    if res is not None:
        ins.append(res)
        specs.append(o_spec)
    ins += list(after)
    specs += [pl.BlockSpec(memory_space=pl.ANY)] * len(after)
    return pl.pallas_call(
        body, name=name, grid=(m // tm, n // tn, nk), in_specs=specs, out_specs=o_spec,
        out_shape=jax.ShapeDtypeStruct((m, n), out_dtype), scratch_shapes=[pltpu.VMEM((tm, tn), F32)],
        compiler_params=_params(3))(*ins)


def _rms_fwd(x, g, name):
    t = x.shape[0]
    tb = _tile(t, 512)

    def body(x_ref, g_ref, o_ref):
        xv = x_ref[...]
        rinv = lax.rsqrt(jnp.mean(xv * xv, axis=-1, keepdims=True) + NORM_EPS)
        o_ref[...] = (xv * rinv * g_ref[...]).astype(MXU)

    return pl.pallas_call(
        body, name=name, grid=(t // tb,),
        in_specs=[pl.BlockSpec((tb, D), lambda i: (i, 0)), pl.BlockSpec((1, D), lambda i: (0, 0))],
        out_specs=pl.BlockSpec((tb, D), lambda i: (i, 0)), out_shape=jax.ShapeDtypeStruct((t, D), MXU),
        compiler_params=_params(1))(x, g)


def _rms_bwd(x, g, dh, dres, name, after=()):
    t = x.shape[0]
    tb = _tile(t, 512)

    def body(x_ref, g_ref, dh_ref, dres_ref, *rest):
        dx_ref, dg_ref = rest[-2:]

        @pl.when(pl.program_id(0) == 0)
        def _():
            dg_ref[...] = jnp.zeros_like(dg_ref)

        xv = x_ref[...]
        rinv = lax.rsqrt(jnp.mean(xv * xv, axis=-1, keepdims=True) + NORM_EPS)
        xhat = xv * rinv
        dhv = dh_ref[...]
        dg_ref[...] += jnp.sum(dhv * xhat, axis=0, keepdims=True)
        dxhat = dhv * g_ref[...]
        dx_ref[...] = dres_ref[...] + rinv * (dxhat - xhat * jnp.mean(dxhat * xhat, axis=-1, keepdims=True))

    row = pl.BlockSpec((tb, D), lambda i: (i, 0))
    vec = pl.BlockSpec((1, D), lambda i: (0, 0))
    return pl.pallas_call(
        body, name=name, grid=(t // tb,),
        in_specs=[row, vec, row, row] + [pl.BlockSpec(memory_space=pl.ANY)] * len(after), out_specs=[row, vec],
        out_shape=[jax.ShapeDtypeStruct((t, D), F32), jax.ShapeDtypeStruct((1, D), F32)],
        compiler_params=_params(1))(x, g, dh, dres, *after)


def _loss_head(x, g, target, name):
    t = x.shape[0]
    tb = _tile(t, 512)

    def body(x_ref, g_ref, t_ref, loss_ref, dx_ref, dg_ref):
        @pl.when(pl.program_id(0) == 0)
        def _():
            dg_ref[...] = jnp.zeros_like(dg_ref)
            loss_ref[...] = jnp.zeros_like(loss_ref)

        xv = x_ref[...]
        gv = g_ref[...]
        rinv = lax.rsqrt(jnp.mean(xv * xv, axis=-1, keepdims=True) + NORM_EPS)
        xhat = xv * rinv
        err = xhat * gv - t_ref[...]
        per_tok = jnp.mean(err * err, axis=-1, keepdims=True)
        loss_ref[...] += jnp.broadcast_to(0.5 * jnp.sum(per_tok, axis=0, keepdims=True), loss_ref.shape)
        dy = err * (1.0 / D)
        dg_ref[...] += jnp.sum(dy * xhat, axis=0, keepdims=True)
        dxhat = dy * gv
        dx_ref[...] = rinv * (dxhat - xhat * jnp.mean(dxhat * xhat, axis=-1, keepdims=True))

    row = pl.BlockSpec((tb, D), lambda i: (i, 0))
    vec = pl.BlockSpec((1, D), lambda i: (0, 0))
    return pl.pallas_call(
        body, name=name, grid=(t // tb,), in_specs=[row, vec, row],
        out_specs=[pl.BlockSpec((1, LANES), lambda i: (0, 0)), row, vec],
        out_shape=[jax.ShapeDtypeStruct((1, LANES), F32), jax.ShapeDtypeStruct((t, D), F32),
                   jax.ShapeDtypeStruct((1, D), F32)],
        compiler_params=_params(1))(x, g, target)


def _ffn_fwd(x, h, wg, wu, wd, name, after=()):
    t = x.shape[0]
    tb, fb = _tile(t, 1024), 256
    nf = FF // fb

    def body(x_ref, h_ref, wg_ref, wu_ref, wd_ref, *rest):
        o_ref, acc = rest[-2:]
        f = pl.program_id(1)

        @pl.when(f == 0)
        def _():
            acc[...] = jnp.zeros_like(acc)

        hv = h_ref[...]
        gate = _dot(hv, wg_ref[...], _NT)
        up = _dot(hv, wu_ref[...], _NT)
        act = (gate * _sig(gate) * up).astype(MXU)
        acc[...] += _dot(act, wd_ref[...])

        @pl.when(f == nf - 1)
        def _():
            o_ref[...] = x_ref[...] + 0.5 * acc[...]

    row = pl.BlockSpec((tb, D), lambda i, f: (i, 0))
    hidden = pl.BlockSpec((fb, D), lambda i, f: (f, 0))
    return pl.pallas_call(
        body, name=name, grid=(t // tb, nf),
        in_specs=[row, row, hidden, hidden, hidden]
        + [pl.BlockSpec(memory_space=pl.ANY)] * len(after), out_specs=row,
        out_shape=jax.ShapeDtypeStruct((t, D), F32), scratch_shapes=[pltpu.VMEM((tb, D), F32)],
        compiler_params=_params(2))(x, h, wg, wu, wd, *after)


def _ffn_bwd(h, dy, wg, wu, wd, name):
    t = h.shape[0]
    tb, fb = _tile(t, 1024), 256
    nf = FF // fb

    def body(h_ref, dy_ref, wg_ref, wu_ref, wd_ref, act_ref, dg_ref, du_ref, dout_ref):
        hv = h_ref[...]
        dout = (0.5 * dy_ref[...]).astype(MXU)
        dout_ref[...] = dout
        gate = _dot(hv, wg_ref[...], _NT)
        up = _dot(hv, wu_ref[...], _NT)
        dact = _dot(dout, wd_ref[...], _NT)
        s = _sig(gate)
        silu = gate * s
        act_ref[...] = (silu * up).astype(MXU)
        du_ref[...] = (dact * silu).astype(MXU)
        dg_ref[...] = (dact * up * _dsilu(gate, s)).astype(MXU)

    row = pl.BlockSpec((tb, D), lambda i, f: (i, 0))
    hidden = pl.BlockSpec((fb, D), lambda i, f: (f, 0))
    hid = pl.BlockSpec((tb, fb), lambda i, f: (i, f))
    hid_shape = jax.ShapeDtypeStruct((t, FF), MXU)
    return pl.pallas_call(
        body, name=name, grid=(t // tb, nf),
        in_specs=[row, row, hidden, hidden, hidden],
        out_specs=[hid, hid, hid, row],
        out_shape=[hid_shape, hid_shape, hid_shape, jax.ShapeDtypeStruct((t, D), MXU)],
        compiler_params=_params(2))(h, dy, wg, wu, wd)


def _hgrn_chunk(qa, fa, lbl):
    c = HG_CHUNK
    lb = _sig(lbl[0:1, :] - lbl[1:2, :])
    sf = _sig(fa)
    forget = lb + (1.0 - lb) * sf
    kh = 1.0 - forget
    row, col = _iota((c, c), 0), _iota((c, c), 1)
    b = _dot((col <= row).astype(F32), jnp.log(forget), precision=_HI)
    bref, blast = b[c // 2:c // 2 + 1, :], b[c - 1:c, :]
    sq = _sig(qa)
    q = qa * sq
    qt, kt = q * jnp.exp(b - bref), kh * jnp.exp(bref - b)
    qb, kl = q * jnp.exp(b), kh * jnp.exp(blast - b)
    causal = col <= row
    return dict(lb=lb, sf=sf, forget=forget, sq=sq, qt=qt, kt=kt, qb=qb, kl=kl, decay=jnp.exp(blast),
                causal=causal, e_q=jnp.exp(b), e_qt=jnp.exp(b - bref), e_kt=jnp.exp(bref - b),
                e_kl=jnp.exp(blast - b))


def _hgrn_specs(t):
    c = HG_CHUNK
    n = t // c
    return c, n, WA // LANES, HG_STEP_CHUNKS if n % HG_STEP_CHUNKS == 0 else 1


def _hgrn_fwd(p_all, lbl, onorm, name):
    t = p_all.shape[0]
    c, n, nh, m = _hgrn_specs(t)

    def body(q_ref, f_ref, i_ref, g_ref, lbl_ref, on_ref, oa_ref, oraw_ref, st_ref, state):
        @pl.when(pl.program_id(0) == 0)
        def _():
            state[...] = jnp.zeros_like(state)

        heads = [slice(h * LANES, (h + 1) * LANES) for h in range(nh)]
        sts = [state[h] for h in range(nh)]
        for sub in range(m):
            rows = slice(sub * c, (sub + 1) * c)
            ks = [_hgrn_chunk(q_ref[rows, at], f_ref[rows, at], lbl_ref[:, at]) for at in heads]
            vs = [i_ref[rows, at] for at in heads]
            for h in range(nh):
                st_ref[h, sub] = sts[h]
            scores = [jnp.where(k["causal"], _dot(k["qt"], k["kt"], _NT, _HI), 0.0) for k in ks]
            outs = [_dot(a, v, precision=_HI) + _dot(k["qb"], st, _NT, _HI) for a, v, k, st in zip(scores, vs, ks, sts)]
            sts = [st * k["decay"] + _dot(v, k["kl"], _TN, _HI) for st, k, v in zip(sts, ks, vs)]
            for at, o in zip(heads, outs):
                oraw_ref[rows, at] = o
                rinv = lax.rsqrt(jnp.mean(o * o, axis=-1, keepdims=True) + NORM_EPS)
                ga = g_ref[rows, at]
                oa_ref[rows, at] = (o * rinv * on_ref[:, at] * (ga * _sig(ga))).astype(MXU)
        for h in range(nh):
            state[h] = sts[h]

    def blk(j):
        return pl.BlockSpec((m * c, WA), lambda i: (i, j))

    return pl.pallas_call(
        body, name=name, grid=(n // m,),
        in_specs=[blk(0), blk(1), blk(2), blk(3), pl.BlockSpec((2, WA), lambda i: (0, 0)),
                  pl.BlockSpec((1, WA), lambda i: (0, 0))],
        out_specs=[blk(0), blk(0), pl.BlockSpec((nh, m, LANES, LANES), lambda i: (0, i, 0, 0))],
        out_shape=[jax.ShapeDtypeStruct((t, WA), MXU), jax.ShapeDtypeStruct((t, WA), F32),
                   jax.ShapeDtypeStruct((nh, n, LANES, LANES), F32)],
        scratch_shapes=[pltpu.VMEM((nh, LANES, LANES), F32)], compiler_params=_params(1))(
            p_all, p_all, p_all, p_all, lbl, onorm)


def _hgrn_bwd(p_all, lbl, onorm, oraw, states, doa, name):
    t = p_all.shape[0]
    c, n, nh, m = _hgrn_specs(t)

    def body(q_ref, f_ref, i_ref, g_ref, lbl_ref, on_ref, oraw_ref, st_ref, doa_ref,
             dq_ref, df_ref, di_ref, dg_ref, don_ref, dlbl_ref, dstate, dlb):
        @pl.when(pl.program_id(0) == 0)
        def _():
            dstate[...] = jnp.zeros_like(dstate)
            dlb[...] = jnp.zeros_like(dlb)
            don_ref[...] = jnp.zeros_like(don_ref)

        heads = [slice(h * LANES, (h + 1) * LANES) for h in range(nh)]
        dsts = [dstate[h] for h in range(nh)]
        step = _iota((c, LANES), 0)
        row, col = _iota((c, c), 0), _iota((c, c), 1)
        for sub in reversed(range(m)):
            rows = slice(sub * c, (sub + 1) * c)
            work = []
            for h, at in enumerate(heads):
                qa, fa, v, ga = q_ref[rows, at], f_ref[rows, at], i_ref[rows, at], g_ref[rows, at]
                k = _hgrn_chunk(qa, fa, lbl_ref[:, at])
                o = oraw_ref[rows, at]
                gain = on_ref[:, at]
                rinv = lax.rsqrt(jnp.mean(o * o, axis=-1, keepdims=True) + NORM_EPS)
                on = o * rinv
                sg = _sig(ga)
                gate = ga * sg
                dout = doa_ref[rows, at]
                don_ref[:, at] += jnp.sum(dout * on * gate, axis=0, keepdims=True)
                dg_ref[rows, at] = (dout * on * gain * _dsilu(ga, sg)).astype(MXU)
                d_on = dout * gain * gate
                do = rinv * (d_on - on * jnp.mean(d_on * on, axis=-1, keepdims=True))
                work.append(dict(at=at, qa=qa, v=v, k=k, do=do, st=st_ref[h, sub]))
            for x, dst_next in zip(work, dsts):
                k, do = x["k"], x["do"]
                x["a"] = jnp.where(k["causal"], _dot(k["qt"], k["kt"], _NT, _HI), 0.0)
                x["dqb"] = _dot(do, x["st"], precision=_HI)
                x["dst"] = dst_next * k["decay"] + _dot(do, k["qb"], _TN, _HI)
                x["da"] = jnp.where(k["causal"], _dot(do, x["v"], _NT, _HI), 0.0)
            for x, dst_next in zip(work, dsts):
                k = x["k"]
                x["dqt"] = _dot(x["da"], k["kt"], precision=_HI)
                x["dkt"] = _dot(x["da"], k["qt"], _TN, _HI)
                x["dv"] = _dot(x["a"], x["do"], _TN, _HI) + _dot(k["kl"], dst_next, _NT, _HI)
                x["dkl"] = _dot(x["v"], dst_next, precision=_HI)
            for x, dst_next in zip(work, dsts):
                k, at, dqt, dkt, dkl, dqb = x["k"], x["at"], x["dqt"], x["dkt"], x["dkl"], x["dqb"]
                ddecay = jnp.sum(dst_next * x["st"], axis=0, keepdims=True)
                dq = dqb * k["e_q"] + dqt * k["e_qt"]
                dk = dkt * k["e_kt"] + dkl * k["e_kl"]
                tq, tk, tl = dqt * k["qt"], dkt * k["kt"], dkl * k["kl"]
                db = dqb * k["qb"] + tq - tk - tl
                dbref = jnp.sum(tk - tq, axis=0, keepdims=True)
                dblast = jnp.sum(tl, axis=0, keepdims=True) + ddecay * k["decay"]
                db = db + jnp.where(step == c // 2, dbref, 0.0) + jnp.where(step == c - 1, dblast, 0.0)
                dlogf = _dot((col >= row).astype(F32), db, precision=_HI)
                dq_ref[rows, at] = (dq * _dsilu(x["qa"], k["sq"])).astype(MXU)
                di_ref[rows, at] = x["dv"].astype(MXU)
                dforget = dlogf / k["forget"] - dk
                sf, lb = k["sf"], k["lb"]
                df_ref[rows, at] = (dforget * (1.0 - lb) * sf * (1.0 - sf)).astype(MXU)
                dlb[:, at] += jnp.sum(dforget * (1.0 - sf), axis=0, keepdims=True)
                dl0 = dlb[:, at] * lb * (1.0 - lb)
                dlbl_ref[:, at] = jnp.where(_iota((2, LANES), 0) == 0, dl0, -dl0)
            dsts = [x["dst"] for x in work]
        for h in range(nh):
            dstate[h] = dsts[h]

    last = n // m - 1

    def blk(j):
        return pl.BlockSpec((m * c, WA), lambda i: (last - i, j))

    vec = pl.BlockSpec((1, WA), lambda i: (0, 0))
    lg = pl.BlockSpec((2, WA), lambda i: (0, 0))
    grad = jax.ShapeDtypeStruct((t, WA), MXU)
    return pl.pallas_call(
        body, name=name, grid=(n // m,),
        in_specs=[blk(0), blk(1), blk(2), blk(3), lg, vec, blk(0),
                  pl.BlockSpec((nh, m, LANES, LANES), lambda i: (0, last - i, 0, 0)), blk(0)],
        out_specs=[blk(0), blk(0), blk(0), blk(0), vec, lg],
        out_shape=[grad, grad, grad, grad, jax.ShapeDtypeStruct((1, WA), F32), jax.ShapeDtypeStruct((2, WA), F32)],
        scratch_shapes=[pltpu.VMEM((nh, LANES, LANES), F32), pltpu.VMEM((1, WA), F32)],
        compiler_params=_params(1))(p_all, p_all, p_all, p_all, lbl, onorm, oraw, states, doa)


def _lora_act(x):
    lane = _iota(x.shape, 1)
    n_w, n_a, n_g = LORA
    return jnp.where(lane < n_w, jnp.tanh(x),
                     jnp.where(lane < n_w + n_a, x, jnp.where(lane < n_w + n_a + n_g, _sig(x), 0.0)))


def _lora_dact(x):
    lane = _iota(x.shape, 1)
    n_w, n_a, n_g = LORA
    th, s = jnp.tanh(x), _sig(x)
    return jnp.where(lane < n_w, 1.0 - th * th,
                     jnp.where(lane < n_w + n_a, 1.0, jnp.where(lane < n_w + n_a + n_g, s * (1.0 - s), 0.0)))


def _shift_down(cur, prev8, first):
    rolled = pltpu.roll(cur, 1, 0)
    edge = prev8[7:8, :] * jnp.where(first, 0.0, 1.0)
    return jnp.where(_iota(cur.shape, 0) == 0, edge, rolled)


def _shift_up(cur, next8, last):
    rows = cur.shape[0]
    rolled = pltpu.roll(cur, rows - 1, 0)
    edge = next8[0:1, :] * jnp.where(last, 0.0, 1.0)
    return jnp.where(_iota(cur.shape, 0) == rows - 1, edge, rolled)


def _rwkv_inputs(refs, first, ones):
    (pr, pk, pv, plo, qr, qk, qv, qlo, mr, mk, mv, mlo, w2c, w0, a0, kk_w, ka_w) = refs
    mix = lambda cur, prev, mu: cur[...] + mu[...] * (_shift_down(cur[...], prev[...], first) - cur[...])
    r, k, v, lo = mix(pr, qr, mr), mix(pk, qk, mk), mix(pv, qv, mv), mix(plo, qlo, mlo)
    z = _lora_act(lo)
    lin = _dot(z.astype(MXU), w2c[...])
    sg = _sig(w0[...] + lin[:, :WB])
    decay = jnp.exp(-DECAY_C * sg)
    a = _sig(a0[...] + lin[:, WB:2 * WB])
    g = lin[:, 2 * WB:]
    kk0 = k * kk_w[...]
    nrm = jnp.sqrt(_split_dot(kk0 * kk0, ones, 3))
    den = jnp.maximum(nrm, L2_EPS)
    kk = kk0 / den
    k2 = k * (1.0 + (a - 1.0) * ka_w[...])
    return dict(r=r, k=k, v=v, lo=lo, z=z, sg=sg, decay=decay, a=a, g=g, kk=kk, den=den, nrm=nrm, k2=k2)


def _rwkv_in_specs(t, tb):
    nt8 = tb // 8

    def cur(w, j):
        return pl.BlockSpec((tb, w), lambda i: (i, j))

    def prev(w, j):
        return pl.BlockSpec((8, w), lambda i: (jnp.maximum(i * nt8 - 1, 0), j))

    def vec(w, j=0):
        return pl.BlockSpec((1, w), lambda i: (0, j))

    return [cur(WB, COL_R), cur(WB, COL_K), cur(WB, COL_V), cur(256, COL_L),
            prev(WB, COL_R), prev(WB, COL_K), prev(WB, COL_V), prev(256, COL_L),
            vec(WB, 0), vec(WB, 1), vec(WB, 2), vec(256, 6),
            pl.BlockSpec((256, 3 * WB), lambda i: (0, 0)), vec(WB), vec(WB), vec(WB), vec(WB)]


def _rwkv_in_args(p_all, mu_pad, w2cat, w0, a0, k_k, k_a):
    return (p_all,) * 8 + (mu_pad,) * 4 + (w2cat, w0, a0, k_k, k_a)


def _rwkv_prep(p_all, mu_pad, w2cat, w0, a0, k_k, k_a, name):
    t = p_all.shape[0]
    tb = _tile(t, 256)

    def body(*refs):
        ins, outs = refs[:17], refs[17:]
        q = _rwkv_inputs(ins, pl.program_id(0) == 0, _head_ones(WB, HD_B))
        for ref, val in zip(outs, (q["r"], q["decay"], q["k2"], q["v"], -q["kk"], q["kk"] * q["a"], q["g"])):
            ref[...] = val

    out = pl.BlockSpec((tb, WB), lambda i: (i, 0))
    return pl.pallas_call(
        body, name=name, grid=(t // tb,), in_specs=_rwkv_in_specs(t, tb), out_specs=[out] * 7,
        out_shape=[jax.ShapeDtypeStruct((t, WB), F32)] * 7, compiler_params=_params(1))(
            *_rwkv_in_args(p_all, mu_pad, w2cat, w0, a0, k_k, k_a))


def _pair_rows(x8, i):
    return jnp.concatenate([jnp.broadcast_to(x8[i:i + 1, p * LANES:(p + 1) * LANES], (HD_B, LANES))
                            for p in range(4)], axis=0)


def _pair_sums(x):
    return jnp.concatenate([jnp.sum(x[p * HD_B:(p + 1) * HD_B], axis=0, keepdims=True) for p in range(4)], axis=1)


def _put_row(buf, i, row):
    return jnp.where(_iota(buf.shape, 0) == i, row, buf)


def _pieces(x):
    hi = x.astype(jnp.bfloat16).astype(F32)
    lo = (x - hi).astype(jnp.bfloat16).astype(F32)
    upper = (_iota((x.shape[0], LANES), 1) & (HD_B // 2)) != 0
    swapped = [jnp.where(upper, pltpu.roll(lo[:, p * LANES:(p + 1) * LANES], HD_B // 2, 1),
                         pltpu.roll(lo[:, p * LANES:(p + 1) * LANES], LANES - HD_B // 2, 1)) for p in range(4)]
    return hi, jnp.concatenate(swapped, axis=1)


def _scan_consts():
    row, lane = _iota((HD_B, LANES), 0), _iota((HD_B, LANES), 1) & (HD_B - 1)
    either = ((row ^ lane) & (HD_B // 2 - 1)) == 0
    return ((row ^ lane) & (HD_B // 2)) != 0, either.astype(jnp.bfloat16), _head_ones(LANES, HD_B)


def _pair_cols(many, consts):
    swapped, either, ones = consts
    tiles = []
    for (hi8, lo8), i in many:
        for p in range(4):
            lanes = slice(p * LANES, (p + 1) * LANES)
            hi = jnp.broadcast_to(hi8[i:i + 1, lanes], (16, LANES)).astype(jnp.bfloat16)
            lo = jnp.broadcast_to(lo8[i:i + 1, lanes], (16, LANES)).astype(jnp.bfloat16)
            for g in range(HD_B // 16):
                rows = slice(g * 16, (g + 1) * 16)
                tiles.append(jnp.where(swapped[rows], lo, hi) * either[rows])
    out = _dot(jnp.concatenate(tiles, axis=0), ones)
    return [out[m * 4 * HD_B:(m + 1) * 4 * HD_B] for m in range(len(many))]


def _block_products(w8):
    rows = _iota(w8.shape, 0)
    down, up = w8, w8
    for shift in (1, 2, 4):
        down = down * jnp.where(rows >= shift, pltpu.roll(down, shift, 0), 1.0)
        up = up * jnp.where(rows < 8 - shift, pltpu.roll(up, 8 - shift, 0), 1.0)
    return down, up


def _blocked_loop(n_blocks, prepare, advance, init):
    unroll = SCAN_UNROLL if n_blocks % SCAN_UNROLL == 0 else 1

    def trip(g, carry):
        prepared = [prepare(g * unroll + i) for i in range(unroll)]
        for p in prepared:
            carry = advance(p, carry)
        return carry

    return lax.fori_loop(0, n_blocks // unroll, trip, init)


def _rwkv_scan_fwd(r, w, k, v, a, b, name):
    t = r.shape[0]
    cc = min(t, SCAN_CHUNK)

    def body(r_ref, w_ref, k_ref, v_ref, a_ref, b_ref, y_ref, sa_ref, state):
        @pl.when(pl.program_id(0) == 0)
        def _():
            state[...] = jnp.zeros_like(state)

        consts = _scan_consts()

        def prepare(j):
            rows = pl.ds(pl.multiple_of(j * 8, 8), 8)
            r8, w8, k8, v8, a8, b8 = (ref[rows, :] for ref in (r_ref, w_ref, k_ref, v_ref, a_ref, b_ref))
            decay, _ = _block_products(w8)
            before = jnp.where(_iota(w8.shape, 0) == 0, 1.0, pltpu.roll(decay, 1, 0))
            inv = 1.0 / decay
            scaled = [_pieces(x) for x in (a8 * before, b8 * inv, k8 * inv, r8 * decay)]
            return rows, v8, _pair_cols([(x, i) for i in range(8) for x in scaled] + [(_pieces(decay), 7)], consts)

        def advance(prepared, sk):
            rows, v8, cols = prepared
            y8 = jnp.zeros((8, WB), F32)
            sa8 = jnp.zeros((8, WB), F32)
            for i in range(8):
                a_c, b_c, k_c, r_c = cols[4 * i:4 * i + 4]
                sa = _pair_sums(sk * a_c)
                sk = sk + b_c * _pair_rows(sa, 0) + k_c * _pair_rows(v8, i)
                y8 = _put_row(y8, i, _pair_sums(sk * r_c))
                sa8 = _put_row(sa8, i, sa)
            y_ref[rows, :] = y8
            sa_ref[rows, :] = sa8
            return sk * cols[-1]

        state[...] = _blocked_loop(cc // 8, prepare, advance, state[...])

    row = pl.BlockSpec((cc, WB), lambda i: (i, 0))
    return pl.pallas_call(
        body, name=name, grid=(t // cc,), in_specs=[row] * 6, out_specs=[row, row],
        out_shape=[jax.ShapeDtypeStruct((t, WB), F32)] * 2,
        scratch_shapes=[pltpu.VMEM((4 * HD_B, LANES), F32)], compiler_params=_params(1))(r, w, k, v, a, b)


def _rwkv_states(sa, w, k, v, b, name):
    t = sa.shape[0]
    cc = min(t, SCAN_CHUNK)

    def body(sa_ref, w_ref, k_ref, v_ref, b_ref, sall_ref, state):
        @pl.when(pl.program_id(0) == 0)
        def _():
            state[...] = jnp.zeros_like(state)

        consts = _scan_consts()

        def prepare(j):
            base = pl.multiple_of(j * 8, 8)
            sa8, w8, k8, v8, b8 = (ref[pl.ds(base, 8), :] for ref in (sa_ref, w_ref, k_ref, v_ref, b_ref))
            sap, vp = _pieces(sa8), _pieces(v8)
            return base, w8, k8, b8, _pair_cols([(x, i) for i in range(8) for x in (sap, vp)], consts)

        def advance(prepared, sv):
            base, w8, k8, b8, cols = prepared
            for i in range(8):
                sv = sv * _pair_rows(w8, i) + cols[2 * i] * _pair_rows(b8, i) + cols[2 * i + 1] * _pair_rows(k8, i)
                sall_ref[base + i] = sv
            return sv

        state[...] = _blocked_loop(cc // 8, prepare, advance, state[...])

    row = pl.BlockSpec((cc, WB), lambda i: (i, 0))
    return pl.pallas_call(
        body, name=name, grid=(t // cc,), in_specs=[row] * 5,
        out_specs=pl.BlockSpec((cc, 4 * HD_B, LANES), lambda i: (i, 0, 0)),
        out_shape=jax.ShapeDtypeStruct((t, 4 * HD_B, LANES), F32),
        scratch_shapes=[pltpu.VMEM((4 * HD_B, LANES), F32)], compiler_params=_params(1))(sa, w, k, v, b)


def _rwkv_scan_bwd(dy, r, w, k, a, b, name):
    t = r.shape[0]
    cc = min(t, SCAN_CHUNK)
    n = t // cc

    def body(dy_ref, r_ref, w_ref, k_ref, a_ref, b_ref, dsa_ref, dv_ref, dstate):
        @pl.when(pl.program_id(0) == 0)
        def _():
            dstate[...] = jnp.zeros_like(dstate)

        consts = _scan_consts()

        steps = range(7, -1, -1)

        def prepare(jj):
            rows = pl.ds(pl.multiple_of((cc // 8 - 1 - jj) * 8, 8), 8)
            dy8, r8, w8, k8, a8, b8 = (ref[rows, :] for ref in (dy_ref, r_ref, w_ref, k_ref, a_ref, b_ref))
            _, upto = _block_products(w8)
            later = jnp.where(_iota(w8.shape, 0) == 7, 1.0, pltpu.roll(upto, 7, 0))
            scaled = [_pieces(x) for x in (r8 / later, b8 * later, k8 * later, a8 / upto)]
            return rows, dy8, _pair_cols([(x, i) for i in steps for x in scaled] + [(_pieces(upto), 0)], consts)

        def advance(prepared, ds):
            rows, dy8, cols = prepared
            dsa8, dv8 = jnp.zeros((8, WB), F32), jnp.zeros((8, WB), F32)
            for n_done, i in enumerate(steps):
                r_c, b_c, k_c, a_c = cols[4 * n_done:4 * n_done + 4]
                ds = ds + r_c * _pair_rows(dy8, i)
                dsa = _pair_sums(ds * b_c)
                dv8 = _put_row(dv8, i, _pair_sums(ds * k_c))
                dsa8 = _put_row(dsa8, i, dsa)
                ds = ds + a_c * _pair_rows(dsa, 0)
            dsa_ref[rows, :] = dsa8
            dv_ref[rows, :] = dv8
            return ds * cols[-1]

        dstate[...] = _blocked_loop(cc // 8, prepare, advance, dstate[...])

    row = pl.BlockSpec((cc, WB), lambda i: (n - 1 - i, 0))
    return pl.pallas_call(
        body, name=name, grid=(n,), in_specs=[row] * 6, out_specs=[row] * 2,
        out_shape=[jax.ShapeDtypeStruct((t, WB), F32)] * 2,
        scratch_shapes=[pltpu.VMEM((4 * HD_B, LANES), F32)], compiler_params=_params(1))(dy, r, w, k, a, b)


def _rwkv_scan_bwd_values(dy, r, w, v, a, sa, dsa, sall, name):
    t = r.shape[0]
    cc = min(t, SCAN_CHUNK)
    n = t // cc

    def body(dy_ref, r_ref, w_ref, v_ref, a_ref, sa_ref, dsa_ref, sall_ref, sprev_ref,
             dr_ref, dw_ref, dk_ref, da_ref, db_ref, dstate):
        @pl.when(pl.program_id(0) == 0)
        def _():
            dstate[...] = jnp.zeros_like(dstate)

        consts = _scan_consts()
        before_chunk = jnp.where(pl.program_id(0) == n - 1, 0.0, 1.0) * sprev_ref[0]

        steps = range(7, -1, -1)

        def prepare(jj):
            j = cc // 8 - 1 - jj
            base = pl.multiple_of(j * 8, 8)
            dy8, r8, w8, v8, a8, sa8, dsa8 = (ref[pl.ds(base, 8), :] for ref in
                                              (dy_ref, r_ref, w_ref, v_ref, a_ref, sa_ref, dsa_ref))
            dyp, vp, sap, dsap = (_pieces(x) for x in (dy8, v8, sa8, dsa8))
            return j, base, r8, w8, a8, _pair_cols([(x, i) for i in steps for x in (dyp, vp, sap, dsap)], consts)

        def advance(prepared, carry):
            ds, sc = carry
            j, base, r8, w8, a8, cols = prepared
            rows = pl.ds(base, 8)
            outs = [jnp.zeros((8, WB), F32) for _ in range(5)]
            for n_done, i in enumerate(steps):
                if i > 0:
                    sp = sall_ref[base + i - 1]
                else:
                    sp = jnp.where(j == 0, before_chunk, sall_ref[jnp.maximum(base - 1, 0)])
                dy_c, v_c, sa_c, dsa_c = cols[4 * n_done:4 * n_done + 4]
                ds = ds + dy_c * _pair_rows(r8, i)
                vals = (_pair_sums(sc * dy_c), _pair_sums(ds * sp), _pair_sums(ds * v_c),
                        _pair_sums(sp * dsa_c), _pair_sums(ds * sa_c))
                outs = [_put_row(o, i, val) for o, val in zip(outs, vals)]
                ds = ds * _pair_rows(w8, i) + dsa_c * _pair_rows(a8, i)
                sc = sp
            for ref, o in zip((dr_ref, dw_ref, dk_ref, da_ref, db_ref), outs):
                ref[rows, :] = o
            return ds, sc

        ds, _ = _blocked_loop(cc // 8, prepare, advance, (dstate[...], sall_ref[cc - 1]))
        dstate[...] = ds

    row = pl.BlockSpec((cc, WB), lambda i: (n - 1 - i, 0))
    return pl.pallas_call(
        body, name=name, grid=(n,),
        in_specs=[row] * 7 + [pl.BlockSpec((cc, 4 * HD_B, LANES), lambda i: (n - 1 - i, 0, 0)),
                              pl.BlockSpec((1, 4 * HD_B, LANES), lambda i: (jnp.maximum((n - 1 - i) * cc - 1, 0), 0, 0))],
        out_specs=[row] * 5, out_shape=[jax.ShapeDtypeStruct((t, WB), F32)] * 5,
        scratch_shapes=[pltpu.VMEM((4 * HD_B, LANES), F32)], compiler_params=_params(1))(
            dy, r, w, v, a, sa, dsa, sall, sall)


def _rwkv_post(y, r, k2, v, g, r_k, gn_w, gn_b, name):
    t = y.shape[0]
    tb = _tile(t, 256)

    def body(y_ref, r_ref, k_ref, v_ref, g_ref, rk_ref, gw_ref, gb_ref, o_ref):
        ones = _head_ones(WB, HD_B)
        yv = y_ref[...]
        yc = yv - _split_dot(yv, ones, 3) * (1.0 / HD_B)
        rstd = lax.rsqrt(_split_dot(yc * yc, ones, 3) * (1.0 / HD_B) + GN_EPS)
        rk = _split_dot(r_ref[...] * k_ref[...] * rk_ref[...], ones, 3)
        o_ref[...] = ((yc * rstd * gw_ref[...] + gb_ref[...] + rk * v_ref[...]) * g_ref[...]).astype(MXU)

    row = pl.BlockSpec((tb, WB), lambda i: (i, 0))
    vec = pl.BlockSpec((1, WB), lambda i: (0, 0))
    return pl.pallas_call(
        body, name=name, grid=(t // tb,), in_specs=[row] * 5 + [vec] * 3, out_specs=row,
        out_shape=jax.ShapeDtypeStruct((t, WB), MXU), compiler_params=_params(1))(y, r, k2, v, g, r_k, gn_w, gn_b)


def _rwkv_post_bwd(dob, y, r, k2, v, g, r_k, gn_w, gn_b, name):
    t = y.shape[0]
    tb = _tile(t, 256)

    def body(do_ref, y_ref, r_ref, k_ref, v_ref, g_ref, rk_ref, gw_ref, gb_ref,
             dy_ref, dg_ref, dr_ref, dk_ref, dv_ref, dgw_ref, dgb_ref, drk_ref):
        @pl.when(pl.program_id(0) == 0)
        def _():
            dgw_ref[...] = jnp.zeros_like(dgw_ref)
            dgb_ref[...] = jnp.zeros_like(dgb_ref)
            drk_ref[...] = jnp.zeros_like(drk_ref)

        ones = _head_ones(WB, HD_B)
        seg = lambda x: _split_dot(x, ones, 3)
        yv, rv, kv, vv, gv = y_ref[...], r_ref[...], k_ref[...], v_ref[...], g_ref[...]
        yc = yv - seg(yv) * (1.0 / HD_B)
        rstd = lax.rsqrt(seg(yc * yc) * (1.0 / HD_B) + GN_EPS)
        yn = yc * rstd
        rk = seg(rv * kv * rk_ref[...])
        dob_v = do_ref[...]
        dg_ref[...] = dob_v * (yn * gw_ref[...] + gb_ref[...] + rk * vv)
        dyg = dob_v * gv
        dgw_ref[...] += jnp.sum(dyg * yn, axis=0, keepdims=True)
        dgb_ref[...] += jnp.sum(dyg, axis=0, keepdims=True)
        dyn = dyg * gw_ref[...]
        dy_ref[...] = rstd * (dyn - (seg(dyn) + yn * seg(dyn * yn)) * (1.0 / HD_B))
        drk = seg(dyg * vv)
        dv_ref[...] = dyg * rk
        dr_ref[...] = drk * kv * rk_ref[...]
        dk_ref[...] = drk * rv * rk_ref[...]
        drk_ref[...] += jnp.sum(drk * rv * kv, axis=0, keepdims=True)

    row = pl.BlockSpec((tb, WB), lambda i: (i, 0))
    vec = pl.BlockSpec((1, WB), lambda i: (0, 0))
    full, small = jax.ShapeDtypeStruct((t, WB), F32), jax.ShapeDtypeStruct((1, WB), F32)
    return pl.pallas_call(
        body, name=name, grid=(t // tb,),
        in_specs=[pl.BlockSpec((tb, WB), lambda i: (i, dob.shape[1] // WB - 1))] + [row] * 5 + [vec] * 3,
        out_specs=[row] * 5 + [vec] * 3,
        out_shape=[full] * 5 + [small] * 3, compiler_params=_params(1))(dob, y, r, k2, v, g, r_k, gn_w, gn_b)


def _rwkv_prep_bwd(grads, p_all, mu_pad, w2cat, w0, a0, k_k, k_a, name):
    t = p_all.shape[0]
    tb = _tile(t, 256)

    def body(*refs):
        g_refs, ins, outs = refs[:10], refs[10:27], refs[27:]
        dr_s, dw, dk2_s, dv_s, das, dbs, dg, dr_b, dk2_b, dv_b = (ref[...] for ref in g_refs)
        dr_ref, dk_ref, dv_ref, dlo_ref, dw2_ref, dw0_ref, da0_ref, dkk_ref, dka_ref = outs

        @pl.when(pl.program_id(0) == 0)
        def _():
            for ref in (dw2_ref, dw0_ref, da0_ref, dkk_ref, dka_ref):
                ref[...] = jnp.zeros_like(ref)

        ones = _head_ones(WB, HD_B)
        q = _rwkv_inputs(ins, pl.program_id(0) == 0, ones)
        kk_w, ka_w = ins[15][...], ins[16][...]
        a, kk, k = q["a"], q["kk"], q["k"]
        dk2 = dk2_s + dk2_b
        dkk = dbs * a - das
        da = dbs * kk + dk2 * k * ka_w
        dk = dk2 * (1.0 + (a - 1.0) * ka_w)
        dka_ref[...] += jnp.sum(dk2 * k * (a - 1.0), axis=0, keepdims=True)
        proj = jnp.where(q["nrm"] > L2_EPS, _split_dot(dkk * kk, ones, 3), 0.0)
        dkk0 = (dkk - kk * proj) / q["den"]
        dk = dk + dkk0 * kk_w
        dkk_ref[...] += jnp.sum(dkk0 * k, axis=0, keepdims=True)
        dal = da * a * (1.0 - a)
        da0_ref[...] += jnp.sum(dal, axis=0, keepdims=True)
        sg = q["sg"]
        dwl = dw * q["decay"] * (-DECAY_C) * sg * (1.0 - sg)
        dw0_ref[...] += jnp.sum(dwl, axis=0, keepdims=True)
        dlin = jnp.concatenate([dwl, dal, dg], axis=1).astype(MXU)
        dw2_ref[...] += _dot(q["z"].astype(MXU), dlin, _TN)
        dz = _dot(dlin, ins[12][...], _NT)
        dlo_ref[...] = dz * _lora_dact(q["lo"])
        dr_ref[...] = dr_s + dr_b
        dk_ref[...] = dk
        dv_ref[...] = dv_s + dv_b

    row = pl.BlockSpec((tb, WB), lambda i: (i, 0))
    vec = pl.BlockSpec((1, WB), lambda i: (0, 0))
    full, small = jax.ShapeDtypeStruct((t, WB), F32), jax.ShapeDtypeStruct((1, WB), F32)
    return pl.pallas_call(
        body, name=name, grid=(t // tb,), in_specs=[row] * 10 + _rwkv_in_specs(t, tb),
        out_specs=[row] * 3 + [pl.BlockSpec((tb, 256), lambda i: (i, 0)),
                               pl.BlockSpec((256, 3 * WB), lambda i: (0, 0))] + [vec] * 4,
        out_shape=[full] * 3 + [jax.ShapeDtypeStruct((t, 256), F32), jax.ShapeDtypeStruct((256, 3 * WB), F32)]
        + [small] * 4, compiler_params=_params(1))(*grads, *_rwkv_in_args(p_all, mu_pad, w2cat, w0, a0, k_k, k_a))


def _shift_bwd(dshifted, p_all, mu_pad, name):
    t = p_all.shape[0]
    tb = _tile(t, 256)
    nt, nt8 = t // tb, tb // 8
    widths, cols, mus = (WB, WB, WB, 256), (COL_R, COL_K, COL_V, COL_L), (0, 1, 2, 6)

    def body(*refs):
        d_refs, n_refs, p_refs, q_refs, m_refs = refs[0:4], refs[4:8], refs[8:12], refs[12:16], refs[16:20]
        o_refs, dmu_refs = refs[20:24], refs[24:28]
        i = pl.program_id(0)

        @pl.when(i == 0)
        def _():
            for ref in dmu_refs:
                ref[...] = jnp.zeros_like(ref)

        for d, nx, p, q, m, o, dmu in zip(d_refs, n_refs, p_refs, q_refs, m_refs, o_refs, dmu_refs):
            dv, pv, mu = d[...], p[...], m[...]
            o[...] = (dv * (1.0 - mu) + mu * _shift_up(dv, nx[...], i == nt - 1)).astype(MXU)
            dmu[...] += jnp.sum(dv * (_shift_down(pv, q[...], i == 0) - pv), axis=0, keepdims=True)

    cur_d = [pl.BlockSpec((tb, w), lambda i: (i, 0)) for w in widths]
    next_d = [pl.BlockSpec((8, w), lambda i: (jnp.minimum((i + 1) * nt8, t // 8 - 1), 0)) for w in widths]
    cur_p = [pl.BlockSpec((tb, w), lambda i, j=j: (i, j)) for w, j in zip(widths, cols)]
    prev_p = [pl.BlockSpec((8, w), lambda i, j=j: (jnp.maximum(i * nt8 - 1, 0), j)) for w, j in zip(widths, cols)]
    mu_s = [pl.BlockSpec((1, w), lambda i, j=j: (0, j)) for w, j in zip(widths, mus)]
    vecs = [pl.BlockSpec((1, w), lambda i: (0, 0)) for w in widths]
    return pl.pallas_call(
        body, name=name, grid=(nt,), in_specs=cur_d + next_d + cur_p + prev_p + mu_s, out_specs=cur_d + vecs,
        out_shape=[jax.ShapeDtypeStruct((t, w), MXU) for w in widths]
        + [jax.ShapeDtypeStruct((1, w), F32) for w in widths],
        compiler_params=_params(1))(*dshifted, *dshifted, *(p_all,) * 8, *(mu_pad,) * 4)


def _peer(k):
    x, y, c = (lax.axis_index(n) for n in AXES)
    px = 1 - x if k & 4 else x
    py = 1 - y if k & 2 else y
    pc = 1 - c if k & 1 else c
    return (px, py, pc), 4 * px + 2 * py + pc


def _exchange_copy(src_refs, land_refs, send_sems, recv_sems, per_peer, j, k, arriving):
    _, me = _peer(0)
    peer, idx = _peer(k)
    sem = j * (N_DEV - 1) + k - 1
    return pltpu.make_async_remote_copy(
        src_ref=src_refs[j].at[idx] if per_peer[j] else src_refs[j],
        dst_ref=land_refs[j].at[idx if arriving else me],
        send_sem=send_sems.at[sem], recv_sem=recv_sems.at[sem],
        device_id=peer, device_id_type=pl.DeviceIdType.MESH)


def _exchange_start(srcs, per_peer, name, after=()):
    n = len(srcs)
    shapes = [tuple(s.shape[1:]) if pp else tuple(s.shape) for s, pp in zip(srcs, per_peer)]
    pairs = [(j, k) for k in range(1, N_DEV) for j in range(n)]
    first_out = 2 * n + len(after)

    def body(*refs):
        src_refs, land_refs, (send_sems, recv_sems), token = refs[:n], refs[n:2 * n], refs[first_out:first_out + 2], refs[-1]
        for j, k in pairs:
            _exchange_copy(src_refs, land_refs, send_sems, recv_sems, per_peer, j, k, False).start()
        token[...] = jnp.zeros_like(token)

    hbm, sem = pl.BlockSpec(memory_space=pltpu.HBM), pl.BlockSpec(memory_space=pltpu.SEMAPHORE)
    lands = [lax.empty((N_DEV,) + shp, s.dtype) for shp, s in zip(shapes, srcs)]
    operands = [pltpu.with_memory_space_constraint(a, pltpu.HBM) for a in list(srcs) + lands]
    n_sems = n * (N_DEV - 1)
    out = pl.pallas_call(
        body, name=name, in_specs=[hbm] * (2 * n) + [pl.BlockSpec(memory_space=pl.ANY)] * len(after),
        out_specs=[sem, sem] + [hbm] * (2 * n) + [pl.BlockSpec(memory_space=pltpu.VMEM)],
        out_shape=[pltpu.SemaphoreType.DMA((n_sems,)), pltpu.SemaphoreType.DMA((n_sems,))]
        + [pltpu.HBM(a.shape, a.dtype) for a in operands] + [jax.ShapeDtypeStruct((8, LANES), F32)],
        input_output_aliases={j: 2 + j for j in range(2 * n)},
        compiler_params=pltpu.CompilerParams(has_side_effects=pltpu.SideEffectType.DATAFLOW_SIDE_EFFECTING))(
            *operands, *after)
    return (out[0], out[1], out[2:2 + n], out[2 + n:2 + 2 * n], per_peer), out[-1]


def _exchange_wait(handle, after, name):
    send_sems, recv_sems, srcs, lands, per_peer = handle
    n = len(srcs)
    pairs = [(j, k) for k in range(1, N_DEV) for j in range(n)]

    def body(*refs):
        src_refs, land_refs, (send_sems, recv_sems) = refs[:n], refs[n:2 * n], refs[2 * n:2 * n + 2]
        for j, k in pairs:
            _exchange_copy(src_refs, land_refs, send_sems, recv_sems, per_peer, j, k, False).wait_send()
            _exchange_copy(src_refs, land_refs, send_sems, recv_sems, per_peer, j, k, True).wait_recv()

    hbm, sem = pl.BlockSpec(memory_space=pltpu.HBM), pl.BlockSpec(memory_space=pltpu.SEMAPHORE)
    out = pl.pallas_call(
        body, name=name, in_specs=[hbm] * (2 * n) + [sem, sem, pl.BlockSpec(memory_space=pl.ANY)],
        out_specs=[hbm] * (2 * n), out_shape=[pltpu.HBM(a.shape, a.dtype) for a in list(srcs) + list(lands)],
        input_output_aliases={j: j for j in range(2 * n)},
        compiler_params=pltpu.CompilerParams(has_side_effects=pltpu.SideEffectType.DATAFLOW_SIDE_EFFECTING))(
            *srcs, *lands, send_sems, recv_sems, after)
    return out[n:]


def _adam_update(g, w, m, v):
    c1, c2 = 1.0 - ADAM_B1 ** ADAM_STEP, 1.0 - ADAM_B2 ** ADAM_STEP
    nm = ADAM_B1 * m + (1.0 - ADAM_B1) * g
    nv = ADAM_B2 * v + (1.0 - ADAM_B2) * (g * g)
    return -ADAM_LR * ((nm / c1) / (jnp.sqrt(nv / c2) + ADAM_EPS) + ADAM_WD * w), nm, nv


def _row_tile(rows, cols):
    padded = -(-cols // LANES) * LANES
    cap = max(16, ADAM_BLOCK_BYTES // (N_DEV * padded * 4))
    best = rows
    for t in range(16, min(rows, cap) + 1, 16):
        if rows % t == 0:
            best = t
    return best


def _sum_in_device_order(received, mine):
    x, y, c = (lax.axis_index(n) for n in AXES)
    me = 4 * x + 2 * y + c
    total = None
    for d in range(N_DEV):
        term = jnp.where(me == d, mine, received(d)).astype(F32)
        total = term if total is None else total + term
    return total


def _sum_parts(parts, mine, name):
    _, rows, cols = parts.shape
    tb = _row_tile(rows, cols)

    def body(p_ref, mine_ref, g_ref):
        g_ref[...] = _sum_in_device_order(lambda d: p_ref[d], mine_ref[...])

    tile = pl.BlockSpec((tb, cols), lambda i: (i, 0))
    return pl.pallas_call(
        body, name=name, grid=(rows // tb,), in_specs=[pl.BlockSpec((N_DEV, tb, cols), lambda i: (0, i, 0)), tile],
        out_specs=tile, out_shape=jax.ShapeDtypeStruct((rows, cols), F32), compiler_params=_params(1))(parts, mine)


def _adamw(grad, w, m, v, name):
    _, rows, cols = w.shape
    tb = _row_tile(rows, cols)
    summed = not isinstance(grad, tuple)

    def body(*refs):
        w_ref, m_ref, v_ref, g_ref, d_ref, nm_ref, nv_ref = refs[-7:]
        g = refs[0][...] if summed else _sum_in_device_order(lambda d: refs[0][d], refs[1][...])
        g_ref[0] = g
        d_ref[0], nm_ref[0], nv_ref[0] = _adam_update(g, w_ref[0], m_ref[0], v_ref[0])

    row = pl.BlockSpec((1, tb, cols), lambda i: (0, i, 0))
    tile = pl.BlockSpec((tb, cols), lambda i: (i, 0))
    grad_specs = [tile] if summed else [pl.BlockSpec((N_DEV, tb, cols), lambda i: (0, i, 0)), tile]
    out = jax.ShapeDtypeStruct(w.shape, F32)
    return pl.pallas_call(
        body, name=name, grid=(rows // tb,), in_specs=grad_specs + [row, row, row], out_specs=[row] * 4,
        out_shape=[out] * 4, compiler_params=_params(1))(*((grad,) if summed else grad), w, m, v)


def _adamw_small(parts, mine, ws, ms, vs, name):
    n = len(ws)

    def body(*refs):
        p_ref, mine_ref = refs[:2]
        w_refs, m_refs, v_refs = refs[2:2 + n], refs[2 + n:2 + 2 * n], refs[2 + 2 * n:2 + 3 * n]
        outs = refs[2 + 3 * n:]
        base = 0
        for j in range(n):
            rows, cols = ws[j].shape
            size = rows * cols
            for ch in range(-(-size // LANES)):
                r, c0 = divmod(ch * LANES, cols)
                width = min(LANES, cols - c0)
                packed = (slice(base + ch, base + ch + 1), slice(0, width))
                g = _sum_in_device_order(lambda d: p_ref[(d,) + packed], mine_ref[packed])
                at = (slice(r, r + 1), slice(c0, c0 + width))
                delta, nm, nv = _adam_update(g, w_refs[j][at], m_refs[j][at], v_refs[j][at])
                for out, val in zip((outs[j], outs[n + j], outs[2 * n + j], outs[3 * n + j]), (g, delta, nm, nv)):
                    out[at] = val
            base += -(-size // (8 * LANES)) * 8

    vmem = pl.BlockSpec(memory_space=pltpu.VMEM)
    res = pl.pallas_call(
        body, name=name, in_specs=[vmem] * (2 + 3 * n), out_specs=[vmem] * (4 * n),
        out_shape=[jax.ShapeDtypeStruct(a.shape, F32) for a in ws] * 4)(parts, mine, *ws, *ms, *vs)
    return res[:n], res[n:2 * n], res[2 * n:3 * n], res[3 * n:]


def _rows(a, multiple):
    flat = a.reshape(-1)
    pad = -flat.shape[0] % (multiple * LANES)
    if pad:
        flat = jnp.concatenate([flat, jnp.zeros((pad,), a.dtype)])
    return flat.reshape(-1, LANES)


def _pack(arrs, multiple):
    return jnp.concatenate([_rows(a, multiple) for a in arrs], axis=0)


def _gathered_to_full(g, name, shard_shape):
    g = g.reshape((N_DEV,) + shard_shape)
    if name in COL_SHARDED:
        return jnp.transpose(g, (1, 0, 2)).reshape(shard_shape[0], N_DEV * shard_shape[1])
    return g.reshape(N_DEV * shard_shape[0], shard_shape[1])


def _full_to_per_device(full, name):
    if name in COL_SHARDED:
        r, c = full.shape
        return jnp.transpose(full.reshape(r, N_DEV, c // N_DEV), (1, 0, 2))
    return full.reshape(N_DEV, full.shape[0] // N_DEV, full.shape[1])


def _w2cat(w2, a2, g2):
    n_w, n_a, n_g = LORA
    out = jnp.zeros((256, 3 * WB), w2.dtype)
    out = out.at[0:n_w, 0:WB].set(w2)
    out = out.at[n_w:n_w + n_a, WB:2 * WB].set(a2)
    return out.at[n_w + n_a:n_w + n_a + n_g, 2 * WB:].set(g2)


class _Local:
    def __init__(self, w):
        self.w = w

    def weights(self, group, after=None):
        return self.w

    def started(self):
        return ()

    def send(self, grads, names):
        return ()


class _Overlapped:
    GROUPS = {"ffn1": ("ffn1_w_gate", "ffn1_w_up", "ffn1_w_down"),
              "mixer_in": ("w_in", "rwkv_w2", "rwkv_a2", "rwkv_g2"),
              "late": ("w_out", "ffn2_w_gate", "ffn2_w_up", "ffn2_w_down")}

    def __init__(self, wts):
        x, y, c = (lax.axis_index(n) for n in AXES)
        self.wts, self.me, self.gathers, self.sends = wts, 4 * x + 2 * y + c, {}, []
        self._gather("ffn1", ())

    def _gather(self, group, after):
        names = self.GROUPS[group]
        shards = [(jnp.swapaxes(self.wts[n], 1, 2) if n in SENT_TRANSPOSED else self.wts[n]).astype(MXU) for n in names]
        handle, token = _exchange_start(shards, [False] * len(names), "gather_" + group, after)
        self.gathers[group] = (names, shards, handle, token)
        self.newest = token

    def started(self):
        return (self.newest,)

    def _own_slot(self, land, mine):
        return lax.dynamic_update_slice(land, mine[None], (self.me,) + (0,) * mine.ndim)

    def weights(self, group, after=None):
        names, shards, handle, token = self.gathers[group]
        lands = _exchange_wait(handle, token if after is None else after, "gathered_" + group)
        w = {n: _gathered_to_full(self._own_slot(land, own), n, own.shape[1:])
             for n, own, land in zip(names, shards, lands)}
        order = list(self.GROUPS)
        if group != order[-1]:
            self._gather(order[order.index(group) + 1], (w[names[0]],))
        if group == "ffn1":
            for n in SMALL:
                keep = n in ("hgrn_lb_logits", "rwkv_r_k", "final_norm")
                w[n] = self.wts[n] if keep else self.wts[n].reshape(1, -1)
        return w

    def send(self, grads, names, small=None):
        contrib = [_full_to_per_device(grads[n], n).astype(WIRE) for n in names]
        per_peer = [True] * len(names)
        if small is not None:
            names, contrib, per_peer = names + ("small",), contrib + [small], per_peer + [False]
        handle, token = _exchange_start(contrib, per_peer, "scatter_" + names[0])
        self.sends.append((names, contrib, per_peer, handle))
        self.last_token = token
        return (token,)

    def received(self, which, after):
        names, contrib, per_peer, handle = self.sends[which]
        lands = _exchange_wait(handle, after, "scattered_" + names[0])
        return {n: (land, lax.dynamic_index_in_dim(own, self.me, 0, keepdims=False) if pp else own)
                for n, own, pp, land in zip(names, contrib, per_peer, lands)}


def _local_step(x, target, net):
    n_w, n_a, n_g = LORA
    w = dict(net.weights("ffn1"))
    h1 = _rms_fwd(x, w["ffn1_norm"], "ffn1_norm")
    x1 = _ffn_fwd(x, h1, w["ffn1_w_gate"], w["ffn1_w_up"], w["ffn1_w_down"], "ffn1_fwd", after=net.started())
    w.update(net.weights("mixer_in", x1))
    w_in_pad = jnp.pad(w["w_in"], ((0, N_INP - N_IN), (0, 0)))
    mu_pad = jnp.pad(w["rwkv_shift_mu"], ((0, 0), (0, 1792 - 1696)))
    w2cat = _w2cat(w["rwkv_w2"], w["rwkv_a2"], w["rwkv_g2"])
    r_k = w["rwkv_r_k"].reshape(1, WB)
    rw = (mu_pad, w2cat, w["rwkv_w0"], w["rwkv_a0"], w["rwkv_k_k"], w["rwkv_k_a"])

    h2 = _rms_fwd(x1, w["mix_norm"], "mix_norm")
    p_all = _matmul(h2, w_in_pad, tb=True, after=net.started(), name="in_proj")
    oa, oraw, states = _hgrn_fwd(p_all, w["hgrn_lb_logits"], w["hgrn_out_norm"], "hgrn_fwd")
    r, decay, k2, v, sa, sb, g = _rwkv_prep(p_all, *rw, "rwkv_prep")
    y, s_a = _rwkv_scan_fwd(r, decay, k2, v, sa, sb, "rwkv_scan_fwd")
    sall = _rwkv_states(s_a, decay, k2, v, sb, "rwkv_states")
    post_w = (r_k, w["rwkv_gn_w"], w["rwkv_gn_b"])
    ob = _rwkv_post(y, r, k2, v, g, *post_w, "rwkv_post")
    w.update(net.weights("late", ob))
    o = jnp.concatenate([oa, ob], axis=1)
    x2 = _matmul(o, w["w_out"], res=x1, name="out_proj")
    h3 = _rms_fwd(x2, w["ffn2_norm"], "ffn2_norm")
    x3 = _ffn_fwd(x2, h3, w["ffn2_w_gate"], w["ffn2_w_up"], w["ffn2_w_down"], "ffn2_fwd")
    loss, dx3, d_final = _loss_head(x3, w["final_norm"].reshape(1, D), target, "loss_head")

    grads = {"final_norm": d_final.reshape(D)}

    def ffn_back(prefix, h, dy, x_in, norm):
        wg, wu, wd = (w[prefix + s] for s in ("_w_gate", "_w_up", "_w_down"))
        act, dgate, dup, dout = _ffn_bwd(h, dy, wg, wu, wd, prefix + "_bwd")
        dh = _matmul(dup, wu, res=_matmul(dgate, wg, name=prefix + "_dh_gate"), name=prefix + "_dh")
        sent = ()
        for which, a_op, b_op in (("_w_gate", dgate, h), ("_w_up", dup, h), ("_w_down", act, dout)):
            grads[prefix + which] = _matmul(a_op, b_op, ta=True, out_dtype=WIRE, after=sent, name=prefix + "_d" + which)
            sent = net.send(grads, (prefix + which,))
        dx, grads[prefix + "_norm"] = _rms_bwd(x_in, norm, dh, dy, prefix + "_norm_bwd", after=sent)
        return dx

    dx2 = ffn_back("ffn2", h3, dx3, x2, w["ffn2_norm"])
    grads["w_out"] = _matmul(o, dx2, ta=True, out_dtype=WIRE, name="d_w_out")
    sent = net.send(grads, ("w_out",))
    do = _matmul(dx2, w["w_out"], tb=True, after=sent, name="d_mixed")
    dqa, dfa, dia, dga, grads["hgrn_out_norm"], grads["hgrn_lb_logits"] = _hgrn_bwd(
        p_all, w["hgrn_lb_logits"], w["hgrn_out_norm"], oraw, states, do, "hgrn_bwd")
    dy, dg, dr_b, dk2_b, dv_b, grads["rwkv_gn_w"], grads["rwkv_gn_b"], d_rk = _rwkv_post_bwd(
        do, y, r, k2, v, g, *post_w, "rwkv_post_bwd")
    grads["rwkv_r_k"] = d_rk.reshape(w["rwkv_r_k"].shape)
    d_sa, dv = _rwkv_scan_bwd(dy, r, decay, k2, sa, sb, "rwkv_scan_bwd")
    dr, dw, dk2, dsa, dsb = _rwkv_scan_bwd_values(dy, r, decay, v, sa, s_a, d_sa, sall, "rwkv_scan_bwd_values")
    (dsr, dsk, dsv, dslo, dw2cat, grads["rwkv_w0"], grads["rwkv_a0"], grads["rwkv_k_k"],
     grads["rwkv_k_a"]) = _rwkv_prep_bwd((dr, dw, dk2, dv, dsa, dsb, dg, dr_b, dk2_b, dv_b), p_all, *rw,
                                         "rwkv_prep_bwd")
    grads["rwkv_w2"] = dw2cat[0:n_w, 0:WB]
    grads["rwkv_a2"] = dw2cat[n_w:n_w + n_a, WB:2 * WB]
    grads["rwkv_g2"] = dw2cat[n_w + n_a:n_w + n_a + n_g, 2 * WB:]
    dpr, dpk, dpv, dplo, dmu_r, dmu_k, dmu_v, dmu_lo = _shift_bwd((dsr, dsk, dsv, dslo), p_all, mu_pad, "shift_bwd")
    grads["rwkv_shift_mu"] = jnp.concatenate([dmu_r, dmu_k, dmu_v, dmu_lo], axis=1)[:, :1696]
    dp = jnp.concatenate([dqa, dfa, dia, dga, dpr, dpk, dpv, dplo], axis=1)
    grads["w_in"] = _matmul(dp, h2, ta=True, out_dtype=WIRE, name="d_w_in")[:N_IN]
    sent = net.send(grads, ("w_in", "rwkv_w2", "rwkv_a2", "rwkv_g2"))
    dh2 = _matmul(dp, w_in_pad, after=sent, name="d_h2")
    dx1, grads["mix_norm"] = _rms_bwd(x1, w["mix_norm"], dh2, dx2, "mix_norm_bwd")
    dx0 = ffn_back("ffn1", h1, dx1, x, w["ffn1_norm"])
    return loss[0, 0], dx0, grads


def kernel(x, ffn1_norm, ffn1_w_gate, ffn1_w_up, ffn1_w_down, mix_norm, w_in, hgrn_lb_logits, hgrn_out_norm, rwkv_shift_mu, rwkv_w0, rwkv_w2, rwkv_a0, rwkv_a2, rwkv_g2, rwkv_k_k, rwkv_k_a, rwkv_r_k, rwkv_gn_w, rwkv_gn_b, w_out, ffn2_norm, ffn2_w_gate, ffn2_w_up, ffn2_w_down, final_norm, loss_target, m_ffn1_norm, m_ffn1_w_gate, m_ffn1_w_up, m_ffn1_w_down, m_mix_norm, m_w_in, m_hgrn_lb_logits, m_hgrn_out_norm, m_rwkv_shift_mu, m_rwkv_w0, m_rwkv_w2, m_rwkv_a0, m_rwkv_a2, m_rwkv_g2, m_rwkv_k_k, m_rwkv_k_a, m_rwkv_r_k, m_rwkv_gn_w, m_rwkv_gn_b, m_w_out, m_ffn2_norm, m_ffn2_w_gate, m_ffn2_w_up, m_ffn2_w_down, m_final_norm, v_ffn1_norm, v_ffn1_w_gate, v_ffn1_w_up, v_ffn1_w_down, v_mix_norm, v_w_in, v_hgrn_lb_logits, v_hgrn_out_norm, v_rwkv_shift_mu, v_rwkv_w0, v_rwkv_w2, v_rwkv_a0, v_rwkv_a2, v_rwkv_g2, v_rwkv_k_k, v_rwkv_k_a, v_rwkv_r_k, v_rwkv_gn_w, v_rwkv_gn_b, v_w_out, v_ffn2_norm, v_ffn2_w_gate, v_ffn2_w_up, v_ffn2_w_down, v_final_norm):
    args = dict(locals())
    wts = {n: args[n] for n in WEIGHTS}
    mom = {n: args["m_" + n] for n in WEIGHTS}
    var = {n: args["v_" + n] for n in WEIGHTS}
    net = _Overlapped(wts)
    loss, grad_x, grads = _local_step(x[0], loss_target[0], net)
    loss = lax.psum(loss, AXES)
    after, = net.send(grads, (), _pack([grads[n] for n in SMALL], 8))

    new = {}
    two_d = lambda a: a if a.ndim == 2 else a.reshape(1, -1)
    for which in range(len(net.sends)):
        for n, (part, mine) in net.received(which, after).items():
            if n == "small":
                small = _adamw_small(part, mine, *([two_d(src[k]) for k in SMALL] for src in (wts, mom, var)),
                                     "adamw_small")
                for j, k in enumerate(SMALL):
                    new[k] = [res[j].reshape(wts[k].shape) for res in small]
            else:
                grad = (part, mine)
                if n in SENT_TRANSPOSED:
                    grad = jnp.swapaxes(_sum_parts(part, mine, "grad_" + n), 0, 1)
                new[n] = _adamw(grad, wts[n], mom[n], var[n], "adamw_" + n)
                after = new[n][1]
    return (loss, grad_x[None], *[new[n][0] for n in WEIGHTS], *[new[n][1] for n in WEIGHTS],
            *[new[n][2] for n in WEIGHTS], *[new[n][3] for n in WEIGHTS])
```

```python
import functools
import math

import jax
import jax.numpy as jnp
from jax import lax
from jax.experimental import pallas as pl
from jax.experimental.pallas import tpu as pltpu

F32 = jnp.float32
MXU = jnp.bfloat16
WIRE = jnp.bfloat16
D = 1024
FF = 2816
WA = 512
WB = 512
HD_B = 64
N_IN = 3744
N_INP = 3840
COL_R, COL_K, COL_V = 4, 5, 6
COL_L = 14
LORA = (32, 32, 96)
HG_CHUNK = 64
HG_STEP_CHUNKS = 4
SCAN_CHUNK = 64
SCAN_UNROLL = 8
NORM_EPS = 1e-6
GN_EPS = 64e-5
L2_EPS = 1e-12
DECAY_C = math.exp(-0.5)
N_DEV = 8
LANES = 128
ADAM_BLOCK_BYTES = 4 * 1024 * 1024
MATMUL_BLOCK_BYTES = 40 * 1024 * 1024
VMEM_LIMIT = 56 * 1024 * 1024
ADAM_LR, ADAM_B1, ADAM_B2, ADAM_EPS, ADAM_WD, ADAM_STEP = 0.001, 0.9, 0.999, 1e-08, 0.01, 10
AXES = ("x", "y", "c")

SHARDED = ("ffn1_w_gate", "ffn1_w_up", "ffn1_w_down", "w_in", "rwkv_w2", "rwkv_a2", "rwkv_g2", "w_out",
           "ffn2_w_gate", "ffn2_w_up", "ffn2_w_down")
COL_SHARDED = {"rwkv_w2", "rwkv_a2", "rwkv_g2"}
SENT_TRANSPOSED = {"ffn1_w_gate", "ffn1_w_up", "w_in", "ffn2_w_gate", "ffn2_w_up"}
SMALL = ("ffn1_norm", "mix_norm", "hgrn_lb_logits", "hgrn_out_norm", "rwkv_shift_mu", "rwkv_w0", "rwkv_a0",
         "rwkv_k_k", "rwkv_k_a", "rwkv_r_k", "rwkv_gn_w", "rwkv_gn_b", "ffn2_norm", "final_norm")
WEIGHTS = ("ffn1_norm", "ffn1_w_gate", "ffn1_w_up", "ffn1_w_down", "mix_norm", "w_in", "hgrn_lb_logits",
           "hgrn_out_norm", "rwkv_shift_mu", "rwkv_w0", "rwkv_w2", "rwkv_a0", "rwkv_a2", "rwkv_g2", "rwkv_k_k",
           "rwkv_k_a", "rwkv_r_k", "rwkv_gn_w", "rwkv_gn_b", "w_out", "ffn2_norm", "ffn2_w_gate", "ffn2_w_up",
           "ffn2_w_down", "final_norm")


def _tile(n, cap):
    if n <= cap:
        return n
    for t in range(cap - cap % LANES, 0, -LANES):
        if n % t == 0:
            return t
    raise ValueError((n, cap))


def _params(n_axes):
    return pltpu.CompilerParams(dimension_semantics=("arbitrary",) * n_axes, vmem_limit_bytes=VMEM_LIMIT)


def _sig(x):
    return jax.nn.sigmoid(x)


def _dsilu(z, s):
    return s * (1.0 + z * (1.0 - s))


def _dot(a, b, dims=((1,), (0,)), precision=None):
    return lax.dot_general(a, b, (dims, ((), ())), preferred_element_type=F32, precision=precision)


_NT = ((1,), (1,))
_TN = ((0,), (0,))
_HI = lax.Precision.HIGH


def _iota(shape, dim):
    return lax.broadcasted_iota(jnp.int32, shape, dim)


def _split_dot(x, ones, passes):
    hi = x.astype(jnp.bfloat16)
    acc = _dot(hi, ones)
    rem = x
    for _ in range(passes - 1):
        rem = rem - hi.astype(F32)
        hi = rem.astype(jnp.bfloat16)
        acc = acc + _dot(hi, ones)
    return acc


def _head_ones(n, width):
    shift = width.bit_length() - 1
    return (_iota((n, n), 0) >> shift == _iota((n, n), 1) >> shift).astype(jnp.bfloat16)


def _matmul(a, b, *, ta=False, tb=False, out_dtype=F32, res=None, after=(), name):
    m, k = (a.shape[1], a.shape[0]) if ta else a.shape
    n = b.shape[0] if tb else b.shape[1]
    tm, tn = _tile(m, 1408), _tile(n, 1408)
    in_bytes = max(a.dtype.itemsize, b.dtype.itemsize)
    for tk in (_tile(k, 1408), _tile(k, 1024), _tile(k, 512), _tile(k, 256)):
        if 2 * (tm + tn) * tk * in_bytes + 3 * tm * tn * 4 <= MATMUL_BLOCK_BYTES:
            break
    nk = k // tk
    dims = ((0 if ta else 1,), (1 if tb else 0,))

    def body(*refs):
        a_ref, b_ref = refs[:2]
        o_ref, acc = refs[-2:]
        kk = pl.program_id(2)

        @pl.when(kk == 0)
        def _():
            acc[...] = jnp.zeros_like(acc)

        acc[...] += _dot(a_ref[...].astype(MXU), b_ref[...].astype(MXU), dims)

        @pl.when(kk == nk - 1)
        def _():
            v = acc[...]
            if res is not None:
                v = v + refs[2][...]
            o_ref[...] = v.astype(out_dtype)

    a_spec = pl.BlockSpec((tk, tm), lambda i, j, kk: (kk, i)) if ta else pl.BlockSpec((tm, tk), lambda i, j, kk: (i, kk))
    b_spec = pl.BlockSpec((tn, tk), lambda i, j, kk: (j, kk)) if tb else pl.BlockSpec((tk, tn), lambda i, j, kk: (kk, j))
    o_spec = pl.BlockSpec((tm, tn), lambda i, j, kk: (i, j))
    ins, specs = [a, b], [a_spec, b_spec]
    if res is not None:
        ins.append(res)
        specs.append(o_spec)
    ins += list(after)
    specs += [pl.BlockSpec(memory_space=pl.ANY)] * len(after)
    return pl.pallas_call(
        body, name=name, grid=(m // tm, n // tn, nk), in_specs=specs, out_specs=o_spec,
        out_shape=jax.ShapeDtypeStruct((m, n), out_dtype), scratch_shapes=[pltpu.VMEM((tm, tn), F32)],
        compiler_params=_params(3))(*ins)


def _rms_fwd(x, g, name):
    t = x.shape[0]
    tb = _tile(t, 512)

    def body(x_ref, g_ref, o_ref):
        xv = x_ref[...]
        rinv = lax.rsqrt(jnp.mean(xv * xv, axis=-1, keepdims=True) + NORM_EPS)
        o_ref[...] = (xv * rinv * g_ref[...]).astype(MXU)

    return pl.pallas_call(
        body, name=name, grid=(t // tb,),
        in_specs=[pl.BlockSpec((tb, D), lambda i: (i, 0)), pl.BlockSpec((1, D), lambda i: (0, 0))],
        out_specs=pl.BlockSpec((tb, D), lambda i: (i, 0)), out_shape=jax.ShapeDtypeStruct((t, D), MXU),
        compiler_params=_params(1))(x, g)


def _rms_bwd(x, g, dh, dres, name, after=()):
    t = x.shape[0]
    tb = _tile(t, 512)

    def body(x_ref, g_ref, dh_ref, dres_ref, *rest):
        dx_ref, dg_ref = rest[-2:]

        @pl.when(pl.program_id(0) == 0)
        def _():
            dg_ref[...] = jnp.zeros_like(dg_ref)

        xv = x_ref[...]
        rinv = lax.rsqrt(jnp.mean(xv * xv, axis=-1, keepdims=True) + NORM_EPS)
        xhat = xv * rinv
        dhv = dh_ref[...]
        dg_ref[...] += jnp.sum(dhv * xhat, axis=0, keepdims=True)
        dxhat = dhv * g_ref[...]
        dx_ref[...] = dres_ref[...] + rinv * (dxhat - xhat * jnp.mean(dxhat * xhat, axis=-1, keepdims=True))

    row = pl.BlockSpec((tb, D), lambda i: (i, 0))
    vec = pl.BlockSpec((1, D), lambda i: (0, 0))
    return pl.pallas_call(
        body, name=name, grid=(t // tb,),
        in_specs=[row, vec, row, row] + [pl.BlockSpec(memory_space=pl.ANY)] * len(after), out_specs=[row, vec],
        out_shape=[jax.ShapeDtypeStruct((t, D), F32), jax.ShapeDtypeStruct((1, D), F32)],
        compiler_params=_params(1))(x, g, dh, dres, *after)


def _loss_head(x, g, target, name):
    t = x.shape[0]
    tb = _tile(t, 512)

    def body(x_ref, g_ref, t_ref, loss_ref, dx_ref, dg_ref):
        @pl.when(pl.program_id(0) == 0)
        def _():
            dg_ref[...] = jnp.zeros_like(dg_ref)
            loss_ref[...] = jnp.zeros_like(loss_ref)

        xv = x_ref[...]
        gv = g_ref[...]
        rinv = lax.rsqrt(jnp.mean(xv * xv, axis=-1, keepdims=True) + NORM_EPS)
        xhat = xv * rinv
        err = xhat * gv - t_ref[...]
        per_tok = jnp.mean(err * err, axis=-1, keepdims=True)
        loss_ref[...] += jnp.broadcast_to(0.5 * jnp.sum(per_tok, axis=0, keepdims=True), loss_ref.shape)
        dy = err * (1.0 / D)
        dg_ref[...] += jnp.sum(dy * xhat, axis=0, keepdims=True)
        dxhat = dy * gv
        dx_ref[...] = rinv * (dxhat - xhat * jnp.mean(dxhat * xhat, axis=-1, keepdims=True))

    row = pl.BlockSpec((tb, D), lambda i: (i, 0))
    vec = pl.BlockSpec((1, D), lambda i: (0, 0))
    return pl.pallas_call(
        body, name=name, grid=(t // tb,), in_specs=[row, vec, row],
        out_specs=[pl.BlockSpec((1, LANES), lambda i: (0, 0)), row, vec],
        out_shape=[jax.ShapeDtypeStruct((1, LANES), F32), jax.ShapeDtypeStruct((t, D), F32),
                   jax.ShapeDtypeStruct((1, D), F32)],
        compiler_params=_params(1))(x, g, target)


def _ffn_fwd(x, h, wg, wu, wd, name, after=()):
    t = x.shape[0]
    tb, fb = _tile(t, 1024), 256

    def body(h_ref, wg_ref, wu_ref, *rest):
        hv = h_ref[...]
        gate = _dot(hv, wg_ref[...], _NT)
        up = _dot(hv, wu_ref[...], _NT)
        rest[-1][...] = (0.5 * (gate * _sig(gate) * up)).astype(MXU)

    hidden = pl.BlockSpec((fb, D), lambda i, f: (f, 0))
    act = pl.pallas_call(
        body, name=name, grid=(t // tb, FF // fb),
        in_specs=[pl.BlockSpec((tb, D), lambda i, f: (i, 0)), hidden, hidden]
        + [pl.BlockSpec(memory_space=pl.ANY)] * len(after), out_specs=pl.BlockSpec((tb, fb), lambda i, f: (i, f)),
        out_shape=jax.ShapeDtypeStruct((t, FF), MXU), compiler_params=_params(2))(h, wg, wu, *after)
    return _matmul(act, wd, res=x, name=name + "_down")


def _ffn_bwd(h, dy, wg, wu, wd, name):
    t = h.shape[0]
    tb, fb = _tile(t, 1024), 256
    nf = FF // fb

    def body(h_ref, dy_ref, wg_ref, wu_ref, wd_ref, act_ref, dg_ref, du_ref, dout_ref):
        hv = h_ref[...]
        dout = (0.5 * dy_ref[...]).astype(MXU)
        dout_ref[...] = dout
        gate = _dot(hv, wg_ref[...], _NT)
        up = _dot(hv, wu_ref[...], _NT)
        dact = _dot(dout, wd_ref[...], _NT)
        s = _sig(gate)
        silu = gate * s
        act_ref[...] = (silu * up).astype(MXU)
        du_ref[...] = (dact * silu).astype(MXU)
        dg_ref[...] = (dact * up * _dsilu(gate, s)).astype(MXU)

    row = pl.BlockSpec((tb, D), lambda i, f: (i, 0))
    hidden = pl.BlockSpec((fb, D), lambda i, f: (f, 0))
    hid = pl.BlockSpec((tb, fb), lambda i, f: (i, f))
    hid_shape = jax.ShapeDtypeStruct((t, FF), MXU)
    return pl.pallas_call(
        body, name=name, grid=(t // tb, nf),
        in_specs=[row, row, hidden, hidden, hidden],
        out_specs=[hid, hid, hid, row],
        out_shape=[hid_shape, hid_shape, hid_shape, jax.ShapeDtypeStruct((t, D), MXU)],
        compiler_params=_params(2))(h, dy, wg, wu, wd)


def _hgrn_chunk(qa, fa, lbl):
    c = HG_CHUNK
    lb = _sig(lbl[0:1, :] - lbl[1:2, :])
    sf = _sig(fa)
    forget = lb + (1.0 - lb) * sf
    kh = 1.0 - forget
    row, col = _iota((c, c), 0), _iota((c, c), 1)
    b = _dot((col <= row).astype(F32), jnp.log(forget), precision=_HI)
    bref, blast = b[c // 2:c // 2 + 1, :], b[c - 1:c, :]
    sq = _sig(qa)
    q = qa * sq
    qt, kt = q * jnp.exp(b - bref), kh * jnp.exp(bref - b)
    qb, kl = q * jnp.exp(b), kh * jnp.exp(blast - b)
    causal = col <= row
    return dict(lb=lb, sf=sf, forget=forget, sq=sq, qt=qt, kt=kt, qb=qb, kl=kl, decay=jnp.exp(blast),
                causal=causal, e_q=jnp.exp(b), e_qt=jnp.exp(b - bref), e_kt=jnp.exp(bref - b),
                e_kl=jnp.exp(blast - b))


def _hgrn_specs(t):
    c = HG_CHUNK
    n = t // c
    return c, n, WA // LANES, HG_STEP_CHUNKS if n % HG_STEP_CHUNKS == 0 else 1


def _hgrn_fwd(p_all, lbl, onorm, name):
    t = p_all.shape[0]
    c, n, nh, m = _hgrn_specs(t)

    def body(q_ref, f_ref, i_ref, g_ref, lbl_ref, on_ref, oa_ref, oraw_ref, st_ref, state):
        @pl.when(pl.program_id(0) == 0)
        def _():
            state[...] = jnp.zeros_like(state)

        heads = [slice(h * LANES, (h + 1) * LANES) for h in range(nh)]
        sts = [state[h] for h in range(nh)]
        for sub in range(m):
            rows = slice(sub * c, (sub + 1) * c)
            ks = [_hgrn_chunk(q_ref[rows, at], f_ref[rows, at], lbl_ref[:, at]) for at in heads]
            vs = [i_ref[rows, at] for at in heads]
            for h in range(nh):
                st_ref[h, sub] = sts[h]
            scores = [jnp.where(k["causal"], _dot(k["qt"], k["kt"], _NT, _HI), 0.0) for k in ks]
            outs = [_dot(a, v, precision=_HI) + _dot(k["qb"], st, _NT, _HI) for a, v, k, st in zip(scores, vs, ks, sts)]
            sts = [st * k["decay"] + _dot(v, k["kl"], _TN, _HI) for st, k, v in zip(sts, ks, vs)]
            for at, o in zip(heads, outs):
                oraw_ref[rows, at] = o
                rinv = lax.rsqrt(jnp.mean(o * o, axis=-1, keepdims=True) + NORM_EPS)
                ga = g_ref[rows, at]
                oa_ref[rows, at] = (o * rinv * on_ref[:, at] * (ga * _sig(ga))).astype(MXU)
        for h in range(nh):
            state[h] = sts[h]

    def blk(j):
        return pl.BlockSpec((m * c, WA), lambda i: (i, j))

    return pl.pallas_call(
        body, name=name, grid=(n // m,),
        in_specs=[blk(0), blk(1), blk(2), blk(3), pl.BlockSpec((2, WA), lambda i: (0, 0)),
                  pl.BlockSpec((1, WA), lambda i: (0, 0))],
        out_specs=[blk(0), blk(0), pl.BlockSpec((nh, m, LANES, LANES), lambda i: (0, i, 0, 0))],
        out_shape=[jax.ShapeDtypeStruct((t, WA), MXU), jax.ShapeDtypeStruct((t, WA), F32),
                   jax.ShapeDtypeStruct((nh, n, LANES, LANES), F32)],
        scratch_shapes=[pltpu.VMEM((nh, LANES, LANES), F32)], compiler_params=_params(1))(
            p_all, p_all, p_all, p_all, lbl, onorm)


def _hgrn_bwd(p_all, lbl, onorm, oraw, states, doa, name):
    t = p_all.shape[0]
    c, n, nh, m = _hgrn_specs(t)

    def body(q_ref, f_ref, i_ref, g_ref, lbl_ref, on_ref, oraw_ref, st_ref, doa_ref,
             dq_ref, df_ref, di_ref, dg_ref, don_ref, dlbl_ref, dstate, dlb):
        @pl.when(pl.program_id(0) == 0)
        def _():
            dstate[...] = jnp.zeros_like(dstate)
            dlb[...] = jnp.zeros_like(dlb)
            don_ref[...] = jnp.zeros_like(don_ref)

        heads = [slice(h * LANES, (h + 1) * LANES) for h in range(nh)]
        dsts = [dstate[h] for h in range(nh)]
        step = _iota((c, LANES), 0)
        row, col = _iota((c, c), 0), _iota((c, c), 1)
        for sub in reversed(range(m)):
            rows = slice(sub * c, (sub + 1) * c)
            work = []
            for h, at in enumerate(heads):
                qa, fa, v, ga = q_ref[rows, at], f_ref[rows, at], i_ref[rows, at], g_ref[rows, at]
                k = _hgrn_chunk(qa, fa, lbl_ref[:, at])
                o = oraw_ref[rows, at]
                gain = on_ref[:, at]
                rinv = lax.rsqrt(jnp.mean(o * o, axis=-1, keepdims=True) + NORM_EPS)
                on = o * rinv
                sg = _sig(ga)
                gate = ga * sg
                dout = doa_ref[rows, at]
                don_ref[:, at] += jnp.sum(dout * on * gate, axis=0, keepdims=True)
                dg_ref[rows, at] = (dout * on * gain * _dsilu(ga, sg)).astype(MXU)
                d_on = dout * gain * gate
                do = rinv * (d_on - on * jnp.mean(d_on * on, axis=-1, keepdims=True))
                work.append(dict(at=at, qa=qa, v=v, k=k, do=do, st=st_ref[h, sub]))
            for x, dst_next in zip(work, dsts):
                k, do = x["k"], x["do"]
                x["a"] = jnp.where(k["causal"], _dot(k["qt"], k["kt"], _NT, _HI), 0.0)
                x["dqb"] = _dot(do, x["st"], precision=_HI)
                x["dst"] = dst_next * k["decay"] + _dot(do, k["qb"], _TN, _HI)
                x["da"] = jnp.where(k["causal"], _dot(do, x["v"], _NT, _HI), 0.0)
            for x, dst_next in zip(work, dsts):
                k = x["k"]
                x["dqt"] = _dot(x["da"], k["kt"], precision=_HI)
                x["dkt"] = _dot(x["da"], k["qt"], _TN, _HI)
                x["dv"] = _dot(x["a"], x["do"], _TN, _HI) + _dot(k["kl"], dst_next, _NT, _HI)
                x["dkl"] = _dot(x["v"], dst_next, precision=_HI)
            for x, dst_next in zip(work, dsts):
                k, at, dqt, dkt, dkl, dqb = x["k"], x["at"], x["dqt"], x["dkt"], x["dkl"], x["dqb"]
                ddecay = jnp.sum(dst_next * x["st"], axis=0, keepdims=True)
                dq = dqb * k["e_q"] + dqt * k["e_qt"]
                dk = dkt * k["e_kt"] + dkl * k["e_kl"]
                tq, tk, tl = dqt * k["qt"], dkt * k["kt"], dkl * k["kl"]
                db = dqb * k["qb"] + tq - tk - tl
                dbref = jnp.sum(tk - tq, axis=0, keepdims=True)
                dblast = jnp.sum(tl, axis=0, keepdims=True) + ddecay * k["decay"]
                db = db + jnp.where(step == c // 2, dbref, 0.0) + jnp.where(step == c - 1, dblast, 0.0)
                dlogf = _dot((col >= row).astype(F32), db, precision=_HI)
                dq_ref[rows, at] = (dq * _dsilu(x["qa"], k["sq"])).astype(MXU)
                di_ref[rows, at] = x["dv"].astype(MXU)
                dforget = dlogf / k["forget"] - dk
                sf, lb = k["sf"], k["lb"]
                df_ref[rows, at] = (dforget * (1.0 - lb) * sf * (1.0 - sf)).astype(MXU)
                dlb[:, at] += jnp.sum(dforget * (1.0 - sf), axis=0, keepdims=True)
                dl0 = dlb[:, at] * lb * (1.0 - lb)
                dlbl_ref[:, at] = jnp.where(_iota((2, LANES), 0) == 0, dl0, -dl0)
            dsts = [x["dst"] for x in work]
        for h in range(nh):
            dstate[h] = dsts[h]

    last = n // m - 1

    def blk(j):
        return pl.BlockSpec((m * c, WA), lambda i: (last - i, j))

    vec = pl.BlockSpec((1, WA), lambda i: (0, 0))
    lg = pl.BlockSpec((2, WA), lambda i: (0, 0))
    grad = jax.ShapeDtypeStruct((t, WA), MXU)
    return pl.pallas_call(
        body, name=name, grid=(n // m,),
        in_specs=[blk(0), blk(1), blk(2), blk(3), lg, vec, blk(0),
                  pl.BlockSpec((nh, m, LANES, LANES), lambda i: (0, last - i, 0, 0)), blk(0)],
        out_specs=[blk(0), blk(0), blk(0), blk(0), vec, lg],
        out_shape=[grad, grad, grad, grad, jax.ShapeDtypeStruct((1, WA), F32), jax.ShapeDtypeStruct((2, WA), F32)],
        scratch_shapes=[pltpu.VMEM((nh, LANES, LANES), F32), pltpu.VMEM((1, WA), F32)],
        compiler_params=_params(1))(p_all, p_all, p_all, p_all, lbl, onorm, oraw, states, doa)


def _lora_act(x):
    lane = _iota(x.shape, 1)
    n_w, n_a, n_g = LORA
    return jnp.where(lane < n_w, jnp.tanh(x),
                     jnp.where(lane < n_w + n_a, x, jnp.where(lane < n_w + n_a + n_g, _sig(x), 0.0)))


def _lora_dact(x):
    lane = _iota(x.shape, 1)
    n_w, n_a, n_g = LORA
    th, s = jnp.tanh(x), _sig(x)
    return jnp.where(lane < n_w, 1.0 - th * th,
                     jnp.where(lane < n_w + n_a, 1.0, jnp.where(lane < n_w + n_a + n_g, s * (1.0 - s), 0.0)))


def _shift_down(cur, prev8, first):
    rolled = pltpu.roll(cur, 1, 0)
    edge = prev8[7:8, :] * jnp.where(first, 0.0, 1.0)
    return jnp.where(_iota(cur.shape, 0) == 0, edge, rolled)


def _shift_up(cur, next8, last):
    rows = cur.shape[0]
    rolled = pltpu.roll(cur, rows - 1, 0)
    edge = next8[0:1, :] * jnp.where(last, 0.0, 1.0)
    return jnp.where(_iota(cur.shape, 0) == rows - 1, edge, rolled)


def _rwkv_inputs(refs, first, ones):
    (pr, pk, pv, plo, qr, qk, qv, qlo, mr, mk, mv, mlo, w2c, w0, a0, kk_w, ka_w) = refs
    mix = lambda cur, prev, mu: cur[...] + mu[...] * (_shift_down(cur[...], prev[...], first) - cur[...])
    r, k, v, lo = mix(pr, qr, mr), mix(pk, qk, mk), mix(pv, qv, mv), mix(plo, qlo, mlo)
    z = _lora_act(lo)
    lin = _dot(z.astype(MXU), w2c[...])
    sg = _sig(w0[...] + lin[:, :WB])
    decay = jnp.exp(-DECAY_C * sg)
    a = _sig(a0[...] + lin[:, WB:2 * WB])
    g = lin[:, 2 * WB:]
    kk0 = k * kk_w[...]
    nrm = jnp.sqrt(_split_dot(kk0 * kk0, ones, 3))
    den = jnp.maximum(nrm, L2_EPS)
    kk = kk0 / den
    k2 = k * (1.0 + (a - 1.0) * ka_w[...])
    return dict(r=r, k=k, v=v, lo=lo, z=z, sg=sg, decay=decay, a=a, g=g, kk=kk, den=den, nrm=nrm, k2=k2)


def _rwkv_in_specs(t, tb):
    nt8 = tb // 8

    def cur(w, j):
        return pl.BlockSpec((tb, w), lambda i: (i, j))

    def prev(w, j):
        return pl.BlockSpec((8, w), lambda i: (jnp.maximum(i * nt8 - 1, 0), j))

    def vec(w, j=0):
        return pl.BlockSpec((1, w), lambda i: (0, j))

    return [cur(WB, COL_R), cur(WB, COL_K), cur(WB, COL_V), cur(256, COL_L),
            prev(WB, COL_R), prev(WB, COL_K), prev(WB, COL_V), prev(256, COL_L),
            vec(WB, 0), vec(WB, 1), vec(WB, 2), vec(256, 6),
            pl.BlockSpec((256, 3 * WB), lambda i: (0, 0)), vec(WB), vec(WB), vec(WB), vec(WB)]


def _rwkv_in_args(p_all, mu_pad, w2cat, w0, a0, k_k, k_a):
    return (p_all,) * 8 + (mu_pad,) * 4 + (w2cat, w0, a0, k_k, k_a)


def _rwkv_prep(p_all, mu_pad, w2cat, w0, a0, k_k, k_a, name):
    t = p_all.shape[0]
    tb = _tile(t, 256)

    def body(*refs):
        ins, outs = refs[:17], refs[17:]
        q = _rwkv_inputs(ins, pl.program_id(0) == 0, _head_ones(WB, HD_B))
        for ref, val in zip(outs, (q["r"], q["decay"], q["k2"], q["v"], -q["kk"], q["kk"] * q["a"], q["g"])):
            ref[...] = val

    out = pl.BlockSpec((tb, WB), lambda i: (i, 0))
    return pl.pallas_call(
        body, name=name, grid=(t // tb,), in_specs=_rwkv_in_specs(t, tb), out_specs=[out] * 7,
        out_shape=[jax.ShapeDtypeStruct((t, WB), F32)] * 7, compiler_params=_params(1))(
            *_rwkv_in_args(p_all, mu_pad, w2cat, w0, a0, k_k, k_a))


def _pair_rows(x8, i):
    return jnp.concatenate([jnp.broadcast_to(x8[i:i + 1, p * LANES:(p + 1) * LANES], (HD_B, LANES))
                            for p in range(4)], axis=0)


def _pair_sums(x):
    return jnp.concatenate([jnp.sum(x[p * HD_B:(p + 1) * HD_B], axis=0, keepdims=True) for p in range(4)], axis=1)


def _put_row(buf, i, row):
    return jnp.where(_iota(buf.shape, 0) == i, row, buf)


def _pieces(x):
    hi = x.astype(jnp.bfloat16).astype(F32)
    lo = (x - hi).astype(jnp.bfloat16).astype(F32)
    upper = (_iota((x.shape[0], LANES), 1) & (HD_B // 2)) != 0
    swapped = [jnp.where(upper, pltpu.roll(lo[:, p * LANES:(p + 1) * LANES], HD_B // 2, 1),
                         pltpu.roll(lo[:, p * LANES:(p + 1) * LANES], LANES - HD_B // 2, 1)) for p in range(4)]
    return hi, jnp.concatenate(swapped, axis=1)


def _scan_consts():
    row, lane = _iota((HD_B, LANES), 0), _iota((HD_B, LANES), 1) & (HD_B - 1)
    either = ((row ^ lane) & (HD_B // 2 - 1)) == 0
    return ((row ^ lane) & (HD_B // 2)) != 0, either.astype(jnp.bfloat16), _head_ones(LANES, HD_B)


def _pair_cols(many, consts):
    swapped, either, ones = consts
    tiles = []
    for (hi8, lo8), i in many:
        for p in range(4):
            lanes = slice(p * LANES, (p + 1) * LANES)
            hi = jnp.broadcast_to(hi8[i:i + 1, lanes], (16, LANES)).astype(jnp.bfloat16)
            lo = jnp.broadcast_to(lo8[i:i + 1, lanes], (16, LANES)).astype(jnp.bfloat16)
            for g in range(HD_B // 16):
                rows = slice(g * 16, (g + 1) * 16)
                tiles.append(jnp.where(swapped[rows], lo, hi) * either[rows])
    out = _dot(jnp.concatenate(tiles, axis=0), ones)
    return [out[m * 4 * HD_B:(m + 1) * 4 * HD_B] for m in range(len(many))]


def _block_products(w8):
    rows = _iota(w8.shape, 0)
    down, up = w8, w8
    for shift in (1, 2, 4):
        down = down * jnp.where(rows >= shift, pltpu.roll(down, shift, 0), 1.0)
        up = up * jnp.where(rows < 8 - shift, pltpu.roll(up, 8 - shift, 0), 1.0)
    return down, up


def _blocked_loop(n_blocks, prepare, advance, init):
    unroll = SCAN_UNROLL if n_blocks % SCAN_UNROLL == 0 else 1

    def trip(g, carry):
        prepared = [prepare(g * unroll + i) for i in range(unroll)]
        for p in prepared:
            carry = advance(p, carry)
        return carry

    return lax.fori_loop(0, n_blocks // unroll, trip, init)


def _rwkv_scan_fwd(r, w, k, v, a, b, name):
    t = r.shape[0]
    cc = min(t, SCAN_CHUNK)

    def body(r_ref, w_ref, k_ref, v_ref, a_ref, b_ref, y_ref, sa_ref, state):
        @pl.when(pl.program_id(0) == 0)
        def _():
            state[...] = jnp.zeros_like(state)

        consts = _scan_consts()

        def prepare(j):
            rows = pl.ds(pl.multiple_of(j * 8, 8), 8)
            r8, w8, k8, v8, a8, b8 = (ref[rows, :] for ref in (r_ref, w_ref, k_ref, v_ref, a_ref, b_ref))
            decay, _ = _block_products(w8)
            before = jnp.where(_iota(w8.shape, 0) == 0, 1.0, pltpu.roll(decay, 1, 0))
            inv = 1.0 / decay
            scaled = [_pieces(x) for x in (a8 * before, b8 * inv, k8 * inv, r8 * decay)]
            return rows, v8, _pair_cols([(x, i) for i in range(8) for x in scaled] + [(_pieces(decay), 7)], consts)

        def advance(prepared, sk):
            rows, v8, cols = prepared
            y8 = jnp.zeros((8, WB), F32)
            sa8 = jnp.zeros((8, WB), F32)
            for i in range(8):
                a_c, b_c, k_c, r_c = cols[4 * i:4 * i + 4]
                sa = _pair_sums(sk * a_c)
                sk = sk + b_c * _pair_rows(sa, 0) + k_c * _pair_rows(v8, i)
                y8 = _put_row(y8, i, _pair_sums(sk * r_c))
                sa8 = _put_row(sa8, i, sa)
            y_ref[rows, :] = y8
            sa_ref[rows, :] = sa8
            return sk * cols[-1]

        state[...] = _blocked_loop(cc // 8, prepare, advance, state[...])

    row = pl.BlockSpec((cc, WB), lambda i: (i, 0))
    return pl.pallas_call(
        body, name=name, grid=(t // cc,), in_specs=[row] * 6, out_specs=[row, row],
        out_shape=[jax.ShapeDtypeStruct((t, WB), F32)] * 2,
        scratch_shapes=[pltpu.VMEM((4 * HD_B, LANES), F32)], compiler_params=_params(1))(r, w, k, v, a, b)


def _rwkv_states(sa, w, k, v, b, name):
    t = sa.shape[0]
    cc = min(t, SCAN_CHUNK)

    def body(sa_ref, w_ref, k_ref, v_ref, b_ref, sall_ref, state):
        @pl.when(pl.program_id(0) == 0)
        def _():
            state[...] = jnp.zeros_like(state)

        consts = _scan_consts()

        def prepare(j):
            base = pl.multiple_of(j * 8, 8)
            sa8, w8, k8, v8, b8 = (ref[pl.ds(base, 8), :] for ref in (sa_ref, w_ref, k_ref, v_ref, b_ref))
            sap, vp = _pieces(sa8), _pieces(v8)
            return base, w8, k8, b8, _pair_cols([(x, i) for i in range(8) for x in (sap, vp)], consts)

        def advance(prepared, sv):
            base, w8, k8, b8, cols = prepared
            for i in range(8):
                sv = sv * _pair_rows(w8, i) + cols[2 * i] * _pair_rows(b8, i) + cols[2 * i + 1] * _pair_rows(k8, i)
                sall_ref[base + i] = sv
            return sv

        state[...] = _blocked_loop(cc // 8, prepare, advance, state[...])

    row = pl.BlockSpec((cc, WB), lambda i: (i, 0))
    return pl.pallas_call(
        body, name=name, grid=(t // cc,), in_specs=[row] * 5,
        out_specs=pl.BlockSpec((cc, 4 * HD_B, LANES), lambda i: (i, 0, 0)),
        out_shape=jax.ShapeDtypeStruct((t, 4 * HD_B, LANES), F32),
        scratch_shapes=[pltpu.VMEM((4 * HD_B, LANES), F32)], compiler_params=_params(1))(sa, w, k, v, b)


def _rwkv_scan_bwd(dy, r, w, k, a, b, name):
    t = r.shape[0]
    cc = min(t, SCAN_CHUNK)
    n = t // cc

    def body(dy_ref, r_ref, w_ref, k_ref, a_ref, b_ref, dsa_ref, dv_ref, dstate):
        @pl.when(pl.program_id(0) == 0)
        def _():
            dstate[...] = jnp.zeros_like(dstate)

        consts = _scan_consts()

        steps = range(7, -1, -1)

        def prepare(jj):
            rows = pl.ds(pl.multiple_of((cc // 8 - 1 - jj) * 8, 8), 8)
            dy8, r8, w8, k8, a8, b8 = (ref[rows, :] for ref in (dy_ref, r_ref, w_ref, k_ref, a_ref, b_ref))
            _, upto = _block_products(w8)
            later = jnp.where(_iota(w8.shape, 0) == 7, 1.0, pltpu.roll(upto, 7, 0))
            scaled = [_pieces(x) for x in (r8 / later, b8 * later, k8 * later, a8 / upto)]
            return rows, dy8, _pair_cols([(x, i) for i in steps for x in scaled] + [(_pieces(upto), 0)], consts)

        def advance(prepared, ds):
            rows, dy8, cols = prepared
            dsa8, dv8 = jnp.zeros((8, WB), F32), jnp.zeros((8, WB), F32)
            for n_done, i in enumerate(steps):
                r_c, b_c, k_c, a_c = cols[4 * n_done:4 * n_done + 4]
                ds = ds + r_c * _pair_rows(dy8, i)
                dsa = _pair_sums(ds * b_c)
                dv8 = _put_row(dv8, i, _pair_sums(ds * k_c))
                dsa8 = _put_row(dsa8, i, dsa)
                ds = ds + a_c * _pair_rows(dsa, 0)
            dsa_ref[rows, :] = dsa8
            dv_ref[rows, :] = dv8
            return ds * cols[-1]

        dstate[...] = _blocked_loop(cc // 8, prepare, advance, dstate[...])

    row = pl.BlockSpec((cc, WB), lambda i: (n - 1 - i, 0))
    return pl.pallas_call(
        body, name=name, grid=(n,), in_specs=[row] * 6, out_specs=[row] * 2,
        out_shape=[jax.ShapeDtypeStruct((t, WB), F32)] * 2,
        scratch_shapes=[pltpu.VMEM((4 * HD_B, LANES), F32)], compiler_params=_params(1))(dy, r, w, k, a, b)


def _rwkv_scan_bwd_values(dy, r, w, v, a, sa, dsa, sall, name):
    t = r.shape[0]
    cc = min(t, SCAN_CHUNK)
    n = t // cc

    def body(dy_ref, r_ref, w_ref, v_ref, a_ref, sa_ref, dsa_ref, sall_ref, sprev_ref,
             dr_ref, dw_ref, dk_ref, da_ref, db_ref, dstate):
        @pl.when(pl.program_id(0) == 0)
        def _():
            dstate[...] = jnp.zeros_like(dstate)

        consts = _scan_consts()
        before_chunk = jnp.where(pl.program_id(0) == n - 1, 0.0, 1.0) * sprev_ref[0]

        steps = range(7, -1, -1)

        def prepare(jj):
            j = cc // 8 - 1 - jj
            base = pl.multiple_of(j * 8, 8)
            dy8, r8, w8, v8, a8, sa8, dsa8 = (ref[pl.ds(base, 8), :] for ref in
                                              (dy_ref, r_ref, w_ref, v_ref, a_ref, sa_ref, dsa_ref))
            dyp, vp, sap, dsap = (_pieces(x) for x in (dy8, v8, sa8, dsa8))
            return j, base, r8, w8, a8, _pair_cols([(x, i) for i in steps for x in (dyp, vp, sap, dsap)], consts)

        def advance(prepared, carry):
            ds, sc = carry
            j, base, r8, w8, a8, cols = prepared
            rows = pl.ds(base, 8)
            outs = [jnp.zeros((8, WB), F32) for _ in range(5)]
            for n_done, i in enumerate(steps):
                if i > 0:
                    sp = sall_ref[base + i - 1]
                else:
                    sp = jnp.where(j == 0, before_chunk, sall_ref[jnp.maximum(base - 1, 0)])
                dy_c, v_c, sa_c, dsa_c = cols[4 * n_done:4 * n_done + 4]
                ds = ds + dy_c * _pair_rows(r8, i)
                vals = (_pair_sums(sc * dy_c), _pair_sums(ds * sp), _pair_sums(ds * v_c),
                        _pair_sums(sp * dsa_c), _pair_sums(ds * sa_c))
                outs = [_put_row(o, i, val) for o, val in zip(outs, vals)]
                ds = ds * _pair_rows(w8, i) + dsa_c * _pair_rows(a8, i)
                sc = sp
            for ref, o in zip((dr_ref, dw_ref, dk_ref, da_ref, db_ref), outs):
                ref[rows, :] = o
            return ds, sc

        ds, _ = _blocked_loop(cc // 8, prepare, advance, (dstate[...], sall_ref[cc - 1]))
        dstate[...] = ds

    row = pl.BlockSpec((cc, WB), lambda i: (n - 1 - i, 0))
    return pl.pallas_call(
        body, name=name, grid=(n,),
        in_specs=[row] * 7 + [pl.BlockSpec((cc, 4 * HD_B, LANES), lambda i: (n - 1 - i, 0, 0)),
                              pl.BlockSpec((1, 4 * HD_B, LANES), lambda i: (jnp.maximum((n - 1 - i) * cc - 1, 0), 0, 0))],
        out_specs=[row] * 5, out_shape=[jax.ShapeDtypeStruct((t, WB), F32)] * 5,
        scratch_shapes=[pltpu.VMEM((4 * HD_B, LANES), F32)], compiler_params=_params(1))(
            dy, r, w, v, a, sa, dsa, sall, sall)


def _rwkv_post(y, r, k2, v, g, r_k, gn_w, gn_b, name):
    t = y.shape[0]
    tb = _tile(t, 256)

    def body(y_ref, r_ref, k_ref, v_ref, g_ref, rk_ref, gw_ref, gb_ref, o_ref):
        ones = _head_ones(WB, HD_B)
        yv = y_ref[...]
        yc = yv - _split_dot(yv, ones, 3) * (1.0 / HD_B)
        rstd = lax.rsqrt(_split_dot(yc * yc, ones, 3) * (1.0 / HD_B) + GN_EPS)
        rk = _split_dot(r_ref[...] * k_ref[...] * rk_ref[...], ones, 3)
        o_ref[...] = ((yc * rstd * gw_ref[...] + gb_ref[...] + rk * v_ref[...]) * g_ref[...]).astype(MXU)

    row = pl.BlockSpec((tb, WB), lambda i: (i, 0))
    vec = pl.BlockSpec((1, WB), lambda i: (0, 0))
    return pl.pallas_call(
        body, name=name, grid=(t // tb,), in_specs=[row] * 5 + [vec] * 3, out_specs=row,
        out_shape=jax.ShapeDtypeStruct((t, WB), MXU), compiler_params=_params(1))(y, r, k2, v, g, r_k, gn_w, gn_b)


def _rwkv_post_bwd(dob, y, r, k2, v, g, r_k, gn_w, gn_b, name):
    t = y.shape[0]
    tb = _tile(t, 256)

    def body(do_ref, y_ref, r_ref, k_ref, v_ref, g_ref, rk_ref, gw_ref, gb_ref,
             dy_ref, dg_ref, dr_ref, dk_ref, dv_ref, dgw_ref, dgb_ref, drk_ref):
        @pl.when(pl.program_id(0) == 0)
        def _():
            dgw_ref[...] = jnp.zeros_like(dgw_ref)
            dgb_ref[...] = jnp.zeros_like(dgb_ref)
            drk_ref[...] = jnp.zeros_like(drk_ref)

        ones = _head_ones(WB, HD_B)
        seg = lambda x: _split_dot(x, ones, 3)
        yv, rv, kv, vv, gv = y_ref[...], r_ref[...], k_ref[...], v_ref[...], g_ref[...]
        yc = yv - seg(yv) * (1.0 / HD_B)
        rstd = lax.rsqrt(seg(yc * yc) * (1.0 / HD_B) + GN_EPS)
        yn = yc * rstd
        rk = seg(rv * kv * rk_ref[...])
        dob_v = do_ref[...]
        dg_ref[...] = dob_v * (yn * gw_ref[...] + gb_ref[...] + rk * vv)
        dyg = dob_v * gv
        dgw_ref[...] += jnp.sum(dyg * yn, axis=0, keepdims=True)
        dgb_ref[...] += jnp.sum(dyg, axis=0, keepdims=True)
        dyn = dyg * gw_ref[...]
        dy_ref[...] = rstd * (dyn - (seg(dyn) + yn * seg(dyn * yn)) * (1.0 / HD_B))
        drk = seg(dyg * vv)
        dv_ref[...] = dyg * rk
        dr_ref[...] = drk * kv * rk_ref[...]
        dk_ref[...] = drk * rv * rk_ref[...]
        drk_ref[...] += jnp.sum(drk * rv * kv, axis=0, keepdims=True)

    row = pl.BlockSpec((tb, WB), lambda i: (i, 0))
    vec = pl.BlockSpec((1, WB), lambda i: (0, 0))
    full, small = jax.ShapeDtypeStruct((t, WB), F32), jax.ShapeDtypeStruct((1, WB), F32)
    return pl.pallas_call(
        body, name=name, grid=(t // tb,),
        in_specs=[pl.BlockSpec((tb, WB), lambda i: (i, dob.shape[1] // WB - 1))] + [row] * 5 + [vec] * 3,
        out_specs=[row] * 5 + [vec] * 3,
        out_shape=[full] * 5 + [small] * 3, compiler_params=_params(1))(dob, y, r, k2, v, g, r_k, gn_w, gn_b)


def _rwkv_prep_bwd(grads, p_all, mu_pad, w2cat, w0, a0, k_k, k_a, name):
    t = p_all.shape[0]
    tb = _tile(t, 256)

    def body(*refs):
        g_refs, ins, outs = refs[:10], refs[10:27], refs[27:]
        dr_s, dw, dk2_s, dv_s, das, dbs, dg, dr_b, dk2_b, dv_b = (ref[...] for ref in g_refs)
        dr_ref, dk_ref, dv_ref, dlo_ref, dw2_ref, dw0_ref, da0_ref, dkk_ref, dka_ref = outs

        @pl.when(pl.program_id(0) == 0)
        def _():
            for ref in (dw2_ref, dw0_ref, da0_ref, dkk_ref, dka_ref):
                ref[...] = jnp.zeros_like(ref)

        ones = _head_ones(WB, HD_B)
        q = _rwkv_inputs(ins, pl.program_id(0) == 0, ones)
        kk_w, ka_w = ins[15][...], ins[16][...]
        a, kk, k = q["a"], q["kk"], q["k"]
        dk2 = dk2_s + dk2_b
        dkk = dbs * a - das
        da = dbs * kk + dk2 * k * ka_w
        dk = dk2 * (1.0 + (a - 1.0) * ka_w)
        dka_ref[...] += jnp.sum(dk2 * k * (a - 1.0), axis=0, keepdims=True)
        proj = jnp.where(q["nrm"] > L2_EPS, _split_dot(dkk * kk, ones, 3), 0.0)
        dkk0 = (dkk - kk * proj) / q["den"]
        dk = dk + dkk0 * kk_w
        dkk_ref[...] += jnp.sum(dkk0 * k, axis=0, keepdims=True)
        dal = da * a * (1.0 - a)
        da0_ref[...] += jnp.sum(dal, axis=0, keepdims=True)
        sg = q["sg"]
        dwl = dw * q["decay"] * (-DECAY_C) * sg * (1.0 - sg)
        dw0_ref[...] += jnp.sum(dwl, axis=0, keepdims=True)
        dlin = jnp.concatenate([dwl, dal, dg], axis=1).astype(MXU)
        dw2_ref[...] += _dot(q["z"].astype(MXU), dlin, _TN)
        dz = _dot(dlin, ins[12][...], _NT)
        dlo_ref[...] = dz * _lora_dact(q["lo"])
        dr_ref[...] = dr_s + dr_b
        dk_ref[...] = dk
        dv_ref[...] = dv_s + dv_b

    row = pl.BlockSpec((tb, WB), lambda i: (i, 0))
    vec = pl.BlockSpec((1, WB), lambda i: (0, 0))
    full, small = jax.ShapeDtypeStruct((t, WB), F32), jax.ShapeDtypeStruct((1, WB), F32)
    return pl.pallas_call(
        body, name=name, grid=(t // tb,), in_specs=[row] * 10 + _rwkv_in_specs(t, tb),
        out_specs=[row] * 3 + [pl.BlockSpec((tb, 256), lambda i: (i, 0)),
                               pl.BlockSpec((256, 3 * WB), lambda i: (0, 0))] + [vec] * 4,
        out_shape=[full] * 3 + [jax.ShapeDtypeStruct((t, 256), F32), jax.ShapeDtypeStruct((256, 3 * WB), F32)]
        + [small] * 4, compiler_params=_params(1))(*grads, *_rwkv_in_args(p_all, mu_pad, w2cat, w0, a0, k_k, k_a))


def _shift_bwd(dshifted, p_all, mu_pad, name):
    t = p_all.shape[0]
    tb = _tile(t, 256)
    nt, nt8 = t // tb, tb // 8
    widths, cols, mus = (WB, WB, WB, 256), (COL_R, COL_K, COL_V, COL_L), (0, 1, 2, 6)

    def body(*refs):
        d_refs, n_refs, p_refs, q_refs, m_refs = refs[0:4], refs[4:8], refs[8:12], refs[12:16], refs[16:20]
        o_refs, dmu_refs = refs[20:24], refs[24:28]
        i = pl.program_id(0)

        @pl.when(i == 0)
        def _():
            for ref in dmu_refs:
                ref[...] = jnp.zeros_like(ref)

        for d, nx, p, q, m, o, dmu in zip(d_refs, n_refs, p_refs, q_refs, m_refs, o_refs, dmu_refs):
            dv, pv, mu = d[...], p[...], m[...]
            o[...] = (dv * (1.0 - mu) + mu * _shift_up(dv, nx[...], i == nt - 1)).astype(MXU)
            dmu[...] += jnp.sum(dv * (_shift_down(pv, q[...], i == 0) - pv), axis=0, keepdims=True)

    cur_d = [pl.BlockSpec((tb, w), lambda i: (i, 0)) for w in widths]
    next_d = [pl.BlockSpec((8, w), lambda i: (jnp.minimum((i + 1) * nt8, t // 8 - 1), 0)) for w in widths]
    cur_p = [pl.BlockSpec((tb, w), lambda i, j=j: (i, j)) for w, j in zip(widths, cols)]
    prev_p = [pl.BlockSpec((8, w), lambda i, j=j: (jnp.maximum(i * nt8 - 1, 0), j)) for w, j in zip(widths, cols)]
    mu_s = [pl.BlockSpec((1, w), lambda i, j=j: (0, j)) for w, j in zip(widths, mus)]
    vecs = [pl.BlockSpec((1, w), lambda i: (0, 0)) for w in widths]
    return pl.pallas_call(
        body, name=name, grid=(nt,), in_specs=cur_d + next_d + cur_p + prev_p + mu_s, out_specs=cur_d + vecs,
        out_shape=[jax.ShapeDtypeStruct((t, w), MXU) for w in widths]
        + [jax.ShapeDtypeStruct((1, w), F32) for w in widths],
        compiler_params=_params(1))(*dshifted, *dshifted, *(p_all,) * 8, *(mu_pad,) * 4)


def _peer(k):
    x, y, c = (lax.axis_index(n) for n in AXES)
    px = 1 - x if k & 4 else x
    py = 1 - y if k & 2 else y
    pc = 1 - c if k & 1 else c
    return (px, py, pc), 4 * px + 2 * py + pc


def _exchange_copy(src_refs, land_refs, send_sems, recv_sems, per_peer, j, k, arriving):
    _, me = _peer(0)
    peer, idx = _peer(k)
    sem = j * (N_DEV - 1) + k - 1
    return pltpu.make_async_remote_copy(
        src_ref=src_refs[j].at[idx] if per_peer[j] else src_refs[j],
        dst_ref=land_refs[j].at[idx if arriving else me],
        send_sem=send_sems.at[sem], recv_sem=recv_sems.at[sem],
        device_id=peer, device_id_type=pl.DeviceIdType.MESH)


def _exchange_start(srcs, per_peer, name, after=()):
    n = len(srcs)
    shapes = [tuple(s.shape[1:]) if pp else tuple(s.shape) for s, pp in zip(srcs, per_peer)]
    pairs = [(j, k) for k in range(1, N_DEV) for j in range(n)]
    first_out = 2 * n + len(after)

    def body(*refs):
        src_refs, land_refs, (send_sems, recv_sems), token = refs[:n], refs[n:2 * n], refs[first_out:first_out + 2], refs[-1]
        for j, k in pairs:
            _exchange_copy(src_refs, land_refs, send_sems, recv_sems, per_peer, j, k, False).start()
        token[...] = jnp.zeros_like(token)

    hbm, sem = pl.BlockSpec(memory_space=pltpu.HBM), pl.BlockSpec(memory_space=pltpu.SEMAPHORE)
    lands = [lax.empty((N_DEV,) + shp, s.dtype) for shp, s in zip(shapes, srcs)]
    operands = [pltpu.with_memory_space_constraint(a, pltpu.HBM) for a in list(srcs) + lands]
    n_sems = n * (N_DEV - 1)
    out = pl.pallas_call(
        body, name=name, in_specs=[hbm] * (2 * n) + [pl.BlockSpec(memory_space=pl.ANY)] * len(after),
        out_specs=[sem, sem] + [hbm] * (2 * n) + [pl.BlockSpec(memory_space=pltpu.VMEM)],
        out_shape=[pltpu.SemaphoreType.DMA((n_sems,)), pltpu.SemaphoreType.DMA((n_sems,))]
        + [pltpu.HBM(a.shape, a.dtype) for a in operands] + [jax.ShapeDtypeStruct((8, LANES), F32)],
        input_output_aliases={j: 2 + j for j in range(2 * n)},
        compiler_params=pltpu.CompilerParams(has_side_effects=pltpu.SideEffectType.DATAFLOW_SIDE_EFFECTING))(
            *operands, *after)
    return (out[0], out[1], out[2:2 + n], out[2 + n:2 + 2 * n], per_peer), out[-1]


def _exchange_wait(handle, after, name):
    send_sems, recv_sems, srcs, lands, per_peer = handle
    n = len(srcs)
    pairs = [(j, k) for k in range(1, N_DEV) for j in range(n)]

    def body(*refs):
        src_refs, land_refs, (send_sems, recv_sems) = refs[:n], refs[n:2 * n], refs[2 * n:2 * n + 2]
        for j, k in pairs:
            _exchange_copy(src_refs, land_refs, send_sems, recv_sems, per_peer, j, k, False).wait_send()
            _exchange_copy(src_refs, land_refs, send_sems, recv_sems, per_peer, j, k, True).wait_recv()

    hbm, sem = pl.BlockSpec(memory_space=pltpu.HBM), pl.BlockSpec(memory_space=pltpu.SEMAPHORE)
    out = pl.pallas_call(
        body, name=name, in_specs=[hbm] * (2 * n) + [sem, sem, pl.BlockSpec(memory_space=pl.ANY)],
        out_specs=[hbm] * (2 * n), out_shape=[pltpu.HBM(a.shape, a.dtype) for a in list(srcs) + list(lands)],
        input_output_aliases={j: j for j in range(2 * n)},
        compiler_params=pltpu.CompilerParams(has_side_effects=pltpu.SideEffectType.DATAFLOW_SIDE_EFFECTING))(
            *srcs, *lands, send_sems, recv_sems, after)
    return out[n:]


def _adam_update(g, w, m, v):
    c1, c2 = 1.0 - ADAM_B1 ** ADAM_STEP, 1.0 - ADAM_B2 ** ADAM_STEP
    nm = ADAM_B1 * m + (1.0 - ADAM_B1) * g
    nv = ADAM_B2 * v + (1.0 - ADAM_B2) * (g * g)
    return -ADAM_LR * ((nm / c1) / (jnp.sqrt(nv / c2) + ADAM_EPS) + ADAM_WD * w), nm, nv


def _row_tile(rows, cols):
    padded = -(-cols // LANES) * LANES
    cap = max(16, ADAM_BLOCK_BYTES // (N_DEV * padded * 4))
    best = rows
    for t in range(16, min(rows, cap) + 1, 16):
        if rows % t == 0:
            best = t
    return best


def _sum_in_device_order(received, mine):
    x, y, c = (lax.axis_index(n) for n in AXES)
    me = 4 * x + 2 * y + c
    total = None
    for d in range(N_DEV):
        term = jnp.where(me == d, mine, received(d)).astype(F32)
        total = term if total is None else total + term
    return total


def _sum_parts(parts, mine, name):
    _, rows, cols = parts.shape
    tb = _row_tile(rows, cols)

    def body(p_ref, mine_ref, g_ref):
        g_ref[...] = _sum_in_device_order(lambda d: p_ref[d], mine_ref[...])

    tile = pl.BlockSpec((tb, cols), lambda i: (i, 0))
    return pl.pallas_call(
        body, name=name, grid=(rows // tb,), in_specs=[pl.BlockSpec((N_DEV, tb, cols), lambda i: (0, i, 0)), tile],
        out_specs=tile, out_shape=jax.ShapeDtypeStruct((rows, cols), F32), compiler_params=_params(1))(parts, mine)


def _adamw(grad, w, m, v, name):
    _, rows, cols = w.shape
    tb = _row_tile(rows, cols)
    summed = not isinstance(grad, tuple)

    def body(*refs):
        w_ref, m_ref, v_ref, g_ref, d_ref, nm_ref, nv_ref = refs[-7:]
        g = refs[0][...] if summed else _sum_in_device_order(lambda d: refs[0][d], refs[1][...])
        g_ref[0] = g
        d_ref[0], nm_ref[0], nv_ref[0] = _adam_update(g, w_ref[0], m_ref[0], v_ref[0])

    row = pl.BlockSpec((1, tb, cols), lambda i: (0, i, 0))
    tile = pl.BlockSpec((tb, cols), lambda i: (i, 0))
    grad_specs = [tile] if summed else [pl.BlockSpec((N_DEV, tb, cols), lambda i: (0, i, 0)), tile]
    out = jax.ShapeDtypeStruct(w.shape, F32)
    return pl.pallas_call(
        body, name=name, grid=(rows // tb,), in_specs=grad_specs + [row, row, row], out_specs=[row] * 4,
        out_shape=[out] * 4, compiler_params=_params(1))(*((grad,) if summed else grad), w, m, v)


def _adamw_small(parts, mine, ws, ms, vs, name):
    n = len(ws)

    def body(*refs):
        p_ref, mine_ref = refs[:2]
        w_refs, m_refs, v_refs = refs[2:2 + n], refs[2 + n:2 + 2 * n], refs[2 + 2 * n:2 + 3 * n]
        outs = refs[2 + 3 * n:]
        base = 0
        for j in range(n):
            rows, cols = ws[j].shape
            size = rows * cols
            for ch in range(-(-size // LANES)):
                r, c0 = divmod(ch * LANES, cols)
                width = min(LANES, cols - c0)
                packed = (slice(base + ch, base + ch + 1), slice(0, width))
                g = _sum_in_device_order(lambda d: p_ref[(d,) + packed], mine_ref[packed])
                at = (slice(r, r + 1), slice(c0, c0 + width))
                delta, nm, nv = _adam_update(g, w_refs[j][at], m_refs[j][at], v_refs[j][at])
                for out, val in zip((outs[j], outs[n + j], outs[2 * n + j], outs[3 * n + j]), (g, delta, nm, nv)):
                    out[at] = val
            base += -(-size // (8 * LANES)) * 8

    vmem = pl.BlockSpec(memory_space=pltpu.VMEM)
    res = pl.pallas_call(
        body, name=name, in_specs=[vmem] * (2 + 3 * n), out_specs=[vmem] * (4 * n),
        out_shape=[jax.ShapeDtypeStruct(a.shape, F32) for a in ws] * 4)(parts, mine, *ws, *ms, *vs)
    return res[:n], res[n:2 * n], res[2 * n:3 * n], res[3 * n:]


def _rows(a, multiple):
    flat = a.reshape(-1)
    pad = -flat.shape[0] % (multiple * LANES)
    if pad:
        flat = jnp.concatenate([flat, jnp.zeros((pad,), a.dtype)])
    return flat.reshape(-1, LANES)


def _pack(arrs, multiple):
    return jnp.concatenate([_rows(a, multiple) for a in arrs], axis=0)


def _gathered_to_full(g, name, shard_shape):
    g = g.reshape((N_DEV,) + shard_shape)
    if name in COL_SHARDED:
        return jnp.transpose(g, (1, 0, 2)).reshape(shard_shape[0], N_DEV * shard_shape[1])
    return g.reshape(N_DEV * shard_shape[0], shard_shape[1])


def _full_to_per_device(full, name):
    if name in COL_SHARDED:
        r, c = full.shape
        return jnp.transpose(full.reshape(r, N_DEV, c // N_DEV), (1, 0, 2))
    return full.reshape(N_DEV, full.shape[0] // N_DEV, full.shape[1])


def _w2cat(w2, a2, g2):
    n_w, n_a, n_g = LORA
    out = jnp.zeros((256, 3 * WB), w2.dtype)
    out = out.at[0:n_w, 0:WB].set(w2)
    out = out.at[n_w:n_w + n_a, WB:2 * WB].set(a2)
    return out.at[n_w + n_a:n_w + n_a + n_g, 2 * WB:].set(g2)


class _Local:
    def __init__(self, w):
        self.w = w

    def weights(self, group, after=None):
        return self.w

    def started(self):
        return ()

    def send(self, grads, names):
        return ()


class _Overlapped:
    GROUPS = {"ffn1": ("ffn1_w_gate", "ffn1_w_up", "ffn1_w_down"),
              "mixer_in": ("w_in", "rwkv_w2", "rwkv_a2", "rwkv_g2"),
              "late": ("w_out", "ffn2_w_gate", "ffn2_w_up", "ffn2_w_down")}

    def __init__(self, wts):
        x, y, c = (lax.axis_index(n) for n in AXES)
        self.wts, self.me, self.gathers, self.sends = wts, 4 * x + 2 * y + c, {}, []
        self._gather("ffn1", ())

    def _gather(self, group, after):
        names = self.GROUPS[group]
        shards = [(jnp.swapaxes(self.wts[n], 1, 2) if n in SENT_TRANSPOSED else self.wts[n]).astype(MXU) for n in names]
        handle, token = _exchange_start(shards, [False] * len(names), "gather_" + group, after)
        self.gathers[group] = (names, shards, handle, token)
        self.newest = token

    def started(self):
        return (self.newest,)

    def _own_slot(self, land, mine):
        return lax.dynamic_update_slice(land, mine[None], (self.me,) + (0,) * mine.ndim)

    def weights(self, group, after=None):
        names, shards, handle, token = self.gathers[group]
        lands = _exchange_wait(handle, token if after is None else after, "gathered_" + group)
        w = {n: _gathered_to_full(self._own_slot(land, own), n, own.shape[1:])
             for n, own, land in zip(names, shards, lands)}
        order = list(self.GROUPS)
        if group != order[-1]:
            self._gather(order[order.index(group) + 1], (w[names[0]],))
        if group == "ffn1":
            for n in SMALL:
                keep = n in ("hgrn_lb_logits", "rwkv_r_k", "final_norm")
                w[n] = self.wts[n] if keep else self.wts[n].reshape(1, -1)
        return w

    def send(self, grads, names, small=None):
        contrib = [_full_to_per_device(grads[n], n).astype(WIRE) for n in names]
        per_peer = [True] * len(names)
        if small is not None:
            names, contrib, per_peer = names + ("small",), contrib + [small], per_peer + [False]
        handle, token = _exchange_start(contrib, per_peer, "scatter_" + names[0])
        self.sends.append((names, contrib, per_peer, handle))
        self.last_token = token
        return (token,)

    def received(self, which, after):
        names, contrib, per_peer, handle = self.sends[which]
        lands = _exchange_wait(handle, after, "scattered_" + names[0])
        return {n: (land, lax.dynamic_index_in_dim(own, self.me, 0, keepdims=False) if pp else own)
                for n, own, pp, land in zip(names, contrib, per_peer, lands)}


def _local_step(x, target, net):
    n_w, n_a, n_g = LORA
    w = dict(net.weights("ffn1"))
    h1 = _rms_fwd(x, w["ffn1_norm"], "ffn1_norm")
    x1 = _ffn_fwd(x, h1, w["ffn1_w_gate"], w["ffn1_w_up"], w["ffn1_w_down"], "ffn1_fwd", after=net.started())
    w.update(net.weights("mixer_in", x1))
    w_in_pad = jnp.pad(w["w_in"], ((0, N_INP - N_IN), (0, 0)))
    mu_pad = jnp.pad(w["rwkv_shift_mu"], ((0, 0), (0, 1792 - 1696)))
    w2cat = _w2cat(w["rwkv_w2"], w["rwkv_a2"], w["rwkv_g2"])
    r_k = w["rwkv_r_k"].reshape(1, WB)
    rw = (mu_pad, w2cat, w["rwkv_w0"], w["rwkv_a0"], w["rwkv_k_k"], w["rwkv_k_a"])

    h2 = _rms_fwd(x1, w["mix_norm"], "mix_norm")
    p_all = _matmul(h2, w_in_pad, tb=True, after=net.started(), name="in_proj")
    oa, oraw, states = _hgrn_fwd(p_all, w["hgrn_lb_logits"], w["hgrn_out_norm"], "hgrn_fwd")
    r, decay, k2, v, sa, sb, g = _rwkv_prep(p_all, *rw, "rwkv_prep")
    y, s_a = _rwkv_scan_fwd(r, decay, k2, v, sa, sb, "rwkv_scan_fwd")
    sall = _rwkv_states(s_a, decay, k2, v, sb, "rwkv_states")
    post_w = (r_k, w["rwkv_gn_w"], w["rwkv_gn_b"])
    ob = _rwkv_post(y, r, k2, v, g, *post_w, "rwkv_post")
    w.update(net.weights("late", ob))
    o = jnp.concatenate([oa, ob], axis=1)
    x2 = _matmul(o, w["w_out"], res=x1, name="out_proj")
    h3 = _rms_fwd(x2, w["ffn2_norm"], "ffn2_norm")
    x3 = _ffn_fwd(x2, h3, w["ffn2_w_gate"], w["ffn2_w_up"], w["ffn2_w_down"], "ffn2_fwd")
    loss, dx3, d_final = _loss_head(x3, w["final_norm"].reshape(1, D), target, "loss_head")

    grads = {"final_norm": d_final.reshape(D)}

    def ffn_back(prefix, h, dy, x_in, norm):
        wg, wu, wd = (w[prefix + s] for s in ("_w_gate", "_w_up", "_w_down"))
        act, dgate, dup, dout = _ffn_bwd(h, dy, wg, wu, wd, prefix + "_bwd")
        dh = _matmul(dup, wu, res=_matmul(dgate, wg, name=prefix + "_dh_gate"), name=prefix + "_dh")
        sent = ()
        for which, a_op, b_op in (("_w_gate", dgate, h), ("_w_up", dup, h), ("_w_down", act, dout)):
            grads[prefix + which] = _matmul(a_op, b_op, ta=True, out_dtype=WIRE, after=sent, name=prefix + "_d" + which)
            sent = net.send(grads, (prefix + which,))
        dx, grads[prefix + "_norm"] = _rms_bwd(x_in, norm, dh, dy, prefix + "_norm_bwd", after=sent)
        return dx

    dx2 = ffn_back("ffn2", h3, dx3, x2, w["ffn2_norm"])
    grads["w_out"] = _matmul(o, dx2, ta=True, out_dtype=WIRE, name="d_w_out")
    sent = net.send(grads, ("w_out",))
    do = _matmul(dx2, w["w_out"], tb=True, after=sent, name="d_mixed")
    dqa, dfa, dia, dga, grads["hgrn_out_norm"], grads["hgrn_lb_logits"] = _hgrn_bwd(
        p_all, w["hgrn_lb_logits"], w["hgrn_out_norm"], oraw, states, do, "hgrn_bwd")
    dy, dg, dr_b, dk2_b, dv_b, grads["rwkv_gn_w"], grads["rwkv_gn_b"], d_rk = _rwkv_post_bwd(
        do, y, r, k2, v, g, *post_w, "rwkv_post_bwd")
    grads["rwkv_r_k"] = d_rk.reshape(w["rwkv_r_k"].shape)
    d_sa, dv = _rwkv_scan_bwd(dy, r, decay, k2, sa, sb, "rwkv_scan_bwd")
    dr, dw, dk2, dsa, dsb = _rwkv_scan_bwd_values(dy, r, decay, v, sa, s_a, d_sa, sall, "rwkv_scan_bwd_values")
    (dsr, dsk, dsv, dslo, dw2cat, grads["rwkv_w0"], grads["rwkv_a0"], grads["rwkv_k_k"],
     grads["rwkv_k_a"]) = _rwkv_prep_bwd((dr, dw, dk2, dv, dsa, dsb, dg, dr_b, dk2_b, dv_b), p_all, *rw,
                                         "rwkv_prep_bwd")
    grads["rwkv_w2"] = dw2cat[0:n_w, 0:WB]
    grads["rwkv_a2"] = dw2cat[n_w:n_w + n_a, WB:2 * WB]
    grads["rwkv_g2"] = dw2cat[n_w + n_a:n_w + n_a + n_g, 2 * WB:]
    dpr, dpk, dpv, dplo, dmu_r, dmu_k, dmu_v, dmu_lo = _shift_bwd((dsr, dsk, dsv, dslo), p_all, mu_pad, "shift_bwd")
    grads["rwkv_shift_mu"] = jnp.concatenate([dmu_r, dmu_k, dmu_v, dmu_lo], axis=1)[:, :1696]
    dp = jnp.concatenate([dqa, dfa, dia, dga, dpr, dpk, dpv, dplo], axis=1)
    grads["w_in"] = _matmul(dp, h2, ta=True, out_dtype=WIRE, name="d_w_in")[:N_IN]
    sent = net.send(grads, ("w_in", "rwkv_w2", "rwkv_a2", "rwkv_g2"))
    dh2 = _matmul(dp, w_in_pad, after=sent, name="d_h2")
    dx1, grads["mix_norm"] = _rms_bwd(x1, w["mix_norm"], dh2, dx2, "mix_norm_bwd")
    dx0 = ffn_back("ffn1", h1, dx1, x, w["ffn1_norm"])
    return loss[0, 0], dx0, grads


def kernel(x, ffn1_norm, ffn1_w_gate, ffn1_w_up, ffn1_w_down, mix_norm, w_in, hgrn_lb_logits, hgrn_out_norm, rwkv_shift_mu, rwkv_w0, rwkv_w2, rwkv_a0, rwkv_a2, rwkv_g2, rwkv_k_k, rwkv_k_a, rwkv_r_k, rwkv_gn_w, rwkv_gn_b, w_out, ffn2_norm, ffn2_w_gate, ffn2_w_up, ffn2_w_down, final_norm, loss_target, m_ffn1_norm, m_ffn1_w_gate, m_ffn1_w_up, m_ffn1_w_down, m_mix_norm, m_w_in, m_hgrn_lb_logits, m_hgrn_out_norm, m_rwkv_shift_mu, m_rwkv_w0, m_rwkv_w2, m_rwkv_a0, m_rwkv_a2, m_rwkv_g2, m_rwkv_k_k, m_rwkv_k_a, m_rwkv_r_k, m_rwkv_gn_w, m_rwkv_gn_b, m_w_out, m_ffn2_norm, m_ffn2_w_gate, m_ffn2_w_up, m_ffn2_w_down, m_final_norm, v_ffn1_norm, v_ffn1_w_gate, v_ffn1_w_up, v_ffn1_w_down, v_mix_norm, v_w_in, v_hgrn_lb_logits, v_hgrn_out_norm, v_rwkv_shift_mu, v_rwkv_w0, v_rwkv_w2, v_rwkv_a0, v_rwkv_a2, v_rwkv_g2, v_rwkv_k_k, v_rwkv_k_a, v_rwkv_r_k, v_rwkv_gn_w, v_rwkv_gn_b, v_w_out, v_ffn2_norm, v_ffn2_w_gate, v_ffn2_w_up, v_ffn2_w_down, v_final_norm):
    args = dict(locals())
    wts = {n: args[n] for n in WEIGHTS}
    mom = {n: args["m_" + n] for n in WEIGHTS}
    var = {n: args["v_" + n] for n in WEIGHTS}
    net = _Overlapped(wts)
    loss, grad_x, grads = _local_step(x[0], loss_target[0], net)
    loss = lax.psum(loss, AXES)
    after, = net.send(grads, (), _pack([grads[n] for n in SMALL], 8))

    new = {}
    two_d = lambda a: a if a.ndim == 2 else a.reshape(1, -1)
    for which in range(len(net.sends)):
        for n, (part, mine) in net.received(which, after).items():
            if n == "small":
                small = _adamw_small(part, mine, *([two_d(src[k]) for k in SMALL] for src in (wts, mom, var)),
                                     "adamw_small")
                for j, k in enumerate(SMALL):
                    new[k] = [res[j].reshape(wts[k].shape) for res in small]
            else:
                grad = (part, mine)
                if n in SENT_TRANSPOSED:
                    grad = jnp.swapaxes(_sum_parts(part, mine, "grad_" + n), 0, 1)
                new[n] = _adamw(grad, wts[n], mom[n], var[n], "adamw_" + n)
                after = new[n][1]
    return (loss, grad_x[None], *[new[n][0] for n in WEIGHTS], *[new[n][1] for n in WEIGHTS],
            *[new[n][2] for n in WEIGHTS], *[new[n][3] for n in WEIGHTS])
```

```python
import functools
import math

import jax
import jax.numpy as jnp
from jax import lax
from jax.experimental import pallas as pl
from jax.experimental.pallas import tpu as pltpu

F32 = jnp.float32
MXU = jnp.bfloat16
WIRE = jnp.bfloat16
D = 1024
FF = 2816
WA = 512
WB = 512
HD_B = 64
N_IN = 3744
N_INP = 3840
COL_R, COL_K, COL_V = 4, 5, 6
COL_L = 14
LORA = (32, 32, 96)
HG_CHUNK = 64
HG_STEP_CHUNKS = 4
SCAN_CHUNK = 64
SCAN_UNROLL = 8
NORM_EPS = 1e-6
GN_EPS = 64e-5
L2_EPS = 1e-12
DECAY_C = math.exp(-0.5)
N_DEV = 8
LANES = 128
ADAM_BLOCK_BYTES = 4 * 1024 * 1024
MATMUL_BLOCK_BYTES = 40 * 1024 * 1024
VMEM_LIMIT = 56 * 1024 * 1024
ADAM_LR, ADAM_B1, ADAM_B2, ADAM_EPS, ADAM_WD, ADAM_STEP = 0.001, 0.9, 0.999, 1e-08, 0.01, 10
AXES = ("x", "y", "c")

SHARDED = ("ffn1_w_gate", "ffn1_w_up", "ffn1_w_down", "w_in", "rwkv_w2", "rwkv_a2", "rwkv_g2", "w_out",
           "ffn2_w_gate", "ffn2_w_up", "ffn2_w_down")
COL_SHARDED = {"rwkv_w2", "rwkv_a2", "rwkv_g2"}
SENT_TRANSPOSED = {"ffn1_w_gate", "ffn1_w_up", "w_in", "ffn2_w_gate", "ffn2_w_up"}
SMALL = ("ffn1_norm", "mix_norm", "hgrn_lb_logits", "hgrn_out_norm", "rwkv_shift_mu", "rwkv_w0", "rwkv_a0",
         "rwkv_k_k", "rwkv_k_a", "rwkv_r_k", "rwkv_gn_w", "rwkv_gn_b", "ffn2_norm", "final_norm")
WEIGHTS = ("ffn1_norm", "ffn1_w_gate", "ffn1_w_up", "ffn1_w_down", "mix_norm", "w_in", "hgrn_lb_logits",
           "hgrn_out_norm", "rwkv_shift_mu", "rwkv_w0", "rwkv_w2", "rwkv_a0", "rwkv_a2", "rwkv_g2", "rwkv_k_k",
           "rwkv_k_a", "rwkv_r_k", "rwkv_gn_w", "rwkv_gn_b", "w_out", "ffn2_norm", "ffn2_w_gate", "ffn2_w_up",
           "ffn2_w_down", "final_norm")


def _tile(n, cap):
    if n <= cap:
        return n
    for t in range(cap - cap % LANES, 0, -LANES):
        if n % t == 0:
            return t
    raise ValueError((n, cap))


def _params(n_axes):
    return pltpu.CompilerParams(dimension_semantics=("arbitrary",) * n_axes, vmem_limit_bytes=VMEM_LIMIT)


def _sig(x):
    return jax.nn.sigmoid(x)


def _dsilu(z, s):
    return s * (1.0 + z * (1.0 - s))


def _dot(a, b, dims=((1,), (0,)), precision=None):
    return lax.dot_general(a, b, (dims, ((), ())), preferred_element_type=F32, precision=precision)


_NT = ((1,), (1,))
_TN = ((0,), (0,))
_HI = lax.Precision.HIGH


def _iota(shape, dim):
    return lax.broadcasted_iota(jnp.int32, shape, dim)


def _split_dot(x, ones, passes):
    hi = x.astype(jnp.bfloat16)
    acc = _dot(hi, ones)
    rem = x
    for _ in range(passes - 1):
        rem = rem - hi.astype(F32)
        hi = rem.astype(jnp.bfloat16)
        acc = acc + _dot(hi, ones)
    return acc


def _head_ones(n, width):
    shift = width.bit_length() - 1
    return (_iota((n, n), 0) >> shift == _iota((n, n), 1) >> shift).astype(jnp.bfloat16)


def _matmul(a, b, *, ta=False, tb=False, out_dtype=F32, res=None, after=(), name):
    m, k = (a.shape[1], a.shape[0]) if ta else a.shape
    n = b.shape[0] if tb else b.shape[1]
    tm, tn = _tile(m, 1408), _tile(n, 1408)
    in_bytes = max(a.dtype.itemsize, b.dtype.itemsize)
    for tk in (_tile(k, 1408), _tile(k, 1024), _tile(k, 512), _tile(k, 256)):
        if 2 * (tm + tn) * tk * in_bytes + 3 * tm * tn * 4 <= MATMUL_BLOCK_BYTES:
            break
    nk = k // tk
    dims = ((0 if ta else 1,), (1 if tb else 0,))

    def body(*refs):
        a_ref, b_ref = refs[:2]
        o_ref, acc = refs[-2:]
        kk = pl.program_id(2)

        @pl.when(kk == 0)
        def _():
            acc[...] = jnp.zeros_like(acc)

        acc[...] += _dot(a_ref[...].astype(MXU), b_ref[...].astype(MXU), dims)

        @pl.when(kk == nk - 1)
        def _():
            v = acc[...]
            if res is not None:
                v = v + refs[2][...]
            o_ref[...] = v.astype(out_dtype)

    a_spec = pl.BlockSpec((tk, tm), lambda i, j, kk: (kk, i)) if ta else pl.BlockSpec((tm, tk), lambda i, j, kk: (i, kk))
    b_spec = pl.BlockSpec((tn, tk), lambda i, j, kk: (j, kk)) if tb else pl.BlockSpec((tk, tn), lambda i, j, kk: (kk, j))
    o_spec = pl.BlockSpec((tm, tn), lambda i, j, kk: (i, j))
    ins, specs = [a, b], [a_spec, b_spec]
    if res is not None:
        ins.append(res)
        specs.append(o_spec)
    ins += list(after)
    specs += [pl.BlockSpec(memory_space=pl.ANY)] * len(after)
    return pl.pallas_call(
        body, name=name, grid=(m // tm, n // tn, nk), in_specs=specs, out_specs=o_spec,
        out_shape=jax.ShapeDtypeStruct((m, n), out_dtype), scratch_shapes=[pltpu.VMEM((tm, tn), F32)],
        compiler_params=_params(3))(*ins)


def _rms_fwd(x, g, name, after=()):
    t = x.shape[0]
    tb = _tile(t, 512)

    def body(x_ref, g_ref, *rest):
        xv = x_ref[...]
        rinv = lax.rsqrt(jnp.mean(xv * xv, axis=-1, keepdims=True) + NORM_EPS)
        rest[-1][...] = (xv * rinv * g_ref[...]).astype(MXU)

    return pl.pallas_call(
        body, name=name, grid=(t // tb,),
        in_specs=[pl.BlockSpec((tb, D), lambda i: (i, 0)), pl.BlockSpec((1, D), lambda i: (0, 0))]
        + [pl.BlockSpec(memory_space=pl.ANY)] * len(after),
        out_specs=pl.BlockSpec((tb, D), lambda i: (i, 0)), out_shape=jax.ShapeDtypeStruct((t, D), MXU),
        compiler_params=_params(1))(x, g, *after)


def _rms_bwd(x, g, dh, dres, name, after=()):
    t = x.shape[0]
    tb = _tile(t, 512)

    def body(x_ref, g_ref, dh_ref, dres_ref, *rest):
        dx_ref, dg_ref = rest[-2:]

        @pl.when(pl.program_id(0) == 0)
        def _():
            dg_ref[...] = jnp.zeros_like(dg_ref)

        xv = x_ref[...]
        rinv = lax.rsqrt(jnp.mean(xv * xv, axis=-1, keepdims=True) + NORM_EPS)
        xhat = xv * rinv
        dhv = dh_ref[...]
        dg_ref[...] += jnp.sum(dhv * xhat, axis=0, keepdims=True)
        dxhat = dhv * g_ref[...]
        dx_ref[...] = dres_ref[...] + rinv * (dxhat - xhat * jnp.mean(dxhat * xhat, axis=-1, keepdims=True))

    row = pl.BlockSpec((tb, D), lambda i: (i, 0))
    vec = pl.BlockSpec((1, D), lambda i: (0, 0))
    return pl.pallas_call(
        body, name=name, grid=(t // tb,),
        in_specs=[row, vec, row, row] + [pl.BlockSpec(memory_space=pl.ANY)] * len(after), out_specs=[row, vec],
        out_shape=[jax.ShapeDtypeStruct((t, D), F32), jax.ShapeDtypeStruct((1, D), F32)],
        compiler_params=_params(1))(x, g, dh, dres, *after)


def _loss_head(x, g, target, name):
    t = x.shape[0]
    tb = _tile(t, 512)

    def body(x_ref, g_ref, t_ref, loss_ref, dx_ref, dg_ref):
        @pl.when(pl.program_id(0) == 0)
        def _():
            dg_ref[...] = jnp.zeros_like(dg_ref)
            loss_ref[...] = jnp.zeros_like(loss_ref)

        xv = x_ref[...]
        gv = g_ref[...]
        rinv = lax.rsqrt(jnp.mean(xv * xv, axis=-1, keepdims=True) + NORM_EPS)
        xhat = xv * rinv
        err = xhat * gv - t_ref[...]
        per_tok = jnp.mean(err * err, axis=-1, keepdims=True)
        loss_ref[...] += jnp.broadcast_to(0.5 * jnp.sum(per_tok, axis=0, keepdims=True), loss_ref.shape)
        dy = err * (1.0 / D)
        dg_ref[...] += jnp.sum(dy * xhat, axis=0, keepdims=True)
        dxhat = dy * gv
        dx_ref[...] = rinv * (dxhat - xhat * jnp.mean(dxhat * xhat, axis=-1, keepdims=True))

    row = pl.BlockSpec((tb, D), lambda i: (i, 0))
    vec = pl.BlockSpec((1, D), lambda i: (0, 0))
    return pl.pallas_call(
        body, name=name, grid=(t // tb,), in_specs=[row, vec, row],
        out_specs=[pl.BlockSpec((1, LANES), lambda i: (0, 0)), row, vec],
        out_shape=[jax.ShapeDtypeStruct((1, LANES), F32), jax.ShapeDtypeStruct((t, D), F32),
                   jax.ShapeDtypeStruct((1, D), F32)],
        compiler_params=_params(1))(x, g, target)


def _ffn_fwd(x, h, wg, wu, wd, name, after=()):
    t = x.shape[0]
    tb, fb = _tile(t, 1024), 256
    nf = FF // fb

    def body(x_ref, h_ref, wg_ref, wu_ref, wd_ref, *rest):
        o_ref, acc = rest[-2:]
        f = pl.program_id(1)

        @pl.when(f == 0)
        def _():
            acc[...] = jnp.zeros_like(acc)

        hv = h_ref[...]
        gate = _dot(hv, wg_ref[...], _NT)
        up = _dot(hv, wu_ref[...], _NT)
        act = (gate * _sig(gate) * up).astype(MXU)
        acc[...] += _dot(act, wd_ref[...])

        @pl.when(f == nf - 1)
        def _():
            o_ref[...] = x_ref[...] + 0.5 * acc[...]

    row = pl.BlockSpec((tb, D), lambda i, f: (i, 0))
    hidden = pl.BlockSpec((fb, D), lambda i, f: (f, 0))
    return pl.pallas_call(
        body, name=name, grid=(t // tb, nf),
        in_specs=[row, row, hidden, hidden, hidden]
        + [pl.BlockSpec(memory_space=pl.ANY)] * len(after), out_specs=row,
        out_shape=jax.ShapeDtypeStruct((t, D), F32), scratch_shapes=[pltpu.VMEM((tb, D), F32)],
        compiler_params=_params(2))(x, h, wg, wu, wd, *after)


def _ffn_bwd(h, dy, wg, wu, wd, name):
    t = h.shape[0]
    tb, fb = _tile(t, 1024), 256
    nf = FF // fb

    def body(h_ref, dy_ref, wg_ref, wu_ref, wd_ref, act_ref, dg_ref, du_ref, dout_ref):
        hv = h_ref[...]
        dout = (0.5 * dy_ref[...]).astype(MXU)
        dout_ref[...] = dout
        gate = _dot(hv, wg_ref[...], _NT)
        up = _dot(hv, wu_ref[...], _NT)
        dact = _dot(dout, wd_ref[...], _NT)
        s = _sig(gate)
        silu = gate * s
        act_ref[...] = (silu * up).astype(MXU)
        du_ref[...] = (dact * silu).astype(MXU)
        dg_ref[...] = (dact * up * _dsilu(gate, s)).astype(MXU)

    row = pl.BlockSpec((tb, D), lambda i, f: (i, 0))
    hidden = pl.BlockSpec((fb, D), lambda i, f: (f, 0))
    hid = pl.BlockSpec((tb, fb), lambda i, f: (i, f))
    hid_shape = jax.ShapeDtypeStruct((t, FF), MXU)
    return pl.pallas_call(
        body, name=name, grid=(t // tb, nf),
        in_specs=[row, row, hidden, hidden, hidden],
        out_specs=[hid, hid, hid, row],
        out_shape=[hid_shape, hid_shape, hid_shape, jax.ShapeDtypeStruct((t, D), MXU)],
        compiler_params=_params(2))(h, dy, wg, wu, wd)


def _hgrn_chunk(qa, fa, lbl):
    c = HG_CHUNK
    lb = _sig(lbl[0:1, :] - lbl[1:2, :])
    sf = _sig(fa)
    forget = lb + (1.0 - lb) * sf
    kh = 1.0 - forget
    row, col = _iota((c, c), 0), _iota((c, c), 1)
    b = _dot((col <= row).astype(F32), jnp.log(forget), precision=_HI)
    bref, blast = b[c // 2:c // 2 + 1, :], b[c - 1:c, :]
    sq = _sig(qa)
    q = qa * sq
    qt, kt = q * jnp.exp(b - bref), kh * jnp.exp(bref - b)
    qb, kl = q * jnp.exp(b), kh * jnp.exp(blast - b)
    causal = col <= row
    return dict(lb=lb, sf=sf, forget=forget, sq=sq, qt=qt, kt=kt, qb=qb, kl=kl, decay=jnp.exp(blast),
                causal=causal, e_q=jnp.exp(b), e_qt=jnp.exp(b - bref), e_kt=jnp.exp(bref - b),
                e_kl=jnp.exp(blast - b))


def _hgrn_specs(t):
    c = HG_CHUNK
    n = t // c
    return c, n, WA // LANES, HG_STEP_CHUNKS if n % HG_STEP_CHUNKS == 0 else 1


def _hgrn_fwd(p_all, lbl, onorm, name):
    t = p_all.shape[0]
    c, n, nh, m = _hgrn_specs(t)

    def body(q_ref, f_ref, i_ref, g_ref, lbl_ref, on_ref, oa_ref, oraw_ref, st_ref, state):
        @pl.when(pl.program_id(0) == 0)
        def _():
            state[...] = jnp.zeros_like(state)

        heads = [slice(h * LANES, (h + 1) * LANES) for h in range(nh)]
        sts = [state[h] for h in range(nh)]
        for sub in range(m):
            rows = slice(sub * c, (sub + 1) * c)
            ks = [_hgrn_chunk(q_ref[rows, at], f_ref[rows, at], lbl_ref[:, at]) for at in heads]
            vs = [i_ref[rows, at] for at in heads]
            for h in range(nh):
                st_ref[h, sub] = sts[h]
            scores = [jnp.where(k["causal"], _dot(k["qt"], k["kt"], _NT, _HI), 0.0) for k in ks]
            outs = [_dot(a, v, precision=_HI) + _dot(k["qb"], st, _NT, _HI) for a, v, k, st in zip(scores, vs, ks, sts)]
            sts = [st * k["decay"] + _dot(v, k["kl"], _TN, _HI) for st, k, v in zip(sts, ks, vs)]
            for at, o in zip(heads, outs):
                oraw_ref[rows, at] = o
                rinv = lax.rsqrt(jnp.mean(o * o, axis=-1, keepdims=True) + NORM_EPS)
                ga = g_ref[rows, at]
                oa_ref[rows, at] = (o * rinv * on_ref[:, at] * (ga * _sig(ga))).astype(MXU)
        for h in range(nh):
            state[h] = sts[h]

    def blk(j):
        return pl.BlockSpec((m * c, WA), lambda i: (i, j))

    return pl.pallas_call(
        body, name=name, grid=(n // m,),
        in_specs=[blk(0), blk(1), blk(2), blk(3), pl.BlockSpec((2, WA), lambda i: (0, 0)),
                  pl.BlockSpec((1, WA), lambda i: (0, 0))],
        out_specs=[blk(0), blk(0), pl.BlockSpec((nh, m, LANES, LANES), lambda i: (0, i, 0, 0))],
        out_shape=[jax.ShapeDtypeStruct((t, WA), MXU), jax.ShapeDtypeStruct((t, WA), F32),
                   jax.ShapeDtypeStruct((nh, n, LANES, LANES), F32)],
        scratch_shapes=[pltpu.VMEM((nh, LANES, LANES), F32)], compiler_params=_params(1))(
            p_all, p_all, p_all, p_all, lbl, onorm)


def _hgrn_bwd(p_all, lbl, onorm, oraw, states, doa, name):
    t = p_all.shape[0]
    c, n, nh, m = _hgrn_specs(t)

    def body(q_ref, f_ref, i_ref, g_ref, lbl_ref, on_ref, oraw_ref, st_ref, doa_ref,
             dq_ref, df_ref, di_ref, dg_ref, don_ref, dlbl_ref, dstate, dlb):
        @pl.when(pl.program_id(0) == 0)
        def _():
            dstate[...] = jnp.zeros_like(dstate)
            dlb[...] = jnp.zeros_like(dlb)
            don_ref[...] = jnp.zeros_like(don_ref)

        heads = [slice(h * LANES, (h + 1) * LANES) for h in range(nh)]
        dsts = [dstate[h] for h in range(nh)]
        step = _iota((c, LANES), 0)
        row, col = _iota((c, c), 0), _iota((c, c), 1)
        for sub in reversed(range(m)):
            rows = slice(sub * c, (sub + 1) * c)
            work = []
            for h, at in enumerate(heads):
                qa, fa, v, ga = q_ref[rows, at], f_ref[rows, at], i_ref[rows, at], g_ref[rows, at]
                k = _hgrn_chunk(qa, fa, lbl_ref[:, at])
                o = oraw_ref[rows, at]
                gain = on_ref[:, at]
                rinv = lax.rsqrt(jnp.mean(o * o, axis=-1, keepdims=True) + NORM_EPS)
                on = o * rinv
                sg = _sig(ga)
                gate = ga * sg
                dout = doa_ref[rows, at]
                don_ref[:, at] += jnp.sum(dout * on * gate, axis=0, keepdims=True)
                dg_ref[rows, at] = (dout * on * gain * _dsilu(ga, sg)).astype(MXU)
                d_on = dout * gain * gate
                do = rinv * (d_on - on * jnp.mean(d_on * on, axis=-1, keepdims=True))
                work.append(dict(at=at, qa=qa, v=v, k=k, do=do, st=st_ref[h, sub]))
            for x, dst_next in zip(work, dsts):
                k, do = x["k"], x["do"]
                x["a"] = jnp.where(k["causal"], _dot(k["qt"], k["kt"], _NT, _HI), 0.0)
                x["dqb"] = _dot(do, x["st"], precision=_HI)
                x["dst"] = dst_next * k["decay"] + _dot(do, k["qb"], _TN, _HI)
                x["da"] = jnp.where(k["causal"], _dot(do, x["v"], _NT, _HI), 0.0)
            for x, dst_next in zip(work, dsts):
                k = x["k"]
                x["dqt"] = _dot(x["da"], k["kt"], precision=_HI)
                x["dkt"] = _dot(x["da"], k["qt"], _TN, _HI)
                x["dv"] = _dot(x["a"], x["do"], _TN, _HI) + _dot(k["kl"], dst_next, _NT, _HI)
                x["dkl"] = _dot(x["v"], dst_next, precision=_HI)
            for x, dst_next in zip(work, dsts):
                k, at, dqt, dkt, dkl, dqb = x["k"], x["at"], x["dqt"], x["dkt"], x["dkl"], x["dqb"]
                ddecay = jnp.sum(dst_next * x["st"], axis=0, keepdims=True)
                dq = dqb * k["e_q"] + dqt * k["e_qt"]
                dk = dkt * k["e_kt"] + dkl * k["e_kl"]
                tq, tk, tl = dqt * k["qt"], dkt * k["kt"], dkl * k["kl"]
                db = dqb * k["qb"] + tq - tk - tl
                dbref = jnp.sum(tk - tq, axis=0, keepdims=True)
                dblast = jnp.sum(tl, axis=0, keepdims=True) + ddecay * k["decay"]
                db = db + jnp.where(step == c // 2, dbref, 0.0) + jnp.where(step == c - 1, dblast, 0.0)
                dlogf = _dot((col >= row).astype(F32), db, precision=_HI)
                dq_ref[rows, at] = (dq * _dsilu(x["qa"], k["sq"])).astype(MXU)
                di_ref[rows, at] = x["dv"].astype(MXU)
                dforget = dlogf / k["forget"] - dk
                sf, lb = k["sf"], k["lb"]
                df_ref[rows, at] = (dforget * (1.0 - lb) * sf * (1.0 - sf)).astype(MXU)
                dlb[:, at] += jnp.sum(dforget * (1.0 - sf), axis=0, keepdims=True)
                dl0 = dlb[:, at] * lb * (1.0 - lb)
                dlbl_ref[:, at] = jnp.where(_iota((2, LANES), 0) == 0, dl0, -dl0)
            dsts = [x["dst"] for x in work]
        for h in range(nh):
            dstate[h] = dsts[h]

    last = n // m - 1

    def blk(j):
        return pl.BlockSpec((m * c, WA), lambda i: (last - i, j))

    vec = pl.BlockSpec((1, WA), lambda i: (0, 0))
    lg = pl.BlockSpec((2, WA), lambda i: (0, 0))
    grad = jax.ShapeDtypeStruct((t, WA), MXU)
    return pl.pallas_call(
        body, name=name, grid=(n // m,),
        in_specs=[blk(0), blk(1), blk(2), blk(3), lg, vec, blk(0),
                  pl.BlockSpec((nh, m, LANES, LANES), lambda i: (0, last - i, 0, 0)), blk(0)],
        out_specs=[blk(0), blk(0), blk(0), blk(0), vec, lg],
        out_shape=[grad, grad, grad, grad, jax.ShapeDtypeStruct((1, WA), F32), jax.ShapeDtypeStruct((2, WA), F32)],
        scratch_shapes=[pltpu.VMEM((nh, LANES, LANES), F32), pltpu.VMEM((1, WA), F32)],
        compiler_params=_params(1))(p_all, p_all, p_all, p_all, lbl, onorm, oraw, states, doa)


def _lora_act(x):
    lane = _iota(x.shape, 1)
    n_w, n_a, n_g = LORA
    return jnp.where(lane < n_w, jnp.tanh(x),
                     jnp.where(lane < n_w + n_a, x, jnp.where(lane < n_w + n_a + n_g, _sig(x), 0.0)))


def _lora_dact(x):
    lane = _iota(x.shape, 1)
    n_w, n_a, n_g = LORA
    th, s = jnp.tanh(x), _sig(x)
    return jnp.where(lane < n_w, 1.0 - th * th,
                     jnp.where(lane < n_w + n_a, 1.0, jnp.where(lane < n_w + n_a + n_g, s * (1.0 - s), 0.0)))


def _shift_down(cur, prev8, first):
    rolled = pltpu.roll(cur, 1, 0)
    edge = prev8[7:8, :] * jnp.where(first, 0.0, 1.0)
    return jnp.where(_iota(cur.shape, 0) == 0, edge, rolled)


def _shift_up(cur, next8, last):
    rows = cur.shape[0]
    rolled = pltpu.roll(cur, rows - 1, 0)
    edge = next8[0:1, :] * jnp.where(last, 0.0, 1.0)
    return jnp.where(_iota(cur.shape, 0) == rows - 1, edge, rolled)


def _rwkv_inputs(refs, first, ones):
    (pr, pk, pv, plo, qr, qk, qv, qlo, mr, mk, mv, mlo, w2c, w0, a0, kk_w, ka_w) = refs
    mix = lambda cur, prev, mu: cur[...] + mu[...] * (_shift_down(cur[...], prev[...], first) - cur[...])
    r, k, v, lo = mix(pr, qr, mr), mix(pk, qk, mk), mix(pv, qv, mv), mix(plo, qlo, mlo)
    z = _lora_act(lo)
    lin = _dot(z.astype(MXU), w2c[...])
    sg = _sig(w0[...] + lin[:, :WB])
    decay = jnp.exp(-DECAY_C * sg)
    a = _sig(a0[...] + lin[:, WB:2 * WB])
    g = lin[:, 2 * WB:]
    kk0 = k * kk_w[...]
    nrm = jnp.sqrt(_split_dot(kk0 * kk0, ones, 3))
    den = jnp.maximum(nrm, L2_EPS)
    kk = kk0 / den
    k2 = k * (1.0 + (a - 1.0) * ka_w[...])
    return dict(r=r, k=k, v=v, lo=lo, z=z, sg=sg, decay=decay, a=a, g=g, kk=kk, den=den, nrm=nrm, k2=k2)


def _rwkv_in_specs(t, tb):
    nt8 = tb // 8

    def cur(w, j):
        return pl.BlockSpec((tb, w), lambda i: (i, j))

    def prev(w, j):
        return pl.BlockSpec((8, w), lambda i: (jnp.maximum(i * nt8 - 1, 0), j))

    def vec(w, j=0):
        return pl.BlockSpec((1, w), lambda i: (0, j))

    return [cur(WB, COL_R), cur(WB, COL_K), cur(WB, COL_V), cur(256, COL_L),
            prev(WB, COL_R), prev(WB, COL_K), prev(WB, COL_V), prev(256, COL_L),
            vec(WB, 0), vec(WB, 1), vec(WB, 2), vec(256, 6),
            pl.BlockSpec((256, 3 * WB), lambda i: (0, 0)), vec(WB), vec(WB), vec(WB), vec(WB)]


def _rwkv_in_args(p_all, mu_pad, w2cat, w0, a0, k_k, k_a):
    return (p_all,) * 8 + (mu_pad,) * 4 + (w2cat, w0, a0, k_k, k_a)


def _rwkv_prep(p_all, mu_pad, w2cat, w0, a0, k_k, k_a, name):
    t = p_all.shape[0]
    tb = _tile(t, 256)

    def body(*refs):
        ins, outs = refs[:17], refs[17:]
        q = _rwkv_inputs(ins, pl.program_id(0) == 0, _head_ones(WB, HD_B))
        for ref, val in zip(outs, (q["r"], q["decay"], q["k2"], q["v"], -q["kk"], q["kk"] * q["a"], q["g"])):
            ref[...] = val

    out = pl.BlockSpec((tb, WB), lambda i: (i, 0))
    return pl.pallas_call(
        body, name=name, grid=(t // tb,), in_specs=_rwkv_in_specs(t, tb), out_specs=[out] * 7,
        out_shape=[jax.ShapeDtypeStruct((t, WB), F32)] * 7, compiler_params=_params(1))(
            *_rwkv_in_args(p_all, mu_pad, w2cat, w0, a0, k_k, k_a))


def _pair_rows(x8, i):
    return jnp.concatenate([jnp.broadcast_to(x8[i:i + 1, p * LANES:(p + 1) * LANES], (HD_B, LANES))
                            for p in range(4)], axis=0)


def _pair_sums(x):
    return jnp.concatenate([jnp.sum(x[p * HD_B:(p + 1) * HD_B], axis=0, keepdims=True) for p in range(4)], axis=1)


def _put_row(buf, i, row):
    return jnp.where(_iota(buf.shape, 0) == i, row, buf)


def _pieces(x):
    hi = x.astype(jnp.bfloat16).astype(F32)
    lo = (x - hi).astype(jnp.bfloat16).astype(F32)
    upper = (_iota((x.shape[0], LANES), 1) & (HD_B // 2)) != 0
    swapped = [jnp.where(upper, pltpu.roll(lo[:, p * LANES:(p + 1) * LANES], HD_B // 2, 1),
                         pltpu.roll(lo[:, p * LANES:(p + 1) * LANES], LANES - HD_B // 2, 1)) for p in range(4)]
    return hi, jnp.concatenate(swapped, axis=1)


def _scan_consts():
    row, lane = _iota((HD_B, LANES), 0), _iota((HD_B, LANES), 1) & (HD_B - 1)
    either = ((row ^ lane) & (HD_B // 2 - 1)) == 0
    return ((row ^ lane) & (HD_B // 2)) != 0, either.astype(jnp.bfloat16), _head_ones(LANES, HD_B)


def _pair_cols(many, consts):
    swapped, either, ones = consts
    tiles = []
    for (hi8, lo8), i in many:
        for p in range(4):
            lanes = slice(p * LANES, (p + 1) * LANES)
            hi = jnp.broadcast_to(hi8[i:i + 1, lanes], (16, LANES)).astype(jnp.bfloat16)
            lo = jnp.broadcast_to(lo8[i:i + 1, lanes], (16, LANES)).astype(jnp.bfloat16)
            for g in range(HD_B // 16):
                rows = slice(g * 16, (g + 1) * 16)
                tiles.append(jnp.where(swapped[rows], lo, hi) * either[rows])
    out = _dot(jnp.concatenate(tiles, axis=0), ones)
    return [out[m * 4 * HD_B:(m + 1) * 4 * HD_B] for m in range(len(many))]


def _block_products(w8):
    rows = _iota(w8.shape, 0)
    down, up = w8, w8
    for shift in (1, 2, 4):
        down = down * jnp.where(rows >= shift, pltpu.roll(down, shift, 0), 1.0)
        up = up * jnp.where(rows < 8 - shift, pltpu.roll(up, 8 - shift, 0), 1.0)
    return down, up


def _blocked_loop(n_blocks, prepare, advance, init):
    unroll = SCAN_UNROLL if n_blocks % SCAN_UNROLL == 0 else 1

    def trip(g, carry):
        prepared = [prepare(g * unroll + i) for i in range(unroll)]
        for p in prepared:
            carry = advance(p, carry)
        return carry

    return lax.fori_loop(0, n_blocks // unroll, trip, init)


def _rwkv_scan_fwd(r, w, k, v, a, b, name):
    t = r.shape[0]
    cc = min(t, SCAN_CHUNK)

    def body(r_ref, w_ref, k_ref, v_ref, a_ref, b_ref, y_ref, sa_ref, state):
        @pl.when(pl.program_id(0) == 0)
        def _():
            state[...] = jnp.zeros_like(state)

        consts = _scan_consts()

        def prepare(j):
            rows = pl.ds(pl.multiple_of(j * 8, 8), 8)
            r8, w8, k8, v8, a8, b8 = (ref[rows, :] for ref in (r_ref, w_ref, k_ref, v_ref, a_ref, b_ref))
            decay, _ = _block_products(w8)
            before = jnp.where(_iota(w8.shape, 0) == 0, 1.0, pltpu.roll(decay, 1, 0))
            inv = 1.0 / decay
            scaled = [_pieces(x) for x in (a8 * before, b8 * inv, k8 * inv, r8 * decay)]
            return rows, v8, _pair_cols([(x, i) for i in range(8) for x in scaled] + [(_pieces(decay), 7)], consts)

        def advance(prepared, sk):
            rows, v8, cols = prepared
            y8 = jnp.zeros((8, WB), F32)
            sa8 = jnp.zeros((8, WB), F32)
            for i in range(8):
                a_c, b_c, k_c, r_c = cols[4 * i:4 * i + 4]
                sa = _pair_sums(sk * a_c)
                sk = sk + b_c * _pair_rows(sa, 0) + k_c * _pair_rows(v8, i)
                y8 = _put_row(y8, i, _pair_sums(sk * r_c))
                sa8 = _put_row(sa8, i, sa)
            y_ref[rows, :] = y8
            sa_ref[rows, :] = sa8
            return sk * cols[-1]

        state[...] = _blocked_loop(cc // 8, prepare, advance, state[...])

    row = pl.BlockSpec((cc, WB), lambda i: (i, 0))
    return pl.pallas_call(
        body, name=name, grid=(t // cc,), in_specs=[row] * 6, out_specs=[row, row],
        out_shape=[jax.ShapeDtypeStruct((t, WB), F32)] * 2,
        scratch_shapes=[pltpu.VMEM((4 * HD_B, LANES), F32)], compiler_params=_params(1))(r, w, k, v, a, b)


def _rwkv_states(sa, w, k, v, b, name):
    t = sa.shape[0]
    cc = min(t, SCAN_CHUNK)

    def body(sa_ref, w_ref, k_ref, v_ref, b_ref, sall_ref, state):
        @pl.when(pl.program_id(0) == 0)
        def _():
            state[...] = jnp.zeros_like(state)

        consts = _scan_consts()

        def prepare(j):
            base = pl.multiple_of(j * 8, 8)
            sa8, w8, k8, v8, b8 = (ref[pl.ds(base, 8), :] for ref in (sa_ref, w_ref, k_ref, v_ref, b_ref))
            sap, vp = _pieces(sa8), _pieces(v8)
            return base, w8, k8, b8, _pair_cols([(x, i) for i in range(8) for x in (sap, vp)], consts)

        def advance(prepared, sv):
            base, w8, k8, b8, cols = prepared
            for i in range(8):
                sv = sv * _pair_rows(w8, i) + cols[2 * i] * _pair_rows(b8, i) + cols[2 * i + 1] * _pair_rows(k8, i)
                sall_ref[base + i] = sv
            return sv

        state[...] = _blocked_loop(cc // 8, prepare, advance, state[...])

    row = pl.BlockSpec((cc, WB), lambda i: (i, 0))
    return pl.pallas_call(
        body, name=name, grid=(t // cc,), in_specs=[row] * 5,
        out_specs=pl.BlockSpec((cc, 4 * HD_B, LANES), lambda i: (i, 0, 0)),
        out_shape=jax.ShapeDtypeStruct((t, 4 * HD_B, LANES), F32),
        scratch_shapes=[pltpu.VMEM((4 * HD_B, LANES), F32)], compiler_params=_params(1))(sa, w, k, v, b)


def _rwkv_scan_bwd(dy, r, w, k, a, b, name):
    t = r.shape[0]
    cc = min(t, SCAN_CHUNK)
    n = t // cc

    def body(dy_ref, r_ref, w_ref, k_ref, a_ref, b_ref, dsa_ref, dv_ref, dstate):
        @pl.when(pl.program_id(0) == 0)
        def _():
            dstate[...] = jnp.zeros_like(dstate)

        consts = _scan_consts()

        steps = range(7, -1, -1)

        def prepare(jj):
            rows = pl.ds(pl.multiple_of((cc // 8 - 1 - jj) * 8, 8), 8)
            dy8, r8, w8, k8, a8, b8 = (ref[rows, :] for ref in (dy_ref, r_ref, w_ref, k_ref, a_ref, b_ref))
            _, upto = _block_products(w8)
            later = jnp.where(_iota(w8.shape, 0) == 7, 1.0, pltpu.roll(upto, 7, 0))
            scaled = [_pieces(x) for x in (r8 / later, b8 * later, k8 * later, a8 / upto)]
            return rows, dy8, _pair_cols([(x, i) for i in steps for x in scaled] + [(_pieces(upto), 0)], consts)

        def advance(prepared, ds):
            rows, dy8, cols = prepared
            dsa8, dv8 = jnp.zeros((8, WB), F32), jnp.zeros((8, WB), F32)
            for n_done, i in enumerate(steps):
                r_c, b_c, k_c, a_c = cols[4 * n_done:4 * n_done + 4]
                ds = ds + r_c * _pair_rows(dy8, i)
                dsa = _pair_sums(ds * b_c)
                dv8 = _put_row(dv8, i, _pair_sums(ds * k_c))
                dsa8 = _put_row(dsa8, i, dsa)
                ds = ds + a_c * _pair_rows(dsa, 0)
            dsa_ref[rows, :] = dsa8
            dv_ref[rows, :] = dv8
            return ds * cols[-1]

        dstate[...] = _blocked_loop(cc // 8, prepare, advance, dstate[...])

    row = pl.BlockSpec((cc, WB), lambda i: (n - 1 - i, 0))
    return pl.pallas_call(
        body, name=name, grid=(n,), in_specs=[row] * 6, out_specs=[row] * 2,
        out_shape=[jax.ShapeDtypeStruct((t, WB), F32)] * 2,
        scratch_shapes=[pltpu.VMEM((4 * HD_B, LANES), F32)], compiler_params=_params(1))(dy, r, w, k, a, b)


def _rwkv_scan_bwd_values(dy, r, w, v, a, sa, dsa, sall, name):
    t = r.shape[0]
    cc = min(t, SCAN_CHUNK)
    n = t // cc

    def body(dy_ref, r_ref, w_ref, v_ref, a_ref, sa_ref, dsa_ref, sall_ref, sprev_ref,
             dr_ref, dw_ref, dk_ref, da_ref, db_ref, dstate):
        @pl.when(pl.program_id(0) == 0)
        def _():
            dstate[...] = jnp.zeros_like(dstate)

        consts = _scan_consts()
        before_chunk = jnp.where(pl.program_id(0) == n - 1, 0.0, 1.0) * sprev_ref[0]

        steps = range(7, -1, -1)

        def prepare(jj):
            j = cc // 8 - 1 - jj
            base = pl.multiple_of(j * 8, 8)
            dy8, r8, w8, v8, a8, sa8, dsa8 = (ref[pl.ds(base, 8), :] for ref in
                                              (dy_ref, r_ref, w_ref, v_ref, a_ref, sa_ref, dsa_ref))
            dyp, vp, sap, dsap = (_pieces(x) for x in (dy8, v8, sa8, dsa8))
            return j, base, r8, w8, a8, _pair_cols([(x, i) for i in steps for x in (dyp, vp, sap, dsap)], consts)

        def advance(prepared, carry):
            ds, sc = carry
            j, base, r8, w8, a8, cols = prepared
            rows = pl.ds(base, 8)
            outs = [jnp.zeros((8, WB), F32) for _ in range(5)]
            for n_done, i in enumerate(steps):
                if i > 0:
                    sp = sall_ref[base + i - 1]
                else:
                    sp = jnp.where(j == 0, before_chunk, sall_ref[jnp.maximum(base - 1, 0)])
                dy_c, v_c, sa_c, dsa_c = cols[4 * n_done:4 * n_done + 4]
                ds = ds + dy_c * _pair_rows(r8, i)
                vals = (_pair_sums(sc * dy_c), _pair_sums(ds * sp), _pair_sums(ds * v_c),
                        _pair_sums(sp * dsa_c), _pair_sums(ds * sa_c))
                outs = [_put_row(o, i, val) for o, val in zip(outs, vals)]
                ds = ds * _pair_rows(w8, i) + dsa_c * _pair_rows(a8, i)
                sc = sp
            for ref, o in zip((dr_ref, dw_ref, dk_ref, da_ref, db_ref), outs):
                ref[rows, :] = o
            return ds, sc

        ds, _ = _blocked_loop(cc // 8, prepare, advance, (dstate[...], sall_ref[cc - 1]))
        dstate[...] = ds

    row = pl.BlockSpec((cc, WB), lambda i: (n - 1 - i, 0))
    return pl.pallas_call(
        body, name=name, grid=(n,),
        in_specs=[row] * 7 + [pl.BlockSpec((cc, 4 * HD_B, LANES), lambda i: (n - 1 - i, 0, 0)),
                              pl.BlockSpec((1, 4 * HD_B, LANES), lambda i: (jnp.maximum((n - 1 - i) * cc - 1, 0), 0, 0))],
        out_specs=[row] * 5, out_shape=[jax.ShapeDtypeStruct((t, WB), F32)] * 5,
        scratch_shapes=[pltpu.VMEM((4 * HD_B, LANES), F32)], compiler_params=_params(1))(
            dy, r, w, v, a, sa, dsa, sall, sall)


def _rwkv_post(y, r, k2, v, g, r_k, gn_w, gn_b, name):
    t = y.shape[0]
    tb = _tile(t, 256)

    def body(y_ref, r_ref, k_ref, v_ref, g_ref, rk_ref, gw_ref, gb_ref, o_ref):
        ones = _head_ones(WB, HD_B)
        yv = y_ref[...]
        yc = yv - _split_dot(yv, ones, 3) * (1.0 / HD_B)
        rstd = lax.rsqrt(_split_dot(yc * yc, ones, 3) * (1.0 / HD_B) + GN_EPS)
        rk = _split_dot(r_ref[...] * k_ref[...] * rk_ref[...], ones, 3)
        o_ref[...] = ((yc * rstd * gw_ref[...] + gb_ref[...] + rk * v_ref[...]) * g_ref[...]).astype(MXU)

    row = pl.BlockSpec((tb, WB), lambda i: (i, 0))
    vec = pl.BlockSpec((1, WB), lambda i: (0, 0))
    return pl.pallas_call(
        body, name=name, grid=(t // tb,), in_specs=[row] * 5 + [vec] * 3, out_specs=row,
        out_shape=jax.ShapeDtypeStruct((t, WB), MXU), compiler_params=_params(1))(y, r, k2, v, g, r_k, gn_w, gn_b)


def _rwkv_post_bwd(dob, y, r, k2, v, g, r_k, gn_w, gn_b, name):
    t = y.shape[0]
    tb = _tile(t, 256)

    def body(do_ref, y_ref, r_ref, k_ref, v_ref, g_ref, rk_ref, gw_ref, gb_ref,
             dy_ref, dg_ref, dr_ref, dk_ref, dv_ref, dgw_ref, dgb_ref, drk_ref):
        @pl.when(pl.program_id(0) == 0)
        def _():
            dgw_ref[...] = jnp.zeros_like(dgw_ref)
            dgb_ref[...] = jnp.zeros_like(dgb_ref)
            drk_ref[...] = jnp.zeros_like(drk_ref)

        ones = _head_ones(WB, HD_B)
        seg = lambda x: _split_dot(x, ones, 3)
        yv, rv, kv, vv, gv = y_ref[...], r_ref[...], k_ref[...], v_ref[...], g_ref[...]
        yc = yv - seg(yv) * (1.0 / HD_B)
        rstd = lax.rsqrt(seg(yc * yc) * (1.0 / HD_B) + GN_EPS)
        yn = yc * rstd
        rk = seg(rv * kv * rk_ref[...])
        dob_v = do_ref[...]
        dg_ref[...] = dob_v * (yn * gw_ref[...] + gb_ref[...] + rk * vv)
        dyg = dob_v * gv
        dgw_ref[...] += jnp.sum(dyg * yn, axis=0, keepdims=True)
        dgb_ref[...] += jnp.sum(dyg, axis=0, keepdims=True)
        dyn = dyg * gw_ref[...]
        dy_ref[...] = rstd * (dyn - (seg(dyn) + yn * seg(dyn * yn)) * (1.0 / HD_B))
        drk = seg(dyg * vv)
        dv_ref[...] = dyg * rk
        dr_ref[...] = drk * kv * rk_ref[...]
        dk_ref[...] = drk * rv * rk_ref[...]
        drk_ref[...] += jnp.sum(drk * rv * kv, axis=0, keepdims=True)

    row = pl.BlockSpec((tb, WB), lambda i: (i, 0))
    vec = pl.BlockSpec((1, WB), lambda i: (0, 0))
    full, small = jax.ShapeDtypeStruct((t, WB), F32), jax.ShapeDtypeStruct((1, WB), F32)
    return pl.pallas_call(
        body, name=name, grid=(t // tb,),
        in_specs=[pl.BlockSpec((tb, WB), lambda i: (i, dob.shape[1] // WB - 1))] + [row] * 5 + [vec] * 3,
        out_specs=[row] * 5 + [vec] * 3,
        out_shape=[full] * 5 + [small] * 3, compiler_params=_params(1))(dob, y, r, k2, v, g, r_k, gn_w, gn_b)


def _rwkv_prep_bwd(grads, p_all, mu_pad, w2cat, w0, a0, k_k, k_a, name):
    t = p_all.shape[0]
    tb = _tile(t, 256)

    def body(*refs):
        g_refs, ins, outs = refs[:10], refs[10:27], refs[27:]
        dr_s, dw, dk2_s, dv_s, das, dbs, dg, dr_b, dk2_b, dv_b = (ref[...] for ref in g_refs)
        dr_ref, dk_ref, dv_ref, dlo_ref, dw2_ref, dw0_ref, da0_ref, dkk_ref, dka_ref = outs

        @pl.when(pl.program_id(0) == 0)
        def _():
            for ref in (dw2_ref, dw0_ref, da0_ref, dkk_ref, dka_ref):
                ref[...] = jnp.zeros_like(ref)

        ones = _head_ones(WB, HD_B)
        q = _rwkv_inputs(ins, pl.program_id(0) == 0, ones)
        kk_w, ka_w = ins[15][...], ins[16][...]
        a, kk, k = q["a"], q["kk"], q["k"]
        dk2 = dk2_s + dk2_b
        dkk = dbs * a - das
        da = dbs * kk + dk2 * k * ka_w
        dk = dk2 * (1.0 + (a - 1.0) * ka_w)
        dka_ref[...] += jnp.sum(dk2 * k * (a - 1.0), axis=0, keepdims=True)
        proj = jnp.where(q["nrm"] > L2_EPS, _split_dot(dkk * kk, ones, 3), 0.0)
        dkk0 = (dkk - kk * proj) / q["den"]
        dk = dk + dkk0 * kk_w
        dkk_ref[...] += jnp.sum(dkk0 * k, axis=0, keepdims=True)
        dal = da * a * (1.0 - a)
        da0_ref[...] += jnp.sum(dal, axis=0, keepdims=True)
        sg = q["sg"]
        dwl = dw * q["decay"] * (-DECAY_C) * sg * (1.0 - sg)
        dw0_ref[...] += jnp.sum(dwl, axis=0, keepdims=True)
        dlin = jnp.concatenate([dwl, dal, dg], axis=1).astype(MXU)
        dw2_ref[...] += _dot(q["z"].astype(MXU), dlin, _TN)
        dz = _dot(dlin, ins[12][...], _NT)
        dlo_ref[...] = dz * _lora_dact(q["lo"])
        dr_ref[...] = dr_s + dr_b
        dk_ref[...] = dk
        dv_ref[...] = dv_s + dv_b

    row = pl.BlockSpec((tb, WB), lambda i: (i, 0))
    vec = pl.BlockSpec((1, WB), lambda i: (0, 0))
    full, small = jax.ShapeDtypeStruct((t, WB), F32), jax.ShapeDtypeStruct((1, WB), F32)
    return pl.pallas_call(
        body, name=name, grid=(t // tb,), in_specs=[row] * 10 + _rwkv_in_specs(t, tb),
        out_specs=[row] * 3 + [pl.BlockSpec((tb, 256), lambda i: (i, 0)),
                               pl.BlockSpec((256, 3 * WB), lambda i: (0, 0))] + [vec] * 4,
        out_shape=[full] * 3 + [jax.ShapeDtypeStruct((t, 256), F32), jax.ShapeDtypeStruct((256, 3 * WB), F32)]
        + [small] * 4, compiler_params=_params(1))(*grads, *_rwkv_in_args(p_all, mu_pad, w2cat, w0, a0, k_k, k_a))


def _shift_bwd(dshifted, p_all, mu_pad, name):
    t = p_all.shape[0]
    tb = _tile(t, 256)
    nt, nt8 = t // tb, tb // 8
    widths, cols, mus = (WB, WB, WB, 256), (COL_R, COL_K, COL_V, COL_L), (0, 1, 2, 6)

    def body(*refs):
        d_refs, n_refs, p_refs, q_refs, m_refs = refs[0:4], refs[4:8], refs[8:12], refs[12:16], refs[16:20]
        o_refs, dmu_refs = refs[20:24], refs[24:28]
        i = pl.program_id(0)

        @pl.when(i == 0)
        def _():
            for ref in dmu_refs:
                ref[...] = jnp.zeros_like(ref)

        for d, nx, p, q, m, o, dmu in zip(d_refs, n_refs, p_refs, q_refs, m_refs, o_refs, dmu_refs):
            dv, pv, mu = d[...], p[...], m[...]
            o[...] = (dv * (1.0 - mu) + mu * _shift_up(dv, nx[...], i == nt - 1)).astype(MXU)
            dmu[...] += jnp.sum(dv * (_shift_down(pv, q[...], i == 0) - pv), axis=0, keepdims=True)

    cur_d = [pl.BlockSpec((tb, w), lambda i: (i, 0)) for w in widths]
    next_d = [pl.BlockSpec((8, w), lambda i: (jnp.minimum((i + 1) * nt8, t // 8 - 1), 0)) for w in widths]
    cur_p = [pl.BlockSpec((tb, w), lambda i, j=j: (i, j)) for w, j in zip(widths, cols)]
    prev_p = [pl.BlockSpec((8, w), lambda i, j=j: (jnp.maximum(i * nt8 - 1, 0), j)) for w, j in zip(widths, cols)]
    mu_s = [pl.BlockSpec((1, w), lambda i, j=j: (0, j)) for w, j in zip(widths, mus)]
    vecs = [pl.BlockSpec((1, w), lambda i: (0, 0)) for w in widths]
    return pl.pallas_call(
        body, name=name, grid=(nt,), in_specs=cur_d + next_d + cur_p + prev_p + mu_s, out_specs=cur_d + vecs,
        out_shape=[jax.ShapeDtypeStruct((t, w), MXU) for w in widths]
        + [jax.ShapeDtypeStruct((1, w), F32) for w in widths],
        compiler_params=_params(1))(*dshifted, *dshifted, *(p_all,) * 8, *(mu_pad,) * 4)


def _peer(k):
    x, y, c = (lax.axis_index(n) for n in AXES)
    px = 1 - x if k & 4 else x
    py = 1 - y if k & 2 else y
    pc = 1 - c if k & 1 else c
    return (px, py, pc), 4 * px + 2 * py + pc


def _exchange_copy(src_refs, land_refs, send_sems, recv_sems, per_peer, j, k, arriving):
    _, me = _peer(0)
    peer, idx = _peer(k)
    sem = j * (N_DEV - 1) + k - 1
    return pltpu.make_async_remote_copy(
        src_ref=src_refs[j].at[idx] if per_peer[j] else src_refs[j],
        dst_ref=land_refs[j].at[idx if arriving else me],
        send_sem=send_sems.at[sem], recv_sem=recv_sems.at[sem],
        device_id=peer, device_id_type=pl.DeviceIdType.MESH)


def _exchange_start(srcs, per_peer, name, after=()):
    n = len(srcs)
    shapes = [tuple(s.shape[1:]) if pp else tuple(s.shape) for s, pp in zip(srcs, per_peer)]
    pairs = [(j, k) for k in range(1, N_DEV) for j in range(n)]
    first_out = 2 * n + len(after)

    def body(*refs):
        src_refs, land_refs, (send_sems, recv_sems), token = refs[:n], refs[n:2 * n], refs[first_out:first_out + 2], refs[-1]
        for j, k in pairs:
            _exchange_copy(src_refs, land_refs, send_sems, recv_sems, per_peer, j, k, False).start()
        token[...] = jnp.zeros_like(token)

    hbm, sem = pl.BlockSpec(memory_space=pltpu.HBM), pl.BlockSpec(memory_space=pltpu.SEMAPHORE)
    lands = [lax.empty((N_DEV,) + shp, s.dtype) for shp, s in zip(shapes, srcs)]
    operands = [pltpu.with_memory_space_constraint(a, pltpu.HBM) for a in list(srcs) + lands]
    n_sems = n * (N_DEV - 1)
    out = pl.pallas_call(
        body, name=name, in_specs=[hbm] * (2 * n) + [pl.BlockSpec(memory_space=pl.ANY)] * len(after),
        out_specs=[sem, sem] + [hbm] * (2 * n) + [pl.BlockSpec(memory_space=pltpu.VMEM)],
        out_shape=[pltpu.SemaphoreType.DMA((n_sems,)), pltpu.SemaphoreType.DMA((n_sems,))]
        + [pltpu.HBM(a.shape, a.dtype) for a in operands] + [jax.ShapeDtypeStruct((8, LANES), F32)],
        input_output_aliases={j: 2 + j for j in range(2 * n)},
        compiler_params=pltpu.CompilerParams(has_side_effects=pltpu.SideEffectType.DATAFLOW_SIDE_EFFECTING))(
            *operands, *after)
    return (out[0], out[1], out[2:2 + n], out[2 + n:2 + 2 * n], per_peer), out[-1]


def _exchange_wait(handle, after, name):
    send_sems, recv_sems, srcs, lands, per_peer = handle
    n = len(srcs)
    pairs = [(j, k) for k in range(1, N_DEV) for j in range(n)]

    def body(*refs):
        src_refs, land_refs, (send_sems, recv_sems) = refs[:n], refs[n:2 * n], refs[2 * n:2 * n + 2]
        for j, k in pairs:
            _exchange_copy(src_refs, land_refs, send_sems, recv_sems, per_peer, j, k, False).wait_send()
            _exchange_copy(src_refs, land_refs, send_sems, recv_sems, per_peer, j, k, True).wait_recv()

    hbm, sem = pl.BlockSpec(memory_space=pltpu.HBM), pl.BlockSpec(memory_space=pltpu.SEMAPHORE)
    out = pl.pallas_call(
        body, name=name, in_specs=[hbm] * (2 * n) + [sem, sem, pl.BlockSpec(memory_space=pl.ANY)],
        out_specs=[hbm] * (2 * n), out_shape=[pltpu.HBM(a.shape, a.dtype) for a in list(srcs) + list(lands)],
        input_output_aliases={j: j for j in range(2 * n)},
        compiler_params=pltpu.CompilerParams(has_side_effects=pltpu.SideEffectType.DATAFLOW_SIDE_EFFECTING))(
            *srcs, *lands, send_sems, recv_sems, after)
    return out[n:]


def _adam_update(g, w, m, v):
    c1, c2 = 1.0 - ADAM_B1 ** ADAM_STEP, 1.0 - ADAM_B2 ** ADAM_STEP
    nm = ADAM_B1 * m + (1.0 - ADAM_B1) * g
    nv = ADAM_B2 * v + (1.0 - ADAM_B2) * (g * g)
    return -ADAM_LR * ((nm / c1) / (jnp.sqrt(nv / c2) + ADAM_EPS) + ADAM_WD * w), nm, nv


def _row_tile(rows, cols):
    padded = -(-cols // LANES) * LANES
    cap = max(16, ADAM_BLOCK_BYTES // (N_DEV * padded * 4))
    best = rows
    for t in range(16, min(rows, cap) + 1, 16):
        if rows % t == 0:
            best = t
    return best


def _sum_in_device_order(received, mine):
    x, y, c = (lax.axis_index(n) for n in AXES)
    me = 4 * x + 2 * y + c
    total = None
    for d in range(N_DEV):
        term = jnp.where(me == d, mine, received(d)).astype(F32)
        total = term if total is None else total + term
    return total


def _sum_parts(parts, mine, name):
    _, rows, cols = parts.shape
    tb = _row_tile(rows, cols)

    def body(p_ref, mine_ref, g_ref):
        g_ref[...] = _sum_in_device_order(lambda d: p_ref[d], mine_ref[...])

    tile = pl.BlockSpec((tb, cols), lambda i: (i, 0))
    return pl.pallas_call(
        body, name=name, grid=(rows // tb,), in_specs=[pl.BlockSpec((N_DEV, tb, cols), lambda i: (0, i, 0)), tile],
        out_specs=tile, out_shape=jax.ShapeDtypeStruct((rows, cols), F32), compiler_params=_params(1))(parts, mine)


def _adamw(grad, w, m, v, name):
    _, rows, cols = w.shape
    tb = _row_tile(rows, cols)
    summed = not isinstance(grad, tuple)

    def body(*refs):
        w_ref, m_ref, v_ref, g_ref, d_ref, nm_ref, nv_ref = refs[-7:]
        g = refs[0][...] if summed else _sum_in_device_order(lambda d: refs[0][d], refs[1][...])
        g_ref[0] = g
        d_ref[0], nm_ref[0], nv_ref[0] = _adam_update(g, w_ref[0], m_ref[0], v_ref[0])

    row = pl.BlockSpec((1, tb, cols), lambda i: (0, i, 0))
    tile = pl.BlockSpec((tb, cols), lambda i: (i, 0))
    grad_specs = [tile] if summed else [pl.BlockSpec((N_DEV, tb, cols), lambda i: (0, i, 0)), tile]
    out = jax.ShapeDtypeStruct(w.shape, F32)
    return pl.pallas_call(
        body, name=name, grid=(rows // tb,), in_specs=grad_specs + [row, row, row], out_specs=[row] * 4,
        out_shape=[out] * 4, compiler_params=_params(1))(*((grad,) if summed else grad), w, m, v)


def _adamw_small(parts, mine, ws, ms, vs, name):
    n = len(ws)

    def body(*refs):
        p_ref, mine_ref = refs[:2]
        w_refs, m_refs, v_refs = refs[2:2 + n], refs[2 + n:2 + 2 * n], refs[2 + 2 * n:2 + 3 * n]
        outs = refs[2 + 3 * n:]
        base = 0
        for j in range(n):
            rows, cols = ws[j].shape
            size = rows * cols
            for ch in range(-(-size // LANES)):
                r, c0 = divmod(ch * LANES, cols)
                width = min(LANES, cols - c0)
                packed = (slice(base + ch, base + ch + 1), slice(0, width))
                g = _sum_in_device_order(lambda d: p_ref[(d,) + packed], mine_ref[packed])
                at = (slice(r, r + 1), slice(c0, c0 + width))
                delta, nm, nv = _adam_update(g, w_refs[j][at], m_refs[j][at], v_refs[j][at])
                for out, val in zip((outs[j], outs[n + j], outs[2 * n + j], outs[3 * n + j]), (g, delta, nm, nv)):
                    out[at] = val
            base += -(-size // (8 * LANES)) * 8
        outs[4 * n][...] = _sum_in_device_order(lambda d: p_ref[d, base:base + 1, :], mine_ref[base:base + 1, :])

    vmem = pl.BlockSpec(memory_space=pltpu.VMEM)
    res = pl.pallas_call(
        body, name=name, in_specs=[vmem] * (2 + 3 * n), out_specs=[vmem] * (4 * n + 1),
        out_shape=[jax.ShapeDtypeStruct(a.shape, F32) for a in ws] * 4 + [jax.ShapeDtypeStruct((1, LANES), F32)])(
            parts, mine, *ws, *ms, *vs)
    return res[:n], res[n:2 * n], res[2 * n:3 * n], res[3 * n:4 * n], res[4 * n][0, 0]


def _rows(a, multiple):
    flat = a.reshape(-1)
    pad = -flat.shape[0] % (multiple * LANES)
    if pad:
        flat = jnp.concatenate([flat, jnp.zeros((pad,), a.dtype)])
    return flat.reshape(-1, LANES)


def _pack(arrs, multiple):
    return jnp.concatenate([_rows(a, multiple) for a in arrs], axis=0)


def _gathered_to_full(g, name, shard_shape):
    g = g.reshape((N_DEV,) + shard_shape)
    if name in COL_SHARDED:
        return jnp.transpose(g, (1, 0, 2)).reshape(shard_shape[0], N_DEV * shard_shape[1])
    return g.reshape(N_DEV * shard_shape[0], shard_shape[1])


def _full_to_per_device(full, name):
    if name in COL_SHARDED:
        r, c = full.shape
        return jnp.transpose(full.reshape(r, N_DEV, c // N_DEV), (1, 0, 2))
    return full.reshape(N_DEV, full.shape[0] // N_DEV, full.shape[1])


def _w2cat(w2, a2, g2):
    n_w, n_a, n_g = LORA
    out = jnp.zeros((256, 3 * WB), w2.dtype)
    out = out.at[0:n_w, 0:WB].set(w2)
    out = out.at[n_w:n_w + n_a, WB:2 * WB].set(a2)
    return out.at[n_w + n_a:n_w + n_a + n_g, 2 * WB:].set(g2)


class _Local:
    def __init__(self, w):
        self.w = w

    def weights(self, group, after=None):
        return self.w

    def small(self):
        return self.w

    def started(self):
        return ()

    def send(self, grads, names):
        return ()


class _Overlapped:
    GROUPS = {"ffn1": ("ffn1_w_gate", "ffn1_w_up", "ffn1_w_down"),
              "mixer_in": ("w_in", "rwkv_w2", "rwkv_a2", "rwkv_g2"),
              "late": ("w_out", "ffn2_w_gate", "ffn2_w_up", "ffn2_w_down")}

    def __init__(self, wts):
        x, y, c = (lax.axis_index(n) for n in AXES)
        self.wts, self.me, self.gathers, self.sends = wts, 4 * x + 2 * y + c, {}, []
        self._gather("ffn1", ())

    def _gather(self, group, after):
        names = self.GROUPS[group]
        shards = [(jnp.swapaxes(self.wts[n], 1, 2) if n in SENT_TRANSPOSED else self.wts[n]).astype(MXU) for n in names]
        handle, token = _exchange_start(shards, [False] * len(names), "gather_" + group, after)
        self.gathers[group] = (names, shards, handle, token)
        self.newest = token

    def started(self):
        return (self.newest,)

    def _own_slot(self, land, mine):
        return lax.dynamic_update_slice(land, mine[None], (self.me,) + (0,) * mine.ndim)

    def weights(self, group, after=None):
        names, shards, handle, token = self.gathers[group]
        lands = _exchange_wait(handle, token if after is None else after, "gathered_" + group)
        w = {n: _gathered_to_full(self._own_slot(land, own), n, own.shape[1:])
             for n, own, land in zip(names, shards, lands)}
        order = list(self.GROUPS)
        if group != order[-1]:
            self._gather(order[order.index(group) + 1], (w[names[0]],))
        return w

    def small(self):
        keep = ("hgrn_lb_logits", "rwkv_r_k", "final_norm")
        return {n: self.wts[n] if n in keep else self.wts[n].reshape(1, -1) for n in SMALL}

    def send(self, grads, names, small=None):
        contrib = [_full_to_per_device(grads[n], n).astype(WIRE) for n in names]
        per_peer = [True] * len(names)
        if small is not None:
            names, contrib, per_peer = names + ("small",), contrib + [small], per_peer + [False]
        handle, token = _exchange_start(contrib, per_peer, "scatter_" + names[0])
        self.sends.append((names, contrib, per_peer, handle))
        self.last_token = token
        return (token,)

    def received(self, which, after):
        names, contrib, per_peer, handle = self.sends[which]
        lands = _exchange_wait(handle, after, "scattered_" + names[0])
        return {n: (land, lax.dynamic_index_in_dim(own, self.me, 0, keepdims=False) if pp else own)
                for n, own, pp, land in zip(names, contrib, per_peer, lands)}


def _local_step(x, target, net):
    n_w, n_a, n_g = LORA
    w = dict(net.small())
    h1 = _rms_fwd(x, w["ffn1_norm"], "ffn1_norm", after=net.started())
    w.update(net.weights("ffn1", h1))
    x1 = _ffn_fwd(x, h1, w["ffn1_w_gate"], w["ffn1_w_up"], w["ffn1_w_down"], "ffn1_fwd", after=net.started())
    w.update(net.weights("mixer_in", x1))
    w_in_pad = jnp.pad(w["w_in"], ((0, N_INP - N_IN), (0, 0)))
    mu_pad = jnp.pad(w["rwkv_shift_mu"], ((0, 0), (0, 1792 - 1696)))
    w2cat = _w2cat(w["rwkv_w2"], w["rwkv_a2"], w["rwkv_g2"])
    r_k = w["rwkv_r_k"].reshape(1, WB)
    rw = (mu_pad, w2cat, w["rwkv_w0"], w["rwkv_a0"], w["rwkv_k_k"], w["rwkv_k_a"])

    h2 = _rms_fwd(x1, w["mix_norm"], "mix_norm")
    p_all = _matmul(h2, w_in_pad, tb=True, after=net.started(), name="in_proj")
    oa, oraw, states = _hgrn_fwd(p_all, w["hgrn_lb_logits"], w["hgrn_out_norm"], "hgrn_fwd")
    r, decay, k2, v, sa, sb, g = _rwkv_prep(p_all, *rw, "rwkv_prep")
    y, s_a = _rwkv_scan_fwd(r, decay, k2, v, sa, sb, "rwkv_scan_fwd")
    sall = _rwkv_states(s_a, decay, k2, v, sb, "rwkv_states")
    post_w = (r_k, w["rwkv_gn_w"], w["rwkv_gn_b"])
    ob = _rwkv_post(y, r, k2, v, g, *post_w, "rwkv_post")
    w.update(net.weights("late", ob))
    o = jnp.concatenate([oa, ob], axis=1)
    x2 = _matmul(o, w["w_out"], res=x1, name="out_proj")
    h3 = _rms_fwd(x2, w["ffn2_norm"], "ffn2_norm")
    x3 = _ffn_fwd(x2, h3, w["ffn2_w_gate"], w["ffn2_w_up"], w["ffn2_w_down"], "ffn2_fwd")
    loss, dx3, d_final = _loss_head(x3, w["final_norm"].reshape(1, D), target, "loss_head")

    grads = {"final_norm": d_final.reshape(D)}

    def ffn_back(prefix, h, dy, x_in, norm):
        wg, wu, wd = (w[prefix + s] for s in ("_w_gate", "_w_up", "_w_down"))
        act, dgate, dup, dout = _ffn_bwd(h, dy, wg, wu, wd, prefix + "_bwd")
        dh = _matmul(dup, wu, res=_matmul(dgate, wg, name=prefix + "_dh_gate"), name=prefix + "_dh")
        sent = ()
        for which, a_op, b_op in (("_w_gate", dgate, h), ("_w_up", dup, h), ("_w_down", act, dout)):
            grads[prefix + which] = _matmul(a_op, b_op, ta=True, out_dtype=WIRE, after=sent, name=prefix + "_d" + which)
            sent = net.send(grads, (prefix + which,))
        dx, grads[prefix + "_norm"] = _rms_bwd(x_in, norm, dh, dy, prefix + "_norm_bwd", after=sent)
        return dx

    dx2 = ffn_back("ffn2", h3, dx3, x2, w["ffn2_norm"])
    grads["w_out"] = _matmul(o, dx2, ta=True, out_dtype=WIRE, name="d_w_out")
    sent = net.send(grads, ("w_out",))
    do = _matmul(dx2, w["w_out"], tb=True, after=sent, name="d_mixed")
    dqa, dfa, dia, dga, grads["hgrn_out_norm"], grads["hgrn_lb_logits"] = _hgrn_bwd(
        p_all, w["hgrn_lb_logits"], w["hgrn_out_norm"], oraw, states, do, "hgrn_bwd")
    dy, dg, dr_b, dk2_b, dv_b, grads["rwkv_gn_w"], grads["rwkv_gn_b"], d_rk = _rwkv_post_bwd(
        do, y, r, k2, v, g, *post_w, "rwkv_post_bwd")
    grads["rwkv_r_k"] = d_rk.reshape(w["rwkv_r_k"].shape)
    d_sa, dv = _rwkv_scan_bwd(dy, r, decay, k2, sa, sb, "rwkv_scan_bwd")
    dr, dw, dk2, dsa, dsb = _rwkv_scan_bwd_values(dy, r, decay, v, sa, s_a, d_sa, sall, "rwkv_scan_bwd_values")
    (dsr, dsk, dsv, dslo, dw2cat, grads["rwkv_w0"], grads["rwkv_a0"], grads["rwkv_k_k"],
     grads["rwkv_k_a"]) = _rwkv_prep_bwd((dr, dw, dk2, dv, dsa, dsb, dg, dr_b, dk2_b, dv_b), p_all, *rw,
                                         "rwkv_prep_bwd")
    grads["rwkv_w2"] = dw2cat[0:n_w, 0:WB]
    grads["rwkv_a2"] = dw2cat[n_w:n_w + n_a, WB:2 * WB]
    grads["rwkv_g2"] = dw2cat[n_w + n_a:n_w + n_a + n_g, 2 * WB:]
    dpr, dpk, dpv, dplo, dmu_r, dmu_k, dmu_v, dmu_lo = _shift_bwd((dsr, dsk, dsv, dslo), p_all, mu_pad, "shift_bwd")
    grads["rwkv_shift_mu"] = jnp.concatenate([dmu_r, dmu_k, dmu_v, dmu_lo], axis=1)[:, :1696]
    dp = jnp.concatenate([dqa, dfa, dia, dga, dpr, dpk, dpv, dplo], axis=1)
    grads["w_in"] = _matmul(dp, h2, ta=True, out_dtype=WIRE, name="d_w_in")[:N_IN]
    sent = net.send(grads, ("w_in", "rwkv_w2", "rwkv_a2", "rwkv_g2"))
    dh2 = _matmul(dp, w_in_pad, after=sent, name="d_h2")
    dx1, grads["mix_norm"] = _rms_bwd(x1, w["mix_norm"], dh2, dx2, "mix_norm_bwd")
    dx0 = ffn_back("ffn1", h1, dx1, x, w["ffn1_norm"])
    return loss[0, 0], dx0, grads


def kernel(x, ffn1_norm, ffn1_w_gate, ffn1_w_up, ffn1_w_down, mix_norm, w_in, hgrn_lb_logits, hgrn_out_norm, rwkv_shift_mu, rwkv_w0, rwkv_w2, rwkv_a0, rwkv_a2, rwkv_g2, rwkv_k_k, rwkv_k_a, rwkv_r_k, rwkv_gn_w, rwkv_gn_b, w_out, ffn2_norm, ffn2_w_gate, ffn2_w_up, ffn2_w_down, final_norm, loss_target, m_ffn1_norm, m_ffn1_w_gate, m_ffn1_w_up, m_ffn1_w_down, m_mix_norm, m_w_in, m_hgrn_lb_logits, m_hgrn_out_norm, m_rwkv_shift_mu, m_rwkv_w0, m_rwkv_w2, m_rwkv_a0, m_rwkv_a2, m_rwkv_g2, m_rwkv_k_k, m_rwkv_k_a, m_rwkv_r_k, m_rwkv_gn_w, m_rwkv_gn_b, m_w_out, m_ffn2_norm, m_ffn2_w_gate, m_ffn2_w_up, m_ffn2_w_down, m_final_norm, v_ffn1_norm, v_ffn1_w_gate, v_ffn1_w_up, v_ffn1_w_down, v_mix_norm, v_w_in, v_hgrn_lb_logits, v_hgrn_out_norm, v_rwkv_shift_mu, v_rwkv_w0, v_rwkv_w2, v_rwkv_a0, v_rwkv_a2, v_rwkv_g2, v_rwkv_k_k, v_rwkv_k_a, v_rwkv_r_k, v_rwkv_gn_w, v_rwkv_gn_b, v_w_out, v_ffn2_norm, v_ffn2_w_gate, v_ffn2_w_up, v_ffn2_w_down, v_final_norm):
    args = dict(locals())
    wts = {n: args[n] for n in WEIGHTS}
    mom = {n: args["m_" + n] for n in WEIGHTS}
    var = {n: args["v_" + n] for n in WEIGHTS}
    net = _Overlapped(wts)
    loss, grad_x, grads = _local_step(x[0], loss_target[0], net)
    after, = net.send(grads, (), _pack([grads[n] for n in SMALL] + [jnp.full((LANES,), loss)], 8))

    new = {}
    two_d = lambda a: a if a.ndim == 2 else a.reshape(1, -1)
    for which in range(len(net.sends)):
        for n, (part, mine) in net.received(which, after).items():
            if n == "small":
                *small, loss = _adamw_small(part, mine, *([two_d(src[k]) for k in SMALL] for src in (wts, mom, var)),
                                            "adamw_small")
                for j, k in enumerate(SMALL):
                    new[k] = [res[j].reshape(wts[k].shape) for res in small]
            else:
                grad = (part, mine)
                if n in SENT_TRANSPOSED:
                    grad = jnp.swapaxes(_sum_parts(part, mine, "grad_" + n), 0, 1)
                new[n] = _adamw(grad, wts[n], mom[n], var[n], "adamw_" + n)
                after = new[n][1]
    return (loss, grad_x[None], *[new[n][0] for n in WEIGHTS], *[new[n][1] for n in WEIGHTS],
            *[new[n][2] for n in WEIGHTS], *[new[n][3] for n in WEIGHTS])
```
